```python
import math
import jax, jax.numpy as jnp
from jax import lax
import numpy as np

D_MODEL = 1024
BATCH = 16
SEQ = 4096
DEPTH = 2

HEAD_DIM = 64
DIFF_HEADS = 6
DIFF_QK_DIM = HEAD_DIM // 2
RWKV_HEADS = 6
FOX_HEADS = 4
DIFF_WIDTH = DIFF_HEADS * HEAD_DIM
RWKV_WIDTH = RWKV_HEADS * HEAD_DIM
FOX_WIDTH = FOX_HEADS * HEAD_DIM
MIX_WIDTH = DIFF_WIDTH + RWKV_WIDTH + FOX_WIDTH
DECAY_LORA = 64
AAA_LORA = 64
GATE_LORA = 128
DIFF_COLS = 3 * DIFF_WIDTH
RWKV_COLS = 3 * RWKV_WIDTH + DECAY_LORA + AAA_LORA + GATE_LORA
FOX_COLS = 3 * FOX_WIDTH + FOX_HEADS
IN_COLS = DIFF_COLS + RWKV_COLS + FOX_COLS
QB = 128

PEER_HEADS = 8
PEER_NKEYS = 128
PEER_EXPERTS = PEER_NKEYS * PEER_NKEYS
PEER_TOPK = 16
PEER_QDIM = 256
PEER_HALF = PEER_QDIM // 2
PEER_CHUNK = 128

RMS_EPS = 1e-6
RWKV_LN_EPS = 64e-5

kernel_name = 'hymba_diff_rwkv7_fox_peer_adaln'


def _rmsnorm(x, g):
    xf = x.astype(jnp.float32)
    y = xf * lax.rsqrt(jnp.mean(xf * xf, axis=-1, keepdims=True) + RMS_EPS)
    return (y * g.astype(jnp.float32)).astype(x.dtype)


def _heads(t, n_heads):
    B, S, _ = t.shape
    return t.reshape(B, S, n_heads, -1).transpose(0, 2, 1, 3)


def _merge_blocks(o):
    nb, B, H, qb, d = o.shape
    return o.transpose(1, 0, 3, 2, 4).reshape(B, nb * qb, H * d)


def _causal_mask(q0, S):
    qpos = q0 + jnp.arange(QB)
    kpos = jnp.arange(S)
    return qpos[:, None] >= kpos[None, :]


def diff_attention(cols, lam_params, subln_g, layer_idx):
    B, S, _ = cols.shape
    q = cols[..., :DIFF_WIDTH].reshape(B, S, DIFF_HEADS, 2, DIFF_QK_DIM)
    k = cols[..., DIFF_WIDTH:2 * DIFF_WIDTH].reshape(B, S, DIFF_HEADS, 2, DIFF_QK_DIM)
    v = _heads(cols[..., 2 * DIFF_WIDTH:], DIFF_HEADS)
    q1 = q[:, :, :, 0].transpose(0, 2, 1, 3)
    q2 = q[:, :, :, 1].transpose(0, 2, 1, 3)
    k1 = k[:, :, :, 0].transpose(0, 2, 1, 3)
    k2 = k[:, :, :, 1].transpose(0, 2, 1, 3)
    lam_init = 0.8 - 0.6 * math.exp(-0.3 * layer_idx)
    lp = lam_params.astype(jnp.float32)
    lam = jnp.exp(jnp.dot(lp[0], lp[1])) - jnp.exp(jnp.dot(lp[2], lp[3])) + lam_init
    scale = DIFF_QK_DIM ** -0.5

    def block(i):
        q0 = i * QB
        qa = lax.dynamic_slice_in_dim(q1, q0, QB, axis=2)
        qb = lax.dynamic_slice_in_dim(q2, q0, QB, axis=2)
        mask = _causal_mask(q0, S)
        s1 = jnp.einsum('bhqd,bhkd->bhqk', qa, k1).astype(jnp.float32) * scale
        s2 = jnp.einsum('bhqd,bhkd->bhqk', qb, k2).astype(jnp.float32) * scale
        p = (jax.nn.softmax(jnp.where(mask, s1, -jnp.inf), axis=-1)
             - lam * jax.nn.softmax(jnp.where(mask, s2, -jnp.inf), axis=-1))
        return jnp.einsum('bhqk,bhkd->bhqd', p.astype(v.dtype), v)

    o = lax.map(block, jnp.arange(S // QB))
    o = _rmsnorm(o, subln_g) * (1.0 - lam_init)
    return _merge_blocks(o)


def rwkv7_time_mix(cols, mu, w0, w2, a0, a2, g2, k_k, k_a, r_k, ln_g, ln_b):
    B, S, _ = cols.shape
    W = RWKV_WIDTH
    prev = jnp.pad(cols, ((0, 0), (1, 0), (0, 0)))[:, :-1]
    cols = cols + (prev - cols) * mu
    o1 = 3 * W + DECAY_LORA
    o2 = o1 + AAA_LORA
    r = cols[..., :W]
    k = cols[..., W:2 * W]
    v = cols[..., 2 * W:3 * W]
    xw = cols[..., 3 * W:o1]
    xa = cols[..., o1:o2]
    xg = cols[..., o2:]
    w = -jax.nn.softplus(-(w0 + jnp.tanh(xw) @ w2)) - 0.5
    decay = jnp.exp(-jnp.exp(w.astype(jnp.float32)))
    a = jax.nn.sigmoid(a0 + xa @ a2)
    g = jax.nn.sigmoid(xg) @ g2

    def hs(t):
        return t.reshape(B, S, RWKV_HEADS, HEAD_DIM).astype(jnp.float32)

    kk = hs(k * k_k)
    kk = kk / jnp.maximum(jnp.sqrt(jnp.sum(kk * kk, axis=-1, keepdims=True)), 1e-12)
    k = k * (1.0 + (a - 1.0) * k_a)
    r_h, k_h, v_h, w_h, a_h = hs(r), hs(k), hs(v), hs(decay), hs(a)
    a_vec = -kk
    b_vec = kk * a_h

    def step(state, inp):
        r_t, w_t, k_t, v_t, av, bv = inp
        sa = jnp.einsum('bhvk,bhk->bhv', state, av)
        state = (state * w_t[:, :, None, :] + sa[..., None] * bv[:, :, None, :]
                 + v_t[..., None] * k_t[:, :, None, :])
        y = jnp.einsum('bhvk,bhk->bhv', state, r_t)
        return state, y

    tm = lambda t: jnp.moveaxis(t, 1, 0)
    s0 = jnp.zeros((B, RWKV_HEADS, HEAD_DIM, HEAD_DIM), jnp.float32)
    _, y = lax.scan(step, s0, (tm(r_h), tm(w_h), tm(k_h), tm(v_h), tm(a_vec), tm(b_vec)))
    y = jnp.moveaxis(y, 0, 1)
    mean = jnp.mean(y, axis=-1, keepdims=True)
    var = jnp.mean((y - mean) ** 2, axis=-1, keepdims=True)
    y = ((y - mean) * lax.rsqrt(var + RWKV_LN_EPS)).reshape(B, S, W)
    y = y * ln_g.astype(jnp.float32) + ln_b.astype(jnp.float32)
    bonus = jnp.sum(r_h * k_h * r_k.astype(jnp.float32), axis=-1, keepdims=True) * v_h
    y = (y + bonus.reshape(B, S, W)) * g.astype(jnp.float32)
    return y.astype(cols.dtype)


def forgetting_attention(cols, b_f):
    B, S, _ = cols.shape
    q = _heads(cols[..., :FOX_WIDTH], FOX_HEADS)
    k = _heads(cols[..., FOX_WIDTH:2 * FOX_WIDTH], FOX_HEADS)
    v = _heads(cols[..., 2 * FOX_WIDTH:3 * FOX_WIDTH], FOX_HEADS)
    f_logit = cols[..., 3 * FOX_WIDTH:]
    logf = jax.nn.log_sigmoid(f_logit.astype(jnp.float32) + b_f.astype(jnp.float32))
    cum = jnp.cumsum(logf, axis=1).transpose(0, 2, 1)
    scale = HEAD_DIM ** -0.5

    def block(i):
        q0 = i * QB
        qb = lax.dynamic_slice_in_dim(q, q0, QB, axis=2)
        cq = lax.dynamic_slice_in_dim(cum, q0, QB, axis=2)
        mask = _causal_mask(q0, S)
        logits = (jnp.einsum('bhqd,bhkd->bhqk', qb, k).astype(jnp.float32) * scale
                  + cq[..., :, None] - cum[..., None, :])
        p = jax.nn.softmax(jnp.where(mask, logits, -jnp.inf), axis=-1)
        return jnp.einsum('bhqk,bhkd->bhqd', p.astype(v.dtype), v)

    return _merge_blocks(lax.map(block, jnp.arange(S // QB)))


def peer_ffn(h, wq, subkeys, u_tab, v_tab):
    B, S, D = h.shape
    h_chunks = h.reshape(B * S // PEER_CHUNK, PEER_CHUNK, D)
    kk2 = PEER_TOPK * PEER_TOPK

    def chunk(hc):
        q = (hc @ wq).reshape(PEER_CHUNK, PEER_HEADS, 2, PEER_HALF)
        s = jnp.einsum('thcd,hcnd->thcn', q, subkeys).astype(jnp.float32)
        sv, si = lax.top_k(s, PEER_TOPK)
        cand = (sv[:, :, 0, :, None] + sv[:, :, 1, None, :]).reshape(PEER_CHUNK, PEER_HEADS, kk2)
        cidx = (si[:, :, 0, :, None] * PEER_NKEYS + si[:, :, 1, None, :]).reshape(PEER_CHUNK, PEER_HEADS, kk2)
        fv, fpos = lax.top_k(cand, PEER_TOPK)
        eidx = jnp.take_along_axis(cidx, fpos, axis=-1)
        gate = jax.nn.softmax(fv, axis=-1)
        u = jnp.take(u_tab, eidx, axis=0)
        act = jnp.einsum('thkd,td->thk', u, hc).astype(jnp.float32)
        coef = (gate * jax.nn.gelu(act, approximate=False)).astype(hc.dtype)
        vsel = jnp.take(v_tab, eidx, axis=0)
        return jnp.einsum('thk,thkd->td', coef, vsel)

    return lax.map(chunk, h_chunks).reshape(B, S, D)


def setup_inputs(seed: int = 0) -> dict:
    key = jax.random.key(seed)
    ks = jax.random.split(key, 32)
    L, D = DEPTH, D_MODEL
    f32 = jnp.float32

    def nrm(k, shape, s):
        return jax.random.normal(k, shape, f32) * s

    return {
        'x': nrm(ks[0], (BATCH, SEQ, D), 1.0),
        'c': nrm(ks[1], (BATCH, D), 1.0),
        'norm_mix_g': 1.0 + nrm(ks[2], (L, D), 0.02),
        'norm_ffn_g': 1.0 + nrm(ks[3], (L, D), 0.02),
        'final_norm_g': 1.0 + nrm(ks[4], (D,), 0.02),
        'ada_w': nrm(ks[5], (L, D, 6 * D), 0.5 * D ** -0.5),
        'ada_b': nrm(ks[6], (L, 6 * D), 0.02),
        'w_in': nrm(ks[7], (L, D, IN_COLS), D ** -0.5),
        'w_out': nrm(ks[8], (L, MIX_WIDTH, D), MIX_WIDTH ** -0.5),
        'dif_lam': nrm(ks[9], (L, 4, DIFF_QK_DIM), 0.1),
        'dif_subln_g': 1.0 + nrm(ks[10], (L, 2 * DIFF_QK_DIM), 0.02),
        'rw_mu': jax.random.uniform(ks[11], (L, RWKV_COLS), f32, 0.0, 1.0),
        'rw_w0': jax.random.uniform(ks[12], (L, RWKV_WIDTH), f32, -6.5, -1.5),
        'rw_w2': nrm(ks[13], (L, DECAY_LORA, RWKV_WIDTH), 0.1),
        'rw_a0': nrm(ks[14], (L, RWKV_WIDTH), 0.1),
        'rw_a2': nrm(ks[15], (L, AAA_LORA, RWKV_WIDTH), 0.1),
        'rw_g2': nrm(ks[16], (L, GATE_LORA, RWKV_WIDTH), GATE_LORA ** -0.5),
        'rw_kk': 1.0 + nrm(ks[17], (L, RWKV_WIDTH), 0.1),
        'rw_ka': 1.0 + nrm(ks[18], (L, RWKV_WIDTH), 0.1),
        'rw_rk': nrm(ks[19], (L, RWKV_HEADS, HEAD_DIM), 0.1),
        'rw_ln_g': 1.0 + nrm(ks[20], (L, RWKV_WIDTH), 0.02),
        'rw_ln_b': nrm(ks[21], (L, RWKV_WIDTH), 0.02),
        'fox_bf': 1.0 + nrm(ks[22], (L, FOX_HEADS), 0.5),
        'peer_wq': nrm(ks[23], (L, D, PEER_HEADS * PEER_QDIM), D ** -0.5),
        'peer_subkeys': nrm(ks[24], (L, PEER_HEADS, 2, PEER_NKEYS, PEER_HALF), PEER_HALF ** -0.5),
        'peer_u': nrm(ks[25], (L, PEER_EXPERTS, D), D ** -0.5),
        'peer_v': nrm(ks[26], (L, PEER_EXPERTS, D), 1.0),
    }


def reference(x, c, norm_mix_g, norm_ffn_g, final_norm_g, ada_w, ada_b, w_in, w_out,
              dif_lam, dif_subln_g, rw_mu, rw_w0, rw_w2, rw_a0, rw_a2, rw_g2, rw_kk, rw_ka,
              rw_rk, rw_ln_g, rw_ln_b, fox_bf, peer_wq, peer_subkeys, peer_u, peer_v):
    c_act = jax.nn.silu(c)
    for l in range(DEPTH):
        mod = c_act @ ada_w[l] + ada_b[l]
        sh_m, sc_m, g_m, sh_f, sc_f, g_f = jnp.split(mod, 6, axis=-1)
        h = _rmsnorm(x, norm_mix_g[l]) * (1.0 + sc_m[:, None]) + sh_m[:, None]
        cols = h @ w_in[l]
        y_a = diff_attention(cols[..., :DIFF_COLS], dif_lam[l], dif_subln_g[l], l)
        y_b = rwkv7_time_mix(cols[..., DIFF_COLS:DIFF_COLS + RWKV_COLS], rw_mu[l], rw_w0[l],
                             rw_w2[l], rw_a0[l], rw_a2[l], rw_g2[l], rw_kk[l], rw_ka[l],
                             rw_rk[l], rw_ln_g[l], rw_ln_b[l])
        y_c = forgetting_attention(cols[..., DIFF_COLS + RWKV_COLS:], fox_bf[l])
        mix = jnp.concatenate([y_a, y_b, y_c], axis=-1) @ w_out[l]
        x = x + g_m[:, None] * mix
        h = _rmsnorm(x, norm_ffn_g[l]) * (1.0 + sc_f[:, None]) + sh_f[:, None]
        x = x + g_f[:, None] * peer_ffn(h, peer_wq[l], peer_subkeys[l], peer_u[l], peer_v[l])
    return _rmsnorm(x, final_norm_g)
```

```python
import functools
import math

import jax
import jax.numpy as jnp
from jax import lax
from jax.experimental import pallas as pl
from jax.experimental.pallas import tpu as pltpu

F32 = jnp.float32
BF16 = jnp.bfloat16
HIGHEST = lax.Precision.HIGHEST

D_MODEL = 1024
HEAD_DIM = 64
DIFF_HEADS = 6
DIFF_QK_DIM = HEAD_DIM // 2
RWKV_HEADS = 6
FOX_HEADS = 4
DIFF_WIDTH = DIFF_HEADS * HEAD_DIM
RWKV_WIDTH = RWKV_HEADS * HEAD_DIM
FOX_WIDTH = FOX_HEADS * HEAD_DIM
DECAY_LORA = 64
AAA_LORA = 64
GATE_LORA = 128
DIFF_COLS = 3 * DIFF_WIDTH
RWKV_COLS = 3 * RWKV_WIDTH + DECAY_LORA + AAA_LORA + GATE_LORA
PEER_HEADS = 8
PEER_NKEYS = 128
PEER_TOPK = 16
PEER_QDIM = 256
PEER_HALF = PEER_QDIM // 2
RMS_EPS = 1e-6
RWKV_LN_EPS = 64e-5

LANES = 128
RW_PAD_COLS = 3 * RWKV_WIDTH + 3 * LANES
FOX_QKV = 3 * FOX_WIDTH
IN_PAD_COLS = DIFF_COLS + RW_PAD_COLS + FOX_QKV + LANES
VMEM_LIMIT = 56 * 1024 * 1024

RW_CHUNK = 64


def _params(sem, vmem=VMEM_LIMIT):
    return pltpu.CompilerParams(dimension_semantics=sem, vmem_limit_bytes=vmem)


def _nt(a, b, precision=None):
    return lax.dot_general(a, b, (((1,), (1,)), ((), ())), preferred_element_type=F32, precision=precision)


def _mm(a, b, precision=None):
    return jnp.dot(a, b, preferred_element_type=F32, precision=precision)


def _ada_kernel(c_ref, w_ref, b_ref, o_ref):
    c = c_ref[...]
    ca = c * jax.nn.sigmoid(c)
    o_ref[...] = _mm(ca, w_ref[...], HIGHEST) + b_ref[...]


def _ada_mod(c, ada_w, ada_b):
    L, D, N = ada_w.shape
    B = c.shape[0]
    tn = 1536
    return pl.pallas_call(
        _ada_kernel,
        out_shape=jax.ShapeDtypeStruct((L, B, N), F32),
        grid=(L, N // tn),
        in_specs=[
            pl.BlockSpec((B, D), lambda l, j: (0, 0)),
            pl.BlockSpec((None, D, tn), lambda l, j: (l, 0, j)),
            pl.BlockSpec((None, 1, tn), lambda l, j: (l, 0, j)),
        ],
        out_specs=pl.BlockSpec((None, B, tn), lambda l, j: (l, 0, j)),
        compiler_params=_params(("parallel", "parallel")),
        name="ada_mod",
    )(c, ada_w, ada_b.reshape(L, 1, N))


def _inproj_kernel(x_ref, g_ref, sh_ref, sc_ref, w_ref, d_ref, r_ref, f_ref, fl_ref):
    x = x_ref[...]
    ms = jnp.mean(x * x, axis=-1, keepdims=True)
    y = x * lax.rsqrt(ms + RMS_EPS) * g_ref[...]
    h = (y * (1.0 + sc_ref[...]) + sh_ref[...]).astype(BF16)
    o1 = DIFF_COLS
    o2 = o1 + RW_PAD_COLS
    o3 = o2 + FOX_QKV
    d_ref[...] = _mm(h, w_ref[:, 0:o1]).astype(BF16)
    r_ref[...] = _mm(h, w_ref[:, o1:o2])
    f_ref[...] = _mm(h, w_ref[:, o2:o3]).astype(BF16)
    fl_ref[...] = _mm(h, w_ref[:, o3:o3 + LANES])


def _in_proj(x2, g, mod4, w_pad, S):
    T, D = x2.shape
    tm = 512 if S % 512 == 0 else S
    nb = S // tm
    row = lambda i: (i, 0)
    return pl.pallas_call(
        _inproj_kernel,
        out_shape=(
            jax.ShapeDtypeStruct((T, DIFF_COLS), BF16),
            jax.ShapeDtypeStruct((T, RW_PAD_COLS), F32),
            jax.ShapeDtypeStruct((T, FOX_QKV), BF16),
            jax.ShapeDtypeStruct((T, LANES), F32),
        ),
        grid=(T // tm,),
        in_specs=[
            pl.BlockSpec((tm, D), row),
            pl.BlockSpec((1, D), lambda i: (0, 0)),
            pl.BlockSpec((None, None, 1, D), lambda i: (i // nb, 0, 0, 0)),
            pl.BlockSpec((None, None, 1, D), lambda i: (i // nb, 1, 0, 0)),
            pl.BlockSpec((D, IN_PAD_COLS), lambda i: (0, 0)),
        ],
        out_specs=(
            pl.BlockSpec((tm, DIFF_COLS), row),
            pl.BlockSpec((tm, RW_PAD_COLS), row),
            pl.BlockSpec((tm, FOX_QKV), row),
            pl.BlockSpec((tm, LANES), row),
        ),
        compiler_params=_params(("parallel",)),
        name="in_proj",
    )(x2, g.reshape(1, D), mod4, mod4, w_pad)


def _pad_w_in(w_in):
    D = w_in.shape[0]
    W = RWKV_WIDTH
    o = DIFF_COLS
    z64 = jnp.zeros((D, LANES - DECAY_LORA), w_in.dtype)
    rw = w_in[:, o:o + RWKV_COLS]
    fx = w_in[:, o + RWKV_COLS:]
    zf = jnp.zeros((D, LANES - FOX_HEADS), w_in.dtype)
    return jnp.concatenate([
        w_in[:, :o],
        rw[:, :3 * W], rw[:, 3 * W:3 * W + DECAY_LORA], z64,
        rw[:, 3 * W + DECAY_LORA:3 * W + DECAY_LORA + AAA_LORA], z64,
        rw[:, 3 * W + DECAY_LORA + AAA_LORA:],
        fx[:, :FOX_QKV], fx[:, FOX_QKV:], zf,
    ], axis=1).astype(BF16)


def _diff_attn_kernel(lam_ref, g_ref, q_ref, k_ref, v_ref, o_ref, m_ref, l_ref, acc_ref, *, tq, lam_init):
    i = pl.program_id(2)
    scale = DIFF_QK_DIM ** -0.5
    lane = lax.broadcasted_iota(jnp.int32, (1, LANES), 1)
    q = q_ref[...]
    zero = jnp.zeros_like(q)
    qm = [jnp.where((lane >= DIFF_QK_DIM * x) & (lane < DIFF_QK_DIM * (x + 1)), q, zero) for x in range(4)]
    m_ref[...] = jnp.full(m_ref.shape, -jnp.inf, F32)
    l_ref[...] = jnp.zeros(l_ref.shape, F32)
    acc_ref[...] = jnp.zeros(acc_ref.shape, F32)

    def step(j, diag):
        off = pl.multiple_of(j * tq, tq)
        k = k_ref[pl.ds(off, tq), :]
        v = v_ref[pl.ds(off, tq), :]
        if diag:
            rr = lax.broadcasted_iota(jnp.int32, (tq, tq), 0)
            cc = lax.broadcasted_iota(jnp.int32, (tq, tq), 1)
            keep = rr >= cc
        for x in range(4):
            s = _nt(qm[x], k) * scale
            if diag:
                s = jnp.where(keep, s, -jnp.inf)
            m_old = m_ref[x]
            m_new = jnp.maximum(m_old, jnp.max(s, axis=-1, keepdims=True))
            alpha = jnp.exp(m_old - m_new)
            p = jnp.exp(s - m_new)
            l_ref[x] = alpha * l_ref[x] + jnp.sum(p, axis=-1, keepdims=True)
            acc_ref[x] = alpha * acc_ref[x] + _mm(p.astype(BF16), v)
            m_ref[x] = m_new

    def body(j, carry):
        step(j, False)
        return carry

    lax.fori_loop(0, i, body, 0)
    step(i, True)

    lp = lam_ref[...]
    lam = (jnp.exp(jnp.sum(lp[0:1] * lp[1:2], axis=-1, keepdims=True))
           - jnp.exp(jnp.sum(lp[2:3] * lp[3:4], axis=-1, keepdims=True)) + lam_init)
    outs = []
    for hh in range(2):
        o1 = acc_ref[2 * hh] / l_ref[2 * hh]
        o2 = acc_ref[2 * hh + 1] / l_ref[2 * hh + 1]
        outs.append(o1 - lam * o2)
    o = jnp.where(lane < HEAD_DIM, outs[0], outs[1])
    r_i = lax.broadcasted_iota(jnp.int32, (LANES, LANES), 0) // HEAD_DIM
    c_i = lax.broadcasted_iota(jnp.int32, (LANES, LANES), 1) // HEAD_DIM
    seg = (r_i == c_i).astype(F32)
    ms = _mm(o * o, seg, HIGHEST) * (1.0 / HEAD_DIM)
    y = o * lax.rsqrt(ms + RMS_EPS) * g_ref[...] * (1.0 - lam_init)
    o_ref[...] = y.astype(o_ref.dtype)


def _diff_attention(dcols, lam_params, subln_g, layer_idx, B, S):
    T = B * S
    tq = 256 if S % 256 == 0 else S
    nq = S // tq
    npair = DIFF_HEADS // 2
    lam_init = 0.8 - 0.6 * math.exp(-0.3 * layer_idx)
    g2 = jnp.concatenate([subln_g, subln_g]).reshape(1, LANES)
    return pl.pallas_call(
        functools.partial(_diff_attn_kernel, tq=tq, lam_init=lam_init),
        out_shape=jax.ShapeDtypeStruct((T, DIFF_WIDTH), BF16),
        grid=(B, npair, nq),
        in_specs=[
            pl.BlockSpec((4, DIFF_QK_DIM), lambda b, p, i: (0, 0)),
            pl.BlockSpec((1, LANES), lambda b, p, i: (0, 0)),
            pl.BlockSpec((tq, LANES), lambda b, p, i: (b * nq + i, p)),
            pl.BlockSpec((S, LANES), lambda b, p, i: (b, npair + p)),
            pl.BlockSpec((S, LANES), lambda b, p, i: (b, 2 * npair + p)),
        ],
        out_specs=pl.BlockSpec((tq, LANES), lambda b, p, i: (b * nq + i, p)),
        scratch_shapes=[
            pltpu.VMEM((4, tq, 1), F32),
            pltpu.VMEM((4, tq, 1), F32),
            pltpu.VMEM((4, tq, LANES), F32),
        ],
        compiler_params=_params(("parallel", "parallel", "arbitrary")),
        name="diff_attn",
    )(lam_params, g2, dcols, dcols, dcols)


def _fox_cum_kernel(f_ref, b_ref, col_ref, row_ref, *, S, tc):
    rr = lax.broadcasted_iota(jnp.int32, (tc, tc), 0)
    cc = lax.broadcasted_iota(jnp.int32, (tc, tc), 1)
    tri = (rr >= cc).astype(F32)
    carry = jnp.zeros((1, LANES), F32)
    for c in range(S // tc):
        z = f_ref[c * tc:(c + 1) * tc, :] + b_ref[...]
        logf = -(jnp.maximum(-z, 0.0) + jnp.log(1.0 + jnp.exp(-jnp.abs(z))))
        cum = _mm(tri, logf, HIGHEST) + carry
        carry = cum[tc - 1:tc, :]
        col_ref[c * tc:(c + 1) * tc, :] = cum
        row_ref[:, c * tc:(c + 1) * tc] = cum.T[0:8, :]


def _fox_cum(fl, b_f, B, S):
    T = B * S
    tc = 256 if S % 256 == 0 else S
    bpad = jnp.zeros((1, LANES), F32).at[0, :FOX_HEADS].set(b_f.astype(F32))
    return pl.pallas_call(
        functools.partial(_fox_cum_kernel, S=S, tc=tc),
        out_shape=(jax.ShapeDtypeStruct((T, LANES), F32), jax.ShapeDtypeStruct((B, 8, S), F32)),
        grid=(B,),
        in_specs=[pl.BlockSpec((S, LANES), lambda b: (b, 0)), pl.BlockSpec((1, LANES), lambda b: (0, 0))],
        out_specs=(pl.BlockSpec((S, LANES), lambda b: (b, 0)), pl.BlockSpec((None, 8, S), lambda b: (b, 0, 0))),
        compiler_params=_params(("parallel",)),
        name="fox_cum",
    )(fl, bpad)


def _fox_attn_kernel(q_ref, k_ref, v_ref, cc_ref, cr_ref, o_ref, m_ref, l_ref, acc_ref, *, tq):
    p_id = pl.program_id(1)
    i = pl.program_id(2)
    scale = HEAD_DIM ** -0.5
    lane = lax.broadcasted_iota(jnp.int32, (1, LANES), 1)
    q = q_ref[...]
    zero = jnp.zeros_like(q)
    qm = [jnp.where((lane >= HEAD_DIM * x) & (lane < HEAD_DIM * (x + 1)), q, zero) for x in range(2)]
    ccol = cc_ref[...]
    cq = [jnp.sum(jnp.where(lane == 2 * p_id + x, ccol, 0.0), axis=-1, keepdims=True) for x in range(2)]
    m_ref[...] = jnp.full(m_ref.shape, -jnp.inf, F32)
    l_ref[...] = jnp.zeros(l_ref.shape, F32)
    acc_ref[...] = jnp.zeros(acc_ref.shape, F32)

    def step(j, diag):
        off = pl.multiple_of(j * tq, tq)
        k = k_ref[pl.ds(off, tq), :]
        v = v_ref[pl.ds(off, tq), :]
        if diag:
            rr = lax.broadcasted_iota(jnp.int32, (tq, tq), 0)
            cc = lax.broadcasted_iota(jnp.int32, (tq, tq), 1)
            keep = rr >= cc
        for x in range(2):
            ck = cr_ref[2 * p_id + x, pl.ds(j, 1), :]
            s = _nt(qm[x], k) * scale + (cq[x] - ck)
            if diag:
                s = jnp.where(keep, s, -jnp.inf)
            m_old = m_ref[x]
            m_new = jnp.maximum(m_old, jnp.max(s, axis=-1, keepdims=True))
            alpha = jnp.exp(m_old - m_new)
            p = jnp.exp(s - m_new)
            l_ref[x] = alpha * l_ref[x] + jnp.sum(p, axis=-1, keepdims=True)
            acc_ref[x] = alpha * acc_ref[x] + _mm(p.astype(BF16), v)
            m_ref[x] = m_new

    def body(j, carry):
        step(j, False)
        return carry

    lax.fori_loop(0, i, body, 0)
    step(i, True)
    o = jnp.where(lane < HEAD_DIM, acc_ref[0] / l_ref[0], acc_ref[1] / l_ref[1])
    o_ref[...] = o.astype(o_ref.dtype)


def _fox_attention(fqkv, ccol, crow, B, S):
    T = B * S
    tq = 256 if S % 256 == 0 else S
    nq = S // tq
    npair = FOX_HEADS // 2
    crow4 = crow.reshape(B, 8, nq, tq)
    return pl.pallas_call(
        functools.partial(_fox_attn_kernel, tq=tq),
        out_shape=jax.ShapeDtypeStruct((T, FOX_WIDTH), BF16),
        grid=(B, npair, nq),
        in_specs=[
            pl.BlockSpec((tq, LANES), lambda b, p, i: (b * nq + i, p)),
            pl.BlockSpec((S, LANES), lambda b, p, i: (b, npair + p)),
            pl.BlockSpec((S, LANES), lambda b, p, i: (b, 2 * npair + p)),
            pl.BlockSpec((tq, LANES), lambda b, p, i: (b * nq + i, 0)),
            pl.BlockSpec((None, 8, nq, tq), lambda b, p, i: (b, 0, 0, 0)),
        ],
        out_specs=pl.BlockSpec((tq, LANES), lambda b, p, i: (b * nq + i, p)),
        scratch_shapes=[
            pltpu.VMEM((2, tq, 1), F32),
            pltpu.VMEM((2, tq, 1), F32),
            pltpu.VMEM((2, tq, LANES), F32),
        ],
        compiler_params=_params(("parallel", "parallel", "arbitrary")),
        name="fox_attn",
    )(fqkv, fqkv, fqkv, ccol, crow4)


def _seg_sum(x, seg, npair):
    return jnp.concatenate(
        [_mm(x[:, LANES * p:LANES * (p + 1)], seg, HIGHEST) for p in range(npair)], axis=1)


def _rwkv_kernel(x_ref, mu_ref, w0_ref, w2_ref, a0_ref, a2_ref, g2_ref, kk_ref, ka_ref, rk_ref, lng_ref, lnb_ref,
                 o_ref, carry_ref, st_ref, at_ref, rt_ref, bt_ref, kt_ref, v_ref, wc_ref, y_ref, g_ref, bon_ref,
                 *, tt, prec):
    i = pl.program_id(1)
    W = RWKV_WIDTH
    C = RW_CHUNK
    npair = RWKV_HEADS // 2

    @pl.when(i == 0)
    def _():
        carry_ref[...] = jnp.zeros(carry_ref.shape, F32)
        st_ref[...] = jnp.zeros(st_ref.shape, F32)

    x = x_ref[...]
    rows = lax.broadcasted_iota(jnp.int32, (tt, 1), 0)
    prev = jnp.where(rows == 0, carry_ref[...], pltpu.roll(x, 1, axis=0))
    carry_ref[...] = x[tt - 1:tt, :]
    xs = x + (prev - x) * mu_ref[...]
    r = xs[:, 0:W]
    k = xs[:, W:2 * W]
    v = xs[:, 2 * W:3 * W]
    xw = xs[:, 3 * W:3 * W + LANES]
    xa = xs[:, 3 * W + LANES:3 * W + 2 * LANES]
    xg = xs[:, 3 * W + 2 * LANES:]
    wl = w0_ref[...] + _mm(jnp.tanh(xw), w2_ref[...], HIGHEST)
    w = -(jnp.maximum(-wl, 0.0) + jnp.log(1.0 + jnp.exp(-jnp.abs(wl)))) - 0.5
    logdec = -jnp.exp(w)
    a = jax.nn.sigmoid(a0_ref[...] + _mm(xa, a2_ref[...], HIGHEST))
    g_ref[...] = _mm(jax.nn.sigmoid(xg), g2_ref[...], HIGHEST)

    r_i = lax.broadcasted_iota(jnp.int32, (LANES, LANES), 0)
    c_i = lax.broadcasted_iota(jnp.int32, (LANES, LANES), 1)
    seg = ((r_i // HEAD_DIM) == (c_i // HEAD_DIM)).astype(F32)
    kkv = k * kk_ref[...]
    kkn = kkv / jnp.maximum(jnp.sqrt(_seg_sum(kkv * kkv, seg, npair)), 1e-12)
    k2 = k * (1.0 + (a - 1.0) * ka_ref[...])
    bon_ref[...] = _seg_sum(r * k2 * rk_ref[...], seg, npair) * v

    rt_i = lax.broadcasted_iota(jnp.int32, (tt, tt), 0)
    ct_i = lax.broadcasted_iota(jnp.int32, (tt, tt), 1)
    tri = (((rt_i // C) == (ct_i // C)) & (rt_i >= ct_i)).astype(F32)
    cum = _mm(tri, logdec, HIGHEST)
    winv = jnp.exp(-cum)
    wcum = jnp.exp(cum)
    at_ref[...] = -kkn * jnp.exp(cum - logdec)
    bt_ref[...] = kkn * a * winv
    kt_ref[...] = k2 * winv
    rt_ref[...] = r * wcum
    v_ref[...] = v
    wc_ref[...] = wcum

    lane = lax.broadcasted_iota(jnp.int32, (1, LANES), 1)
    lo = lane < HEAD_DIM
    tpos = r_i % C
    ipos = c_i % C
    strict = tpos > ipos
    incl = tpos >= ipos
    eye = r_i == c_i

    def stack2(m):
        return jnp.concatenate([jnp.where(lo, m, 0.0), jnp.where(lo, 0.0, m)], axis=0)

    def cast(m):
        return m if prec is not None else m.astype(BF16)

    def mm(p, q):
        return _mm(cast(p), cast(q), prec)

    def nt(p, q):
        return _nt(cast(p), cast(q), prec)

    def chunk(c, carry):
        r0 = pl.multiple_of(c * C, C)
        for p in range(npair):
            ls = slice(LANES * p, LANES * (p + 1))
            a2 = stack2(at_ref[pl.ds(r0, C), ls])
            r2 = stack2(rt_ref[pl.ds(r0, C), ls])
            b2 = stack2(bt_ref[pl.ds(r0, C), ls])
            k2s = stack2(kt_ref[pl.ds(r0, C), ls])
            v2 = stack2(v_ref[pl.ds(r0, C), ls])
            wlast = wc_ref[pl.ds(pl.multiple_of(r0 + C - 8, 8), 8), ls][7:8, :]
            ar = jnp.concatenate([a2, r2], axis=0)
            mb = nt(ar, b2)
            mk = nt(ar, k2s)
            lab = jnp.where(strict, mb[0:LANES], 0.0)
            mrb = jnp.where(incl, mb[LANES:], 0.0)
            lak = jnp.where(strict, mk[0:LANES], 0.0)
            mrk = jnp.where(incl, mk[LANES:], 0.0)
            xx = jnp.concatenate([a2, mm(lak, v2)], axis=1)
            lp = lab
            xx = xx + mm(lp, xx)
            for _ in range(5):
                lp = mm(lp, lp)
                xx = xx + mm(lp, xx)
            mq = mm(mrb, xx)
            q1 = r2 + mq[:, 0:LANES]
            q2 = mq[:, LANES:] + mm(mrk, v2)
            bx = mm((b2 * wlast).T, xx)
            kv = mm((k2s * wlast).T, v2)
            gmat = jnp.where(eye, jnp.broadcast_to(wlast, (LANES, LANES)), 0.0) + bx[:, 0:LANES]
            hmat = bx[:, LANES:] + kv
            gs = mm(jnp.concatenate([gmat, q1], axis=0), st_ref[p])
            st_ref[p] = gs[0:LANES] + hmat
            yy = gs[LANES:] + q2
            y_ref[pl.ds(r0, C), ls] = yy[0:C] + yy[C:]
        return carry

    lax.fori_loop(0, tt // C, chunk, 0)

    y = y_ref[...]
    inv = 1.0 / HEAD_DIM
    mean = _seg_sum(y, seg, npair) * inv
    yc = y - mean
    var = _seg_sum(yc * yc, seg, npair) * inv
    yn = yc * lax.rsqrt(var + RWKV_LN_EPS) * lng_ref[...] + lnb_ref[...]
    o_ref[...] = ((yn + bon_ref[...]) * g_ref[...]).astype(o_ref.dtype)


def _rwkv(rcols, mu, w0, w2, a0, a2, g2, k_k, k_a, r_k, ln_g, ln_b, B, S, prec=HIGHEST):
    T = B * S
    W = RWKV_WIDTH
    tt = 512 if S % 512 == 0 else S
    nt_ = S // tt
    npair = RWKV_HEADS // 2
    pad = LANES - DECAY_LORA
    mu_p = jnp.concatenate([mu[:3 * W], mu[3 * W:3 * W + DECAY_LORA], jnp.zeros((pad,), F32),
                            mu[3 * W + DECAY_LORA:3 * W + DECAY_LORA + AAA_LORA], jnp.zeros((pad,), F32),
                            mu[3 * W + DECAY_LORA + AAA_LORA:]]).reshape(1, RW_PAD_COLS)
    w2p = jnp.concatenate([w2, jnp.zeros((pad, W), F32)], axis=0)
    a2p = jnp.concatenate([a2, jnp.zeros((pad, W), F32)], axis=0)
    vec = lambda t: t.reshape(1, W).astype(F32)
    full = lambda shape: pl.BlockSpec(shape, lambda b, i: (0,) * len(shape))
    sc = lambda: pltpu.VMEM((tt, W), F32)
    return pl.pallas_call(
        functools.partial(_rwkv_kernel, tt=tt, prec=prec),
        out_shape=jax.ShapeDtypeStruct((T, W), BF16),
        grid=(B, nt_),
        in_specs=[
            pl.BlockSpec((tt, RW_PAD_COLS), lambda b, i: (b * nt_ + i, 0)),
            full((1, RW_PAD_COLS)), full((1, W)), full((LANES, W)), full((1, W)), full((LANES, W)),
            full((GATE_LORA, W)), full((1, W)), full((1, W)), full((1, W)), full((1, W)), full((1, W)),
        ],
        out_specs=pl.BlockSpec((tt, W), lambda b, i: (b * nt_ + i, 0)),
        scratch_shapes=[
            pltpu.VMEM((1, RW_PAD_COLS), F32),
            pltpu.VMEM((npair, LANES, LANES), F32),
            sc(), sc(), sc(), sc(), sc(), sc(), sc(), sc(), sc(),
        ],
        compiler_params=_params(("parallel", "arbitrary")),
        name="rwkv7",
    )(rcols, mu_p, vec(w0), w2p, vec(a0), a2p, g2, vec(k_k), vec(k_a), vec(r_k), vec(ln_g), vec(ln_b))


def _outproj_kernel(ya_ref, yb_ref, yc_ref, x_ref, w_ref, gm_ref, g_ref, sh_ref, sc_ref, xo_ref, ho_ref):
    o1 = DIFF_WIDTH
    o2 = o1 + RWKV_WIDTH
    mix = (_mm(ya_ref[...], w_ref[0:o1, :]) + _mm(yb_ref[...], w_ref[o1:o2, :])
           + _mm(yc_ref[...], w_ref[o2:, :]))
    xn = x_ref[...] + gm_ref[...] * mix
    xo_ref[...] = xn
    ms = jnp.mean(xn * xn, axis=-1, keepdims=True)
    y = xn * lax.rsqrt(ms + RMS_EPS) * g_ref[...]
    ho_ref[...] = y * (1.0 + sc_ref[...]) + sh_ref[...]


def _out_proj(ya, yb, yc, x2, w_out, g, mod4, S):
    T, D = x2.shape
    tm = 512 if S % 512 == 0 else S
    nb = S // tm
    row = lambda i: (i, 0)
    modspec = lambda which: pl.BlockSpec((None, None, 1, D), lambda i: (i // nb, which, 0, 0))
    return pl.pallas_call(
        _outproj_kernel,
        out_shape=(jax.ShapeDtypeStruct((T, D), F32), jax.ShapeDtypeStruct((T, D), F32)),
        grid=(T // tm,),
        in_specs=[
            pl.BlockSpec((tm, DIFF_WIDTH), row), pl.BlockSpec((tm, RWKV_WIDTH), row), pl.BlockSpec((tm, FOX_WIDTH), row),
            pl.BlockSpec((tm, D), row),
            pl.BlockSpec((D, D), lambda i: (0, 0)),
            modspec(2),
            pl.BlockSpec((1, D), lambda i: (0, 0)),
            modspec(3), modspec(4),
        ],
        out_specs=(pl.BlockSpec((tm, D), row), pl.BlockSpec((tm, D), row)),
        compiler_params=_params(("parallel",)),
        name="out_proj",
    )(ya, yb, yc, x2, w_out.astype(BF16), mod4, g.reshape(1, D), mod4, mod4)


def _top16(s, iota_f, n):
    vals, poss = [], []
    for _ in range(PEER_TOPK):
        m = jnp.max(s, axis=0, keepdims=True)
        pos = jnp.min(jnp.where(s == m, iota_f, float(n)), axis=0, keepdims=True)
        vals.append(m)
        poss.append(pos)
        s = jnp.where(iota_f == pos, -jnp.inf, s)
    return jnp.concatenate(vals, axis=0), jnp.concatenate(poss, axis=0)


def _peer_route_kernel(h_ref, wq_ref, sk_ref, idx_ref, gate_ref, q_scr, e_scr, g_scr):
    K = PEER_TOPK
    hb = h_ref[...].astype(BF16)
    q = _mm(hb, wq_ref[...])
    for hc in range(2 * PEER_HEADS):
        q_scr[hc] = q[:, LANES * hc:LANES * (hc + 1)].astype(BF16)
    iota_n = lax.broadcasted_iota(jnp.int32, (PEER_NKEYS, LANES), 0).astype(F32)
    iota_c = lax.broadcasted_iota(jnp.int32, (K * K, LANES), 0).astype(F32)

    def head(h, carry):
        sv0, si0 = _top16(_nt(sk_ref[2 * h], q_scr[2 * h]), iota_n, PEER_NKEYS)
        sv1, si1 = _top16(_nt(sk_ref[2 * h + 1], q_scr[2 * h + 1]), iota_n, PEER_NKEYS)
        cand = jnp.concatenate([sv0[a:a + 1, :] + sv1 for a in range(K)], axis=0)
        cidx = jnp.concatenate([si0[a:a + 1, :] * float(PEER_NKEYS) + si1 for a in range(K)], axis=0)
        fv, es = [], []
        for _ in range(K):
            m = jnp.max(cand, axis=0, keepdims=True)
            pos = jnp.min(jnp.where(cand == m, iota_c, float(K * K)), axis=0, keepdims=True)
            hit = iota_c == pos
            fv.append(m)
            es.append(jnp.max(jnp.where(hit, cidx, -1.0), axis=0, keepdims=True))
            cand = jnp.where(hit, -jnp.inf, cand)
        fv = jnp.concatenate(fv, axis=0)
        ex = jnp.exp(fv - fv[0:1, :])
        g_scr[h] = ex / jnp.sum(ex, axis=0, keepdims=True)
        e_scr[h] = jnp.concatenate(es, axis=0)
        return carry

    lax.fori_loop(0, PEER_HEADS, head, 0)
    e = e_scr[...].reshape(PEER_HEADS * K, LANES)
    idx_ref[...] = e.T.astype(jnp.int32)
    gate_ref[...] = g_scr[...].reshape(PEER_HEADS * K, LANES)


def _peer_route(h2, wq, subkeys):
    T, D = h2.shape
    tm = LANES
    nq = 2 * PEER_HEADS
    sk = subkeys.reshape(nq, PEER_NKEYS, PEER_HALF).astype(BF16)
    return pl.pallas_call(
        _peer_route_kernel,
        out_shape=(jax.ShapeDtypeStruct((T, PEER_HEADS * PEER_TOPK), jnp.int32),
                   jax.ShapeDtypeStruct((T // tm, PEER_HEADS * PEER_TOPK, tm), F32)),
        grid=(T // tm,),
        in_specs=[
            pl.BlockSpec((tm, D), lambda i: (i, 0)),
            pl.BlockSpec((D, nq * PEER_HALF), lambda i: (0, 0)),
            pl.BlockSpec((nq, PEER_NKEYS, PEER_HALF), lambda i: (0, 0, 0)),
        ],
        out_specs=(pl.BlockSpec((tm, PEER_HEADS * PEER_TOPK), lambda i: (i, 0)),
                   pl.BlockSpec((None, PEER_HEADS * PEER_TOPK, tm), lambda i: (i, 0, 0))),
        scratch_shapes=[
            pltpu.VMEM((nq, tm, PEER_HALF), BF16),
            pltpu.VMEM((PEER_HEADS, PEER_TOPK, tm), F32),
            pltpu.VMEM((PEER_HEADS, PEER_TOPK, tm), F32),
        ],
        compiler_params=_params(("parallel",)),
        name="peer_route",
    )(h2, wq.astype(BF16), sk)


PEER_G = 8
PEER_SLOTS = PEER_HEADS * PEER_TOPK


def _peer_eval_kernel(idx_ref, idxn_ref, gate_ref, h_ref, x_ref, gf_ref, fg_ref, uv_ref, o_ref, buf, sem, *, final):
    G = PEER_G
    R = G * PEER_SLOTS
    D = D_MODEL
    i = pl.program_id(0)
    n = pl.num_programs(0)
    slot = i % 2

    def issue(ids, s):
        def body(r8, carry):
            for u in range(8):
                r = r8 * 8 + u
                pltpu.make_async_copy(uv_ref.at[pl.ds(ids[r], 1), :], buf.at[s, pl.ds(r, 1), :], sem.at[s]).start()
            return carry
        lax.fori_loop(0, R // 8, body, 0)

    @pl.when(i == 0)
    def _():
        issue(idx_ref, 0)

    @pl.when(i + 1 < n)
    def _():
        issue(idxn_ref, 1 - slot)

    pltpu.make_async_copy(uv_ref.at[pl.ds(0, R), :], buf.at[slot], sem.at[slot]).wait()

    lane = lax.broadcasted_iota(jnp.int32, (1, LANES), 1)
    tbase = (i % (LANES // G)) * G
    acts = jnp.zeros((PEER_SLOTS, LANES), F32)
    for g in range(G):
        u = buf[slot, PEER_SLOTS * g:PEER_SLOTS * (g + 1), 0:D]
        prod = u * h_ref[g:g + 1, :]
        part = prod[:, 0:LANES]
        for c in range(1, D // LANES):
            part = part + prod[:, LANES * c:LANES * (c + 1)]
        act = jnp.sum(part, axis=1, keepdims=True)
        acts = jnp.where(lane == tbase + g, act, acts)
    gelu = 0.5 * acts * (1.0 + lax.erf(acts * (2.0 ** -0.5)))
    coef = gate_ref[...] * gelu
    outs = []
    for g in range(G):
        cg = jnp.sum(jnp.where(lane == tbase + g, coef, 0.0), axis=1, keepdims=True)
        v = buf[slot, PEER_SLOTS * g:PEER_SLOTS * (g + 1), D:2 * D]
        outs.append(jnp.sum(v * cg, axis=0, keepdims=True))
    xn = x_ref[...] + gf_ref[...] * jnp.concatenate(outs, axis=0)
    if final:
        ms = jnp.mean(xn * xn, axis=-1, keepdims=True)
        xn = xn * lax.rsqrt(ms + RMS_EPS) * fg_ref[...]
    o_ref[...] = xn


def _peer_eval(eidx, gate_t, h2, x2, mod4, final_g, uv, S, final):
    T, D = x2.shape
    G = PEER_G
    R = G * PEER_SLOTS
    n = T // G
    return pl.pallas_call(
        functools.partial(_peer_eval_kernel, final=final),
        out_shape=jax.ShapeDtypeStruct((T, D), F32),
        grid=(n,),
        in_specs=[
            pl.BlockSpec((R,), lambda i: (i,), memory_space=pltpu.SMEM),
            pl.BlockSpec((R,), lambda i: (jnp.minimum(i + 1, n - 1),), memory_space=pltpu.SMEM),
            pl.BlockSpec((None, PEER_SLOTS, LANES), lambda i: (i // (LANES // G), 0, 0)),
            pl.BlockSpec((G, D), lambda i: (i, 0)),
            pl.BlockSpec((G, D), lambda i: (i, 0)),
            pl.BlockSpec((None, None, 1, D), lambda i: (i // (S // G), 5, 0, 0)),
            pl.BlockSpec((1, D), lambda i: (0, 0)),
            pl.BlockSpec(memory_space=pl.ANY),
        ],
        out_specs=pl.BlockSpec((G, D), lambda i: (i, 0)),
        scratch_shapes=[pltpu.VMEM((2, R, 2 * D), F32), pltpu.SemaphoreType.DMA((2,))],
        compiler_params=_params(("arbitrary",)),
        name="peer_eval",
    )(eidx.reshape(-1), eidx.reshape(-1), gate_t, h2, x2, mod4, final_g.reshape(1, D), uv)


def kernel(x, c, norm_mix_g, norm_ffn_g, final_norm_g, ada_w, ada_b, w_in, w_out, dif_lam, dif_subln_g, rw_mu, rw_w0,
           rw_w2, rw_a0, rw_a2, rw_g2, rw_kk, rw_ka, rw_rk, rw_ln_g, rw_ln_b, fox_bf, peer_wq, peer_subkeys, peer_u,
           peer_v):
    B, S, D = x.shape
    T = B * S
    depth = ada_w.shape[0]
    x2 = x.reshape(T, D)
    mod = _ada_mod(c, ada_w, ada_b)
    for l in range(depth):
        mod4 = mod[l].reshape(B, 6, 1, D)
        dcols, rcols, fqkv, fl = _in_proj(x2, norm_mix_g[l], mod4, _pad_w_in(w_in[l]), S)
        ya = _diff_attention(dcols, dif_lam[l], dif_subln_g[l], l, B, S)
        yb = _rwkv(rcols, rw_mu[l], rw_w0[l], rw_w2[l], rw_a0[l], rw_a2[l], rw_g2[l], rw_kk[l], rw_ka[l],
                   rw_rk[l].reshape(-1), rw_ln_g[l], rw_ln_b[l], B, S, prec=None)
        ccol, crow = _fox_cum(fl, fox_bf[l], B, S)
        yc = _fox_attention(fqkv, ccol, crow, B, S)
        x2, h2 = _out_proj(ya, yb, yc, x2, w_out[l], norm_ffn_g[l], mod4, S)
        eidx, gate_t = _peer_route(h2, peer_wq[l], peer_subkeys[l])
        uv = jnp.concatenate([peer_u[l], peer_v[l]], axis=1)
        x2 = _peer_eval(eidx, gate_t, h2, x2, mod4, final_norm_g, uv, S, final=(l == depth - 1))
    return x2.reshape(B, S, D)
```

```python
import functools
import math

import jax
import jax.numpy as jnp
from jax import lax
from jax.experimental import pallas as pl
from jax.experimental.pallas import tpu as pltpu

F32 = jnp.float32
BF16 = jnp.bfloat16
HIGHEST = lax.Precision.HIGHEST

D_MODEL = 1024
HEAD_DIM = 64
DIFF_HEADS = 6
DIFF_QK_DIM = HEAD_DIM // 2
RWKV_HEADS = 6
FOX_HEADS = 4
DIFF_WIDTH = DIFF_HEADS * HEAD_DIM
RWKV_WIDTH = RWKV_HEADS * HEAD_DIM
FOX_WIDTH = FOX_HEADS * HEAD_DIM
DECAY_LORA = 64
AAA_LORA = 64
GATE_LORA = 128
DIFF_COLS = 3 * DIFF_WIDTH
RWKV_COLS = 3 * RWKV_WIDTH + DECAY_LORA + AAA_LORA + GATE_LORA
PEER_HEADS = 8
PEER_NKEYS = 128
PEER_TOPK = 16
PEER_QDIM = 256
PEER_HALF = PEER_QDIM // 2
RMS_EPS = 1e-6
RWKV_LN_EPS = 64e-5

LANES = 128
SUBLANES = 8
RW_PAD_COLS = 3 * RWKV_WIDTH + 3 * LANES
VMEM_LIMIT = 56 * 1024 * 1024

RW_CHUNK = 64


def _params(sem, vmem=VMEM_LIMIT):
    return pltpu.CompilerParams(dimension_semantics=sem, vmem_limit_bytes=vmem)


def _nt(a, b, precision=None):
    return lax.dot_general(a, b, (((1,), (1,)), ((), ())), preferred_element_type=F32, precision=precision)


def _mm(a, b, precision=None):
    return jnp.dot(a, b, preferred_element_type=F32, precision=precision)


def _ada_kernel(c_ref, w_ref, b_ref, o_ref):
    c = c_ref[...]
    ca = c * jax.nn.sigmoid(c)
    o_ref[...] = _mm(ca, w_ref[...], HIGHEST) + b_ref[...]


def _ada_mod(c, ada_w, ada_b):
    L, D, N = ada_w.shape
    B = c.shape[0]
    tn = 1536
    return pl.pallas_call(
        _ada_kernel,
        out_shape=jax.ShapeDtypeStruct((L, B, N), F32),
        grid=(L, N // tn),
        in_specs=[
            pl.BlockSpec((B, D), lambda l, j: (0, 0)),
            pl.BlockSpec((None, D, tn), lambda l, j: (l, 0, j)),
            pl.BlockSpec((None, 1, tn), lambda l, j: (l, 0, j)),
        ],
        out_specs=pl.BlockSpec((None, B, tn), lambda l, j: (l, 0, j)),
        compiler_params=_params(("parallel", "parallel")),
        name="ada_mod",
    )(c, ada_w, ada_b.reshape(L, 1, N))


ATT_T = 512
QK_DIFF = 2 * DIFF_WIDTH
QK_FOX = 2 * FOX_WIDTH
VT_ROWS = DIFF_WIDTH + FOX_WIDTH
IN_PAD_COLS = QK_DIFF + RW_PAD_COLS + QK_FOX + LANES


def _inproj_kernel(x_ref, g_ref, sh_ref, sc_ref, w_ref, wvt_ref, d_ref, r_ref, f_ref, fl_ref, vtd_ref, vtf_ref, *, ta):
    x = x_ref[...]
    ms = jnp.mean(x * x, axis=-1, keepdims=True)
    y = x * lax.rsqrt(ms + RMS_EPS) * g_ref[...]
    h = (y * (1.0 + sc_ref[...]) + sh_ref[...]).astype(BF16)
    o1 = QK_DIFF
    o2 = o1 + RW_PAD_COLS
    o3 = o2 + QK_FOX
    d_ref[...] = _mm(h, w_ref[:, 0:o1]).astype(BF16)
    r_ref[...] = _mm(h, w_ref[:, o1:o2])
    f_ref[...] = _mm(h, w_ref[:, o2:o3]).astype(BF16)
    fl_ref[...] = _mm(h, w_ref[:, o3:o3 + LANES])
    vt = _nt(wvt_ref[...], h).astype(BF16)
    for s in range(x.shape[0] // ta):
        vtd_ref[s] = vt[0:DIFF_WIDTH, ta * s:ta * (s + 1)]
        vtf_ref[s] = vt[DIFF_WIDTH:, ta * s:ta * (s + 1)]


def _in_proj(x2, g, mod4, w_pad, w_vt, S):
    T, D = x2.shape
    tm = 512 if S % 512 == 0 else S
    ta = min(ATT_T, S)
    nb = S // tm
    row = lambda i: (i, 0)
    return pl.pallas_call(
        functools.partial(_inproj_kernel, ta=ta),
        out_shape=(
            jax.ShapeDtypeStruct((T, QK_DIFF), BF16),
            jax.ShapeDtypeStruct((T, RW_PAD_COLS), F32),
            jax.ShapeDtypeStruct((T, QK_FOX), BF16),
            jax.ShapeDtypeStruct((T, LANES), F32),
            jax.ShapeDtypeStruct((T // ta, DIFF_WIDTH, ta), BF16),
            jax.ShapeDtypeStruct((T // ta, FOX_WIDTH, ta), BF16),
        ),
        grid=(T // tm,),
        in_specs=[
            pl.BlockSpec((tm, D), row),
            pl.BlockSpec((1, D), lambda i: (0, 0)),
            pl.BlockSpec((None, None, 1, D), lambda i: (i // nb, 0, 0, 0)),
            pl.BlockSpec((None, None, 1, D), lambda i: (i // nb, 1, 0, 0)),
            pl.BlockSpec((D, IN_PAD_COLS), lambda i: (0, 0)),
            pl.BlockSpec((VT_ROWS, D), lambda i: (0, 0)),
        ],
        out_specs=(
            pl.BlockSpec((tm, QK_DIFF), row),
            pl.BlockSpec((tm, RW_PAD_COLS), row),
            pl.BlockSpec((tm, QK_FOX), row),
            pl.BlockSpec((tm, LANES), row),
            pl.BlockSpec((tm // ta, DIFF_WIDTH, ta), lambda i: (i, 0, 0)),
            pl.BlockSpec((tm // ta, FOX_WIDTH, ta), lambda i: (i, 0, 0)),
        ),
        compiler_params=_params(("parallel",)),
        name="in_proj",
    )(x2, g.reshape(1, D), mod4, mod4, w_pad, w_vt)


def _pad_w_in(w_in):
    D = w_in.shape[0]
    W = RWKV_WIDTH
    o = DIFF_COLS
    z64 = jnp.zeros((D, LANES - DECAY_LORA), w_in.dtype)
    rw = w_in[:, o:o + RWKV_COLS]
    fx = w_in[:, o + RWKV_COLS:]
    zf = jnp.zeros((D, LANES - FOX_HEADS), w_in.dtype)
    w_pad = jnp.concatenate([
        w_in[:, :QK_DIFF],
        rw[:, :3 * W], rw[:, 3 * W:3 * W + DECAY_LORA], z64,
        rw[:, 3 * W + DECAY_LORA:3 * W + DECAY_LORA + AAA_LORA], z64,
        rw[:, 3 * W + DECAY_LORA + AAA_LORA:],
        fx[:, :QK_FOX], fx[:, 3 * FOX_WIDTH:], zf,
    ], axis=1).astype(BF16)
    w_vt = jnp.concatenate([w_in[:, QK_DIFF:o], fx[:, QK_FOX:3 * FOX_WIDTH]], axis=1).T.astype(BF16)
    return w_pad, w_vt


LOG2E = math.log2(math.e)


def _flash_update(x, s2, vt, m_ref, l_ref, acc_ref):
    m_old = m_ref[x]
    m_new = jnp.maximum(m_old, jnp.max(s2, axis=0, keepdims=True))
    alpha = jnp.exp2(m_old - m_new)
    p = jnp.exp2(s2 - m_new)
    l_ref[x] = alpha * l_ref[x] + jnp.sum(p, axis=0, keepdims=True)
    acc_ref[x] = alpha * acc_ref[x] + _mm(vt, p.astype(BF16))
    m_ref[x] = m_new


def _diff_attn_kernel(lam_ref, g_ref, q_ref, k_ref, vt_ref, o_ref, m_ref, l_ref, acc_ref, *, tq, lam_init):
    i = pl.program_id(2)
    c = (DIFF_QK_DIM ** -0.5) * LOG2E
    lane = lax.broadcasted_iota(jnp.int32, (1, LANES), 1)
    q = q_ref[...]
    zero = jnp.zeros_like(q)
    qm = [jnp.where((lane >= DIFF_QK_DIM * x) & (lane < DIFF_QK_DIM * (x + 1)), q, zero) for x in range(4)]
    m_ref[...] = jnp.full(m_ref.shape, -jnp.inf, F32)
    l_ref[...] = jnp.zeros(l_ref.shape, F32)
    acc_ref[...] = jnp.zeros(acc_ref.shape, F32)

    def step(j, diag):
        k = k_ref[pl.ds(pl.multiple_of(j * tq, tq), tq), :]
        vt = vt_ref[j]
        if diag:
            keep = (lax.broadcasted_iota(jnp.int32, (tq, tq), 1) >= lax.broadcasted_iota(jnp.int32, (tq, tq), 0))
        for x in range(4):
            s2 = _nt(k, qm[x]) * c
            if diag:
                s2 = jnp.where(keep, s2, -jnp.inf)
            _flash_update(x, s2, vt, m_ref, l_ref, acc_ref)

    def body(j, carry):
        step(j, False)
        return carry

    lax.fori_loop(0, i, body, 0)
    step(i, True)

    lp = lam_ref[...]
    lam = (jnp.exp(jnp.sum(lp[0:1] * lp[1:2], axis=-1, keepdims=True))
           - jnp.exp(jnp.sum(lp[2:3] * lp[3:4], axis=-1, keepdims=True)) + lam_init)
    outs = [acc_ref[2 * hh] / l_ref[2 * hh] - lam * (acc_ref[2 * hh + 1] / l_ref[2 * hh + 1]) for hh in range(2)]
    row = lax.broadcasted_iota(jnp.int32, (LANES, 1), 0)
    o = jnp.where(row < HEAD_DIM, outs[0], outs[1])
    sq = o * o
    ms = jnp.where(row < HEAD_DIM, jnp.sum(sq[0:HEAD_DIM], axis=0, keepdims=True),
                   jnp.sum(sq[HEAD_DIM:], axis=0, keepdims=True)) * (1.0 / HEAD_DIM)
    y = o * lax.rsqrt(ms + RMS_EPS) * g_ref[...] * (1.0 - lam_init)
    o_ref[...] = y.T.astype(o_ref.dtype)


def _diff_attention(dqk, vtd, lam_params, subln_g, layer_idx, B, S):
    T = B * S
    tq = min(ATT_T, S)
    nq = S // tq
    npair = DIFF_HEADS // 2
    lam_init = 0.8 - 0.6 * math.exp(-0.3 * layer_idx)
    g2 = jnp.concatenate([subln_g, subln_g]).reshape(LANES, 1).astype(F32)
    return pl.pallas_call(
        functools.partial(_diff_attn_kernel, tq=tq, lam_init=lam_init),
        out_shape=jax.ShapeDtypeStruct((T, DIFF_WIDTH), BF16),
        grid=(B, npair, nq),
        in_specs=[
            pl.BlockSpec((4, DIFF_QK_DIM), lambda b, p, i: (0, 0)),
            pl.BlockSpec((LANES, 1), lambda b, p, i: (0, 0)),
            pl.BlockSpec((tq, LANES), lambda b, p, i: (b * nq + i, p)),
            pl.BlockSpec((S, LANES), lambda b, p, i: (b, npair + p)),
            pl.BlockSpec((nq, LANES, tq), lambda b, p, i: (b, p, 0)),
        ],
        out_specs=pl.BlockSpec((tq, LANES), lambda b, p, i: (b * nq + i, p)),
        scratch_shapes=[
            pltpu.VMEM((4, 1, tq), F32),
            pltpu.VMEM((4, 1, tq), F32),
            pltpu.VMEM((4, LANES, tq), F32),
        ],
        compiler_params=_params(("parallel", "parallel", "arbitrary")),
        name="diff_attn",
    )(lam_params, g2, dqk, dqk, vtd)


def _fox_cum_kernel(f_ref, b_ref, rep_ref, row_ref, *, S, tc):
    rr = lax.broadcasted_iota(jnp.int32, (tc, tc), 0)
    cc = lax.broadcasted_iota(jnp.int32, (tc, tc), 1)
    tri = (rr >= cc).astype(F32)
    sel_r = lax.broadcasted_iota(jnp.int32, (LANES, LANES), 0)
    carry = jnp.zeros((1, LANES), F32)
    for c in range(S // tc):
        z = f_ref[c * tc:(c + 1) * tc, :] + b_ref[...]
        logf = -(jnp.maximum(-z, 0.0) + jnp.log(1.0 + jnp.exp(-jnp.abs(z))))
        cum = _mm(tri, logf, HIGHEST) + carry
        carry = cum[tc - 1:tc, :]
        row_ref[:, c * tc:(c + 1) * tc] = cum.T[0:8, :]
        for h in range(FOX_HEADS):
            rep_ref[h, c * tc:(c + 1) * tc, :] = _mm(cum, (sel_r == h).astype(F32), HIGHEST)


def _fox_cum(fl, b_f, B, S):
    tc = 256 if S % 256 == 0 else S
    bpad = jnp.zeros((1, LANES), F32).at[0, :FOX_HEADS].set(b_f.astype(F32))
    return pl.pallas_call(
        functools.partial(_fox_cum_kernel, S=S, tc=tc),
        out_shape=(jax.ShapeDtypeStruct((B, FOX_HEADS, S, LANES), F32), jax.ShapeDtypeStruct((B, 8, S), F32)),
        grid=(B,),
        in_specs=[pl.BlockSpec((S, LANES), lambda b: (b, 0)), pl.BlockSpec((1, LANES), lambda b: (0, 0))],
        out_specs=(pl.BlockSpec((None, FOX_HEADS, S, LANES), lambda b: (b, 0, 0, 0)),
                   pl.BlockSpec((None, 8, S), lambda b: (b, 0, 0))),
        compiler_params=_params(("parallel",)),
        name="fox_cum",
    )(fl, bpad)


def _fox_attn_kernel(q_ref, k_ref, vt_ref, c0_ref, c1_ref, cr_ref, o_ref, m_ref, l_ref, acc_ref, *, tq):
    p_id = pl.program_id(1)
    i = pl.program_id(2)
    c = (HEAD_DIM ** -0.5) * LOG2E
    lane = lax.broadcasted_iota(jnp.int32, (1, LANES), 1)
    q = q_ref[...]
    zero = jnp.zeros_like(q)
    qm = [jnp.where((lane >= HEAD_DIM * x) & (lane < HEAD_DIM * (x + 1)), q, zero) for x in range(2)]
    ck_refs = (c0_ref, c1_ref)
    cq = [cr_ref[2 * p_id + x, pl.ds(i, 1), :] for x in range(2)]
    m_ref[...] = jnp.full(m_ref.shape, -jnp.inf, F32)
    l_ref[...] = jnp.zeros(l_ref.shape, F32)
    acc_ref[...] = jnp.zeros(acc_ref.shape, F32)

    def step(j, diag):
        off = pl.multiple_of(j * tq, tq)
        k = k_ref[pl.ds(off, tq), :]
        vt = vt_ref[j]
        if diag:
            keep = (lax.broadcasted_iota(jnp.int32, (tq, tq), 1) >= lax.broadcasted_iota(jnp.int32, (tq, tq), 0))
        for x in range(2):
            ck = ck_refs[x][pl.ds(off, tq), :]
            bias = (cq[x] - jnp.concatenate([ck] * (tq // LANES), axis=1)) * LOG2E
            s2 = _nt(k, qm[x]) * c + bias
            if diag:
                s2 = jnp.where(keep, s2, -jnp.inf)
            _flash_update(x, s2, vt, m_ref, l_ref, acc_ref)

    def body(j, carry):
        step(j, False)
        return carry

    lax.fori_loop(0, i, body, 0)
    step(i, True)
    row = lax.broadcasted_iota(jnp.int32, (LANES, 1), 0)
    o = jnp.where(row < HEAD_DIM, acc_ref[0] / l_ref[0], acc_ref[1] / l_ref[1])
    o_ref[...] = o.T.astype(o_ref.dtype)


def _fox_attention(fqk, vtf, crep, crow, B, S):
    T = B * S
    tq = min(ATT_T, S)
    nq = S // tq
    npair = FOX_HEADS // 2
    crow4 = crow.reshape(B, 8, nq, tq)
    rep = lambda x: pl.BlockSpec((None, None, S, LANES), lambda b, p, i: (b, 2 * p + x, 0, 0))
    return pl.pallas_call(
        functools.partial(_fox_attn_kernel, tq=tq),
        out_shape=jax.ShapeDtypeStruct((T, FOX_WIDTH), BF16),
        grid=(B, npair, nq),
        in_specs=[
            pl.BlockSpec((tq, LANES), lambda b, p, i: (b * nq + i, p)),
            pl.BlockSpec((S, LANES), lambda b, p, i: (b, npair + p)),
            pl.BlockSpec((nq, LANES, tq), lambda b, p, i: (b, p, 0)),
            rep(0), rep(1),
            pl.BlockSpec((None, 8, nq, tq), lambda b, p, i: (b, 0, 0, 0)),
        ],
        out_specs=pl.BlockSpec((tq, LANES), lambda b, p, i: (b * nq + i, p)),
        scratch_shapes=[
            pltpu.VMEM((2, 1, tq), F32),
            pltpu.VMEM((2, 1, tq), F32),
            pltpu.VMEM((2, LANES, tq), F32),
        ],
        compiler_params=_params(("parallel", "parallel", "arbitrary")),
        name="fox_attn",
    )(fqk, fqk, vtf, crep, crep, crow4)


def _seg_sum(x, seg, npair):
    return jnp.concatenate(
        [_mm(x[:, LANES * p:LANES * (p + 1)], seg, HIGHEST) for p in range(npair)], axis=1)


def _rwkv_kernel(x_ref, mu_ref, w0_ref, w2_ref, a0_ref, a2_ref, g2_ref, kk_ref, ka_ref, rk_ref, lng_ref, lnb_ref,
                 o_ref, carry_ref, st_ref, at_ref, rt_ref, bt_ref, kt_ref, v_ref, wc_ref, y_ref, g_ref, bon_ref,
                 *, tt, prec):
    i = pl.program_id(1)
    W = RWKV_WIDTH
    C = RW_CHUNK
    npair = RWKV_HEADS // 2

    @pl.when(i == 0)
    def _():
        carry_ref[...] = jnp.zeros(carry_ref.shape, F32)
        st_ref[...] = jnp.zeros(st_ref.shape, F32)

    x = x_ref[...]
    rows = lax.broadcasted_iota(jnp.int32, (tt, 1), 0)
    prev = jnp.where(rows == 0, carry_ref[...], pltpu.roll(x, 1, axis=0))
    carry_ref[...] = x[tt - 1:tt, :]
    xs = x + (prev - x) * mu_ref[...]
    r = xs[:, 0:W]
    k = xs[:, W:2 * W]
    v = xs[:, 2 * W:3 * W]
    xw = xs[:, 3 * W:3 * W + LANES]
    xa = xs[:, 3 * W + LANES:3 * W + 2 * LANES]
    xg = xs[:, 3 * W + 2 * LANES:]
    wl = w0_ref[...] + _mm(jnp.tanh(xw), w2_ref[...], HIGHEST)
    w = -(jnp.maximum(-wl, 0.0) + jnp.log(1.0 + jnp.exp(-jnp.abs(wl)))) - 0.5
    logdec = -jnp.exp(w)
    a = jax.nn.sigmoid(a0_ref[...] + _mm(xa, a2_ref[...], HIGHEST))
    g_ref[...] = _mm(jax.nn.sigmoid(xg), g2_ref[...], HIGHEST)

    r_i = lax.broadcasted_iota(jnp.int32, (LANES, LANES), 0)
    c_i = lax.broadcasted_iota(jnp.int32, (LANES, LANES), 1)
    seg = ((r_i // HEAD_DIM) == (c_i // HEAD_DIM)).astype(F32)
    kkv = k * kk_ref[...]
    kkn = kkv / jnp.maximum(jnp.sqrt(_seg_sum(kkv * kkv, seg, npair)), 1e-12)
    k2 = k * (1.0 + (a - 1.0) * ka_ref[...])
    bon_ref[...] = _seg_sum(r * k2 * rk_ref[...], seg, npair) * v

    rt_i = lax.broadcasted_iota(jnp.int32, (tt, tt), 0)
    ct_i = lax.broadcasted_iota(jnp.int32, (tt, tt), 1)
    tri = (((rt_i // C) == (ct_i // C)) & (rt_i >= ct_i)).astype(F32)
    cum = _mm(tri, logdec, HIGHEST)
    winv = jnp.exp(-cum)
    wcum = jnp.exp(cum)
    at_ref[...] = -kkn * jnp.exp(cum - logdec)
    bt_ref[...] = kkn * a * winv
    kt_ref[...] = k2 * winv
    rt_ref[...] = r * wcum
    v_ref[...] = v
    wc_ref[...] = wcum

    lane = lax.broadcasted_iota(jnp.int32, (1, LANES), 1)
    lo = lane < HEAD_DIM
    tpos = r_i % C
    ipos = c_i % C
    strict = tpos > ipos
    incl = tpos >= ipos
    eye = r_i == c_i

    def stack2(m):
        return jnp.concatenate([jnp.where(lo, m, 0.0), jnp.where(lo, 0.0, m)], axis=0)

    def cast(m):
        return m if prec is not None else m.astype(BF16)

    def mm(p, q):
        return _mm(cast(p), cast(q), prec)

    def nt(p, q):
        return _nt(cast(p), cast(q), prec)

    def chunk(c, carry):
        r0 = pl.multiple_of(c * C, C)
        for p in range(npair):
            ls = slice(LANES * p, LANES * (p + 1))
            a2 = stack2(at_ref[pl.ds(r0, C), ls])
            r2 = stack2(rt_ref[pl.ds(r0, C), ls])
            b2 = stack2(bt_ref[pl.ds(r0, C), ls])
            k2s = stack2(kt_ref[pl.ds(r0, C), ls])
            v2 = stack2(v_ref[pl.ds(r0, C), ls])
            wlast = wc_ref[pl.ds(pl.multiple_of(r0 + C - 8, 8), 8), ls][7:8, :]
            ar = jnp.concatenate([a2, r2], axis=0)
            mb = nt(ar, b2)
            mk = nt(ar, k2s)
            lab = jnp.where(strict, mb[0:LANES], 0.0)
            mrb = jnp.where(incl, mb[LANES:], 0.0)
            lak = jnp.where(strict, mk[0:LANES], 0.0)
            mrk = jnp.where(incl, mk[LANES:], 0.0)
            xx = jnp.concatenate([a2, mm(lak, v2)], axis=1)
            lp = lab
            xx = xx + mm(lp, xx)
            for _ in range(5):
                lp = mm(lp, lp)
                xx = xx + mm(lp, xx)
            mq = mm(mrb, xx)
            q1 = r2 + mq[:, 0:LANES]
            q2 = mq[:, LANES:] + mm(mrk, v2)
            bx = mm((b2 * wlast).T, xx)
            kv = mm((k2s * wlast).T, v2)
            gmat = jnp.where(eye, jnp.broadcast_to(wlast, (LANES, LANES)), 0.0) + bx[:, 0:LANES]
            hmat = bx[:, LANES:] + kv
            gs = mm(jnp.concatenate([gmat, q1], axis=0), st_ref[p])
            st_ref[p] = gs[0:LANES] + hmat
            yy = gs[LANES:] + q2
            y_ref[pl.ds(r0, C), ls] = yy[0:C] + yy[C:]
        return carry

    lax.fori_loop(0, tt // C, chunk, 0)

    y = y_ref[...]
    inv = 1.0 / HEAD_DIM
    mean = _seg_sum(y, seg, npair) * inv
    yc = y - mean
    var = _seg_sum(yc * yc, seg, npair) * inv
    yn = yc * lax.rsqrt(var + RWKV_LN_EPS) * lng_ref[...] + lnb_ref[...]
    o_ref[...] = ((yn + bon_ref[...]) * g_ref[...]).astype(o_ref.dtype)


def _rwkv(rcols, mu, w0, w2, a0, a2, g2, k_k, k_a, r_k, ln_g, ln_b, B, S, prec=HIGHEST):
    T = B * S
    W = RWKV_WIDTH
    tt = 512 if S % 512 == 0 else S
    nt_ = S // tt
    npair = RWKV_HEADS // 2
    pad = LANES - DECAY_LORA
    mu_p = jnp.concatenate([mu[:3 * W], mu[3 * W:3 * W + DECAY_LORA], jnp.zeros((pad,), F32),
                            mu[3 * W + DECAY_LORA:3 * W + DECAY_LORA + AAA_LORA], jnp.zeros((pad,), F32),
                            mu[3 * W + DECAY_LORA + AAA_LORA:]]).reshape(1, RW_PAD_COLS)
    w2p = jnp.concatenate([w2, jnp.zeros((pad, W), F32)], axis=0)
    a2p = jnp.concatenate([a2, jnp.zeros((pad, W), F32)], axis=0)
    vec = lambda t: t.reshape(1, W).astype(F32)
    full = lambda shape: pl.BlockSpec(shape, lambda b, i: (0,) * len(shape))
    sc = lambda: pltpu.VMEM((tt, W), F32)
    return pl.pallas_call(
        functools.partial(_rwkv_kernel, tt=tt, prec=prec),
        out_shape=jax.ShapeDtypeStruct((T, W), BF16),
        grid=(B, nt_),
        in_specs=[
            pl.BlockSpec((tt, RW_PAD_COLS), lambda b, i: (b * nt_ + i, 0)),
            full((1, RW_PAD_COLS)), full((1, W)), full((LANES, W)), full((1, W)), full((LANES, W)),
            full((GATE_LORA, W)), full((1, W)), full((1, W)), full((1, W)), full((1, W)), full((1, W)),
        ],
        out_specs=pl.BlockSpec((tt, W), lambda b, i: (b * nt_ + i, 0)),
        scratch_shapes=[
            pltpu.VMEM((1, RW_PAD_COLS), F32),
            pltpu.VMEM((npair, LANES, LANES), F32),
            sc(), sc(), sc(), sc(), sc(), sc(), sc(), sc(), sc(),
        ],
        compiler_params=_params(("parallel", "arbitrary")),
        name="rwkv7",
    )(rcols, mu_p, vec(w0), w2p, vec(a0), a2p, g2, vec(k_k), vec(k_a), vec(r_k), vec(ln_g), vec(ln_b))


def _outproj_kernel(ya_ref, yb_ref, yc_ref, x_ref, w_ref, gm_ref, g_ref, sh_ref, sc_ref, xo_ref, ho_ref):
    o1 = DIFF_WIDTH
    o2 = o1 + RWKV_WIDTH
    mix = (_mm(ya_ref[...], w_ref[0:o1, :]) + _mm(yb_ref[...], w_ref[o1:o2, :])
           + _mm(yc_ref[...], w_ref[o2:, :]))
    xn = x_ref[...] + gm_ref[...] * mix
    xo_ref[...] = xn
    ms = jnp.mean(xn * xn, axis=-1, keepdims=True)
    y = xn * lax.rsqrt(ms + RMS_EPS) * g_ref[...]
    ho_ref[...] = y * (1.0 + sc_ref[...]) + sh_ref[...]


def _out_proj(ya, yb, yc, x2, w_out, g, mod4, S):
    T, D = x2.shape
    tm = 512 if S % 512 == 0 else S
    nb = S // tm
    row = lambda i: (i, 0)
    modspec = lambda which: pl.BlockSpec((None, None, 1, D), lambda i: (i // nb, which, 0, 0))
    return pl.pallas_call(
        _outproj_kernel,
        out_shape=(jax.ShapeDtypeStruct((T, D), F32), jax.ShapeDtypeStruct((T, D), F32)),
        grid=(T // tm,),
        in_specs=[
            pl.BlockSpec((tm, DIFF_WIDTH), row), pl.BlockSpec((tm, RWKV_WIDTH), row), pl.BlockSpec((tm, FOX_WIDTH), row),
            pl.BlockSpec((tm, D), row),
            pl.BlockSpec((D, D), lambda i: (0, 0)),
            modspec(2),
            pl.BlockSpec((1, D), lambda i: (0, 0)),
            modspec(3), modspec(4),
        ],
        out_specs=(pl.BlockSpec((tm, D), row), pl.BlockSpec((tm, D), row)),
        compiler_params=_params(("parallel",)),
        name="out_proj",
    )(ya, yb, yc, x2, w_out.astype(BF16), mod4, g.reshape(1, D), mod4, mod4)


def _top16(s, iota_f, n):
    vals, poss = [], []
    for _ in range(PEER_TOPK):
        m = jnp.max(s, axis=0, keepdims=True)
        pos = jnp.min(jnp.where(s == m, iota_f, float(n)), axis=0, keepdims=True)
        vals.append(m)
        poss.append(pos)
        s = jnp.where(iota_f == pos, -jnp.inf, s)
    return jnp.concatenate(vals, axis=0), jnp.concatenate(poss, axis=0)


def _peer_route_kernel(h_ref, wq_ref, sk_ref, idx_ref, gate_ref, q_scr, e_scr, g_scr):
    K = PEER_TOPK
    hb = h_ref[...].astype(BF16)
    q = _mm(hb, wq_ref[...])
    for hc in range(2 * PEER_HEADS):
        q_scr[hc] = q[:, LANES * hc:LANES * (hc + 1)].astype(BF16)
    iota_n = lax.broadcasted_iota(jnp.int32, (PEER_NKEYS, LANES), 0).astype(F32)
    iota_c = lax.broadcasted_iota(jnp.int32, (K * K, LANES), 0).astype(F32)

    def head(h, carry):
        sv0, si0 = _top16(_nt(sk_ref[2 * h], q_scr[2 * h]), iota_n, PEER_NKEYS)
        sv1, si1 = _top16(_nt(sk_ref[2 * h + 1], q_scr[2 * h + 1]), iota_n, PEER_NKEYS)
        cand = jnp.concatenate([sv0[a:a + 1, :] + sv1 for a in range(K)], axis=0)
        cidx = jnp.concatenate([si0[a:a + 1, :] * float(PEER_NKEYS) + si1 for a in range(K)], axis=0)
        fv, es = [], []
        for _ in range(K):
            m = jnp.max(cand, axis=0, keepdims=True)
            pos = jnp.min(jnp.where(cand == m, iota_c, float(K * K)), axis=0, keepdims=True)
            hit = iota_c == pos
            fv.append(m)
            es.append(jnp.max(jnp.where(hit, cidx, -1.0), axis=0, keepdims=True))
            cand = jnp.where(hit, -jnp.inf, cand)
        fv = jnp.concatenate(fv, axis=0)
        ex = jnp.exp(fv - fv[0:1, :])
        g_scr[h] = ex / jnp.sum(ex, axis=0, keepdims=True)
        e_scr[h] = jnp.concatenate(es, axis=0)
        return carry

    lax.fori_loop(0, PEER_HEADS, head, 0)
    e = e_scr[...].reshape(PEER_HEADS * K, LANES)
    idx_ref[...] = e.T.astype(jnp.int32)
    gate_ref[...] = g_scr[...].reshape(PEER_HEADS * K, LANES)


def _peer_route(h2, wq, subkeys):
    T, D = h2.shape
    tm = LANES
    nq = 2 * PEER_HEADS
    sk = subkeys.reshape(nq, PEER_NKEYS, PEER_HALF).astype(BF16)
    return pl.pallas_call(
        _peer_route_kernel,
        out_shape=(jax.ShapeDtypeStruct((T, PEER_HEADS * PEER_TOPK), jnp.int32),
                   jax.ShapeDtypeStruct((T // tm, PEER_HEADS * PEER_TOPK, tm), F32)),
        grid=(T // tm,),
        in_specs=[
            pl.BlockSpec((tm, D), lambda i: (i, 0)),
            pl.BlockSpec((D, nq * PEER_HALF), lambda i: (0, 0)),
            pl.BlockSpec((nq, PEER_NKEYS, PEER_HALF), lambda i: (0, 0, 0)),
        ],
        out_specs=(pl.BlockSpec((tm, PEER_HEADS * PEER_TOPK), lambda i: (i, 0)),
                   pl.BlockSpec((None, PEER_HEADS * PEER_TOPK, tm), lambda i: (i, 0, 0))),
        scratch_shapes=[
            pltpu.VMEM((nq, tm, PEER_HALF), BF16),
            pltpu.VMEM((PEER_HEADS, PEER_TOPK, tm), F32),
            pltpu.VMEM((PEER_HEADS, PEER_TOPK, tm), F32),
        ],
        compiler_params=_params(("parallel",)),
        name="peer_route",
    )(h2, wq.astype(BF16), sk)


PEER_G = 16
PEER_SLOTS = PEER_HEADS * PEER_TOPK


def _peer_eval_kernel(idx_ref, idxn_ref, gate_ref, h_ref, x_ref, gf_ref, fg_ref, uv_ref, o_ref, buf, sem, *, final):
    G = PEER_G
    R = G * PEER_SLOTS
    D = D_MODEL
    i = pl.program_id(0)
    n = pl.num_programs(0)
    slot = i % 2

    def issue(ids, s):
        def body(r8, carry):
            for u in range(SUBLANES):
                pltpu.make_async_copy(uv_ref.at[ids[r8 * SUBLANES + u]], buf.at[s, r8, pl.ds(u, 1), :],
                                      sem.at[s]).start(priority=u % 2)
            return carry
        lax.fori_loop(0, R // SUBLANES, body, 0)

    @pl.when(i == 0)
    def _():
        issue(idx_ref, 0)

    @pl.when(i + 1 < n)
    def _():
        issue(idxn_ref, 1 - slot)

    pltpu.make_async_copy(buf.at[slot], buf.at[slot], sem.at[slot]).wait()

    lane = lax.broadcasted_iota(jnp.int32, (1, LANES), 1)
    tbase = (i % (LANES // G)) * G
    tiles = PEER_SLOTS // SUBLANES
    acts = jnp.zeros((PEER_SLOTS, LANES), F32)
    for g in range(G):
        u = buf[slot, tiles * g:tiles * (g + 1), :, 0:D].reshape(PEER_SLOTS, D)
        prod = u * h_ref[g:g + 1, :]
        part = prod[:, 0:LANES]
        for c in range(1, D // LANES):
            part = part + prod[:, LANES * c:LANES * (c + 1)]
        act = jnp.sum(part, axis=1, keepdims=True)
        acts = jnp.where(lane == tbase + g, act, acts)
    gelu = 0.5 * acts * (1.0 + lax.erf(acts * (2.0 ** -0.5)))
    coef = gate_ref[...] * gelu
    outs = []
    for g in range(G):
        cg = jnp.sum(jnp.where(lane == tbase + g, coef, 0.0), axis=1, keepdims=True)
        v = buf[slot, tiles * g:tiles * (g + 1), :, D:2 * D].reshape(PEER_SLOTS, D)
        outs.append(jnp.sum(v * cg, axis=0, keepdims=True))
    xn = x_ref[...] + gf_ref[...] * jnp.concatenate(outs, axis=0)
    if final:
        ms = jnp.mean(xn * xn, axis=-1, keepdims=True)
        xn = xn * lax.rsqrt(ms + RMS_EPS) * fg_ref[...]
    o_ref[...] = xn


def _peer_eval(eidx, gate_t, h2, x2, mod4, final_g, uv, S, final):
    T, D = x2.shape
    G = PEER_G
    R = G * PEER_SLOTS
    n = T // G
    return pl.pallas_call(
        functools.partial(_peer_eval_kernel, final=final),
        out_shape=jax.ShapeDtypeStruct((T, D), F32),
        grid=(n,),
        in_specs=[
            pl.BlockSpec((R,), lambda i: (i,), memory_space=pltpu.SMEM),
            pl.BlockSpec((R,), lambda i: (jnp.minimum(i + 1, n - 1),), memory_space=pltpu.SMEM),
            pl.BlockSpec((None, PEER_SLOTS, LANES), lambda i: (i // (LANES // G), 0, 0)),
            pl.BlockSpec((G, D), lambda i: (i, 0)),
            pl.BlockSpec((G, D), lambda i: (i, 0)),
            pl.BlockSpec((None, None, 1, D), lambda i: (i // (S // G), 5, 0, 0)),
            pl.BlockSpec((1, D), lambda i: (0, 0)),
            pl.BlockSpec(memory_space=pl.ANY),
        ],
        out_specs=pl.BlockSpec((G, D), lambda i: (i, 0)),
        scratch_shapes=[pltpu.VMEM((2, R // SUBLANES, SUBLANES, 2 * D), F32), pltpu.SemaphoreType.DMA((2,))],
        compiler_params=_params(("arbitrary",)),
        name="peer_eval",
    )(eidx.reshape(-1), eidx.reshape(-1), gate_t, h2, x2, mod4, final_g.reshape(1, D), uv)


def kernel(x, c, norm_mix_g, norm_ffn_g, final_norm_g, ada_w, ada_b, w_in, w_out, dif_lam, dif_subln_g, rw_mu, rw_w0,
           rw_w2, rw_a0, rw_a2, rw_g2, rw_kk, rw_ka, rw_rk, rw_ln_g, rw_ln_b, fox_bf, peer_wq, peer_subkeys, peer_u,
           peer_v):
    B, S, D = x.shape
    T = B * S
    depth = ada_w.shape[0]
    x2 = x.reshape(T, D)
    mod = _ada_mod(c, ada_w, ada_b)
    for l in range(depth):
        mod4 = mod[l].reshape(B, 6, 1, D)
        w_pad, w_vt = _pad_w_in(w_in[l])
        dqk, rcols, fqk, fl, vtd, vtf = _in_proj(x2, norm_mix_g[l], mod4, w_pad, w_vt, S)
        ya = _diff_attention(dqk, vtd, dif_lam[l], dif_subln_g[l], l, B, S)
        yb = _rwkv(rcols, rw_mu[l], rw_w0[l], rw_w2[l], rw_a0[l], rw_a2[l], rw_g2[l], rw_kk[l], rw_ka[l],
                   rw_rk[l].reshape(-1), rw_ln_g[l], rw_ln_b[l], B, S, prec=None)
        crep, crow = _fox_cum(fl, fox_bf[l], B, S)
        yc = _fox_attention(fqk, vtf, crep, crow, B, S)
        x2, h2 = _out_proj(ya, yb, yc, x2, w_out[l], norm_ffn_g[l], mod4, S)
        eidx, gate_t = _peer_route(h2, peer_wq[l], peer_subkeys[l])
        uv = jnp.concatenate([peer_u[l], peer_v[l]], axis=1).reshape(-1, 1, 2 * D)
        x2 = _peer_eval(eidx, gate_t, h2, x2, mod4, final_norm_g, uv, S, final=(l == depth - 1))
    return x2.reshape(B, S, D)
```

```python
import functools
import math

import jax
import jax.numpy as jnp
from jax import lax
from jax.experimental import pallas as pl
from jax.experimental.pallas import tpu as pltpu

F32 = jnp.float32
BF16 = jnp.bfloat16
HIGHEST = lax.Precision.HIGHEST

D_MODEL = 1024
HEAD_DIM = 64
DIFF_HEADS = 6
DIFF_QK_DIM = HEAD_DIM // 2
RWKV_HEADS = 6
FOX_HEADS = 4
DIFF_WIDTH = DIFF_HEADS * HEAD_DIM
RWKV_WIDTH = RWKV_HEADS * HEAD_DIM
FOX_WIDTH = FOX_HEADS * HEAD_DIM
DECAY_LORA = 64
AAA_LORA = 64
GATE_LORA = 128
DIFF_COLS = 3 * DIFF_WIDTH
RWKV_COLS = 3 * RWKV_WIDTH + DECAY_LORA + AAA_LORA + GATE_LORA
PEER_HEADS = 8
PEER_NKEYS = 128
PEER_TOPK = 16
PEER_QDIM = 256
PEER_HALF = PEER_QDIM // 2
RMS_EPS = 1e-6
RWKV_LN_EPS = 64e-5

LANES = 128
SUBLANES = 8
RW_PAD_COLS = 3 * RWKV_WIDTH + 3 * LANES
VMEM_LIMIT = 56 * 1024 * 1024

RW_CHUNK = 64
RW_CHUNKS_PER_ITER = 2


def _params(sem, vmem=VMEM_LIMIT):
    return pltpu.CompilerParams(dimension_semantics=sem, vmem_limit_bytes=vmem)


def _nt(a, b, precision=None):
    return lax.dot_general(a, b, (((1,), (1,)), ((), ())), preferred_element_type=F32, precision=precision)


def _mm(a, b, precision=None):
    return jnp.dot(a, b, preferred_element_type=F32, precision=precision)


def _ada_kernel(c_ref, w_ref, b_ref, o_ref):
    c = c_ref[...]
    ca = c * jax.nn.sigmoid(c)
    o_ref[...] = _mm(ca, w_ref[...], HIGHEST) + b_ref[...]


def _ada_mod(c, ada_w, ada_b):
    L, D, N = ada_w.shape
    B = c.shape[0]
    tn = 1536
    return pl.pallas_call(
        _ada_kernel,
        out_shape=jax.ShapeDtypeStruct((L, B, N), F32),
        grid=(L, N // tn),
        in_specs=[
            pl.BlockSpec((B, D), lambda l, j: (0, 0)),
            pl.BlockSpec((None, D, tn), lambda l, j: (l, 0, j)),
            pl.BlockSpec((None, 1, tn), lambda l, j: (l, 0, j)),
        ],
        out_specs=pl.BlockSpec((None, B, tn), lambda l, j: (l, 0, j)),
        compiler_params=_params(("parallel", "parallel")),
        name="ada_mod",
    )(c, ada_w, ada_b.reshape(L, 1, N))


ATT_T = 512
QK_DIFF = 2 * DIFF_WIDTH
QK_FOX = 2 * FOX_WIDTH
VT_ROWS = DIFF_WIDTH + FOX_WIDTH
IN_PAD_COLS = QK_DIFF + RW_PAD_COLS + QK_FOX + LANES


def _inproj_kernel(x_ref, g_ref, sh_ref, sc_ref, w_ref, wvt_ref, d_ref, r_ref, f_ref, fl_ref, vtd_ref, vtf_ref, *, ta):
    x = x_ref[...]
    ms = jnp.mean(x * x, axis=-1, keepdims=True)
    y = x * lax.rsqrt(ms + RMS_EPS) * g_ref[...]
    h = (y * (1.0 + sc_ref[...]) + sh_ref[...]).astype(BF16)
    o1 = QK_DIFF
    o2 = o1 + RW_PAD_COLS
    o3 = o2 + QK_FOX
    d_ref[...] = _mm(h, w_ref[:, 0:o1]).astype(BF16)
    r_ref[...] = _mm(h, w_ref[:, o1:o2])
    f_ref[...] = _mm(h, w_ref[:, o2:o3]).astype(BF16)
    fl_ref[...] = _mm(h, w_ref[:, o3:o3 + LANES])
    vt = _nt(wvt_ref[...], h).astype(BF16)
    for s in range(x.shape[0] // ta):
        vtd_ref[s] = vt[0:DIFF_WIDTH, ta * s:ta * (s + 1)]
        vtf_ref[s] = vt[DIFF_WIDTH:, ta * s:ta * (s + 1)]


def _in_proj(x2, g, mod4, w_pad, w_vt, S):
    T, D = x2.shape
    tm = 512 if S % 512 == 0 else S
    ta = min(ATT_T, S)
    nb = S // tm
    row = lambda i: (i, 0)
    return pl.pallas_call(
        functools.partial(_inproj_kernel, ta=ta),
        out_shape=(
            jax.ShapeDtypeStruct((T, QK_DIFF), BF16),
            jax.ShapeDtypeStruct((T, RW_PAD_COLS), F32),
            jax.ShapeDtypeStruct((T, QK_FOX), BF16),
            jax.ShapeDtypeStruct((T, LANES), F32),
            jax.ShapeDtypeStruct((T // ta, DIFF_WIDTH, ta), BF16),
            jax.ShapeDtypeStruct((T // ta, FOX_WIDTH, ta), BF16),
        ),
        grid=(T // tm,),
        in_specs=[
            pl.BlockSpec((tm, D), row),
            pl.BlockSpec((1, D), lambda i: (0, 0)),
            pl.BlockSpec((None, None, 1, D), lambda i: (i // nb, 0, 0, 0)),
            pl.BlockSpec((None, None, 1, D), lambda i: (i // nb, 1, 0, 0)),
            pl.BlockSpec((D, IN_PAD_COLS), lambda i: (0, 0)),
            pl.BlockSpec((VT_ROWS, D), lambda i: (0, 0)),
        ],
        out_specs=(
            pl.BlockSpec((tm, QK_DIFF), row),
            pl.BlockSpec((tm, RW_PAD_COLS), row),
            pl.BlockSpec((tm, QK_FOX), row),
            pl.BlockSpec((tm, LANES), row),
            pl.BlockSpec((tm // ta, DIFF_WIDTH, ta), lambda i: (i, 0, 0)),
            pl.BlockSpec((tm // ta, FOX_WIDTH, ta), lambda i: (i, 0, 0)),
        ),
        compiler_params=_params(("parallel",)),
        name="in_proj",
    )(x2, g.reshape(1, D), mod4, mod4, w_pad, w_vt)


def _pad_w_in(w_in):
    D = w_in.shape[0]
    W = RWKV_WIDTH
    o = DIFF_COLS
    z64 = jnp.zeros((D, LANES - DECAY_LORA), w_in.dtype)
    rw = w_in[:, o:o + RWKV_COLS]
    fx = w_in[:, o + RWKV_COLS:]
    zf = jnp.zeros((D, LANES - FOX_HEADS), w_in.dtype)
    w_pad = jnp.concatenate([
        w_in[:, :QK_DIFF],
        rw[:, :3 * W], rw[:, 3 * W:3 * W + DECAY_LORA], z64,
        rw[:, 3 * W + DECAY_LORA:3 * W + DECAY_LORA + AAA_LORA], z64,
        rw[:, 3 * W + DECAY_LORA + AAA_LORA:],
        fx[:, :QK_FOX], fx[:, 3 * FOX_WIDTH:], zf,
    ], axis=1).astype(BF16)
    w_vt = jnp.concatenate([w_in[:, QK_DIFF:o], fx[:, QK_FOX:3 * FOX_WIDTH]], axis=1).T.astype(BF16)
    return w_pad, w_vt


LOG2E = math.log2(math.e)


def _flash_step(s2s, vt, m_ref, l_ref, acc_ref):
    n = len(s2s)
    m_old = [m_ref[x] for x in range(n)]
    m_new = [jnp.maximum(m_old[x], jnp.max(s2s[x], axis=0, keepdims=True)) for x in range(n)]
    alpha = [jnp.exp2(m_old[x] - m_new[x]) for x in range(n)]
    p = [jnp.exp2(s2s[x] - m_new[x]) for x in range(n)]
    pv = [_mm(vt, p[x].astype(BF16)) for x in range(n)]
    for x in range(n):
        l_ref[x] = alpha[x] * l_ref[x] + jnp.sum(p[x], axis=0, keepdims=True)
        acc_ref[x] = alpha[x] * acc_ref[x] + pv[x]
        m_ref[x] = m_new[x]


def _diff_attn_kernel(lam_ref, g_ref, q_ref, k_ref, vt_ref, o_ref, m_ref, l_ref, acc_ref, *, tq, lam_init):
    i = pl.program_id(2)
    c = (DIFF_QK_DIM ** -0.5) * LOG2E
    lane = lax.broadcasted_iota(jnp.int32, (1, LANES), 1)
    q = q_ref[...]
    zero = jnp.zeros_like(q)
    qm = [jnp.where((lane >= DIFF_QK_DIM * x) & (lane < DIFF_QK_DIM * (x + 1)), q, zero) for x in range(4)]
    m_ref[...] = jnp.full(m_ref.shape, -jnp.inf, F32)
    l_ref[...] = jnp.zeros(l_ref.shape, F32)
    acc_ref[...] = jnp.zeros(acc_ref.shape, F32)

    def step(j, diag):
        k = k_ref[pl.ds(pl.multiple_of(j * tq, tq), tq), :]
        vt = vt_ref[j]
        if diag:
            keep = (lax.broadcasted_iota(jnp.int32, (tq, tq), 1) >= lax.broadcasted_iota(jnp.int32, (tq, tq), 0))
        s2s = [_nt(k, qm[x]) * c for x in range(4)]
        if diag:
            s2s = [jnp.where(keep, s2, -jnp.inf) for s2 in s2s]
        _flash_step(s2s, vt, m_ref, l_ref, acc_ref)

    def body(j, carry):
        step(j, False)
        return carry

    lax.fori_loop(0, i, body, 0)
    step(i, True)

    lp = lam_ref[...]
    lam = (jnp.exp(jnp.sum(lp[0:1] * lp[1:2], axis=-1, keepdims=True))
           - jnp.exp(jnp.sum(lp[2:3] * lp[3:4], axis=-1, keepdims=True)) + lam_init)
    outs = [acc_ref[2 * hh] / l_ref[2 * hh] - lam * (acc_ref[2 * hh + 1] / l_ref[2 * hh + 1]) for hh in range(2)]
    row = lax.broadcasted_iota(jnp.int32, (LANES, 1), 0)
    o = jnp.where(row < HEAD_DIM, outs[0], outs[1])
    sq = o * o
    ms = jnp.where(row < HEAD_DIM, jnp.sum(sq[0:HEAD_DIM], axis=0, keepdims=True),
                   jnp.sum(sq[HEAD_DIM:], axis=0, keepdims=True)) * (1.0 / HEAD_DIM)
    y = o * lax.rsqrt(ms + RMS_EPS) * g_ref[...] * (1.0 - lam_init)
    o_ref[...] = y.T.astype(o_ref.dtype)


def _diff_attention(dqk, vtd, lam_params, subln_g, layer_idx, B, S):
    T = B * S
    tq = min(ATT_T, S)
    nq = S // tq
    npair = DIFF_HEADS // 2
    lam_init = 0.8 - 0.6 * math.exp(-0.3 * layer_idx)
    g2 = jnp.concatenate([subln_g, subln_g]).reshape(LANES, 1).astype(F32)
    return pl.pallas_call(
        functools.partial(_diff_attn_kernel, tq=tq, lam_init=lam_init),
        out_shape=jax.ShapeDtypeStruct((T, DIFF_WIDTH), BF16),
        grid=(B, npair, nq),
        in_specs=[
            pl.BlockSpec((4, DIFF_QK_DIM), lambda b, p, i: (0, 0)),
            pl.BlockSpec((LANES, 1), lambda b, p, i: (0, 0)),
            pl.BlockSpec((tq, LANES), lambda b, p, i: (b * nq + i, p)),
            pl.BlockSpec((S, LANES), lambda b, p, i: (b, npair + p)),
            pl.BlockSpec((nq, LANES, tq), lambda b, p, i: (b, p, 0)),
        ],
        out_specs=pl.BlockSpec((tq, LANES), lambda b, p, i: (b * nq + i, p)),
        scratch_shapes=[
            pltpu.VMEM((4, 1, tq), F32),
            pltpu.VMEM((4, 1, tq), F32),
            pltpu.VMEM((4, LANES, tq), F32),
        ],
        compiler_params=_params(("parallel", "parallel", "arbitrary")),
        name="diff_attn",
    )(lam_params, g2, dqk, dqk, vtd)


def _fox_cum_kernel(f_ref, b_ref, rep_ref, row_ref, *, S, tc):
    rr = lax.broadcasted_iota(jnp.int32, (tc, tc), 0)
    cc = lax.broadcasted_iota(jnp.int32, (tc, tc), 1)
    tri = (rr >= cc).astype(F32)
    sel_r = lax.broadcasted_iota(jnp.int32, (LANES, LANES), 0)
    carry = jnp.zeros((1, LANES), F32)
    for c in range(S // tc):
        z = f_ref[c * tc:(c + 1) * tc, :] + b_ref[...]
        logf = -(jnp.maximum(-z, 0.0) + jnp.log(1.0 + jnp.exp(-jnp.abs(z))))
        cum = _mm(tri, logf, HIGHEST) + carry
        carry = cum[tc - 1:tc, :]
        row_ref[:, c * tc:(c + 1) * tc] = cum.T[0:8, :]
        for h in range(FOX_HEADS):
            rep_ref[h, c * tc:(c + 1) * tc, :] = _mm(cum, (sel_r == h).astype(F32), HIGHEST)


def _fox_cum(fl, b_f, B, S):
    tc = 256 if S % 256 == 0 else S
    bpad = jnp.zeros((1, LANES), F32).at[0, :FOX_HEADS].set(b_f.astype(F32))
    return pl.pallas_call(
        functools.partial(_fox_cum_kernel, S=S, tc=tc),
        out_shape=(jax.ShapeDtypeStruct((B, FOX_HEADS, S, LANES), F32), jax.ShapeDtypeStruct((B, 8, S), F32)),
        grid=(B,),
        in_specs=[pl.BlockSpec((S, LANES), lambda b: (b, 0)), pl.BlockSpec((1, LANES), lambda b: (0, 0))],
        out_specs=(pl.BlockSpec((None, FOX_HEADS, S, LANES), lambda b: (b, 0, 0, 0)),
                   pl.BlockSpec((None, 8, S), lambda b: (b, 0, 0))),
        compiler_params=_params(("parallel",)),
        name="fox_cum",
    )(fl, bpad)


def _fox_attn_kernel(q_ref, k_ref, vt_ref, c0_ref, c1_ref, cr_ref, o_ref, m_ref, l_ref, acc_ref, *, tq):
    p_id = pl.program_id(1)
    i = pl.program_id(2)
    c = (HEAD_DIM ** -0.5) * LOG2E
    lane = lax.broadcasted_iota(jnp.int32, (1, LANES), 1)
    q = q_ref[...]
    zero = jnp.zeros_like(q)
    qm = [jnp.where((lane >= HEAD_DIM * x) & (lane < HEAD_DIM * (x + 1)), q, zero) for x in range(2)]
    ck_refs = (c0_ref, c1_ref)
    cq = [cr_ref[2 * p_id + x, pl.ds(i, 1), :] for x in range(2)]
    m_ref[...] = jnp.full(m_ref.shape, -jnp.inf, F32)
    l_ref[...] = jnp.zeros(l_ref.shape, F32)
    acc_ref[...] = jnp.zeros(acc_ref.shape, F32)

    def step(j, diag):
        off = pl.multiple_of(j * tq, tq)
        k = k_ref[pl.ds(off, tq), :]
        vt = vt_ref[j]
        if diag:
            keep = (lax.broadcasted_iota(jnp.int32, (tq, tq), 1) >= lax.broadcasted_iota(jnp.int32, (tq, tq), 0))
        s2s = []
        for x in range(2):
            ck = ck_refs[x][pl.ds(off, tq), :]
            bias = (cq[x] - jnp.concatenate([ck] * (tq // LANES), axis=1)) * LOG2E
            s2s.append(_nt(k, qm[x]) * c + bias)
        if diag:
            s2s = [jnp.where(keep, s2, -jnp.inf) for s2 in s2s]
        _flash_step(s2s, vt, m_ref, l_ref, acc_ref)

    def body(j, carry):
        step(j, False)
        return carry

    lax.fori_loop(0, i, body, 0)
    step(i, True)
    row = lax.broadcasted_iota(jnp.int32, (LANES, 1), 0)
    o = jnp.where(row < HEAD_DIM, acc_ref[0] / l_ref[0], acc_ref[1] / l_ref[1])
    o_ref[...] = o.T.astype(o_ref.dtype)


def _fox_attention(fqk, vtf, crep, crow, B, S):
    T = B * S
    tq = min(ATT_T, S)
    nq = S // tq
    npair = FOX_HEADS // 2
    crow4 = crow.reshape(B, 8, nq, tq)
    rep = lambda x: pl.BlockSpec((None, None, S, LANES), lambda b, p, i: (b, 2 * p + x, 0, 0))
    return pl.pallas_call(
        functools.partial(_fox_attn_kernel, tq=tq),
        out_shape=jax.ShapeDtypeStruct((T, FOX_WIDTH), BF16),
        grid=(B, npair, nq),
        in_specs=[
            pl.BlockSpec((tq, LANES), lambda b, p, i: (b * nq + i, p)),
            pl.BlockSpec((S, LANES), lambda b, p, i: (b, npair + p)),
            pl.BlockSpec((nq, LANES, tq), lambda b, p, i: (b, p, 0)),
            rep(0), rep(1),
            pl.BlockSpec((None, 8, nq, tq), lambda b, p, i: (b, 0, 0, 0)),
        ],
        out_specs=pl.BlockSpec((tq, LANES), lambda b, p, i: (b * nq + i, p)),
        scratch_shapes=[
            pltpu.VMEM((2, 1, tq), F32),
            pltpu.VMEM((2, 1, tq), F32),
            pltpu.VMEM((2, LANES, tq), F32),
        ],
        compiler_params=_params(("parallel", "parallel", "arbitrary")),
        name="fox_attn",
    )(fqk, fqk, vtf, crep, crep, crow4)


def _seg_sum(x, seg, npair):
    return jnp.concatenate(
        [_mm(x[:, LANES * p:LANES * (p + 1)], seg, HIGHEST) for p in range(npair)], axis=1)


def _rwkv_kernel(x_ref, mu_ref, w0_ref, w2_ref, a0_ref, a2_ref, g2_ref, kk_ref, ka_ref, rk_ref, lng_ref, lnb_ref,
                 o_ref, carry_ref, st_ref, at_ref, rt_ref, bt_ref, kt_ref, v_ref, wc_ref, y_ref, g_ref, bon_ref,
                 *, tt, prec):
    i = pl.program_id(1)
    W = RWKV_WIDTH
    C = RW_CHUNK
    CPI = RW_CHUNKS_PER_ITER
    npair = RWKV_HEADS // 2

    @pl.when(i == 0)
    def _():
        carry_ref[...] = jnp.zeros(carry_ref.shape, F32)
        st_ref[...] = jnp.zeros(st_ref.shape, F32)

    x = x_ref[...]
    rows = lax.broadcasted_iota(jnp.int32, (tt, 1), 0)
    prev = jnp.where(rows == 0, carry_ref[...], pltpu.roll(x, 1, axis=0))
    carry_ref[...] = x[tt - 1:tt, :]
    xs = x + (prev - x) * mu_ref[...]
    r = xs[:, 0:W]
    k = xs[:, W:2 * W]
    v = xs[:, 2 * W:3 * W]
    xw = xs[:, 3 * W:3 * W + LANES]
    xa = xs[:, 3 * W + LANES:3 * W + 2 * LANES]
    xg = xs[:, 3 * W + 2 * LANES:]
    wl = w0_ref[...] + _mm(jnp.tanh(xw), w2_ref[...], HIGHEST)
    w = -(jnp.maximum(-wl, 0.0) + jnp.log(1.0 + jnp.exp(-jnp.abs(wl)))) - 0.5
    logdec = -jnp.exp(w)
    a = jax.nn.sigmoid(a0_ref[...] + _mm(xa, a2_ref[...], HIGHEST))
    g_ref[...] = _mm(jax.nn.sigmoid(xg), g2_ref[...], HIGHEST)

    r_i = lax.broadcasted_iota(jnp.int32, (LANES, LANES), 0)
    c_i = lax.broadcasted_iota(jnp.int32, (LANES, LANES), 1)
    seg = ((r_i // HEAD_DIM) == (c_i // HEAD_DIM)).astype(F32)
    kkv = k * kk_ref[...]
    kkn = kkv / jnp.maximum(jnp.sqrt(_seg_sum(kkv * kkv, seg, npair)), 1e-12)
    k2 = k * (1.0 + (a - 1.0) * ka_ref[...])
    bon_ref[...] = _seg_sum(r * k2 * rk_ref[...], seg, npair) * v

    rt_i = lax.broadcasted_iota(jnp.int32, (tt, tt), 0)
    ct_i = lax.broadcasted_iota(jnp.int32, (tt, tt), 1)
    tri = (((rt_i // C) == (ct_i // C)) & (rt_i >= ct_i)).astype(F32)
    cum = _mm(tri, logdec, HIGHEST)
    winv = jnp.exp(-cum)
    wcum = jnp.exp(cum)
    at_ref[...] = -kkn * jnp.exp(cum - logdec)
    bt_ref[...] = kkn * a * winv
    kt_ref[...] = k2 * winv
    rt_ref[...] = r * wcum
    v_ref[...] = v
    wc_ref[...] = wcum

    lane = lax.broadcasted_iota(jnp.int32, (1, LANES), 1)
    lo = lane < HEAD_DIM
    tpos = r_i % C
    ipos = c_i % C
    strict = tpos > ipos
    incl = tpos >= ipos
    eye = r_i == c_i

    def stack2(m):
        return jnp.concatenate([jnp.where(lo, m, 0.0), jnp.where(lo, 0.0, m)], axis=0)

    def cast(m):
        return m if prec is not None else m.astype(BF16)

    def mm(p, q):
        return _mm(cast(p), cast(q), prec)

    def nt(p, q):
        return _nt(cast(p), cast(q), prec)

    def chunks(cc, carry):
        units = [(ci, p) for ci in range(CPI) for p in range(npair)]
        r0 = [pl.multiple_of((cc * CPI + ci) * C, C) for ci in range(CPI)]
        ld = lambda ref, ci, p: stack2(ref[pl.ds(r0[ci], C), LANES * p:LANES * (p + 1)])
        a2 = {u: ld(at_ref, *u) for u in units}
        r2 = {u: ld(rt_ref, *u) for u in units}
        b2 = {u: ld(bt_ref, *u) for u in units}
        k2s = {u: ld(kt_ref, *u) for u in units}
        v2 = {u: ld(v_ref, *u) for u in units}
        wl = {(ci, p): wc_ref[pl.ds(pl.multiple_of(r0[ci] + C - 8, 8), 8), LANES * p:LANES * (p + 1)][7:8, :]
              for (ci, p) in units}
        ar = {u: jnp.concatenate([a2[u], r2[u]], axis=0) for u in units}
        mb = {u: nt(ar[u], b2[u]) for u in units}
        mk = {u: nt(ar[u], k2s[u]) for u in units}
        lab = {u: jnp.where(strict, mb[u][0:LANES], 0.0) for u in units}
        mrb = {u: jnp.where(incl, mb[u][LANES:], 0.0) for u in units}
        lak = {u: jnp.where(strict, mk[u][0:LANES], 0.0) for u in units}
        mrk = {u: jnp.where(incl, mk[u][LANES:], 0.0) for u in units}
        xx = {u: jnp.concatenate([a2[u], mm(lak[u], v2[u])], axis=1) for u in units}
        lp = lab
        for it in range(6):
            xx = {u: xx[u] + mm(lp[u], xx[u]) for u in units}
            if it < 5:
                lp = {u: mm(lp[u], lp[u]) for u in units}
        mq = {u: mm(mrb[u], xx[u]) for u in units}
        mv = {u: mm(mrk[u], v2[u]) for u in units}
        bx = {u: mm((b2[u] * wl[u]).T, xx[u]) for u in units}
        kv = {u: mm((k2s[u] * wl[u]).T, v2[u]) for u in units}
        st = [st_ref[p] for p in range(npair)]
        for (ci, p) in units:
            u = (ci, p)
            q1 = r2[u] + mq[u][:, 0:LANES]
            q2 = mq[u][:, LANES:] + mv[u]
            gmat = jnp.where(eye, jnp.broadcast_to(wl[u], (LANES, LANES)), 0.0) + bx[u][:, 0:LANES]
            hmat = bx[u][:, LANES:] + kv[u]
            gs = mm(jnp.concatenate([gmat, q1], axis=0), st[p])
            st[p] = gs[0:LANES] + hmat
            yy = gs[LANES:] + q2
            y_ref[pl.ds(r0[ci], C), LANES * p:LANES * (p + 1)] = yy[0:C] + yy[C:]
        for p in range(npair):
            st_ref[p] = st[p]
        return carry

    lax.fori_loop(0, tt // (C * CPI), chunks, 0)

    y = y_ref[...]
    inv = 1.0 / HEAD_DIM
    mean = _seg_sum(y, seg, npair) * inv
    yc = y - mean
    var = _seg_sum(yc * yc, seg, npair) * inv
    yn = yc * lax.rsqrt(var + RWKV_LN_EPS) * lng_ref[...] + lnb_ref[...]
    o_ref[...] = ((yn + bon_ref[...]) * g_ref[...]).astype(o_ref.dtype)


def _rwkv(rcols, mu, w0, w2, a0, a2, g2, k_k, k_a, r_k, ln_g, ln_b, B, S, prec=HIGHEST):
    T = B * S
    W = RWKV_WIDTH
    tt = 512 if S % 512 == 0 else S
    nt_ = S // tt
    npair = RWKV_HEADS // 2
    pad = LANES - DECAY_LORA
    mu_p = jnp.concatenate([mu[:3 * W], mu[3 * W:3 * W + DECAY_LORA], jnp.zeros((pad,), F32),
                            mu[3 * W + DECAY_LORA:3 * W + DECAY_LORA + AAA_LORA], jnp.zeros((pad,), F32),
                            mu[3 * W + DECAY_LORA + AAA_LORA:]]).reshape(1, RW_PAD_COLS)
    w2p = jnp.concatenate([w2, jnp.zeros((pad, W), F32)], axis=0)
    a2p = jnp.concatenate([a2, jnp.zeros((pad, W), F32)], axis=0)
    vec = lambda t: t.reshape(1, W).astype(F32)
    full = lambda shape: pl.BlockSpec(shape, lambda b, i: (0,) * len(shape))
    sc = lambda: pltpu.VMEM((tt, W), F32)
    return pl.pallas_call(
        functools.partial(_rwkv_kernel, tt=tt, prec=prec),
        out_shape=jax.ShapeDtypeStruct((T, W), BF16),
        grid=(B, nt_),
        in_specs=[
            pl.BlockSpec((tt, RW_PAD_COLS), lambda b, i: (b * nt_ + i, 0)),
            full((1, RW_PAD_COLS)), full((1, W)), full((LANES, W)), full((1, W)), full((LANES, W)),
            full((GATE_LORA, W)), full((1, W)), full((1, W)), full((1, W)), full((1, W)), full((1, W)),
        ],
        out_specs=pl.BlockSpec((tt, W), lambda b, i: (b * nt_ + i, 0)),
        scratch_shapes=[
            pltpu.VMEM((1, RW_PAD_COLS), F32),
            pltpu.VMEM((npair, LANES, LANES), F32),
            sc(), sc(), sc(), sc(), sc(), sc(), sc(), sc(), sc(),
        ],
        compiler_params=_params(("parallel", "arbitrary")),
        name="rwkv7",
    )(rcols, mu_p, vec(w0), w2p, vec(a0), a2p, g2, vec(k_k), vec(k_a), vec(r_k), vec(ln_g), vec(ln_b))


def _outproj_kernel(ya_ref, yb_ref, yc_ref, x_ref, w_ref, gm_ref, g_ref, sh_ref, sc_ref, xo_ref, ho_ref):
    o1 = DIFF_WIDTH
    o2 = o1 + RWKV_WIDTH
    mix = (_mm(ya_ref[...], w_ref[0:o1, :]) + _mm(yb_ref[...], w_ref[o1:o2, :])
           + _mm(yc_ref[...], w_ref[o2:, :]))
    xn = x_ref[...] + gm_ref[...] * mix
    xo_ref[...] = xn
    ms = jnp.mean(xn * xn, axis=-1, keepdims=True)
    y = xn * lax.rsqrt(ms + RMS_EPS) * g_ref[...]
    ho_ref[...] = y * (1.0 + sc_ref[...]) + sh_ref[...]


def _out_proj(ya, yb, yc, x2, w_out, g, mod4, S):
    T, D = x2.shape
    tm = 512 if S % 512 == 0 else S
    nb = S // tm
    row = lambda i: (i, 0)
    modspec = lambda which: pl.BlockSpec((None, None, 1, D), lambda i: (i // nb, which, 0, 0))
    return pl.pallas_call(
        _outproj_kernel,
        out_shape=(jax.ShapeDtypeStruct((T, D), F32), jax.ShapeDtypeStruct((T, D), F32)),
        grid=(T // tm,),
        in_specs=[
            pl.BlockSpec((tm, DIFF_WIDTH), row), pl.BlockSpec((tm, RWKV_WIDTH), row), pl.BlockSpec((tm, FOX_WIDTH), row),
            pl.BlockSpec((tm, D), row),
            pl.BlockSpec((D, D), lambda i: (0, 0)),
            modspec(2),
            pl.BlockSpec((1, D), lambda i: (0, 0)),
            modspec(3), modspec(4),
        ],
        out_specs=(pl.BlockSpec((tm, D), row), pl.BlockSpec((tm, D), row)),
        compiler_params=_params(("parallel",)),
        name="out_proj",
    )(ya, yb, yc, x2, w_out.astype(BF16), mod4, g.reshape(1, D), mod4, mod4)


def _top16(s, iota_f, n):
    vals, poss = [], []
    for _ in range(PEER_TOPK):
        m = jnp.max(s, axis=0, keepdims=True)
        pos = jnp.min(jnp.where(s == m, iota_f, float(n)), axis=0, keepdims=True)
        vals.append(m)
        poss.append(pos)
        s = jnp.where(iota_f == pos, -jnp.inf, s)
    return jnp.concatenate(vals, axis=0), jnp.concatenate(poss, axis=0)


ROUTE_UNROLL = 4
PEER_NCAND = 56


def _peer_cand_tables():
    K = PEER_TOPK
    pairs = [(a, b) for a in range(K) for b in range(K) if (a + 1) * (b + 1) <= K]
    n = PEER_NCAND
    p0 = [[0.0] * K for _ in range(n)]
    p1 = [[0.0] * K for _ in range(n)]
    pad = [0.0] * n
    pos = [float(K * K + r) for r in range(n)]
    for r, (a, b) in enumerate(pairs):
        p0[r][a] = 1.0
        p1[r][b] = 1.0
        pos[r] = float(a * K + b)
    for r in range(len(pairs), n):
        pad[r] = -float("inf")
    col = lambda v: jnp.broadcast_to(jnp.asarray(v, F32)[:, None], (n, LANES))
    return jnp.asarray(p0, F32), jnp.asarray(p1, F32), col(pad), col(pos)


def _peer_route_kernel(h_ref, wq_ref, sk_ref, p0_ref, p1_ref, cpad_ref, cpos_ref, idx_ref, gate_ref, q_scr, e_scr,
                       g_scr):
    K = PEER_TOPK
    hb = h_ref[...].astype(BF16)
    q = _mm(hb, wq_ref[...])
    for hc in range(2 * PEER_HEADS):
        q_scr[hc] = q[:, LANES * hc:LANES * (hc + 1)].astype(BF16)
    iota_n = lax.broadcasted_iota(jnp.int32, (PEER_NKEYS, LANES), 0).astype(F32)
    cpos = cpos_ref[...]

    def one_head(h):
        sv0, si0 = _top16(_nt(sk_ref[2 * h], q_scr[2 * h]), iota_n, PEER_NKEYS)
        sv1, si1 = _top16(_nt(sk_ref[2 * h + 1], q_scr[2 * h + 1]), iota_n, PEER_NKEYS)
        cand = _mm(p0_ref[...], sv0, HIGHEST) + _mm(p1_ref[...], sv1, HIGHEST) + cpad_ref[...]
        cidx = _mm(p0_ref[...], si0) * float(PEER_NKEYS) + _mm(p1_ref[...], si1)
        fv, es = [], []
        for _ in range(K):
            m = jnp.max(cand, axis=0, keepdims=True)
            pos = jnp.min(jnp.where(cand == m, cpos, float(2 * K * K)), axis=0, keepdims=True)
            hit = cpos == pos
            fv.append(m)
            es.append(jnp.max(jnp.where(hit, cidx, -1.0), axis=0, keepdims=True))
            cand = jnp.where(hit, -jnp.inf, cand)
        fv = jnp.concatenate(fv, axis=0)
        ex = jnp.exp(fv - fv[0:1, :])
        g_scr[h] = ex / jnp.sum(ex, axis=0, keepdims=True)
        e_scr[h] = jnp.concatenate(es, axis=0)

    def heads(hh, carry):
        for j in range(ROUTE_UNROLL):
            one_head(hh * ROUTE_UNROLL + j)
        return carry

    lax.fori_loop(0, PEER_HEADS // ROUTE_UNROLL, heads, 0)
    e = e_scr[...].reshape(PEER_HEADS * K, LANES)
    idx_ref[...] = e.T.astype(jnp.int32)
    gate_ref[...] = g_scr[...].reshape(PEER_HEADS * K, LANES)


def _peer_route(h2, wq, subkeys):
    T, D = h2.shape
    tm = LANES
    nq = 2 * PEER_HEADS
    sk = subkeys.reshape(nq, PEER_NKEYS, PEER_HALF).astype(BF16)
    p0, p1, cpad, cpos = _peer_cand_tables()
    const = lambda shape: pl.BlockSpec(shape, lambda i: (0,) * len(shape))
    return pl.pallas_call(
        _peer_route_kernel,
        out_shape=(jax.ShapeDtypeStruct((T, PEER_HEADS * PEER_TOPK), jnp.int32),
                   jax.ShapeDtypeStruct((T // tm, PEER_HEADS * PEER_TOPK, tm), F32)),
        grid=(T // tm,),
        in_specs=[
            pl.BlockSpec((tm, D), lambda i: (i, 0)),
            const((D, nq * PEER_HALF)),
            const((nq, PEER_NKEYS, PEER_HALF)),
            const((PEER_NCAND, PEER_TOPK)), const((PEER_NCAND, PEER_TOPK)),
            const((PEER_NCAND, LANES)), const((PEER_NCAND, LANES)),
        ],
        out_specs=(pl.BlockSpec((tm, PEER_HEADS * PEER_TOPK), lambda i: (i, 0)),
                   pl.BlockSpec((None, PEER_HEADS * PEER_TOPK, tm), lambda i: (i, 0, 0))),
        scratch_shapes=[
            pltpu.VMEM((nq, tm, PEER_HALF), BF16),
            pltpu.VMEM((PEER_HEADS, PEER_TOPK, tm), F32),
            pltpu.VMEM((PEER_HEADS, PEER_TOPK, tm), F32),
        ],
        compiler_params=_params(("parallel",)),
        name="peer_route",
    )(h2, wq.astype(BF16), sk, p0, p1, cpad, cpos)


PEER_G = 16
PEER_SLOTS = PEER_HEADS * PEER_TOPK


def _peer_eval_kernel(idx_ref, idxn_ref, gate_ref, h_ref, x_ref, gf_ref, fg_ref, uv_ref, o_ref, buf, sem, *, final):
    G = PEER_G
    R = G * PEER_SLOTS
    D = D_MODEL
    tiles = PEER_SLOTS // SUBLANES
    i = pl.program_id(0)
    n = pl.num_programs(0)

    def start(ids, off, s, t, u):
        pltpu.make_async_copy(uv_ref.at[ids[off + t * SUBLANES + u]], buf.at[s, t, pl.ds(u, 1), :],
                              sem.at[s]).start(priority=u % 2)

    def wait(s):
        pltpu.make_async_copy(buf.at[s], buf.at[s], sem.at[s]).wait()

    @pl.when(i == 0)
    def _():
        def body(t, carry):
            for u in range(SUBLANES):
                start(idx_ref, 0, 0, t, u)
            return carry
        lax.fori_loop(0, R // SUBLANES, body, 0)

    lane = lax.broadcasted_iota(jnp.int32, (1, LANES), 1)
    tbase = (i % (LANES // (2 * G))) * (2 * G)
    gate = gate_ref[...]
    outs = []
    for s in range(2):
        wait(s)
        nxt_ids, nxt_off = (idx_ref, R) if s == 0 else (idxn_ref, 0)
        for g in range(G):
            for t in range(tiles * g, tiles * (g + 1)):
                for u in range(SUBLANES):
                    start(nxt_ids, nxt_off, 1 - s, t, u)
            u_rows = buf[s, tiles * g:tiles * (g + 1), :, 0:D].reshape(PEER_SLOTS, D)
            prod = u_rows * h_ref[G * s + g:G * s + g + 1, :]
            part = prod[:, 0:LANES]
            for c in range(1, D // LANES):
                part = part + prod[:, LANES * c:LANES * (c + 1)]
            act = jnp.sum(part, axis=1, keepdims=True)
            gcol = jnp.sum(jnp.where(lane == tbase + G * s + g, gate, 0.0), axis=1, keepdims=True)
            coef = gcol * (0.5 * act * (1.0 + lax.erf(act * (2.0 ** -0.5))))
            v_rows = buf[s, tiles * g:tiles * (g + 1), :, D:2 * D].reshape(PEER_SLOTS, D)
            outs.append(jnp.sum(v_rows * coef, axis=0, keepdims=True))
    xn = x_ref[...] + gf_ref[...] * jnp.concatenate(outs, axis=0)
    if final:
        ms = jnp.mean(xn * xn, axis=-1, keepdims=True)
        xn = xn * lax.rsqrt(ms + RMS_EPS) * fg_ref[...]
    o_ref[...] = xn

    @pl.when(i == n - 1)
    def _():
        wait(0)


def _peer_eval(eidx, gate_t, h2, x2, mod4, final_g, uv, S, final):
    T, D = x2.shape
    G = PEER_G
    R = G * PEER_SLOTS
    n = T // (2 * G)
    return pl.pallas_call(
        functools.partial(_peer_eval_kernel, final=final),
        out_shape=jax.ShapeDtypeStruct((T, D), F32),
        grid=(n,),
        in_specs=[
            pl.BlockSpec((2 * R,), lambda i: (i,), memory_space=pltpu.SMEM),
            pl.BlockSpec((R,), lambda i: (jnp.minimum(2 * i + 2, 2 * n - 2),), memory_space=pltpu.SMEM),
            pl.BlockSpec((None, PEER_SLOTS, LANES), lambda i: (i // (LANES // (2 * G)), 0, 0)),
            pl.BlockSpec((2 * G, D), lambda i: (i, 0)),
            pl.BlockSpec((2 * G, D), lambda i: (i, 0)),
            pl.BlockSpec((None, None, 1, D), lambda i: (i // (S // (2 * G)), 5, 0, 0)),
            pl.BlockSpec((1, D), lambda i: (0, 0)),
            pl.BlockSpec(memory_space=pl.ANY),
        ],
        out_specs=pl.BlockSpec((2 * G, D), lambda i: (i, 0)),
        scratch_shapes=[pltpu.VMEM((2, R // SUBLANES, SUBLANES, 2 * D), F32), pltpu.SemaphoreType.DMA((2,))],
        compiler_params=_params(("arbitrary",)),
        name="peer_eval",
    )(eidx.reshape(-1), eidx.reshape(-1), gate_t, h2, x2, mod4, final_g.reshape(1, D), uv)


def kernel(x, c, norm_mix_g, norm_ffn_g, final_norm_g, ada_w, ada_b, w_in, w_out, dif_lam, dif_subln_g, rw_mu, rw_w0,
           rw_w2, rw_a0, rw_a2, rw_g2, rw_kk, rw_ka, rw_rk, rw_ln_g, rw_ln_b, fox_bf, peer_wq, peer_subkeys, peer_u,
           peer_v):
    B, S, D = x.shape
    T = B * S
    depth = ada_w.shape[0]
    x2 = x.reshape(T, D)
    mod = _ada_mod(c, ada_w, ada_b)
    for l in range(depth):
        mod4 = mod[l].reshape(B, 6, 1, D)
        w_pad, w_vt = _pad_w_in(w_in[l])
        dqk, rcols, fqk, fl, vtd, vtf = _in_proj(x2, norm_mix_g[l], mod4, w_pad, w_vt, S)
        ya = _diff_attention(dqk, vtd, dif_lam[l], dif_subln_g[l], l, B, S)
        yb = _rwkv(rcols, rw_mu[l], rw_w0[l], rw_w2[l], rw_a0[l], rw_a2[l], rw_g2[l], rw_kk[l], rw_ka[l],
                   rw_rk[l].reshape(-1), rw_ln_g[l], rw_ln_b[l], B, S, prec=None)
        crep, crow = _fox_cum(fl, fox_bf[l], B, S)
        yc = _fox_attention(fqk, vtf, crep, crow, B, S)
        x2, h2 = _out_proj(ya, yb, yc, x2, w_out[l], norm_ffn_g[l], mod4, S)
        eidx, gate_t = _peer_route(h2, peer_wq[l], peer_subkeys[l])
        uv = jnp.concatenate([peer_u[l], peer_v[l]], axis=1).reshape(-1, 1, 2 * D)
        x2 = _peer_eval(eidx, gate_t, h2, x2, mod4, final_norm_g, uv, S, final=(l == depth - 1))
    return x2.reshape(B, S, D)
```

```python
import functools
import math

import jax
import jax.numpy as jnp
from jax import lax
from jax.experimental import pallas as pl
from jax.experimental.pallas import tpu as pltpu

F32 = jnp.float32
BF16 = jnp.bfloat16
HIGHEST = lax.Precision.HIGHEST

D_MODEL = 1024
HEAD_DIM = 64
DIFF_HEADS = 6
DIFF_QK_DIM = HEAD_DIM // 2
RWKV_HEADS = 6
FOX_HEADS = 4
DIFF_WIDTH = DIFF_HEADS * HEAD_DIM
RWKV_WIDTH = RWKV_HEADS * HEAD_DIM
FOX_WIDTH = FOX_HEADS * HEAD_DIM
DECAY_LORA = 64
AAA_LORA = 64
GATE_LORA = 128
DIFF_COLS = 3 * DIFF_WIDTH
RWKV_COLS = 3 * RWKV_WIDTH + DECAY_LORA + AAA_LORA + GATE_LORA
PEER_HEADS = 8
PEER_NKEYS = 128
PEER_TOPK = 16
PEER_QDIM = 256
PEER_HALF = PEER_QDIM // 2
RMS_EPS = 1e-6
RWKV_LN_EPS = 64e-5

LANES = 128
SUBLANES = 8
RW_PAD_COLS = 3 * RWKV_WIDTH + 3 * LANES
VMEM_LIMIT = 56 * 1024 * 1024

RW_CHUNK = 64
RW_CHUNKS_PER_ITER = 2


def _params(sem, vmem=VMEM_LIMIT):
    return pltpu.CompilerParams(dimension_semantics=sem, vmem_limit_bytes=vmem)


def _nt(a, b, precision=None):
    return lax.dot_general(a, b, (((1,), (1,)), ((), ())), preferred_element_type=F32, precision=precision)


def _mm(a, b, precision=None):
    return jnp.dot(a, b, preferred_element_type=F32, precision=precision)


def _ada_kernel(c_ref, w_ref, b_ref, o_ref):
    c = c_ref[...]
    ca = c * jax.nn.sigmoid(c)
    o_ref[...] = _mm(ca, w_ref[...], HIGHEST) + b_ref[...]


def _ada_mod(c, ada_w, ada_b):
    L, D, N = ada_w.shape
    B = c.shape[0]
    tn = 1536
    return pl.pallas_call(
        _ada_kernel,
        out_shape=jax.ShapeDtypeStruct((L, B, N), F32),
        grid=(L, N // tn),
        in_specs=[
            pl.BlockSpec((B, D), lambda l, j: (0, 0)),
            pl.BlockSpec((None, D, tn), lambda l, j: (l, 0, j)),
            pl.BlockSpec((None, 1, tn), lambda l, j: (l, 0, j)),
        ],
        out_specs=pl.BlockSpec((None, B, tn), lambda l, j: (l, 0, j)),
        compiler_params=_params(("parallel", "parallel")),
        name="ada_mod",
    )(c, ada_w, ada_b.reshape(L, 1, N))


ATT_T = 512
QK_DIFF = 2 * DIFF_WIDTH
QK_FOX = 2 * FOX_WIDTH
VT_ROWS = DIFF_WIDTH + FOX_WIDTH
IN_PAD_COLS = QK_DIFF + RW_PAD_COLS + QK_FOX + LANES


def _inproj_kernel(x_ref, g_ref, sh_ref, sc_ref, w_ref, wvt_ref, d_ref, r_ref, f_ref, fl_ref, vtd_ref, vtf_ref, *, ta):
    x = x_ref[...]
    ms = jnp.mean(x * x, axis=-1, keepdims=True)
    y = x * lax.rsqrt(ms + RMS_EPS) * g_ref[...]
    h = (y * (1.0 + sc_ref[...]) + sh_ref[...]).astype(BF16)
    o1 = QK_DIFF
    o2 = o1 + RW_PAD_COLS
    o3 = o2 + QK_FOX
    d_ref[...] = _mm(h, w_ref[:, 0:o1]).astype(BF16)
    r_ref[...] = _mm(h, w_ref[:, o1:o2])
    f_ref[...] = _mm(h, w_ref[:, o2:o3]).astype(BF16)
    fl_ref[...] = _mm(h, w_ref[:, o3:o3 + LANES])
    vt = _nt(wvt_ref[...], h).astype(BF16)
    for s in range(x.shape[0] // ta):
        vtd_ref[s] = vt[0:DIFF_WIDTH, ta * s:ta * (s + 1)]
        vtf_ref[s] = vt[DIFF_WIDTH:, ta * s:ta * (s + 1)]


def _in_proj(x2, g, mod4, w_pad, w_vt, S):
    T, D = x2.shape
    tm = 512 if S % 512 == 0 else S
    ta = min(ATT_T, S)
    nb = S // tm
    row = lambda i: (i, 0)
    return pl.pallas_call(
        functools.partial(_inproj_kernel, ta=ta),
        out_shape=(
            jax.ShapeDtypeStruct((T, QK_DIFF), BF16),
            jax.ShapeDtypeStruct((T, RW_PAD_COLS), F32),
            jax.ShapeDtypeStruct((T, QK_FOX), BF16),
            jax.ShapeDtypeStruct((T, LANES), F32),
            jax.ShapeDtypeStruct((T // ta, DIFF_WIDTH, ta), BF16),
            jax.ShapeDtypeStruct((T // ta, FOX_WIDTH, ta), BF16),
        ),
        grid=(T // tm,),
        in_specs=[
            pl.BlockSpec((tm, D), row),
            pl.BlockSpec((1, D), lambda i: (0, 0)),
            pl.BlockSpec((None, None, 1, D), lambda i: (i // nb, 0, 0, 0)),
            pl.BlockSpec((None, None, 1, D), lambda i: (i // nb, 1, 0, 0)),
            pl.BlockSpec((D, IN_PAD_COLS), lambda i: (0, 0)),
            pl.BlockSpec((VT_ROWS, D), lambda i: (0, 0)),
        ],
        out_specs=(
            pl.BlockSpec((tm, QK_DIFF), row),
            pl.BlockSpec((tm, RW_PAD_COLS), row),
            pl.BlockSpec((tm, QK_FOX), row),
            pl.BlockSpec((tm, LANES), row),
            pl.BlockSpec((tm // ta, DIFF_WIDTH, ta), lambda i: (i, 0, 0)),
            pl.BlockSpec((tm // ta, FOX_WIDTH, ta), lambda i: (i, 0, 0)),
        ),
        compiler_params=_params(("parallel",)),
        name="in_proj",
    )(x2, g.reshape(1, D), mod4, mod4, w_pad, w_vt)


def _pad_w_in(w_in):
    D = w_in.shape[0]
    W = RWKV_WIDTH
    o = DIFF_COLS
    z64 = jnp.zeros((D, LANES - DECAY_LORA), w_in.dtype)
    rw = w_in[:, o:o + RWKV_COLS]
    fx = w_in[:, o + RWKV_COLS:]
    zf = jnp.zeros((D, LANES - FOX_HEADS), w_in.dtype)
    w_pad = jnp.concatenate([
        w_in[:, :QK_DIFF],
        rw[:, :3 * W], rw[:, 3 * W:3 * W + DECAY_LORA], z64,
        rw[:, 3 * W + DECAY_LORA:3 * W + DECAY_LORA + AAA_LORA], z64,
        rw[:, 3 * W + DECAY_LORA + AAA_LORA:],
        fx[:, :QK_FOX], fx[:, 3 * FOX_WIDTH:], zf,
    ], axis=1).astype(BF16)
    w_vt = jnp.concatenate([w_in[:, QK_DIFF:o], fx[:, QK_FOX:3 * FOX_WIDTH]], axis=1).T.astype(BF16)
    return w_pad, w_vt


LOG2E = math.log2(math.e)


def _flash_step(s2s, vt, m_ref, l_ref, acc_ref):
    n = len(s2s)
    m_old = [m_ref[x] for x in range(n)]
    m_new = [jnp.maximum(m_old[x], jnp.max(s2s[x], axis=0, keepdims=True)) for x in range(n)]
    alpha = [jnp.exp2(m_old[x] - m_new[x]) for x in range(n)]
    p = [jnp.exp2(s2s[x] - m_new[x]) for x in range(n)]
    pv = [_mm(vt, p[x].astype(BF16)) for x in range(n)]
    for x in range(n):
        l_ref[x] = alpha[x] * l_ref[x] + jnp.sum(p[x], axis=0, keepdims=True)
        acc_ref[x] = alpha[x] * acc_ref[x] + pv[x]
        m_ref[x] = m_new[x]


def _diff_attn_kernel(lam_ref, g_ref, q_ref, k_ref, vt_ref, o_ref, m_ref, l_ref, acc_ref, *, tq, lam_init):
    i = pl.program_id(2)
    c = (DIFF_QK_DIM ** -0.5) * LOG2E
    lane = lax.broadcasted_iota(jnp.int32, (1, LANES), 1)
    q = q_ref[...]
    zero = jnp.zeros_like(q)
    qm = [jnp.where((lane >= DIFF_QK_DIM * x) & (lane < DIFF_QK_DIM * (x + 1)), q, zero) for x in range(4)]
    m_ref[...] = jnp.full(m_ref.shape, -jnp.inf, F32)
    l_ref[...] = jnp.zeros(l_ref.shape, F32)
    acc_ref[...] = jnp.zeros(acc_ref.shape, F32)

    def step(j, diag):
        k = k_ref[pl.ds(pl.multiple_of(j * tq, tq), tq), :]
        vt = vt_ref[j]
        if diag:
            keep = (lax.broadcasted_iota(jnp.int32, (tq, tq), 1) >= lax.broadcasted_iota(jnp.int32, (tq, tq), 0))
        s2s = [_nt(k, qm[x]) * c for x in range(4)]
        if diag:
            s2s = [jnp.where(keep, s2, -jnp.inf) for s2 in s2s]
        _flash_step(s2s, vt, m_ref, l_ref, acc_ref)

    def body(j, carry):
        step(j, False)
        return carry

    lax.fori_loop(0, i, body, 0)
    step(i, True)

    lp = lam_ref[...]
    lam = (jnp.exp(jnp.sum(lp[0:1] * lp[1:2], axis=-1, keepdims=True))
           - jnp.exp(jnp.sum(lp[2:3] * lp[3:4], axis=-1, keepdims=True)) + lam_init)
    outs = [acc_ref[2 * hh] / l_ref[2 * hh] - lam * (acc_ref[2 * hh + 1] / l_ref[2 * hh + 1]) for hh in range(2)]
    row = lax.broadcasted_iota(jnp.int32, (LANES, 1), 0)
    o = jnp.where(row < HEAD_DIM, outs[0], outs[1])
    sq = o * o
    ms = jnp.where(row < HEAD_DIM, jnp.sum(sq[0:HEAD_DIM], axis=0, keepdims=True),
                   jnp.sum(sq[HEAD_DIM:], axis=0, keepdims=True)) * (1.0 / HEAD_DIM)
    y = o * lax.rsqrt(ms + RMS_EPS) * g_ref[...] * (1.0 - lam_init)
    o_ref[...] = y.T.astype(o_ref.dtype)


def _diff_attention(dqk, vtd, lam_params, subln_g, layer_idx, B, S):
    T = B * S
    tq = min(ATT_T, S)
    nq = S // tq
    npair = DIFF_HEADS // 2
    lam_init = 0.8 - 0.6 * math.exp(-0.3 * layer_idx)
    g2 = jnp.concatenate([subln_g, subln_g]).reshape(LANES, 1).astype(F32)
    return pl.pallas_call(
        functools.partial(_diff_attn_kernel, tq=tq, lam_init=lam_init),
        out_shape=jax.ShapeDtypeStruct((T, DIFF_WIDTH), BF16),
        grid=(B, npair, nq),
        in_specs=[
            pl.BlockSpec((4, DIFF_QK_DIM), lambda b, p, i: (0, 0)),
            pl.BlockSpec((LANES, 1), lambda b, p, i: (0, 0)),
            pl.BlockSpec((tq, LANES), lambda b, p, i: (b * nq + i, p)),
            pl.BlockSpec((S, LANES), lambda b, p, i: (b, npair + p)),
            pl.BlockSpec((nq, LANES, tq), lambda b, p, i: (b, p, 0)),
        ],
        out_specs=pl.BlockSpec((tq, LANES), lambda b, p, i: (b * nq + i, p)),
        scratch_shapes=[
            pltpu.VMEM((4, 1, tq), F32),
            pltpu.VMEM((4, 1, tq), F32),
            pltpu.VMEM((4, LANES, tq), F32),
        ],
        compiler_params=_params(("parallel", "parallel", "arbitrary")),
        name="diff_attn",
    )(lam_params, g2, dqk, dqk, vtd)


def _fox_cum_kernel(f_ref, b_ref, rep_ref, row_ref, *, S, tc):
    rr = lax.broadcasted_iota(jnp.int32, (tc, tc), 0)
    cc = lax.broadcasted_iota(jnp.int32, (tc, tc), 1)
    tri = (rr >= cc).astype(F32)
    sel_r = lax.broadcasted_iota(jnp.int32, (LANES, LANES), 0)
    carry = jnp.zeros((1, LANES), F32)
    for c in range(S // tc):
        z = f_ref[c * tc:(c + 1) * tc, :] + b_ref[...]
        logf = -(jnp.maximum(-z, 0.0) + jnp.log(1.0 + jnp.exp(-jnp.abs(z))))
        cum = _mm(tri, logf, HIGHEST) + carry
        carry = cum[tc - 1:tc, :]
        row_ref[:, c * tc:(c + 1) * tc] = cum.T[0:8, :]
        for h in range(FOX_HEADS):
            rep_ref[h, c * tc:(c + 1) * tc, :] = _mm(cum, (sel_r == h).astype(F32), HIGHEST)


def _fox_cum(fl, b_f, B, S):
    tc = 256 if S % 256 == 0 else S
    bpad = jnp.zeros((1, LANES), F32).at[0, :FOX_HEADS].set(b_f.astype(F32))
    return pl.pallas_call(
        functools.partial(_fox_cum_kernel, S=S, tc=tc),
        out_shape=(jax.ShapeDtypeStruct((B, FOX_HEADS, S, LANES), F32), jax.ShapeDtypeStruct((B, 8, S), F32)),
        grid=(B,),
        in_specs=[pl.BlockSpec((S, LANES), lambda b: (b, 0)), pl.BlockSpec((1, LANES), lambda b: (0, 0))],
        out_specs=(pl.BlockSpec((None, FOX_HEADS, S, LANES), lambda b: (b, 0, 0, 0)),
                   pl.BlockSpec((None, 8, S), lambda b: (b, 0, 0))),
        compiler_params=_params(("parallel",)),
        name="fox_cum",
    )(fl, bpad)


def _fox_attn_kernel(q_ref, k_ref, vt_ref, c0_ref, c1_ref, cr_ref, o_ref, m_ref, l_ref, acc_ref, *, tq):
    p_id = pl.program_id(1)
    i = pl.program_id(2)
    c = (HEAD_DIM ** -0.5) * LOG2E
    lane = lax.broadcasted_iota(jnp.int32, (1, LANES), 1)
    q = q_ref[...]
    zero = jnp.zeros_like(q)
    qm = [jnp.where((lane >= HEAD_DIM * x) & (lane < HEAD_DIM * (x + 1)), q, zero) for x in range(2)]
    ck_refs = (c0_ref, c1_ref)
    cq = [cr_ref[2 * p_id + x, pl.ds(i, 1), :] for x in range(2)]
    m_ref[...] = jnp.full(m_ref.shape, -jnp.inf, F32)
    l_ref[...] = jnp.zeros(l_ref.shape, F32)
    acc_ref[...] = jnp.zeros(acc_ref.shape, F32)

    def step(j, diag):
        off = pl.multiple_of(j * tq, tq)
        k = k_ref[pl.ds(off, tq), :]
        vt = vt_ref[j]
        if diag:
            keep = (lax.broadcasted_iota(jnp.int32, (tq, tq), 1) >= lax.broadcasted_iota(jnp.int32, (tq, tq), 0))
        s2s = []
        for x in range(2):
            ck = ck_refs[x][pl.ds(off, tq), :]
            bias = (cq[x] - jnp.concatenate([ck] * (tq // LANES), axis=1)) * LOG2E
            s2s.append(_nt(k, qm[x]) * c + bias)
        if diag:
            s2s = [jnp.where(keep, s2, -jnp.inf) for s2 in s2s]
        _flash_step(s2s, vt, m_ref, l_ref, acc_ref)

    def body(j, carry):
        step(j, False)
        return carry

    lax.fori_loop(0, i, body, 0)
    step(i, True)
    row = lax.broadcasted_iota(jnp.int32, (LANES, 1), 0)
    o = jnp.where(row < HEAD_DIM, acc_ref[0] / l_ref[0], acc_ref[1] / l_ref[1])
    o_ref[...] = o.T.astype(o_ref.dtype)


def _fox_attention(fqk, vtf, crep, crow, B, S):
    T = B * S
    tq = min(ATT_T, S)
    nq = S // tq
    npair = FOX_HEADS // 2
    crow4 = crow.reshape(B, 8, nq, tq)
    rep = lambda x: pl.BlockSpec((None, None, S, LANES), lambda b, p, i: (b, 2 * p + x, 0, 0))
    return pl.pallas_call(
        functools.partial(_fox_attn_kernel, tq=tq),
        out_shape=jax.ShapeDtypeStruct((T, FOX_WIDTH), BF16),
        grid=(B, npair, nq),
        in_specs=[
            pl.BlockSpec((tq, LANES), lambda b, p, i: (b * nq + i, p)),
            pl.BlockSpec((S, LANES), lambda b, p, i: (b, npair + p)),
            pl.BlockSpec((nq, LANES, tq), lambda b, p, i: (b, p, 0)),
            rep(0), rep(1),
            pl.BlockSpec((None, 8, nq, tq), lambda b, p, i: (b, 0, 0, 0)),
        ],
        out_specs=pl.BlockSpec((tq, LANES), lambda b, p, i: (b * nq + i, p)),
        scratch_shapes=[
            pltpu.VMEM((2, 1, tq), F32),
            pltpu.VMEM((2, 1, tq), F32),
            pltpu.VMEM((2, LANES, tq), F32),
        ],
        compiler_params=_params(("parallel", "parallel", "arbitrary")),
        name="fox_attn",
    )(fqk, fqk, vtf, crep, crep, crow4)


def _seg_sum(x, seg, npair):
    return jnp.concatenate(
        [_mm(x[:, LANES * p:LANES * (p + 1)], seg, HIGHEST) for p in range(npair)], axis=1)


def _rwkv_kernel(x_ref, mu_ref, w0_ref, w2_ref, a0_ref, a2_ref, g2_ref, kk_ref, ka_ref, rk_ref, lng_ref, lnb_ref,
                 o_ref, carry_ref, st_ref, at_ref, rt_ref, bt_ref, kt_ref, v_ref, wc_ref, y_ref, g_ref, bon_ref,
                 *, tt, prec):
    i = pl.program_id(1)
    W = RWKV_WIDTH
    C = RW_CHUNK
    CPI = RW_CHUNKS_PER_ITER
    npair = RWKV_HEADS // 2

    @pl.when(i == 0)
    def _():
        carry_ref[...] = jnp.zeros(carry_ref.shape, F32)
        st_ref[...] = jnp.zeros(st_ref.shape, F32)

    x = x_ref[...]
    rows = lax.broadcasted_iota(jnp.int32, (tt, 1), 0)
    prev = jnp.where(rows == 0, carry_ref[...], pltpu.roll(x, 1, axis=0))
    carry_ref[...] = x[tt - 1:tt, :]
    xs = x + (prev - x) * mu_ref[...]
    r = xs[:, 0:W]
    k = xs[:, W:2 * W]
    v = xs[:, 2 * W:3 * W]
    xw = xs[:, 3 * W:3 * W + LANES]
    xa = xs[:, 3 * W + LANES:3 * W + 2 * LANES]
    xg = xs[:, 3 * W + 2 * LANES:]
    wl = w0_ref[...] + _mm(jnp.tanh(xw), w2_ref[...], HIGHEST)
    w = -(jnp.maximum(-wl, 0.0) + jnp.log(1.0 + jnp.exp(-jnp.abs(wl)))) - 0.5
    logdec = -jnp.exp(w)
    a = jax.nn.sigmoid(a0_ref[...] + _mm(xa, a2_ref[...], HIGHEST))
    g_ref[...] = _mm(jax.nn.sigmoid(xg), g2_ref[...], HIGHEST)

    r_i = lax.broadcasted_iota(jnp.int32, (LANES, LANES), 0)
    c_i = lax.broadcasted_iota(jnp.int32, (LANES, LANES), 1)
    seg = ((r_i // HEAD_DIM) == (c_i // HEAD_DIM)).astype(F32)
    kkv = k * kk_ref[...]
    kkn = kkv / jnp.maximum(jnp.sqrt(_seg_sum(kkv * kkv, seg, npair)), 1e-12)
    k2 = k * (1.0 + (a - 1.0) * ka_ref[...])
    bon_ref[...] = _seg_sum(r * k2 * rk_ref[...], seg, npair) * v

    rt_i = lax.broadcasted_iota(jnp.int32, (tt, tt), 0)
    ct_i = lax.broadcasted_iota(jnp.int32, (tt, tt), 1)
    tri = (((rt_i // C) == (ct_i // C)) & (rt_i >= ct_i)).astype(F32)
    cum = _mm(tri, logdec, HIGHEST)
    winv = jnp.exp(-cum)
    wcum = jnp.exp(cum)
    at_ref[...] = -kkn * jnp.exp(cum - logdec)
    bt_ref[...] = kkn * a * winv
    kt_ref[...] = k2 * winv
    rt_ref[...] = r * wcum
    v_ref[...] = v
    wc_ref[...] = wcum

    lane = lax.broadcasted_iota(jnp.int32, (1, LANES), 1)
    lo = lane < HEAD_DIM
    tpos = r_i % C
    ipos = c_i % C
    strict = tpos > ipos
    incl = tpos >= ipos
    eye = r_i == c_i

    def stack2(m):
        return jnp.concatenate([jnp.where(lo, m, 0.0), jnp.where(lo, 0.0, m)], axis=0)

    def cast(m):
        return m if prec is not None else m.astype(BF16)

    def mm(p, q):
        return _mm(cast(p), cast(q), prec)

    def nt(p, q):
        return _nt(cast(p), cast(q), prec)

    def chunks(cc, carry):
        units = [(ci, p) for ci in range(CPI) for p in range(npair)]
        r0 = [pl.multiple_of((cc * CPI + ci) * C, C) for ci in range(CPI)]
        ld = lambda ref, ci, p: stack2(ref[pl.ds(r0[ci], C), LANES * p:LANES * (p + 1)])
        a2 = {u: ld(at_ref, *u) for u in units}
        r2 = {u: ld(rt_ref, *u) for u in units}
        b2 = {u: ld(bt_ref, *u) for u in units}
        k2s = {u: ld(kt_ref, *u) for u in units}
        v2 = {u: ld(v_ref, *u) for u in units}
        wl = {(ci, p): wc_ref[pl.ds(pl.multiple_of(r0[ci] + C - 8, 8), 8), LANES * p:LANES * (p + 1)][7:8, :]
              for (ci, p) in units}
        ar = {u: jnp.concatenate([a2[u], r2[u]], axis=0) for u in units}
        mb = {u: nt(ar[u], b2[u]) for u in units}
        mk = {u: nt(ar[u], k2s[u]) for u in units}
        lab = {u: jnp.where(strict, mb[u][0:LANES], 0.0) for u in units}
        mrb = {u: jnp.where(incl, mb[u][LANES:], 0.0) for u in units}
        lak = {u: jnp.where(strict, mk[u][0:LANES], 0.0) for u in units}
        mrk = {u: jnp.where(incl, mk[u][LANES:], 0.0) for u in units}
        xx = {u: jnp.concatenate([a2[u], mm(lak[u], v2[u])], axis=1) for u in units}
        lp = lab
        for it in range(6):
            xx = {u: xx[u] + mm(lp[u], xx[u]) for u in units}
            if it < 5:
                lp = {u: mm(lp[u], lp[u]) for u in units}
        mq = {u: mm(mrb[u], xx[u]) for u in units}
        mv = {u: mm(mrk[u], v2[u]) for u in units}
        bx = {u: mm((b2[u] * wl[u]).T, xx[u]) for u in units}
        kv = {u: mm((k2s[u] * wl[u]).T, v2[u]) for u in units}
        st = [st_ref[p] for p in range(npair)]
        for (ci, p) in units:
            u = (ci, p)
            q1 = r2[u] + mq[u][:, 0:LANES]
            q2 = mq[u][:, LANES:] + mv[u]
            gmat = jnp.where(eye, jnp.broadcast_to(wl[u], (LANES, LANES)), 0.0) + bx[u][:, 0:LANES]
            hmat = bx[u][:, LANES:] + kv[u]
            gs = mm(jnp.concatenate([gmat, q1], axis=0), st[p])
            st[p] = gs[0:LANES] + hmat
            yy = gs[LANES:] + q2
            y_ref[pl.ds(r0[ci], C), LANES * p:LANES * (p + 1)] = yy[0:C] + yy[C:]
        for p in range(npair):
            st_ref[p] = st[p]
        return carry

    lax.fori_loop(0, tt // (C * CPI), chunks, 0)

    y = y_ref[...]
    inv = 1.0 / HEAD_DIM
    mean = _seg_sum(y, seg, npair) * inv
    yc = y - mean
    var = _seg_sum(yc * yc, seg, npair) * inv
    yn = yc * lax.rsqrt(var + RWKV_LN_EPS) * lng_ref[...] + lnb_ref[...]
    o_ref[...] = ((yn + bon_ref[...]) * g_ref[...]).astype(o_ref.dtype)


def _rwkv(rcols, mu, w0, w2, a0, a2, g2, k_k, k_a, r_k, ln_g, ln_b, B, S, prec=HIGHEST):
    T = B * S
    W = RWKV_WIDTH
    tt = 512 if S % 512 == 0 else S
    nt_ = S // tt
    npair = RWKV_HEADS // 2
    pad = LANES - DECAY_LORA
    mu_p = jnp.concatenate([mu[:3 * W], mu[3 * W:3 * W + DECAY_LORA], jnp.zeros((pad,), F32),
                            mu[3 * W + DECAY_LORA:3 * W + DECAY_LORA + AAA_LORA], jnp.zeros((pad,), F32),
                            mu[3 * W + DECAY_LORA + AAA_LORA:]]).reshape(1, RW_PAD_COLS)
    w2p = jnp.concatenate([w2, jnp.zeros((pad, W), F32)], axis=0)
    a2p = jnp.concatenate([a2, jnp.zeros((pad, W), F32)], axis=0)
    vec = lambda t: t.reshape(1, W).astype(F32)
    full = lambda shape: pl.BlockSpec(shape, lambda b, i: (0,) * len(shape))
    sc = lambda: pltpu.VMEM((tt, W), F32)
    return pl.pallas_call(
        functools.partial(_rwkv_kernel, tt=tt, prec=prec),
        out_shape=jax.ShapeDtypeStruct((T, W), BF16),
        grid=(B, nt_),
        in_specs=[
            pl.BlockSpec((tt, RW_PAD_COLS), lambda b, i: (b * nt_ + i, 0)),
            full((1, RW_PAD_COLS)), full((1, W)), full((LANES, W)), full((1, W)), full((LANES, W)),
            full((GATE_LORA, W)), full((1, W)), full((1, W)), full((1, W)), full((1, W)), full((1, W)),
        ],
        out_specs=pl.BlockSpec((tt, W), lambda b, i: (b * nt_ + i, 0)),
        scratch_shapes=[
            pltpu.VMEM((1, RW_PAD_COLS), F32),
            pltpu.VMEM((npair, LANES, LANES), F32),
            sc(), sc(), sc(), sc(), sc(), sc(), sc(), sc(), sc(),
        ],
        compiler_params=_params(("parallel", "arbitrary")),
        name="rwkv7",
    )(rcols, mu_p, vec(w0), w2p, vec(a0), a2p, g2, vec(k_k), vec(k_a), vec(r_k), vec(ln_g), vec(ln_b))


def _outproj_kernel(ya_ref, yb_ref, yc_ref, x_ref, w_ref, gm_ref, g_ref, sh_ref, sc_ref, xo_ref, ho_ref):
    o1 = DIFF_WIDTH
    o2 = o1 + RWKV_WIDTH
    mix = (_mm(ya_ref[...], w_ref[0:o1, :]) + _mm(yb_ref[...], w_ref[o1:o2, :])
           + _mm(yc_ref[...], w_ref[o2:, :]))
    xn = x_ref[...] + gm_ref[...] * mix
    xo_ref[...] = xn
    ms = jnp.mean(xn * xn, axis=-1, keepdims=True)
    y = xn * lax.rsqrt(ms + RMS_EPS) * g_ref[...]
    ho_ref[...] = y * (1.0 + sc_ref[...]) + sh_ref[...]


def _out_proj(ya, yb, yc, x2, w_out, g, mod4, S):
    T, D = x2.shape
    tm = 512 if S % 512 == 0 else S
    nb = S // tm
    row = lambda i: (i, 0)
    modspec = lambda which: pl.BlockSpec((None, None, 1, D), lambda i: (i // nb, which, 0, 0))
    return pl.pallas_call(
        _outproj_kernel,
        out_shape=(jax.ShapeDtypeStruct((T, D), F32), jax.ShapeDtypeStruct((T, D), F32)),
        grid=(T // tm,),
        in_specs=[
            pl.BlockSpec((tm, DIFF_WIDTH), row), pl.BlockSpec((tm, RWKV_WIDTH), row), pl.BlockSpec((tm, FOX_WIDTH), row),
            pl.BlockSpec((tm, D), row),
            pl.BlockSpec((D, D), lambda i: (0, 0)),
            modspec(2),
            pl.BlockSpec((1, D), lambda i: (0, 0)),
            modspec(3), modspec(4),
        ],
        out_specs=(pl.BlockSpec((tm, D), row), pl.BlockSpec((tm, D), row)),
        compiler_params=_params(("parallel",)),
        name="out_proj",
    )(ya, yb, yc, x2, w_out.astype(BF16), mod4, g.reshape(1, D), mod4, mod4)


def _top16(s, iota_f, n):
    vals, poss = [], []
    for _ in range(PEER_TOPK):
        m = jnp.max(s, axis=0, keepdims=True)
        pos = jnp.min(jnp.where(s == m, iota_f, float(n)), axis=0, keepdims=True)
        vals.append(m)
        poss.append(pos)
        s = jnp.where(iota_f == pos, -jnp.inf, s)
    return jnp.concatenate(vals, axis=0), jnp.concatenate(poss, axis=0)


ROUTE_UNROLL = 4
PEER_NCAND = 56


def _peer_cand_tables():
    K = PEER_TOPK
    pairs = [(a, b) for a in range(K) for b in range(K) if (a + 1) * (b + 1) <= K]
    n = PEER_NCAND
    p0 = [[0.0] * K for _ in range(n)]
    p1 = [[0.0] * K for _ in range(n)]
    pad = [0.0] * n
    pos = [float(K * K + r) for r in range(n)]
    for r, (a, b) in enumerate(pairs):
        p0[r][a] = 1.0
        p1[r][b] = 1.0
        pos[r] = float(a * K + b)
    for r in range(len(pairs), n):
        pad[r] = -float("inf")
    col = lambda v: jnp.broadcast_to(jnp.asarray(v, F32)[:, None], (n, LANES))
    return jnp.asarray(p0, F32), jnp.asarray(p1, F32), col(pad), col(pos)


def _peer_route_kernel(h_ref, wq_ref, sk_ref, p0_ref, p1_ref, cpad_ref, cpos_ref, idx_ref, gate_ref, q_scr, e_scr,
                       g_scr):
    K = PEER_TOPK
    hb = h_ref[...].astype(BF16)
    q = _mm(hb, wq_ref[...])
    for hc in range(2 * PEER_HEADS):
        q_scr[hc] = q[:, LANES * hc:LANES * (hc + 1)].astype(BF16)
    iota_n = lax.broadcasted_iota(jnp.int32, (PEER_NKEYS, LANES), 0).astype(F32)
    cpos = cpos_ref[...]

    def one_head(h):
        sv0, si0 = _top16(_nt(sk_ref[2 * h], q_scr[2 * h]), iota_n, PEER_NKEYS)
        sv1, si1 = _top16(_nt(sk_ref[2 * h + 1], q_scr[2 * h + 1]), iota_n, PEER_NKEYS)
        cand = _mm(p0_ref[...], sv0, HIGHEST) + _mm(p1_ref[...], sv1, HIGHEST) + cpad_ref[...]
        cidx = _mm(p0_ref[...], si0) * float(PEER_NKEYS) + _mm(p1_ref[...], si1)
        fv, es = [], []
        for _ in range(K):
            m = jnp.max(cand, axis=0, keepdims=True)
            pos = jnp.min(jnp.where(cand == m, cpos, float(2 * K * K)), axis=0, keepdims=True)
            hit = cpos == pos
            fv.append(m)
            es.append(jnp.max(jnp.where(hit, cidx, -1.0), axis=0, keepdims=True))
            cand = jnp.where(hit, -jnp.inf, cand)
        fv = jnp.concatenate(fv, axis=0)
        ex = jnp.exp(fv - fv[0:1, :])
        g_scr[h] = ex / jnp.sum(ex, axis=0, keepdims=True)
        e_scr[h] = jnp.concatenate(es, axis=0)

    def heads(hh, carry):
        for j in range(ROUTE_UNROLL):
            one_head(hh * ROUTE_UNROLL + j)
        return carry

    lax.fori_loop(0, PEER_HEADS // ROUTE_UNROLL, heads, 0)
    e = e_scr[...].reshape(PEER_HEADS * K, LANES)
    idx_ref[...] = e.T.astype(jnp.int32)
    gate_ref[...] = g_scr[...].reshape(PEER_HEADS * K, LANES)


def _peer_route(h2, wq, subkeys):
    T, D = h2.shape
    tm = LANES
    nq = 2 * PEER_HEADS
    sk = subkeys.reshape(nq, PEER_NKEYS, PEER_HALF).astype(BF16)
    p0, p1, cpad, cpos = _peer_cand_tables()
    const = lambda shape: pl.BlockSpec(shape, lambda i: (0,) * len(shape))
    return pl.pallas_call(
        _peer_route_kernel,
        out_shape=(jax.ShapeDtypeStruct((T, PEER_HEADS * PEER_TOPK), jnp.int32),
                   jax.ShapeDtypeStruct((T // tm, PEER_HEADS * PEER_TOPK, tm), F32)),
        grid=(T // tm,),
        in_specs=[
            pl.BlockSpec((tm, D), lambda i: (i, 0)),
            const((D, nq * PEER_HALF)),
            const((nq, PEER_NKEYS, PEER_HALF)),
            const((PEER_NCAND, PEER_TOPK)), const((PEER_NCAND, PEER_TOPK)),
            const((PEER_NCAND, LANES)), const((PEER_NCAND, LANES)),
        ],
        out_specs=(pl.BlockSpec((tm, PEER_HEADS * PEER_TOPK), lambda i: (i, 0)),
                   pl.BlockSpec((None, PEER_HEADS * PEER_TOPK, tm), lambda i: (i, 0, 0))),
        scratch_shapes=[
            pltpu.VMEM((nq, tm, PEER_HALF), BF16),
            pltpu.VMEM((PEER_HEADS, PEER_TOPK, tm), F32),
            pltpu.VMEM((PEER_HEADS, PEER_TOPK, tm), F32),
        ],
        compiler_params=_params(("parallel",)),
        name="peer_route",
    )(h2, wq.astype(BF16), sk, p0, p1, cpad, cpos)


PEER_G = 16
PEER_SLOTS = PEER_HEADS * PEER_TOPK


def _peer_eval_kernel(idx_ref, idxn_ref, gate_ref, h_ref, x_ref, gf_ref, fg_ref, uv_ref, o_ref, buf, sem, *, final):
    G = PEER_G
    R = G * PEER_SLOTS
    D = D_MODEL
    tiles = PEER_SLOTS // SUBLANES
    i = pl.program_id(0)
    n = pl.num_programs(0)

    def start(ids, off, s, t, u):
        pltpu.make_async_copy(uv_ref.at[ids[off + t * SUBLANES + u]], buf.at[s, t, pl.ds(u, 1), :],
                              sem.at[s]).start(priority=u % 2)

    def wait(s):
        pltpu.make_async_copy(buf.at[s], buf.at[s], sem.at[s]).wait()

    @pl.when(i == 0)
    def _():
        def body(t, carry):
            for u in range(SUBLANES):
                start(idx_ref, 0, 0, t, u)
            return carry
        lax.fori_loop(0, R // SUBLANES, body, 0)

    lane = lax.broadcasted_iota(jnp.int32, (1, LANES), 1)
    tbase = (i % (LANES // (2 * G))) * (2 * G)
    gate = gate_ref[...]
    outs = []
    for s in range(2):
        wait(s)
        nxt_ids, nxt_off = (idx_ref, R) if s == 0 else (idxn_ref, 0)
        for g in range(G):
            for t in range(tiles * g, tiles * (g + 1)):
                for u in range(SUBLANES):
                    start(nxt_ids, nxt_off, 1 - s, t, u)
            w_rows = buf[s, tiles * g:tiles * (g + 1)].reshape(PEER_SLOTS, D)
            u_rows = lax.bitcast_convert_type(w_rows & jnp.uint32(0xFFFF0000), F32)
            prod = u_rows * h_ref[G * s + g:G * s + g + 1, :]
            part = prod[:, 0:LANES]
            for c in range(1, D // LANES):
                part = part + prod[:, LANES * c:LANES * (c + 1)]
            act = jnp.sum(part, axis=1, keepdims=True)
            gcol = jnp.sum(jnp.where(lane == tbase + G * s + g, gate, 0.0), axis=1, keepdims=True)
            coef = gcol * (0.5 * act * (1.0 + lax.erf(act * (2.0 ** -0.5))))
            v_rows = lax.bitcast_convert_type(w_rows << 16, F32)
            outs.append(jnp.sum(v_rows * coef, axis=0, keepdims=True))
    xn = x_ref[...] + gf_ref[...] * jnp.concatenate(outs, axis=0)
    if final:
        ms = jnp.mean(xn * xn, axis=-1, keepdims=True)
        xn = xn * lax.rsqrt(ms + RMS_EPS) * fg_ref[...]
    o_ref[...] = xn

    @pl.when(i == n - 1)
    def _():
        wait(0)


def _peer_eval(eidx, gate_t, h2, x2, mod4, final_g, uv, S, final):
    T, D = x2.shape
    G = PEER_G
    R = G * PEER_SLOTS
    n = T // (2 * G)
    return pl.pallas_call(
        functools.partial(_peer_eval_kernel, final=final),
        out_shape=jax.ShapeDtypeStruct((T, D), F32),
        grid=(n,),
        in_specs=[
            pl.BlockSpec((2 * R,), lambda i: (i,), memory_space=pltpu.SMEM),
            pl.BlockSpec((R,), lambda i: (jnp.minimum(2 * i + 2, 2 * n - 2),), memory_space=pltpu.SMEM),
            pl.BlockSpec((None, PEER_SLOTS, LANES), lambda i: (i // (LANES // (2 * G)), 0, 0)),
            pl.BlockSpec((2 * G, D), lambda i: (i, 0)),
            pl.BlockSpec((2 * G, D), lambda i: (i, 0)),
            pl.BlockSpec((None, None, 1, D), lambda i: (i // (S // (2 * G)), 5, 0, 0)),
            pl.BlockSpec((1, D), lambda i: (0, 0)),
            pl.BlockSpec(memory_space=pl.ANY),
        ],
        out_specs=pl.BlockSpec((2 * G, D), lambda i: (i, 0)),
        scratch_shapes=[pltpu.VMEM((2, R // SUBLANES, SUBLANES, D), jnp.uint32), pltpu.SemaphoreType.DMA((2,))],
        compiler_params=_params(("arbitrary",)),
        name="peer_eval",
    )(eidx.reshape(-1), eidx.reshape(-1), gate_t, h2, x2, mod4, final_g.reshape(1, D), uv)


def _pack_uv(u, v):
    hi = lax.bitcast_convert_type(u.astype(BF16), jnp.uint16).astype(jnp.uint32) << 16
    lo = lax.bitcast_convert_type(v.astype(BF16), jnp.uint16).astype(jnp.uint32)
    return (hi | lo).reshape(u.shape[0], 1, u.shape[1])


def kernel(x, c, norm_mix_g, norm_ffn_g, final_norm_g, ada_w, ada_b, w_in, w_out, dif_lam, dif_subln_g, rw_mu, rw_w0,
           rw_w2, rw_a0, rw_a2, rw_g2, rw_kk, rw_ka, rw_rk, rw_ln_g, rw_ln_b, fox_bf, peer_wq, peer_subkeys, peer_u,
           peer_v):
    B, S, D = x.shape
    T = B * S
    depth = ada_w.shape[0]
    x2 = x.reshape(T, D)
    mod = _ada_mod(c, ada_w, ada_b)
    for l in range(depth):
        mod4 = mod[l].reshape(B, 6, 1, D)
        w_pad, w_vt = _pad_w_in(w_in[l])
        dqk, rcols, fqk, fl, vtd, vtf = _in_proj(x2, norm_mix_g[l], mod4, w_pad, w_vt, S)
        ya = _diff_attention(dqk, vtd, dif_lam[l], dif_subln_g[l], l, B, S)
        yb = _rwkv(rcols, rw_mu[l], rw_w0[l], rw_w2[l], rw_a0[l], rw_a2[l], rw_g2[l], rw_kk[l], rw_ka[l],
                   rw_rk[l].reshape(-1), rw_ln_g[l], rw_ln_b[l], B, S, prec=None)
        crep, crow = _fox_cum(fl, fox_bf[l], B, S)
        yc = _fox_attention(fqk, vtf, crep, crow, B, S)
        x2, h2 = _out_proj(ya, yb, yc, x2, w_out[l], norm_ffn_g[l], mod4, S)
        eidx, gate_t = _peer_route(h2, peer_wq[l], peer_subkeys[l])
        uv = _pack_uv(peer_u[l], peer_v[l])
        x2 = _peer_eval(eidx, gate_t, h2, x2, mod4, final_norm_g, uv, S, final=(l == depth - 1))
    return x2.reshape(B, S, D)
```

```python
import functools
import math

import jax
import jax.numpy as jnp
from jax import lax
from jax.experimental import pallas as pl
from jax.experimental.pallas import tpu as pltpu
from jax.experimental.pallas import tpu_sc as plsc

F32 = jnp.float32
BF16 = jnp.bfloat16
HIGHEST = lax.Precision.HIGHEST

D_MODEL = 1024
HEAD_DIM = 64
DIFF_HEADS = 6
DIFF_QK_DIM = HEAD_DIM // 2
RWKV_HEADS = 6
FOX_HEADS = 4
DIFF_WIDTH = DIFF_HEADS * HEAD_DIM
RWKV_WIDTH = RWKV_HEADS * HEAD_DIM
FOX_WIDTH = FOX_HEADS * HEAD_DIM
DECAY_LORA = 64
AAA_LORA = 64
GATE_LORA = 128
DIFF_COLS = 3 * DIFF_WIDTH
RWKV_COLS = 3 * RWKV_WIDTH + DECAY_LORA + AAA_LORA + GATE_LORA
PEER_HEADS = 8
PEER_NKEYS = 128
PEER_TOPK = 16
PEER_QDIM = 256
PEER_HALF = PEER_QDIM // 2
RMS_EPS = 1e-6
RWKV_LN_EPS = 64e-5

LANES = 128
SUBLANES = 8
RW_PAD_COLS = 3 * RWKV_WIDTH + 3 * LANES
VMEM_LIMIT = 56 * 1024 * 1024

RW_CHUNK = 64
RW_CHUNKS_PER_ITER = 2


def _params(sem, vmem=VMEM_LIMIT):
    return pltpu.CompilerParams(dimension_semantics=sem, vmem_limit_bytes=vmem)


def _nt(a, b, precision=None):
    return lax.dot_general(a, b, (((1,), (1,)), ((), ())), preferred_element_type=F32, precision=precision)


def _mm(a, b, precision=None):
    return jnp.dot(a, b, preferred_element_type=F32, precision=precision)


def _ada_kernel(c_ref, w_ref, b_ref, o_ref):
    c = c_ref[...]
    ca = c * jax.nn.sigmoid(c)
    o_ref[...] = _mm(ca, w_ref[...], HIGHEST) + b_ref[...]


def _ada_mod(c, ada_w, ada_b):
    L, D, N = ada_w.shape
    B = c.shape[0]
    tn = 1536
    return pl.pallas_call(
        _ada_kernel,
        out_shape=jax.ShapeDtypeStruct((L, B, N), F32),
        grid=(L, N // tn),
        in_specs=[
            pl.BlockSpec((B, D), lambda l, j: (0, 0)),
            pl.BlockSpec((None, D, tn), lambda l, j: (l, 0, j)),
            pl.BlockSpec((None, 1, tn), lambda l, j: (l, 0, j)),
        ],
        out_specs=pl.BlockSpec((None, B, tn), lambda l, j: (l, 0, j)),
        compiler_params=_params(("parallel", "parallel")),
        name="ada_mod",
    )(c, ada_w, ada_b.reshape(L, 1, N))


ATT_T = 512
QK_DIFF = 2 * DIFF_WIDTH
QK_FOX = 2 * FOX_WIDTH
VT_ROWS = DIFF_WIDTH + FOX_WIDTH
IN_PAD_COLS = QK_DIFF + RW_PAD_COLS + QK_FOX + LANES


def _inproj_kernel(x_ref, g_ref, sh_ref, sc_ref, w_ref, wvt_ref, d_ref, r_ref, f_ref, fl_ref, vtd_ref, vtf_ref, *, ta):
    x = x_ref[...]
    ms = jnp.mean(x * x, axis=-1, keepdims=True)
    y = x * lax.rsqrt(ms + RMS_EPS) * g_ref[...]
    h = (y * (1.0 + sc_ref[...]) + sh_ref[...]).astype(BF16)
    o1 = QK_DIFF
    o2 = o1 + RW_PAD_COLS
    o3 = o2 + QK_FOX
    d_ref[...] = _mm(h, w_ref[:, 0:o1]).astype(BF16)
    r_ref[...] = _mm(h, w_ref[:, o1:o2])
    f_ref[...] = _mm(h, w_ref[:, o2:o3]).astype(BF16)
    fl_ref[...] = _mm(h, w_ref[:, o3:o3 + LANES])
    vt = _nt(wvt_ref[...], h).astype(BF16)
    for s in range(x.shape[0] // ta):
        vtd_ref[s] = vt[0:DIFF_WIDTH, ta * s:ta * (s + 1)]
        vtf_ref[s] = vt[DIFF_WIDTH:, ta * s:ta * (s + 1)]


def _in_proj(x2, g, mod4, w_pad, w_vt, S):
    T, D = x2.shape
    tm = 512 if S % 512 == 0 else S
    ta = min(ATT_T, S)
    nb = S // tm
    row = lambda i: (i, 0)
    return pl.pallas_call(
        functools.partial(_inproj_kernel, ta=ta),
        out_shape=(
            jax.ShapeDtypeStruct((T, QK_DIFF), BF16),
            jax.ShapeDtypeStruct((T, RW_PAD_COLS), F32),
            jax.ShapeDtypeStruct((T, QK_FOX), BF16),
            jax.ShapeDtypeStruct((T, LANES), F32),
            jax.ShapeDtypeStruct((T // ta, DIFF_WIDTH, ta), BF16),
            jax.ShapeDtypeStruct((T // ta, FOX_WIDTH, ta), BF16),
        ),
        grid=(T // tm,),
        in_specs=[
            pl.BlockSpec((tm, D), row),
            pl.BlockSpec((1, D), lambda i: (0, 0)),
            pl.BlockSpec((None, None, 1, D), lambda i: (i // nb, 0, 0, 0)),
            pl.BlockSpec((None, None, 1, D), lambda i: (i // nb, 1, 0, 0)),
            pl.BlockSpec((D, IN_PAD_COLS), lambda i: (0, 0)),
            pl.BlockSpec((VT_ROWS, D), lambda i: (0, 0)),
        ],
        out_specs=(
            pl.BlockSpec((tm, QK_DIFF), row),
            pl.BlockSpec((tm, RW_PAD_COLS), row),
            pl.BlockSpec((tm, QK_FOX), row),
            pl.BlockSpec((tm, LANES), row),
            pl.BlockSpec((tm // ta, DIFF_WIDTH, ta), lambda i: (i, 0, 0)),
            pl.BlockSpec((tm // ta, FOX_WIDTH, ta), lambda i: (i, 0, 0)),
        ),
        compiler_params=_params(("parallel",)),
        name="in_proj",
    )(x2, g.reshape(1, D), mod4, mod4, w_pad, w_vt)


def _pad_w_in(w_in):
    D = w_in.shape[0]
    W = RWKV_WIDTH
    o = DIFF_COLS
    z64 = jnp.zeros((D, LANES - DECAY_LORA), w_in.dtype)
    rw = w_in[:, o:o + RWKV_COLS]
    fx = w_in[:, o + RWKV_COLS:]
    zf = jnp.zeros((D, LANES - FOX_HEADS), w_in.dtype)
    w_pad = jnp.concatenate([
        w_in[:, :QK_DIFF],
        rw[:, :3 * W], rw[:, 3 * W:3 * W + DECAY_LORA], z64,
        rw[:, 3 * W + DECAY_LORA:3 * W + DECAY_LORA + AAA_LORA], z64,
        rw[:, 3 * W + DECAY_LORA + AAA_LORA:],
        fx[:, :QK_FOX], fx[:, 3 * FOX_WIDTH:], zf,
    ], axis=1).astype(BF16)
    w_vt = jnp.concatenate([w_in[:, QK_DIFF:o], fx[:, QK_FOX:3 * FOX_WIDTH]], axis=1).T.astype(BF16)
    return w_pad, w_vt


LOG2E = math.log2(math.e)


def _flash_step(s2s, vt, m_ref, l_ref, acc_ref):
    n = len(s2s)
    m_old = [m_ref[x] for x in range(n)]
    m_new = [jnp.maximum(m_old[x], jnp.max(s2s[x], axis=0, keepdims=True)) for x in range(n)]
    alpha = [jnp.exp2(m_old[x] - m_new[x]) for x in range(n)]
    p = [jnp.exp2(s2s[x] - m_new[x]) for x in range(n)]
    pv = [_mm(vt, p[x].astype(BF16)) for x in range(n)]
    for x in range(n):
        l_ref[x] = alpha[x] * l_ref[x] + jnp.sum(p[x], axis=0, keepdims=True)
        acc_ref[x] = alpha[x] * acc_ref[x] + pv[x]
        m_ref[x] = m_new[x]


def _diff_attn_kernel(lam_ref, g_ref, q_ref, k_ref, vt_ref, o_ref, m_ref, l_ref, acc_ref, *, tq, lam_init):
    i = pl.program_id(2)
    c = (DIFF_QK_DIM ** -0.5) * LOG2E
    lane = lax.broadcasted_iota(jnp.int32, (1, LANES), 1)
    q = q_ref[...]
    zero = jnp.zeros_like(q)
    qm = [jnp.where((lane >= DIFF_QK_DIM * x) & (lane < DIFF_QK_DIM * (x + 1)), q, zero) for x in range(4)]
    m_ref[...] = jnp.full(m_ref.shape, -jnp.inf, F32)
    l_ref[...] = jnp.zeros(l_ref.shape, F32)
    acc_ref[...] = jnp.zeros(acc_ref.shape, F32)

    def step(j, diag):
        k = k_ref[pl.ds(pl.multiple_of(j * tq, tq), tq), :]
        vt = vt_ref[j]
        if diag:
            keep = (lax.broadcasted_iota(jnp.int32, (tq, tq), 1) >= lax.broadcasted_iota(jnp.int32, (tq, tq), 0))
        s2s = [_nt(k, qm[x]) * c for x in range(4)]
        if diag:
            s2s = [jnp.where(keep, s2, -jnp.inf) for s2 in s2s]
        _flash_step(s2s, vt, m_ref, l_ref, acc_ref)

    def body(j, carry):
        step(j, False)
        return carry

    lax.fori_loop(0, i, body, 0)
    step(i, True)

    lp = lam_ref[...]
    lam = (jnp.exp(jnp.sum(lp[0:1] * lp[1:2], axis=-1, keepdims=True))
           - jnp.exp(jnp.sum(lp[2:3] * lp[3:4], axis=-1, keepdims=True)) + lam_init)
    outs = [acc_ref[2 * hh] / l_ref[2 * hh] - lam * (acc_ref[2 * hh + 1] / l_ref[2 * hh + 1]) for hh in range(2)]
    row = lax.broadcasted_iota(jnp.int32, (LANES, 1), 0)
    o = jnp.where(row < HEAD_DIM, outs[0], outs[1])
    sq = o * o
    ms = jnp.where(row < HEAD_DIM, jnp.sum(sq[0:HEAD_DIM], axis=0, keepdims=True),
                   jnp.sum(sq[HEAD_DIM:], axis=0, keepdims=True)) * (1.0 / HEAD_DIM)
    y = o * lax.rsqrt(ms + RMS_EPS) * g_ref[...] * (1.0 - lam_init)
    o_ref[...] = y.T.astype(o_ref.dtype)


def _diff_attention(dqk, vtd, lam_params, subln_g, layer_idx, B, S):
    T = B * S
    tq = min(ATT_T, S)
    nq = S // tq
    npair = DIFF_HEADS // 2
    lam_init = 0.8 - 0.6 * math.exp(-0.3 * layer_idx)
    g2 = jnp.concatenate([subln_g, subln_g]).reshape(LANES, 1).astype(F32)
    return pl.pallas_call(
        functools.partial(_diff_attn_kernel, tq=tq, lam_init=lam_init),
        out_shape=jax.ShapeDtypeStruct((T, DIFF_WIDTH), BF16),
        grid=(B, npair, nq),
        in_specs=[
            pl.BlockSpec((4, DIFF_QK_DIM), lambda b, p, i: (0, 0)),
            pl.BlockSpec((LANES, 1), lambda b, p, i: (0, 0)),
            pl.BlockSpec((tq, LANES), lambda b, p, i: (b * nq + i, p)),
            pl.BlockSpec((S, LANES), lambda b, p, i: (b, npair + p)),
            pl.BlockSpec((nq, LANES, tq), lambda b, p, i: (b, p, 0)),
        ],
        out_specs=pl.BlockSpec((tq, LANES), lambda b, p, i: (b * nq + i, p)),
        scratch_shapes=[
            pltpu.VMEM((4, 1, tq), F32),
            pltpu.VMEM((4, 1, tq), F32),
            pltpu.VMEM((4, LANES, tq), F32),
        ],
        compiler_params=_params(("parallel", "parallel", "arbitrary")),
        name="diff_attn",
    )(lam_params, g2, dqk, dqk, vtd)


def _fox_cum_kernel(f_ref, b_ref, rep_ref, row_ref, *, S, tc):
    rr = lax.broadcasted_iota(jnp.int32, (tc, tc), 0)
    cc = lax.broadcasted_iota(jnp.int32, (tc, tc), 1)
    tri = (rr >= cc).astype(F32)
    sel_r = lax.broadcasted_iota(jnp.int32, (LANES, LANES), 0)
    carry = jnp.zeros((1, LANES), F32)
    for c in range(S // tc):
        z = f_ref[c * tc:(c + 1) * tc, :] + b_ref[...]
        logf = -(jnp.maximum(-z, 0.0) + jnp.log(1.0 + jnp.exp(-jnp.abs(z))))
        cum = _mm(tri, logf, HIGHEST) + carry
        carry = cum[tc - 1:tc, :]
        row_ref[:, c * tc:(c + 1) * tc] = cum.T[0:8, :]
        for h in range(FOX_HEADS):
            rep_ref[h, c * tc:(c + 1) * tc, :] = _mm(cum, (sel_r == h).astype(F32), HIGHEST)


def _fox_cum(fl, b_f, B, S):
    tc = 256 if S % 256 == 0 else S
    bpad = jnp.zeros((1, LANES), F32).at[0, :FOX_HEADS].set(b_f.astype(F32))
    return pl.pallas_call(
        functools.partial(_fox_cum_kernel, S=S, tc=tc),
        out_shape=(jax.ShapeDtypeStruct((B, FOX_HEADS, S, LANES), F32), jax.ShapeDtypeStruct((B, 8, S), F32)),
        grid=(B,),
        in_specs=[pl.BlockSpec((S, LANES), lambda b: (b, 0)), pl.BlockSpec((1, LANES), lambda b: (0, 0))],
        out_specs=(pl.BlockSpec((None, FOX_HEADS, S, LANES), lambda b: (b, 0, 0, 0)),
                   pl.BlockSpec((None, 8, S), lambda b: (b, 0, 0))),
        compiler_params=_params(("parallel",)),
        name="fox_cum",
    )(fl, bpad)


def _fox_attn_kernel(q_ref, k_ref, vt_ref, c0_ref, c1_ref, cr_ref, o_ref, m_ref, l_ref, acc_ref, *, tq):
    p_id = pl.program_id(1)
    i = pl.program_id(2)
    c = (HEAD_DIM ** -0.5) * LOG2E
    lane = lax.broadcasted_iota(jnp.int32, (1, LANES), 1)
    q = q_ref[...]
    zero = jnp.zeros_like(q)
    qm = [jnp.where((lane >= HEAD_DIM * x) & (lane < HEAD_DIM * (x + 1)), q, zero) for x in range(2)]
    ck_refs = (c0_ref, c1_ref)
    cq = [cr_ref[2 * p_id + x, pl.ds(i, 1), :] for x in range(2)]
    m_ref[...] = jnp.full(m_ref.shape, -jnp.inf, F32)
    l_ref[...] = jnp.zeros(l_ref.shape, F32)
    acc_ref[...] = jnp.zeros(acc_ref.shape, F32)

    def step(j, diag):
        off = pl.multiple_of(j * tq, tq)
        k = k_ref[pl.ds(off, tq), :]
        vt = vt_ref[j]
        if diag:
            keep = (lax.broadcasted_iota(jnp.int32, (tq, tq), 1) >= lax.broadcasted_iota(jnp.int32, (tq, tq), 0))
        s2s = []
        for x in range(2):
            ck = ck_refs[x][pl.ds(off, tq), :]
            bias = (cq[x] - jnp.concatenate([ck] * (tq // LANES), axis=1)) * LOG2E
            s2s.append(_nt(k, qm[x]) * c + bias)
        if diag:
            s2s = [jnp.where(keep, s2, -jnp.inf) for s2 in s2s]
        _flash_step(s2s, vt, m_ref, l_ref, acc_ref)

    def body(j, carry):
        step(j, False)
        return carry

    lax.fori_loop(0, i, body, 0)
    step(i, True)
    row = lax.broadcasted_iota(jnp.int32, (LANES, 1), 0)
    o = jnp.where(row < HEAD_DIM, acc_ref[0] / l_ref[0], acc_ref[1] / l_ref[1])
    o_ref[...] = o.T.astype(o_ref.dtype)


def _fox_attention(fqk, vtf, crep, crow, B, S):
    T = B * S
    tq = min(ATT_T, S)
    nq = S // tq
    npair = FOX_HEADS // 2
    crow4 = crow.reshape(B, 8, nq, tq)
    rep = lambda x: pl.BlockSpec((None, None, S, LANES), lambda b, p, i: (b, 2 * p + x, 0, 0))
    return pl.pallas_call(
        functools.partial(_fox_attn_kernel, tq=tq),
        out_shape=jax.ShapeDtypeStruct((T, FOX_WIDTH), BF16),
        grid=(B, npair, nq),
        in_specs=[
            pl.BlockSpec((tq, LANES), lambda b, p, i: (b * nq + i, p)),
            pl.BlockSpec((S, LANES), lambda b, p, i: (b, npair + p)),
            pl.BlockSpec((nq, LANES, tq), lambda b, p, i: (b, p, 0)),
            rep(0), rep(1),
            pl.BlockSpec((None, 8, nq, tq), lambda b, p, i: (b, 0, 0, 0)),
        ],
        out_specs=pl.BlockSpec((tq, LANES), lambda b, p, i: (b * nq + i, p)),
        scratch_shapes=[
            pltpu.VMEM((2, 1, tq), F32),
            pltpu.VMEM((2, 1, tq), F32),
            pltpu.VMEM((2, LANES, tq), F32),
        ],
        compiler_params=_params(("parallel", "parallel", "arbitrary")),
        name="fox_attn",
    )(fqk, fqk, vtf, crep, crep, crow4)


def _seg_sum(x, seg, npair):
    return jnp.concatenate(
        [_mm(x[:, LANES * p:LANES * (p + 1)], seg, HIGHEST) for p in range(npair)], axis=1)


def _rwkv_kernel(x_ref, mu_ref, w0_ref, w2_ref, a0_ref, a2_ref, g2_ref, kk_ref, ka_ref, rk_ref, lng_ref, lnb_ref,
                 o_ref, carry_ref, st_ref, at_ref, rt_ref, bt_ref, kt_ref, v_ref, wc_ref, y_ref, g_ref, bon_ref,
                 *, tt, prec):
    i = pl.program_id(1)
    W = RWKV_WIDTH
    C = RW_CHUNK
    CPI = RW_CHUNKS_PER_ITER
    npair = RWKV_HEADS // 2

    @pl.when(i == 0)
    def _():
        carry_ref[...] = jnp.zeros(carry_ref.shape, F32)
        st_ref[...] = jnp.zeros(st_ref.shape, F32)

    x = x_ref[...]
    rows = lax.broadcasted_iota(jnp.int32, (tt, 1), 0)
    prev = jnp.where(rows == 0, carry_ref[...], pltpu.roll(x, 1, axis=0))
    carry_ref[...] = x[tt - 1:tt, :]
    xs = x + (prev - x) * mu_ref[...]
    r = xs[:, 0:W]
    k = xs[:, W:2 * W]
    v = xs[:, 2 * W:3 * W]
    xw = xs[:, 3 * W:3 * W + LANES]
    xa = xs[:, 3 * W + LANES:3 * W + 2 * LANES]
    xg = xs[:, 3 * W + 2 * LANES:]
    wl = w0_ref[...] + _mm(jnp.tanh(xw), w2_ref[...], HIGHEST)
    w = -(jnp.maximum(-wl, 0.0) + jnp.log(1.0 + jnp.exp(-jnp.abs(wl)))) - 0.5
    logdec = -jnp.exp(w)
    a = jax.nn.sigmoid(a0_ref[...] + _mm(xa, a2_ref[...], HIGHEST))
    g_ref[...] = _mm(jax.nn.sigmoid(xg), g2_ref[...], HIGHEST)

    r_i = lax.broadcasted_iota(jnp.int32, (LANES, LANES), 0)
    c_i = lax.broadcasted_iota(jnp.int32, (LANES, LANES), 1)
    seg = ((r_i // HEAD_DIM) == (c_i // HEAD_DIM)).astype(F32)
    kkv = k * kk_ref[...]
    kkn = kkv / jnp.maximum(jnp.sqrt(_seg_sum(kkv * kkv, seg, npair)), 1e-12)
    k2 = k * (1.0 + (a - 1.0) * ka_ref[...])
    bon_ref[...] = _seg_sum(r * k2 * rk_ref[...], seg, npair) * v

    rt_i = lax.broadcasted_iota(jnp.int32, (tt, tt), 0)
    ct_i = lax.broadcasted_iota(jnp.int32, (tt, tt), 1)
    tri = (((rt_i // C) == (ct_i // C)) & (rt_i >= ct_i)).astype(F32)
    cum = _mm(tri, logdec, HIGHEST)
    winv = jnp.exp(-cum)
    wcum = jnp.exp(cum)
    at_ref[...] = -kkn * jnp.exp(cum - logdec)
    bt_ref[...] = kkn * a * winv
    kt_ref[...] = k2 * winv
    rt_ref[...] = r * wcum
    v_ref[...] = v
    wc_ref[...] = wcum

    lane = lax.broadcasted_iota(jnp.int32, (1, LANES), 1)
    lo = lane < HEAD_DIM
    tpos = r_i % C
    ipos = c_i % C
    strict = tpos > ipos
    incl = tpos >= ipos
    eye = r_i == c_i

    def stack2(m):
        return jnp.concatenate([jnp.where(lo, m, 0.0), jnp.where(lo, 0.0, m)], axis=0)

    def cast(m):
        return m if prec is not None else m.astype(BF16)

    def mm(p, q):
        return _mm(cast(p), cast(q), prec)

    def nt(p, q):
        return _nt(cast(p), cast(q), prec)

    def chunks(cc, carry):
        units = [(ci, p) for ci in range(CPI) for p in range(npair)]
        r0 = [pl.multiple_of((cc * CPI + ci) * C, C) for ci in range(CPI)]
        ld = lambda ref, ci, p: stack2(ref[pl.ds(r0[ci], C), LANES * p:LANES * (p + 1)])
        a2 = {u: ld(at_ref, *u) for u in units}
        r2 = {u: ld(rt_ref, *u) for u in units}
        b2 = {u: ld(bt_ref, *u) for u in units}
        k2s = {u: ld(kt_ref, *u) for u in units}
        v2 = {u: ld(v_ref, *u) for u in units}
        wl = {(ci, p): wc_ref[pl.ds(pl.multiple_of(r0[ci] + C - 8, 8), 8), LANES * p:LANES * (p + 1)][7:8, :]
              for (ci, p) in units}
        ar = {u: jnp.concatenate([a2[u], r2[u]], axis=0) for u in units}
        mb = {u: nt(ar[u], b2[u]) for u in units}
        mk = {u: nt(ar[u], k2s[u]) for u in units}
        lab = {u: jnp.where(strict, mb[u][0:LANES], 0.0) for u in units}
        mrb = {u: jnp.where(incl, mb[u][LANES:], 0.0) for u in units}
        lak = {u: jnp.where(strict, mk[u][0:LANES], 0.0) for u in units}
        mrk = {u: jnp.where(incl, mk[u][LANES:], 0.0) for u in units}
        xx = {u: jnp.concatenate([a2[u], mm(lak[u], v2[u])], axis=1) for u in units}
        lp = lab
        for it in range(6):
            xx = {u: xx[u] + mm(lp[u], xx[u]) for u in units}
            if it < 5:
                lp = {u: mm(lp[u], lp[u]) for u in units}
        mq = {u: mm(mrb[u], xx[u]) for u in units}
        mv = {u: mm(mrk[u], v2[u]) for u in units}
        bx = {u: mm((b2[u] * wl[u]).T, xx[u]) for u in units}
        kv = {u: mm((k2s[u] * wl[u]).T, v2[u]) for u in units}
        st = [st_ref[p] for p in range(npair)]
        for (ci, p) in units:
            u = (ci, p)
            q1 = r2[u] + mq[u][:, 0:LANES]
            q2 = mq[u][:, LANES:] + mv[u]
            gmat = jnp.where(eye, jnp.broadcast_to(wl[u], (LANES, LANES)), 0.0) + bx[u][:, 0:LANES]
            hmat = bx[u][:, LANES:] + kv[u]
            gs = mm(jnp.concatenate([gmat, q1], axis=0), st[p])
            st[p] = gs[0:LANES] + hmat
            yy = gs[LANES:] + q2
            y_ref[pl.ds(r0[ci], C), LANES * p:LANES * (p + 1)] = yy[0:C] + yy[C:]
        for p in range(npair):
            st_ref[p] = st[p]
        return carry

    lax.fori_loop(0, tt // (C * CPI), chunks, 0)

    y = y_ref[...]
    inv = 1.0 / HEAD_DIM
    mean = _seg_sum(y, seg, npair) * inv
    yc = y - mean
    var = _seg_sum(yc * yc, seg, npair) * inv
    yn = yc * lax.rsqrt(var + RWKV_LN_EPS) * lng_ref[...] + lnb_ref[...]
    o_ref[...] = ((yn + bon_ref[...]) * g_ref[...]).astype(o_ref.dtype)


def _rwkv(rcols, mu, w0, w2, a0, a2, g2, k_k, k_a, r_k, ln_g, ln_b, B, S, prec=HIGHEST):
    T = B * S
    W = RWKV_WIDTH
    tt = 512 if S % 512 == 0 else S
    nt_ = S // tt
    npair = RWKV_HEADS // 2
    pad = LANES - DECAY_LORA
    mu_p = jnp.concatenate([mu[:3 * W], mu[3 * W:3 * W + DECAY_LORA], jnp.zeros((pad,), F32),
                            mu[3 * W + DECAY_LORA:3 * W + DECAY_LORA + AAA_LORA], jnp.zeros((pad,), F32),
                            mu[3 * W + DECAY_LORA + AAA_LORA:]]).reshape(1, RW_PAD_COLS)
    w2p = jnp.concatenate([w2, jnp.zeros((pad, W), F32)], axis=0)
    a2p = jnp.concatenate([a2, jnp.zeros((pad, W), F32)], axis=0)
    vec = lambda t: t.reshape(1, W).astype(F32)
    full = lambda shape: pl.BlockSpec(shape, lambda b, i: (0,) * len(shape))
    sc = lambda: pltpu.VMEM((tt, W), F32)
    return pl.pallas_call(
        functools.partial(_rwkv_kernel, tt=tt, prec=prec),
        out_shape=jax.ShapeDtypeStruct((T, W), BF16),
        grid=(B, nt_),
        in_specs=[
            pl.BlockSpec((tt, RW_PAD_COLS), lambda b, i: (b * nt_ + i, 0)),
            full((1, RW_PAD_COLS)), full((1, W)), full((LANES, W)), full((1, W)), full((LANES, W)),
            full((GATE_LORA, W)), full((1, W)), full((1, W)), full((1, W)), full((1, W)), full((1, W)),
        ],
        out_specs=pl.BlockSpec((tt, W), lambda b, i: (b * nt_ + i, 0)),
        scratch_shapes=[
            pltpu.VMEM((1, RW_PAD_COLS), F32),
            pltpu.VMEM((npair, LANES, LANES), F32),
            sc(), sc(), sc(), sc(), sc(), sc(), sc(), sc(), sc(),
        ],
        compiler_params=_params(("parallel", "arbitrary")),
        name="rwkv7",
    )(rcols, mu_p, vec(w0), w2p, vec(a0), a2p, g2, vec(k_k), vec(k_a), vec(r_k), vec(ln_g), vec(ln_b))


def _outproj_kernel(ya_ref, yb_ref, yc_ref, x_ref, w_ref, gm_ref, g_ref, sh_ref, sc_ref, xo_ref, ho_ref):
    o1 = DIFF_WIDTH
    o2 = o1 + RWKV_WIDTH
    mix = (_mm(ya_ref[...], w_ref[0:o1, :]) + _mm(yb_ref[...], w_ref[o1:o2, :])
           + _mm(yc_ref[...], w_ref[o2:, :]))
    xn = x_ref[...] + gm_ref[...] * mix
    xo_ref[...] = xn
    ms = jnp.mean(xn * xn, axis=-1, keepdims=True)
    y = xn * lax.rsqrt(ms + RMS_EPS) * g_ref[...]
    ho_ref[...] = y * (1.0 + sc_ref[...]) + sh_ref[...]


def _out_proj(ya, yb, yc, x2, w_out, g, mod4, S):
    T, D = x2.shape
    tm = 512 if S % 512 == 0 else S
    nb = S // tm
    row = lambda i: (i, 0)
    modspec = lambda which: pl.BlockSpec((None, None, 1, D), lambda i: (i // nb, which, 0, 0))
    return pl.pallas_call(
        _outproj_kernel,
        out_shape=(jax.ShapeDtypeStruct((T, D), F32), jax.ShapeDtypeStruct((T, D), F32)),
        grid=(T // tm,),
        in_specs=[
            pl.BlockSpec((tm, DIFF_WIDTH), row), pl.BlockSpec((tm, RWKV_WIDTH), row), pl.BlockSpec((tm, FOX_WIDTH), row),
            pl.BlockSpec((tm, D), row),
            pl.BlockSpec((D, D), lambda i: (0, 0)),
            modspec(2),
            pl.BlockSpec((1, D), lambda i: (0, 0)),
            modspec(3), modspec(4),
        ],
        out_specs=(pl.BlockSpec((tm, D), row), pl.BlockSpec((tm, D), row)),
        compiler_params=_params(("parallel",)),
        name="out_proj",
    )(ya, yb, yc, x2, w_out.astype(BF16), mod4, g.reshape(1, D), mod4, mod4)


def _top16(s, iota_f, n):
    vals, poss = [], []
    for _ in range(PEER_TOPK):
        m = jnp.max(s, axis=0, keepdims=True)
        pos = jnp.min(jnp.where(s == m, iota_f, float(n)), axis=0, keepdims=True)
        vals.append(m)
        poss.append(pos)
        s = jnp.where(iota_f == pos, -jnp.inf, s)
    return jnp.concatenate(vals, axis=0), jnp.concatenate(poss, axis=0)


ROUTE_UNROLL = 4
PEER_NCAND = 56


def _peer_cand_tables():
    K = PEER_TOPK
    pairs = [(a, b) for a in range(K) for b in range(K) if (a + 1) * (b + 1) <= K]
    n = PEER_NCAND
    p0 = [[0.0] * K for _ in range(n)]
    p1 = [[0.0] * K for _ in range(n)]
    pad = [0.0] * n
    pos = [float(K * K + r) for r in range(n)]
    for r, (a, b) in enumerate(pairs):
        p0[r][a] = 1.0
        p1[r][b] = 1.0
        pos[r] = float(a * K + b)
    for r in range(len(pairs), n):
        pad[r] = -float("inf")
    col = lambda v: jnp.broadcast_to(jnp.asarray(v, F32)[:, None], (n, LANES))
    return jnp.asarray(p0, F32), jnp.asarray(p1, F32), col(pad), col(pos)


def _peer_route_kernel(h_ref, wq_ref, sk_ref, p0_ref, p1_ref, cpad_ref, cpos_ref, idx_ref, gate_ref, q_scr, e_scr,
                       g_scr):
    K = PEER_TOPK
    hb = h_ref[...].astype(BF16)
    q = _mm(hb, wq_ref[...])
    for hc in range(2 * PEER_HEADS):
        q_scr[hc] = q[:, LANES * hc:LANES * (hc + 1)].astype(BF16)
    iota_n = lax.broadcasted_iota(jnp.int32, (PEER_NKEYS, LANES), 0).astype(F32)
    cpos = cpos_ref[...]

    def one_head(h):
        sv0, si0 = _top16(_nt(sk_ref[2 * h], q_scr[2 * h]), iota_n, PEER_NKEYS)
        sv1, si1 = _top16(_nt(sk_ref[2 * h + 1], q_scr[2 * h + 1]), iota_n, PEER_NKEYS)
        cand = _mm(p0_ref[...], sv0, HIGHEST) + _mm(p1_ref[...], sv1, HIGHEST) + cpad_ref[...]
        cidx = _mm(p0_ref[...], si0) * float(PEER_NKEYS) + _mm(p1_ref[...], si1)
        fv, es = [], []
        for _ in range(K):
            m = jnp.max(cand, axis=0, keepdims=True)
            pos = jnp.min(jnp.where(cand == m, cpos, float(2 * K * K)), axis=0, keepdims=True)
            hit = cpos == pos
            fv.append(m)
            es.append(jnp.max(jnp.where(hit, cidx, -1.0), axis=0, keepdims=True))
            cand = jnp.where(hit, -jnp.inf, cand)
        fv = jnp.concatenate(fv, axis=0)
        ex = jnp.exp(fv - fv[0:1, :])
        g_scr[h] = ex / jnp.sum(ex, axis=0, keepdims=True)
        e_scr[h] = jnp.concatenate(es, axis=0)

    def heads(hh, carry):
        for j in range(ROUTE_UNROLL):
            one_head(hh * ROUTE_UNROLL + j)
        return carry

    lax.fori_loop(0, PEER_HEADS // ROUTE_UNROLL, heads, 0)
    e = e_scr[...].reshape(PEER_HEADS * K, LANES)
    idx_ref[...] = e.T.astype(jnp.int32)
    gate_ref[...] = g_scr[...].reshape(PEER_HEADS * K, LANES)


def _peer_route(h2, wq, subkeys):
    T, D = h2.shape
    tm = LANES
    nq = 2 * PEER_HEADS
    sk = subkeys.reshape(nq, PEER_NKEYS, PEER_HALF).astype(BF16)
    p0, p1, cpad, cpos = _peer_cand_tables()
    const = lambda shape: pl.BlockSpec(shape, lambda i: (0,) * len(shape))
    return pl.pallas_call(
        _peer_route_kernel,
        out_shape=(jax.ShapeDtypeStruct((T, PEER_HEADS * PEER_TOPK), jnp.int32),
                   jax.ShapeDtypeStruct((T // tm, PEER_HEADS * PEER_TOPK, tm), F32)),
        grid=(T // tm,),
        in_specs=[
            pl.BlockSpec((tm, D), lambda i: (i, 0)),
            const((D, nq * PEER_HALF)),
            const((nq, PEER_NKEYS, PEER_HALF)),
            const((PEER_NCAND, PEER_TOPK)), const((PEER_NCAND, PEER_TOPK)),
            const((PEER_NCAND, LANES)), const((PEER_NCAND, LANES)),
        ],
        out_specs=(pl.BlockSpec((tm, PEER_HEADS * PEER_TOPK), lambda i: (i, 0)),
                   pl.BlockSpec((None, PEER_HEADS * PEER_TOPK, tm), lambda i: (i, 0, 0))),
        scratch_shapes=[
            pltpu.VMEM((nq, tm, PEER_HALF), BF16),
            pltpu.VMEM((PEER_HEADS, PEER_TOPK, tm), F32),
            pltpu.VMEM((PEER_HEADS, PEER_TOPK, tm), F32),
        ],
        compiler_params=_params(("parallel",)),
        name="peer_route",
    )(h2, wq.astype(BF16), sk, p0, p1, cpad, cpos)


PEER_G = 16
PEER_SLOTS = PEER_HEADS * PEER_TOPK


def _peer_eval_kernel(idx_ref, idxn_ref, gate_ref, h_ref, x_ref, gf_ref, fg_ref, uv_ref, o_ref, buf, sem, *, final):
    G = PEER_G
    R = G * PEER_SLOTS
    D = D_MODEL
    tiles = PEER_SLOTS // SUBLANES
    i = pl.program_id(0)
    n = pl.num_programs(0)

    def start(ids, off, s, t, u):
        pltpu.make_async_copy(uv_ref.at[ids[off + t * SUBLANES + u]], buf.at[s, t, pl.ds(u, 1), :],
                              sem.at[s]).start(priority=u % 2)

    def wait(s):
        pltpu.make_async_copy(buf.at[s], buf.at[s], sem.at[s]).wait()

    @pl.when(i == 0)
    def _():
        def body(t, carry):
            for u in range(SUBLANES):
                start(idx_ref, 0, 0, t, u)
            return carry
        lax.fori_loop(0, R // SUBLANES, body, 0)

    lane = lax.broadcasted_iota(jnp.int32, (1, LANES), 1)
    tbase = (i % (LANES // (2 * G))) * (2 * G)
    gate = gate_ref[...]
    outs = []
    for s in range(2):
        wait(s)
        nxt_ids, nxt_off = (idx_ref, R) if s == 0 else (idxn_ref, 0)
        for g in range(G):
            for t in range(tiles * g, tiles * (g + 1)):
                for u in range(SUBLANES):
                    start(nxt_ids, nxt_off, 1 - s, t, u)
            w_rows = buf[s, tiles * g:tiles * (g + 1)].reshape(PEER_SLOTS, D)
            u_rows = lax.bitcast_convert_type(w_rows & jnp.uint32(0xFFFF0000), F32)
            prod = u_rows * h_ref[G * s + g:G * s + g + 1, :]
            part = prod[:, 0:LANES]
            for c in range(1, D // LANES):
                part = part + prod[:, LANES * c:LANES * (c + 1)]
            act = jnp.sum(part, axis=1, keepdims=True)
            gcol = jnp.sum(jnp.where(lane == tbase + G * s + g, gate, 0.0), axis=1, keepdims=True)
            coef = gcol * (0.5 * act * (1.0 + lax.erf(act * (2.0 ** -0.5))))
            v_rows = lax.bitcast_convert_type(w_rows << 16, F32)
            outs.append(jnp.sum(v_rows * coef, axis=0, keepdims=True))
    xn = x_ref[...] + gf_ref[...] * jnp.concatenate(outs, axis=0)
    if final:
        ms = jnp.mean(xn * xn, axis=-1, keepdims=True)
        xn = xn * lax.rsqrt(ms + RMS_EPS) * fg_ref[...]
    o_ref[...] = xn

    @pl.when(i == n - 1)
    def _():
        wait(0)


def _peer_eval(eidx, gate_t, h2, x2, mod4, final_g, uv, S, final, ntok=None):
    T, D = x2.shape
    T = T if ntok is None else ntok
    G = PEER_G
    R = G * PEER_SLOTS
    n = T // (2 * G)
    return pl.pallas_call(
        functools.partial(_peer_eval_kernel, final=final),
        out_shape=jax.ShapeDtypeStruct((T, D), F32),
        grid=(n,),
        in_specs=[
            pl.BlockSpec((2 * R,), lambda i: (i,), memory_space=pltpu.SMEM),
            pl.BlockSpec((R,), lambda i: (jnp.minimum(2 * i + 2, 2 * n - 2),), memory_space=pltpu.SMEM),
            pl.BlockSpec((None, PEER_SLOTS, LANES), lambda i: (i // (LANES // (2 * G)), 0, 0)),
            pl.BlockSpec((2 * G, D), lambda i: (i, 0)),
            pl.BlockSpec((2 * G, D), lambda i: (i, 0)),
            pl.BlockSpec((None, None, 1, D), lambda i: (i // (S // (2 * G)), 5, 0, 0)),
            pl.BlockSpec((1, D), lambda i: (0, 0)),
            pl.BlockSpec(memory_space=pl.ANY),
        ],
        out_specs=pl.BlockSpec((2 * G, D), lambda i: (i, 0)),
        scratch_shapes=[pltpu.VMEM((2, R // SUBLANES, SUBLANES, D), jnp.uint32), pltpu.SemaphoreType.DMA((2,))],
        compiler_params=_params(("arbitrary",)),
        name="peer_eval",
    )(eidx.reshape(-1), eidx.reshape(-1), gate_t, h2, x2, mod4, final_g.reshape(1, D), uv)


SC_WORKERS = 32
SC_WINDOW = 128
SC_ROWS = 64
PEER_GD = 16
SC_SHARE_NUM, SC_SHARE_DEN = 1, 4


def _sc_gather(tab, idx):
    n = idx.shape[0]
    width = tab.shape[1]
    per = n // SC_WORKERS
    mesh = plsc.VectorSubcoreMesh(core_axis_name="core", subcore_axis_name="subcore")

    @pl.kernel(out_type=jax.ShapeDtypeStruct((n, width), tab.dtype), mesh=mesh,
               scratch_types=[pltpu.VMEM((SC_WINDOW,), jnp.int32), pltpu.VMEM((SC_ROWS, width), tab.dtype)])
    def gather(x_hbm, i_hbm, o_hbm, idx_v, rows_v):
        wid = lax.axis_index("core") * (SC_WORKERS // 2) + lax.axis_index("subcore")

        @pl.loop(0, per // SC_WINDOW)
        def _(w):
            base = wid * per + w * SC_WINDOW
            pltpu.sync_copy(i_hbm.at[pl.ds(base, SC_WINDOW)], idx_v)
            for k in range(SC_WINDOW // SC_ROWS):
                pltpu.sync_copy(x_hbm.at[idx_v.at[pl.ds(k * SC_ROWS, SC_ROWS)]], rows_v)
                pltpu.sync_copy(rows_v, o_hbm.at[pl.ds(base + k * SC_ROWS, SC_ROWS)])

    return gather(tab, idx)


def _peer_dense_kernel(rows_ref, gate_ref, h_ref, x_ref, gf_ref, fg_ref, o_ref, *, final, blk0):
    G = PEER_GD
    D = D_MODEL
    i = pl.program_id(0)
    lane = lax.broadcasted_iota(jnp.int32, (1, LANES), 1)
    tbase = ((blk0 + i) % (LANES // G)) * G
    gate = gate_ref[...]
    outs = []
    for g in range(G):
        w_rows = rows_ref[PEER_SLOTS * g:PEER_SLOTS * (g + 1), :]
        u_rows = lax.bitcast_convert_type(w_rows & jnp.uint32(0xFFFF0000), F32)
        prod = u_rows * h_ref[g:g + 1, :]
        part = prod[:, 0:LANES]
        for c in range(1, D // LANES):
            part = part + prod[:, LANES * c:LANES * (c + 1)]
        act = jnp.sum(part, axis=1, keepdims=True)
        gcol = jnp.sum(jnp.where(lane == tbase + g, gate, 0.0), axis=1, keepdims=True)
        coef = gcol * (0.5 * act * (1.0 + lax.erf(act * (2.0 ** -0.5))))
        v_rows = lax.bitcast_convert_type(w_rows << 16, F32)
        outs.append(jnp.sum(v_rows * coef, axis=0, keepdims=True))
    xn = x_ref[...] + gf_ref[...] * jnp.concatenate(outs, axis=0)
    if final:
        ms = jnp.mean(xn * xn, axis=-1, keepdims=True)
        xn = xn * lax.rsqrt(ms + RMS_EPS) * fg_ref[...]
    o_ref[...] = xn


def _peer_eval_dense(rows, gate_t, h2, x2, mod4, final_g, S, final, tok0):
    T, D = x2.shape
    G = PEER_GD
    tb = rows.shape[0] // PEER_SLOTS
    blk0 = tok0 // G
    return pl.pallas_call(
        functools.partial(_peer_dense_kernel, final=final, blk0=blk0),
        out_shape=jax.ShapeDtypeStruct((tb, D), F32),
        grid=(tb // G,),
        in_specs=[
            pl.BlockSpec((G * PEER_SLOTS, D), lambda i: (i, 0)),
            pl.BlockSpec((None, PEER_SLOTS, LANES), lambda i: ((blk0 + i) // (LANES // G), 0, 0)),
            pl.BlockSpec((G, D), lambda i: (blk0 + i, 0)),
            pl.BlockSpec((G, D), lambda i: (blk0 + i, 0)),
            pl.BlockSpec((None, None, 1, D), lambda i: ((blk0 + i) // (S // G), 5, 0, 0)),
            pl.BlockSpec((1, D), lambda i: (0, 0)),
        ],
        out_specs=pl.BlockSpec((G, D), lambda i: (i, 0)),
        compiler_params=_params(("parallel",)),
        name="peer_dense",
    )(rows, gate_t, h2, x2, mod4, final_g.reshape(1, D))


def _pack_uv(u, v):
    hi = lax.bitcast_convert_type(u.astype(BF16), jnp.uint16).astype(jnp.uint32) << 16
    lo = lax.bitcast_convert_type(v.astype(BF16), jnp.uint16).astype(jnp.uint32)
    return (hi | lo).reshape(u.shape[0], 1, u.shape[1])


def kernel(x, c, norm_mix_g, norm_ffn_g, final_norm_g, ada_w, ada_b, w_in, w_out, dif_lam, dif_subln_g, rw_mu, rw_w0,
           rw_w2, rw_a0, rw_a2, rw_g2, rw_kk, rw_ka, rw_rk, rw_ln_g, rw_ln_b, fox_bf, peer_wq, peer_subkeys, peer_u,
           peer_v):
    B, S, D = x.shape
    T = B * S
    depth = ada_w.shape[0]
    x2 = x.reshape(T, D)
    mod = _ada_mod(c, ada_w, ada_b)
    for l in range(depth):
        mod4 = mod[l].reshape(B, 6, 1, D)
        w_pad, w_vt = _pad_w_in(w_in[l])
        dqk, rcols, fqk, fl, vtd, vtf = _in_proj(x2, norm_mix_g[l], mod4, w_pad, w_vt, S)
        ya = _diff_attention(dqk, vtd, dif_lam[l], dif_subln_g[l], l, B, S)
        yb = _rwkv(rcols, rw_mu[l], rw_w0[l], rw_w2[l], rw_a0[l], rw_a2[l], rw_g2[l], rw_kk[l], rw_ka[l],
                   rw_rk[l].reshape(-1), rw_ln_g[l], rw_ln_b[l], B, S, prec=None)
        crep, crow = _fox_cum(fl, fox_bf[l], B, S)
        yc = _fox_attention(fqk, vtf, crep, crow, B, S)
        x2, h2 = _out_proj(ya, yb, yc, x2, w_out[l], norm_ffn_g[l], mod4, S)
        eidx, gate_t = _peer_route(h2, peer_wq[l], peer_subkeys[l])
        uv = _pack_uv(peer_u[l], peer_v[l])
        final = l == depth - 1
        tb = (T * SC_SHARE_NUM // SC_SHARE_DEN) // (SC_WORKERS * SC_WINDOW) * (SC_WORKERS * SC_WINDOW)
        ta = T - tb
        rows_b = _sc_gather(uv.reshape(-1, D), eidx.reshape(-1)[ta * PEER_SLOTS:])
        xa = _peer_eval(eidx, gate_t, h2, x2, mod4, final_norm_g, uv, S, final, ntok=ta)
        xb = _peer_eval_dense(rows_b, gate_t, h2, x2, mod4, final_norm_g, S, final, ta)
        x2 = jnp.concatenate([xa, xb], axis=0)
    return x2.reshape(B, S, D)
```

```python
import functools
import math

import jax
import jax.numpy as jnp
from jax import lax
from jax.experimental import pallas as pl
from jax.experimental.pallas import tpu as pltpu
from jax.experimental.pallas import tpu_sc as plsc

F32 = jnp.float32
BF16 = jnp.bfloat16
HIGHEST = lax.Precision.HIGHEST

D_MODEL = 1024
HEAD_DIM = 64
DIFF_HEADS = 6
DIFF_QK_DIM = HEAD_DIM // 2
RWKV_HEADS = 6
FOX_HEADS = 4
DIFF_WIDTH = DIFF_HEADS * HEAD_DIM
RWKV_WIDTH = RWKV_HEADS * HEAD_DIM
FOX_WIDTH = FOX_HEADS * HEAD_DIM
DECAY_LORA = 64
AAA_LORA = 64
GATE_LORA = 128
DIFF_COLS = 3 * DIFF_WIDTH
RWKV_COLS = 3 * RWKV_WIDTH + DECAY_LORA + AAA_LORA + GATE_LORA
PEER_HEADS = 8
PEER_NKEYS = 128
PEER_TOPK = 16
PEER_QDIM = 256
PEER_HALF = PEER_QDIM // 2
RMS_EPS = 1e-6
RWKV_LN_EPS = 64e-5

LANES = 128
SUBLANES = 8
RW_PAD_COLS = 3 * RWKV_WIDTH + 3 * LANES
VMEM_LIMIT = 56 * 1024 * 1024

RW_CHUNK = 64
RW_CHUNKS_PER_ITER = 2


def _params(sem, vmem=VMEM_LIMIT):
    return pltpu.CompilerParams(dimension_semantics=sem, vmem_limit_bytes=vmem)


def _nt(a, b, precision=None):
    return lax.dot_general(a, b, (((1,), (1,)), ((), ())), preferred_element_type=F32, precision=precision)


def _mm(a, b, precision=None):
    return jnp.dot(a, b, preferred_element_type=F32, precision=precision)


def _ada_kernel(c_ref, w_ref, b_ref, o_ref):
    c = c_ref[...]
    ca = c * jax.nn.sigmoid(c)
    o_ref[...] = _mm(ca, w_ref[...], HIGHEST) + b_ref[...]


def _ada_mod(c, ada_w, ada_b):
    L, D, N = ada_w.shape
    B = c.shape[0]
    tn = 1536
    return pl.pallas_call(
        _ada_kernel,
        out_shape=jax.ShapeDtypeStruct((L, B, N), F32),
        grid=(L, N // tn),
        in_specs=[
            pl.BlockSpec((B, D), lambda l, j: (0, 0)),
            pl.BlockSpec((None, D, tn), lambda l, j: (l, 0, j)),
            pl.BlockSpec((None, 1, tn), lambda l, j: (l, 0, j)),
        ],
        out_specs=pl.BlockSpec((None, B, tn), lambda l, j: (l, 0, j)),
        compiler_params=_params(("parallel", "parallel")),
        name="ada_mod",
    )(c, ada_w, ada_b.reshape(L, 1, N))


ATT_T = 512
QK_DIFF = 2 * DIFF_WIDTH
QK_FOX = 2 * FOX_WIDTH
VT_ROWS = DIFF_WIDTH + FOX_WIDTH
IN_PAD_COLS = QK_DIFF + RW_PAD_COLS + QK_FOX + LANES


def _inproj_kernel(x_ref, g_ref, sh_ref, sc_ref, w_ref, wvt_ref, d_ref, r_ref, f_ref, fl_ref, vtd_ref, vtf_ref, *, ta):
    x = x_ref[...]
    ms = jnp.mean(x * x, axis=-1, keepdims=True)
    y = x * lax.rsqrt(ms + RMS_EPS) * g_ref[...]
    h = (y * (1.0 + sc_ref[...]) + sh_ref[...]).astype(BF16)
    o1 = QK_DIFF
    o2 = o1 + RW_PAD_COLS
    o3 = o2 + QK_FOX
    d_ref[...] = _mm(h, w_ref[:, 0:o1]).astype(BF16)
    r_ref[...] = _mm(h, w_ref[:, o1:o2])
    f_ref[...] = _mm(h, w_ref[:, o2:o3]).astype(BF16)
    fl_ref[...] = _mm(h, w_ref[:, o3:o3 + LANES])
    vt = _nt(wvt_ref[...], h).astype(BF16)
    for s in range(x.shape[0] // ta):
        vtd_ref[s] = vt[0:DIFF_WIDTH, ta * s:ta * (s + 1)]
        vtf_ref[s] = vt[DIFF_WIDTH:, ta * s:ta * (s + 1)]


def _in_proj(x2, g, mod4, w_pad, w_vt, S):
    T, D = x2.shape
    tm = 512 if S % 512 == 0 else S
    ta = min(ATT_T, S)
    nb = S // tm
    row = lambda i: (i, 0)
    return pl.pallas_call(
        functools.partial(_inproj_kernel, ta=ta),
        out_shape=(
            jax.ShapeDtypeStruct((T, QK_DIFF), BF16),
            jax.ShapeDtypeStruct((T, RW_PAD_COLS), F32),
            jax.ShapeDtypeStruct((T, QK_FOX), BF16),
            jax.ShapeDtypeStruct((T, LANES), F32),
            jax.ShapeDtypeStruct((T // ta, DIFF_WIDTH, ta), BF16),
            jax.ShapeDtypeStruct((T // ta, FOX_WIDTH, ta), BF16),
        ),
        grid=(T // tm,),
        in_specs=[
            pl.BlockSpec((tm, D), row),
            pl.BlockSpec((1, D), lambda i: (0, 0)),
            pl.BlockSpec((None, None, 1, D), lambda i: (i // nb, 0, 0, 0)),
            pl.BlockSpec((None, None, 1, D), lambda i: (i // nb, 1, 0, 0)),
            pl.BlockSpec((D, IN_PAD_COLS), lambda i: (0, 0)),
            pl.BlockSpec((VT_ROWS, D), lambda i: (0, 0)),
        ],
        out_specs=(
            pl.BlockSpec((tm, QK_DIFF), row),
            pl.BlockSpec((tm, RW_PAD_COLS), row),
            pl.BlockSpec((tm, QK_FOX), row),
            pl.BlockSpec((tm, LANES), row),
            pl.BlockSpec((tm // ta, DIFF_WIDTH, ta), lambda i: (i, 0, 0)),
            pl.BlockSpec((tm // ta, FOX_WIDTH, ta), lambda i: (i, 0, 0)),
        ),
        compiler_params=_params(("parallel",)),
        name="in_proj",
    )(x2, g.reshape(1, D), mod4, mod4, w_pad, w_vt)


def _pad_w_in(w_in):
    D = w_in.shape[0]
    W = RWKV_WIDTH
    o = DIFF_COLS
    z64 = jnp.zeros((D, LANES - DECAY_LORA), w_in.dtype)
    rw = w_in[:, o:o + RWKV_COLS]
    fx = w_in[:, o + RWKV_COLS:]
    zf = jnp.zeros((D, LANES - FOX_HEADS), w_in.dtype)
    w_pad = jnp.concatenate([
        w_in[:, :QK_DIFF],
        rw[:, :3 * W], rw[:, 3 * W:3 * W + DECAY_LORA], z64,
        rw[:, 3 * W + DECAY_LORA:3 * W + DECAY_LORA + AAA_LORA], z64,
        rw[:, 3 * W + DECAY_LORA + AAA_LORA:],
        fx[:, :QK_FOX], fx[:, 3 * FOX_WIDTH:], zf,
    ], axis=1).astype(BF16)
    w_vt = jnp.concatenate([w_in[:, QK_DIFF:o], fx[:, QK_FOX:3 * FOX_WIDTH]], axis=1).T.astype(BF16)
    return w_pad, w_vt


LOG2E = math.log2(math.e)


def _flash_step(s2s, vt, m_ref, l_ref, acc_ref):
    n = len(s2s)
    m_old = [m_ref[x] for x in range(n)]
    m_new = [jnp.maximum(m_old[x], jnp.max(s2s[x], axis=0, keepdims=True)) for x in range(n)]
    alpha = [jnp.exp2(m_old[x] - m_new[x]) for x in range(n)]
    p = [jnp.exp2(s2s[x] - m_new[x]) for x in range(n)]
    pv = [_mm(vt, p[x].astype(BF16)) for x in range(n)]
    for x in range(n):
        l_ref[x] = alpha[x] * l_ref[x] + jnp.sum(p[x], axis=0, keepdims=True)
        acc_ref[x] = alpha[x] * acc_ref[x] + pv[x]
        m_ref[x] = m_new[x]


def _diff_attn_kernel(lam_ref, g_ref, q_ref, k_ref, vt_ref, o_ref, m_ref, l_ref, acc_ref, *, tq, lam_init):
    i = pl.program_id(2)
    c = (DIFF_QK_DIM ** -0.5) * LOG2E
    lane = lax.broadcasted_iota(jnp.int32, (1, LANES), 1)
    q = q_ref[...]
    zero = jnp.zeros_like(q)
    qm = [jnp.where((lane >= DIFF_QK_DIM * x) & (lane < DIFF_QK_DIM * (x + 1)), q, zero) for x in range(4)]
    m_ref[...] = jnp.full(m_ref.shape, -jnp.inf, F32)
    l_ref[...] = jnp.zeros(l_ref.shape, F32)
    acc_ref[...] = jnp.zeros(acc_ref.shape, F32)

    def step(j, diag):
        k = k_ref[pl.ds(pl.multiple_of(j * tq, tq), tq), :]
        vt = vt_ref[j]
        if diag:
            keep = (lax.broadcasted_iota(jnp.int32, (tq, tq), 1) >= lax.broadcasted_iota(jnp.int32, (tq, tq), 0))
        s2s = [_nt(k, qm[x]) * c for x in range(4)]
        if diag:
            s2s = [jnp.where(keep, s2, -jnp.inf) for s2 in s2s]
        _flash_step(s2s, vt, m_ref, l_ref, acc_ref)

    def body(j, carry):
        step(j, False)
        return carry

    lax.fori_loop(0, i, body, 0)
    step(i, True)

    lp = lam_ref[...]
    lam = (jnp.exp(jnp.sum(lp[0:1] * lp[1:2], axis=-1, keepdims=True))
           - jnp.exp(jnp.sum(lp[2:3] * lp[3:4], axis=-1, keepdims=True)) + lam_init)
    outs = [acc_ref[2 * hh] / l_ref[2 * hh] - lam * (acc_ref[2 * hh + 1] / l_ref[2 * hh + 1]) for hh in range(2)]
    row = lax.broadcasted_iota(jnp.int32, (LANES, 1), 0)
    o = jnp.where(row < HEAD_DIM, outs[0], outs[1])
    sq = o * o
    ms = jnp.where(row < HEAD_DIM, jnp.sum(sq[0:HEAD_DIM], axis=0, keepdims=True),
                   jnp.sum(sq[HEAD_DIM:], axis=0, keepdims=True)) * (1.0 / HEAD_DIM)
    y = o * lax.rsqrt(ms + RMS_EPS) * g_ref[...] * (1.0 - lam_init)
    o_ref[...] = y.T.astype(o_ref.dtype)


def _diff_attention(dqk, vtd, lam_params, subln_g, layer_idx, B, S):
    T = B * S
    tq = min(ATT_T, S)
    nq = S // tq
    npair = DIFF_HEADS // 2
    lam_init = 0.8 - 0.6 * math.exp(-0.3 * layer_idx)
    g2 = jnp.concatenate([subln_g, subln_g]).reshape(LANES, 1).astype(F32)
    return pl.pallas_call(
        functools.partial(_diff_attn_kernel, tq=tq, lam_init=lam_init),
        out_shape=jax.ShapeDtypeStruct((T, DIFF_WIDTH), BF16),
        grid=(B, npair, nq),
        in_specs=[
            pl.BlockSpec((4, DIFF_QK_DIM), lambda b, p, i: (0, 0)),
            pl.BlockSpec((LANES, 1), lambda b, p, i: (0, 0)),
            pl.BlockSpec((tq, LANES), lambda b, p, i: (b * nq + i, p)),
            pl.BlockSpec((S, LANES), lambda b, p, i: (b, npair + p)),
            pl.BlockSpec((nq, LANES, tq), lambda b, p, i: (b, p, 0)),
        ],
        out_specs=pl.BlockSpec((tq, LANES), lambda b, p, i: (b * nq + i, p)),
        scratch_shapes=[
            pltpu.VMEM((4, 1, tq), F32),
            pltpu.VMEM((4, 1, tq), F32),
            pltpu.VMEM((4, LANES, tq), F32),
        ],
        compiler_params=_params(("parallel", "parallel", "arbitrary")),
        name="diff_attn",
    )(lam_params, g2, dqk, dqk, vtd)


def _fox_cum_kernel(f_ref, b_ref, rep_ref, row_ref, *, S, tc):
    rr = lax.broadcasted_iota(jnp.int32, (tc, tc), 0)
    cc = lax.broadcasted_iota(jnp.int32, (tc, tc), 1)
    tri = (rr >= cc).astype(F32)
    sel_r = lax.broadcasted_iota(jnp.int32, (LANES, LANES), 0)
    carry = jnp.zeros((1, LANES), F32)
    for c in range(S // tc):
        z = f_ref[c * tc:(c + 1) * tc, :] + b_ref[...]
        logf = -(jnp.maximum(-z, 0.0) + jnp.log(1.0 + jnp.exp(-jnp.abs(z))))
        cum = _mm(tri, logf, HIGHEST) + carry
        carry = cum[tc - 1:tc, :]
        row_ref[:, c * tc:(c + 1) * tc] = cum.T[0:8, :]
        for h in range(FOX_HEADS):
            rep_ref[h, c * tc:(c + 1) * tc, :] = _mm(cum, (sel_r == h).astype(F32), HIGHEST)


def _fox_cum(fl, b_f, B, S):
    tc = 256 if S % 256 == 0 else S
    bpad = jnp.zeros((1, LANES), F32).at[0, :FOX_HEADS].set(b_f.astype(F32))
    return pl.pallas_call(
        functools.partial(_fox_cum_kernel, S=S, tc=tc),
        out_shape=(jax.ShapeDtypeStruct((B, FOX_HEADS, S, LANES), F32), jax.ShapeDtypeStruct((B, 8, S), F32)),
        grid=(B,),
        in_specs=[pl.BlockSpec((S, LANES), lambda b: (b, 0)), pl.BlockSpec((1, LANES), lambda b: (0, 0))],
        out_specs=(pl.BlockSpec((None, FOX_HEADS, S, LANES), lambda b: (b, 0, 0, 0)),
                   pl.BlockSpec((None, 8, S), lambda b: (b, 0, 0))),
        compiler_params=_params(("parallel",)),
        name="fox_cum",
    )(fl, bpad)


def _fox_attn_kernel(q_ref, k_ref, vt_ref, c0_ref, c1_ref, cr_ref, o_ref, m_ref, l_ref, acc_ref, *, tq):
    p_id = pl.program_id(1)
    i = pl.program_id(2)
    c = (HEAD_DIM ** -0.5) * LOG2E
    lane = lax.broadcasted_iota(jnp.int32, (1, LANES), 1)
    q = q_ref[...]
    zero = jnp.zeros_like(q)
    qm = [jnp.where((lane >= HEAD_DIM * x) & (lane < HEAD_DIM * (x + 1)), q, zero) for x in range(2)]
    ck_refs = (c0_ref, c1_ref)
    cq = [cr_ref[2 * p_id + x, pl.ds(i, 1), :] for x in range(2)]
    m_ref[...] = jnp.full(m_ref.shape, -jnp.inf, F32)
    l_ref[...] = jnp.zeros(l_ref.shape, F32)
    acc_ref[...] = jnp.zeros(acc_ref.shape, F32)

    def step(j, diag):
        off = pl.multiple_of(j * tq, tq)
        k = k_ref[pl.ds(off, tq), :]
        vt = vt_ref[j]
        if diag:
            keep = (lax.broadcasted_iota(jnp.int32, (tq, tq), 1) >= lax.broadcasted_iota(jnp.int32, (tq, tq), 0))
        s2s = []
        for x in range(2):
            ck = ck_refs[x][pl.ds(off, tq), :]
            bias = (cq[x] - jnp.concatenate([ck] * (tq // LANES), axis=1)) * LOG2E
            s2s.append(_nt(k, qm[x]) * c + bias)
        if diag:
            s2s = [jnp.where(keep, s2, -jnp.inf) for s2 in s2s]
        _flash_step(s2s, vt, m_ref, l_ref, acc_ref)

    def body(j, carry):
        step(j, False)
        return carry

    lax.fori_loop(0, i, body, 0)
    step(i, True)
    row = lax.broadcasted_iota(jnp.int32, (LANES, 1), 0)
    o = jnp.where(row < HEAD_DIM, acc_ref[0] / l_ref[0], acc_ref[1] / l_ref[1])
    o_ref[...] = o.T.astype(o_ref.dtype)


def _fox_attention(fqk, vtf, crep, crow, B, S):
    T = B * S
    tq = min(ATT_T, S)
    nq = S // tq
    npair = FOX_HEADS // 2
    crow4 = crow.reshape(B, 8, nq, tq)
    rep = lambda x: pl.BlockSpec((None, None, S, LANES), lambda b, p, i: (b, 2 * p + x, 0, 0))
    return pl.pallas_call(
        functools.partial(_fox_attn_kernel, tq=tq),
        out_shape=jax.ShapeDtypeStruct((T, FOX_WIDTH), BF16),
        grid=(B, npair, nq),
        in_specs=[
            pl.BlockSpec((tq, LANES), lambda b, p, i: (b * nq + i, p)),
            pl.BlockSpec((S, LANES), lambda b, p, i: (b, npair + p)),
            pl.BlockSpec((nq, LANES, tq), lambda b, p, i: (b, p, 0)),
            rep(0), rep(1),
            pl.BlockSpec((None, 8, nq, tq), lambda b, p, i: (b, 0, 0, 0)),
        ],
        out_specs=pl.BlockSpec((tq, LANES), lambda b, p, i: (b * nq + i, p)),
        scratch_shapes=[
            pltpu.VMEM((2, 1, tq), F32),
            pltpu.VMEM((2, 1, tq), F32),
            pltpu.VMEM((2, LANES, tq), F32),
        ],
        compiler_params=_params(("parallel", "parallel", "arbitrary")),
        name="fox_attn",
    )(fqk, fqk, vtf, crep, crep, crow4)


def _seg_sum(x, seg, npair):
    return jnp.concatenate(
        [_mm(x[:, LANES * p:LANES * (p + 1)], seg, HIGHEST) for p in range(npair)], axis=1)


def _rwkv_kernel(x_ref, mu_ref, w0_ref, w2_ref, a0_ref, a2_ref, g2_ref, kk_ref, ka_ref, rk_ref, lng_ref, lnb_ref,
                 o_ref, carry_ref, st_ref, at_ref, rt_ref, bt_ref, kt_ref, v_ref, wc_ref, y_ref, g_ref, bon_ref,
                 *, tt, prec):
    i = pl.program_id(1)
    W = RWKV_WIDTH
    C = RW_CHUNK
    CPI = RW_CHUNKS_PER_ITER
    npair = RWKV_HEADS // 2

    @pl.when(i == 0)
    def _():
        carry_ref[...] = jnp.zeros(carry_ref.shape, F32)
        st_ref[...] = jnp.zeros(st_ref.shape, F32)

    x = x_ref[...]
    rows = lax.broadcasted_iota(jnp.int32, (tt, 1), 0)
    prev = jnp.where(rows == 0, carry_ref[...], pltpu.roll(x, 1, axis=0))
    carry_ref[...] = x[tt - 1:tt, :]
    xs = x + (prev - x) * mu_ref[...]
    r = xs[:, 0:W]
    k = xs[:, W:2 * W]
    v = xs[:, 2 * W:3 * W]
    xw = xs[:, 3 * W:3 * W + LANES]
    xa = xs[:, 3 * W + LANES:3 * W + 2 * LANES]
    xg = xs[:, 3 * W + 2 * LANES:]
    wl = w0_ref[...] + _mm(jnp.tanh(xw), w2_ref[...], HIGHEST)
    w = -(jnp.maximum(-wl, 0.0) + jnp.log(1.0 + jnp.exp(-jnp.abs(wl)))) - 0.5
    logdec = -jnp.exp(w)
    a = jax.nn.sigmoid(a0_ref[...] + _mm(xa, a2_ref[...], HIGHEST))
    g_ref[...] = _mm(jax.nn.sigmoid(xg), g2_ref[...], HIGHEST)

    r_i = lax.broadcasted_iota(jnp.int32, (LANES, LANES), 0)
    c_i = lax.broadcasted_iota(jnp.int32, (LANES, LANES), 1)
    seg = ((r_i // HEAD_DIM) == (c_i // HEAD_DIM)).astype(F32)
    kkv = k * kk_ref[...]
    kkn = kkv / jnp.maximum(jnp.sqrt(_seg_sum(kkv * kkv, seg, npair)), 1e-12)
    k2 = k * (1.0 + (a - 1.0) * ka_ref[...])
    bon_ref[...] = _seg_sum(r * k2 * rk_ref[...], seg, npair) * v

    rt_i = lax.broadcasted_iota(jnp.int32, (tt, tt), 0)
    ct_i = lax.broadcasted_iota(jnp.int32, (tt, tt), 1)
    tri = (((rt_i // C) == (ct_i // C)) & (rt_i >= ct_i)).astype(F32)
    cum = _mm(tri, logdec, HIGHEST)
    winv = jnp.exp(-cum)
    wcum = jnp.exp(cum)
    at_ref[...] = -kkn * jnp.exp(cum - logdec)
    bt_ref[...] = kkn * a * winv
    kt_ref[...] = k2 * winv
    rt_ref[...] = r * wcum
    v_ref[...] = v
    wc_ref[...] = wcum

    lane = lax.broadcasted_iota(jnp.int32, (1, LANES), 1)
    lo = lane < HEAD_DIM
    tpos = r_i % C
    ipos = c_i % C
    strict = tpos > ipos
    incl = tpos >= ipos
    eye = r_i == c_i

    def stack2(m):
        return jnp.concatenate([jnp.where(lo, m, 0.0), jnp.where(lo, 0.0, m)], axis=0)

    def cast(m):
        return m if prec is not None else m.astype(BF16)

    def mm(p, q):
        return _mm(cast(p), cast(q), prec)

    def nt(p, q):
        return _nt(cast(p), cast(q), prec)

    def chunks(cc, carry):
        units = [(ci, p) for ci in range(CPI) for p in range(npair)]
        r0 = [pl.multiple_of((cc * CPI + ci) * C, C) for ci in range(CPI)]
        ld = lambda ref, ci, p: stack2(ref[pl.ds(r0[ci], C), LANES * p:LANES * (p + 1)])
        a2 = {u: ld(at_ref, *u) for u in units}
        r2 = {u: ld(rt_ref, *u) for u in units}
        b2 = {u: ld(bt_ref, *u) for u in units}
        k2s = {u: ld(kt_ref, *u) for u in units}
        v2 = {u: ld(v_ref, *u) for u in units}
        wl = {(ci, p): wc_ref[pl.ds(pl.multiple_of(r0[ci] + C - 8, 8), 8), LANES * p:LANES * (p + 1)][7:8, :]
              for (ci, p) in units}
        ar = {u: jnp.concatenate([a2[u], r2[u]], axis=0) for u in units}
        mb = {u: nt(ar[u], b2[u]) for u in units}
        mk = {u: nt(ar[u], k2s[u]) for u in units}
        lab = {u: jnp.where(strict, mb[u][0:LANES], 0.0) for u in units}
        mrb = {u: jnp.where(incl, mb[u][LANES:], 0.0) for u in units}
        lak = {u: jnp.where(strict, mk[u][0:LANES], 0.0) for u in units}
        mrk = {u: jnp.where(incl, mk[u][LANES:], 0.0) for u in units}
        xx = {u: jnp.concatenate([a2[u], mm(lak[u], v2[u])], axis=1) for u in units}
        lp = lab
        for it in range(6):
            xx = {u: xx[u] + mm(lp[u], xx[u]) for u in units}
            if it < 5:
                lp = {u: mm(lp[u], lp[u]) for u in units}
        mq = {u: mm(mrb[u], xx[u]) for u in units}
        mv = {u: mm(mrk[u], v2[u]) for u in units}
        bx = {u: mm((b2[u] * wl[u]).T, xx[u]) for u in units}
        kv = {u: mm((k2s[u] * wl[u]).T, v2[u]) for u in units}
        st = [st_ref[p] for p in range(npair)]
        for (ci, p) in units:
            u = (ci, p)
            q1 = r2[u] + mq[u][:, 0:LANES]
            q2 = mq[u][:, LANES:] + mv[u]
            gmat = jnp.where(eye, jnp.broadcast_to(wl[u], (LANES, LANES)), 0.0) + bx[u][:, 0:LANES]
            hmat = bx[u][:, LANES:] + kv[u]
            gs = mm(jnp.concatenate([gmat, q1], axis=0), st[p])
            st[p] = gs[0:LANES] + hmat
            yy = gs[LANES:] + q2
            y_ref[pl.ds(r0[ci], C), LANES * p:LANES * (p + 1)] = yy[0:C] + yy[C:]
        for p in range(npair):
            st_ref[p] = st[p]
        return carry

    lax.fori_loop(0, tt // (C * CPI), chunks, 0)

    y = y_ref[...]
    inv = 1.0 / HEAD_DIM
    mean = _seg_sum(y, seg, npair) * inv
    yc = y - mean
    var = _seg_sum(yc * yc, seg, npair) * inv
    yn = yc * lax.rsqrt(var + RWKV_LN_EPS) * lng_ref[...] + lnb_ref[...]
    o_ref[...] = ((yn + bon_ref[...]) * g_ref[...]).astype(o_ref.dtype)


def _rwkv(rcols, mu, w0, w2, a0, a2, g2, k_k, k_a, r_k, ln_g, ln_b, B, S, prec=HIGHEST):
    T = B * S
    W = RWKV_WIDTH
    tt = 512 if S % 512 == 0 else S
    nt_ = S // tt
    npair = RWKV_HEADS // 2
    pad = LANES - DECAY_LORA
    mu_p = jnp.concatenate([mu[:3 * W], mu[3 * W:3 * W + DECAY_LORA], jnp.zeros((pad,), F32),
                            mu[3 * W + DECAY_LORA:3 * W + DECAY_LORA + AAA_LORA], jnp.zeros((pad,), F32),
                            mu[3 * W + DECAY_LORA + AAA_LORA:]]).reshape(1, RW_PAD_COLS)
    w2p = jnp.concatenate([w2, jnp.zeros((pad, W), F32)], axis=0)
    a2p = jnp.concatenate([a2, jnp.zeros((pad, W), F32)], axis=0)
    vec = lambda t: t.reshape(1, W).astype(F32)
    full = lambda shape: pl.BlockSpec(shape, lambda b, i: (0,) * len(shape))
    sc = lambda: pltpu.VMEM((tt, W), F32)
    return pl.pallas_call(
        functools.partial(_rwkv_kernel, tt=tt, prec=prec),
        out_shape=jax.ShapeDtypeStruct((T, W), BF16),
        grid=(B, nt_),
        in_specs=[
            pl.BlockSpec((tt, RW_PAD_COLS), lambda b, i: (b * nt_ + i, 0)),
            full((1, RW_PAD_COLS)), full((1, W)), full((LANES, W)), full((1, W)), full((LANES, W)),
            full((GATE_LORA, W)), full((1, W)), full((1, W)), full((1, W)), full((1, W)), full((1, W)),
        ],
        out_specs=pl.BlockSpec((tt, W), lambda b, i: (b * nt_ + i, 0)),
        scratch_shapes=[
            pltpu.VMEM((1, RW_PAD_COLS), F32),
            pltpu.VMEM((npair, LANES, LANES), F32),
            sc(), sc(), sc(), sc(), sc(), sc(), sc(), sc(), sc(),
        ],
        compiler_params=_params(("parallel", "arbitrary")),
        name="rwkv7",
    )(rcols, mu_p, vec(w0), w2p, vec(a0), a2p, g2, vec(k_k), vec(k_a), vec(r_k), vec(ln_g), vec(ln_b))


def _outproj_kernel(ya_ref, yb_ref, yc_ref, x_ref, w_ref, gm_ref, g_ref, sh_ref, sc_ref, xo_ref, ho_ref):
    o1 = DIFF_WIDTH
    o2 = o1 + RWKV_WIDTH
    mix = (_mm(ya_ref[...], w_ref[0:o1, :]) + _mm(yb_ref[...], w_ref[o1:o2, :])
           + _mm(yc_ref[...], w_ref[o2:, :]))
    xn = x_ref[...] + gm_ref[...] * mix
    xo_ref[...] = xn
    ms = jnp.mean(xn * xn, axis=-1, keepdims=True)
    y = xn * lax.rsqrt(ms + RMS_EPS) * g_ref[...]
    ho_ref[...] = y * (1.0 + sc_ref[...]) + sh_ref[...]


def _out_proj(ya, yb, yc, x2, w_out, g, mod4, S):
    T, D = x2.shape
    tm = 512 if S % 512 == 0 else S
    nb = S // tm
    row = lambda i: (i, 0)
    modspec = lambda which: pl.BlockSpec((None, None, 1, D), lambda i: (i // nb, which, 0, 0))
    return pl.pallas_call(
        _outproj_kernel,
        out_shape=(jax.ShapeDtypeStruct((T, D), F32), jax.ShapeDtypeStruct((T, D), F32)),
        grid=(T // tm,),
        in_specs=[
            pl.BlockSpec((tm, DIFF_WIDTH), row), pl.BlockSpec((tm, RWKV_WIDTH), row), pl.BlockSpec((tm, FOX_WIDTH), row),
            pl.BlockSpec((tm, D), row),
            pl.BlockSpec((D, D), lambda i: (0, 0)),
            modspec(2),
            pl.BlockSpec((1, D), lambda i: (0, 0)),
            modspec(3), modspec(4),
        ],
        out_specs=(pl.BlockSpec((tm, D), row), pl.BlockSpec((tm, D), row)),
        compiler_params=_params(("parallel",)),
        name="out_proj",
    )(ya, yb, yc, x2, w_out.astype(BF16), mod4, g.reshape(1, D), mod4, mod4)


def _top16(s, iota_f, n):
    vals, poss = [], []
    for _ in range(PEER_TOPK):
        m = jnp.max(s, axis=0, keepdims=True)
        pos = jnp.min(jnp.where(s == m, iota_f, float(n)), axis=0, keepdims=True)
        vals.append(m)
        poss.append(pos)
        s = jnp.where(iota_f == pos, -jnp.inf, s)
    return jnp.concatenate(vals, axis=0), jnp.concatenate(poss, axis=0)


ROUTE_UNROLL = 4
PEER_NCAND = 56


def _peer_cand_tables():
    K = PEER_TOPK
    pairs = [(a, b) for a in range(K) for b in range(K) if (a + 1) * (b + 1) <= K]
    n = PEER_NCAND
    p0 = [[0.0] * K for _ in range(n)]
    p1 = [[0.0] * K for _ in range(n)]
    pad = [0.0] * n
    pos = [float(K * K + r) for r in range(n)]
    for r, (a, b) in enumerate(pairs):
        p0[r][a] = 1.0
        p1[r][b] = 1.0
        pos[r] = float(a * K + b)
    for r in range(len(pairs), n):
        pad[r] = -float("inf")
    col = lambda v: jnp.broadcast_to(jnp.asarray(v, F32)[:, None], (n, LANES))
    return jnp.asarray(p0, F32), jnp.asarray(p1, F32), col(pad), col(pos)


def _peer_route_kernel(h_ref, wq_ref, sk_ref, p0_ref, p1_ref, cpad_ref, cpos_ref, idx_ref, gate_ref, q_scr, e_scr,
                       g_scr):
    K = PEER_TOPK
    hb = h_ref[...].astype(BF16)
    q = _mm(hb, wq_ref[...])
    for hc in range(2 * PEER_HEADS):
        q_scr[hc] = q[:, LANES * hc:LANES * (hc + 1)].astype(BF16)
    iota_n = lax.broadcasted_iota(jnp.int32, (PEER_NKEYS, LANES), 0).astype(F32)
    cpos = cpos_ref[...]

    def one_head(h):
        sv0, si0 = _top16(_nt(sk_ref[2 * h], q_scr[2 * h]), iota_n, PEER_NKEYS)
        sv1, si1 = _top16(_nt(sk_ref[2 * h + 1], q_scr[2 * h + 1]), iota_n, PEER_NKEYS)
        cand = _mm(p0_ref[...], sv0, HIGHEST) + _mm(p1_ref[...], sv1, HIGHEST) + cpad_ref[...]
        cidx = _mm(p0_ref[...], si0) * float(PEER_NKEYS) + _mm(p1_ref[...], si1)
        fv, es = [], []
        for _ in range(K):
            m = jnp.max(cand, axis=0, keepdims=True)
            pos = jnp.min(jnp.where(cand == m, cpos, float(2 * K * K)), axis=0, keepdims=True)
            hit = cpos == pos
            fv.append(m)
            es.append(jnp.max(jnp.where(hit, cidx, -1.0), axis=0, keepdims=True))
            cand = jnp.where(hit, -jnp.inf, cand)
        fv = jnp.concatenate(fv, axis=0)
        ex = jnp.exp(fv - fv[0:1, :])
        g_scr[h] = ex / jnp.sum(ex, axis=0, keepdims=True)
        e_scr[h] = jnp.concatenate(es, axis=0)

    def heads(hh, carry):
        for j in range(ROUTE_UNROLL):
            one_head(hh * ROUTE_UNROLL + j)
        return carry

    lax.fori_loop(0, PEER_HEADS // ROUTE_UNROLL, heads, 0)
    e = e_scr[...].reshape(PEER_HEADS * K, LANES)
    idx_ref[...] = e.T.astype(jnp.int32)
    gate_ref[...] = g_scr[...].reshape(PEER_HEADS * K, LANES)


def _peer_route(h2, wq, subkeys):
    T, D = h2.shape
    tm = LANES
    nq = 2 * PEER_HEADS
    sk = subkeys.reshape(nq, PEER_NKEYS, PEER_HALF).astype(BF16)
    p0, p1, cpad, cpos = _peer_cand_tables()
    const = lambda shape: pl.BlockSpec(shape, lambda i: (0,) * len(shape))
    return pl.pallas_call(
        _peer_route_kernel,
        out_shape=(jax.ShapeDtypeStruct((T, PEER_HEADS * PEER_TOPK), jnp.int32),
                   jax.ShapeDtypeStruct((T // tm, PEER_HEADS * PEER_TOPK, tm), F32)),
        grid=(T // tm,),
        in_specs=[
            pl.BlockSpec((tm, D), lambda i: (i, 0)),
            const((D, nq * PEER_HALF)),
            const((nq, PEER_NKEYS, PEER_HALF)),
            const((PEER_NCAND, PEER_TOPK)), const((PEER_NCAND, PEER_TOPK)),
            const((PEER_NCAND, LANES)), const((PEER_NCAND, LANES)),
        ],
        out_specs=(pl.BlockSpec((tm, PEER_HEADS * PEER_TOPK), lambda i: (i, 0)),
                   pl.BlockSpec((None, PEER_HEADS * PEER_TOPK, tm), lambda i: (i, 0, 0))),
        scratch_shapes=[
            pltpu.VMEM((nq, tm, PEER_HALF), BF16),
            pltpu.VMEM((PEER_HEADS, PEER_TOPK, tm), F32),
            pltpu.VMEM((PEER_HEADS, PEER_TOPK, tm), F32),
        ],
        compiler_params=_params(("parallel",)),
        name="peer_route",
    )(h2, wq.astype(BF16), sk, p0, p1, cpad, cpos)


PEER_G = 16
PEER_SLOTS = PEER_HEADS * PEER_TOPK


def _peer_eval_kernel(idx_ref, idxn_ref, gate_ref, h_ref, x_ref, gf_ref, fg_ref, uv_ref, o_ref, buf, sem, *, final):
    G = PEER_G
    R = G * PEER_SLOTS
    D = D_MODEL
    tiles = PEER_SLOTS // SUBLANES
    i = pl.program_id(0)
    n = pl.num_programs(0)

    def start(ids, off, s, t, u):
        pltpu.make_async_copy(uv_ref.at[ids[off + t * SUBLANES + u]], buf.at[s, t, pl.ds(u, 1), :],
                              sem.at[s]).start(priority=u % 2)

    def wait(s):
        pltpu.make_async_copy(buf.at[s], buf.at[s], sem.at[s]).wait()

    @pl.when(i == 0)
    def _():
        def body(t, carry):
            for u in range(SUBLANES):
                start(idx_ref, 0, 0, t, u)
            return carry
        lax.fori_loop(0, R // SUBLANES, body, 0)

    lane = lax.broadcasted_iota(jnp.int32, (1, LANES), 1)
    tbase = (i % (LANES // (2 * G))) * (2 * G)
    gate = gate_ref[...]
    outs = []
    for s in range(2):
        wait(s)
        nxt_ids, nxt_off = (idx_ref, R) if s == 0 else (idxn_ref, 0)
        for g in range(G):
            for t in range(tiles * g, tiles * (g + 1)):
                for u in range(SUBLANES):
                    start(nxt_ids, nxt_off, 1 - s, t, u)
            w_rows = buf[s, tiles * g:tiles * (g + 1)].reshape(PEER_SLOTS, D)
            u_rows = lax.bitcast_convert_type(w_rows & jnp.uint32(0xFFFF0000), F32)
            prod = u_rows * h_ref[G * s + g:G * s + g + 1, :]
            part = prod[:, 0:LANES]
            for c in range(1, D // LANES):
                part = part + prod[:, LANES * c:LANES * (c + 1)]
            act = jnp.sum(part, axis=1, keepdims=True)
            gcol = jnp.sum(jnp.where(lane == tbase + G * s + g, gate, 0.0), axis=1, keepdims=True)
            coef = gcol * (0.5 * act * (1.0 + lax.erf(act * (2.0 ** -0.5))))
            v_rows = lax.bitcast_convert_type(w_rows << 16, F32)
            outs.append(jnp.sum(v_rows * coef, axis=0, keepdims=True))
    xn = x_ref[...] + gf_ref[...] * jnp.concatenate(outs, axis=0)
    if final:
        ms = jnp.mean(xn * xn, axis=-1, keepdims=True)
        xn = xn * lax.rsqrt(ms + RMS_EPS) * fg_ref[...]
    o_ref[...] = xn

    @pl.when(i == n - 1)
    def _():
        wait(0)


def _peer_eval(eidx, gate_t, h2, x2, mod4, final_g, uv, S, final, ntok=None):
    T, D = x2.shape
    T = T if ntok is None else ntok
    G = PEER_G
    R = G * PEER_SLOTS
    n = T // (2 * G)
    return pl.pallas_call(
        functools.partial(_peer_eval_kernel, final=final),
        out_shape=jax.ShapeDtypeStruct((T, D), F32),
        grid=(n,),
        in_specs=[
            pl.BlockSpec((2 * R,), lambda i: (i,), memory_space=pltpu.SMEM),
            pl.BlockSpec((R,), lambda i: (jnp.minimum(2 * i + 2, 2 * n - 2),), memory_space=pltpu.SMEM),
            pl.BlockSpec((None, PEER_SLOTS, LANES), lambda i: (i // (LANES // (2 * G)), 0, 0)),
            pl.BlockSpec((2 * G, D), lambda i: (i, 0)),
            pl.BlockSpec((2 * G, D), lambda i: (i, 0)),
            pl.BlockSpec((None, None, 1, D), lambda i: (i // (S // (2 * G)), 5, 0, 0)),
            pl.BlockSpec((1, D), lambda i: (0, 0)),
            pl.BlockSpec(memory_space=pl.ANY),
        ],
        out_specs=pl.BlockSpec((2 * G, D), lambda i: (i, 0)),
        scratch_shapes=[pltpu.VMEM((2, R // SUBLANES, SUBLANES, D), jnp.uint32), pltpu.SemaphoreType.DMA((2,))],
        compiler_params=_params(("arbitrary",)),
        name="peer_eval",
    )(eidx.reshape(-1), eidx.reshape(-1), gate_t, h2, x2, mod4, final_g.reshape(1, D), uv)


SC_WORKERS = 32
SC_WINDOW = 128
SC_ROWS = 64
PEER_GD = 16
SC_SHARE_NUM, SC_SHARE_DEN = 1, 2


def _sc_gather(tab, idx):
    n = idx.shape[0]
    width = tab.shape[1]
    per = n // SC_WORKERS
    mesh = plsc.VectorSubcoreMesh(core_axis_name="core", subcore_axis_name="subcore")

    @pl.kernel(out_type=jax.ShapeDtypeStruct((n, width), tab.dtype), mesh=mesh,
               scratch_types=[pltpu.VMEM((SC_WINDOW,), jnp.int32), pltpu.VMEM((SC_ROWS, width), tab.dtype)])
    def gather(x_hbm, i_hbm, o_hbm, idx_v, rows_v):
        wid = lax.axis_index("core") * (SC_WORKERS // 2) + lax.axis_index("subcore")

        @pl.loop(0, per // SC_WINDOW)
        def _(w):
            base = wid * per + w * SC_WINDOW
            pltpu.sync_copy(i_hbm.at[pl.ds(base, SC_WINDOW)], idx_v)
            for k in range(SC_WINDOW // SC_ROWS):
                pltpu.sync_copy(x_hbm.at[idx_v.at[pl.ds(k * SC_ROWS, SC_ROWS)]], rows_v)
                pltpu.sync_copy(rows_v, o_hbm.at[pl.ds(base + k * SC_ROWS, SC_ROWS)])

    return gather(tab, idx)


def _peer_dense_kernel(rows_ref, gate_ref, h_ref, x_ref, gf_ref, fg_ref, o_ref, *, final, blk0):
    G = PEER_GD
    D = D_MODEL
    i = pl.program_id(0)
    lane = lax.broadcasted_iota(jnp.int32, (1, LANES), 1)
    tbase = ((blk0 + i) % (LANES // G)) * G
    gate = gate_ref[...]
    outs = []
    for g in range(G):
        w_rows = rows_ref[PEER_SLOTS * g:PEER_SLOTS * (g + 1), :]
        u_rows = lax.bitcast_convert_type(w_rows & jnp.uint32(0xFFFF0000), F32)
        prod = u_rows * h_ref[g:g + 1, :]
        part = prod[:, 0:LANES]
        for c in range(1, D // LANES):
            part = part + prod[:, LANES * c:LANES * (c + 1)]
        act = jnp.sum(part, axis=1, keepdims=True)
        gcol = jnp.sum(jnp.where(lane == tbase + g, gate, 0.0), axis=1, keepdims=True)
        coef = gcol * (0.5 * act * (1.0 + lax.erf(act * (2.0 ** -0.5))))
        v_rows = lax.bitcast_convert_type(w_rows << 16, F32)
        outs.append(jnp.sum(v_rows * coef, axis=0, keepdims=True))
    xn = x_ref[...] + gf_ref[...] * jnp.concatenate(outs, axis=0)
    if final:
        ms = jnp.mean(xn * xn, axis=-1, keepdims=True)
        xn = xn * lax.rsqrt(ms + RMS_EPS) * fg_ref[...]
    o_ref[...] = xn


def _peer_eval_dense(rows, gate_t, h2, x2, mod4, final_g, S, final, tok0):
    T, D = x2.shape
    G = PEER_GD
    tb = rows.shape[0] // PEER_SLOTS
    blk0 = tok0 // G
    return pl.pallas_call(
        functools.partial(_peer_dense_kernel, final=final, blk0=blk0),
        out_shape=jax.ShapeDtypeStruct((tb, D), F32),
        grid=(tb // G,),
        in_specs=[
            pl.BlockSpec((G * PEER_SLOTS, D), lambda i: (i, 0)),
            pl.BlockSpec((None, PEER_SLOTS, LANES), lambda i: ((blk0 + i) // (LANES // G), 0, 0)),
            pl.BlockSpec((G, D), lambda i: (blk0 + i, 0)),
            pl.BlockSpec((G, D), lambda i: (blk0 + i, 0)),
            pl.BlockSpec((None, None, 1, D), lambda i: ((blk0 + i) // (S // G), 5, 0, 0)),
            pl.BlockSpec((1, D), lambda i: (0, 0)),
        ],
        out_specs=pl.BlockSpec((G, D), lambda i: (i, 0)),
        compiler_params=_params(("parallel",)),
        name="peer_dense",
    )(rows, gate_t, h2, x2, mod4, final_g.reshape(1, D))


def _pack_uv(u, v):
    hi = lax.bitcast_convert_type(u.astype(BF16), jnp.uint16).astype(jnp.uint32) << 16
    lo = lax.bitcast_convert_type(v.astype(BF16), jnp.uint16).astype(jnp.uint32)
    return (hi | lo).reshape(u.shape[0], 1, u.shape[1])


def kernel(x, c, norm_mix_g, norm_ffn_g, final_norm_g, ada_w, ada_b, w_in, w_out, dif_lam, dif_subln_g, rw_mu, rw_w0,
           rw_w2, rw_a0, rw_a2, rw_g2, rw_kk, rw_ka, rw_rk, rw_ln_g, rw_ln_b, fox_bf, peer_wq, peer_subkeys, peer_u,
           peer_v):
    B, S, D = x.shape
    T = B * S
    depth = ada_w.shape[0]
    x2 = x.reshape(T, D)
    mod = _ada_mod(c, ada_w, ada_b)
    for l in range(depth):
        mod4 = mod[l].reshape(B, 6, 1, D)
        w_pad, w_vt = _pad_w_in(w_in[l])
        dqk, rcols, fqk, fl, vtd, vtf = _in_proj(x2, norm_mix_g[l], mod4, w_pad, w_vt, S)
        ya = _diff_attention(dqk, vtd, dif_lam[l], dif_subln_g[l], l, B, S)
        yb = _rwkv(rcols, rw_mu[l], rw_w0[l], rw_w2[l], rw_a0[l], rw_a2[l], rw_g2[l], rw_kk[l], rw_ka[l],
                   rw_rk[l].reshape(-1), rw_ln_g[l], rw_ln_b[l], B, S, prec=None)
        crep, crow = _fox_cum(fl, fox_bf[l], B, S)
        yc = _fox_attention(fqk, vtf, crep, crow, B, S)
        x2, h2 = _out_proj(ya, yb, yc, x2, w_out[l], norm_ffn_g[l], mod4, S)
        eidx, gate_t = _peer_route(h2, peer_wq[l], peer_subkeys[l])
        uv = _pack_uv(peer_u[l], peer_v[l])
        final = l == depth - 1
        tb = (T * SC_SHARE_NUM // SC_SHARE_DEN) // (SC_WORKERS * SC_WINDOW) * (SC_WORKERS * SC_WINDOW)
        ta = T - tb
        rows_b = _sc_gather(uv.reshape(-1, D), eidx.reshape(-1)[ta * PEER_SLOTS:])
        xa = _peer_eval(eidx, gate_t, h2, x2, mod4, final_norm_g, uv, S, final, ntok=ta)
        xb = _peer_eval_dense(rows_b, gate_t, h2, x2, mod4, final_norm_g, S, final, ta)
        x2 = jnp.concatenate([xa, xb], axis=0)
    return x2.reshape(B, S, D)
```

```python
import functools
import math

import jax
import jax.numpy as jnp
from jax import lax
from jax.experimental import pallas as pl
from jax.experimental.pallas import tpu as pltpu
from jax.experimental.pallas import tpu_sc as plsc

F32 = jnp.float32
BF16 = jnp.bfloat16
HIGHEST = lax.Precision.HIGHEST

D_MODEL = 1024
HEAD_DIM = 64
DIFF_HEADS = 6
DIFF_QK_DIM = HEAD_DIM // 2
RWKV_HEADS = 6
FOX_HEADS = 4
DIFF_WIDTH = DIFF_HEADS * HEAD_DIM
RWKV_WIDTH = RWKV_HEADS * HEAD_DIM
FOX_WIDTH = FOX_HEADS * HEAD_DIM
DECAY_LORA = 64
AAA_LORA = 64
GATE_LORA = 128
DIFF_COLS = 3 * DIFF_WIDTH
RWKV_COLS = 3 * RWKV_WIDTH + DECAY_LORA + AAA_LORA + GATE_LORA
PEER_HEADS = 8
PEER_NKEYS = 128
PEER_TOPK = 16
PEER_QDIM = 256
PEER_HALF = PEER_QDIM // 2
RMS_EPS = 1e-6
RWKV_LN_EPS = 64e-5

LANES = 128
SUBLANES = 8
RW_PAD_COLS = 3 * RWKV_WIDTH + 3 * LANES
VMEM_LIMIT = 56 * 1024 * 1024

RW_CHUNK = 64
RW_CHUNKS_PER_ITER = 2


def _params(sem, vmem=VMEM_LIMIT):
    return pltpu.CompilerParams(dimension_semantics=sem, vmem_limit_bytes=vmem)


def _nt(a, b, precision=None):
    return lax.dot_general(a, b, (((1,), (1,)), ((), ())), preferred_element_type=F32, precision=precision)


def _mm(a, b, precision=None):
    return jnp.dot(a, b, preferred_element_type=F32, precision=precision)


def _ada_kernel(c_ref, w_ref, b_ref, o_ref):
    c = c_ref[...]
    ca = c * jax.nn.sigmoid(c)
    o_ref[...] = _mm(ca, w_ref[...], HIGHEST) + b_ref[...]


def _ada_mod(c, ada_w, ada_b):
    L, D, N = ada_w.shape
    B = c.shape[0]
    tn = 1536
    return pl.pallas_call(
        _ada_kernel,
        out_shape=jax.ShapeDtypeStruct((L, B, N), F32),
        grid=(L, N // tn),
        in_specs=[
            pl.BlockSpec((B, D), lambda l, j: (0, 0)),
            pl.BlockSpec((None, D, tn), lambda l, j: (l, 0, j)),
            pl.BlockSpec((None, 1, tn), lambda l, j: (l, 0, j)),
        ],
        out_specs=pl.BlockSpec((None, B, tn), lambda l, j: (l, 0, j)),
        compiler_params=_params(("parallel", "parallel")),
        name="ada_mod",
    )(c, ada_w, ada_b.reshape(L, 1, N))


ATT_T = 512
QK_DIFF = 2 * DIFF_WIDTH
QK_FOX = 2 * FOX_WIDTH
VT_ROWS = DIFF_WIDTH + FOX_WIDTH
IN_PAD_COLS = QK_DIFF + RW_PAD_COLS + QK_FOX + LANES


def _inproj_kernel(x_ref, g_ref, sh_ref, sc_ref, w_ref, wvt_ref, d_ref, r_ref, f_ref, fl_ref, vtd_ref, vtf_ref, *, ta):
    x = x_ref[...]
    ms = jnp.mean(x * x, axis=-1, keepdims=True)
    y = x * lax.rsqrt(ms + RMS_EPS) * g_ref[...]
    h = (y * (1.0 + sc_ref[...]) + sh_ref[...]).astype(BF16)
    o1 = QK_DIFF
    o2 = o1 + RW_PAD_COLS
    o3 = o2 + QK_FOX
    d_ref[...] = _mm(h, w_ref[:, 0:o1]).astype(BF16)
    r_ref[...] = _mm(h, w_ref[:, o1:o2])
    f_ref[...] = _mm(h, w_ref[:, o2:o3]).astype(BF16)
    fl_ref[...] = _mm(h, w_ref[:, o3:o3 + LANES])
    vt = _nt(wvt_ref[...], h).astype(BF16)
    for s in range(x.shape[0] // ta):
        vtd_ref[s] = vt[0:DIFF_WIDTH, ta * s:ta * (s + 1)]
        vtf_ref[s] = vt[DIFF_WIDTH:, ta * s:ta * (s + 1)]


def _in_proj(x2, g, mod4, w_pad, w_vt, S):
    T, D = x2.shape
    tm = 512 if S % 512 == 0 else S
    ta = min(ATT_T, S)
    nb = S // tm
    row = lambda i: (i, 0)
    return pl.pallas_call(
        functools.partial(_inproj_kernel, ta=ta),
        out_shape=(
            jax.ShapeDtypeStruct((T, QK_DIFF), BF16),
            jax.ShapeDtypeStruct((T, RW_PAD_COLS), F32),
            jax.ShapeDtypeStruct((T, QK_FOX), BF16),
            jax.ShapeDtypeStruct((T, LANES), F32),
            jax.ShapeDtypeStruct((T // ta, DIFF_WIDTH, ta), BF16),
            jax.ShapeDtypeStruct((T // ta, FOX_WIDTH, ta), BF16),
        ),
        grid=(T // tm,),
        in_specs=[
            pl.BlockSpec((tm, D), row),
            pl.BlockSpec((1, D), lambda i: (0, 0)),
            pl.BlockSpec((None, None, 1, D), lambda i: (i // nb, 0, 0, 0)),
            pl.BlockSpec((None, None, 1, D), lambda i: (i // nb, 1, 0, 0)),
            pl.BlockSpec((D, IN_PAD_COLS), lambda i: (0, 0)),
            pl.BlockSpec((VT_ROWS, D), lambda i: (0, 0)),
        ],
        out_specs=(
            pl.BlockSpec((tm, QK_DIFF), row),
            pl.BlockSpec((tm, RW_PAD_COLS), row),
            pl.BlockSpec((tm, QK_FOX), row),
            pl.BlockSpec((tm, LANES), row),
            pl.BlockSpec((tm // ta, DIFF_WIDTH, ta), lambda i: (i, 0, 0)),
            pl.BlockSpec((tm // ta, FOX_WIDTH, ta), lambda i: (i, 0, 0)),
        ),
        compiler_params=_params(("parallel",)),
        name="in_proj",
    )(x2, g.reshape(1, D), mod4, mod4, w_pad, w_vt)


def _pad_w_in(w_in):
    D = w_in.shape[0]
    W = RWKV_WIDTH
    o = DIFF_COLS
    z64 = jnp.zeros((D, LANES - DECAY_LORA), w_in.dtype)
    rw = w_in[:, o:o + RWKV_COLS]
    fx = w_in[:, o + RWKV_COLS:]
    zf = jnp.zeros((D, LANES - FOX_HEADS), w_in.dtype)
    w_pad = jnp.concatenate([
        w_in[:, :QK_DIFF],
        rw[:, :3 * W], rw[:, 3 * W:3 * W + DECAY_LORA], z64,
        rw[:, 3 * W + DECAY_LORA:3 * W + DECAY_LORA + AAA_LORA], z64,
        rw[:, 3 * W + DECAY_LORA + AAA_LORA:],
        fx[:, :QK_FOX], fx[:, 3 * FOX_WIDTH:], zf,
    ], axis=1).astype(BF16)
    w_vt = jnp.concatenate([w_in[:, QK_DIFF:o], fx[:, QK_FOX:3 * FOX_WIDTH]], axis=1).T.astype(BF16)
    return w_pad, w_vt


LOG2E = math.log2(math.e)


def _flash_step(s2s, vt, m_ref, l_ref, acc_ref):
    n = len(s2s)
    m_old = [m_ref[x] for x in range(n)]
    m_new = [jnp.maximum(m_old[x], jnp.max(s2s[x], axis=0, keepdims=True)) for x in range(n)]
    alpha = [jnp.exp2(m_old[x] - m_new[x]) for x in range(n)]
    p = [jnp.exp2(s2s[x] - m_new[x]) for x in range(n)]
    pv = [_mm(vt, p[x].astype(BF16)) for x in range(n)]
    for x in range(n):
        l_ref[x] = alpha[x] * l_ref[x] + jnp.sum(p[x], axis=0, keepdims=True)
        acc_ref[x] = alpha[x] * acc_ref[x] + pv[x]
        m_ref[x] = m_new[x]


def _diff_attn_kernel(lam_ref, g_ref, q_ref, k_ref, vt_ref, o_ref, m_ref, l_ref, acc_ref, *, tq, lam_init):
    i = pl.program_id(2)
    c = (DIFF_QK_DIM ** -0.5) * LOG2E
    lane = lax.broadcasted_iota(jnp.int32, (1, LANES), 1)
    q = q_ref[...]
    zero = jnp.zeros_like(q)
    qm = [jnp.where((lane >= DIFF_QK_DIM * x) & (lane < DIFF_QK_DIM * (x + 1)), q, zero) for x in range(4)]
    m_ref[...] = jnp.full(m_ref.shape, -jnp.inf, F32)
    l_ref[...] = jnp.zeros(l_ref.shape, F32)
    acc_ref[...] = jnp.zeros(acc_ref.shape, F32)

    def step(j, diag):
        k = k_ref[pl.ds(pl.multiple_of(j * tq, tq), tq), :]
        vt = vt_ref[j]
        if diag:
            keep = (lax.broadcasted_iota(jnp.int32, (tq, tq), 1) >= lax.broadcasted_iota(jnp.int32, (tq, tq), 0))
        s2s = [_nt(k, qm[x]) * c for x in range(4)]
        if diag:
            s2s = [jnp.where(keep, s2, -jnp.inf) for s2 in s2s]
        _flash_step(s2s, vt, m_ref, l_ref, acc_ref)

    def body(j, carry):
        step(j, False)
        return carry

    lax.fori_loop(0, i, body, 0)
    step(i, True)

    lp = lam_ref[...]
    lam = (jnp.exp(jnp.sum(lp[0:1] * lp[1:2], axis=-1, keepdims=True))
           - jnp.exp(jnp.sum(lp[2:3] * lp[3:4], axis=-1, keepdims=True)) + lam_init)
    outs = [acc_ref[2 * hh] / l_ref[2 * hh] - lam * (acc_ref[2 * hh + 1] / l_ref[2 * hh + 1]) for hh in range(2)]
    row = lax.broadcasted_iota(jnp.int32, (LANES, 1), 0)
    o = jnp.where(row < HEAD_DIM, outs[0], outs[1])
    sq = o * o
    ms = jnp.where(row < HEAD_DIM, jnp.sum(sq[0:HEAD_DIM], axis=0, keepdims=True),
                   jnp.sum(sq[HEAD_DIM:], axis=0, keepdims=True)) * (1.0 / HEAD_DIM)
    y = o * lax.rsqrt(ms + RMS_EPS) * g_ref[...] * (1.0 - lam_init)
    o_ref[...] = y.T.astype(o_ref.dtype)


def _diff_attention(dqk, vtd, lam_params, subln_g, layer_idx, B, S):
    T = B * S
    tq = min(ATT_T, S)
    nq = S // tq
    npair = DIFF_HEADS // 2
    lam_init = 0.8 - 0.6 * math.exp(-0.3 * layer_idx)
    g2 = jnp.concatenate([subln_g, subln_g]).reshape(LANES, 1).astype(F32)
    return pl.pallas_call(
        functools.partial(_diff_attn_kernel, tq=tq, lam_init=lam_init),
        out_shape=jax.ShapeDtypeStruct((T, DIFF_WIDTH), BF16),
        grid=(B, npair, nq),
        in_specs=[
            pl.BlockSpec((4, DIFF_QK_DIM), lambda b, p, i: (0, 0)),
            pl.BlockSpec((LANES, 1), lambda b, p, i: (0, 0)),
            pl.BlockSpec((tq, LANES), lambda b, p, i: (b * nq + i, p)),
            pl.BlockSpec((S, LANES), lambda b, p, i: (b, npair + p)),
            pl.BlockSpec((nq, LANES, tq), lambda b, p, i: (b, p, 0)),
        ],
        out_specs=pl.BlockSpec((tq, LANES), lambda b, p, i: (b * nq + i, p)),
        scratch_shapes=[
            pltpu.VMEM((4, 1, tq), F32),
            pltpu.VMEM((4, 1, tq), F32),
            pltpu.VMEM((4, LANES, tq), F32),
        ],
        compiler_params=_params(("parallel", "parallel", "arbitrary")),
        name="diff_attn",
    )(lam_params, g2, dqk, dqk, vtd)


def _fox_cum_kernel(f_ref, b_ref, rep_ref, row_ref, *, S, tc):
    rr = lax.broadcasted_iota(jnp.int32, (tc, tc), 0)
    cc = lax.broadcasted_iota(jnp.int32, (tc, tc), 1)
    tri = (rr >= cc).astype(F32)
    sel_r = lax.broadcasted_iota(jnp.int32, (LANES, LANES), 0)
    carry = jnp.zeros((1, LANES), F32)
    for c in range(S // tc):
        z = f_ref[c * tc:(c + 1) * tc, :] + b_ref[...]
        logf = -(jnp.maximum(-z, 0.0) + jnp.log(1.0 + jnp.exp(-jnp.abs(z))))
        cum = _mm(tri, logf, HIGHEST) + carry
        carry = cum[tc - 1:tc, :]
        row_ref[:, c * tc:(c + 1) * tc] = cum.T[0:8, :]
        for h in range(FOX_HEADS):
            rep_ref[h, c * tc:(c + 1) * tc, :] = _mm(cum, (sel_r == h).astype(F32), HIGHEST)


def _fox_cum(fl, b_f, B, S):
    tc = 256 if S % 256 == 0 else S
    bpad = jnp.zeros((1, LANES), F32).at[0, :FOX_HEADS].set(b_f.astype(F32))
    return pl.pallas_call(
        functools.partial(_fox_cum_kernel, S=S, tc=tc),
        out_shape=(jax.ShapeDtypeStruct((B, FOX_HEADS, S, LANES), F32), jax.ShapeDtypeStruct((B, 8, S), F32)),
        grid=(B,),
        in_specs=[pl.BlockSpec((S, LANES), lambda b: (b, 0)), pl.BlockSpec((1, LANES), lambda b: (0, 0))],
        out_specs=(pl.BlockSpec((None, FOX_HEADS, S, LANES), lambda b: (b, 0, 0, 0)),
                   pl.BlockSpec((None, 8, S), lambda b: (b, 0, 0))),
        compiler_params=_params(("parallel",)),
        name="fox_cum",
    )(fl, bpad)


def _fox_attn_kernel(q_ref, k_ref, vt_ref, c0_ref, c1_ref, cr_ref, o_ref, m_ref, l_ref, acc_ref, *, tq):
    p_id = pl.program_id(1)
    i = pl.program_id(2)
    c = (HEAD_DIM ** -0.5) * LOG2E
    lane = lax.broadcasted_iota(jnp.int32, (1, LANES), 1)
    q = q_ref[...]
    zero = jnp.zeros_like(q)
    qm = [jnp.where((lane >= HEAD_DIM * x) & (lane < HEAD_DIM * (x + 1)), q, zero) for x in range(2)]
    ck_refs = (c0_ref, c1_ref)
    cq = [cr_ref[2 * p_id + x, pl.ds(i, 1), :] for x in range(2)]
    m_ref[...] = jnp.full(m_ref.shape, -jnp.inf, F32)
    l_ref[...] = jnp.zeros(l_ref.shape, F32)
    acc_ref[...] = jnp.zeros(acc_ref.shape, F32)

    def step(j, diag):
        off = pl.multiple_of(j * tq, tq)
        k = k_ref[pl.ds(off, tq), :]
        vt = vt_ref[j]
        if diag:
            keep = (lax.broadcasted_iota(jnp.int32, (tq, tq), 1) >= lax.broadcasted_iota(jnp.int32, (tq, tq), 0))
        s2s = []
        for x in range(2):
            ck = ck_refs[x][pl.ds(off, tq), :]
            bias = (cq[x] - jnp.concatenate([ck] * (tq // LANES), axis=1)) * LOG2E
            s2s.append(_nt(k, qm[x]) * c + bias)
        if diag:
            s2s = [jnp.where(keep, s2, -jnp.inf) for s2 in s2s]
        _flash_step(s2s, vt, m_ref, l_ref, acc_ref)

    def body(j, carry):
        step(j, False)
        return carry

    lax.fori_loop(0, i, body, 0)
    step(i, True)
    row = lax.broadcasted_iota(jnp.int32, (LANES, 1), 0)
    o = jnp.where(row < HEAD_DIM, acc_ref[0] / l_ref[0], acc_ref[1] / l_ref[1])
    o_ref[...] = o.T.astype(o_ref.dtype)


def _fox_attention(fqk, vtf, crep, crow, B, S):
    T = B * S
    tq = min(ATT_T, S)
    nq = S // tq
    npair = FOX_HEADS // 2
    crow4 = crow.reshape(B, 8, nq, tq)
    rep = lambda x: pl.BlockSpec((None, None, S, LANES), lambda b, p, i: (b, 2 * p + x, 0, 0))
    return pl.pallas_call(
        functools.partial(_fox_attn_kernel, tq=tq),
        out_shape=jax.ShapeDtypeStruct((T, FOX_WIDTH), BF16),
        grid=(B, npair, nq),
        in_specs=[
            pl.BlockSpec((tq, LANES), lambda b, p, i: (b * nq + i, p)),
            pl.BlockSpec((S, LANES), lambda b, p, i: (b, npair + p)),
            pl.BlockSpec((nq, LANES, tq), lambda b, p, i: (b, p, 0)),
            rep(0), rep(1),
            pl.BlockSpec((None, 8, nq, tq), lambda b, p, i: (b, 0, 0, 0)),
        ],
        out_specs=pl.BlockSpec((tq, LANES), lambda b, p, i: (b * nq + i, p)),
        scratch_shapes=[
            pltpu.VMEM((2, 1, tq), F32),
            pltpu.VMEM((2, 1, tq), F32),
            pltpu.VMEM((2, LANES, tq), F32),
        ],
        compiler_params=_params(("parallel", "parallel", "arbitrary")),
        name="fox_attn",
    )(fqk, fqk, vtf, crep, crep, crow4)


def _split3(x):
    hi = x.astype(BF16)
    r1 = x - hi.astype(F32)
    mid = r1.astype(BF16)
    lo = (r1 - mid.astype(F32)).astype(BF16)
    return hi, mid, lo


def _seg_sum(x, seg, npair):
    parts = _split3(x)
    return jnp.concatenate(
        [sum(_mm(t[:, LANES * p:LANES * (p + 1)], seg) for t in parts) for p in range(npair)], axis=1)


def _rwkv_kernel(x_ref, mu_ref, w0_ref, w2_ref, a0_ref, a2_ref, g2_ref, kk_ref, ka_ref, rk_ref, lng_ref, lnb_ref,
                 o_ref, carry_ref, st_ref, at_ref, rt_ref, bt_ref, kt_ref, v_ref, wc_ref, y_ref, g_ref, bon_ref,
                 *, tt, prec):
    i = pl.program_id(1)
    W = RWKV_WIDTH
    C = RW_CHUNK
    CPI = RW_CHUNKS_PER_ITER
    npair = RWKV_HEADS // 2

    @pl.when(i == 0)
    def _():
        carry_ref[...] = jnp.zeros(carry_ref.shape, F32)
        st_ref[...] = jnp.zeros(st_ref.shape, F32)

    x = x_ref[...]
    rows = lax.broadcasted_iota(jnp.int32, (tt, 1), 0)
    prev = jnp.where(rows == 0, carry_ref[...], pltpu.roll(x, 1, axis=0))
    carry_ref[...] = x[tt - 1:tt, :]
    xs = x + (prev - x) * mu_ref[...]
    r = xs[:, 0:W]
    k = xs[:, W:2 * W]
    v = xs[:, 2 * W:3 * W]
    xw = xs[:, 3 * W:3 * W + LANES]
    xa = xs[:, 3 * W + LANES:3 * W + 2 * LANES]
    xg = xs[:, 3 * W + 2 * LANES:]
    wl = w0_ref[...] + _mm(jnp.tanh(xw), w2_ref[...], HIGHEST)
    w = -(jnp.maximum(-wl, 0.0) + jnp.log(1.0 + jnp.exp(-jnp.abs(wl)))) - 0.5
    logdec = -jnp.exp(w)
    a = jax.nn.sigmoid(a0_ref[...] + _mm(xa, a2_ref[...], HIGHEST))
    g_ref[...] = _mm(jax.nn.sigmoid(xg), g2_ref[...], HIGHEST)

    r_i = lax.broadcasted_iota(jnp.int32, (LANES, LANES), 0)
    c_i = lax.broadcasted_iota(jnp.int32, (LANES, LANES), 1)
    seg = ((r_i // HEAD_DIM) == (c_i // HEAD_DIM)).astype(BF16)
    kkv = k * kk_ref[...]
    kkn = kkv / jnp.maximum(jnp.sqrt(_seg_sum(kkv * kkv, seg, npair)), 1e-12)
    k2 = k * (1.0 + (a - 1.0) * ka_ref[...])
    bon_ref[...] = _seg_sum(r * k2 * rk_ref[...], seg, npair) * v

    rt_i = lax.broadcasted_iota(jnp.int32, (tt, tt), 0)
    ct_i = lax.broadcasted_iota(jnp.int32, (tt, tt), 1)
    tri = (((rt_i // C) == (ct_i // C)) & (rt_i >= ct_i)).astype(BF16)
    cum = sum(_mm(tri, t) for t in _split3(logdec))
    winv = jnp.exp(-cum)
    wcum = jnp.exp(cum)
    at_ref[...] = -kkn * jnp.exp(cum - logdec)
    bt_ref[...] = kkn * a * winv
    kt_ref[...] = k2 * winv
    rt_ref[...] = r * wcum
    v_ref[...] = v
    wc_ref[...] = wcum

    lane = lax.broadcasted_iota(jnp.int32, (1, LANES), 1)
    lo = lane < HEAD_DIM
    tpos = r_i % C
    ipos = c_i % C
    strict = tpos > ipos
    incl = tpos >= ipos
    eye = r_i == c_i

    def stack2(m):
        return jnp.concatenate([jnp.where(lo, m, 0.0), jnp.where(lo, 0.0, m)], axis=0)

    def cast(m):
        return m if prec is not None else m.astype(BF16)

    def mm(p, q):
        return _mm(cast(p), cast(q), prec)

    def nt(p, q):
        return _nt(cast(p), cast(q), prec)

    def chunks(cc, carry):
        units = [(ci, p) for ci in range(CPI) for p in range(npair)]
        r0 = [pl.multiple_of((cc * CPI + ci) * C, C) for ci in range(CPI)]
        ld = lambda ref, ci, p: stack2(ref[pl.ds(r0[ci], C), LANES * p:LANES * (p + 1)])
        a2 = {u: ld(at_ref, *u) for u in units}
        r2 = {u: ld(rt_ref, *u) for u in units}
        b2 = {u: ld(bt_ref, *u) for u in units}
        k2s = {u: ld(kt_ref, *u) for u in units}
        v2 = {u: ld(v_ref, *u) for u in units}
        wl = {(ci, p): wc_ref[pl.ds(pl.multiple_of(r0[ci] + C - 8, 8), 8), LANES * p:LANES * (p + 1)][7:8, :]
              for (ci, p) in units}
        ar = {u: jnp.concatenate([a2[u], r2[u]], axis=0) for u in units}
        mb = {u: nt(ar[u], b2[u]) for u in units}
        mk = {u: nt(ar[u], k2s[u]) for u in units}
        lab = {u: jnp.where(strict, mb[u][0:LANES], 0.0) for u in units}
        mrb = {u: jnp.where(incl, mb[u][LANES:], 0.0) for u in units}
        lak = {u: jnp.where(strict, mk[u][0:LANES], 0.0) for u in units}
        mrk = {u: jnp.where(incl, mk[u][LANES:], 0.0) for u in units}
        xx = {u: jnp.concatenate([a2[u], mm(lak[u], v2[u])], axis=1) for u in units}
        lp = lab
        for it in range(6):
            xx = {u: xx[u] + mm(lp[u], xx[u]) for u in units}
            if it < 5:
                lp = {u: mm(lp[u], lp[u]) for u in units}
        mq = {u: mm(mrb[u], xx[u]) for u in units}
        mv = {u: mm(mrk[u], v2[u]) for u in units}
        bx = {u: mm((b2[u] * wl[u]).T, xx[u]) for u in units}
        kv = {u: mm((k2s[u] * wl[u]).T, v2[u]) for u in units}
        st = [st_ref[p] for p in range(npair)]
        for (ci, p) in units:
            u = (ci, p)
            q1 = r2[u] + mq[u][:, 0:LANES]
            q2 = mq[u][:, LANES:] + mv[u]
            gmat = jnp.where(eye, jnp.broadcast_to(wl[u], (LANES, LANES)), 0.0) + bx[u][:, 0:LANES]
            hmat = bx[u][:, LANES:] + kv[u]
            gs = mm(jnp.concatenate([gmat, q1], axis=0), st[p])
            st[p] = gs[0:LANES] + hmat
            yy = gs[LANES:] + q2
            y_ref[pl.ds(r0[ci], C), LANES * p:LANES * (p + 1)] = yy[0:C] + yy[C:]
        for p in range(npair):
            st_ref[p] = st[p]
        return carry

    lax.fori_loop(0, tt // (C * CPI), chunks, 0)

    y = y_ref[...]
    inv = 1.0 / HEAD_DIM
    mean = _seg_sum(y, seg, npair) * inv
    yc = y - mean
    var = _seg_sum(yc * yc, seg, npair) * inv
    yn = yc * lax.rsqrt(var + RWKV_LN_EPS) * lng_ref[...] + lnb_ref[...]
    o_ref[...] = ((yn + bon_ref[...]) * g_ref[...]).astype(o_ref.dtype)


def _rwkv(rcols, mu, w0, w2, a0, a2, g2, k_k, k_a, r_k, ln_g, ln_b, B, S, prec=HIGHEST):
    T = B * S
    W = RWKV_WIDTH
    tt = 512 if S % 512 == 0 else S
    nt_ = S // tt
    npair = RWKV_HEADS // 2
    pad = LANES - DECAY_LORA
    mu_p = jnp.concatenate([mu[:3 * W], mu[3 * W:3 * W + DECAY_LORA], jnp.zeros((pad,), F32),
                            mu[3 * W + DECAY_LORA:3 * W + DECAY_LORA + AAA_LORA], jnp.zeros((pad,), F32),
                            mu[3 * W + DECAY_LORA + AAA_LORA:]]).reshape(1, RW_PAD_COLS)
    w2p = jnp.concatenate([w2, jnp.zeros((pad, W), F32)], axis=0)
    a2p = jnp.concatenate([a2, jnp.zeros((pad, W), F32)], axis=0)
    vec = lambda t: t.reshape(1, W).astype(F32)
    full = lambda shape: pl.BlockSpec(shape, lambda b, i: (0,) * len(shape))
    sc = lambda: pltpu.VMEM((tt, W), F32)
    return pl.pallas_call(
        functools.partial(_rwkv_kernel, tt=tt, prec=prec),
        out_shape=jax.ShapeDtypeStruct((T, W), BF16),
        grid=(B, nt_),
        in_specs=[
            pl.BlockSpec((tt, RW_PAD_COLS), lambda b, i: (b * nt_ + i, 0)),
            full((1, RW_PAD_COLS)), full((1, W)), full((LANES, W)), full((1, W)), full((LANES, W)),
            full((GATE_LORA, W)), full((1, W)), full((1, W)), full((1, W)), full((1, W)), full((1, W)),
        ],
        out_specs=pl.BlockSpec((tt, W), lambda b, i: (b * nt_ + i, 0)),
        scratch_shapes=[
            pltpu.VMEM((1, RW_PAD_COLS), F32),
            pltpu.VMEM((npair, LANES, LANES), F32),
            sc(), sc(), sc(), sc(), sc(), sc(), sc(), sc(), sc(),
        ],
        compiler_params=_params(("parallel", "arbitrary")),
        name="rwkv7",
    )(rcols, mu_p, vec(w0), w2p, vec(a0), a2p, g2, vec(k_k), vec(k_a), vec(r_k), vec(ln_g), vec(ln_b))


def _outproj_kernel(ya_ref, yb_ref, yc_ref, x_ref, w_ref, gm_ref, g_ref, sh_ref, sc_ref, xo_ref, ho_ref):
    o1 = DIFF_WIDTH
    o2 = o1 + RWKV_WIDTH
    mix = (_mm(ya_ref[...], w_ref[0:o1, :]) + _mm(yb_ref[...], w_ref[o1:o2, :])
           + _mm(yc_ref[...], w_ref[o2:, :]))
    xn = x_ref[...] + gm_ref[...] * mix
    xo_ref[...] = xn
    ms = jnp.mean(xn * xn, axis=-1, keepdims=True)
    y = xn * lax.rsqrt(ms + RMS_EPS) * g_ref[...]
    ho_ref[...] = y * (1.0 + sc_ref[...]) + sh_ref[...]


def _out_proj(ya, yb, yc, x2, w_out, g, mod4, S):
    T, D = x2.shape
    tm = 512 if S % 512 == 0 else S
    nb = S // tm
    row = lambda i: (i, 0)
    modspec = lambda which: pl.BlockSpec((None, None, 1, D), lambda i: (i // nb, which, 0, 0))
    return pl.pallas_call(
        _outproj_kernel,
        out_shape=(jax.ShapeDtypeStruct((T, D), F32), jax.ShapeDtypeStruct((T, D), F32)),
        grid=(T // tm,),
        in_specs=[
            pl.BlockSpec((tm, DIFF_WIDTH), row), pl.BlockSpec((tm, RWKV_WIDTH), row), pl.BlockSpec((tm, FOX_WIDTH), row),
            pl.BlockSpec((tm, D), row),
            pl.BlockSpec((D, D), lambda i: (0, 0)),
            modspec(2),
            pl.BlockSpec((1, D), lambda i: (0, 0)),
            modspec(3), modspec(4),
        ],
        out_specs=(pl.BlockSpec((tm, D), row), pl.BlockSpec((tm, D), row)),
        compiler_params=_params(("parallel",)),
        name="out_proj",
    )(ya, yb, yc, x2, w_out.astype(BF16), mod4, g.reshape(1, D), mod4, mod4)


def _top16(s, iota_f, n):
    vals, poss = [], []
    for _ in range(PEER_TOPK):
        m = jnp.max(s, axis=0, keepdims=True)
        pos = jnp.min(jnp.where(s == m, iota_f, float(n)), axis=0, keepdims=True)
        vals.append(m)
        poss.append(pos)
        s = jnp.where(iota_f == pos, -jnp.inf, s)
    return jnp.concatenate(vals, axis=0), jnp.concatenate(poss, axis=0)


ROUTE_UNROLL = 4
PEER_NCAND = 56


def _peer_cand_tables():
    K = PEER_TOPK
    pairs = [(a, b) for a in range(K) for b in range(K) if (a + 1) * (b + 1) <= K]
    n = PEER_NCAND
    p0 = [[0.0] * K for _ in range(n)]
    p1 = [[0.0] * K for _ in range(n)]
    pad = [0.0] * n
    pos = [float(K * K + r) for r in range(n)]
    for r, (a, b) in enumerate(pairs):
        p0[r][a] = 1.0
        p1[r][b] = 1.0
        pos[r] = float(a * K + b)
    for r in range(len(pairs), n):
        pad[r] = -float("inf")
    col = lambda v: jnp.broadcast_to(jnp.asarray(v, F32)[:, None], (n, LANES))
    return jnp.asarray(p0, F32), jnp.asarray(p1, F32), col(pad), col(pos)


def _peer_route_kernel(h_ref, wq_ref, sk_ref, p0_ref, p1_ref, cpad_ref, cpos_ref, idx_ref, gate_ref, q_scr, e_scr,
                       g_scr):
    K = PEER_TOPK
    hb = h_ref[...].astype(BF16)
    q = _mm(hb, wq_ref[...])
    for hc in range(2 * PEER_HEADS):
        q_scr[hc] = q[:, LANES * hc:LANES * (hc + 1)].astype(BF16)
    iota_n = lax.broadcasted_iota(jnp.int32, (PEER_NKEYS, LANES), 0).astype(F32)
    cpos = cpos_ref[...]

    def one_head(h):
        sv0, si0 = _top16(_nt(sk_ref[2 * h], q_scr[2 * h]), iota_n, PEER_NKEYS)
        sv1, si1 = _top16(_nt(sk_ref[2 * h + 1], q_scr[2 * h + 1]), iota_n, PEER_NKEYS)
        cand = _mm(p0_ref[...], sv0, HIGHEST) + _mm(p1_ref[...], sv1, HIGHEST) + cpad_ref[...]
        cidx = _mm(p0_ref[...], si0) * float(PEER_NKEYS) + _mm(p1_ref[...], si1)
        fv, es = [], []
        for _ in range(K):
            m = jnp.max(cand, axis=0, keepdims=True)
            pos = jnp.min(jnp.where(cand == m, cpos, float(2 * K * K)), axis=0, keepdims=True)
            hit = cpos == pos
            fv.append(m)
            es.append(jnp.max(jnp.where(hit, cidx, -1.0), axis=0, keepdims=True))
            cand = jnp.where(hit, -jnp.inf, cand)
        fv = jnp.concatenate(fv, axis=0)
        ex = jnp.exp(fv - fv[0:1, :])
        g_scr[h] = ex / jnp.sum(ex, axis=0, keepdims=True)
        e_scr[h] = jnp.concatenate(es, axis=0)

    def heads(hh, carry):
        for j in range(ROUTE_UNROLL):
            one_head(hh * ROUTE_UNROLL + j)
        return carry

    lax.fori_loop(0, PEER_HEADS // ROUTE_UNROLL, heads, 0)
    e = e_scr[...].reshape(PEER_HEADS * K, LANES)
    idx_ref[...] = e.T.astype(jnp.int32)
    gate_ref[...] = g_scr[...].reshape(PEER_HEADS * K, LANES)


def _peer_route(h2, wq, subkeys):
    T, D = h2.shape
    tm = LANES
    nq = 2 * PEER_HEADS
    sk = subkeys.reshape(nq, PEER_NKEYS, PEER_HALF).astype(BF16)
    p0, p1, cpad, cpos = _peer_cand_tables()
    const = lambda shape: pl.BlockSpec(shape, lambda i: (0,) * len(shape))
    return pl.pallas_call(
        _peer_route_kernel,
        out_shape=(jax.ShapeDtypeStruct((T, PEER_HEADS * PEER_TOPK), jnp.int32),
                   jax.ShapeDtypeStruct((T // tm, PEER_HEADS * PEER_TOPK, tm), F32)),
        grid=(T // tm,),
        in_specs=[
            pl.BlockSpec((tm, D), lambda i: (i, 0)),
            const((D, nq * PEER_HALF)),
            const((nq, PEER_NKEYS, PEER_HALF)),
            const((PEER_NCAND, PEER_TOPK)), const((PEER_NCAND, PEER_TOPK)),
            const((PEER_NCAND, LANES)), const((PEER_NCAND, LANES)),
        ],
        out_specs=(pl.BlockSpec((tm, PEER_HEADS * PEER_TOPK), lambda i: (i, 0)),
                   pl.BlockSpec((None, PEER_HEADS * PEER_TOPK, tm), lambda i: (i, 0, 0))),
        scratch_shapes=[
            pltpu.VMEM((nq, tm, PEER_HALF), BF16),
            pltpu.VMEM((PEER_HEADS, PEER_TOPK, tm), F32),
            pltpu.VMEM((PEER_HEADS, PEER_TOPK, tm), F32),
        ],
        compiler_params=_params(("parallel",)),
        name="peer_route",
    )(h2, wq.astype(BF16), sk, p0, p1, cpad, cpos)


PEER_G = 16
PEER_SLOTS = PEER_HEADS * PEER_TOPK


def _peer_eval_kernel(idx_ref, idxn_ref, gate_ref, h_ref, x_ref, gf_ref, fg_ref, uv_ref, o_ref, buf, sem, *, final):
    G = PEER_G
    R = G * PEER_SLOTS
    D = D_MODEL
    tiles = PEER_SLOTS // SUBLANES
    i = pl.program_id(0)
    n = pl.num_programs(0)

    def start(ids, off, s, t, u):
        pltpu.make_async_copy(uv_ref.at[ids[off + t * SUBLANES + u]], buf.at[s, t, pl.ds(u, 1), :],
                              sem.at[s]).start(priority=u % 2)

    def wait(s):
        pltpu.make_async_copy(buf.at[s], buf.at[s], sem.at[s]).wait()

    @pl.when(i == 0)
    def _():
        def body(t, carry):
            for u in range(SUBLANES):
                start(idx_ref, 0, 0, t, u)
            return carry
        lax.fori_loop(0, R // SUBLANES, body, 0)

    lane = lax.broadcasted_iota(jnp.int32, (1, LANES), 1)
    tbase = (i % (LANES // (2 * G))) * (2 * G)
    gate = gate_ref[...]
    outs = []
    for s in range(2):
        wait(s)
        nxt_ids, nxt_off = (idx_ref, R) if s == 0 else (idxn_ref, 0)
        for g in range(G):
            for t in range(tiles * g, tiles * (g + 1)):
                for u in range(SUBLANES):
                    start(nxt_ids, nxt_off, 1 - s, t, u)
            w_rows = buf[s, tiles * g:tiles * (g + 1)].reshape(PEER_SLOTS, D)
            u_rows = lax.bitcast_convert_type(w_rows & jnp.uint32(0xFFFF0000), F32)
            prod = u_rows * h_ref[G * s + g:G * s + g + 1, :]
            part = prod[:, 0:LANES]
            for c in range(1, D // LANES):
                part = part + prod[:, LANES * c:LANES * (c + 1)]
            act = jnp.sum(part, axis=1, keepdims=True)
            gcol = jnp.sum(jnp.where(lane == tbase + G * s + g, gate, 0.0), axis=1, keepdims=True)
            coef = gcol * (0.5 * act * (1.0 + lax.erf(act * (2.0 ** -0.5))))
            v_rows = lax.bitcast_convert_type(w_rows << 16, F32)
            outs.append(jnp.sum(v_rows * coef, axis=0, keepdims=True))
    xn = x_ref[...] + gf_ref[...] * jnp.concatenate(outs, axis=0)
    if final:
        ms = jnp.mean(xn * xn, axis=-1, keepdims=True)
        xn = xn * lax.rsqrt(ms + RMS_EPS) * fg_ref[...]
    o_ref[...] = xn

    @pl.when(i == n - 1)
    def _():
        wait(0)


def _peer_eval(eidx, gate_t, h2, x2, mod4, final_g, uv, S, final, ntok=None):
    T, D = x2.shape
    T = T if ntok is None else ntok
    G = PEER_G
    R = G * PEER_SLOTS
    n = T // (2 * G)
    return pl.pallas_call(
        functools.partial(_peer_eval_kernel, final=final),
        out_shape=jax.ShapeDtypeStruct((T, D), F32),
        grid=(n,),
        in_specs=[
            pl.BlockSpec((2 * R,), lambda i: (i,), memory_space=pltpu.SMEM),
            pl.BlockSpec((R,), lambda i: (jnp.minimum(2 * i + 2, 2 * n - 2),), memory_space=pltpu.SMEM),
            pl.BlockSpec((None, PEER_SLOTS, LANES), lambda i: (i // (LANES // (2 * G)), 0, 0)),
            pl.BlockSpec((2 * G, D), lambda i: (i, 0)),
            pl.BlockSpec((2 * G, D), lambda i: (i, 0)),
            pl.BlockSpec((None, None, 1, D), lambda i: (i // (S // (2 * G)), 5, 0, 0)),
            pl.BlockSpec((1, D), lambda i: (0, 0)),
            pl.BlockSpec(memory_space=pl.ANY),
        ],
        out_specs=pl.BlockSpec((2 * G, D), lambda i: (i, 0)),
        scratch_shapes=[pltpu.VMEM((2, R // SUBLANES, SUBLANES, D), jnp.uint32), pltpu.SemaphoreType.DMA((2,))],
        compiler_params=_params(("arbitrary",)),
        name="peer_eval",
    )(eidx.reshape(-1), eidx.reshape(-1), gate_t, h2, x2, mod4, final_g.reshape(1, D), uv)


SC_WORKERS = 32
SC_WINDOW = 128
SC_ROWS = 32
PEER_GD = 16
SC_SHARE_NUM, SC_SHARE_DEN = 1, 2


def _sc_gather(tab, idx):
    n = idx.shape[0]
    width = tab.shape[1]
    per = n // SC_WORKERS
    nsub = SC_WINDOW // SC_ROWS
    mesh = plsc.VectorSubcoreMesh(core_axis_name="core", subcore_axis_name="subcore")

    @pl.kernel(out_type=jax.ShapeDtypeStruct((n, width), tab.dtype), mesh=mesh,
               scratch_types=[pltpu.VMEM((SC_WINDOW,), jnp.int32), pltpu.VMEM((2, SC_ROWS, width), tab.dtype),
                              pltpu.SemaphoreType.DMA((2,)), pltpu.SemaphoreType.DMA((2,))])
    def gather(x_hbm, i_hbm, o_hbm, idx_v, rows_v, gsem, wsem):
        wid = lax.axis_index("core") * (SC_WORKERS // 2) + lax.axis_index("subcore")

        def start_gather(k):
            return pltpu.async_copy(x_hbm.at[idx_v.at[pl.ds(k * SC_ROWS, SC_ROWS)]], rows_v.at[k % 2], gsem.at[k % 2])

        @pl.loop(0, per // SC_WINDOW)
        def _(w):
            base = wid * per + w * SC_WINDOW
            pltpu.sync_copy(i_hbm.at[pl.ds(base, SC_WINDOW)], idx_v)
            gat = [start_gather(0)] + [None] * (nsub - 1)
            wrt = [None] * nsub
            for k in range(nsub):
                if k + 1 < nsub:
                    if k >= 1:
                        wrt[k - 1].wait()
                    gat[k + 1] = start_gather(k + 1)
                gat[k].wait()
                wrt[k] = pltpu.async_copy(rows_v.at[k % 2], o_hbm.at[pl.ds(base + k * SC_ROWS, SC_ROWS)],
                                          wsem.at[k % 2])
            for k in range(max(nsub - 2, 0), nsub):
                wrt[k].wait()

    return gather(tab, idx)


def _peer_dense_kernel(rows_ref, gate_ref, h_ref, x_ref, gf_ref, fg_ref, o_ref, *, final, blk0):
    G = PEER_GD
    D = D_MODEL
    i = pl.program_id(0)
    lane = lax.broadcasted_iota(jnp.int32, (1, LANES), 1)
    tbase = ((blk0 + i) % (LANES // G)) * G
    gate = gate_ref[...]
    outs = []
    for g in range(G):
        w_rows = rows_ref[PEER_SLOTS * g:PEER_SLOTS * (g + 1), :]
        u_rows = lax.bitcast_convert_type(w_rows & jnp.uint32(0xFFFF0000), F32)
        prod = u_rows * h_ref[g:g + 1, :]
        part = prod[:, 0:LANES]
        for c in range(1, D // LANES):
            part = part + prod[:, LANES * c:LANES * (c + 1)]
        act = jnp.sum(part, axis=1, keepdims=True)
        gcol = jnp.sum(jnp.where(lane == tbase + g, gate, 0.0), axis=1, keepdims=True)
        coef = gcol * (0.5 * act * (1.0 + lax.erf(act * (2.0 ** -0.5))))
        v_rows = lax.bitcast_convert_type(w_rows << 16, F32)
        outs.append(jnp.sum(v_rows * coef, axis=0, keepdims=True))
    xn = x_ref[...] + gf_ref[...] * jnp.concatenate(outs, axis=0)
    if final:
        ms = jnp.mean(xn * xn, axis=-1, keepdims=True)
        xn = xn * lax.rsqrt(ms + RMS_EPS) * fg_ref[...]
    o_ref[...] = xn


def _peer_eval_dense(rows, gate_t, h2, x2, mod4, final_g, S, final, tok0):
    T, D = x2.shape
    G = PEER_GD
    tb = rows.shape[0] // PEER_SLOTS
    blk0 = tok0 // G
    return pl.pallas_call(
        functools.partial(_peer_dense_kernel, final=final, blk0=blk0),
        out_shape=jax.ShapeDtypeStruct((tb, D), F32),
        grid=(tb // G,),
        in_specs=[
            pl.BlockSpec((G * PEER_SLOTS, D), lambda i: (i, 0)),
            pl.BlockSpec((None, PEER_SLOTS, LANES), lambda i: ((blk0 + i) // (LANES // G), 0, 0)),
            pl.BlockSpec((G, D), lambda i: (blk0 + i, 0)),
            pl.BlockSpec((G, D), lambda i: (blk0 + i, 0)),
            pl.BlockSpec((None, None, 1, D), lambda i: ((blk0 + i) // (S // G), 5, 0, 0)),
            pl.BlockSpec((1, D), lambda i: (0, 0)),
        ],
        out_specs=pl.BlockSpec((G, D), lambda i: (i, 0)),
        compiler_params=_params(("parallel",)),
        name="peer_dense",
    )(rows, gate_t, h2, x2, mod4, final_g.reshape(1, D))


def _pack_uv(u, v):
    hi = lax.bitcast_convert_type(u.astype(BF16), jnp.uint16).astype(jnp.uint32) << 16
    lo = lax.bitcast_convert_type(v.astype(BF16), jnp.uint16).astype(jnp.uint32)
    return (hi | lo).reshape(u.shape[0], 1, u.shape[1])


def kernel(x, c, norm_mix_g, norm_ffn_g, final_norm_g, ada_w, ada_b, w_in, w_out, dif_lam, dif_subln_g, rw_mu, rw_w0,
           rw_w2, rw_a0, rw_a2, rw_g2, rw_kk, rw_ka, rw_rk, rw_ln_g, rw_ln_b, fox_bf, peer_wq, peer_subkeys, peer_u,
           peer_v):
    B, S, D = x.shape
    T = B * S
    depth = ada_w.shape[0]
    x2 = x.reshape(T, D)
    mod = _ada_mod(c, ada_w, ada_b)
    for l in range(depth):
        mod4 = mod[l].reshape(B, 6, 1, D)
        w_pad, w_vt = _pad_w_in(w_in[l])
        dqk, rcols, fqk, fl, vtd, vtf = _in_proj(x2, norm_mix_g[l], mod4, w_pad, w_vt, S)
        ya = _diff_attention(dqk, vtd, dif_lam[l], dif_subln_g[l], l, B, S)
        yb = _rwkv(rcols, rw_mu[l], rw_w0[l], rw_w2[l], rw_a0[l], rw_a2[l], rw_g2[l], rw_kk[l], rw_ka[l],
                   rw_rk[l].reshape(-1), rw_ln_g[l], rw_ln_b[l], B, S, prec=None)
        crep, crow = _fox_cum(fl, fox_bf[l], B, S)
        yc = _fox_attention(fqk, vtf, crep, crow, B, S)
        x2, h2 = _out_proj(ya, yb, yc, x2, w_out[l], norm_ffn_g[l], mod4, S)
        eidx, gate_t = _peer_route(h2, peer_wq[l], peer_subkeys[l])
        uv = _pack_uv(peer_u[l], peer_v[l])
        final = l == depth - 1
        tb = (T * SC_SHARE_NUM // SC_SHARE_DEN) // (SC_WORKERS * SC_WINDOW) * (SC_WORKERS * SC_WINDOW)
        ta = T - tb
        rows_b = _sc_gather(uv.reshape(-1, D), eidx.reshape(-1)[ta * PEER_SLOTS:])
        xa = _peer_eval(eidx, gate_t, h2, x2, mod4, final_norm_g, uv, S, final, ntok=ta)
        xb = _peer_eval_dense(rows_b, gate_t, h2, x2, mod4, final_norm_g, S, final, ta)
        x2 = jnp.concatenate([xa, xb], axis=0)
    return x2.reshape(B, S, D)
```

```python
import functools
import math

import jax
import jax.numpy as jnp
from jax import lax
from jax.experimental import pallas as pl
from jax.experimental.pallas import tpu as pltpu
from jax.experimental.pallas import tpu_sc as plsc

F32 = jnp.float32
BF16 = jnp.bfloat16
HIGHEST = lax.Precision.HIGHEST

D_MODEL = 1024
HEAD_DIM = 64
DIFF_HEADS = 6
DIFF_QK_DIM = HEAD_DIM // 2
RWKV_HEADS = 6
FOX_HEADS = 4
DIFF_WIDTH = DIFF_HEADS * HEAD_DIM
RWKV_WIDTH = RWKV_HEADS * HEAD_DIM
FOX_WIDTH = FOX_HEADS * HEAD_DIM
DECAY_LORA = 64
AAA_LORA = 64
GATE_LORA = 128
DIFF_COLS = 3 * DIFF_WIDTH
RWKV_COLS = 3 * RWKV_WIDTH + DECAY_LORA + AAA_LORA + GATE_LORA
PEER_HEADS = 8
PEER_NKEYS = 128
PEER_TOPK = 16
PEER_QDIM = 256
PEER_HALF = PEER_QDIM // 2
RMS_EPS = 1e-6
RWKV_LN_EPS = 64e-5

LANES = 128
SUBLANES = 8
RW_PAD_COLS = 3 * RWKV_WIDTH + 3 * LANES
VMEM_LIMIT = 56 * 1024 * 1024

RW_CHUNK = 64
RW_CHUNKS_PER_ITER = 2


def _params(sem, vmem=VMEM_LIMIT):
    return pltpu.CompilerParams(dimension_semantics=sem, vmem_limit_bytes=vmem)


def _nt(a, b, precision=None):
    return lax.dot_general(a, b, (((1,), (1,)), ((), ())), preferred_element_type=F32, precision=precision)


def _mm(a, b, precision=None):
    return jnp.dot(a, b, preferred_element_type=F32, precision=precision)


def _ada_kernel(c_ref, w_ref, b_ref, o_ref):
    c = c_ref[...]
    ca = c * jax.nn.sigmoid(c)
    o_ref[...] = _mm(ca, w_ref[...], HIGHEST) + b_ref[...]


def _ada_mod(c, ada_w, ada_b):
    L, D, N = ada_w.shape
    B = c.shape[0]
    tn = 1536
    return pl.pallas_call(
        _ada_kernel,
        out_shape=jax.ShapeDtypeStruct((L, B, N), F32),
        grid=(L, N // tn),
        in_specs=[
            pl.BlockSpec((B, D), lambda l, j: (0, 0)),
            pl.BlockSpec((None, D, tn), lambda l, j: (l, 0, j)),
            pl.BlockSpec((None, 1, tn), lambda l, j: (l, 0, j)),
        ],
        out_specs=pl.BlockSpec((None, B, tn), lambda l, j: (l, 0, j)),
        compiler_params=_params(("parallel", "parallel")),
        name="ada_mod",
    )(c, ada_w, ada_b.reshape(L, 1, N))


ATT_T = 512
QK_DIFF = 2 * DIFF_WIDTH
QK_FOX = 2 * FOX_WIDTH
VT_ROWS = DIFF_WIDTH + FOX_WIDTH
IN_PAD_COLS = QK_DIFF + RW_PAD_COLS + QK_FOX + LANES


def _inproj_kernel(x_ref, g_ref, sh_ref, sc_ref, w_ref, wvt_ref, d_ref, r_ref, f_ref, fl_ref, vtd_ref, vtf_ref, *, ta):
    x = x_ref[...]
    ms = jnp.mean(x * x, axis=-1, keepdims=True)
    y = x * lax.rsqrt(ms + RMS_EPS) * g_ref[...]
    h = (y * (1.0 + sc_ref[...]) + sh_ref[...]).astype(BF16)
    o1 = QK_DIFF
    o2 = o1 + RW_PAD_COLS
    o3 = o2 + QK_FOX
    d_ref[...] = _mm(h, w_ref[:, 0:o1]).astype(BF16)
    r_ref[...] = _mm(h, w_ref[:, o1:o2])
    f_ref[...] = _mm(h, w_ref[:, o2:o3]).astype(BF16)
    fl_ref[...] = _mm(h, w_ref[:, o3:o3 + LANES])
    vt = _nt(wvt_ref[...], h).astype(BF16)
    for s in range(x.shape[0] // ta):
        vtd_ref[s] = vt[0:DIFF_WIDTH, ta * s:ta * (s + 1)]
        vtf_ref[s] = vt[DIFF_WIDTH:, ta * s:ta * (s + 1)]


def _in_proj(x2, g, mod4, w_pad, w_vt, S):
    T, D = x2.shape
    tm = 512 if S % 512 == 0 else S
    ta = min(ATT_T, S)
    nb = S // tm
    row = lambda i: (i, 0)
    return pl.pallas_call(
        functools.partial(_inproj_kernel, ta=ta),
        out_shape=(
            jax.ShapeDtypeStruct((T, QK_DIFF), BF16),
            jax.ShapeDtypeStruct((T, RW_PAD_COLS), F32),
            jax.ShapeDtypeStruct((T, QK_FOX), BF16),
            jax.ShapeDtypeStruct((T, LANES), F32),
            jax.ShapeDtypeStruct((T // ta, DIFF_WIDTH, ta), BF16),
            jax.ShapeDtypeStruct((T // ta, FOX_WIDTH, ta), BF16),
        ),
        grid=(T // tm,),
        in_specs=[
            pl.BlockSpec((tm, D), row),
            pl.BlockSpec((1, D), lambda i: (0, 0)),
            pl.BlockSpec((None, None, 1, D), lambda i: (i // nb, 0, 0, 0)),
            pl.BlockSpec((None, None, 1, D), lambda i: (i // nb, 1, 0, 0)),
            pl.BlockSpec((D, IN_PAD_COLS), lambda i: (0, 0)),
            pl.BlockSpec((VT_ROWS, D), lambda i: (0, 0)),
        ],
        out_specs=(
            pl.BlockSpec((tm, QK_DIFF), row),
            pl.BlockSpec((tm, RW_PAD_COLS), row),
            pl.BlockSpec((tm, QK_FOX), row),
            pl.BlockSpec((tm, LANES), row),
            pl.BlockSpec((tm // ta, DIFF_WIDTH, ta), lambda i: (i, 0, 0)),
            pl.BlockSpec((tm // ta, FOX_WIDTH, ta), lambda i: (i, 0, 0)),
        ),
        compiler_params=_params(("parallel",)),
        name="in_proj",
    )(x2, g.reshape(1, D), mod4, mod4, w_pad, w_vt)


def _pad_w_in(w_in):
    D = w_in.shape[0]
    W = RWKV_WIDTH
    o = DIFF_COLS
    z64 = jnp.zeros((D, LANES - DECAY_LORA), w_in.dtype)
    rw = w_in[:, o:o + RWKV_COLS]
    fx = w_in[:, o + RWKV_COLS:]
    zf = jnp.zeros((D, LANES - FOX_HEADS), w_in.dtype)
    w_pad = jnp.concatenate([
        w_in[:, :QK_DIFF],
        rw[:, :3 * W], rw[:, 3 * W:3 * W + DECAY_LORA], z64,
        rw[:, 3 * W + DECAY_LORA:3 * W + DECAY_LORA + AAA_LORA], z64,
        rw[:, 3 * W + DECAY_LORA + AAA_LORA:],
        fx[:, :QK_FOX], fx[:, 3 * FOX_WIDTH:], zf,
    ], axis=1).astype(BF16)
    w_vt = jnp.concatenate([w_in[:, QK_DIFF:o], fx[:, QK_FOX:3 * FOX_WIDTH]], axis=1).T.astype(BF16)
    return w_pad, w_vt


LOG2E = math.log2(math.e)


def _flash_step(s2s, vt, m_ref, l_ref, acc_ref):
    n = len(s2s)
    m_old = [m_ref[x] for x in range(n)]
    m_new = [jnp.maximum(m_old[x], jnp.max(s2s[x], axis=0, keepdims=True)) for x in range(n)]
    alpha = [jnp.exp2(m_old[x] - m_new[x]) for x in range(n)]
    p = [jnp.exp2(s2s[x] - m_new[x]) for x in range(n)]
    pv = [_mm(vt, p[x].astype(BF16)) for x in range(n)]
    for x in range(n):
        l_ref[x] = alpha[x] * l_ref[x] + jnp.sum(p[x], axis=0, keepdims=True)
        acc_ref[x] = alpha[x] * acc_ref[x] + pv[x]
        m_ref[x] = m_new[x]


def _diff_attn_kernel(lam_ref, g_ref, q_ref, k_ref, vt_ref, o_ref, m_ref, l_ref, acc_ref, *, tq, lam_init):
    i = pl.program_id(2)
    c = (DIFF_QK_DIM ** -0.5) * LOG2E
    lane = lax.broadcasted_iota(jnp.int32, (1, LANES), 1)
    q = q_ref[...]
    zero = jnp.zeros_like(q)
    qm = [jnp.where((lane >= DIFF_QK_DIM * x) & (lane < DIFF_QK_DIM * (x + 1)), q, zero) for x in range(4)]
    m_ref[...] = jnp.full(m_ref.shape, -jnp.inf, F32)
    l_ref[...] = jnp.zeros(l_ref.shape, F32)
    acc_ref[...] = jnp.zeros(acc_ref.shape, F32)

    def step(j, diag):
        k = k_ref[pl.ds(pl.multiple_of(j * tq, tq), tq), :]
        vt = vt_ref[j]
        if diag:
            keep = (lax.broadcasted_iota(jnp.int32, (tq, tq), 1) >= lax.broadcasted_iota(jnp.int32, (tq, tq), 0))
        s2s = [_nt(k, qm[x]) * c for x in range(4)]
        if diag:
            s2s = [jnp.where(keep, s2, -jnp.inf) for s2 in s2s]
        _flash_step(s2s, vt, m_ref, l_ref, acc_ref)

    def body(j, carry):
        step(j, False)
        return carry

    lax.fori_loop(0, i, body, 0)
    step(i, True)

    lp = lam_ref[...]
    lam = (jnp.exp(jnp.sum(lp[0:1] * lp[1:2], axis=-1, keepdims=True))
           - jnp.exp(jnp.sum(lp[2:3] * lp[3:4], axis=-1, keepdims=True)) + lam_init)
    outs = [acc_ref[2 * hh] / l_ref[2 * hh] - lam * (acc_ref[2 * hh + 1] / l_ref[2 * hh + 1]) for hh in range(2)]
    row = lax.broadcasted_iota(jnp.int32, (LANES, 1), 0)
    o = jnp.where(row < HEAD_DIM, outs[0], outs[1])
    sq = o * o
    ms = jnp.where(row < HEAD_DIM, jnp.sum(sq[0:HEAD_DIM], axis=0, keepdims=True),
                   jnp.sum(sq[HEAD_DIM:], axis=0, keepdims=True)) * (1.0 / HEAD_DIM)
    y = o * lax.rsqrt(ms + RMS_EPS) * g_ref[...] * (1.0 - lam_init)
    o_ref[...] = y.T.astype(o_ref.dtype)


def _diff_attention(dqk, vtd, lam_params, subln_g, layer_idx, B, S):
    T = B * S
    tq = min(ATT_T, S)
    nq = S // tq
    npair = DIFF_HEADS // 2
    lam_init = 0.8 - 0.6 * math.exp(-0.3 * layer_idx)
    g2 = jnp.concatenate([subln_g, subln_g]).reshape(LANES, 1).astype(F32)
    return pl.pallas_call(
        functools.partial(_diff_attn_kernel, tq=tq, lam_init=lam_init),
        out_shape=jax.ShapeDtypeStruct((T, DIFF_WIDTH), BF16),
        grid=(B, npair, nq),
        in_specs=[
            pl.BlockSpec((4, DIFF_QK_DIM), lambda b, p, i: (0, 0)),
            pl.BlockSpec((LANES, 1), lambda b, p, i: (0, 0)),
            pl.BlockSpec((tq, LANES), lambda b, p, i: (b * nq + i, p)),
            pl.BlockSpec((S, LANES), lambda b, p, i: (b, npair + p)),
            pl.BlockSpec((nq, LANES, tq), lambda b, p, i: (b, p, 0)),
        ],
        out_specs=pl.BlockSpec((tq, LANES), lambda b, p, i: (b * nq + i, p)),
        scratch_shapes=[
            pltpu.VMEM((4, 1, tq), F32),
            pltpu.VMEM((4, 1, tq), F32),
            pltpu.VMEM((4, LANES, tq), F32),
        ],
        compiler_params=_params(("parallel", "parallel", "arbitrary")),
        name="diff_attn",
    )(lam_params, g2, dqk, dqk, vtd)


def _fox_cum_kernel(f_ref, b_ref, rep_ref, row_ref, *, S, tc):
    rr = lax.broadcasted_iota(jnp.int32, (tc, tc), 0)
    cc = lax.broadcasted_iota(jnp.int32, (tc, tc), 1)
    tri = (rr >= cc).astype(F32)
    sel_r = lax.broadcasted_iota(jnp.int32, (LANES, LANES), 0)
    carry = jnp.zeros((1, LANES), F32)
    for c in range(S // tc):
        z = f_ref[c * tc:(c + 1) * tc, :] + b_ref[...]
        logf = -(jnp.maximum(-z, 0.0) + jnp.log(1.0 + jnp.exp(-jnp.abs(z))))
        cum = _mm(tri, logf, HIGHEST) + carry
        carry = cum[tc - 1:tc, :]
        row_ref[:, c * tc:(c + 1) * tc] = cum.T[0:8, :]
        for h in range(FOX_HEADS):
            rep_ref[h, c * tc:(c + 1) * tc, :] = _mm(cum, (sel_r == h).astype(F32), HIGHEST)


def _fox_cum(fl, b_f, B, S):
    tc = 256 if S % 256 == 0 else S
    bpad = jnp.zeros((1, LANES), F32).at[0, :FOX_HEADS].set(b_f.astype(F32))
    return pl.pallas_call(
        functools.partial(_fox_cum_kernel, S=S, tc=tc),
        out_shape=(jax.ShapeDtypeStruct((B, FOX_HEADS, S, LANES), F32), jax.ShapeDtypeStruct((B, 8, S), F32)),
        grid=(B,),
        in_specs=[pl.BlockSpec((S, LANES), lambda b: (b, 0)), pl.BlockSpec((1, LANES), lambda b: (0, 0))],
        out_specs=(pl.BlockSpec((None, FOX_HEADS, S, LANES), lambda b: (b, 0, 0, 0)),
                   pl.BlockSpec((None, 8, S), lambda b: (b, 0, 0))),
        compiler_params=_params(("parallel",)),
        name="fox_cum",
    )(fl, bpad)


def _fox_attn_kernel(q_ref, k_ref, vt_ref, c0_ref, c1_ref, cr_ref, o_ref, m_ref, l_ref, acc_ref, *, tq):
    p_id = pl.program_id(1)
    i = pl.program_id(2)
    c = (HEAD_DIM ** -0.5) * LOG2E
    lane = lax.broadcasted_iota(jnp.int32, (1, LANES), 1)
    q = q_ref[...]
    zero = jnp.zeros_like(q)
    qm = [jnp.where((lane >= HEAD_DIM * x) & (lane < HEAD_DIM * (x + 1)), q, zero) for x in range(2)]
    ck_refs = (c0_ref, c1_ref)
    cq = [cr_ref[2 * p_id + x, pl.ds(i, 1), :] for x in range(2)]
    m_ref[...] = jnp.full(m_ref.shape, -jnp.inf, F32)
    l_ref[...] = jnp.zeros(l_ref.shape, F32)
    acc_ref[...] = jnp.zeros(acc_ref.shape, F32)

    def step(j, diag):
        off = pl.multiple_of(j * tq, tq)
        k = k_ref[pl.ds(off, tq), :]
        vt = vt_ref[j]
        if diag:
            keep = (lax.broadcasted_iota(jnp.int32, (tq, tq), 1) >= lax.broadcasted_iota(jnp.int32, (tq, tq), 0))
        s2s = []
        for x in range(2):
            ck = ck_refs[x][pl.ds(off, tq), :]
            bias = (cq[x] - jnp.concatenate([ck] * (tq // LANES), axis=1)) * LOG2E
            s2s.append(_nt(k, qm[x]) * c + bias)
        if diag:
            s2s = [jnp.where(keep, s2, -jnp.inf) for s2 in s2s]
        _flash_step(s2s, vt, m_ref, l_ref, acc_ref)

    def body(j, carry):
        step(j, False)
        return carry

    lax.fori_loop(0, i, body, 0)
    step(i, True)
    row = lax.broadcasted_iota(jnp.int32, (LANES, 1), 0)
    o = jnp.where(row < HEAD_DIM, acc_ref[0] / l_ref[0], acc_ref[1] / l_ref[1])
    o_ref[...] = o.T.astype(o_ref.dtype)


def _fox_attention(fqk, vtf, crep, crow, B, S):
    T = B * S
    tq = min(ATT_T, S)
    nq = S // tq
    npair = FOX_HEADS // 2
    crow4 = crow.reshape(B, 8, nq, tq)
    rep = lambda x: pl.BlockSpec((None, None, S, LANES), lambda b, p, i: (b, 2 * p + x, 0, 0))
    return pl.pallas_call(
        functools.partial(_fox_attn_kernel, tq=tq),
        out_shape=jax.ShapeDtypeStruct((T, FOX_WIDTH), BF16),
        grid=(B, npair, nq),
        in_specs=[
            pl.BlockSpec((tq, LANES), lambda b, p, i: (b * nq + i, p)),
            pl.BlockSpec((S, LANES), lambda b, p, i: (b, npair + p)),
            pl.BlockSpec((nq, LANES, tq), lambda b, p, i: (b, p, 0)),
            rep(0), rep(1),
            pl.BlockSpec((None, 8, nq, tq), lambda b, p, i: (b, 0, 0, 0)),
        ],
        out_specs=pl.BlockSpec((tq, LANES), lambda b, p, i: (b * nq + i, p)),
        scratch_shapes=[
            pltpu.VMEM((2, 1, tq), F32),
            pltpu.VMEM((2, 1, tq), F32),
            pltpu.VMEM((2, LANES, tq), F32),
        ],
        compiler_params=_params(("parallel", "parallel", "arbitrary")),
        name="fox_attn",
    )(fqk, fqk, vtf, crep, crep, crow4)


def _split3(x):
    hi = x.astype(BF16)
    r1 = x - hi.astype(F32)
    mid = r1.astype(BF16)
    lo = (r1 - mid.astype(F32)).astype(BF16)
    return hi, mid, lo


def _seg_sum(x, seg, npair):
    parts = _split3(x)
    return jnp.concatenate(
        [sum(_mm(t[:, LANES * p:LANES * (p + 1)], seg) for t in parts) for p in range(npair)], axis=1)


def _rwkv_kernel(x_ref, mu_ref, w0_ref, w2_ref, a0_ref, a2_ref, g2_ref, kk_ref, ka_ref, rk_ref, lng_ref, lnb_ref,
                 o_ref, carry_ref, st_ref, at_ref, rt_ref, bt_ref, kt_ref, v_ref, wc_ref, y_ref, g_ref, bon_ref,
                 *, tt, prec):
    i = pl.program_id(1)
    W = RWKV_WIDTH
    C = RW_CHUNK
    CPI = RW_CHUNKS_PER_ITER
    npair = RWKV_HEADS // 2

    @pl.when(i == 0)
    def _():
        carry_ref[...] = jnp.zeros(carry_ref.shape, F32)
        st_ref[...] = jnp.zeros(st_ref.shape, F32)

    x = x_ref[...]
    rows = lax.broadcasted_iota(jnp.int32, (tt, 1), 0)
    prev = jnp.where(rows == 0, carry_ref[...], pltpu.roll(x, 1, axis=0))
    carry_ref[...] = x[tt - 1:tt, :]
    xs = x + (prev - x) * mu_ref[...]
    r = xs[:, 0:W]
    k = xs[:, W:2 * W]
    v = xs[:, 2 * W:3 * W]
    xw = xs[:, 3 * W:3 * W + LANES]
    xa = xs[:, 3 * W + LANES:3 * W + 2 * LANES]
    xg = xs[:, 3 * W + 2 * LANES:]
    wl = w0_ref[...] + _mm(jnp.tanh(xw), w2_ref[...], HIGHEST)
    w = -(jnp.maximum(-wl, 0.0) + jnp.log(1.0 + jnp.exp(-jnp.abs(wl)))) - 0.5
    logdec = -jnp.exp(w)
    a = jax.nn.sigmoid(a0_ref[...] + _mm(xa, a2_ref[...], HIGHEST))
    g_ref[...] = _mm(jax.nn.sigmoid(xg), g2_ref[...], HIGHEST)

    r_i = lax.broadcasted_iota(jnp.int32, (LANES, LANES), 0)
    c_i = lax.broadcasted_iota(jnp.int32, (LANES, LANES), 1)
    seg = ((r_i // HEAD_DIM) == (c_i // HEAD_DIM)).astype(BF16)
    kkv = k * kk_ref[...]
    kkn = kkv / jnp.maximum(jnp.sqrt(_seg_sum(kkv * kkv, seg, npair)), 1e-12)
    k2 = k * (1.0 + (a - 1.0) * ka_ref[...])
    bon_ref[...] = _seg_sum(r * k2 * rk_ref[...], seg, npair) * v

    rt_i = lax.broadcasted_iota(jnp.int32, (tt, tt), 0)
    ct_i = lax.broadcasted_iota(jnp.int32, (tt, tt), 1)
    tri = (((rt_i // C) == (ct_i // C)) & (rt_i >= ct_i)).astype(BF16)
    cum = sum(_mm(tri, t) for t in _split3(logdec))
    winv = jnp.exp(-cum)
    wcum = jnp.exp(cum)
    at_ref[...] = -kkn * jnp.exp(cum - logdec)
    bt_ref[...] = kkn * a * winv
    kt_ref[...] = k2 * winv
    rt_ref[...] = r * wcum
    v_ref[...] = v
    wc_ref[...] = wcum

    lane = lax.broadcasted_iota(jnp.int32, (1, LANES), 1)
    lo = lane < HEAD_DIM
    tpos = r_i % C
    ipos = c_i % C
    strict = tpos > ipos
    incl = tpos >= ipos
    eye = r_i == c_i

    def stack2(m):
        return jnp.concatenate([jnp.where(lo, m, 0.0), jnp.where(lo, 0.0, m)], axis=0)

    def cast(m):
        return m if prec is not None else m.astype(BF16)

    def mm(p, q):
        return _mm(cast(p), cast(q), prec)

    def nt(p, q):
        return _nt(cast(p), cast(q), prec)

    def chunks(cc, carry):
        units = [(ci, p) for ci in range(CPI) for p in range(npair)]
        r0 = [pl.multiple_of((cc * CPI + ci) * C, C) for ci in range(CPI)]
        ld = lambda ref, ci, p: stack2(ref[pl.ds(r0[ci], C), LANES * p:LANES * (p + 1)])
        a2 = {u: ld(at_ref, *u) for u in units}
        r2 = {u: ld(rt_ref, *u) for u in units}
        b2 = {u: ld(bt_ref, *u) for u in units}
        k2s = {u: ld(kt_ref, *u) for u in units}
        v2 = {u: ld(v_ref, *u) for u in units}
        wl = {(ci, p): wc_ref[pl.ds(pl.multiple_of(r0[ci] + C - 8, 8), 8), LANES * p:LANES * (p + 1)][7:8, :]
              for (ci, p) in units}
        ar = {u: jnp.concatenate([a2[u], r2[u]], axis=0) for u in units}
        mb = {u: nt(ar[u], b2[u]) for u in units}
        mk = {u: nt(ar[u], k2s[u]) for u in units}
        lab = {u: jnp.where(strict, mb[u][0:LANES], 0.0) for u in units}
        mrb = {u: jnp.where(incl, mb[u][LANES:], 0.0) for u in units}
        lak = {u: jnp.where(strict, mk[u][0:LANES], 0.0) for u in units}
        mrk = {u: jnp.where(incl, mk[u][LANES:], 0.0) for u in units}
        xx = {u: jnp.concatenate([a2[u], mm(lak[u], v2[u])], axis=1) for u in units}
        lp = lab
        for it in range(6):
            xx = {u: xx[u] + mm(lp[u], xx[u]) for u in units}
            if it < 5:
                lp = {u: mm(lp[u], lp[u]) for u in units}
        mq = {u: mm(mrb[u], xx[u]) for u in units}
        mv = {u: mm(mrk[u], v2[u]) for u in units}
        bx = {u: mm((b2[u] * wl[u]).T, xx[u]) for u in units}
        kv = {u: mm((k2s[u] * wl[u]).T, v2[u]) for u in units}
        st = [st_ref[p] for p in range(npair)]
        for (ci, p) in units:
            u = (ci, p)
            q1 = r2[u] + mq[u][:, 0:LANES]
            q2 = mq[u][:, LANES:] + mv[u]
            gmat = jnp.where(eye, jnp.broadcast_to(wl[u], (LANES, LANES)), 0.0) + bx[u][:, 0:LANES]
            hmat = bx[u][:, LANES:] + kv[u]
            gs = mm(jnp.concatenate([gmat, q1], axis=0), st[p])
            st[p] = gs[0:LANES] + hmat
            yy = gs[LANES:] + q2
            y_ref[pl.ds(r0[ci], C), LANES * p:LANES * (p + 1)] = yy[0:C] + yy[C:]
        for p in range(npair):
            st_ref[p] = st[p]
        return carry

    lax.fori_loop(0, tt // (C * CPI), chunks, 0)

    y = y_ref[...]
    inv = 1.0 / HEAD_DIM
    mean = _seg_sum(y, seg, npair) * inv
    yc = y - mean
    var = _seg_sum(yc * yc, seg, npair) * inv
    yn = yc * lax.rsqrt(var + RWKV_LN_EPS) * lng_ref[...] + lnb_ref[...]
    o_ref[...] = ((yn + bon_ref[...]) * g_ref[...]).astype(o_ref.dtype)


def _rwkv(rcols, mu, w0, w2, a0, a2, g2, k_k, k_a, r_k, ln_g, ln_b, B, S, prec=HIGHEST):
    T = B * S
    W = RWKV_WIDTH
    tt = 512 if S % 512 == 0 else S
    nt_ = S // tt
    npair = RWKV_HEADS // 2
    pad = LANES - DECAY_LORA
    mu_p = jnp.concatenate([mu[:3 * W], mu[3 * W:3 * W + DECAY_LORA], jnp.zeros((pad,), F32),
                            mu[3 * W + DECAY_LORA:3 * W + DECAY_LORA + AAA_LORA], jnp.zeros((pad,), F32),
                            mu[3 * W + DECAY_LORA + AAA_LORA:]]).reshape(1, RW_PAD_COLS)
    w2p = jnp.concatenate([w2, jnp.zeros((pad, W), F32)], axis=0)
    a2p = jnp.concatenate([a2, jnp.zeros((pad, W), F32)], axis=0)
    vec = lambda t: t.reshape(1, W).astype(F32)
    full = lambda shape: pl.BlockSpec(shape, lambda b, i: (0,) * len(shape))
    sc = lambda: pltpu.VMEM((tt, W), F32)
    return pl.pallas_call(
        functools.partial(_rwkv_kernel, tt=tt, prec=prec),
        out_shape=jax.ShapeDtypeStruct((T, W), BF16),
        grid=(B, nt_),
        in_specs=[
            pl.BlockSpec((tt, RW_PAD_COLS), lambda b, i: (b * nt_ + i, 0)),
            full((1, RW_PAD_COLS)), full((1, W)), full((LANES, W)), full((1, W)), full((LANES, W)),
            full((GATE_LORA, W)), full((1, W)), full((1, W)), full((1, W)), full((1, W)), full((1, W)),
        ],
        out_specs=pl.BlockSpec((tt, W), lambda b, i: (b * nt_ + i, 0)),
        scratch_shapes=[
            pltpu.VMEM((1, RW_PAD_COLS), F32),
            pltpu.VMEM((npair, LANES, LANES), F32),
            sc(), sc(), sc(), sc(), sc(), sc(), sc(), sc(), sc(),
        ],
        compiler_params=_params(("parallel", "arbitrary")),
        name="rwkv7",
    )(rcols, mu_p, vec(w0), w2p, vec(a0), a2p, g2, vec(k_k), vec(k_a), vec(r_k), vec(ln_g), vec(ln_b))


def _outproj_kernel(ya_ref, yb_ref, yc_ref, x_ref, w_ref, gm_ref, g_ref, sh_ref, sc_ref, xo_ref, ho_ref):
    o1 = DIFF_WIDTH
    o2 = o1 + RWKV_WIDTH
    mix = (_mm(ya_ref[...], w_ref[0:o1, :]) + _mm(yb_ref[...], w_ref[o1:o2, :])
           + _mm(yc_ref[...], w_ref[o2:, :]))
    xn = x_ref[...] + gm_ref[...] * mix
    xo_ref[...] = xn
    ms = jnp.mean(xn * xn, axis=-1, keepdims=True)
    y = xn * lax.rsqrt(ms + RMS_EPS) * g_ref[...]
    ho_ref[...] = y * (1.0 + sc_ref[...]) + sh_ref[...]


def _out_proj(ya, yb, yc, x2, w_out, g, mod4, S):
    T, D = x2.shape
    tm = 512 if S % 512 == 0 else S
    nb = S // tm
    row = lambda i: (i, 0)
    modspec = lambda which: pl.BlockSpec((None, None, 1, D), lambda i: (i // nb, which, 0, 0))
    return pl.pallas_call(
        _outproj_kernel,
        out_shape=(jax.ShapeDtypeStruct((T, D), F32), jax.ShapeDtypeStruct((T, D), F32)),
        grid=(T // tm,),
        in_specs=[
            pl.BlockSpec((tm, DIFF_WIDTH), row), pl.BlockSpec((tm, RWKV_WIDTH), row), pl.BlockSpec((tm, FOX_WIDTH), row),
            pl.BlockSpec((tm, D), row),
            pl.BlockSpec((D, D), lambda i: (0, 0)),
            modspec(2),
            pl.BlockSpec((1, D), lambda i: (0, 0)),
            modspec(3), modspec(4),
        ],
        out_specs=(pl.BlockSpec((tm, D), row), pl.BlockSpec((tm, D), row)),
        compiler_params=_params(("parallel",)),
        name="out_proj",
    )(ya, yb, yc, x2, w_out.astype(BF16), mod4, g.reshape(1, D), mod4, mod4)


def _top16(s, iota_f, n):
    vals, poss = [], []
    for _ in range(PEER_TOPK):
        m = jnp.max(s, axis=0, keepdims=True)
        pos = jnp.min(jnp.where(s == m, iota_f, float(n)), axis=0, keepdims=True)
        vals.append(m)
        poss.append(pos)
        s = jnp.where(iota_f == pos, -jnp.inf, s)
    return jnp.concatenate(vals, axis=0), jnp.concatenate(poss, axis=0)


ROUTE_UNROLL = 8
PEER_NCAND = 56


def _peer_cand_tables():
    K = PEER_TOPK
    pairs = [(a, b) for a in range(K) for b in range(K) if (a + 1) * (b + 1) <= K]
    n = PEER_NCAND
    p0 = [[0.0] * K for _ in range(n)]
    p1 = [[0.0] * K for _ in range(n)]
    pad = [0.0] * n
    pos = [float(K * K + r) for r in range(n)]
    for r, (a, b) in enumerate(pairs):
        p0[r][a] = 1.0
        p1[r][b] = 1.0
        pos[r] = float(a * K + b)
    for r in range(len(pairs), n):
        pad[r] = -float("inf")
    col = lambda v: jnp.broadcast_to(jnp.asarray(v, F32)[:, None], (n, LANES))
    return jnp.asarray(p0, F32), jnp.asarray(p1, F32), col(pad), col(pos)


def _peer_route_kernel(h_ref, wq_ref, sk_ref, p0_ref, p1_ref, cpad_ref, cpos_ref, idx_ref, gate_ref, q_scr, e_scr,
                       g_scr):
    K = PEER_TOPK
    hb = h_ref[...].astype(BF16)
    q = _mm(hb, wq_ref[...])
    for hc in range(2 * PEER_HEADS):
        q_scr[hc] = q[:, LANES * hc:LANES * (hc + 1)].astype(BF16)
    iota_n = lax.broadcasted_iota(jnp.int32, (PEER_NKEYS, LANES), 0).astype(F32)
    cpos = cpos_ref[...]

    def one_head(h):
        sv0, si0 = _top16(_nt(sk_ref[2 * h], q_scr[2 * h]), iota_n, PEER_NKEYS)
        sv1, si1 = _top16(_nt(sk_ref[2 * h + 1], q_scr[2 * h + 1]), iota_n, PEER_NKEYS)
        cand = _mm(p0_ref[...], sv0, HIGHEST) + _mm(p1_ref[...], sv1, HIGHEST) + cpad_ref[...]
        cidx = _mm(p0_ref[...], si0) * float(PEER_NKEYS) + _mm(p1_ref[...], si1)
        fv, es = [], []
        for _ in range(K):
            m = jnp.max(cand, axis=0, keepdims=True)
            pos = jnp.min(jnp.where(cand == m, cpos, float(2 * K * K)), axis=0, keepdims=True)
            hit = cpos == pos
            fv.append(m)
            es.append(jnp.max(jnp.where(hit, cidx, -1.0), axis=0, keepdims=True))
            cand = jnp.where(hit, -jnp.inf, cand)
        fv = jnp.concatenate(fv, axis=0)
        ex = jnp.exp(fv - fv[0:1, :])
        g_scr[h] = ex / jnp.sum(ex, axis=0, keepdims=True)
        e_scr[h] = jnp.concatenate(es, axis=0)

    def heads(hh, carry):
        for j in range(ROUTE_UNROLL):
            one_head(hh * ROUTE_UNROLL + j)
        return carry

    lax.fori_loop(0, PEER_HEADS // ROUTE_UNROLL, heads, 0)
    e = e_scr[...].reshape(PEER_HEADS * K, LANES)
    idx_ref[...] = e.T.astype(jnp.int32)
    gate_ref[...] = g_scr[...].reshape(PEER_HEADS * K, LANES)


def _peer_route(h2, wq, subkeys):
    T, D = h2.shape
    tm = LANES
    nq = 2 * PEER_HEADS
    sk = subkeys.reshape(nq, PEER_NKEYS, PEER_HALF).astype(BF16)
    p0, p1, cpad, cpos = _peer_cand_tables()
    const = lambda shape: pl.BlockSpec(shape, lambda i: (0,) * len(shape))
    return pl.pallas_call(
        _peer_route_kernel,
        out_shape=(jax.ShapeDtypeStruct((T, PEER_HEADS * PEER_TOPK), jnp.int32),
                   jax.ShapeDtypeStruct((T // tm, PEER_HEADS * PEER_TOPK, tm), F32)),
        grid=(T // tm,),
        in_specs=[
            pl.BlockSpec((tm, D), lambda i: (i, 0)),
            const((D, nq * PEER_HALF)),
            const((nq, PEER_NKEYS, PEER_HALF)),
            const((PEER_NCAND, PEER_TOPK)), const((PEER_NCAND, PEER_TOPK)),
            const((PEER_NCAND, LANES)), const((PEER_NCAND, LANES)),
        ],
        out_specs=(pl.BlockSpec((tm, PEER_HEADS * PEER_TOPK), lambda i: (i, 0)),
                   pl.BlockSpec((None, PEER_HEADS * PEER_TOPK, tm), lambda i: (i, 0, 0))),
        scratch_shapes=[
            pltpu.VMEM((nq, tm, PEER_HALF), BF16),
            pltpu.VMEM((PEER_HEADS, PEER_TOPK, tm), F32),
            pltpu.VMEM((PEER_HEADS, PEER_TOPK, tm), F32),
        ],
        compiler_params=_params(("parallel",)),
        name="peer_route",
    )(h2, wq.astype(BF16), sk, p0, p1, cpad, cpos)


PEER_G = 16
PEER_SLOTS = PEER_HEADS * PEER_TOPK


def _peer_eval_kernel(idx_ref, idxn_ref, gate_ref, h_ref, x_ref, gf_ref, fg_ref, uv_ref, o_ref, buf, sem, *, final):
    G = PEER_G
    R = G * PEER_SLOTS
    D = D_MODEL
    tiles = PEER_SLOTS // SUBLANES
    i = pl.program_id(0)
    n = pl.num_programs(0)

    def start(ids, off, s, t, u):
        pltpu.make_async_copy(uv_ref.at[ids[off + t * SUBLANES + u]], buf.at[s, t, pl.ds(u, 1), :],
                              sem.at[s]).start(priority=u % 2)

    def wait(s):
        pltpu.make_async_copy(buf.at[s], buf.at[s], sem.at[s]).wait()

    @pl.when(i == 0)
    def _():
        def body(t, carry):
            for u in range(SUBLANES):
                start(idx_ref, 0, 0, t, u)
            return carry
        lax.fori_loop(0, R // SUBLANES, body, 0)

    lane = lax.broadcasted_iota(jnp.int32, (1, LANES), 1)
    tbase = (i % (LANES // (2 * G))) * (2 * G)
    gate = gate_ref[...]
    outs = []
    for s in range(2):
        wait(s)
        nxt_ids, nxt_off = (idx_ref, R) if s == 0 else (idxn_ref, 0)
        for g in range(G):
            for t in range(tiles * g, tiles * (g + 1)):
                for u in range(SUBLANES):
                    start(nxt_ids, nxt_off, 1 - s, t, u)
            w_rows = buf[s, tiles * g:tiles * (g + 1)].reshape(PEER_SLOTS, D)
            u_rows = lax.bitcast_convert_type(w_rows & jnp.uint32(0xFFFF0000), F32)
            prod = u_rows * h_ref[G * s + g:G * s + g + 1, :]
            part = prod[:, 0:LANES]
            for c in range(1, D // LANES):
                part = part + prod[:, LANES * c:LANES * (c + 1)]
            act = jnp.sum(part, axis=1, keepdims=True)
            gcol = jnp.sum(jnp.where(lane == tbase + G * s + g, gate, 0.0), axis=1, keepdims=True)
            coef = gcol * (0.5 * act * (1.0 + lax.erf(act * (2.0 ** -0.5))))
            v_rows = lax.bitcast_convert_type(w_rows << 16, F32)
            outs.append(jnp.sum(v_rows * coef, axis=0, keepdims=True))
    xn = x_ref[...] + gf_ref[...] * jnp.concatenate(outs, axis=0)
    if final:
        ms = jnp.mean(xn * xn, axis=-1, keepdims=True)
        xn = xn * lax.rsqrt(ms + RMS_EPS) * fg_ref[...]
    o_ref[...] = xn

    @pl.when(i == n - 1)
    def _():
        wait(0)


def _peer_eval(eidx, gate_t, h2, x2, mod4, final_g, uv, S, final, ntok=None):
    T, D = x2.shape
    T = T if ntok is None else ntok
    G = PEER_G
    R = G * PEER_SLOTS
    n = T // (2 * G)
    return pl.pallas_call(
        functools.partial(_peer_eval_kernel, final=final),
        out_shape=jax.ShapeDtypeStruct((T, D), F32),
        grid=(n,),
        in_specs=[
            pl.BlockSpec((2 * R,), lambda i: (i,), memory_space=pltpu.SMEM),
            pl.BlockSpec((R,), lambda i: (jnp.minimum(2 * i + 2, 2 * n - 2),), memory_space=pltpu.SMEM),
            pl.BlockSpec((None, PEER_SLOTS, LANES), lambda i: (i // (LANES // (2 * G)), 0, 0)),
            pl.BlockSpec((2 * G, D), lambda i: (i, 0)),
            pl.BlockSpec((2 * G, D), lambda i: (i, 0)),
            pl.BlockSpec((None, None, 1, D), lambda i: (i // (S // (2 * G)), 5, 0, 0)),
            pl.BlockSpec((1, D), lambda i: (0, 0)),
            pl.BlockSpec(memory_space=pl.ANY),
        ],
        out_specs=pl.BlockSpec((2 * G, D), lambda i: (i, 0)),
        scratch_shapes=[pltpu.VMEM((2, R // SUBLANES, SUBLANES, D), jnp.uint32), pltpu.SemaphoreType.DMA((2,))],
        compiler_params=_params(("arbitrary",)),
        name="peer_eval",
    )(eidx.reshape(-1), eidx.reshape(-1), gate_t, h2, x2, mod4, final_g.reshape(1, D), uv)


SC_WORKERS = 32
SC_WINDOW = 128
SC_ROWS = 32
PEER_GD = 16
SC_SHARE_NUM, SC_SHARE_DEN = 9, 16
SC_PARTS = 2


def _sc_gather(tab, idx):
    n = idx.shape[0]
    width = tab.shape[1]
    per = n // SC_WORKERS
    nsub = SC_WINDOW // SC_ROWS
    mesh = plsc.VectorSubcoreMesh(core_axis_name="core", subcore_axis_name="subcore")

    @pl.kernel(out_type=jax.ShapeDtypeStruct((n, width), tab.dtype), mesh=mesh,
               scratch_types=[pltpu.VMEM((SC_WINDOW,), jnp.int32), pltpu.VMEM((2, SC_ROWS, width), tab.dtype),
                              pltpu.SemaphoreType.DMA((2,)), pltpu.SemaphoreType.DMA((2,))])
    def gather(x_hbm, i_hbm, o_hbm, idx_v, rows_v, gsem, wsem):
        wid = lax.axis_index("core") * (SC_WORKERS // 2) + lax.axis_index("subcore")

        def start_gather(k):
            return pltpu.async_copy(x_hbm.at[idx_v.at[pl.ds(k * SC_ROWS, SC_ROWS)]], rows_v.at[k % 2], gsem.at[k % 2])

        @pl.loop(0, per // SC_WINDOW)
        def _(w):
            base = wid * per + w * SC_WINDOW
            pltpu.sync_copy(i_hbm.at[pl.ds(base, SC_WINDOW)], idx_v)
            gat = [start_gather(0)] + [None] * (nsub - 1)
            wrt = [None] * nsub
            for k in range(nsub):
                if k + 1 < nsub:
                    if k >= 1:
                        wrt[k - 1].wait()
                    gat[k + 1] = start_gather(k + 1)
                gat[k].wait()
                wrt[k] = pltpu.async_copy(rows_v.at[k % 2], o_hbm.at[pl.ds(base + k * SC_ROWS, SC_ROWS)],
                                          wsem.at[k % 2])
            for k in range(max(nsub - 2, 0), nsub):
                wrt[k].wait()

    return gather(tab, idx)


def _peer_dense_kernel(rows_ref, gate_ref, h_ref, x_ref, gf_ref, fg_ref, o_ref, *, final, blk0):
    G = PEER_GD
    D = D_MODEL
    i = pl.program_id(0)
    lane = lax.broadcasted_iota(jnp.int32, (1, LANES), 1)
    tbase = ((blk0 + i) % (LANES // G)) * G
    gate = gate_ref[...]
    outs = []
    for g in range(G):
        w_rows = rows_ref[PEER_SLOTS * g:PEER_SLOTS * (g + 1), :]
        u_rows = lax.bitcast_convert_type(w_rows & jnp.uint32(0xFFFF0000), F32)
        prod = u_rows * h_ref[g:g + 1, :]
        part = prod[:, 0:LANES]
        for c in range(1, D // LANES):
            part = part + prod[:, LANES * c:LANES * (c + 1)]
        act = jnp.sum(part, axis=1, keepdims=True)
        gcol = jnp.sum(jnp.where(lane == tbase + g, gate, 0.0), axis=1, keepdims=True)
        coef = gcol * (0.5 * act * (1.0 + lax.erf(act * (2.0 ** -0.5))))
        v_rows = lax.bitcast_convert_type(w_rows << 16, F32)
        outs.append(jnp.sum(v_rows * coef, axis=0, keepdims=True))
    xn = x_ref[...] + gf_ref[...] * jnp.concatenate(outs, axis=0)
    if final:
        ms = jnp.mean(xn * xn, axis=-1, keepdims=True)
        xn = xn * lax.rsqrt(ms + RMS_EPS) * fg_ref[...]
    o_ref[...] = xn


def _peer_eval_dense(rows, gate_t, h2, x2, mod4, final_g, S, final, tok0):
    T, D = x2.shape
    G = PEER_GD
    tb = rows.shape[0] // PEER_SLOTS
    blk0 = tok0 // G
    return pl.pallas_call(
        functools.partial(_peer_dense_kernel, final=final, blk0=blk0),
        out_shape=jax.ShapeDtypeStruct((tb, D), F32),
        grid=(tb // G,),
        in_specs=[
            pl.BlockSpec((G * PEER_SLOTS, D), lambda i: (i, 0)),
            pl.BlockSpec((None, PEER_SLOTS, LANES), lambda i: ((blk0 + i) // (LANES // G), 0, 0)),
            pl.BlockSpec((G, D), lambda i: (blk0 + i, 0)),
            pl.BlockSpec((G, D), lambda i: (blk0 + i, 0)),
            pl.BlockSpec((None, None, 1, D), lambda i: ((blk0 + i) // (S // G), 5, 0, 0)),
            pl.BlockSpec((1, D), lambda i: (0, 0)),
        ],
        out_specs=pl.BlockSpec((G, D), lambda i: (i, 0)),
        compiler_params=_params(("parallel",)),
        name="peer_dense",
    )(rows, gate_t, h2, x2, mod4, final_g.reshape(1, D))


def _pack_uv(u, v):
    hi = lax.bitcast_convert_type(u.astype(BF16), jnp.uint16).astype(jnp.uint32) << 16
    lo = lax.bitcast_convert_type(v.astype(BF16), jnp.uint16).astype(jnp.uint32)
    return (hi | lo).reshape(u.shape[0], 1, u.shape[1])


def kernel(x, c, norm_mix_g, norm_ffn_g, final_norm_g, ada_w, ada_b, w_in, w_out, dif_lam, dif_subln_g, rw_mu, rw_w0,
           rw_w2, rw_a0, rw_a2, rw_g2, rw_kk, rw_ka, rw_rk, rw_ln_g, rw_ln_b, fox_bf, peer_wq, peer_subkeys, peer_u,
           peer_v):
    B, S, D = x.shape
    T = B * S
    depth = ada_w.shape[0]
    x2 = x.reshape(T, D)
    mod = _ada_mod(c, ada_w, ada_b)
    for l in range(depth):
        mod4 = mod[l].reshape(B, 6, 1, D)
        w_pad, w_vt = _pad_w_in(w_in[l])
        dqk, rcols, fqk, fl, vtd, vtf = _in_proj(x2, norm_mix_g[l], mod4, w_pad, w_vt, S)
        ya = _diff_attention(dqk, vtd, dif_lam[l], dif_subln_g[l], l, B, S)
        yb = _rwkv(rcols, rw_mu[l], rw_w0[l], rw_w2[l], rw_a0[l], rw_a2[l], rw_g2[l], rw_kk[l], rw_ka[l],
                   rw_rk[l].reshape(-1), rw_ln_g[l], rw_ln_b[l], B, S, prec=None)
        crep, crow = _fox_cum(fl, fox_bf[l], B, S)
        yc = _fox_attention(fqk, vtf, crep, crow, B, S)
        x2, h2 = _out_proj(ya, yb, yc, x2, w_out[l], norm_ffn_g[l], mod4, S)
        eidx, gate_t = _peer_route(h2, peer_wq[l], peer_subkeys[l])
        uv = _pack_uv(peer_u[l], peer_v[l])
        final = l == depth - 1
        tb = (T * SC_SHARE_NUM // SC_SHARE_DEN) // (SC_WORKERS * SC_WINDOW) * (SC_WORKERS * SC_WINDOW)
        ta = T - tb
        bounds = [ta + tb * j // SC_PARTS for j in range(SC_PARTS + 1)]
        flat = eidx.reshape(-1)
        rows = [_sc_gather(uv.reshape(-1, D), flat[t0 * PEER_SLOTS:t1 * PEER_SLOTS])
                for t0, t1 in zip(bounds[:-1], bounds[1:])]
        outs = [_peer_eval(eidx, gate_t, h2, x2, mod4, final_norm_g, uv, S, final, ntok=ta)]
        outs += [_peer_eval_dense(r, gate_t, h2, x2, mod4, final_norm_g, S, final, t0)
                 for r, t0 in zip(rows, bounds[:-1])]
        x2 = jnp.concatenate(outs, axis=0)
    return x2.reshape(B, S, D)
```

```python
import functools
import math

import jax
import jax.numpy as jnp
from jax import lax
from jax.experimental import pallas as pl
from jax.experimental.pallas import tpu as pltpu
from jax.experimental.pallas import tpu_sc as plsc

F32 = jnp.float32
BF16 = jnp.bfloat16
HIGHEST = lax.Precision.HIGHEST

D_MODEL = 1024
HEAD_DIM = 64
DIFF_HEADS = 6
DIFF_QK_DIM = HEAD_DIM // 2
RWKV_HEADS = 6
FOX_HEADS = 4
DIFF_WIDTH = DIFF_HEADS * HEAD_DIM
RWKV_WIDTH = RWKV_HEADS * HEAD_DIM
FOX_WIDTH = FOX_HEADS * HEAD_DIM
DECAY_LORA = 64
AAA_LORA = 64
GATE_LORA = 128
DIFF_COLS = 3 * DIFF_WIDTH
RWKV_COLS = 3 * RWKV_WIDTH + DECAY_LORA + AAA_LORA + GATE_LORA
PEER_HEADS = 8
PEER_NKEYS = 128
PEER_TOPK = 16
PEER_QDIM = 256
PEER_HALF = PEER_QDIM // 2
RMS_EPS = 1e-6
RWKV_LN_EPS = 64e-5

LANES = 128
SUBLANES = 8
RW_PAD_COLS = 3 * RWKV_WIDTH + 3 * LANES
VMEM_LIMIT = 56 * 1024 * 1024

RW_CHUNK = 64
RW_CHUNKS_PER_ITER = 2


def _params(sem, vmem=VMEM_LIMIT):
    return pltpu.CompilerParams(dimension_semantics=sem, vmem_limit_bytes=vmem)


def _nt(a, b, precision=None):
    return lax.dot_general(a, b, (((1,), (1,)), ((), ())), preferred_element_type=F32, precision=precision)


def _mm(a, b, precision=None):
    return jnp.dot(a, b, preferred_element_type=F32, precision=precision)


def _ada_kernel(c_ref, w_ref, b_ref, o_ref):
    c = c_ref[...]
    ca = c * jax.nn.sigmoid(c)
    o_ref[...] = _mm(ca, w_ref[...], HIGHEST) + b_ref[...]


def _ada_mod(c, ada_w, ada_b):
    L, D, N = ada_w.shape
    B = c.shape[0]
    tn = 1536
    return pl.pallas_call(
        _ada_kernel,
        out_shape=jax.ShapeDtypeStruct((L, B, N), F32),
        grid=(L, N // tn),
        in_specs=[
            pl.BlockSpec((B, D), lambda l, j: (0, 0)),
            pl.BlockSpec((None, D, tn), lambda l, j: (l, 0, j)),
            pl.BlockSpec((None, 1, tn), lambda l, j: (l, 0, j)),
        ],
        out_specs=pl.BlockSpec((None, B, tn), lambda l, j: (l, 0, j)),
        compiler_params=_params(("parallel", "parallel")),
        name="ada_mod",
    )(c, ada_w, ada_b.reshape(L, 1, N))


ATT_T = 512
QK_DIFF = 2 * DIFF_WIDTH
QK_FOX = 2 * FOX_WIDTH
VT_ROWS = DIFF_WIDTH + FOX_WIDTH
IN_PAD_COLS = QK_DIFF + RW_PAD_COLS + QK_FOX + LANES


def _inproj_kernel(x_ref, g_ref, sh_ref, sc_ref, w_ref, wvt_ref, d_ref, r_ref, f_ref, fl_ref, vtd_ref, vtf_ref, *, ta):
    x = x_ref[...]
    ms = jnp.mean(x * x, axis=-1, keepdims=True)
    y = x * lax.rsqrt(ms + RMS_EPS) * g_ref[...]
    h = (y * (1.0 + sc_ref[...]) + sh_ref[...]).astype(BF16)
    o1 = QK_DIFF
    o2 = o1 + RW_PAD_COLS
    o3 = o2 + QK_FOX
    d_ref[...] = _mm(h, w_ref[:, 0:o1]).astype(BF16)
    r_ref[...] = _mm(h, w_ref[:, o1:o2])
    f_ref[...] = _mm(h, w_ref[:, o2:o3]).astype(BF16)
    fl_ref[...] = _mm(h, w_ref[:, o3:o3 + LANES])
    vt = _nt(wvt_ref[...], h).astype(BF16)
    for s in range(x.shape[0] // ta):
        vtd_ref[s] = vt[0:DIFF_WIDTH, ta * s:ta * (s + 1)]
        vtf_ref[s] = vt[DIFF_WIDTH:, ta * s:ta * (s + 1)]


def _in_proj(x2, g, mod4, w_pad, w_vt, S):
    T, D = x2.shape
    tm = 512 if S % 512 == 0 else S
    ta = min(ATT_T, S)
    nb = S // tm
    row = lambda i: (i, 0)
    return pl.pallas_call(
        functools.partial(_inproj_kernel, ta=ta),
        out_shape=(
            jax.ShapeDtypeStruct((T, QK_DIFF), BF16),
            jax.ShapeDtypeStruct((T, RW_PAD_COLS), F32),
            jax.ShapeDtypeStruct((T, QK_FOX), BF16),
            jax.ShapeDtypeStruct((T, LANES), F32),
            jax.ShapeDtypeStruct((T // ta, DIFF_WIDTH, ta), BF16),
            jax.ShapeDtypeStruct((T // ta, FOX_WIDTH, ta), BF16),
        ),
        grid=(T // tm,),
        in_specs=[
            pl.BlockSpec((tm, D), row),
            pl.BlockSpec((1, D), lambda i: (0, 0)),
            pl.BlockSpec((None, None, 1, D), lambda i: (i // nb, 0, 0, 0)),
            pl.BlockSpec((None, None, 1, D), lambda i: (i // nb, 1, 0, 0)),
            pl.BlockSpec((D, IN_PAD_COLS), lambda i: (0, 0)),
            pl.BlockSpec((VT_ROWS, D), lambda i: (0, 0)),
        ],
        out_specs=(
            pl.BlockSpec((tm, QK_DIFF), row),
            pl.BlockSpec((tm, RW_PAD_COLS), row),
            pl.BlockSpec((tm, QK_FOX), row),
            pl.BlockSpec((tm, LANES), row),
            pl.BlockSpec((tm // ta, DIFF_WIDTH, ta), lambda i: (i, 0, 0)),
            pl.BlockSpec((tm // ta, FOX_WIDTH, ta), lambda i: (i, 0, 0)),
        ),
        compiler_params=_params(("parallel",)),
        name="in_proj",
    )(x2, g.reshape(1, D), mod4, mod4, w_pad, w_vt)


def _pad_w_in(w_in):
    D = w_in.shape[0]
    W = RWKV_WIDTH
    o = DIFF_COLS
    z64 = jnp.zeros((D, LANES - DECAY_LORA), w_in.dtype)
    rw = w_in[:, o:o + RWKV_COLS]
    fx = w_in[:, o + RWKV_COLS:]
    zf = jnp.zeros((D, LANES - FOX_HEADS), w_in.dtype)
    w_pad = jnp.concatenate([
        w_in[:, :QK_DIFF],
        rw[:, :3 * W], rw[:, 3 * W:3 * W + DECAY_LORA], z64,
        rw[:, 3 * W + DECAY_LORA:3 * W + DECAY_LORA + AAA_LORA], z64,
        rw[:, 3 * W + DECAY_LORA + AAA_LORA:],
        fx[:, :QK_FOX], fx[:, 3 * FOX_WIDTH:], zf,
    ], axis=1).astype(BF16)
    w_vt = jnp.concatenate([w_in[:, QK_DIFF:o], fx[:, QK_FOX:3 * FOX_WIDTH]], axis=1).T.astype(BF16)
    return w_pad, w_vt


LOG2E = math.log2(math.e)


ACC_ROWS = LANES + 16


def _scaled_q(q, c):
    return (q.astype(F32) * c).astype(BF16)


def _with_ones(vt):
    return jnp.concatenate([vt, jnp.ones((ACC_ROWS - LANES, vt.shape[1]), vt.dtype)], axis=0)


def _flash_step(s2s, vt1, m_ref, acc_ref):
    n = len(s2s)
    m_old = [m_ref[x] for x in range(n)]
    m_new = [jnp.maximum(m_old[x], jnp.max(s2s[x], axis=0, keepdims=True)) for x in range(n)]
    alpha = [jnp.exp2(m_old[x] - m_new[x]) for x in range(n)]
    p = [jnp.exp2(s2s[x] - m_new[x]).astype(BF16) for x in range(n)]
    pv = [_mm(vt1, p[x]) for x in range(n)]
    for x in range(n):
        acc_ref[x] = alpha[x] * acc_ref[x] + pv[x]
        m_ref[x] = m_new[x]


def _diff_attn_kernel(lam_ref, g_ref, q_ref, k_ref, vt_ref, o_ref, m_ref, acc_ref, *, tq, lam_init):
    i = pl.program_id(2)
    lane = lax.broadcasted_iota(jnp.int32, (1, LANES), 1)
    q = _scaled_q(q_ref[...], (DIFF_QK_DIM ** -0.5) * LOG2E)
    zero = jnp.zeros_like(q)
    qm = [jnp.where((lane >= DIFF_QK_DIM * x) & (lane < DIFF_QK_DIM * (x + 1)), q, zero) for x in range(4)]
    m_ref[...] = jnp.full(m_ref.shape, -jnp.inf, F32)
    acc_ref[...] = jnp.zeros(acc_ref.shape, F32)

    def step(j, diag):
        k = k_ref[pl.ds(pl.multiple_of(j * tq, tq), tq), :]
        vt1 = _with_ones(vt_ref[j])
        if diag:
            keep = (lax.broadcasted_iota(jnp.int32, (tq, tq), 1) >= lax.broadcasted_iota(jnp.int32, (tq, tq), 0))
        s2s = [_nt(k, qm[x]) for x in range(4)]
        if diag:
            s2s = [jnp.where(keep, s2, -jnp.inf) for s2 in s2s]
        _flash_step(s2s, vt1, m_ref, acc_ref)

    def body(j, carry):
        step(j, False)
        return carry

    lax.fori_loop(0, i, body, 0)
    step(i, True)

    lp = lam_ref[...]
    lam = (jnp.exp(jnp.sum(lp[0:1] * lp[1:2], axis=-1, keepdims=True))
           - jnp.exp(jnp.sum(lp[2:3] * lp[3:4], axis=-1, keepdims=True)) + lam_init)
    sm = [acc_ref[x, 0:LANES, :] / acc_ref[x, LANES:LANES + 1, :] for x in range(4)]
    outs = [sm[2 * hh] - lam * sm[2 * hh + 1] for hh in range(2)]
    row = lax.broadcasted_iota(jnp.int32, (LANES, 1), 0)
    o = jnp.where(row < HEAD_DIM, outs[0], outs[1])
    sq = o * o
    ms = jnp.where(row < HEAD_DIM, jnp.sum(sq[0:HEAD_DIM], axis=0, keepdims=True),
                   jnp.sum(sq[HEAD_DIM:], axis=0, keepdims=True)) * (1.0 / HEAD_DIM)
    y = o * lax.rsqrt(ms + RMS_EPS) * g_ref[...] * (1.0 - lam_init)
    o_ref[...] = y.T.astype(o_ref.dtype)


def _diff_attention(dqk, vtd, lam_params, subln_g, layer_idx, B, S):
    T = B * S
    tq = min(ATT_T, S)
    nq = S // tq
    npair = DIFF_HEADS // 2
    lam_init = 0.8 - 0.6 * math.exp(-0.3 * layer_idx)
    g2 = jnp.concatenate([subln_g, subln_g]).reshape(LANES, 1).astype(F32)
    return pl.pallas_call(
        functools.partial(_diff_attn_kernel, tq=tq, lam_init=lam_init),
        out_shape=jax.ShapeDtypeStruct((T, DIFF_WIDTH), BF16),
        grid=(B, npair, nq),
        in_specs=[
            pl.BlockSpec((4, DIFF_QK_DIM), lambda b, p, i: (0, 0)),
            pl.BlockSpec((LANES, 1), lambda b, p, i: (0, 0)),
            pl.BlockSpec((tq, LANES), lambda b, p, i: (b * nq + i, p)),
            pl.BlockSpec((S, LANES), lambda b, p, i: (b, npair + p)),
            pl.BlockSpec((nq, LANES, tq), lambda b, p, i: (b, p, 0)),
        ],
        out_specs=pl.BlockSpec((tq, LANES), lambda b, p, i: (b * nq + i, p)),
        scratch_shapes=[
            pltpu.VMEM((4, 1, tq), F32),
            pltpu.VMEM((4, ACC_ROWS, tq), F32),
        ],
        compiler_params=_params(("parallel", "parallel", "arbitrary")),
        name="diff_attn",
    )(lam_params, g2, dqk, dqk, vtd)


def _fox_cum_kernel(f_ref, b_ref, rep_ref, row_ref, *, S, tc):
    rr = lax.broadcasted_iota(jnp.int32, (tc, tc), 0)
    cc = lax.broadcasted_iota(jnp.int32, (tc, tc), 1)
    tri = (rr >= cc).astype(F32)
    sel_r = lax.broadcasted_iota(jnp.int32, (LANES, LANES), 0)
    carry = jnp.zeros((1, LANES), F32)
    for c in range(S // tc):
        z = f_ref[c * tc:(c + 1) * tc, :] + b_ref[...]
        logf = -(jnp.maximum(-z, 0.0) + jnp.log(1.0 + jnp.exp(-jnp.abs(z))))
        cum = _mm(tri, logf, HIGHEST) + carry
        carry = cum[tc - 1:tc, :]
        row_ref[:, c * tc:(c + 1) * tc] = cum.T[0:8, :]
        for h in range(FOX_HEADS):
            rep_ref[h, c * tc:(c + 1) * tc, :] = _mm(cum, (sel_r == h).astype(F32), HIGHEST)


def _fox_cum(fl, b_f, B, S):
    tc = 256 if S % 256 == 0 else S
    bpad = jnp.zeros((1, LANES), F32).at[0, :FOX_HEADS].set(b_f.astype(F32))
    return pl.pallas_call(
        functools.partial(_fox_cum_kernel, S=S, tc=tc),
        out_shape=(jax.ShapeDtypeStruct((B, FOX_HEADS, S, LANES), F32), jax.ShapeDtypeStruct((B, 8, S), F32)),
        grid=(B,),
        in_specs=[pl.BlockSpec((S, LANES), lambda b: (b, 0)), pl.BlockSpec((1, LANES), lambda b: (0, 0))],
        out_specs=(pl.BlockSpec((None, FOX_HEADS, S, LANES), lambda b: (b, 0, 0, 0)),
                   pl.BlockSpec((None, 8, S), lambda b: (b, 0, 0))),
        compiler_params=_params(("parallel",)),
        name="fox_cum",
    )(fl, bpad)


def _fox_attn_kernel(q_ref, k_ref, vt_ref, c0_ref, c1_ref, cr_ref, o_ref, m_ref, acc_ref, *, tq):
    p_id = pl.program_id(1)
    i = pl.program_id(2)
    lane = lax.broadcasted_iota(jnp.int32, (1, LANES), 1)
    q = _scaled_q(q_ref[...], (HEAD_DIM ** -0.5) * LOG2E)
    zero = jnp.zeros_like(q)
    qm = [jnp.where((lane >= HEAD_DIM * x) & (lane < HEAD_DIM * (x + 1)), q, zero) for x in range(2)]
    ck_refs = (c0_ref, c1_ref)
    cq = [cr_ref[2 * p_id + x, pl.ds(i, 1), :] for x in range(2)]
    m_ref[...] = jnp.full(m_ref.shape, -jnp.inf, F32)
    acc_ref[...] = jnp.zeros(acc_ref.shape, F32)

    def step(j, diag):
        off = pl.multiple_of(j * tq, tq)
        k = k_ref[pl.ds(off, tq), :]
        vt1 = _with_ones(vt_ref[j])
        if diag:
            keep = (lax.broadcasted_iota(jnp.int32, (tq, tq), 1) >= lax.broadcasted_iota(jnp.int32, (tq, tq), 0))
        s2s = []
        for x in range(2):
            ck = ck_refs[x][pl.ds(off, tq), :]
            bias = (cq[x] - jnp.concatenate([ck] * (tq // LANES), axis=1)) * LOG2E
            s2s.append(_nt(k, qm[x]) + bias)
        if diag:
            s2s = [jnp.where(keep, s2, -jnp.inf) for s2 in s2s]
        _flash_step(s2s, vt1, m_ref, acc_ref)

    def body(j, carry):
        step(j, False)
        return carry

    lax.fori_loop(0, i, body, 0)
    step(i, True)
    row = lax.broadcasted_iota(jnp.int32, (LANES, 1), 0)
    sm = [acc_ref[x, 0:LANES, :] / acc_ref[x, LANES:LANES + 1, :] for x in range(2)]
    o = jnp.where(row < HEAD_DIM, sm[0], sm[1])
    o_ref[...] = o.T.astype(o_ref.dtype)


def _fox_attention(fqk, vtf, crep, crow, B, S):
    T = B * S
    tq = min(ATT_T, S)
    nq = S // tq
    npair = FOX_HEADS // 2
    crow4 = crow.reshape(B, 8, nq, tq)
    rep = lambda x: pl.BlockSpec((None, None, S, LANES), lambda b, p, i: (b, 2 * p + x, 0, 0))
    return pl.pallas_call(
        functools.partial(_fox_attn_kernel, tq=tq),
        out_shape=jax.ShapeDtypeStruct((T, FOX_WIDTH), BF16),
        grid=(B, npair, nq),
        in_specs=[
            pl.BlockSpec((tq, LANES), lambda b, p, i: (b * nq + i, p)),
            pl.BlockSpec((S, LANES), lambda b, p, i: (b, npair + p)),
            pl.BlockSpec((nq, LANES, tq), lambda b, p, i: (b, p, 0)),
            rep(0), rep(1),
            pl.BlockSpec((None, 8, nq, tq), lambda b, p, i: (b, 0, 0, 0)),
        ],
        out_specs=pl.BlockSpec((tq, LANES), lambda b, p, i: (b * nq + i, p)),
        scratch_shapes=[
            pltpu.VMEM((2, 1, tq), F32),
            pltpu.VMEM((2, ACC_ROWS, tq), F32),
        ],
        compiler_params=_params(("parallel", "parallel", "arbitrary")),
        name="fox_attn",
    )(fqk, fqk, vtf, crep, crep, crow4)


def _split3(x):
    hi = x.astype(BF16)
    r1 = x - hi.astype(F32)
    mid = r1.astype(BF16)
    lo = (r1 - mid.astype(F32)).astype(BF16)
    return hi, mid, lo


def _seg_sum(x, seg, npair):
    parts = _split3(x)
    return jnp.concatenate(
        [sum(_mm(t[:, LANES * p:LANES * (p + 1)], seg) for t in parts) for p in range(npair)], axis=1)


def _rwkv_kernel(x_ref, mu_ref, w0_ref, w2_ref, a0_ref, a2_ref, g2_ref, kk_ref, ka_ref, rk_ref, lng_ref, lnb_ref,
                 o_ref, carry_ref, st_ref, at_ref, rt_ref, bt_ref, kt_ref, v_ref, wc_ref, y_ref, g_ref, bon_ref,
                 *, tt, prec):
    i = pl.program_id(1)
    W = RWKV_WIDTH
    C = RW_CHUNK
    CPI = RW_CHUNKS_PER_ITER
    npair = RWKV_HEADS // 2

    @pl.when(i == 0)
    def _():
        carry_ref[...] = jnp.zeros(carry_ref.shape, F32)
        st_ref[...] = jnp.zeros(st_ref.shape, F32)

    x = x_ref[...]
    rows = lax.broadcasted_iota(jnp.int32, (tt, 1), 0)
    prev = jnp.where(rows == 0, carry_ref[...], pltpu.roll(x, 1, axis=0))
    carry_ref[...] = x[tt - 1:tt, :]
    xs = x + (prev - x) * mu_ref[...]
    r = xs[:, 0:W]
    k = xs[:, W:2 * W]
    v = xs[:, 2 * W:3 * W]
    xw = xs[:, 3 * W:3 * W + LANES]
    xa = xs[:, 3 * W + LANES:3 * W + 2 * LANES]
    xg = xs[:, 3 * W + 2 * LANES:]
    wl = w0_ref[...] + _mm(jnp.tanh(xw), w2_ref[...], HIGHEST)
    w = -(jnp.maximum(-wl, 0.0) + jnp.log(1.0 + jnp.exp(-jnp.abs(wl)))) - 0.5
    logdec = -jnp.exp(w)
    a = jax.nn.sigmoid(a0_ref[...] + _mm(xa, a2_ref[...], HIGHEST))
    g_ref[...] = _mm(jax.nn.sigmoid(xg), g2_ref[...], HIGHEST)

    r_i = lax.broadcasted_iota(jnp.int32, (LANES, LANES), 0)
    c_i = lax.broadcasted_iota(jnp.int32, (LANES, LANES), 1)
    seg = ((r_i // HEAD_DIM) == (c_i // HEAD_DIM)).astype(BF16)
    kkv = k * kk_ref[...]
    kkn = kkv / jnp.maximum(jnp.sqrt(_seg_sum(kkv * kkv, seg, npair)), 1e-12)
    k2 = k * (1.0 + (a - 1.0) * ka_ref[...])
    bon_ref[...] = _seg_sum(r * k2 * rk_ref[...], seg, npair) * v

    rt_i = lax.broadcasted_iota(jnp.int32, (tt, tt), 0)
    ct_i = lax.broadcasted_iota(jnp.int32, (tt, tt), 1)
    tri = (((rt_i // C) == (ct_i // C)) & (rt_i >= ct_i)).astype(BF16)
    cum = sum(_mm(tri, t) for t in _split3(logdec))
    winv = jnp.exp(-cum)
    wcum = jnp.exp(cum)
    at_ref[...] = -kkn * jnp.exp(cum - logdec)
    bt_ref[...] = kkn * a * winv
    kt_ref[...] = k2 * winv
    rt_ref[...] = r * wcum
    v_ref[...] = v
    wc_ref[...] = wcum

    lane = lax.broadcasted_iota(jnp.int32, (1, LANES), 1)
    lo = lane < HEAD_DIM
    tpos = r_i % C
    ipos = c_i % C
    strict = tpos > ipos
    incl = tpos >= ipos
    eye = r_i == c_i

    def stack2(m):
        return jnp.concatenate([jnp.where(lo, m, 0.0), jnp.where(lo, 0.0, m)], axis=0)

    def cast(m):
        return m if prec is not None else m.astype(BF16)

    def mm(p, q):
        return _mm(cast(p), cast(q), prec)

    def nt(p, q):
        return _nt(cast(p), cast(q), prec)

    def chunks(cc, carry):
        units = [(ci, p) for ci in range(CPI) for p in range(npair)]
        r0 = [pl.multiple_of((cc * CPI + ci) * C, C) for ci in range(CPI)]
        ld = lambda ref, ci, p: stack2(ref[pl.ds(r0[ci], C), LANES * p:LANES * (p + 1)])
        a2 = {u: ld(at_ref, *u) for u in units}
        r2 = {u: ld(rt_ref, *u) for u in units}
        b2 = {u: ld(bt_ref, *u) for u in units}
        k2s = {u: ld(kt_ref, *u) for u in units}
        v2 = {u: ld(v_ref, *u) for u in units}
        wl = {(ci, p): wc_ref[pl.ds(pl.multiple_of(r0[ci] + C - 8, 8), 8), LANES * p:LANES * (p + 1)][7:8, :]
              for (ci, p) in units}
        ar = {u: jnp.concatenate([a2[u], r2[u]], axis=0) for u in units}
        mb = {u: nt(ar[u], b2[u]) for u in units}
        mk = {u: nt(ar[u], k2s[u]) for u in units}
        lab = {u: jnp.where(strict, mb[u][0:LANES], 0.0) for u in units}
        mrb = {u: jnp.where(incl, mb[u][LANES:], 0.0) for u in units}
        lak = {u: jnp.where(strict, mk[u][0:LANES], 0.0) for u in units}
        mrk = {u: jnp.where(incl, mk[u][LANES:], 0.0) for u in units}
        xx = {u: jnp.concatenate([a2[u], mm(lak[u], v2[u])], axis=1) for u in units}
        lp = lab
        for it in range(6):
            xx = {u: xx[u] + mm(lp[u], xx[u]) for u in units}
            if it < 5:
                lp = {u: mm(lp[u], lp[u]) for u in units}
        mq = {u: mm(mrb[u], xx[u]) for u in units}
        mv = {u: mm(mrk[u], v2[u]) for u in units}
        bx = {u: mm((b2[u] * wl[u]).T, xx[u]) for u in units}
        kv = {u: mm((k2s[u] * wl[u]).T, v2[u]) for u in units}
        st = [st_ref[p] for p in range(npair)]
        for (ci, p) in units:
            u = (ci, p)
            q1 = r2[u] + mq[u][:, 0:LANES]
            q2 = mq[u][:, LANES:] + mv[u]
            gmat = jnp.where(eye, jnp.broadcast_to(wl[u], (LANES, LANES)), 0.0) + bx[u][:, 0:LANES]
            hmat = bx[u][:, LANES:] + kv[u]
            gs = mm(jnp.concatenate([gmat, q1], axis=0), st[p])
            st[p] = gs[0:LANES] + hmat
            yy = gs[LANES:] + q2
            y_ref[pl.ds(r0[ci], C), LANES * p:LANES * (p + 1)] = yy[0:C] + yy[C:]
        for p in range(npair):
            st_ref[p] = st[p]
        return carry

    lax.fori_loop(0, tt // (C * CPI), chunks, 0)

    y = y_ref[...]
    inv = 1.0 / HEAD_DIM
    mean = _seg_sum(y, seg, npair) * inv
    yc = y - mean
    var = _seg_sum(yc * yc, seg, npair) * inv
    yn = yc * lax.rsqrt(var + RWKV_LN_EPS) * lng_ref[...] + lnb_ref[...]
    o_ref[...] = ((yn + bon_ref[...]) * g_ref[...]).astype(o_ref.dtype)


def _rwkv(rcols, mu, w0, w2, a0, a2, g2, k_k, k_a, r_k, ln_g, ln_b, B, S, prec=HIGHEST):
    T = B * S
    W = RWKV_WIDTH
    tt = 512 if S % 512 == 0 else S
    nt_ = S // tt
    npair = RWKV_HEADS // 2
    pad = LANES - DECAY_LORA
    mu_p = jnp.concatenate([mu[:3 * W], mu[3 * W:3 * W + DECAY_LORA], jnp.zeros((pad,), F32),
                            mu[3 * W + DECAY_LORA:3 * W + DECAY_LORA + AAA_LORA], jnp.zeros((pad,), F32),
                            mu[3 * W + DECAY_LORA + AAA_LORA:]]).reshape(1, RW_PAD_COLS)
    w2p = jnp.concatenate([w2, jnp.zeros((pad, W), F32)], axis=0)
    a2p = jnp.concatenate([a2, jnp.zeros((pad, W), F32)], axis=0)
    vec = lambda t: t.reshape(1, W).astype(F32)
    full = lambda shape: pl.BlockSpec(shape, lambda b, i: (0,) * len(shape))
    sc = lambda: pltpu.VMEM((tt, W), F32)
    return pl.pallas_call(
        functools.partial(_rwkv_kernel, tt=tt, prec=prec),
        out_shape=jax.ShapeDtypeStruct((T, W), BF16),
        grid=(B, nt_),
        in_specs=[
            pl.BlockSpec((tt, RW_PAD_COLS), lambda b, i: (b * nt_ + i, 0)),
            full((1, RW_PAD_COLS)), full((1, W)), full((LANES, W)), full((1, W)), full((LANES, W)),
            full((GATE_LORA, W)), full((1, W)), full((1, W)), full((1, W)), full((1, W)), full((1, W)),
        ],
        out_specs=pl.BlockSpec((tt, W), lambda b, i: (b * nt_ + i, 0)),
        scratch_shapes=[
            pltpu.VMEM((1, RW_PAD_COLS), F32),
            pltpu.VMEM((npair, LANES, LANES), F32),
            sc(), sc(), sc(), sc(), sc(), sc(), sc(), sc(), sc(),
        ],
        compiler_params=_params(("parallel", "arbitrary")),
        name="rwkv7",
    )(rcols, mu_p, vec(w0), w2p, vec(a0), a2p, g2, vec(k_k), vec(k_a), vec(r_k), vec(ln_g), vec(ln_b))


def _outproj_kernel(ya_ref, yb_ref, yc_ref, x_ref, w_ref, gm_ref, g_ref, sh_ref, sc_ref, xo_ref, ho_ref):
    o1 = DIFF_WIDTH
    o2 = o1 + RWKV_WIDTH
    mix = (_mm(ya_ref[...], w_ref[0:o1, :]) + _mm(yb_ref[...], w_ref[o1:o2, :])
           + _mm(yc_ref[...], w_ref[o2:, :]))
    xn = x_ref[...] + gm_ref[...] * mix
    xo_ref[...] = xn
    ms = jnp.mean(xn * xn, axis=-1, keepdims=True)
    y = xn * lax.rsqrt(ms + RMS_EPS) * g_ref[...]
    ho_ref[...] = y * (1.0 + sc_ref[...]) + sh_ref[...]


def _out_proj(ya, yb, yc, x2, w_out, g, mod4, S):
    T, D = x2.shape
    tm = 512 if S % 512 == 0 else S
    nb = S // tm
    row = lambda i: (i, 0)
    modspec = lambda which: pl.BlockSpec((None, None, 1, D), lambda i: (i // nb, which, 0, 0))
    return pl.pallas_call(
        _outproj_kernel,
        out_shape=(jax.ShapeDtypeStruct((T, D), F32), jax.ShapeDtypeStruct((T, D), F32)),
        grid=(T // tm,),
        in_specs=[
            pl.BlockSpec((tm, DIFF_WIDTH), row), pl.BlockSpec((tm, RWKV_WIDTH), row), pl.BlockSpec((tm, FOX_WIDTH), row),
            pl.BlockSpec((tm, D), row),
            pl.BlockSpec((D, D), lambda i: (0, 0)),
            modspec(2),
            pl.BlockSpec((1, D), lambda i: (0, 0)),
            modspec(3), modspec(4),
        ],
        out_specs=(pl.BlockSpec((tm, D), row), pl.BlockSpec((tm, D), row)),
        compiler_params=_params(("parallel",)),
        name="out_proj",
    )(ya, yb, yc, x2, w_out.astype(BF16), mod4, g.reshape(1, D), mod4, mod4)


def _top16(s, iota_f, n):
    vals, poss = [], []
    for _ in range(PEER_TOPK):
        m = jnp.max(s, axis=0, keepdims=True)
        pos = jnp.min(jnp.where(s == m, iota_f, float(n)), axis=0, keepdims=True)
        vals.append(m)
        poss.append(pos)
        s = jnp.where(iota_f == pos, -jnp.inf, s)
    return jnp.concatenate(vals, axis=0), jnp.concatenate(poss, axis=0)


ROUTE_UNROLL = 8
PEER_NCAND = 56


def _peer_cand_tables():
    K = PEER_TOPK
    pairs = [(a, b) for a in range(K) for b in range(K) if (a + 1) * (b + 1) <= K]
    n = PEER_NCAND
    p0 = [[0.0] * K for _ in range(n)]
    p1 = [[0.0] * K for _ in range(n)]
    pad = [0.0] * n
    pos = [float(K * K + r) for r in range(n)]
    for r, (a, b) in enumerate(pairs):
        p0[r][a] = 1.0
        p1[r][b] = 1.0
        pos[r] = float(a * K + b)
    for r in range(len(pairs), n):
        pad[r] = -float("inf")
    col = lambda v: jnp.broadcast_to(jnp.asarray(v, F32)[:, None], (n, LANES))
    return jnp.asarray(p0, F32), jnp.asarray(p1, F32), col(pad), col(pos)


def _peer_route_kernel(h_ref, wq_ref, sk_ref, p0_ref, p1_ref, cpad_ref, cpos_ref, idx_ref, gate_ref, q_scr, e_scr,
                       g_scr):
    K = PEER_TOPK
    hb = h_ref[...].astype(BF16)
    q = _mm(hb, wq_ref[...])
    for hc in range(2 * PEER_HEADS):
        q_scr[hc] = q[:, LANES * hc:LANES * (hc + 1)].astype(BF16)
    iota_n = lax.broadcasted_iota(jnp.int32, (PEER_NKEYS, LANES), 0).astype(F32)
    cpos = cpos_ref[...]

    def one_head(h):
        sv0, si0 = _top16(_nt(sk_ref[2 * h], q_scr[2 * h]), iota_n, PEER_NKEYS)
        sv1, si1 = _top16(_nt(sk_ref[2 * h + 1], q_scr[2 * h + 1]), iota_n, PEER_NKEYS)
        cand = _mm(p0_ref[...], sv0, HIGHEST) + _mm(p1_ref[...], sv1, HIGHEST) + cpad_ref[...]
        cidx = _mm(p0_ref[...], si0) * float(PEER_NKEYS) + _mm(p1_ref[...], si1)
        fv, es = [], []
        for _ in range(K):
            m = jnp.max(cand, axis=0, keepdims=True)
            pos = jnp.min(jnp.where(cand == m, cpos, float(2 * K * K)), axis=0, keepdims=True)
            hit = cpos == pos
            fv.append(m)
            es.append(jnp.max(jnp.where(hit, cidx, -1.0), axis=0, keepdims=True))
            cand = jnp.where(hit, -jnp.inf, cand)
        fv = jnp.concatenate(fv, axis=0)
        ex = jnp.exp(fv - fv[0:1, :])
        g_scr[h] = ex / jnp.sum(ex, axis=0, keepdims=True)
        e_scr[h] = jnp.concatenate(es, axis=0)

    def heads(hh, carry):
        for j in range(ROUTE_UNROLL):
            one_head(hh * ROUTE_UNROLL + j)
        return carry

    lax.fori_loop(0, PEER_HEADS // ROUTE_UNROLL, heads, 0)
    e = e_scr[...].reshape(PEER_HEADS * K, LANES)
    idx_ref[...] = e.T.astype(jnp.int32)
    gate_ref[...] = g_scr[...].reshape(PEER_HEADS * K, LANES)


def _peer_route(h2, wq, subkeys, tok0=0, ntok=None):
    T, D = h2.shape
    T = T - tok0 if ntok is None else ntok
    tm = LANES
    blk0 = tok0 // tm
    nq = 2 * PEER_HEADS
    sk = subkeys.reshape(nq, PEER_NKEYS, PEER_HALF).astype(BF16)
    p0, p1, cpad, cpos = _peer_cand_tables()
    const = lambda shape: pl.BlockSpec(shape, lambda i: (0,) * len(shape))
    return pl.pallas_call(
        _peer_route_kernel,
        out_shape=(jax.ShapeDtypeStruct((T, PEER_HEADS * PEER_TOPK), jnp.int32),
                   jax.ShapeDtypeStruct((T // tm, PEER_HEADS * PEER_TOPK, tm), F32)),
        grid=(T // tm,),
        in_specs=[
            pl.BlockSpec((tm, D), lambda i: (blk0 + i, 0)),
            const((D, nq * PEER_HALF)),
            const((nq, PEER_NKEYS, PEER_HALF)),
            const((PEER_NCAND, PEER_TOPK)), const((PEER_NCAND, PEER_TOPK)),
            const((PEER_NCAND, LANES)), const((PEER_NCAND, LANES)),
        ],
        out_specs=(pl.BlockSpec((tm, PEER_HEADS * PEER_TOPK), lambda i: (i, 0)),
                   pl.BlockSpec((None, PEER_HEADS * PEER_TOPK, tm), lambda i: (i, 0, 0))),
        scratch_shapes=[
            pltpu.VMEM((nq, tm, PEER_HALF), BF16),
            pltpu.VMEM((PEER_HEADS, PEER_TOPK, tm), F32),
            pltpu.VMEM((PEER_HEADS, PEER_TOPK, tm), F32),
        ],
        compiler_params=_params(("parallel",)),
        name="peer_route",
    )(h2, wq.astype(BF16), sk, p0, p1, cpad, cpos)


PEER_G = 16
PEER_SLOTS = PEER_HEADS * PEER_TOPK


def _peer_eval_kernel(idx_ref, idxn_ref, gate_ref, h_ref, x_ref, gf_ref, fg_ref, uv_ref, o_ref, buf, sem, *, final):
    G = PEER_G
    R = G * PEER_SLOTS
    D = D_MODEL
    tiles = PEER_SLOTS // SUBLANES
    i = pl.program_id(0)
    n = pl.num_programs(0)

    def start(ids, off, s, t, u):
        pltpu.make_async_copy(uv_ref.at[ids[off + t * SUBLANES + u]], buf.at[s, t, pl.ds(u, 1), :],
                              sem.at[s]).start(priority=u % 2)

    def wait(s):
        pltpu.make_async_copy(buf.at[s], buf.at[s], sem.at[s]).wait()

    @pl.when(i == 0)
    def _():
        def body(t, carry):
            for u in range(SUBLANES):
                start(idx_ref, 0, 0, t, u)
            return carry
        lax.fori_loop(0, R // SUBLANES, body, 0)

    lane = lax.broadcasted_iota(jnp.int32, (1, LANES), 1)
    tbase = (i % (LANES // (2 * G))) * (2 * G)
    gate = gate_ref[...]
    outs = []
    for s in range(2):
        wait(s)
        nxt_ids, nxt_off = (idx_ref, R) if s == 0 else (idxn_ref, 0)
        for g in range(G):
            for t in range(tiles * g, tiles * (g + 1)):
                for u in range(SUBLANES):
                    start(nxt_ids, nxt_off, 1 - s, t, u)
            w_rows = buf[s, tiles * g:tiles * (g + 1)].reshape(PEER_SLOTS, D)
            u_rows = lax.bitcast_convert_type(w_rows & jnp.uint32(0xFFFF0000), F32)
            prod = u_rows * h_ref[G * s + g:G * s + g + 1, :]
            part = prod[:, 0:LANES]
            for c in range(1, D // LANES):
                part = part + prod[:, LANES * c:LANES * (c + 1)]
            act = jnp.sum(part, axis=1, keepdims=True)
            gcol = jnp.sum(jnp.where(lane == tbase + G * s + g, gate, 0.0), axis=1, keepdims=True)
            coef = gcol * (0.5 * act * (1.0 + lax.erf(act * (2.0 ** -0.5))))
            v_rows = lax.bitcast_convert_type(w_rows << 16, F32)
            outs.append(jnp.sum(v_rows * coef, axis=0, keepdims=True))
    xn = x_ref[...] + gf_ref[...] * jnp.concatenate(outs, axis=0)
    if final:
        ms = jnp.mean(xn * xn, axis=-1, keepdims=True)
        xn = xn * lax.rsqrt(ms + RMS_EPS) * fg_ref[...]
    o_ref[...] = xn

    @pl.when(i == n - 1)
    def _():
        wait(0)


def _peer_eval(eidx, gate_t, h2, x2, mod4, final_g, uv, S, final, ntok=None):
    T, D = x2.shape
    T = T if ntok is None else ntok
    G = PEER_G
    R = G * PEER_SLOTS
    n = T // (2 * G)
    return pl.pallas_call(
        functools.partial(_peer_eval_kernel, final=final),
        out_shape=jax.ShapeDtypeStruct((T, D), F32),
        grid=(n,),
        in_specs=[
            pl.BlockSpec((2 * R,), lambda i: (i,), memory_space=pltpu.SMEM),
            pl.BlockSpec((R,), lambda i: (jnp.minimum(2 * i + 2, 2 * n - 2),), memory_space=pltpu.SMEM),
            pl.BlockSpec((None, PEER_SLOTS, LANES), lambda i: (i // (LANES // (2 * G)), 0, 0)),
            pl.BlockSpec((2 * G, D), lambda i: (i, 0)),
            pl.BlockSpec((2 * G, D), lambda i: (i, 0)),
            pl.BlockSpec((None, None, 1, D), lambda i: (i // (S // (2 * G)), 5, 0, 0)),
            pl.BlockSpec((1, D), lambda i: (0, 0)),
            pl.BlockSpec(memory_space=pl.ANY),
        ],
        out_specs=pl.BlockSpec((2 * G, D), lambda i: (i, 0)),
        scratch_shapes=[pltpu.VMEM((2, R // SUBLANES, SUBLANES, D), jnp.uint32), pltpu.SemaphoreType.DMA((2,))],
        compiler_params=_params(("arbitrary",)),
        name="peer_eval",
    )(eidx.reshape(-1), eidx.reshape(-1), gate_t, h2, x2, mod4, final_g.reshape(1, D), uv)


SC_WORKERS = 32
SC_WINDOW = 128
SC_ROWS = 32
PEER_GD = 16
SC_SHARE_NUM, SC_SHARE_DEN = 9, 16
SC_PARTS = 2


def _sc_gather(tab, idx):
    n = idx.shape[0]
    width = tab.shape[1]
    per = n // SC_WORKERS
    nsub = SC_WINDOW // SC_ROWS
    mesh = plsc.VectorSubcoreMesh(core_axis_name="core", subcore_axis_name="subcore")

    @pl.kernel(out_type=jax.ShapeDtypeStruct((n, width), tab.dtype), mesh=mesh,
               scratch_types=[pltpu.VMEM((SC_WINDOW,), jnp.int32), pltpu.VMEM((2, SC_ROWS, width), tab.dtype),
                              pltpu.SemaphoreType.DMA((2,)), pltpu.SemaphoreType.DMA((2,))])
    def gather(x_hbm, i_hbm, o_hbm, idx_v, rows_v, gsem, wsem):
        wid = lax.axis_index("core") * (SC_WORKERS // 2) + lax.axis_index("subcore")

        def start_gather(k):
            return pltpu.async_copy(x_hbm.at[idx_v.at[pl.ds(k * SC_ROWS, SC_ROWS)]], rows_v.at[k % 2], gsem.at[k % 2])

        @pl.loop(0, per // SC_WINDOW)
        def _(w):
            base = wid * per + w * SC_WINDOW
            pltpu.sync_copy(i_hbm.at[pl.ds(base, SC_WINDOW)], idx_v)
            gat = [start_gather(0)] + [None] * (nsub - 1)
            wrt = [None] * nsub
            for k in range(nsub):
                if k + 1 < nsub:
                    if k >= 1:
                        wrt[k - 1].wait()
                    gat[k + 1] = start_gather(k + 1)
                gat[k].wait()
                wrt[k] = pltpu.async_copy(rows_v.at[k % 2], o_hbm.at[pl.ds(base + k * SC_ROWS, SC_ROWS)],
                                          wsem.at[k % 2])
            for k in range(max(nsub - 2, 0), nsub):
                wrt[k].wait()

    return gather(tab, idx)


def _peer_dense_kernel(rows_ref, gate_ref, h_ref, x_ref, gf_ref, fg_ref, o_ref, *, final):
    G = PEER_GD
    D = D_MODEL
    i = pl.program_id(0)
    lane = lax.broadcasted_iota(jnp.int32, (1, LANES), 1)
    tbase = (i % (LANES // G)) * G
    gate = gate_ref[...]
    outs = []
    for g in range(G):
        w_rows = rows_ref[PEER_SLOTS * g:PEER_SLOTS * (g + 1), :]
        u_rows = lax.bitcast_convert_type(w_rows & jnp.uint32(0xFFFF0000), F32)
        prod = u_rows * h_ref[g:g + 1, :]
        part = prod[:, 0:LANES]
        for c in range(1, D // LANES):
            part = part + prod[:, LANES * c:LANES * (c + 1)]
        act = jnp.sum(part, axis=1, keepdims=True)
        gcol = jnp.sum(jnp.where(lane == tbase + g, gate, 0.0), axis=1, keepdims=True)
        coef = gcol * (0.5 * act * (1.0 + lax.erf(act * (2.0 ** -0.5))))
        v_rows = lax.bitcast_convert_type(w_rows << 16, F32)
        outs.append(jnp.sum(v_rows * coef, axis=0, keepdims=True))
    xn = x_ref[...] + gf_ref[...] * jnp.concatenate(outs, axis=0)
    if final:
        ms = jnp.mean(xn * xn, axis=-1, keepdims=True)
        xn = xn * lax.rsqrt(ms + RMS_EPS) * fg_ref[...]
    o_ref[...] = xn


def _peer_eval_dense(rows, gate_t, h2, x2, mod4, final_g, S, final, tok0):
    T, D = x2.shape
    G = PEER_GD
    tb = rows.shape[0] // PEER_SLOTS
    assert tok0 % LANES == 0 and tb % LANES == 0
    blk0 = tok0 // G
    return pl.pallas_call(
        functools.partial(_peer_dense_kernel, final=final),
        out_shape=jax.ShapeDtypeStruct((tb, D), F32),
        grid=(tb // G,),
        in_specs=[
            pl.BlockSpec((G * PEER_SLOTS, D), lambda i: (i, 0)),
            pl.BlockSpec((None, PEER_SLOTS, LANES), lambda i: (i // (LANES // G), 0, 0)),
            pl.BlockSpec((G, D), lambda i: (blk0 + i, 0)),
            pl.BlockSpec((G, D), lambda i: (blk0 + i, 0)),
            pl.BlockSpec((None, None, 1, D), lambda i: ((blk0 + i) // (S // G), 5, 0, 0)),
            pl.BlockSpec((1, D), lambda i: (0, 0)),
        ],
        out_specs=pl.BlockSpec((G, D), lambda i: (i, 0)),
        compiler_params=_params(("parallel",)),
        name="peer_dense",
    )(rows, gate_t, h2, x2, mod4, final_g.reshape(1, D))


def _pack_uv(u, v):
    hi = lax.bitcast_convert_type(u.astype(BF16), jnp.uint16).astype(jnp.uint32) << 16
    lo = lax.bitcast_convert_type(v.astype(BF16), jnp.uint16).astype(jnp.uint32)
    return (hi | lo).reshape(u.shape[0], 1, u.shape[1])


def kernel(x, c, norm_mix_g, norm_ffn_g, final_norm_g, ada_w, ada_b, w_in, w_out, dif_lam, dif_subln_g, rw_mu, rw_w0,
           rw_w2, rw_a0, rw_a2, rw_g2, rw_kk, rw_ka, rw_rk, rw_ln_g, rw_ln_b, fox_bf, peer_wq, peer_subkeys, peer_u,
           peer_v):
    B, S, D = x.shape
    T = B * S
    depth = ada_w.shape[0]
    x2 = x.reshape(T, D)
    mod = _ada_mod(c, ada_w, ada_b)
    for l in range(depth):
        mod4 = mod[l].reshape(B, 6, 1, D)
        w_pad, w_vt = _pad_w_in(w_in[l])
        dqk, rcols, fqk, fl, vtd, vtf = _in_proj(x2, norm_mix_g[l], mod4, w_pad, w_vt, S)
        ya = _diff_attention(dqk, vtd, dif_lam[l], dif_subln_g[l], l, B, S)
        yb = _rwkv(rcols, rw_mu[l], rw_w0[l], rw_w2[l], rw_a0[l], rw_a2[l], rw_g2[l], rw_kk[l], rw_ka[l],
                   rw_rk[l].reshape(-1), rw_ln_g[l], rw_ln_b[l], B, S, prec=None)
        crep, crow = _fox_cum(fl, fox_bf[l], B, S)
        yc = _fox_attention(fqk, vtf, crep, crow, B, S)
        x2, h2 = _out_proj(ya, yb, yc, x2, w_out[l], norm_ffn_g[l], mod4, S)
        uv = _pack_uv(peer_u[l], peer_v[l])
        final = l == depth - 1
        tb = (T * SC_SHARE_NUM // SC_SHARE_DEN) // (SC_WORKERS * SC_WINDOW) * (SC_WORKERS * SC_WINDOW)
        ta = T - tb
        bounds = [ta + tb * j // SC_PARTS for j in range(SC_PARTS + 1)]
        parts = []
        for t0, t1 in zip(bounds[:-1], bounds[1:]):
            e_p, g_p = _peer_route(h2, peer_wq[l], peer_subkeys[l], t0, t1 - t0)
            parts.append((_sc_gather(uv.reshape(-1, D), e_p.reshape(-1)), g_p, t0))
        e_a, g_a = _peer_route(h2, peer_wq[l], peer_subkeys[l], 0, ta)
        outs = [_peer_eval(e_a, g_a, h2, x2, mod4, final_norm_g, uv, S, final, ntok=ta)]
        outs += [_peer_eval_dense(r, g_p, h2, x2, mod4, final_norm_g, S, final, t0) for r, g_p, t0 in parts]
        x2 = jnp.concatenate(outs, axis=0)
    return x2.reshape(B, S, D)
```

```python
import functools
import math

import jax
import jax.numpy as jnp
from jax import lax
from jax.experimental import pallas as pl
from jax.experimental.pallas import tpu as pltpu
from jax.experimental.pallas import tpu_sc as plsc

F32 = jnp.float32
BF16 = jnp.bfloat16
HIGHEST = lax.Precision.HIGHEST

D_MODEL = 1024
HEAD_DIM = 64
DIFF_HEADS = 6
DIFF_QK_DIM = HEAD_DIM // 2
RWKV_HEADS = 6
FOX_HEADS = 4
DIFF_WIDTH = DIFF_HEADS * HEAD_DIM
RWKV_WIDTH = RWKV_HEADS * HEAD_DIM
FOX_WIDTH = FOX_HEADS * HEAD_DIM
DECAY_LORA = 64
AAA_LORA = 64
GATE_LORA = 128
DIFF_COLS = 3 * DIFF_WIDTH
RWKV_COLS = 3 * RWKV_WIDTH + DECAY_LORA + AAA_LORA + GATE_LORA
PEER_HEADS = 8
PEER_NKEYS = 128
PEER_TOPK = 16
PEER_QDIM = 256
PEER_HALF = PEER_QDIM // 2
RMS_EPS = 1e-6
RWKV_LN_EPS = 64e-5

LANES = 128
SUBLANES = 8
RW_PAD_COLS = 3 * RWKV_WIDTH + 3 * LANES
VMEM_LIMIT = 56 * 1024 * 1024

RW_CHUNK = 64
RW_CHUNKS_PER_ITER = 2


def _params(sem, vmem=VMEM_LIMIT):
    return pltpu.CompilerParams(dimension_semantics=sem, vmem_limit_bytes=vmem)


def _nt(a, b, precision=None):
    return lax.dot_general(a, b, (((1,), (1,)), ((), ())), preferred_element_type=F32, precision=precision)


def _mm(a, b, precision=None):
    return jnp.dot(a, b, preferred_element_type=F32, precision=precision)


def _ada_kernel(c_ref, w_ref, b_ref, o_ref):
    c = c_ref[...]
    ca = c * jax.nn.sigmoid(c)
    o_ref[...] = _mm(ca, w_ref[...], HIGHEST) + b_ref[...]


def _ada_mod(c, ada_w, ada_b):
    L, D, N = ada_w.shape
    B = c.shape[0]
    tn = 1536
    return pl.pallas_call(
        _ada_kernel,
        out_shape=jax.ShapeDtypeStruct((L, B, N), F32),
        grid=(L, N // tn),
        in_specs=[
            pl.BlockSpec((B, D), lambda l, j: (0, 0)),
            pl.BlockSpec((None, D, tn), lambda l, j: (l, 0, j)),
            pl.BlockSpec((None, 1, tn), lambda l, j: (l, 0, j)),
        ],
        out_specs=pl.BlockSpec((None, B, tn), lambda l, j: (l, 0, j)),
        compiler_params=_params(("parallel", "parallel")),
        name="ada_mod",
    )(c, ada_w, ada_b.reshape(L, 1, N))


ATT_T = 512
QK_DIFF = 2 * DIFF_WIDTH
QK_FOX = 2 * FOX_WIDTH
VT_ROWS = DIFF_WIDTH + FOX_WIDTH
IN_PAD_COLS = QK_DIFF + RW_PAD_COLS + QK_FOX + LANES


def _inproj_kernel(x_ref, g_ref, sh_ref, sc_ref, w_ref, wvt_ref, d_ref, r_ref, f_ref, fl_ref, vtd_ref, vtf_ref, *, ta):
    x = x_ref[...]
    ms = jnp.mean(x * x, axis=-1, keepdims=True)
    y = x * lax.rsqrt(ms + RMS_EPS) * g_ref[...]
    h = (y * (1.0 + sc_ref[...]) + sh_ref[...]).astype(BF16)
    o1 = QK_DIFF
    o2 = o1 + RW_PAD_COLS
    o3 = o2 + QK_FOX
    d_ref[...] = _mm(h, w_ref[:, 0:o1]).astype(BF16)
    r_ref[...] = _mm(h, w_ref[:, o1:o2])
    f_ref[...] = _mm(h, w_ref[:, o2:o3]).astype(BF16)
    fl_ref[...] = _mm(h, w_ref[:, o3:o3 + LANES])
    vt = _nt(wvt_ref[...], h).astype(BF16)
    for s in range(x.shape[0] // ta):
        vtd_ref[s] = vt[0:DIFF_WIDTH, ta * s:ta * (s + 1)]
        vtf_ref[s] = vt[DIFF_WIDTH:, ta * s:ta * (s + 1)]


def _in_proj(x2, g, mod4, w_pad, w_vt, S):
    T, D = x2.shape
    tm = 512 if S % 512 == 0 else S
    ta = min(ATT_T, S)
    nb = S // tm
    row = lambda i: (i, 0)
    return pl.pallas_call(
        functools.partial(_inproj_kernel, ta=ta),
        out_shape=(
            jax.ShapeDtypeStruct((T, QK_DIFF), BF16),
            jax.ShapeDtypeStruct((T, RW_PAD_COLS), F32),
            jax.ShapeDtypeStruct((T, QK_FOX), BF16),
            jax.ShapeDtypeStruct((T, LANES), F32),
            jax.ShapeDtypeStruct((T // ta, DIFF_WIDTH, ta), BF16),
            jax.ShapeDtypeStruct((T // ta, FOX_WIDTH, ta), BF16),
        ),
        grid=(T // tm,),
        in_specs=[
            pl.BlockSpec((tm, D), row),
            pl.BlockSpec((1, D), lambda i: (0, 0)),
            pl.BlockSpec((None, None, 1, D), lambda i: (i // nb, 0, 0, 0)),
            pl.BlockSpec((None, None, 1, D), lambda i: (i // nb, 1, 0, 0)),
            pl.BlockSpec((D, IN_PAD_COLS), lambda i: (0, 0)),
            pl.BlockSpec((VT_ROWS, D), lambda i: (0, 0)),
        ],
        out_specs=(
            pl.BlockSpec((tm, QK_DIFF), row),
            pl.BlockSpec((tm, RW_PAD_COLS), row),
            pl.BlockSpec((tm, QK_FOX), row),
            pl.BlockSpec((tm, LANES), row),
            pl.BlockSpec((tm // ta, DIFF_WIDTH, ta), lambda i: (i, 0, 0)),
            pl.BlockSpec((tm // ta, FOX_WIDTH, ta), lambda i: (i, 0, 0)),
        ),
        compiler_params=_params(("parallel",)),
        name="in_proj",
    )(x2, g.reshape(1, D), mod4, mod4, w_pad, w_vt)


def _pad_w_in(w_in):
    D = w_in.shape[0]
    W = RWKV_WIDTH
    o = DIFF_COLS
    z64 = jnp.zeros((D, LANES - DECAY_LORA), w_in.dtype)
    rw = w_in[:, o:o + RWKV_COLS]
    fx = w_in[:, o + RWKV_COLS:]
    zf = jnp.zeros((D, LANES - FOX_HEADS), w_in.dtype)
    w_pad = jnp.concatenate([
        w_in[:, :QK_DIFF],
        rw[:, :3 * W], rw[:, 3 * W:3 * W + DECAY_LORA], z64,
        rw[:, 3 * W + DECAY_LORA:3 * W + DECAY_LORA + AAA_LORA], z64,
        rw[:, 3 * W + DECAY_LORA + AAA_LORA:],
        fx[:, :QK_FOX], fx[:, 3 * FOX_WIDTH:], zf,
    ], axis=1).astype(BF16)
    w_vt = jnp.concatenate([w_in[:, QK_DIFF:o], fx[:, QK_FOX:3 * FOX_WIDTH]], axis=1).T.astype(BF16)
    return w_pad, w_vt


LOG2E = math.log2(math.e)


ACC_ROWS = LANES + 16


def _scaled_q(q, c):
    return (q.astype(F32) * c).astype(BF16)


def _with_ones(vt):
    return jnp.concatenate([vt, jnp.ones((ACC_ROWS - LANES, vt.shape[1]), vt.dtype)], axis=0)


def _flash_step(s2s, vt1, m_ref, acc_ref):
    n = len(s2s)
    m_old = [m_ref[x] for x in range(n)]
    m_new = [jnp.maximum(m_old[x], jnp.max(s2s[x], axis=0, keepdims=True)) for x in range(n)]
    alpha = [jnp.exp2(m_old[x] - m_new[x]) for x in range(n)]
    p = [jnp.exp2(s2s[x] - m_new[x]).astype(BF16) for x in range(n)]
    pv = [_mm(vt1, p[x]) for x in range(n)]
    for x in range(n):
        acc_ref[x] = alpha[x] * acc_ref[x] + pv[x]
        m_ref[x] = m_new[x]


def _diff_attn_kernel(lam_ref, g_ref, q_ref, k_ref, vt_ref, o_ref, m_ref, acc_ref, *, tq, lam_init):
    i = pl.program_id(2)
    lane = lax.broadcasted_iota(jnp.int32, (1, LANES), 1)
    q = _scaled_q(q_ref[...], (DIFF_QK_DIM ** -0.5) * LOG2E)
    zero = jnp.zeros_like(q)
    qm = [jnp.where((lane >= DIFF_QK_DIM * x) & (lane < DIFF_QK_DIM * (x + 1)), q, zero) for x in range(4)]
    m_ref[...] = jnp.full(m_ref.shape, -jnp.inf, F32)
    acc_ref[...] = jnp.zeros(acc_ref.shape, F32)

    def step(j, diag):
        k = k_ref[pl.ds(pl.multiple_of(j * tq, tq), tq), :]
        vt1 = _with_ones(vt_ref[j])
        if diag:
            keep = (lax.broadcasted_iota(jnp.int32, (tq, tq), 1) >= lax.broadcasted_iota(jnp.int32, (tq, tq), 0))
        s2s = [_nt(k, qm[x]) for x in range(4)]
        if diag:
            s2s = [jnp.where(keep, s2, -jnp.inf) for s2 in s2s]
        _flash_step(s2s, vt1, m_ref, acc_ref)

    def body(j, carry):
        step(j, False)
        return carry

    lax.fori_loop(0, i, body, 0)
    step(i, True)

    lp = lam_ref[...]
    lam = (jnp.exp(jnp.sum(lp[0:1] * lp[1:2], axis=-1, keepdims=True))
           - jnp.exp(jnp.sum(lp[2:3] * lp[3:4], axis=-1, keepdims=True)) + lam_init)
    sm = [acc_ref[x, 0:LANES, :] / acc_ref[x, LANES:LANES + 1, :] for x in range(4)]
    outs = [sm[2 * hh] - lam * sm[2 * hh + 1] for hh in range(2)]
    row = lax.broadcasted_iota(jnp.int32, (LANES, 1), 0)
    o = jnp.where(row < HEAD_DIM, outs[0], outs[1])
    sq = o * o
    ms = jnp.where(row < HEAD_DIM, jnp.sum(sq[0:HEAD_DIM], axis=0, keepdims=True),
                   jnp.sum(sq[HEAD_DIM:], axis=0, keepdims=True)) * (1.0 / HEAD_DIM)
    y = o * lax.rsqrt(ms + RMS_EPS) * g_ref[...] * (1.0 - lam_init)
    o_ref[...] = y.T.astype(o_ref.dtype)


def _diff_attention(dqk, vtd, lam_params, subln_g, layer_idx, B, S):
    T = B * S
    tq = min(ATT_T, S)
    nq = S // tq
    npair = DIFF_HEADS // 2
    lam_init = 0.8 - 0.6 * math.exp(-0.3 * layer_idx)
    g2 = jnp.concatenate([subln_g, subln_g]).reshape(LANES, 1).astype(F32)
    return pl.pallas_call(
        functools.partial(_diff_attn_kernel, tq=tq, lam_init=lam_init),
        out_shape=jax.ShapeDtypeStruct((T, DIFF_WIDTH), BF16),
        grid=(B, npair, nq),
        in_specs=[
            pl.BlockSpec((4, DIFF_QK_DIM), lambda b, p, i: (0, 0)),
            pl.BlockSpec((LANES, 1), lambda b, p, i: (0, 0)),
            pl.BlockSpec((tq, LANES), lambda b, p, i: (b * nq + i, p)),
            pl.BlockSpec((S, LANES), lambda b, p, i: (b, npair + p)),
            pl.BlockSpec((nq, LANES, tq), lambda b, p, i: (b, p, 0)),
        ],
        out_specs=pl.BlockSpec((tq, LANES), lambda b, p, i: (b * nq + i, p)),
        scratch_shapes=[
            pltpu.VMEM((4, 1, tq), F32),
            pltpu.VMEM((4, ACC_ROWS, tq), F32),
        ],
        compiler_params=_params(("parallel", "parallel", "arbitrary")),
        name="diff_attn",
    )(lam_params, g2, dqk, dqk, vtd)


def _fox_cum_kernel(f_ref, b_ref, rep_ref, row_ref, *, S, tc):
    rr = lax.broadcasted_iota(jnp.int32, (tc, tc), 0)
    cc = lax.broadcasted_iota(jnp.int32, (tc, tc), 1)
    tri = (rr >= cc).astype(F32)
    sel_r = lax.broadcasted_iota(jnp.int32, (LANES, LANES), 0)
    carry = jnp.zeros((1, LANES), F32)
    for c in range(S // tc):
        z = f_ref[c * tc:(c + 1) * tc, :] + b_ref[...]
        logf = -(jnp.maximum(-z, 0.0) + jnp.log(1.0 + jnp.exp(-jnp.abs(z))))
        cum = _mm(tri, logf, HIGHEST) + carry
        carry = cum[tc - 1:tc, :]
        row_ref[:, c * tc:(c + 1) * tc] = cum.T[0:8, :]
        for h in range(FOX_HEADS):
            rep_ref[h, c * tc:(c + 1) * tc, :] = _mm(cum, (sel_r == h).astype(F32), HIGHEST)


def _fox_cum(fl, b_f, B, S):
    tc = 256 if S % 256 == 0 else S
    bpad = jnp.zeros((1, LANES), F32).at[0, :FOX_HEADS].set(b_f.astype(F32))
    return pl.pallas_call(
        functools.partial(_fox_cum_kernel, S=S, tc=tc),
        out_shape=(jax.ShapeDtypeStruct((B, FOX_HEADS, S, LANES), F32), jax.ShapeDtypeStruct((B, 8, S), F32)),
        grid=(B,),
        in_specs=[pl.BlockSpec((S, LANES), lambda b: (b, 0)), pl.BlockSpec((1, LANES), lambda b: (0, 0))],
        out_specs=(pl.BlockSpec((None, FOX_HEADS, S, LANES), lambda b: (b, 0, 0, 0)),
                   pl.BlockSpec((None, 8, S), lambda b: (b, 0, 0))),
        compiler_params=_params(("parallel",)),
        name="fox_cum",
    )(fl, bpad)


def _fox_attn_kernel(q_ref, k_ref, vt_ref, c0_ref, c1_ref, cr_ref, o_ref, m_ref, acc_ref, *, tq):
    p_id = pl.program_id(1)
    i = pl.program_id(2)
    lane = lax.broadcasted_iota(jnp.int32, (1, LANES), 1)
    q = _scaled_q(q_ref[...], (HEAD_DIM ** -0.5) * LOG2E)
    zero = jnp.zeros_like(q)
    qm = [jnp.where((lane >= HEAD_DIM * x) & (lane < HEAD_DIM * (x + 1)), q, zero) for x in range(2)]
    ck_refs = (c0_ref, c1_ref)
    cq = [cr_ref[2 * p_id + x, pl.ds(i, 1), :] for x in range(2)]
    m_ref[...] = jnp.full(m_ref.shape, -jnp.inf, F32)
    acc_ref[...] = jnp.zeros(acc_ref.shape, F32)

    def step(j, diag):
        off = pl.multiple_of(j * tq, tq)
        k = k_ref[pl.ds(off, tq), :]
        vt1 = _with_ones(vt_ref[j])
        if diag:
            keep = (lax.broadcasted_iota(jnp.int32, (tq, tq), 1) >= lax.broadcasted_iota(jnp.int32, (tq, tq), 0))
        s2s = []
        for x in range(2):
            ck = ck_refs[x][pl.ds(off, tq), :]
            bias = (cq[x] - jnp.concatenate([ck] * (tq // LANES), axis=1)) * LOG2E
            s2s.append(_nt(k, qm[x]) + bias)
        if diag:
            s2s = [jnp.where(keep, s2, -jnp.inf) for s2 in s2s]
        _flash_step(s2s, vt1, m_ref, acc_ref)

    def body(j, carry):
        step(j, False)
        return carry

    lax.fori_loop(0, i, body, 0)
    step(i, True)
    row = lax.broadcasted_iota(jnp.int32, (LANES, 1), 0)
    sm = [acc_ref[x, 0:LANES, :] / acc_ref[x, LANES:LANES + 1, :] for x in range(2)]
    o = jnp.where(row < HEAD_DIM, sm[0], sm[1])
    o_ref[...] = o.T.astype(o_ref.dtype)


def _fox_attention(fqk, vtf, crep, crow, B, S):
    T = B * S
    tq = min(ATT_T, S)
    nq = S // tq
    npair = FOX_HEADS // 2
    crow4 = crow.reshape(B, 8, nq, tq)
    rep = lambda x: pl.BlockSpec((None, None, S, LANES), lambda b, p, i: (b, 2 * p + x, 0, 0))
    return pl.pallas_call(
        functools.partial(_fox_attn_kernel, tq=tq),
        out_shape=jax.ShapeDtypeStruct((T, FOX_WIDTH), BF16),
        grid=(B, npair, nq),
        in_specs=[
            pl.BlockSpec((tq, LANES), lambda b, p, i: (b * nq + i, p)),
            pl.BlockSpec((S, LANES), lambda b, p, i: (b, npair + p)),
            pl.BlockSpec((nq, LANES, tq), lambda b, p, i: (b, p, 0)),
            rep(0), rep(1),
            pl.BlockSpec((None, 8, nq, tq), lambda b, p, i: (b, 0, 0, 0)),
        ],
        out_specs=pl.BlockSpec((tq, LANES), lambda b, p, i: (b * nq + i, p)),
        scratch_shapes=[
            pltpu.VMEM((2, 1, tq), F32),
            pltpu.VMEM((2, ACC_ROWS, tq), F32),
        ],
        compiler_params=_params(("parallel", "parallel", "arbitrary")),
        name="fox_attn",
    )(fqk, fqk, vtf, crep, crep, crow4)


def _split3(x):
    hi = x.astype(BF16)
    r1 = x - hi.astype(F32)
    mid = r1.astype(BF16)
    lo = (r1 - mid.astype(F32)).astype(BF16)
    return hi, mid, lo


def _seg_sum(x, seg, npair):
    parts = _split3(x)
    return jnp.concatenate(
        [sum(_mm(t[:, LANES * p:LANES * (p + 1)], seg) for t in parts) for p in range(npair)], axis=1)


def _rwkv_kernel(x_ref, mu_ref, w0_ref, w2_ref, a0_ref, a2_ref, g2_ref, kk_ref, ka_ref, rk_ref, lng_ref, lnb_ref,
                 o_ref, carry_ref, st_ref, at_ref, rt_ref, bt_ref, kt_ref, v_ref, wc_ref, y_ref, g_ref, bon_ref,
                 *, tt, prec):
    i = pl.program_id(1)
    W = RWKV_WIDTH
    C = RW_CHUNK
    CPI = RW_CHUNKS_PER_ITER
    npair = RWKV_HEADS // 2

    @pl.when(i == 0)
    def _():
        carry_ref[...] = jnp.zeros(carry_ref.shape, F32)
        st_ref[...] = jnp.zeros(st_ref.shape, F32)

    x = x_ref[...]
    rows = lax.broadcasted_iota(jnp.int32, (tt, 1), 0)
    prev = jnp.where(rows == 0, carry_ref[...], pltpu.roll(x, 1, axis=0))
    carry_ref[...] = x[tt - 1:tt, :]
    xs = x + (prev - x) * mu_ref[...]
    r = xs[:, 0:W]
    k = xs[:, W:2 * W]
    v = xs[:, 2 * W:3 * W]
    xw = xs[:, 3 * W:3 * W + LANES]
    xa = xs[:, 3 * W + LANES:3 * W + 2 * LANES]
    xg = xs[:, 3 * W + 2 * LANES:]
    wl = w0_ref[...] + _mm(jnp.tanh(xw), w2_ref[...], HIGHEST)
    w = -(jnp.maximum(-wl, 0.0) + jnp.log(1.0 + jnp.exp(-jnp.abs(wl)))) - 0.5
    logdec = -jnp.exp(w)
    a = jax.nn.sigmoid(a0_ref[...] + _mm(xa, a2_ref[...], HIGHEST))
    g_ref[...] = _mm(jax.nn.sigmoid(xg), g2_ref[...], HIGHEST)

    r_i = lax.broadcasted_iota(jnp.int32, (LANES, LANES), 0)
    c_i = lax.broadcasted_iota(jnp.int32, (LANES, LANES), 1)
    seg = ((r_i // HEAD_DIM) == (c_i // HEAD_DIM)).astype(BF16)
    kkv = k * kk_ref[...]
    kkn = kkv / jnp.maximum(jnp.sqrt(_seg_sum(kkv * kkv, seg, npair)), 1e-12)
    k2 = k * (1.0 + (a - 1.0) * ka_ref[...])
    bon_ref[...] = _seg_sum(r * k2 * rk_ref[...], seg, npair) * v

    rt_i = lax.broadcasted_iota(jnp.int32, (tt, tt), 0)
    ct_i = lax.broadcasted_iota(jnp.int32, (tt, tt), 1)
    tri = (((rt_i // C) == (ct_i // C)) & (rt_i >= ct_i)).astype(BF16)
    cum = sum(_mm(tri, t) for t in _split3(logdec))
    winv = jnp.exp(-cum)
    wcum = jnp.exp(cum)
    at_ref[...] = -kkn * jnp.exp(cum - logdec)
    bt_ref[...] = kkn * a * winv
    kt_ref[...] = k2 * winv
    rt_ref[...] = r * wcum
    v_ref[...] = v
    wc_ref[...] = wcum

    lane = lax.broadcasted_iota(jnp.int32, (1, LANES), 1)
    lo = lane < HEAD_DIM
    tpos = r_i % C
    ipos = c_i % C
    strict = tpos > ipos
    incl = tpos >= ipos
    eye = r_i == c_i

    def stack2(m):
        return jnp.concatenate([jnp.where(lo, m, 0.0), jnp.where(lo, 0.0, m)], axis=0)

    def cast(m):
        return m if prec is not None else m.astype(BF16)

    def mm(p, q):
        return _mm(cast(p), cast(q), prec)

    def nt(p, q):
        return _nt(cast(p), cast(q), prec)

    def chunks(cc, carry):
        units = [(ci, p) for ci in range(CPI) for p in range(npair)]
        r0 = [pl.multiple_of((cc * CPI + ci) * C, C) for ci in range(CPI)]
        ld = lambda ref, ci, p: stack2(ref[pl.ds(r0[ci], C), LANES * p:LANES * (p + 1)])
        a2 = {u: ld(at_ref, *u) for u in units}
        r2 = {u: ld(rt_ref, *u) for u in units}
        b2 = {u: ld(bt_ref, *u) for u in units}
        k2s = {u: ld(kt_ref, *u) for u in units}
        v2 = {u: ld(v_ref, *u) for u in units}
        wl = {(ci, p): wc_ref[pl.ds(pl.multiple_of(r0[ci] + C - 8, 8), 8), LANES * p:LANES * (p + 1)][7:8, :]
              for (ci, p) in units}
        ar = {u: jnp.concatenate([a2[u], r2[u]], axis=0) for u in units}
        mb = {u: nt(ar[u], b2[u]) for u in units}
        mk = {u: nt(ar[u], k2s[u]) for u in units}
        lab = {u: jnp.where(strict, mb[u][0:LANES], 0.0) for u in units}
        mrb = {u: jnp.where(incl, mb[u][LANES:], 0.0) for u in units}
        lak = {u: jnp.where(strict, mk[u][0:LANES], 0.0) for u in units}
        mrk = {u: jnp.where(incl, mk[u][LANES:], 0.0) for u in units}
        xx = {u: jnp.concatenate([a2[u], mm(lak[u], v2[u])], axis=1) for u in units}
        lp = lab
        for it in range(6):
            xx = {u: xx[u] + mm(lp[u], xx[u]) for u in units}
            if it < 5:
                lp = {u: mm(lp[u], lp[u]) for u in units}
        mq = {u: mm(mrb[u], xx[u]) for u in units}
        mv = {u: mm(mrk[u], v2[u]) for u in units}
        bx = {u: mm((b2[u] * wl[u]).T, xx[u]) for u in units}
        kv = {u: mm((k2s[u] * wl[u]).T, v2[u]) for u in units}
        st = [st_ref[p] for p in range(npair)]
        for (ci, p) in units:
            u = (ci, p)
            q1 = r2[u] + mq[u][:, 0:LANES]
            q2 = mq[u][:, LANES:] + mv[u]
            gmat = jnp.where(eye, jnp.broadcast_to(wl[u], (LANES, LANES)), 0.0) + bx[u][:, 0:LANES]
            hmat = bx[u][:, LANES:] + kv[u]
            gs = mm(jnp.concatenate([gmat, q1], axis=0), st[p])
            st[p] = gs[0:LANES] + hmat
            yy = gs[LANES:] + q2
            y_ref[pl.ds(r0[ci], C), LANES * p:LANES * (p + 1)] = yy[0:C] + yy[C:]
        for p in range(npair):
            st_ref[p] = st[p]
        return carry

    lax.fori_loop(0, tt // (C * CPI), chunks, 0)

    y = y_ref[...]
    inv = 1.0 / HEAD_DIM
    mean = _seg_sum(y, seg, npair) * inv
    yc = y - mean
    var = _seg_sum(yc * yc, seg, npair) * inv
    yn = yc * lax.rsqrt(var + RWKV_LN_EPS) * lng_ref[...] + lnb_ref[...]
    o_ref[...] = ((yn + bon_ref[...]) * g_ref[...]).astype(o_ref.dtype)


def _rwkv(rcols, mu, w0, w2, a0, a2, g2, k_k, k_a, r_k, ln_g, ln_b, B, S, prec=HIGHEST):
    T = B * S
    W = RWKV_WIDTH
    tt = 512 if S % 512 == 0 else S
    nt_ = S // tt
    npair = RWKV_HEADS // 2
    pad = LANES - DECAY_LORA
    mu_p = jnp.concatenate([mu[:3 * W], mu[3 * W:3 * W + DECAY_LORA], jnp.zeros((pad,), F32),
                            mu[3 * W + DECAY_LORA:3 * W + DECAY_LORA + AAA_LORA], jnp.zeros((pad,), F32),
                            mu[3 * W + DECAY_LORA + AAA_LORA:]]).reshape(1, RW_PAD_COLS)
    w2p = jnp.concatenate([w2, jnp.zeros((pad, W), F32)], axis=0)
    a2p = jnp.concatenate([a2, jnp.zeros((pad, W), F32)], axis=0)
    vec = lambda t: t.reshape(1, W).astype(F32)
    full = lambda shape: pl.BlockSpec(shape, lambda b, i: (0,) * len(shape))
    sc = lambda: pltpu.VMEM((tt, W), F32)
    return pl.pallas_call(
        functools.partial(_rwkv_kernel, tt=tt, prec=prec),
        out_shape=jax.ShapeDtypeStruct((T, W), BF16),
        grid=(B, nt_),
        in_specs=[
            pl.BlockSpec((tt, RW_PAD_COLS), lambda b, i: (b * nt_ + i, 0)),
            full((1, RW_PAD_COLS)), full((1, W)), full((LANES, W)), full((1, W)), full((LANES, W)),
            full((GATE_LORA, W)), full((1, W)), full((1, W)), full((1, W)), full((1, W)), full((1, W)),
        ],
        out_specs=pl.BlockSpec((tt, W), lambda b, i: (b * nt_ + i, 0)),
        scratch_shapes=[
            pltpu.VMEM((1, RW_PAD_COLS), F32),
            pltpu.VMEM((npair, LANES, LANES), F32),
            sc(), sc(), sc(), sc(), sc(), sc(), sc(), sc(), sc(),
        ],
        compiler_params=_params(("parallel", "arbitrary")),
        name="rwkv7",
    )(rcols, mu_p, vec(w0), w2p, vec(a0), a2p, g2, vec(k_k), vec(k_a), vec(r_k), vec(ln_g), vec(ln_b))


def _outproj_kernel(ya_ref, yb_ref, yc_ref, x_ref, w_ref, gm_ref, g_ref, sh_ref, sc_ref, xo_ref, ho_ref):
    o1 = DIFF_WIDTH
    o2 = o1 + RWKV_WIDTH
    mix = (_mm(ya_ref[...], w_ref[0:o1, :]) + _mm(yb_ref[...], w_ref[o1:o2, :])
           + _mm(yc_ref[...], w_ref[o2:, :]))
    xn = x_ref[...] + gm_ref[...] * mix
    xo_ref[...] = xn
    ms = jnp.mean(xn * xn, axis=-1, keepdims=True)
    y = xn * lax.rsqrt(ms + RMS_EPS) * g_ref[...]
    ho_ref[...] = y * (1.0 + sc_ref[...]) + sh_ref[...]


def _out_proj(ya, yb, yc, x2, w_out, g, mod4, S):
    T, D = x2.shape
    tm = 512 if S % 512 == 0 else S
    nb = S // tm
    row = lambda i: (i, 0)
    modspec = lambda which: pl.BlockSpec((None, None, 1, D), lambda i: (i // nb, which, 0, 0))
    return pl.pallas_call(
        _outproj_kernel,
        out_shape=(jax.ShapeDtypeStruct((T, D), F32), jax.ShapeDtypeStruct((T, D), F32)),
        grid=(T // tm,),
        in_specs=[
            pl.BlockSpec((tm, DIFF_WIDTH), row), pl.BlockSpec((tm, RWKV_WIDTH), row), pl.BlockSpec((tm, FOX_WIDTH), row),
            pl.BlockSpec((tm, D), row),
            pl.BlockSpec((D, D), lambda i: (0, 0)),
            modspec(2),
            pl.BlockSpec((1, D), lambda i: (0, 0)),
            modspec(3), modspec(4),
        ],
        out_specs=(pl.BlockSpec((tm, D), row), pl.BlockSpec((tm, D), row)),
        compiler_params=_params(("parallel",)),
        name="out_proj",
    )(ya, yb, yc, x2, w_out.astype(BF16), mod4, g.reshape(1, D), mod4, mod4)


def _top16(s, iota_f, n):
    vals, poss = [], []
    for _ in range(PEER_TOPK):
        m = jnp.max(s, axis=0, keepdims=True)
        pos = jnp.min(jnp.where(s == m, iota_f, float(n)), axis=0, keepdims=True)
        vals.append(m)
        poss.append(pos)
        s = jnp.where(iota_f == pos, -jnp.inf, s)
    return jnp.concatenate(vals, axis=0), jnp.concatenate(poss, axis=0)


ROUTE_UNROLL = 8
PEER_NCAND = 56


def _peer_cand_tables():
    K = PEER_TOPK
    pairs = [(a, b) for a in range(K) for b in range(K) if (a + 1) * (b + 1) <= K]
    n = PEER_NCAND
    p0 = [[0.0] * K for _ in range(n)]
    p1 = [[0.0] * K for _ in range(n)]
    pad = [0.0] * n
    pos = [float(K * K + r) for r in range(n)]
    for r, (a, b) in enumerate(pairs):
        p0[r][a] = 1.0
        p1[r][b] = 1.0
        pos[r] = float(a * K + b)
    for r in range(len(pairs), n):
        pad[r] = -float("inf")
    col = lambda v: jnp.broadcast_to(jnp.asarray(v, F32)[:, None], (n, LANES))
    return jnp.asarray(p0, F32), jnp.asarray(p1, F32), col(pad), col(pos)


def _peer_route_kernel(h_ref, wq_ref, sk_ref, p0_ref, p1_ref, cpad_ref, cpos_ref, idx_ref, gate_ref, q_scr, e_scr,
                       g_scr):
    K = PEER_TOPK
    hb = h_ref[...].astype(BF16)
    q = _mm(hb, wq_ref[...])
    for hc in range(2 * PEER_HEADS):
        q_scr[hc] = q[:, LANES * hc:LANES * (hc + 1)].astype(BF16)
    iota_n = lax.broadcasted_iota(jnp.int32, (PEER_NKEYS, LANES), 0).astype(F32)
    cpos = cpos_ref[...]

    def one_head(h):
        sv0, si0 = _top16(_nt(sk_ref[2 * h], q_scr[2 * h]), iota_n, PEER_NKEYS)
        sv1, si1 = _top16(_nt(sk_ref[2 * h + 1], q_scr[2 * h + 1]), iota_n, PEER_NKEYS)
        cand = _mm(p0_ref[...], sv0, HIGHEST) + _mm(p1_ref[...], sv1, HIGHEST) + cpad_ref[...]
        cidx = _mm(p0_ref[...], si0) * float(PEER_NKEYS) + _mm(p1_ref[...], si1)
        fv, es = [], []
        for _ in range(K):
            m = jnp.max(cand, axis=0, keepdims=True)
            pos = jnp.min(jnp.where(cand == m, cpos, float(2 * K * K)), axis=0, keepdims=True)
            hit = cpos == pos
            fv.append(m)
            es.append(jnp.max(jnp.where(hit, cidx, -1.0), axis=0, keepdims=True))
            cand = jnp.where(hit, -jnp.inf, cand)
        fv = jnp.concatenate(fv, axis=0)
        ex = jnp.exp(fv - fv[0:1, :])
        g_scr[h] = ex / jnp.sum(ex, axis=0, keepdims=True)
        e_scr[h] = jnp.concatenate(es, axis=0)

    def heads(hh, carry):
        for j in range(ROUTE_UNROLL):
            one_head(hh * ROUTE_UNROLL + j)
        return carry

    lax.fori_loop(0, PEER_HEADS // ROUTE_UNROLL, heads, 0)
    e = e_scr[...].reshape(PEER_HEADS * K, LANES)
    idx_ref[...] = e.T.astype(jnp.int32)
    gate_ref[...] = g_scr[...].reshape(PEER_HEADS * K, LANES)


def _peer_route(h2, wq, subkeys, tok0=0, ntok=None):
    T, D = h2.shape
    T = T - tok0 if ntok is None else ntok
    tm = LANES
    blk0 = tok0 // tm
    nq = 2 * PEER_HEADS
    sk = subkeys.reshape(nq, PEER_NKEYS, PEER_HALF).astype(BF16)
    p0, p1, cpad, cpos = _peer_cand_tables()
    const = lambda shape: pl.BlockSpec(shape, lambda i: (0,) * len(shape))
    return pl.pallas_call(
        _peer_route_kernel,
        out_shape=(jax.ShapeDtypeStruct((T, PEER_HEADS * PEER_TOPK), jnp.int32),
                   jax.ShapeDtypeStruct((T // tm, PEER_HEADS * PEER_TOPK, tm), F32)),
        grid=(T // tm,),
        in_specs=[
            pl.BlockSpec((tm, D), lambda i: (blk0 + i, 0)),
            const((D, nq * PEER_HALF)),
            const((nq, PEER_NKEYS, PEER_HALF)),
            const((PEER_NCAND, PEER_TOPK)), const((PEER_NCAND, PEER_TOPK)),
            const((PEER_NCAND, LANES)), const((PEER_NCAND, LANES)),
        ],
        out_specs=(pl.BlockSpec((tm, PEER_HEADS * PEER_TOPK), lambda i: (i, 0)),
                   pl.BlockSpec((None, PEER_HEADS * PEER_TOPK, tm), lambda i: (i, 0, 0))),
        scratch_shapes=[
            pltpu.VMEM((nq, tm, PEER_HALF), BF16),
            pltpu.VMEM((PEER_HEADS, PEER_TOPK, tm), F32),
            pltpu.VMEM((PEER_HEADS, PEER_TOPK, tm), F32),
        ],
        compiler_params=_params(("parallel",)),
        name="peer_route",
    )(h2, wq.astype(BF16), sk, p0, p1, cpad, cpos)


PEER_G = 16
PEER_SLOTS = PEER_HEADS * PEER_TOPK


def _peer_eval_kernel(idx_ref, idxn_ref, gate_ref, h_ref, x_ref, gf_ref, fg_ref, uv_ref, o_ref, buf, sem, *, final):
    G = PEER_G
    R = G * PEER_SLOTS
    D = D_MODEL
    tiles = PEER_SLOTS // SUBLANES
    i = pl.program_id(0)
    n = pl.num_programs(0)

    def start(ids, off, s, t, u):
        pltpu.make_async_copy(uv_ref.at[ids[off + t * SUBLANES + u]], buf.at[s, t, pl.ds(u, 1), :],
                              sem.at[s]).start(priority=u % 2)

    def wait(s):
        pltpu.make_async_copy(buf.at[s], buf.at[s], sem.at[s]).wait()

    @pl.when(i == 0)
    def _():
        def body(t, carry):
            for u in range(SUBLANES):
                start(idx_ref, 0, 0, t, u)
            return carry
        lax.fori_loop(0, R // SUBLANES, body, 0)

    lane = lax.broadcasted_iota(jnp.int32, (1, LANES), 1)
    tbase = (i % (LANES // (2 * G))) * (2 * G)
    gate = gate_ref[...]
    outs = []
    for s in range(2):
        wait(s)
        nxt_ids, nxt_off = (idx_ref, R) if s == 0 else (idxn_ref, 0)
        for g in range(G):
            for t in range(tiles * g, tiles * (g + 1)):
                for u in range(SUBLANES):
                    start(nxt_ids, nxt_off, 1 - s, t, u)
            w_rows = buf[s, tiles * g:tiles * (g + 1)].reshape(PEER_SLOTS, D)
            u_rows = lax.bitcast_convert_type(w_rows & jnp.uint32(0xFFFF0000), F32)
            prod = u_rows * h_ref[G * s + g:G * s + g + 1, :]
            part = prod[:, 0:LANES]
            for c in range(1, D // LANES):
                part = part + prod[:, LANES * c:LANES * (c + 1)]
            act = jnp.sum(part, axis=1, keepdims=True)
            gcol = jnp.sum(jnp.where(lane == tbase + G * s + g, gate, 0.0), axis=1, keepdims=True)
            coef = gcol * (0.5 * act * (1.0 + lax.erf(act * (2.0 ** -0.5))))
            v_rows = lax.bitcast_convert_type(w_rows << 16, F32)
            outs.append(jnp.sum(v_rows * coef, axis=0, keepdims=True))
    xn = x_ref[...] + gf_ref[...] * jnp.concatenate(outs, axis=0)
    if final:
        ms = jnp.mean(xn * xn, axis=-1, keepdims=True)
        xn = xn * lax.rsqrt(ms + RMS_EPS) * fg_ref[...]
    o_ref[...] = xn

    @pl.when(i == n - 1)
    def _():
        wait(0)


def _peer_eval(eidx, gate_t, h2, x2, mod4, final_g, uv, S, final, ntok=None):
    T, D = x2.shape
    T = T if ntok is None else ntok
    G = PEER_G
    R = G * PEER_SLOTS
    n = T // (2 * G)
    return pl.pallas_call(
        functools.partial(_peer_eval_kernel, final=final),
        out_shape=jax.ShapeDtypeStruct((T, D), F32),
        grid=(n,),
        in_specs=[
            pl.BlockSpec((2 * R,), lambda i: (i,), memory_space=pltpu.SMEM),
            pl.BlockSpec((R,), lambda i: (jnp.minimum(2 * i + 2, 2 * n - 2),), memory_space=pltpu.SMEM),
            pl.BlockSpec((None, PEER_SLOTS, LANES), lambda i: (i // (LANES // (2 * G)), 0, 0)),
            pl.BlockSpec((2 * G, D), lambda i: (i, 0)),
            pl.BlockSpec((2 * G, D), lambda i: (i, 0)),
            pl.BlockSpec((None, None, 1, D), lambda i: (i // (S // (2 * G)), 5, 0, 0)),
            pl.BlockSpec((1, D), lambda i: (0, 0)),
            pl.BlockSpec(memory_space=pl.ANY),
        ],
        out_specs=pl.BlockSpec((2 * G, D), lambda i: (i, 0)),
        scratch_shapes=[pltpu.VMEM((2, R // SUBLANES, SUBLANES, D), jnp.uint32), pltpu.SemaphoreType.DMA((2,))],
        compiler_params=_params(("arbitrary",)),
        name="peer_eval",
    )(eidx.reshape(-1), eidx.reshape(-1), gate_t, h2, x2, mod4, final_g.reshape(1, D), uv)


SC_WORKERS = 32
SC_WINDOW = 128
SC_ROWS = 32
PEER_GD = 16
SC_SHARE_NUM, SC_SHARE_DEN = 5, 8
SC_PARTS = 2


def _sc_gather(tab, idx):
    n = idx.shape[0]
    width = tab.shape[1]
    per = n // SC_WORKERS
    nsub = SC_WINDOW // SC_ROWS
    mesh = plsc.VectorSubcoreMesh(core_axis_name="core", subcore_axis_name="subcore")

    @pl.kernel(out_type=jax.ShapeDtypeStruct((n, width), tab.dtype), mesh=mesh,
               scratch_types=[pltpu.VMEM((SC_WINDOW,), jnp.int32), pltpu.VMEM((2, SC_ROWS, width), tab.dtype),
                              pltpu.SemaphoreType.DMA((2,)), pltpu.SemaphoreType.DMA((2,))])
    def gather(x_hbm, i_hbm, o_hbm, idx_v, rows_v, gsem, wsem):
        wid = lax.axis_index("core") * (SC_WORKERS // 2) + lax.axis_index("subcore")

        def start_gather(k):
            return pltpu.async_copy(x_hbm.at[idx_v.at[pl.ds(k * SC_ROWS, SC_ROWS)]], rows_v.at[k % 2], gsem.at[k % 2])

        @pl.loop(0, per // SC_WINDOW)
        def _(w):
            base = wid * per + w * SC_WINDOW
            pltpu.sync_copy(i_hbm.at[pl.ds(base, SC_WINDOW)], idx_v)
            gat = [start_gather(0)] + [None] * (nsub - 1)
            wrt = [None] * nsub
            for k in range(nsub):
                if k + 1 < nsub:
                    if k >= 1:
                        wrt[k - 1].wait()
                    gat[k + 1] = start_gather(k + 1)
                gat[k].wait()
                wrt[k] = pltpu.async_copy(rows_v.at[k % 2], o_hbm.at[pl.ds(base + k * SC_ROWS, SC_ROWS)],
                                          wsem.at[k % 2])
            for k in range(max(nsub - 2, 0), nsub):
                wrt[k].wait()

    return gather(tab, idx)


def _peer_dense_kernel(rows_ref, gate_ref, h_ref, x_ref, gf_ref, fg_ref, o_ref, *, final):
    G = PEER_GD
    D = D_MODEL
    i = pl.program_id(0)
    lane = lax.broadcasted_iota(jnp.int32, (1, LANES), 1)
    tbase = (i % (LANES // G)) * G
    gate = gate_ref[...]
    outs = []
    for g in range(G):
        w_rows = rows_ref[PEER_SLOTS * g:PEER_SLOTS * (g + 1), :]
        u_rows = lax.bitcast_convert_type(w_rows & jnp.uint32(0xFFFF0000), F32)
        prod = u_rows * h_ref[g:g + 1, :]
        part = prod[:, 0:LANES]
        for c in range(1, D // LANES):
            part = part + prod[:, LANES * c:LANES * (c + 1)]
        act = jnp.sum(part, axis=1, keepdims=True)
        gcol = jnp.sum(jnp.where(lane == tbase + g, gate, 0.0), axis=1, keepdims=True)
        coef = gcol * (0.5 * act * (1.0 + lax.erf(act * (2.0 ** -0.5))))
        v_rows = lax.bitcast_convert_type(w_rows << 16, F32)
        outs.append(jnp.sum(v_rows * coef, axis=0, keepdims=True))
    xn = x_ref[...] + gf_ref[...] * jnp.concatenate(outs, axis=0)
    if final:
        ms = jnp.mean(xn * xn, axis=-1, keepdims=True)
        xn = xn * lax.rsqrt(ms + RMS_EPS) * fg_ref[...]
    o_ref[...] = xn


def _peer_eval_dense(rows, gate_t, h2, x2, mod4, final_g, S, final, tok0):
    T, D = x2.shape
    G = PEER_GD
    tb = rows.shape[0] // PEER_SLOTS
    assert tok0 % LANES == 0 and tb % LANES == 0
    blk0 = tok0 // G
    return pl.pallas_call(
        functools.partial(_peer_dense_kernel, final=final),
        out_shape=jax.ShapeDtypeStruct((tb, D), F32),
        grid=(tb // G,),
        in_specs=[
            pl.BlockSpec((G * PEER_SLOTS, D), lambda i: (i, 0)),
            pl.BlockSpec((None, PEER_SLOTS, LANES), lambda i: (i // (LANES // G), 0, 0)),
            pl.BlockSpec((G, D), lambda i: (blk0 + i, 0)),
            pl.BlockSpec((G, D), lambda i: (blk0 + i, 0)),
            pl.BlockSpec((None, None, 1, D), lambda i: ((blk0 + i) // (S // G), 5, 0, 0)),
            pl.BlockSpec((1, D), lambda i: (0, 0)),
        ],
        out_specs=pl.BlockSpec((G, D), lambda i: (i, 0)),
        compiler_params=_params(("parallel",)),
        name="peer_dense",
    )(rows, gate_t, h2, x2, mod4, final_g.reshape(1, D))


def _pack_uv(u, v):
    hi = lax.bitcast_convert_type(u.astype(BF16), jnp.uint16).astype(jnp.uint32) << 16
    lo = lax.bitcast_convert_type(v.astype(BF16), jnp.uint16).astype(jnp.uint32)
    return (hi | lo).reshape(u.shape[0], 1, u.shape[1])


def kernel(x, c, norm_mix_g, norm_ffn_g, final_norm_g, ada_w, ada_b, w_in, w_out, dif_lam, dif_subln_g, rw_mu, rw_w0,
           rw_w2, rw_a0, rw_a2, rw_g2, rw_kk, rw_ka, rw_rk, rw_ln_g, rw_ln_b, fox_bf, peer_wq, peer_subkeys, peer_u,
           peer_v):
    B, S, D = x.shape
    T = B * S
    depth = ada_w.shape[0]
    x2 = x.reshape(T, D)
    mod = _ada_mod(c, ada_w, ada_b)
    for l in range(depth):
        mod4 = mod[l].reshape(B, 6, 1, D)
        w_pad, w_vt = _pad_w_in(w_in[l])
        dqk, rcols, fqk, fl, vtd, vtf = _in_proj(x2, norm_mix_g[l], mod4, w_pad, w_vt, S)
        ya = _diff_attention(dqk, vtd, dif_lam[l], dif_subln_g[l], l, B, S)
        yb = _rwkv(rcols, rw_mu[l], rw_w0[l], rw_w2[l], rw_a0[l], rw_a2[l], rw_g2[l], rw_kk[l], rw_ka[l],
                   rw_rk[l].reshape(-1), rw_ln_g[l], rw_ln_b[l], B, S, prec=None)
        crep, crow = _fox_cum(fl, fox_bf[l], B, S)
        yc = _fox_attention(fqk, vtf, crep, crow, B, S)
        x2, h2 = _out_proj(ya, yb, yc, x2, w_out[l], norm_ffn_g[l], mod4, S)
        uv = _pack_uv(peer_u[l], peer_v[l])
        final = l == depth - 1
        tb = (T * SC_SHARE_NUM // SC_SHARE_DEN) // (SC_WORKERS * SC_WINDOW) * (SC_WORKERS * SC_WINDOW)
        ta = T - tb
        bounds = [ta + tb * j // SC_PARTS for j in range(SC_PARTS + 1)]
        parts = []
        for t0, t1 in zip(bounds[:-1], bounds[1:]):
            e_p, g_p = _peer_route(h2, peer_wq[l], peer_subkeys[l], t0, t1 - t0)
            parts.append((_sc_gather(uv.reshape(-1, D), e_p.reshape(-1)), g_p, t0))
        e_a, g_a = _peer_route(h2, peer_wq[l], peer_subkeys[l], 0, ta)
        outs = [_peer_eval(e_a, g_a, h2, x2, mod4, final_norm_g, uv, S, final, ntok=ta)]
        outs += [_peer_eval_dense(r, g_p, h2, x2, mod4, final_norm_g, S, final, t0) for r, g_p, t0 in parts]
        x2 = jnp.concatenate(outs, axis=0)
    return x2.reshape(B, S, D)
```

```python
import functools
import math

import jax
import jax.numpy as jnp
from jax import lax
from jax.experimental import pallas as pl
from jax.experimental.pallas import tpu as pltpu
from jax.experimental.pallas import tpu_sc as plsc

F32 = jnp.float32
BF16 = jnp.bfloat16
HIGHEST = lax.Precision.HIGHEST

D_MODEL = 1024
HEAD_DIM = 64
DIFF_HEADS = 6
DIFF_QK_DIM = HEAD_DIM // 2
RWKV_HEADS = 6
FOX_HEADS = 4
DIFF_WIDTH = DIFF_HEADS * HEAD_DIM
RWKV_WIDTH = RWKV_HEADS * HEAD_DIM
FOX_WIDTH = FOX_HEADS * HEAD_DIM
DECAY_LORA = 64
AAA_LORA = 64
GATE_LORA = 128
DIFF_COLS = 3 * DIFF_WIDTH
RWKV_COLS = 3 * RWKV_WIDTH + DECAY_LORA + AAA_LORA + GATE_LORA
PEER_HEADS = 8
PEER_NKEYS = 128
PEER_TOPK = 16
PEER_QDIM = 256
PEER_HALF = PEER_QDIM // 2
RMS_EPS = 1e-6
RWKV_LN_EPS = 64e-5

LANES = 128
SUBLANES = 8
RW_PAD_COLS = 3 * RWKV_WIDTH + 3 * LANES
VMEM_LIMIT = 56 * 1024 * 1024

RW_CHUNK = 64
RW_CHUNKS_PER_ITER = 4


def _params(sem, vmem=VMEM_LIMIT):
    return pltpu.CompilerParams(dimension_semantics=sem, vmem_limit_bytes=vmem)


def _nt(a, b, precision=None):
    return lax.dot_general(a, b, (((1,), (1,)), ((), ())), preferred_element_type=F32, precision=precision)


def _mm(a, b, precision=None):
    return jnp.dot(a, b, preferred_element_type=F32, precision=precision)


def _ada_kernel(c_ref, w_ref, b_ref, o_ref):
    c = c_ref[...]
    ca = c * jax.nn.sigmoid(c)
    o_ref[...] = _mm(ca, w_ref[...], HIGHEST) + b_ref[...]


def _ada_mod(c, ada_w, ada_b):
    L, D, N = ada_w.shape
    B = c.shape[0]
    tn = 1536
    return pl.pallas_call(
        _ada_kernel,
        out_shape=jax.ShapeDtypeStruct((L, B, N), F32),
        grid=(L, N // tn),
        in_specs=[
            pl.BlockSpec((B, D), lambda l, j: (0, 0)),
            pl.BlockSpec((None, D, tn), lambda l, j: (l, 0, j)),
            pl.BlockSpec((None, 1, tn), lambda l, j: (l, 0, j)),
        ],
        out_specs=pl.BlockSpec((None, B, tn), lambda l, j: (l, 0, j)),
        compiler_params=_params(("parallel", "parallel")),
        name="ada_mod",
    )(c, ada_w, ada_b.reshape(L, 1, N))


ATT_T = 512
QK_DIFF = 2 * DIFF_WIDTH
QK_FOX = 2 * FOX_WIDTH
VT_ROWS = DIFF_WIDTH + FOX_WIDTH
IN_PAD_COLS = QK_DIFF + RW_PAD_COLS + QK_FOX + LANES


def _inproj_kernel(x_ref, g_ref, sh_ref, sc_ref, w_ref, wvt_ref, d_ref, r_ref, f_ref, fl_ref, vtd_ref, vtf_ref, *, ta):
    x = x_ref[...]
    ms = jnp.mean(x * x, axis=-1, keepdims=True)
    y = x * lax.rsqrt(ms + RMS_EPS) * g_ref[...]
    h = (y * (1.0 + sc_ref[...]) + sh_ref[...]).astype(BF16)
    o1 = QK_DIFF
    o2 = o1 + RW_PAD_COLS
    o3 = o2 + QK_FOX
    d_ref[...] = _mm(h, w_ref[:, 0:o1]).astype(BF16)
    r_ref[...] = _mm(h, w_ref[:, o1:o2])
    f_ref[...] = _mm(h, w_ref[:, o2:o3]).astype(BF16)
    fl_ref[...] = _mm(h, w_ref[:, o3:o3 + LANES])
    vt = _nt(wvt_ref[...], h).astype(BF16)
    for s in range(x.shape[0] // ta):
        vtd_ref[s] = vt[0:DIFF_WIDTH, ta * s:ta * (s + 1)]
        vtf_ref[s] = vt[DIFF_WIDTH:, ta * s:ta * (s + 1)]


def _in_proj(x2, g, mod4, w_pad, w_vt, S):
    T, D = x2.shape
    tm = 512 if S % 512 == 0 else S
    ta = min(ATT_T, S)
    nb = S // tm
    row = lambda i: (i, 0)
    return pl.pallas_call(
        functools.partial(_inproj_kernel, ta=ta),
        out_shape=(
            jax.ShapeDtypeStruct((T, QK_DIFF), BF16),
            jax.ShapeDtypeStruct((T, RW_PAD_COLS), F32),
            jax.ShapeDtypeStruct((T, QK_FOX), BF16),
            jax.ShapeDtypeStruct((T, LANES), F32),
            jax.ShapeDtypeStruct((T // ta, DIFF_WIDTH, ta), BF16),
            jax.ShapeDtypeStruct((T // ta, FOX_WIDTH, ta), BF16),
        ),
        grid=(T // tm,),
        in_specs=[
            pl.BlockSpec((tm, D), row),
            pl.BlockSpec((1, D), lambda i: (0, 0)),
            pl.BlockSpec((None, None, 1, D), lambda i: (i // nb, 0, 0, 0)),
            pl.BlockSpec((None, None, 1, D), lambda i: (i // nb, 1, 0, 0)),
            pl.BlockSpec((D, IN_PAD_COLS), lambda i: (0, 0)),
            pl.BlockSpec((VT_ROWS, D), lambda i: (0, 0)),
        ],
        out_specs=(
            pl.BlockSpec((tm, QK_DIFF), row),
            pl.BlockSpec((tm, RW_PAD_COLS), row),
            pl.BlockSpec((tm, QK_FOX), row),
            pl.BlockSpec((tm, LANES), row),
            pl.BlockSpec((tm // ta, DIFF_WIDTH, ta), lambda i: (i, 0, 0)),
            pl.BlockSpec((tm // ta, FOX_WIDTH, ta), lambda i: (i, 0, 0)),
        ),
        compiler_params=_params(("parallel",)),
        name="in_proj",
    )(x2, g.reshape(1, D), mod4, mod4, w_pad, w_vt)


def _pad_w_in(w_in):
    D = w_in.shape[0]
    W = RWKV_WIDTH
    o = DIFF_COLS
    z64 = jnp.zeros((D, LANES - DECAY_LORA), w_in.dtype)
    rw = w_in[:, o:o + RWKV_COLS]
    fx = w_in[:, o + RWKV_COLS:]
    zf = jnp.zeros((D, LANES - FOX_HEADS), w_in.dtype)
    w_pad = jnp.concatenate([
        w_in[:, :QK_DIFF],
        rw[:, :3 * W], rw[:, 3 * W:3 * W + DECAY_LORA], z64,
        rw[:, 3 * W + DECAY_LORA:3 * W + DECAY_LORA + AAA_LORA], z64,
        rw[:, 3 * W + DECAY_LORA + AAA_LORA:],
        fx[:, :QK_FOX], fx[:, 3 * FOX_WIDTH:], zf,
    ], axis=1).astype(BF16)
    w_vt = jnp.concatenate([w_in[:, QK_DIFF:o], fx[:, QK_FOX:3 * FOX_WIDTH]], axis=1).T.astype(BF16)
    return w_pad, w_vt


LOG2E = math.log2(math.e)


ACC_ROWS = LANES + 16


def _scaled_q(q, c):
    return (q.astype(F32) * c).astype(BF16)


def _with_ones(vt):
    return jnp.concatenate([vt, jnp.ones((ACC_ROWS - LANES, vt.shape[1]), vt.dtype)], axis=0)


def _flash_step(s2s, vt1, m_ref, acc_ref):
    n = len(s2s)
    m_old = [m_ref[x] for x in range(n)]
    m_new = [jnp.maximum(m_old[x], jnp.max(s2s[x], axis=0, keepdims=True)) for x in range(n)]
    alpha = [jnp.exp2(m_old[x] - m_new[x]) for x in range(n)]
    p = [jnp.exp2(s2s[x] - m_new[x]).astype(BF16) for x in range(n)]
    pv = [_mm(vt1, p[x]) for x in range(n)]
    for x in range(n):
        acc_ref[x] = alpha[x] * acc_ref[x] + pv[x]
        m_ref[x] = m_new[x]


def _diff_attn_kernel(lam_ref, g_ref, q_ref, k_ref, vt_ref, o_ref, m_ref, acc_ref, *, tq, lam_init):
    i = pl.program_id(2)
    lane = lax.broadcasted_iota(jnp.int32, (1, LANES), 1)
    q = _scaled_q(q_ref[...], (DIFF_QK_DIM ** -0.5) * LOG2E)
    zero = jnp.zeros_like(q)
    qm = [jnp.where((lane >= DIFF_QK_DIM * x) & (lane < DIFF_QK_DIM * (x + 1)), q, zero) for x in range(4)]
    m_ref[...] = jnp.full(m_ref.shape, -jnp.inf, F32)
    acc_ref[...] = jnp.zeros(acc_ref.shape, F32)

    def step(j, diag):
        k = k_ref[pl.ds(pl.multiple_of(j * tq, tq), tq), :]
        vt1 = _with_ones(vt_ref[j])
        if diag:
            keep = (lax.broadcasted_iota(jnp.int32, (tq, tq), 1) >= lax.broadcasted_iota(jnp.int32, (tq, tq), 0))
        s2s = [_nt(k, qm[x]) for x in range(4)]
        if diag:
            s2s = [jnp.where(keep, s2, -jnp.inf) for s2 in s2s]
        _flash_step(s2s, vt1, m_ref, acc_ref)

    def body(j, carry):
        step(j, False)
        return carry

    lax.fori_loop(0, i, body, 0)
    step(i, True)

    lp = lam_ref[...]
    lam = (jnp.exp(jnp.sum(lp[0:1] * lp[1:2], axis=-1, keepdims=True))
           - jnp.exp(jnp.sum(lp[2:3] * lp[3:4], axis=-1, keepdims=True)) + lam_init)
    sm = [acc_ref[x, 0:LANES, :] / acc_ref[x, LANES:LANES + 1, :] for x in range(4)]
    outs = [sm[2 * hh] - lam * sm[2 * hh + 1] for hh in range(2)]
    row = lax.broadcasted_iota(jnp.int32, (LANES, 1), 0)
    o = jnp.where(row < HEAD_DIM, outs[0], outs[1])
    sq = o * o
    ms = jnp.where(row < HEAD_DIM, jnp.sum(sq[0:HEAD_DIM], axis=0, keepdims=True),
                   jnp.sum(sq[HEAD_DIM:], axis=0, keepdims=True)) * (1.0 / HEAD_DIM)
    y = o * lax.rsqrt(ms + RMS_EPS) * g_ref[...] * (1.0 - lam_init)
    o_ref[...] = y.T.astype(o_ref.dtype)


def _diff_attention(dqk, vtd, lam_params, subln_g, layer_idx, B, S):
    T = B * S
    tq = min(ATT_T, S)
    nq = S // tq
    npair = DIFF_HEADS // 2
    lam_init = 0.8 - 0.6 * math.exp(-0.3 * layer_idx)
    g2 = jnp.concatenate([subln_g, subln_g]).reshape(LANES, 1).astype(F32)
    return pl.pallas_call(
        functools.partial(_diff_attn_kernel, tq=tq, lam_init=lam_init),
        out_shape=jax.ShapeDtypeStruct((T, DIFF_WIDTH), BF16),
        grid=(B, npair, nq),
        in_specs=[
            pl.BlockSpec((4, DIFF_QK_DIM), lambda b, p, i: (0, 0)),
            pl.BlockSpec((LANES, 1), lambda b, p, i: (0, 0)),
            pl.BlockSpec((tq, LANES), lambda b, p, i: (b * nq + i, p)),
            pl.BlockSpec((S, LANES), lambda b, p, i: (b, npair + p)),
            pl.BlockSpec((nq, LANES, tq), lambda b, p, i: (b, p, 0)),
        ],
        out_specs=pl.BlockSpec((tq, LANES), lambda b, p, i: (b * nq + i, p)),
        scratch_shapes=[
            pltpu.VMEM((4, 1, tq), F32),
            pltpu.VMEM((4, ACC_ROWS, tq), F32),
        ],
        compiler_params=_params(("parallel", "parallel", "arbitrary")),
        name="diff_attn",
    )(lam_params, g2, dqk, dqk, vtd)


def _fox_cum_kernel(f_ref, b_ref, rep_ref, row_ref, *, S, tc):
    rr = lax.broadcasted_iota(jnp.int32, (tc, tc), 0)
    cc = lax.broadcasted_iota(jnp.int32, (tc, tc), 1)
    tri = (rr >= cc).astype(F32)
    sel_r = lax.broadcasted_iota(jnp.int32, (LANES, LANES), 0)
    carry = jnp.zeros((1, LANES), F32)
    for c in range(S // tc):
        z = f_ref[c * tc:(c + 1) * tc, :] + b_ref[...]
        logf = -(jnp.maximum(-z, 0.0) + jnp.log(1.0 + jnp.exp(-jnp.abs(z))))
        cum = _mm(tri, logf, HIGHEST) + carry
        carry = cum[tc - 1:tc, :]
        row_ref[:, c * tc:(c + 1) * tc] = cum.T[0:8, :]
        for h in range(FOX_HEADS):
            rep_ref[h, c * tc:(c + 1) * tc, :] = _mm(cum, (sel_r == h).astype(F32), HIGHEST)


def _fox_cum(fl, b_f, B, S):
    tc = 256 if S % 256 == 0 else S
    bpad = jnp.zeros((1, LANES), F32).at[0, :FOX_HEADS].set(b_f.astype(F32))
    return pl.pallas_call(
        functools.partial(_fox_cum_kernel, S=S, tc=tc),
        out_shape=(jax.ShapeDtypeStruct((B, FOX_HEADS, S, LANES), F32), jax.ShapeDtypeStruct((B, 8, S), F32)),
        grid=(B,),
        in_specs=[pl.BlockSpec((S, LANES), lambda b: (b, 0)), pl.BlockSpec((1, LANES), lambda b: (0, 0))],
        out_specs=(pl.BlockSpec((None, FOX_HEADS, S, LANES), lambda b: (b, 0, 0, 0)),
                   pl.BlockSpec((None, 8, S), lambda b: (b, 0, 0))),
        compiler_params=_params(("parallel",)),
        name="fox_cum",
    )(fl, bpad)


def _fox_attn_kernel(q_ref, k_ref, vt_ref, c0_ref, c1_ref, cr_ref, o_ref, m_ref, acc_ref, *, tq):
    p_id = pl.program_id(1)
    i = pl.program_id(2)
    lane = lax.broadcasted_iota(jnp.int32, (1, LANES), 1)
    q = _scaled_q(q_ref[...], (HEAD_DIM ** -0.5) * LOG2E)
    zero = jnp.zeros_like(q)
    qm = [jnp.where((lane >= HEAD_DIM * x) & (lane < HEAD_DIM * (x + 1)), q, zero) for x in range(2)]
    ck_refs = (c0_ref, c1_ref)
    cq = [cr_ref[2 * p_id + x, pl.ds(i, 1), :] for x in range(2)]
    m_ref[...] = jnp.full(m_ref.shape, -jnp.inf, F32)
    acc_ref[...] = jnp.zeros(acc_ref.shape, F32)

    def step(j, diag):
        off = pl.multiple_of(j * tq, tq)
        k = k_ref[pl.ds(off, tq), :]
        vt1 = _with_ones(vt_ref[j])
        if diag:
            keep = (lax.broadcasted_iota(jnp.int32, (tq, tq), 1) >= lax.broadcasted_iota(jnp.int32, (tq, tq), 0))
        s2s = []
        for x in range(2):
            ck = ck_refs[x][pl.ds(off, tq), :]
            bias = (cq[x] - jnp.concatenate([ck] * (tq // LANES), axis=1)) * LOG2E
            s2s.append(_nt(k, qm[x]) + bias)
        if diag:
            s2s = [jnp.where(keep, s2, -jnp.inf) for s2 in s2s]
        _flash_step(s2s, vt1, m_ref, acc_ref)

    def body(j, carry):
        step(j, False)
        return carry

    lax.fori_loop(0, i, body, 0)
    step(i, True)
    row = lax.broadcasted_iota(jnp.int32, (LANES, 1), 0)
    sm = [acc_ref[x, 0:LANES, :] / acc_ref[x, LANES:LANES + 1, :] for x in range(2)]
    o = jnp.where(row < HEAD_DIM, sm[0], sm[1])
    o_ref[...] = o.T.astype(o_ref.dtype)


def _fox_attention(fqk, vtf, crep, crow, B, S):
    T = B * S
    tq = min(ATT_T, S)
    nq = S // tq
    npair = FOX_HEADS // 2
    crow4 = crow.reshape(B, 8, nq, tq)
    rep = lambda x: pl.BlockSpec((None, None, S, LANES), lambda b, p, i: (b, 2 * p + x, 0, 0))
    return pl.pallas_call(
        functools.partial(_fox_attn_kernel, tq=tq),
        out_shape=jax.ShapeDtypeStruct((T, FOX_WIDTH), BF16),
        grid=(B, npair, nq),
        in_specs=[
            pl.BlockSpec((tq, LANES), lambda b, p, i: (b * nq + i, p)),
            pl.BlockSpec((S, LANES), lambda b, p, i: (b, npair + p)),
            pl.BlockSpec((nq, LANES, tq), lambda b, p, i: (b, p, 0)),
            rep(0), rep(1),
            pl.BlockSpec((None, 8, nq, tq), lambda b, p, i: (b, 0, 0, 0)),
        ],
        out_specs=pl.BlockSpec((tq, LANES), lambda b, p, i: (b * nq + i, p)),
        scratch_shapes=[
            pltpu.VMEM((2, 1, tq), F32),
            pltpu.VMEM((2, ACC_ROWS, tq), F32),
        ],
        compiler_params=_params(("parallel", "parallel", "arbitrary")),
        name="fox_attn",
    )(fqk, fqk, vtf, crep, crep, crow4)


def _split3(x):
    hi = x.astype(BF16)
    r1 = x - hi.astype(F32)
    mid = r1.astype(BF16)
    lo = (r1 - mid.astype(F32)).astype(BF16)
    return hi, mid, lo


def _seg_sum(x, seg, npair):
    parts = _split3(x)
    return jnp.concatenate(
        [sum(_mm(t[:, LANES * p:LANES * (p + 1)], seg) for t in parts) for p in range(npair)], axis=1)


def _rwkv_kernel(x_ref, mu_ref, w0_ref, w2_ref, a0_ref, a2_ref, g2_ref, kk_ref, ka_ref, rk_ref, lng_ref, lnb_ref,
                 o_ref, carry_ref, st_ref, at_ref, rt_ref, bt_ref, kt_ref, v_ref, wc_ref, y_ref, g_ref, bon_ref,
                 *, tt, prec):
    i = pl.program_id(1)
    W = RWKV_WIDTH
    C = RW_CHUNK
    CPI = math.gcd(RW_CHUNKS_PER_ITER, tt // C)
    npair = RWKV_HEADS // 2

    @pl.when(i == 0)
    def _():
        carry_ref[...] = jnp.zeros(carry_ref.shape, F32)
        st_ref[...] = jnp.zeros(st_ref.shape, F32)

    x = x_ref[...]
    rows = lax.broadcasted_iota(jnp.int32, (tt, 1), 0)
    prev = jnp.where(rows == 0, carry_ref[...], pltpu.roll(x, 1, axis=0))
    carry_ref[...] = x[tt - 1:tt, :]
    xs = x + (prev - x) * mu_ref[...]
    r = xs[:, 0:W]
    k = xs[:, W:2 * W]
    v = xs[:, 2 * W:3 * W]
    xw = xs[:, 3 * W:3 * W + LANES]
    xa = xs[:, 3 * W + LANES:3 * W + 2 * LANES]
    xg = xs[:, 3 * W + 2 * LANES:]
    wl = w0_ref[...] + _mm(jnp.tanh(xw), w2_ref[...], HIGHEST)
    w = -(jnp.maximum(-wl, 0.0) + jnp.log(1.0 + jnp.exp(-jnp.abs(wl)))) - 0.5
    logdec = -jnp.exp(w)
    a = jax.nn.sigmoid(a0_ref[...] + _mm(xa, a2_ref[...], HIGHEST))
    g_ref[...] = _mm(jax.nn.sigmoid(xg), g2_ref[...], HIGHEST)

    r_i = lax.broadcasted_iota(jnp.int32, (LANES, LANES), 0)
    c_i = lax.broadcasted_iota(jnp.int32, (LANES, LANES), 1)
    seg = ((r_i // HEAD_DIM) == (c_i // HEAD_DIM)).astype(BF16)
    kkv = k * kk_ref[...]
    kkn = kkv / jnp.maximum(jnp.sqrt(_seg_sum(kkv * kkv, seg, npair)), 1e-12)
    k2 = k * (1.0 + (a - 1.0) * ka_ref[...])
    bon_ref[...] = _seg_sum(r * k2 * rk_ref[...], seg, npair) * v

    rt_i = lax.broadcasted_iota(jnp.int32, (tt, tt), 0)
    ct_i = lax.broadcasted_iota(jnp.int32, (tt, tt), 1)
    tri = (((rt_i // C) == (ct_i // C)) & (rt_i >= ct_i)).astype(BF16)
    cum = sum(_mm(tri, t) for t in _split3(logdec))
    winv = jnp.exp(-cum)
    wcum = jnp.exp(cum)
    at_ref[...] = -kkn * jnp.exp(cum - logdec)
    bt_ref[...] = kkn * a * winv
    kt_ref[...] = k2 * winv
    rt_ref[...] = r * wcum
    v_ref[...] = v
    wc_ref[...] = wcum

    lane = lax.broadcasted_iota(jnp.int32, (1, LANES), 1)
    lo = lane < HEAD_DIM
    tpos = r_i % C
    ipos = c_i % C
    strict = tpos > ipos
    incl = tpos >= ipos
    eye = r_i == c_i

    def stack2(m):
        return jnp.concatenate([jnp.where(lo, m, 0.0), jnp.where(lo, 0.0, m)], axis=0)

    def cast(m):
        return m if prec is not None else m.astype(BF16)

    def mm(p, q):
        return _mm(cast(p), cast(q), prec)

    def nt(p, q):
        return _nt(cast(p), cast(q), prec)

    def chunks(cc, carry):
        units = [(ci, p) for ci in range(CPI) for p in range(npair)]
        r0 = [pl.multiple_of((cc * CPI + ci) * C, C) for ci in range(CPI)]
        ld = lambda ref, ci, p: stack2(ref[pl.ds(r0[ci], C), LANES * p:LANES * (p + 1)])
        a2 = {u: ld(at_ref, *u) for u in units}
        r2 = {u: ld(rt_ref, *u) for u in units}
        b2 = {u: ld(bt_ref, *u) for u in units}
        k2s = {u: ld(kt_ref, *u) for u in units}
        v2 = {u: ld(v_ref, *u) for u in units}
        wl = {(ci, p): wc_ref[pl.ds(pl.multiple_of(r0[ci] + C - 8, 8), 8), LANES * p:LANES * (p + 1)][7:8, :]
              for (ci, p) in units}
        ar = {u: jnp.concatenate([a2[u], r2[u]], axis=0) for u in units}
        mb = {u: nt(ar[u], b2[u]) for u in units}
        mk = {u: nt(ar[u], k2s[u]) for u in units}
        lab = {u: jnp.where(strict, mb[u][0:LANES], 0.0) for u in units}
        mrb = {u: jnp.where(incl, mb[u][LANES:], 0.0) for u in units}
        lak = {u: jnp.where(strict, mk[u][0:LANES], 0.0) for u in units}
        mrk = {u: jnp.where(incl, mk[u][LANES:], 0.0) for u in units}
        xx = {u: jnp.concatenate([a2[u], mm(lak[u], v2[u])], axis=1) for u in units}
        lp = lab
        for it in range(6):
            xx = {u: xx[u] + mm(lp[u], xx[u]) for u in units}
            if it < 5:
                lp = {u: mm(lp[u], lp[u]) for u in units}
        mq = {u: mm(mrb[u], xx[u]) for u in units}
        mv = {u: mm(mrk[u], v2[u]) for u in units}
        bx = {u: mm((b2[u] * wl[u]).T, xx[u]) for u in units}
        kv = {u: mm((k2s[u] * wl[u]).T, v2[u]) for u in units}
        st = [st_ref[p] for p in range(npair)]
        for (ci, p) in units:
            u = (ci, p)
            q1 = r2[u] + mq[u][:, 0:LANES]
            q2 = mq[u][:, LANES:] + mv[u]
            gmat = jnp.where(eye, jnp.broadcast_to(wl[u], (LANES, LANES)), 0.0) + bx[u][:, 0:LANES]
            hmat = bx[u][:, LANES:] + kv[u]
            gs = mm(jnp.concatenate([gmat, q1], axis=0), st[p])
            st[p] = gs[0:LANES] + hmat
            yy = gs[LANES:] + q2
            y_ref[pl.ds(r0[ci], C), LANES * p:LANES * (p + 1)] = yy[0:C] + yy[C:]
        for p in range(npair):
            st_ref[p] = st[p]
        return carry

    lax.fori_loop(0, tt // (C * CPI), chunks, 0)

    y = y_ref[...]
    inv = 1.0 / HEAD_DIM
    mean = _seg_sum(y, seg, npair) * inv
    yc = y - mean
    var = _seg_sum(yc * yc, seg, npair) * inv
    yn = yc * lax.rsqrt(var + RWKV_LN_EPS) * lng_ref[...] + lnb_ref[...]
    o_ref[...] = ((yn + bon_ref[...]) * g_ref[...]).astype(o_ref.dtype)


def _rwkv(rcols, mu, w0, w2, a0, a2, g2, k_k, k_a, r_k, ln_g, ln_b, B, S, prec=HIGHEST):
    T = B * S
    W = RWKV_WIDTH
    tt = 512 if S % 512 == 0 else S
    nt_ = S // tt
    npair = RWKV_HEADS // 2
    pad = LANES - DECAY_LORA
    mu_p = jnp.concatenate([mu[:3 * W], mu[3 * W:3 * W + DECAY_LORA], jnp.zeros((pad,), F32),
                            mu[3 * W + DECAY_LORA:3 * W + DECAY_LORA + AAA_LORA], jnp.zeros((pad,), F32),
                            mu[3 * W + DECAY_LORA + AAA_LORA:]]).reshape(1, RW_PAD_COLS)
    w2p = jnp.concatenate([w2, jnp.zeros((pad, W), F32)], axis=0)
    a2p = jnp.concatenate([a2, jnp.zeros((pad, W), F32)], axis=0)
    vec = lambda t: t.reshape(1, W).astype(F32)
    full = lambda shape: pl.BlockSpec(shape, lambda b, i: (0,) * len(shape))
    sc = lambda: pltpu.VMEM((tt, W), F32)
    return pl.pallas_call(
        functools.partial(_rwkv_kernel, tt=tt, prec=prec),
        out_shape=jax.ShapeDtypeStruct((T, W), BF16),
        grid=(B, nt_),
        in_specs=[
            pl.BlockSpec((tt, RW_PAD_COLS), lambda b, i: (b * nt_ + i, 0)),
            full((1, RW_PAD_COLS)), full((1, W)), full((LANES, W)), full((1, W)), full((LANES, W)),
            full((GATE_LORA, W)), full((1, W)), full((1, W)), full((1, W)), full((1, W)), full((1, W)),
        ],
        out_specs=pl.BlockSpec((tt, W), lambda b, i: (b * nt_ + i, 0)),
        scratch_shapes=[
            pltpu.VMEM((1, RW_PAD_COLS), F32),
            pltpu.VMEM((npair, LANES, LANES), F32),
            sc(), sc(), sc(), sc(), sc(), sc(), sc(), sc(), sc(),
        ],
        compiler_params=_params(("parallel", "arbitrary")),
        name="rwkv7",
    )(rcols, mu_p, vec(w0), w2p, vec(a0), a2p, g2, vec(k_k), vec(k_a), vec(r_k), vec(ln_g), vec(ln_b))


def _outproj_kernel(ya_ref, yb_ref, yc_ref, x_ref, w_ref, gm_ref, g_ref, sh_ref, sc_ref, xo_ref, ho_ref):
    o1 = DIFF_WIDTH
    o2 = o1 + RWKV_WIDTH
    mix = (_mm(ya_ref[...], w_ref[0:o1, :]) + _mm(yb_ref[...], w_ref[o1:o2, :])
           + _mm(yc_ref[...], w_ref[o2:, :]))
    xn = x_ref[...] + gm_ref[...] * mix
    xo_ref[...] = xn
    ms = jnp.mean(xn * xn, axis=-1, keepdims=True)
    y = xn * lax.rsqrt(ms + RMS_EPS) * g_ref[...]
    ho_ref[...] = y * (1.0 + sc_ref[...]) + sh_ref[...]


def _out_proj(ya, yb, yc, x2, w_out, g, mod4, S):
    T, D = x2.shape
    tm = 512 if S % 512 == 0 else S
    nb = S // tm
    row = lambda i: (i, 0)
    modspec = lambda which: pl.BlockSpec((None, None, 1, D), lambda i: (i // nb, which, 0, 0))
    return pl.pallas_call(
        _outproj_kernel,
        out_shape=(jax.ShapeDtypeStruct((T, D), F32), jax.ShapeDtypeStruct((T, D), F32)),
        grid=(T // tm,),
        in_specs=[
            pl.BlockSpec((tm, DIFF_WIDTH), row), pl.BlockSpec((tm, RWKV_WIDTH), row), pl.BlockSpec((tm, FOX_WIDTH), row),
            pl.BlockSpec((tm, D), row),
            pl.BlockSpec((D, D), lambda i: (0, 0)),
            modspec(2),
            pl.BlockSpec((1, D), lambda i: (0, 0)),
            modspec(3), modspec(4),
        ],
        out_specs=(pl.BlockSpec((tm, D), row), pl.BlockSpec((tm, D), row)),
        compiler_params=_params(("parallel",)),
        name="out_proj",
    )(ya, yb, yc, x2, w_out.astype(BF16), mod4, g.reshape(1, D), mod4, mod4)


def _top16(s, iota_f, n):
    vals, poss = [], []
    for _ in range(PEER_TOPK):
        m = jnp.max(s, axis=0, keepdims=True)
        pos = jnp.min(jnp.where(s == m, iota_f, float(n)), axis=0, keepdims=True)
        vals.append(m)
        poss.append(pos)
        s = jnp.where(iota_f == pos, -jnp.inf, s)
    return jnp.concatenate(vals, axis=0), jnp.concatenate(poss, axis=0)


ROUTE_UNROLL = 8
PEER_NCAND = 56


def _peer_cand_tables():
    K = PEER_TOPK
    pairs = [(a, b) for a in range(K) for b in range(K) if (a + 1) * (b + 1) <= K]
    n = PEER_NCAND
    p0 = [[0.0] * K for _ in range(n)]
    p1 = [[0.0] * K for _ in range(n)]
    pad = [0.0] * n
    pos = [float(K * K + r) for r in range(n)]
    for r, (a, b) in enumerate(pairs):
        p0[r][a] = 1.0
        p1[r][b] = 1.0
        pos[r] = float(a * K + b)
    for r in range(len(pairs), n):
        pad[r] = -float("inf")
    col = lambda v: jnp.broadcast_to(jnp.asarray(v, F32)[:, None], (n, LANES))
    return jnp.asarray(p0, F32), jnp.asarray(p1, F32), col(pad), col(pos)


def _peer_route_kernel(h_ref, wq_ref, sk_ref, p0_ref, p1_ref, cpad_ref, cpos_ref, idx_ref, gate_ref, q_scr, e_scr,
                       g_scr):
    K = PEER_TOPK
    hb = h_ref[...].astype(BF16)
    q = _mm(hb, wq_ref[...])
    for hc in range(2 * PEER_HEADS):
        q_scr[hc] = q[:, LANES * hc:LANES * (hc + 1)].astype(BF16)
    iota_n = lax.broadcasted_iota(jnp.int32, (PEER_NKEYS, LANES), 0).astype(F32)
    cpos = cpos_ref[...]

    def one_head(h):
        sv0, si0 = _top16(_nt(sk_ref[2 * h], q_scr[2 * h]), iota_n, PEER_NKEYS)
        sv1, si1 = _top16(_nt(sk_ref[2 * h + 1], q_scr[2 * h + 1]), iota_n, PEER_NKEYS)
        cand = _mm(p0_ref[...], sv0, HIGHEST) + _mm(p1_ref[...], sv1, HIGHEST) + cpad_ref[...]
        cidx = _mm(p0_ref[...], si0) * float(PEER_NKEYS) + _mm(p1_ref[...], si1)
        fv, es = [], []
        for _ in range(K):
            m = jnp.max(cand, axis=0, keepdims=True)
            pos = jnp.min(jnp.where(cand == m, cpos, float(2 * K * K)), axis=0, keepdims=True)
            hit = cpos == pos
            fv.append(m)
            es.append(jnp.max(jnp.where(hit, cidx, -1.0), axis=0, keepdims=True))
            cand = jnp.where(hit, -jnp.inf, cand)
        fv = jnp.concatenate(fv, axis=0)
        ex = jnp.exp(fv - fv[0:1, :])
        g_scr[h] = ex / jnp.sum(ex, axis=0, keepdims=True)
        e_scr[h] = jnp.concatenate(es, axis=0)

    def heads(hh, carry):
        for j in range(ROUTE_UNROLL):
            one_head(hh * ROUTE_UNROLL + j)
        return carry

    lax.fori_loop(0, PEER_HEADS // ROUTE_UNROLL, heads, 0)
    e = e_scr[...].reshape(PEER_HEADS * K, LANES)
    idx_ref[...] = e.T.astype(jnp.int32)
    gate_ref[...] = g_scr[...].reshape(PEER_HEADS * K, LANES)


def _peer_route(h2, wq, subkeys, tok0=0, ntok=None):
    T, D = h2.shape
    T = T - tok0 if ntok is None else ntok
    tm = LANES
    blk0 = tok0 // tm
    nq = 2 * PEER_HEADS
    sk = subkeys.reshape(nq, PEER_NKEYS, PEER_HALF).astype(BF16)
    p0, p1, cpad, cpos = _peer_cand_tables()
    const = lambda shape: pl.BlockSpec(shape, lambda i: (0,) * len(shape))
    return pl.pallas_call(
        _peer_route_kernel,
        out_shape=(jax.ShapeDtypeStruct((T, PEER_HEADS * PEER_TOPK), jnp.int32),
                   jax.ShapeDtypeStruct((T // tm, PEER_HEADS * PEER_TOPK, tm), F32)),
        grid=(T // tm,),
        in_specs=[
            pl.BlockSpec((tm, D), lambda i: (blk0 + i, 0)),
            const((D, nq * PEER_HALF)),
            const((nq, PEER_NKEYS, PEER_HALF)),
            const((PEER_NCAND, PEER_TOPK)), const((PEER_NCAND, PEER_TOPK)),
            const((PEER_NCAND, LANES)), const((PEER_NCAND, LANES)),
        ],
        out_specs=(pl.BlockSpec((tm, PEER_HEADS * PEER_TOPK), lambda i: (i, 0)),
                   pl.BlockSpec((None, PEER_HEADS * PEER_TOPK, tm), lambda i: (i, 0, 0))),
        scratch_shapes=[
            pltpu.VMEM((nq, tm, PEER_HALF), BF16),
            pltpu.VMEM((PEER_HEADS, PEER_TOPK, tm), F32),
            pltpu.VMEM((PEER_HEADS, PEER_TOPK, tm), F32),
        ],
        compiler_params=_params(("parallel",)),
        name="peer_route",
    )(h2, wq.astype(BF16), sk, p0, p1, cpad, cpos)


PEER_G = 16
PEER_SLOTS = PEER_HEADS * PEER_TOPK


def _peer_eval_kernel(idx_ref, idxn_ref, gate_ref, h_ref, x_ref, gf_ref, fg_ref, uv_ref, o_ref, buf, sem, *, final):
    G = PEER_G
    R = G * PEER_SLOTS
    D = D_MODEL
    tiles = PEER_SLOTS // SUBLANES
    i = pl.program_id(0)
    n = pl.num_programs(0)

    def start(ids, off, s, t, u):
        pltpu.make_async_copy(uv_ref.at[ids[off + t * SUBLANES + u]], buf.at[s, t, pl.ds(u, 1), :],
                              sem.at[s]).start(priority=u % 2)

    def wait(s):
        pltpu.make_async_copy(buf.at[s], buf.at[s], sem.at[s]).wait()

    @pl.when(i == 0)
    def _():
        def body(t, carry):
            for u in range(SUBLANES):
                start(idx_ref, 0, 0, t, u)
            return carry
        lax.fori_loop(0, R // SUBLANES, body, 0)

    lane = lax.broadcasted_iota(jnp.int32, (1, LANES), 1)
    tbase = (i % (LANES // (2 * G))) * (2 * G)
    gate = gate_ref[...]
    outs = []
    for s in range(2):
        wait(s)
        nxt_ids, nxt_off = (idx_ref, R) if s == 0 else (idxn_ref, 0)
        for g in range(G):
            for t in range(tiles * g, tiles * (g + 1)):
                for u in range(SUBLANES):
                    start(nxt_ids, nxt_off, 1 - s, t, u)
            w_rows = buf[s, tiles * g:tiles * (g + 1)].reshape(PEER_SLOTS, D)
            u_rows = lax.bitcast_convert_type(w_rows & jnp.uint32(0xFFFF0000), F32)
            prod = u_rows * h_ref[G * s + g:G * s + g + 1, :]
            part = prod[:, 0:LANES]
            for c in range(1, D // LANES):
                part = part + prod[:, LANES * c:LANES * (c + 1)]
            act = jnp.sum(part, axis=1, keepdims=True)
            gcol = jnp.sum(jnp.where(lane == tbase + G * s + g, gate, 0.0), axis=1, keepdims=True)
            coef = gcol * (0.5 * act * (1.0 + lax.erf(act * (2.0 ** -0.5))))
            v_rows = lax.bitcast_convert_type(w_rows << 16, F32)
            outs.append(jnp.sum(v_rows * coef, axis=0, keepdims=True))
    xn = x_ref[...] + gf_ref[...] * jnp.concatenate(outs, axis=0)
    if final:
        ms = jnp.mean(xn * xn, axis=-1, keepdims=True)
        xn = xn * lax.rsqrt(ms + RMS_EPS) * fg_ref[...]
    o_ref[...] = xn

    @pl.when(i == n - 1)
    def _():
        wait(0)


def _peer_eval(eidx, gate_t, h2, x2, mod4, final_g, uv, S, final, ntok=None):
    T, D = x2.shape
    T = T if ntok is None else ntok
    G = PEER_G
    R = G * PEER_SLOTS
    n = T // (2 * G)
    return pl.pallas_call(
        functools.partial(_peer_eval_kernel, final=final),
        out_shape=jax.ShapeDtypeStruct(x2.shape, F32),
        input_output_aliases={4: 0},
        grid=(n,),
        in_specs=[
            pl.BlockSpec((2 * R,), lambda i: (i,), memory_space=pltpu.SMEM),
            pl.BlockSpec((R,), lambda i: (jnp.minimum(2 * i + 2, 2 * n - 2),), memory_space=pltpu.SMEM),
            pl.BlockSpec((None, PEER_SLOTS, LANES), lambda i: (i // (LANES // (2 * G)), 0, 0)),
            pl.BlockSpec((2 * G, D), lambda i: (i, 0)),
            pl.BlockSpec((2 * G, D), lambda i: (i, 0)),
            pl.BlockSpec((None, None, 1, D), lambda i: (i // (S // (2 * G)), 5, 0, 0)),
            pl.BlockSpec((1, D), lambda i: (0, 0)),
            pl.BlockSpec(memory_space=pl.ANY),
        ],
        out_specs=pl.BlockSpec((2 * G, D), lambda i: (i, 0)),
        scratch_shapes=[pltpu.VMEM((2, R // SUBLANES, SUBLANES, D), jnp.uint32), pltpu.SemaphoreType.DMA((2,))],
        compiler_params=_params(("arbitrary",)),
        name="peer_eval",
    )(eidx.reshape(-1), eidx.reshape(-1), gate_t, h2, x2, mod4, final_g.reshape(1, D), uv)


SC_WORKERS = 32
SC_WINDOW = 128
SC_ROWS = 32
PEER_GD = 16
SC_SHARE_NUM, SC_SHARE_DEN = 5, 8
SC_PARTS = 2
BATCH_GROUPS = 2


def _sc_gather(tab, idx):
    n = idx.shape[0]
    width = tab.shape[1]
    per = n // SC_WORKERS
    nsub = SC_WINDOW // SC_ROWS
    mesh = plsc.VectorSubcoreMesh(core_axis_name="core", subcore_axis_name="subcore")

    @pl.kernel(out_type=jax.ShapeDtypeStruct((n, width), tab.dtype), mesh=mesh,
               scratch_types=[pltpu.VMEM((SC_WINDOW,), jnp.int32), pltpu.VMEM((2, SC_ROWS, width), tab.dtype),
                              pltpu.SemaphoreType.DMA((2,)), pltpu.SemaphoreType.DMA((2,))])
    def gather(x_hbm, i_hbm, o_hbm, idx_v, rows_v, gsem, wsem):
        wid = lax.axis_index("core") * (SC_WORKERS // 2) + lax.axis_index("subcore")

        def start_gather(k):
            return pltpu.async_copy(x_hbm.at[idx_v.at[pl.ds(k * SC_ROWS, SC_ROWS)]], rows_v.at[k % 2], gsem.at[k % 2])

        @pl.loop(0, per // SC_WINDOW)
        def _(w):
            base = wid * per + w * SC_WINDOW
            pltpu.sync_copy(i_hbm.at[pl.ds(base, SC_WINDOW)], idx_v)
            gat = [start_gather(0)] + [None] * (nsub - 1)
            wrt = [None] * nsub
            for k in range(nsub):
                if k + 1 < nsub:
                    if k >= 1:
                        wrt[k - 1].wait()
                    gat[k + 1] = start_gather(k + 1)
                gat[k].wait()
                wrt[k] = pltpu.async_copy(rows_v.at[k % 2], o_hbm.at[pl.ds(base + k * SC_ROWS, SC_ROWS)],
                                          wsem.at[k % 2])
            for k in range(max(nsub - 2, 0), nsub):
                wrt[k].wait()

    return gather(tab, idx)


def _peer_dense_kernel(rows_ref, gate_ref, h_ref, x_ref, gf_ref, fg_ref, o_ref, *, final):
    G = PEER_GD
    D = D_MODEL
    i = pl.program_id(0)
    lane = lax.broadcasted_iota(jnp.int32, (1, LANES), 1)
    tbase = (i % (LANES // G)) * G
    gate = gate_ref[...]
    outs = []
    for g in range(G):
        w_rows = rows_ref[PEER_SLOTS * g:PEER_SLOTS * (g + 1), :]
        u_rows = lax.bitcast_convert_type(w_rows & jnp.uint32(0xFFFF0000), F32)
        prod = u_rows * h_ref[g:g + 1, :]
        part = prod[:, 0:LANES]
        for c in range(1, D // LANES):
            part = part + prod[:, LANES * c:LANES * (c + 1)]
        act = jnp.sum(part, axis=1, keepdims=True)
        gcol = jnp.sum(jnp.where(lane == tbase + g, gate, 0.0), axis=1, keepdims=True)
        coef = gcol * (0.5 * act * (1.0 + lax.erf(act * (2.0 ** -0.5))))
        v_rows = lax.bitcast_convert_type(w_rows << 16, F32)
        outs.append(jnp.sum(v_rows * coef, axis=0, keepdims=True))
    xn = x_ref[...] + gf_ref[...] * jnp.concatenate(outs, axis=0)
    if final:
        ms = jnp.mean(xn * xn, axis=-1, keepdims=True)
        xn = xn * lax.rsqrt(ms + RMS_EPS) * fg_ref[...]
    o_ref[...] = xn


def _peer_eval_dense(rows, gate_t, h2, x2, mod4, final_g, S, final, tok0):
    T, D = x2.shape
    G = PEER_GD
    tb = rows.shape[0] // PEER_SLOTS
    assert tok0 % LANES == 0 and tb % LANES == 0
    blk0 = tok0 // G
    return pl.pallas_call(
        functools.partial(_peer_dense_kernel, final=final),
        out_shape=jax.ShapeDtypeStruct(x2.shape, F32),
        input_output_aliases={3: 0},
        grid=(tb // G,),
        in_specs=[
            pl.BlockSpec((G * PEER_SLOTS, D), lambda i: (i, 0)),
            pl.BlockSpec((None, PEER_SLOTS, LANES), lambda i: (i // (LANES // G), 0, 0)),
            pl.BlockSpec((G, D), lambda i: (blk0 + i, 0)),
            pl.BlockSpec((G, D), lambda i: (blk0 + i, 0)),
            pl.BlockSpec((None, None, 1, D), lambda i: ((blk0 + i) // (S // G), 5, 0, 0)),
            pl.BlockSpec((1, D), lambda i: (0, 0)),
        ],
        out_specs=pl.BlockSpec((G, D), lambda i: (blk0 + i, 0)),
        compiler_params=_params(("parallel",)),
        name="peer_dense",
    )(rows, gate_t, h2, x2, mod4, final_g.reshape(1, D))


def _pack_uv(u, v):
    hi = lax.bitcast_convert_type(u.astype(BF16), jnp.uint16).astype(jnp.uint32) << 16
    lo = lax.bitcast_convert_type(v.astype(BF16), jnp.uint16).astype(jnp.uint32)
    return (hi | lo).reshape(u.shape[0], 1, u.shape[1])


def kernel(x, c, norm_mix_g, norm_ffn_g, final_norm_g, ada_w, ada_b, w_in, w_out, dif_lam, dif_subln_g, rw_mu, rw_w0,
           rw_w2, rw_a0, rw_a2, rw_g2, rw_kk, rw_ka, rw_rk, rw_ln_g, rw_ln_b, fox_bf, peer_wq, peer_subkeys, peer_u,
           peer_v):
    B, S, D = x.shape
    depth = ada_w.shape[0]
    mod = _ada_mod(c, ada_w, ada_b)
    ngrp = BATCH_GROUPS if B % BATCH_GROUPS == 0 else 1
    bg = B // ngrp
    tg = bg * S
    prep = [(_pad_w_in(w_in[l]), _pack_uv(peer_u[l], peer_v[l])) for l in range(depth)]

    def mix_route(l, g, x2):
        mod4 = mod[l, g * bg:(g + 1) * bg].reshape(bg, 6, 1, D)
        (w_pad, w_vt), uv = prep[l]
        dqk, rcols, fqk, fl, vtd, vtf = _in_proj(x2, norm_mix_g[l], mod4, w_pad, w_vt, S)
        ya = _diff_attention(dqk, vtd, dif_lam[l], dif_subln_g[l], l, bg, S)
        yb = _rwkv(rcols, rw_mu[l], rw_w0[l], rw_w2[l], rw_a0[l], rw_a2[l], rw_g2[l], rw_kk[l], rw_ka[l],
                   rw_rk[l].reshape(-1), rw_ln_g[l], rw_ln_b[l], bg, S, prec=None)
        crep, crow = _fox_cum(fl, fox_bf[l], bg, S)
        yc = _fox_attention(fqk, vtf, crep, crow, bg, S)
        x2, h2 = _out_proj(ya, yb, yc, x2, w_out[l], norm_ffn_g[l], mod4, S)
        tb = (tg * SC_SHARE_NUM // SC_SHARE_DEN) // (SC_WORKERS * SC_WINDOW) * (SC_WORKERS * SC_WINDOW)
        ta = tg - tb
        bounds = [ta + tb * j // SC_PARTS for j in range(SC_PARTS + 1)]
        parts = []
        for t0, t1 in zip(bounds[:-1], bounds[1:]):
            if t1 == t0:
                continue
            e_p, g_p = _peer_route(h2, peer_wq[l], peer_subkeys[l], t0, t1 - t0)
            parts.append((_sc_gather(uv.reshape(-1, D), e_p.reshape(-1)), g_p, t0))
        e_a, g_a = _peer_route(h2, peer_wq[l], peer_subkeys[l], 0, ta)
        return dict(x2=x2, h2=h2, mod4=mod4, uv=uv, ta=ta, e_a=e_a, g_a=g_a, parts=parts)

    def evaluate(l, st):
        final = l == depth - 1
        x2 = _peer_eval(st["e_a"], st["g_a"], st["h2"], st["x2"], st["mod4"], final_norm_g, st["uv"], S, final,
                        ntok=st["ta"])
        for r, g_p, t0 in st["parts"]:
            x2 = _peer_eval_dense(r, g_p, st["h2"], x2, st["mod4"], final_norm_g, S, final, t0)
        return x2

    xs = [x[g * bg:(g + 1) * bg].reshape(tg, D) for g in range(ngrp)]
    pending = []
    for l in range(depth):
        for g in range(ngrp):
            pending.append((l, g, mix_route(l, g, xs[g])))
            if len(pending) == ngrp:
                l0, g0, st = pending.pop(0)
                xs[g0] = evaluate(l0, st)
    for l0, g0, st in pending:
        xs[g0] = evaluate(l0, st)
    return jnp.concatenate(xs, axis=0).reshape(B, S, D)
```

```python
import functools
import math

import jax
import jax.numpy as jnp
from jax import lax
from jax.experimental import pallas as pl
from jax.experimental.pallas import tpu as pltpu
from jax.experimental.pallas import tpu_sc as plsc

F32 = jnp.float32
BF16 = jnp.bfloat16
HIGHEST = lax.Precision.HIGHEST

D_MODEL = 1024
HEAD_DIM = 64
DIFF_HEADS = 6
DIFF_QK_DIM = HEAD_DIM // 2
RWKV_HEADS = 6
FOX_HEADS = 4
DIFF_WIDTH = DIFF_HEADS * HEAD_DIM
RWKV_WIDTH = RWKV_HEADS * HEAD_DIM
FOX_WIDTH = FOX_HEADS * HEAD_DIM
DECAY_LORA = 64
AAA_LORA = 64
GATE_LORA = 128
DIFF_COLS = 3 * DIFF_WIDTH
RWKV_COLS = 3 * RWKV_WIDTH + DECAY_LORA + AAA_LORA + GATE_LORA
PEER_HEADS = 8
PEER_NKEYS = 128
PEER_TOPK = 16
PEER_QDIM = 256
PEER_HALF = PEER_QDIM // 2
RMS_EPS = 1e-6
RWKV_LN_EPS = 64e-5

LANES = 128
SUBLANES = 8
RW_PAD_COLS = 3 * RWKV_WIDTH + 3 * LANES
VMEM_LIMIT = 56 * 1024 * 1024

RW_CHUNK = 64
RW_CHUNKS_PER_ITER = 4


def _params(sem, vmem=VMEM_LIMIT):
    return pltpu.CompilerParams(dimension_semantics=sem, vmem_limit_bytes=vmem)


def _nt(a, b, precision=None):
    return lax.dot_general(a, b, (((1,), (1,)), ((), ())), preferred_element_type=F32, precision=precision)


def _mm(a, b, precision=None):
    return jnp.dot(a, b, preferred_element_type=F32, precision=precision)


def _ada_kernel(c_ref, w_ref, b_ref, o_ref):
    c = c_ref[...]
    ca = c * jax.nn.sigmoid(c)
    o_ref[...] = _mm(ca, w_ref[...], HIGHEST) + b_ref[...]


def _ada_mod(c, ada_w, ada_b):
    L, D, N = ada_w.shape
    B = c.shape[0]
    tn = 1536
    return pl.pallas_call(
        _ada_kernel,
        out_shape=jax.ShapeDtypeStruct((L, B, N), F32),
        grid=(L, N // tn),
        in_specs=[
            pl.BlockSpec((B, D), lambda l, j: (0, 0)),
            pl.BlockSpec((None, D, tn), lambda l, j: (l, 0, j)),
            pl.BlockSpec((None, 1, tn), lambda l, j: (l, 0, j)),
        ],
        out_specs=pl.BlockSpec((None, B, tn), lambda l, j: (l, 0, j)),
        compiler_params=_params(("parallel", "parallel")),
        name="ada_mod",
    )(c, ada_w, ada_b.reshape(L, 1, N))


ATT_T = 512
QK_DIFF = 2 * DIFF_WIDTH
QK_FOX = 2 * FOX_WIDTH
VT_ROWS = DIFF_WIDTH + FOX_WIDTH
IN_PAD_COLS = QK_DIFF + RW_PAD_COLS + QK_FOX + LANES


def _inproj_kernel(x_ref, g_ref, sh_ref, sc_ref, w_ref, wvt_ref, d_ref, r_ref, f_ref, fl_ref, vtd_ref, vtf_ref, *, ta):
    x = x_ref[...]
    ms = jnp.mean(x * x, axis=-1, keepdims=True)
    y = x * lax.rsqrt(ms + RMS_EPS) * g_ref[...]
    h = (y * (1.0 + sc_ref[...]) + sh_ref[...]).astype(BF16)
    o1 = QK_DIFF
    o2 = o1 + RW_PAD_COLS
    o3 = o2 + QK_FOX
    d_ref[...] = _mm(h, w_ref[:, 0:o1]).astype(BF16)
    r_ref[...] = _mm(h, w_ref[:, o1:o2])
    f_ref[...] = _mm(h, w_ref[:, o2:o3]).astype(BF16)
    fl_ref[...] = _mm(h, w_ref[:, o3:o3 + LANES])
    vt = _nt(wvt_ref[...], h).astype(BF16)
    for s in range(x.shape[0] // ta):
        vtd_ref[s] = vt[0:DIFF_WIDTH, ta * s:ta * (s + 1)]
        vtf_ref[s] = vt[DIFF_WIDTH:, ta * s:ta * (s + 1)]


def _in_proj(x2, g, mod4, w_pad, w_vt, S):
    T, D = x2.shape
    tm = 512 if S % 512 == 0 else S
    ta = min(ATT_T, S)
    nb = S // tm
    row = lambda i: (i, 0)
    return pl.pallas_call(
        functools.partial(_inproj_kernel, ta=ta),
        out_shape=(
            jax.ShapeDtypeStruct((T, QK_DIFF), BF16),
            jax.ShapeDtypeStruct((T, RW_PAD_COLS), F32),
            jax.ShapeDtypeStruct((T, QK_FOX), BF16),
            jax.ShapeDtypeStruct((T, LANES), F32),
            jax.ShapeDtypeStruct((T // ta, DIFF_WIDTH, ta), BF16),
            jax.ShapeDtypeStruct((T // ta, FOX_WIDTH, ta), BF16),
        ),
        grid=(T // tm,),
        in_specs=[
            pl.BlockSpec((tm, D), row),
            pl.BlockSpec((1, D), lambda i: (0, 0)),
            pl.BlockSpec((None, None, 1, D), lambda i: (i // nb, 0, 0, 0)),
            pl.BlockSpec((None, None, 1, D), lambda i: (i // nb, 1, 0, 0)),
            pl.BlockSpec((D, IN_PAD_COLS), lambda i: (0, 0)),
            pl.BlockSpec((VT_ROWS, D), lambda i: (0, 0)),
        ],
        out_specs=(
            pl.BlockSpec((tm, QK_DIFF), row),
            pl.BlockSpec((tm, RW_PAD_COLS), row),
            pl.BlockSpec((tm, QK_FOX), row),
            pl.BlockSpec((tm, LANES), row),
            pl.BlockSpec((tm // ta, DIFF_WIDTH, ta), lambda i: (i, 0, 0)),
            pl.BlockSpec((tm // ta, FOX_WIDTH, ta), lambda i: (i, 0, 0)),
        ),
        compiler_params=_params(("parallel",)),
        name="in_proj",
    )(x2, g.reshape(1, D), mod4, mod4, w_pad, w_vt)


def _pad_w_in(w_in):
    D = w_in.shape[0]
    W = RWKV_WIDTH
    o = DIFF_COLS
    z64 = jnp.zeros((D, LANES - DECAY_LORA), w_in.dtype)
    rw = w_in[:, o:o + RWKV_COLS]
    fx = w_in[:, o + RWKV_COLS:]
    zf = jnp.zeros((D, LANES - FOX_HEADS), w_in.dtype)
    w_pad = jnp.concatenate([
        w_in[:, :QK_DIFF],
        rw[:, :3 * W], rw[:, 3 * W:3 * W + DECAY_LORA], z64,
        rw[:, 3 * W + DECAY_LORA:3 * W + DECAY_LORA + AAA_LORA], z64,
        rw[:, 3 * W + DECAY_LORA + AAA_LORA:],
        fx[:, :QK_FOX], fx[:, 3 * FOX_WIDTH:], zf,
    ], axis=1).astype(BF16)
    w_vt = jnp.concatenate([w_in[:, QK_DIFF:o], fx[:, QK_FOX:3 * FOX_WIDTH]], axis=1).T.astype(BF16)
    return w_pad, w_vt


LOG2E = math.log2(math.e)


ACC_ROWS = LANES + 16


def _scaled_q(q, c):
    return (q.astype(F32) * c).astype(BF16)


def _with_ones(vt):
    return jnp.concatenate([vt, jnp.ones((ACC_ROWS - LANES, vt.shape[1]), vt.dtype)], axis=0)


def _flash_step(s2s, vt1, m_ref, acc_ref):
    n = len(s2s)
    m_old = [m_ref[x] for x in range(n)]
    m_new = [jnp.maximum(m_old[x], jnp.max(s2s[x], axis=0, keepdims=True)) for x in range(n)]
    alpha = [jnp.exp2(m_old[x] - m_new[x]) for x in range(n)]
    p = [jnp.exp2(s2s[x] - m_new[x]).astype(BF16) for x in range(n)]
    pv = [_mm(vt1, p[x]) for x in range(n)]
    for x in range(n):
        acc_ref[x] = alpha[x] * acc_ref[x] + pv[x]
        m_ref[x] = m_new[x]


def _diff_attn_kernel(lam_ref, g_ref, q_ref, k_ref, vt_ref, o_ref, m_ref, acc_ref, *, tq, lam_init):
    i = pl.program_id(2)
    lane = lax.broadcasted_iota(jnp.int32, (1, LANES), 1)
    q = _scaled_q(q_ref[...], (DIFF_QK_DIM ** -0.5) * LOG2E)
    zero = jnp.zeros_like(q)
    qm = [jnp.where((lane >= DIFF_QK_DIM * x) & (lane < DIFF_QK_DIM * (x + 1)), q, zero) for x in range(4)]
    m_ref[...] = jnp.full(m_ref.shape, -jnp.inf, F32)
    acc_ref[...] = jnp.zeros(acc_ref.shape, F32)

    def step(j, diag):
        k = k_ref[pl.ds(pl.multiple_of(j * tq, tq), tq), :]
        vt1 = _with_ones(vt_ref[j])
        if diag:
            keep = (lax.broadcasted_iota(jnp.int32, (tq, tq), 1) >= lax.broadcasted_iota(jnp.int32, (tq, tq), 0))
        s2s = [_nt(k, qm[x]) for x in range(4)]
        if diag:
            s2s = [jnp.where(keep, s2, -jnp.inf) for s2 in s2s]
        _flash_step(s2s, vt1, m_ref, acc_ref)

    def body(j, carry):
        step(j, False)
        return carry

    lax.fori_loop(0, i, body, 0)
    step(i, True)

    lp = lam_ref[...]
    lam = (jnp.exp(jnp.sum(lp[0:1] * lp[1:2], axis=-1, keepdims=True))
           - jnp.exp(jnp.sum(lp[2:3] * lp[3:4], axis=-1, keepdims=True)) + lam_init)
    sm = [acc_ref[x, 0:LANES, :] / acc_ref[x, LANES:LANES + 1, :] for x in range(4)]
    outs = [sm[2 * hh] - lam * sm[2 * hh + 1] for hh in range(2)]
    row = lax.broadcasted_iota(jnp.int32, (LANES, 1), 0)
    o = jnp.where(row < HEAD_DIM, outs[0], outs[1])
    sq = o * o
    ms = jnp.where(row < HEAD_DIM, jnp.sum(sq[0:HEAD_DIM], axis=0, keepdims=True),
                   jnp.sum(sq[HEAD_DIM:], axis=0, keepdims=True)) * (1.0 / HEAD_DIM)
    y = o * lax.rsqrt(ms + RMS_EPS) * g_ref[...] * (1.0 - lam_init)
    o_ref[...] = y.T.astype(o_ref.dtype)


def _diff_attention(dqk, vtd, lam_params, subln_g, layer_idx, B, S):
    T = B * S
    tq = min(ATT_T, S)
    nq = S // tq
    npair = DIFF_HEADS // 2
    lam_init = 0.8 - 0.6 * math.exp(-0.3 * layer_idx)
    g2 = jnp.concatenate([subln_g, subln_g]).reshape(LANES, 1).astype(F32)
    return pl.pallas_call(
        functools.partial(_diff_attn_kernel, tq=tq, lam_init=lam_init),
        out_shape=jax.ShapeDtypeStruct((T, DIFF_WIDTH), BF16),
        grid=(B, npair, nq),
        in_specs=[
            pl.BlockSpec((4, DIFF_QK_DIM), lambda b, p, i: (0, 0)),
            pl.BlockSpec((LANES, 1), lambda b, p, i: (0, 0)),
            pl.BlockSpec((tq, LANES), lambda b, p, i: (b * nq + i, p)),
            pl.BlockSpec((S, LANES), lambda b, p, i: (b, npair + p)),
            pl.BlockSpec((nq, LANES, tq), lambda b, p, i: (b, p, 0)),
        ],
        out_specs=pl.BlockSpec((tq, LANES), lambda b, p, i: (b * nq + i, p)),
        scratch_shapes=[
            pltpu.VMEM((4, 1, tq), F32),
            pltpu.VMEM((4, ACC_ROWS, tq), F32),
        ],
        compiler_params=_params(("parallel", "parallel", "arbitrary")),
        name="diff_attn",
    )(lam_params, g2, dqk, dqk, vtd)


def _fox_cum_kernel(f_ref, b_ref, rep_ref, row_ref, *, S, tc):
    rr = lax.broadcasted_iota(jnp.int32, (tc, tc), 0)
    cc = lax.broadcasted_iota(jnp.int32, (tc, tc), 1)
    tri = (rr >= cc).astype(F32)
    sel_r = lax.broadcasted_iota(jnp.int32, (LANES, LANES), 0)
    carry = jnp.zeros((1, LANES), F32)
    for c in range(S // tc):
        z = f_ref[c * tc:(c + 1) * tc, :] + b_ref[...]
        logf = -(jnp.maximum(-z, 0.0) + jnp.log(1.0 + jnp.exp(-jnp.abs(z))))
        cum = _mm(tri, logf, HIGHEST) + carry
        carry = cum[tc - 1:tc, :]
        row_ref[:, c * tc:(c + 1) * tc] = cum.T[0:8, :]
        for h in range(FOX_HEADS):
            rep_ref[h, c * tc:(c + 1) * tc, :] = _mm(cum, (sel_r == h).astype(F32), HIGHEST)


def _fox_cum(fl, b_f, B, S):
    tc = 256 if S % 256 == 0 else S
    bpad = jnp.zeros((1, LANES), F32).at[0, :FOX_HEADS].set(b_f.astype(F32))
    return pl.pallas_call(
        functools.partial(_fox_cum_kernel, S=S, tc=tc),
        out_shape=(jax.ShapeDtypeStruct((B, FOX_HEADS, S, LANES), F32), jax.ShapeDtypeStruct((B, 8, S), F32)),
        grid=(B,),
        in_specs=[pl.BlockSpec((S, LANES), lambda b: (b, 0)), pl.BlockSpec((1, LANES), lambda b: (0, 0))],
        out_specs=(pl.BlockSpec((None, FOX_HEADS, S, LANES), lambda b: (b, 0, 0, 0)),
                   pl.BlockSpec((None, 8, S), lambda b: (b, 0, 0))),
        compiler_params=_params(("parallel",)),
        name="fox_cum",
    )(fl, bpad)


def _fox_attn_kernel(q_ref, k_ref, vt_ref, c0_ref, c1_ref, cr_ref, o_ref, m_ref, acc_ref, *, tq):
    p_id = pl.program_id(1)
    i = pl.program_id(2)
    lane = lax.broadcasted_iota(jnp.int32, (1, LANES), 1)
    q = _scaled_q(q_ref[...], (HEAD_DIM ** -0.5) * LOG2E)
    zero = jnp.zeros_like(q)
    qm = [jnp.where((lane >= HEAD_DIM * x) & (lane < HEAD_DIM * (x + 1)), q, zero) for x in range(2)]
    ck_refs = (c0_ref, c1_ref)
    cq = [cr_ref[2 * p_id + x, pl.ds(i, 1), :] for x in range(2)]
    m_ref[...] = jnp.full(m_ref.shape, -jnp.inf, F32)
    acc_ref[...] = jnp.zeros(acc_ref.shape, F32)

    def step(j, diag):
        off = pl.multiple_of(j * tq, tq)
        k = k_ref[pl.ds(off, tq), :]
        vt1 = _with_ones(vt_ref[j])
        if diag:
            keep = (lax.broadcasted_iota(jnp.int32, (tq, tq), 1) >= lax.broadcasted_iota(jnp.int32, (tq, tq), 0))
        s2s = []
        for x in range(2):
            ck = ck_refs[x][pl.ds(off, tq), :]
            bias = (cq[x] - jnp.concatenate([ck] * (tq // LANES), axis=1)) * LOG2E
            s2s.append(_nt(k, qm[x]) + bias)
        if diag:
            s2s = [jnp.where(keep, s2, -jnp.inf) for s2 in s2s]
        _flash_step(s2s, vt1, m_ref, acc_ref)

    def body(j, carry):
        step(j, False)
        return carry

    lax.fori_loop(0, i, body, 0)
    step(i, True)
    row = lax.broadcasted_iota(jnp.int32, (LANES, 1), 0)
    sm = [acc_ref[x, 0:LANES, :] / acc_ref[x, LANES:LANES + 1, :] for x in range(2)]
    o = jnp.where(row < HEAD_DIM, sm[0], sm[1])
    o_ref[...] = o.T.astype(o_ref.dtype)


def _fox_attention(fqk, vtf, crep, crow, B, S):
    T = B * S
    tq = min(ATT_T, S)
    nq = S // tq
    npair = FOX_HEADS // 2
    crow4 = crow.reshape(B, 8, nq, tq)
    rep = lambda x: pl.BlockSpec((None, None, S, LANES), lambda b, p, i: (b, 2 * p + x, 0, 0))
    return pl.pallas_call(
        functools.partial(_fox_attn_kernel, tq=tq),
        out_shape=jax.ShapeDtypeStruct((T, FOX_WIDTH), BF16),
        grid=(B, npair, nq),
        in_specs=[
            pl.BlockSpec((tq, LANES), lambda b, p, i: (b * nq + i, p)),
            pl.BlockSpec((S, LANES), lambda b, p, i: (b, npair + p)),
            pl.BlockSpec((nq, LANES, tq), lambda b, p, i: (b, p, 0)),
            rep(0), rep(1),
            pl.BlockSpec((None, 8, nq, tq), lambda b, p, i: (b, 0, 0, 0)),
        ],
        out_specs=pl.BlockSpec((tq, LANES), lambda b, p, i: (b * nq + i, p)),
        scratch_shapes=[
            pltpu.VMEM((2, 1, tq), F32),
            pltpu.VMEM((2, ACC_ROWS, tq), F32),
        ],
        compiler_params=_params(("parallel", "parallel", "arbitrary")),
        name="fox_attn",
    )(fqk, fqk, vtf, crep, crep, crow4)


def _split3(x):
    hi = x.astype(BF16)
    r1 = x - hi.astype(F32)
    mid = r1.astype(BF16)
    lo = (r1 - mid.astype(F32)).astype(BF16)
    return hi, mid, lo


def _seg_sum(x, seg, npair):
    parts = _split3(x)
    return jnp.concatenate(
        [sum(_mm(t[:, LANES * p:LANES * (p + 1)], seg) for t in parts) for p in range(npair)], axis=1)


def _rwkv_kernel(x_ref, mu_ref, w0_ref, w2_ref, a0_ref, a2_ref, g2_ref, kk_ref, ka_ref, rk_ref, lng_ref, lnb_ref,
                 o_ref, carry_ref, st_ref, at_ref, rt_ref, bt_ref, kt_ref, v_ref, wc_ref, y_ref, g_ref, bon_ref,
                 *, tt, prec):
    i = pl.program_id(1)
    W = RWKV_WIDTH
    C = RW_CHUNK
    CPI = math.gcd(RW_CHUNKS_PER_ITER, tt // C)
    npair = RWKV_HEADS // 2

    @pl.when(i == 0)
    def _():
        carry_ref[...] = jnp.zeros(carry_ref.shape, F32)
        st_ref[...] = jnp.zeros(st_ref.shape, F32)

    x = x_ref[...]
    rows = lax.broadcasted_iota(jnp.int32, (tt, 1), 0)
    prev = jnp.where(rows == 0, carry_ref[...], pltpu.roll(x, 1, axis=0))
    carry_ref[...] = x[tt - 1:tt, :]
    xs = x + (prev - x) * mu_ref[...]
    r = xs[:, 0:W]
    k = xs[:, W:2 * W]
    v = xs[:, 2 * W:3 * W]
    xw = xs[:, 3 * W:3 * W + LANES]
    xa = xs[:, 3 * W + LANES:3 * W + 2 * LANES]
    xg = xs[:, 3 * W + 2 * LANES:]
    wl = w0_ref[...] + _mm(jnp.tanh(xw), w2_ref[...], HIGHEST)
    w = -(jnp.maximum(-wl, 0.0) + jnp.log(1.0 + jnp.exp(-jnp.abs(wl)))) - 0.5
    logdec = -jnp.exp(w)
    a = jax.nn.sigmoid(a0_ref[...] + _mm(xa, a2_ref[...], HIGHEST))
    g_ref[...] = _mm(jax.nn.sigmoid(xg), g2_ref[...], HIGHEST)

    r_i = lax.broadcasted_iota(jnp.int32, (LANES, LANES), 0)
    c_i = lax.broadcasted_iota(jnp.int32, (LANES, LANES), 1)
    seg = ((r_i // HEAD_DIM) == (c_i // HEAD_DIM)).astype(BF16)
    kkv = k * kk_ref[...]
    kkn = kkv / jnp.maximum(jnp.sqrt(_seg_sum(kkv * kkv, seg, npair)), 1e-12)
    k2 = k * (1.0 + (a - 1.0) * ka_ref[...])
    bon_ref[...] = _seg_sum(r * k2 * rk_ref[...], seg, npair) * v

    rt_i = lax.broadcasted_iota(jnp.int32, (tt, tt), 0)
    ct_i = lax.broadcasted_iota(jnp.int32, (tt, tt), 1)
    tri = (((rt_i // C) == (ct_i // C)) & (rt_i >= ct_i)).astype(BF16)
    cum = sum(_mm(tri, t) for t in _split3(logdec))
    winv = jnp.exp(-cum)
    wcum = jnp.exp(cum)
    at_ref[...] = -kkn * jnp.exp(cum - logdec)
    bt_ref[...] = kkn * a * winv
    kt_ref[...] = k2 * winv
    rt_ref[...] = r * wcum
    v_ref[...] = v
    wc_ref[...] = wcum

    lane = lax.broadcasted_iota(jnp.int32, (1, LANES), 1)
    lo = lane < HEAD_DIM
    tpos = r_i % C
    ipos = c_i % C
    strict = tpos > ipos
    incl = tpos >= ipos
    eye = r_i == c_i

    def stack2(m):
        return jnp.concatenate([jnp.where(lo, m, 0.0), jnp.where(lo, 0.0, m)], axis=0)

    def cast(m):
        return m if prec is not None else m.astype(BF16)

    def mm(p, q):
        return _mm(cast(p), cast(q), prec)

    def nt(p, q):
        return _nt(cast(p), cast(q), prec)

    def chunks(cc, carry):
        units = [(ci, p) for ci in range(CPI) for p in range(npair)]
        r0 = [pl.multiple_of((cc * CPI + ci) * C, C) for ci in range(CPI)]
        ld = lambda ref, ci, p: stack2(ref[pl.ds(r0[ci], C), LANES * p:LANES * (p + 1)])
        a2 = {u: ld(at_ref, *u) for u in units}
        r2 = {u: ld(rt_ref, *u) for u in units}
        b2 = {u: ld(bt_ref, *u) for u in units}
        k2s = {u: ld(kt_ref, *u) for u in units}
        v2 = {u: ld(v_ref, *u) for u in units}
        wl = {(ci, p): wc_ref[pl.ds(pl.multiple_of(r0[ci] + C - 8, 8), 8), LANES * p:LANES * (p + 1)][7:8, :]
              for (ci, p) in units}
        ar = {u: jnp.concatenate([a2[u], r2[u]], axis=0) for u in units}
        mb = {u: nt(ar[u], b2[u]) for u in units}
        mk = {u: nt(ar[u], k2s[u]) for u in units}
        lab = {u: jnp.where(strict, mb[u][0:LANES], 0.0) for u in units}
        mrb = {u: jnp.where(incl, mb[u][LANES:], 0.0) for u in units}
        lak = {u: jnp.where(strict, mk[u][0:LANES], 0.0) for u in units}
        mrk = {u: jnp.where(incl, mk[u][LANES:], 0.0) for u in units}
        xx = {u: jnp.concatenate([a2[u], mm(lak[u], v2[u])], axis=1) for u in units}
        lp = lab
        for it in range(6):
            xx = {u: xx[u] + mm(lp[u], xx[u]) for u in units}
            if it < 5:
                lp = {u: mm(lp[u], lp[u]) for u in units}
        mq = {u: mm(mrb[u], xx[u]) for u in units}
        mv = {u: mm(mrk[u], v2[u]) for u in units}
        bx = {u: mm((b2[u] * wl[u]).T, xx[u]) for u in units}
        kv = {u: mm((k2s[u] * wl[u]).T, v2[u]) for u in units}
        st = [st_ref[p] for p in range(npair)]
        for (ci, p) in units:
            u = (ci, p)
            q1 = r2[u] + mq[u][:, 0:LANES]
            q2 = mq[u][:, LANES:] + mv[u]
            gmat = jnp.where(eye, jnp.broadcast_to(wl[u], (LANES, LANES)), 0.0) + bx[u][:, 0:LANES]
            hmat = bx[u][:, LANES:] + kv[u]
            gs = mm(jnp.concatenate([gmat, q1], axis=0), st[p])
            st[p] = gs[0:LANES] + hmat
            yy = gs[LANES:] + q2
            y_ref[pl.ds(r0[ci], C), LANES * p:LANES * (p + 1)] = yy[0:C] + yy[C:]
        for p in range(npair):
            st_ref[p] = st[p]
        return carry

    lax.fori_loop(0, tt // (C * CPI), chunks, 0)

    y = y_ref[...]
    inv = 1.0 / HEAD_DIM
    mean = _seg_sum(y, seg, npair) * inv
    yc = y - mean
    var = _seg_sum(yc * yc, seg, npair) * inv
    yn = yc * lax.rsqrt(var + RWKV_LN_EPS) * lng_ref[...] + lnb_ref[...]
    o_ref[...] = ((yn + bon_ref[...]) * g_ref[...]).astype(o_ref.dtype)


def _rwkv(rcols, mu, w0, w2, a0, a2, g2, k_k, k_a, r_k, ln_g, ln_b, B, S, prec=HIGHEST):
    T = B * S
    W = RWKV_WIDTH
    tt = 512 if S % 512 == 0 else S
    nt_ = S // tt
    npair = RWKV_HEADS // 2
    pad = LANES - DECAY_LORA
    mu_p = jnp.concatenate([mu[:3 * W], mu[3 * W:3 * W + DECAY_LORA], jnp.zeros((pad,), F32),
                            mu[3 * W + DECAY_LORA:3 * W + DECAY_LORA + AAA_LORA], jnp.zeros((pad,), F32),
                            mu[3 * W + DECAY_LORA + AAA_LORA:]]).reshape(1, RW_PAD_COLS)
    w2p = jnp.concatenate([w2, jnp.zeros((pad, W), F32)], axis=0)
    a2p = jnp.concatenate([a2, jnp.zeros((pad, W), F32)], axis=0)
    vec = lambda t: t.reshape(1, W).astype(F32)
    full = lambda shape: pl.BlockSpec(shape, lambda b, i: (0,) * len(shape))
    sc = lambda: pltpu.VMEM((tt, W), F32)
    return pl.pallas_call(
        functools.partial(_rwkv_kernel, tt=tt, prec=prec),
        out_shape=jax.ShapeDtypeStruct((T, W), BF16),
        grid=(B, nt_),
        in_specs=[
            pl.BlockSpec((tt, RW_PAD_COLS), lambda b, i: (b * nt_ + i, 0)),
            full((1, RW_PAD_COLS)), full((1, W)), full((LANES, W)), full((1, W)), full((LANES, W)),
            full((GATE_LORA, W)), full((1, W)), full((1, W)), full((1, W)), full((1, W)), full((1, W)),
        ],
        out_specs=pl.BlockSpec((tt, W), lambda b, i: (b * nt_ + i, 0)),
        scratch_shapes=[
            pltpu.VMEM((1, RW_PAD_COLS), F32),
            pltpu.VMEM((npair, LANES, LANES), F32),
            sc(), sc(), sc(), sc(), sc(), sc(), sc(), sc(), sc(),
        ],
        compiler_params=_params(("parallel", "arbitrary")),
        name="rwkv7",
    )(rcols, mu_p, vec(w0), w2p, vec(a0), a2p, g2, vec(k_k), vec(k_a), vec(r_k), vec(ln_g), vec(ln_b))


def _outproj_kernel(ya_ref, yb_ref, yc_ref, x_ref, w_ref, gm_ref, g_ref, sh_ref, sc_ref, xo_ref, ho_ref):
    o1 = DIFF_WIDTH
    o2 = o1 + RWKV_WIDTH
    mix = (_mm(ya_ref[...], w_ref[0:o1, :]) + _mm(yb_ref[...], w_ref[o1:o2, :])
           + _mm(yc_ref[...], w_ref[o2:, :]))
    xn = x_ref[...] + gm_ref[...] * mix
    xo_ref[...] = xn
    ms = jnp.mean(xn * xn, axis=-1, keepdims=True)
    y = xn * lax.rsqrt(ms + RMS_EPS) * g_ref[...]
    ho_ref[...] = y * (1.0 + sc_ref[...]) + sh_ref[...]


def _out_proj(ya, yb, yc, x2, w_out, g, mod4, S):
    T, D = x2.shape
    tm = 512 if S % 512 == 0 else S
    nb = S // tm
    row = lambda i: (i, 0)
    modspec = lambda which: pl.BlockSpec((None, None, 1, D), lambda i: (i // nb, which, 0, 0))
    return pl.pallas_call(
        _outproj_kernel,
        out_shape=(jax.ShapeDtypeStruct((T, D), F32), jax.ShapeDtypeStruct((T, D), F32)),
        grid=(T // tm,),
        in_specs=[
            pl.BlockSpec((tm, DIFF_WIDTH), row), pl.BlockSpec((tm, RWKV_WIDTH), row), pl.BlockSpec((tm, FOX_WIDTH), row),
            pl.BlockSpec((tm, D), row),
            pl.BlockSpec((D, D), lambda i: (0, 0)),
            modspec(2),
            pl.BlockSpec((1, D), lambda i: (0, 0)),
            modspec(3), modspec(4),
        ],
        out_specs=(pl.BlockSpec((tm, D), row), pl.BlockSpec((tm, D), row)),
        compiler_params=_params(("parallel",)),
        name="out_proj",
    )(ya, yb, yc, x2, w_out.astype(BF16), mod4, g.reshape(1, D), mod4, mod4)


def _top16(s, iota_f, n):
    vals, poss = [], []
    for _ in range(PEER_TOPK):
        m = jnp.max(s, axis=0, keepdims=True)
        pos = jnp.min(jnp.where(s == m, iota_f, float(n)), axis=0, keepdims=True)
        vals.append(m)
        poss.append(pos)
        s = jnp.where(iota_f == pos, -jnp.inf, s)
    return jnp.concatenate(vals, axis=0), jnp.concatenate(poss, axis=0)


ROUTE_UNROLL = 8
PEER_NCAND = 56


def _peer_cand_tables():
    K = PEER_TOPK
    pairs = [(a, b) for a in range(K) for b in range(K) if (a + 1) * (b + 1) <= K]
    n = PEER_NCAND
    p0 = [[0.0] * K for _ in range(n)]
    p1 = [[0.0] * K for _ in range(n)]
    pad = [0.0] * n
    pos = [float(K * K + r) for r in range(n)]
    for r, (a, b) in enumerate(pairs):
        p0[r][a] = 1.0
        p1[r][b] = 1.0
        pos[r] = float(a * K + b)
    for r in range(len(pairs), n):
        pad[r] = -float("inf")
    col = lambda v: jnp.broadcast_to(jnp.asarray(v, F32)[:, None], (n, LANES))
    return jnp.asarray(p0, F32), jnp.asarray(p1, F32), col(pad), col(pos)


def _peer_route_kernel(h_ref, wq_ref, sk_ref, p0_ref, p1_ref, cpad_ref, cpos_ref, idx_ref, gate_ref, q_scr, e_scr,
                       g_scr):
    K = PEER_TOPK
    hb = h_ref[...].astype(BF16)
    q = _mm(hb, wq_ref[...])
    for hc in range(2 * PEER_HEADS):
        q_scr[hc] = q[:, LANES * hc:LANES * (hc + 1)].astype(BF16)
    iota_n = lax.broadcasted_iota(jnp.int32, (PEER_NKEYS, LANES), 0).astype(F32)
    cpos = cpos_ref[...]

    def one_head(h):
        sv0, si0 = _top16(_nt(sk_ref[2 * h], q_scr[2 * h]), iota_n, PEER_NKEYS)
        sv1, si1 = _top16(_nt(sk_ref[2 * h + 1], q_scr[2 * h + 1]), iota_n, PEER_NKEYS)
        cand = _mm(p0_ref[...], sv0, HIGHEST) + _mm(p1_ref[...], sv1, HIGHEST) + cpad_ref[...]
        cidx = _mm(p0_ref[...], si0) * float(PEER_NKEYS) + _mm(p1_ref[...], si1)
        fv, es = [], []
        for _ in range(K):
            m = jnp.max(cand, axis=0, keepdims=True)
            pos = jnp.min(jnp.where(cand == m, cpos, float(2 * K * K)), axis=0, keepdims=True)
            hit = cpos == pos
            fv.append(m)
            es.append(jnp.max(jnp.where(hit, cidx, -1.0), axis=0, keepdims=True))
            cand = jnp.where(hit, -jnp.inf, cand)
        fv = jnp.concatenate(fv, axis=0)
        ex = jnp.exp(fv - fv[0:1, :])
        g_scr[h] = ex / jnp.sum(ex, axis=0, keepdims=True)
        e_scr[h] = jnp.concatenate(es, axis=0)

    def heads(hh, carry):
        for j in range(ROUTE_UNROLL):
            one_head(hh * ROUTE_UNROLL + j)
        return carry

    lax.fori_loop(0, PEER_HEADS // ROUTE_UNROLL, heads, 0)
    e = e_scr[...].reshape(PEER_HEADS * K, LANES)
    idx_ref[...] = e.T.astype(jnp.int32)
    gate_ref[...] = g_scr[...].reshape(PEER_HEADS * K, LANES)


def _peer_route(h2, wq, subkeys, tok0=0, ntok=None):
    T, D = h2.shape
    T = T - tok0 if ntok is None else ntok
    tm = LANES
    blk0 = tok0 // tm
    nq = 2 * PEER_HEADS
    sk = subkeys.reshape(nq, PEER_NKEYS, PEER_HALF).astype(BF16)
    p0, p1, cpad, cpos = _peer_cand_tables()
    const = lambda shape: pl.BlockSpec(shape, lambda i: (0,) * len(shape))
    return pl.pallas_call(
        _peer_route_kernel,
        out_shape=(jax.ShapeDtypeStruct((T, PEER_HEADS * PEER_TOPK), jnp.int32),
                   jax.ShapeDtypeStruct((T // tm, PEER_HEADS * PEER_TOPK, tm), F32)),
        grid=(T // tm,),
        in_specs=[
            pl.BlockSpec((tm, D), lambda i: (blk0 + i, 0)),
            const((D, nq * PEER_HALF)),
            const((nq, PEER_NKEYS, PEER_HALF)),
            const((PEER_NCAND, PEER_TOPK)), const((PEER_NCAND, PEER_TOPK)),
            const((PEER_NCAND, LANES)), const((PEER_NCAND, LANES)),
        ],
        out_specs=(pl.BlockSpec((tm, PEER_HEADS * PEER_TOPK), lambda i: (i, 0)),
                   pl.BlockSpec((None, PEER_HEADS * PEER_TOPK, tm), lambda i: (i, 0, 0))),
        scratch_shapes=[
            pltpu.VMEM((nq, tm, PEER_HALF), BF16),
            pltpu.VMEM((PEER_HEADS, PEER_TOPK, tm), F32),
            pltpu.VMEM((PEER_HEADS, PEER_TOPK, tm), F32),
        ],
        compiler_params=_params(("parallel",)),
        name="peer_route",
    )(h2, wq.astype(BF16), sk, p0, p1, cpad, cpos)


PEER_G = 16
PEER_SLOTS = PEER_HEADS * PEER_TOPK


def _peer_eval_kernel(idx_ref, idxn_ref, gate_ref, h_ref, x_ref, gf_ref, fg_ref, uv_ref, o_ref, buf, sem, *, final):
    G = PEER_G
    R = G * PEER_SLOTS
    D = D_MODEL
    tiles = PEER_SLOTS // SUBLANES
    i = pl.program_id(0)
    n = pl.num_programs(0)

    def start(ids, off, s, t, u):
        pltpu.make_async_copy(uv_ref.at[ids[off + t * SUBLANES + u]], buf.at[s, t, pl.ds(u, 1), :],
                              sem.at[s]).start(priority=u % 2)

    def wait(s):
        pltpu.make_async_copy(buf.at[s], buf.at[s], sem.at[s]).wait()

    @pl.when(i == 0)
    def _():
        def body(t, carry):
            for u in range(SUBLANES):
                start(idx_ref, 0, 0, t, u)
            return carry
        lax.fori_loop(0, R // SUBLANES, body, 0)

    lane = lax.broadcasted_iota(jnp.int32, (1, LANES), 1)
    tbase = (i % (LANES // (2 * G))) * (2 * G)
    gate = gate_ref[...]
    outs = []
    for s in range(2):
        wait(s)
        nxt_ids, nxt_off = (idx_ref, R) if s == 0 else (idxn_ref, 0)
        for g in range(G):
            for t in range(tiles * g, tiles * (g + 1)):
                for u in range(SUBLANES):
                    start(nxt_ids, nxt_off, 1 - s, t, u)
            w_rows = buf[s, tiles * g:tiles * (g + 1)].reshape(PEER_SLOTS, D)
            u_rows = lax.bitcast_convert_type(w_rows & jnp.uint32(0xFFFF0000), F32)
            prod = u_rows * h_ref[G * s + g:G * s + g + 1, :]
            part = prod[:, 0:LANES]
            for c in range(1, D // LANES):
                part = part + prod[:, LANES * c:LANES * (c + 1)]
            act = jnp.sum(part, axis=1, keepdims=True)
            gcol = jnp.sum(jnp.where(lane == tbase + G * s + g, gate, 0.0), axis=1, keepdims=True)
            coef = gcol * (0.5 * act * (1.0 + lax.erf(act * (2.0 ** -0.5))))
            v_rows = lax.bitcast_convert_type(w_rows << 16, F32)
            outs.append(jnp.sum(v_rows * coef, axis=0, keepdims=True))
    xn = x_ref[...] + gf_ref[...] * jnp.concatenate(outs, axis=0)
    if final:
        ms = jnp.mean(xn * xn, axis=-1, keepdims=True)
        xn = xn * lax.rsqrt(ms + RMS_EPS) * fg_ref[...]
    o_ref[...] = xn

    @pl.when(i == n - 1)
    def _():
        wait(0)


def _peer_eval(eidx, gate_t, h2, x2, mod4, final_g, uv, S, final, ntok=None):
    T, D = x2.shape
    T = T if ntok is None else ntok
    G = PEER_G
    R = G * PEER_SLOTS
    n = T // (2 * G)
    return pl.pallas_call(
        functools.partial(_peer_eval_kernel, final=final),
        out_shape=jax.ShapeDtypeStruct(x2.shape, F32),
        input_output_aliases={4: 0},
        grid=(n,),
        in_specs=[
            pl.BlockSpec((2 * R,), lambda i: (i,), memory_space=pltpu.SMEM),
            pl.BlockSpec((R,), lambda i: (jnp.minimum(2 * i + 2, 2 * n - 2),), memory_space=pltpu.SMEM),
            pl.BlockSpec((None, PEER_SLOTS, LANES), lambda i: (i // (LANES // (2 * G)), 0, 0)),
            pl.BlockSpec((2 * G, D), lambda i: (i, 0)),
            pl.BlockSpec((2 * G, D), lambda i: (i, 0)),
            pl.BlockSpec((None, None, 1, D), lambda i: (i // (S // (2 * G)), 5, 0, 0)),
            pl.BlockSpec((1, D), lambda i: (0, 0)),
            pl.BlockSpec(memory_space=pl.ANY),
        ],
        out_specs=pl.BlockSpec((2 * G, D), lambda i: (i, 0)),
        scratch_shapes=[pltpu.VMEM((2, R // SUBLANES, SUBLANES, D), jnp.uint32), pltpu.SemaphoreType.DMA((2,))],
        compiler_params=_params(("arbitrary",)),
        name="peer_eval",
    )(eidx.reshape(-1), eidx.reshape(-1), gate_t, h2, x2, mod4, final_g.reshape(1, D), uv)


SC_WORKERS = 32
SC_WINDOW = 128
SC_ROWS = 32
PEER_GD = 16
SC_SHARE_NUM, SC_SHARE_DEN = 3, 4
SC_PARTS = 2
BATCH_GROUPS = 2


def _sc_gather(tab, idx):
    n = idx.shape[0]
    width = tab.shape[1]
    per = n // SC_WORKERS
    nsub = SC_WINDOW // SC_ROWS
    mesh = plsc.VectorSubcoreMesh(core_axis_name="core", subcore_axis_name="subcore")

    @pl.kernel(out_type=jax.ShapeDtypeStruct((n, width), tab.dtype), mesh=mesh,
               scratch_types=[pltpu.VMEM((SC_WINDOW,), jnp.int32), pltpu.VMEM((2, SC_ROWS, width), tab.dtype),
                              pltpu.SemaphoreType.DMA((2,)), pltpu.SemaphoreType.DMA((2,))])
    def gather(x_hbm, i_hbm, o_hbm, idx_v, rows_v, gsem, wsem):
        wid = lax.axis_index("core") * (SC_WORKERS // 2) + lax.axis_index("subcore")

        def start_gather(k):
            return pltpu.async_copy(x_hbm.at[idx_v.at[pl.ds(k * SC_ROWS, SC_ROWS)]], rows_v.at[k % 2], gsem.at[k % 2])

        @pl.loop(0, per // SC_WINDOW)
        def _(w):
            base = wid * per + w * SC_WINDOW
            pltpu.sync_copy(i_hbm.at[pl.ds(base, SC_WINDOW)], idx_v)
            gat = [start_gather(0)] + [None] * (nsub - 1)
            wrt = [None] * nsub
            for k in range(nsub):
                if k + 1 < nsub:
                    if k >= 1:
                        wrt[k - 1].wait()
                    gat[k + 1] = start_gather(k + 1)
                gat[k].wait()
                wrt[k] = pltpu.async_copy(rows_v.at[k % 2], o_hbm.at[pl.ds(base + k * SC_ROWS, SC_ROWS)],
                                          wsem.at[k % 2])
            for k in range(max(nsub - 2, 0), nsub):
                wrt[k].wait()

    return gather(tab, idx)


def _peer_dense_kernel(rows_ref, gate_ref, h_ref, x_ref, gf_ref, fg_ref, o_ref, *, final):
    G = PEER_GD
    D = D_MODEL
    i = pl.program_id(0)
    lane = lax.broadcasted_iota(jnp.int32, (1, LANES), 1)
    tbase = (i % (LANES // G)) * G
    gate = gate_ref[...]
    outs = []
    for g in range(G):
        w_rows = rows_ref[PEER_SLOTS * g:PEER_SLOTS * (g + 1), :]
        u_rows = lax.bitcast_convert_type(w_rows & jnp.uint32(0xFFFF0000), F32)
        prod = u_rows * h_ref[g:g + 1, :]
        part = prod[:, 0:LANES]
        for c in range(1, D // LANES):
            part = part + prod[:, LANES * c:LANES * (c + 1)]
        act = jnp.sum(part, axis=1, keepdims=True)
        gcol = jnp.sum(jnp.where(lane == tbase + g, gate, 0.0), axis=1, keepdims=True)
        coef = gcol * (0.5 * act * (1.0 + lax.erf(act * (2.0 ** -0.5))))
        v_rows = lax.bitcast_convert_type(w_rows << 16, F32)
        outs.append(jnp.sum(v_rows * coef, axis=0, keepdims=True))
    xn = x_ref[...] + gf_ref[...] * jnp.concatenate(outs, axis=0)
    if final:
        ms = jnp.mean(xn * xn, axis=-1, keepdims=True)
        xn = xn * lax.rsqrt(ms + RMS_EPS) * fg_ref[...]
    o_ref[...] = xn


def _peer_eval_dense(rows, gate_t, h2, x2, mod4, final_g, S, final, tok0):
    T, D = x2.shape
    G = PEER_GD
    tb = rows.shape[0] // PEER_SLOTS
    assert tok0 % LANES == 0 and tb % LANES == 0
    blk0 = tok0 // G
    return pl.pallas_call(
        functools.partial(_peer_dense_kernel, final=final),
        out_shape=jax.ShapeDtypeStruct(x2.shape, F32),
        input_output_aliases={3: 0},
        grid=(tb // G,),
        in_specs=[
            pl.BlockSpec((G * PEER_SLOTS, D), lambda i: (i, 0)),
            pl.BlockSpec((None, PEER_SLOTS, LANES), lambda i: (i // (LANES // G), 0, 0)),
            pl.BlockSpec((G, D), lambda i: (blk0 + i, 0)),
            pl.BlockSpec((G, D), lambda i: (blk0 + i, 0)),
            pl.BlockSpec((None, None, 1, D), lambda i: ((blk0 + i) // (S // G), 5, 0, 0)),
            pl.BlockSpec((1, D), lambda i: (0, 0)),
        ],
        out_specs=pl.BlockSpec((G, D), lambda i: (blk0 + i, 0)),
        compiler_params=_params(("parallel",)),
        name="peer_dense",
    )(rows, gate_t, h2, x2, mod4, final_g.reshape(1, D))


def _pack_uv(u, v):
    hi = lax.bitcast_convert_type(u.astype(BF16), jnp.uint16).astype(jnp.uint32) << 16
    lo = lax.bitcast_convert_type(v.astype(BF16), jnp.uint16).astype(jnp.uint32)
    return (hi | lo).reshape(u.shape[0], 1, u.shape[1])


def kernel(x, c, norm_mix_g, norm_ffn_g, final_norm_g, ada_w, ada_b, w_in, w_out, dif_lam, dif_subln_g, rw_mu, rw_w0,
           rw_w2, rw_a0, rw_a2, rw_g2, rw_kk, rw_ka, rw_rk, rw_ln_g, rw_ln_b, fox_bf, peer_wq, peer_subkeys, peer_u,
           peer_v):
    B, S, D = x.shape
    depth = ada_w.shape[0]
    mod = _ada_mod(c, ada_w, ada_b)
    ngrp = BATCH_GROUPS if B % BATCH_GROUPS == 0 else 1
    bg = B // ngrp
    tg = bg * S
    prep = [(_pad_w_in(w_in[l]), _pack_uv(peer_u[l], peer_v[l])) for l in range(depth)]

    def mix_route(l, g, x2):
        mod4 = mod[l, g * bg:(g + 1) * bg].reshape(bg, 6, 1, D)
        (w_pad, w_vt), uv = prep[l]
        dqk, rcols, fqk, fl, vtd, vtf = _in_proj(x2, norm_mix_g[l], mod4, w_pad, w_vt, S)
        ya = _diff_attention(dqk, vtd, dif_lam[l], dif_subln_g[l], l, bg, S)
        yb = _rwkv(rcols, rw_mu[l], rw_w0[l], rw_w2[l], rw_a0[l], rw_a2[l], rw_g2[l], rw_kk[l], rw_ka[l],
                   rw_rk[l].reshape(-1), rw_ln_g[l], rw_ln_b[l], bg, S, prec=None)
        crep, crow = _fox_cum(fl, fox_bf[l], bg, S)
        yc = _fox_attention(fqk, vtf, crep, crow, bg, S)
        x2, h2 = _out_proj(ya, yb, yc, x2, w_out[l], norm_ffn_g[l], mod4, S)
        tb = (tg * SC_SHARE_NUM // SC_SHARE_DEN) // (SC_WORKERS * SC_WINDOW) * (SC_WORKERS * SC_WINDOW)
        ta = tg - tb
        bounds = [ta + tb * j // SC_PARTS for j in range(SC_PARTS + 1)]
        parts = []
        for t0, t1 in zip(bounds[:-1], bounds[1:]):
            if t1 == t0:
                continue
            e_p, g_p = _peer_route(h2, peer_wq[l], peer_subkeys[l], t0, t1 - t0)
            parts.append((_sc_gather(uv.reshape(-1, D), e_p.reshape(-1)), g_p, t0))
        e_a, g_a = _peer_route(h2, peer_wq[l], peer_subkeys[l], 0, ta)
        return dict(x2=x2, h2=h2, mod4=mod4, uv=uv, ta=ta, e_a=e_a, g_a=g_a, parts=parts)

    def evaluate(l, st):
        final = l == depth - 1
        x2 = _peer_eval(st["e_a"], st["g_a"], st["h2"], st["x2"], st["mod4"], final_norm_g, st["uv"], S, final,
                        ntok=st["ta"])
        for r, g_p, t0 in st["parts"]:
            x2 = _peer_eval_dense(r, g_p, st["h2"], x2, st["mod4"], final_norm_g, S, final, t0)
        return x2

    xs = [x[g * bg:(g + 1) * bg].reshape(tg, D) for g in range(ngrp)]
    pending = []
    for l in range(depth):
        for g in range(ngrp):
            pending.append((l, g, mix_route(l, g, xs[g])))
            if len(pending) == ngrp:
                l0, g0, st = pending.pop(0)
                xs[g0] = evaluate(l0, st)
    for l0, g0, st in pending:
        xs[g0] = evaluate(l0, st)
    return jnp.concatenate(xs, axis=0).reshape(B, S, D)
```

```python
import functools
import math

import jax
import jax.numpy as jnp
from jax import lax
from jax.experimental import pallas as pl
from jax.experimental.pallas import tpu as pltpu
from jax.experimental.pallas import tpu_sc as plsc

F32 = jnp.float32
BF16 = jnp.bfloat16
HIGHEST = lax.Precision.HIGHEST

D_MODEL = 1024
HEAD_DIM = 64
DIFF_HEADS = 6
DIFF_QK_DIM = HEAD_DIM // 2
RWKV_HEADS = 6
FOX_HEADS = 4
DIFF_WIDTH = DIFF_HEADS * HEAD_DIM
RWKV_WIDTH = RWKV_HEADS * HEAD_DIM
FOX_WIDTH = FOX_HEADS * HEAD_DIM
DECAY_LORA = 64
AAA_LORA = 64
GATE_LORA = 128
DIFF_COLS = 3 * DIFF_WIDTH
RWKV_COLS = 3 * RWKV_WIDTH + DECAY_LORA + AAA_LORA + GATE_LORA
PEER_HEADS = 8
PEER_NKEYS = 128
PEER_TOPK = 16
PEER_QDIM = 256
PEER_HALF = PEER_QDIM // 2
RMS_EPS = 1e-6
RWKV_LN_EPS = 64e-5

LANES = 128
SUBLANES = 8
RW_PAD_COLS = 3 * RWKV_WIDTH + 3 * LANES
VMEM_LIMIT = 56 * 1024 * 1024

RW_CHUNK = 64
RW_CHUNKS_PER_ITER = 4


def _params(sem, vmem=VMEM_LIMIT):
    return pltpu.CompilerParams(dimension_semantics=sem, vmem_limit_bytes=vmem)


def _nt(a, b, precision=None):
    return lax.dot_general(a, b, (((1,), (1,)), ((), ())), preferred_element_type=F32, precision=precision)


def _mm(a, b, precision=None):
    return jnp.dot(a, b, preferred_element_type=F32, precision=precision)


def _ada_kernel(c_ref, w_ref, b_ref, o_ref):
    c = c_ref[...]
    ca = c * jax.nn.sigmoid(c)
    o_ref[...] = _mm(ca, w_ref[...], HIGHEST) + b_ref[...]


def _ada_mod(c, ada_w, ada_b):
    L, D, N = ada_w.shape
    B = c.shape[0]
    tn = 1536
    return pl.pallas_call(
        _ada_kernel,
        out_shape=jax.ShapeDtypeStruct((L, B, N), F32),
        grid=(L, N // tn),
        in_specs=[
            pl.BlockSpec((B, D), lambda l, j: (0, 0)),
            pl.BlockSpec((None, D, tn), lambda l, j: (l, 0, j)),
            pl.BlockSpec((None, 1, tn), lambda l, j: (l, 0, j)),
        ],
        out_specs=pl.BlockSpec((None, B, tn), lambda l, j: (l, 0, j)),
        compiler_params=_params(("parallel", "parallel")),
        name="ada_mod",
    )(c, ada_w, ada_b.reshape(L, 1, N))


ATT_T = 512
QK_DIFF = 2 * DIFF_WIDTH
QK_FOX = 2 * FOX_WIDTH
VT_ROWS = DIFF_WIDTH + FOX_WIDTH
IN_PAD_COLS = QK_DIFF + RW_PAD_COLS + QK_FOX + LANES


def _inproj_kernel(x_ref, g_ref, sh_ref, sc_ref, w_ref, wvt_ref, d_ref, r_ref, f_ref, fl_ref, vtd_ref, vtf_ref, *, ta):
    x = x_ref[...]
    ms = jnp.mean(x * x, axis=-1, keepdims=True)
    y = x * lax.rsqrt(ms + RMS_EPS) * g_ref[...]
    h = (y * (1.0 + sc_ref[...]) + sh_ref[...]).astype(BF16)
    o1 = QK_DIFF
    o2 = o1 + RW_PAD_COLS
    o3 = o2 + QK_FOX
    d_ref[...] = _mm(h, w_ref[:, 0:o1]).astype(BF16)
    r_ref[...] = _mm(h, w_ref[:, o1:o2])
    f_ref[...] = _mm(h, w_ref[:, o2:o3]).astype(BF16)
    fl_ref[...] = _mm(h, w_ref[:, o3:o3 + LANES])
    vt = _nt(wvt_ref[...], h).astype(BF16)
    for s in range(x.shape[0] // ta):
        vtd_ref[s] = vt[0:DIFF_WIDTH, ta * s:ta * (s + 1)]
        vtf_ref[s] = vt[DIFF_WIDTH:, ta * s:ta * (s + 1)]


def _in_proj(x2, g, mod4, w_pad, w_vt, S):
    T, D = x2.shape
    tm = 512 if S % 512 == 0 else S
    ta = min(ATT_T, S)
    nb = S // tm
    row = lambda i: (i, 0)
    return pl.pallas_call(
        functools.partial(_inproj_kernel, ta=ta),
        out_shape=(
            jax.ShapeDtypeStruct((T, QK_DIFF), BF16),
            jax.ShapeDtypeStruct((T, RW_PAD_COLS), F32),
            jax.ShapeDtypeStruct((T, QK_FOX), BF16),
            jax.ShapeDtypeStruct((T, LANES), F32),
            jax.ShapeDtypeStruct((T // ta, DIFF_WIDTH, ta), BF16),
            jax.ShapeDtypeStruct((T // ta, FOX_WIDTH, ta), BF16),
        ),
        grid=(T // tm,),
        in_specs=[
            pl.BlockSpec((tm, D), row),
            pl.BlockSpec((1, D), lambda i: (0, 0)),
            pl.BlockSpec((None, None, 1, D), lambda i: (i // nb, 0, 0, 0)),
            pl.BlockSpec((None, None, 1, D), lambda i: (i // nb, 1, 0, 0)),
            pl.BlockSpec((D, IN_PAD_COLS), lambda i: (0, 0)),
            pl.BlockSpec((VT_ROWS, D), lambda i: (0, 0)),
        ],
        out_specs=(
            pl.BlockSpec((tm, QK_DIFF), row),
            pl.BlockSpec((tm, RW_PAD_COLS), row),
            pl.BlockSpec((tm, QK_FOX), row),
            pl.BlockSpec((tm, LANES), row),
            pl.BlockSpec((tm // ta, DIFF_WIDTH, ta), lambda i: (i, 0, 0)),
            pl.BlockSpec((tm // ta, FOX_WIDTH, ta), lambda i: (i, 0, 0)),
        ),
        compiler_params=_params(("parallel",)),
        name="in_proj",
    )(x2, g.reshape(1, D), mod4, mod4, w_pad, w_vt)


def _pad_w_in(w_in):
    D = w_in.shape[0]
    W = RWKV_WIDTH
    o = DIFF_COLS
    z64 = jnp.zeros((D, LANES - DECAY_LORA), w_in.dtype)
    rw = w_in[:, o:o + RWKV_COLS]
    fx = w_in[:, o + RWKV_COLS:]
    zf = jnp.zeros((D, LANES - FOX_HEADS), w_in.dtype)
    w_pad = jnp.concatenate([
        w_in[:, :QK_DIFF],
        rw[:, :3 * W], rw[:, 3 * W:3 * W + DECAY_LORA], z64,
        rw[:, 3 * W + DECAY_LORA:3 * W + DECAY_LORA + AAA_LORA], z64,
        rw[:, 3 * W + DECAY_LORA + AAA_LORA:],
        fx[:, :QK_FOX], fx[:, 3 * FOX_WIDTH:], zf,
    ], axis=1).astype(BF16)
    w_vt = jnp.concatenate([w_in[:, QK_DIFF:o], fx[:, QK_FOX:3 * FOX_WIDTH]], axis=1).T.astype(BF16)
    return w_pad, w_vt


LOG2E = math.log2(math.e)


ACC_ROWS = LANES + 16


def _scaled_q(q, c):
    return (q.astype(F32) * c).astype(BF16)


def _with_ones(vt):
    return jnp.concatenate([vt, jnp.ones((ACC_ROWS - LANES, vt.shape[1]), vt.dtype)], axis=0)


def _flash_step(s2s, vt1, m_ref, acc_ref):
    n = len(s2s)
    m_old = [m_ref[x] for x in range(n)]
    m_new = [jnp.maximum(m_old[x], jnp.max(s2s[x], axis=0, keepdims=True)) for x in range(n)]
    alpha = [jnp.exp2(m_old[x] - m_new[x]) for x in range(n)]
    p = [jnp.exp2(s2s[x] - m_new[x]).astype(BF16) for x in range(n)]
    pv = [_mm(vt1, p[x]) for x in range(n)]
    for x in range(n):
        acc_ref[x] = alpha[x] * acc_ref[x] + pv[x]
        m_ref[x] = m_new[x]


def _diff_attn_kernel(lam_ref, g_ref, q_ref, k_ref, vt_ref, o_ref, m_ref, acc_ref, *, tq, lam_init):
    i = pl.program_id(2)
    lane = lax.broadcasted_iota(jnp.int32, (1, LANES), 1)
    q = _scaled_q(q_ref[...], (DIFF_QK_DIM ** -0.5) * LOG2E)
    zero = jnp.zeros_like(q)
    qm = [jnp.where((lane >= DIFF_QK_DIM * x) & (lane < DIFF_QK_DIM * (x + 1)), q, zero) for x in range(4)]
    m_ref[...] = jnp.full(m_ref.shape, -jnp.inf, F32)
    acc_ref[...] = jnp.zeros(acc_ref.shape, F32)

    def step(j, diag):
        k = k_ref[pl.ds(pl.multiple_of(j * tq, tq), tq), :]
        vt1 = _with_ones(vt_ref[j])
        if diag:
            keep = (lax.broadcasted_iota(jnp.int32, (tq, tq), 1) >= lax.broadcasted_iota(jnp.int32, (tq, tq), 0))
        s2s = [_nt(k, qm[x]) for x in range(4)]
        if diag:
            s2s = [jnp.where(keep, s2, -jnp.inf) for s2 in s2s]
        _flash_step(s2s, vt1, m_ref, acc_ref)

    def body(j, carry):
        step(j, False)
        return carry

    lax.fori_loop(0, i, body, 0)
    step(i, True)

    lp = lam_ref[...]
    lam = (jnp.exp(jnp.sum(lp[0:1] * lp[1:2], axis=-1, keepdims=True))
           - jnp.exp(jnp.sum(lp[2:3] * lp[3:4], axis=-1, keepdims=True)) + lam_init)
    sm = [acc_ref[x, 0:LANES, :] / acc_ref[x, LANES:LANES + 1, :] for x in range(4)]
    outs = [sm[2 * hh] - lam * sm[2 * hh + 1] for hh in range(2)]
    row = lax.broadcasted_iota(jnp.int32, (LANES, 1), 0)
    o = jnp.where(row < HEAD_DIM, outs[0], outs[1])
    sq = o * o
    ms = jnp.where(row < HEAD_DIM, jnp.sum(sq[0:HEAD_DIM], axis=0, keepdims=True),
                   jnp.sum(sq[HEAD_DIM:], axis=0, keepdims=True)) * (1.0 / HEAD_DIM)
    y = o * lax.rsqrt(ms + RMS_EPS) * g_ref[...] * (1.0 - lam_init)
    o_ref[...] = y.T.astype(o_ref.dtype)


def _diff_attention(dqk, vtd, lam_params, subln_g, layer_idx, B, S):
    T = B * S
    tq = min(ATT_T, S)
    nq = S // tq
    npair = DIFF_HEADS // 2
    lam_init = 0.8 - 0.6 * math.exp(-0.3 * layer_idx)
    g2 = jnp.concatenate([subln_g, subln_g]).reshape(LANES, 1).astype(F32)
    return pl.pallas_call(
        functools.partial(_diff_attn_kernel, tq=tq, lam_init=lam_init),
        out_shape=jax.ShapeDtypeStruct((T, DIFF_WIDTH), BF16),
        grid=(B, npair, nq),
        in_specs=[
            pl.BlockSpec((4, DIFF_QK_DIM), lambda b, p, i: (0, 0)),
            pl.BlockSpec((LANES, 1), lambda b, p, i: (0, 0)),
            pl.BlockSpec((tq, LANES), lambda b, p, i: (b * nq + i, p)),
            pl.BlockSpec((S, LANES), lambda b, p, i: (b, npair + p)),
            pl.BlockSpec((nq, LANES, tq), lambda b, p, i: (b, p, 0)),
        ],
        out_specs=pl.BlockSpec((tq, LANES), lambda b, p, i: (b * nq + i, p)),
        scratch_shapes=[
            pltpu.VMEM((4, 1, tq), F32),
            pltpu.VMEM((4, ACC_ROWS, tq), F32),
        ],
        compiler_params=_params(("parallel", "parallel", "arbitrary")),
        name="diff_attn",
    )(lam_params, g2, dqk, dqk, vtd)


def _fox_cum_kernel(f_ref, b_ref, rep_ref, row_ref, *, S, tc):
    rr = lax.broadcasted_iota(jnp.int32, (tc, tc), 0)
    cc = lax.broadcasted_iota(jnp.int32, (tc, tc), 1)
    tri = (rr >= cc).astype(F32)
    sel_r = lax.broadcasted_iota(jnp.int32, (LANES, LANES), 0)
    carry = jnp.zeros((1, LANES), F32)
    for c in range(S // tc):
        z = f_ref[c * tc:(c + 1) * tc, :] + b_ref[...]
        logf = -(jnp.maximum(-z, 0.0) + jnp.log(1.0 + jnp.exp(-jnp.abs(z))))
        cum = _mm(tri, logf, HIGHEST) + carry
        carry = cum[tc - 1:tc, :]
        row_ref[:, c * tc:(c + 1) * tc] = cum.T[0:8, :]
        for h in range(FOX_HEADS):
            rep_ref[h, c * tc:(c + 1) * tc, :] = _mm(cum, (sel_r == h).astype(F32), HIGHEST)


def _fox_cum(fl, b_f, B, S):
    tc = 256 if S % 256 == 0 else S
    bpad = jnp.zeros((1, LANES), F32).at[0, :FOX_HEADS].set(b_f.astype(F32))
    return pl.pallas_call(
        functools.partial(_fox_cum_kernel, S=S, tc=tc),
        out_shape=(jax.ShapeDtypeStruct((B, FOX_HEADS, S, LANES), F32), jax.ShapeDtypeStruct((B, 8, S), F32)),
        grid=(B,),
        in_specs=[pl.BlockSpec((S, LANES), lambda b: (b, 0)), pl.BlockSpec((1, LANES), lambda b: (0, 0))],
        out_specs=(pl.BlockSpec((None, FOX_HEADS, S, LANES), lambda b: (b, 0, 0, 0)),
                   pl.BlockSpec((None, 8, S), lambda b: (b, 0, 0))),
        compiler_params=_params(("parallel",)),
        name="fox_cum",
    )(fl, bpad)


def _fox_attn_kernel(q_ref, k_ref, vt_ref, c0_ref, c1_ref, cr_ref, o_ref, m_ref, acc_ref, *, tq):
    p_id = pl.program_id(1)
    i = pl.program_id(2)
    lane = lax.broadcasted_iota(jnp.int32, (1, LANES), 1)
    q = _scaled_q(q_ref[...], (HEAD_DIM ** -0.5) * LOG2E)
    zero = jnp.zeros_like(q)
    qm = [jnp.where((lane >= HEAD_DIM * x) & (lane < HEAD_DIM * (x + 1)), q, zero) for x in range(2)]
    ck_refs = (c0_ref, c1_ref)
    cq = [cr_ref[2 * p_id + x, pl.ds(i, 1), :] for x in range(2)]
    m_ref[...] = jnp.full(m_ref.shape, -jnp.inf, F32)
    acc_ref[...] = jnp.zeros(acc_ref.shape, F32)

    def step(j, diag):
        off = pl.multiple_of(j * tq, tq)
        k = k_ref[pl.ds(off, tq), :]
        vt1 = _with_ones(vt_ref[j])
        if diag:
            keep = (lax.broadcasted_iota(jnp.int32, (tq, tq), 1) >= lax.broadcasted_iota(jnp.int32, (tq, tq), 0))
        s2s = []
        for x in range(2):
            ck = ck_refs[x][pl.ds(off, tq), :]
            bias = (cq[x] - jnp.concatenate([ck] * (tq // LANES), axis=1)) * LOG2E
            s2s.append(_nt(k, qm[x]) + bias)
        if diag:
            s2s = [jnp.where(keep, s2, -jnp.inf) for s2 in s2s]
        _flash_step(s2s, vt1, m_ref, acc_ref)

    def body(j, carry):
        step(j, False)
        return carry

    lax.fori_loop(0, i, body, 0)
    step(i, True)
    row = lax.broadcasted_iota(jnp.int32, (LANES, 1), 0)
    sm = [acc_ref[x, 0:LANES, :] / acc_ref[x, LANES:LANES + 1, :] for x in range(2)]
    o = jnp.where(row < HEAD_DIM, sm[0], sm[1])
    o_ref[...] = o.T.astype(o_ref.dtype)


def _fox_attention(fqk, vtf, crep, crow, B, S):
    T = B * S
    tq = min(ATT_T, S)
    nq = S // tq
    npair = FOX_HEADS // 2
    crow4 = crow.reshape(B, 8, nq, tq)
    rep = lambda x: pl.BlockSpec((None, None, S, LANES), lambda b, p, i: (b, 2 * p + x, 0, 0))
    return pl.pallas_call(
        functools.partial(_fox_attn_kernel, tq=tq),
        out_shape=jax.ShapeDtypeStruct((T, FOX_WIDTH), BF16),
        grid=(B, npair, nq),
        in_specs=[
            pl.BlockSpec((tq, LANES), lambda b, p, i: (b * nq + i, p)),
            pl.BlockSpec((S, LANES), lambda b, p, i: (b, npair + p)),
            pl.BlockSpec((nq, LANES, tq), lambda b, p, i: (b, p, 0)),
            rep(0), rep(1),
            pl.BlockSpec((None, 8, nq, tq), lambda b, p, i: (b, 0, 0, 0)),
        ],
        out_specs=pl.BlockSpec((tq, LANES), lambda b, p, i: (b * nq + i, p)),
        scratch_shapes=[
            pltpu.VMEM((2, 1, tq), F32),
            pltpu.VMEM((2, ACC_ROWS, tq), F32),
        ],
        compiler_params=_params(("parallel", "parallel", "arbitrary")),
        name="fox_attn",
    )(fqk, fqk, vtf, crep, crep, crow4)


def _split3(x):
    hi = x.astype(BF16)
    r1 = x - hi.astype(F32)
    mid = r1.astype(BF16)
    lo = (r1 - mid.astype(F32)).astype(BF16)
    return hi, mid, lo


def _seg_sum(x, seg, npair):
    parts = _split3(x)
    return jnp.concatenate(
        [sum(_mm(t[:, LANES * p:LANES * (p + 1)], seg) for t in parts) for p in range(npair)], axis=1)


def _rwkv_kernel(x_ref, mu_ref, w0_ref, w2_ref, a0_ref, a2_ref, g2_ref, kk_ref, ka_ref, rk_ref, lng_ref, lnb_ref,
                 o_ref, carry_ref, st_ref, at_ref, rt_ref, bt_ref, kt_ref, v_ref, wc_ref, y_ref, g_ref, bon_ref,
                 *, tt, prec):
    i = pl.program_id(1)
    W = RWKV_WIDTH
    C = RW_CHUNK
    CPI = math.gcd(RW_CHUNKS_PER_ITER, tt // C)
    npair = RWKV_HEADS // 2

    @pl.when(i == 0)
    def _():
        carry_ref[...] = jnp.zeros(carry_ref.shape, F32)
        st_ref[...] = jnp.zeros(st_ref.shape, F32)

    x = x_ref[...]
    rows = lax.broadcasted_iota(jnp.int32, (tt, 1), 0)
    prev = jnp.where(rows == 0, carry_ref[...], pltpu.roll(x, 1, axis=0))
    carry_ref[...] = x[tt - 1:tt, :]
    xs = x + (prev - x) * mu_ref[...]
    r = xs[:, 0:W]
    k = xs[:, W:2 * W]
    v = xs[:, 2 * W:3 * W]
    xw = xs[:, 3 * W:3 * W + LANES]
    xa = xs[:, 3 * W + LANES:3 * W + 2 * LANES]
    xg = xs[:, 3 * W + 2 * LANES:]
    wl = w0_ref[...] + _mm(jnp.tanh(xw), w2_ref[...], HIGHEST)
    w = -(jnp.maximum(-wl, 0.0) + jnp.log(1.0 + jnp.exp(-jnp.abs(wl)))) - 0.5
    logdec = -jnp.exp(w)
    a = jax.nn.sigmoid(a0_ref[...] + _mm(xa, a2_ref[...], HIGHEST))
    g_ref[...] = _mm(jax.nn.sigmoid(xg), g2_ref[...], HIGHEST)

    r_i = lax.broadcasted_iota(jnp.int32, (LANES, LANES), 0)
    c_i = lax.broadcasted_iota(jnp.int32, (LANES, LANES), 1)
    seg = ((r_i // HEAD_DIM) == (c_i // HEAD_DIM)).astype(BF16)
    kkv = k * kk_ref[...]
    kkn = kkv / jnp.maximum(jnp.sqrt(_seg_sum(kkv * kkv, seg, npair)), 1e-12)
    k2 = k * (1.0 + (a - 1.0) * ka_ref[...])
    bon_ref[...] = _seg_sum(r * k2 * rk_ref[...], seg, npair) * v

    rt_i = lax.broadcasted_iota(jnp.int32, (tt, tt), 0)
    ct_i = lax.broadcasted_iota(jnp.int32, (tt, tt), 1)
    tri = (((rt_i // C) == (ct_i // C)) & (rt_i >= ct_i)).astype(BF16)
    cum = sum(_mm(tri, t) for t in _split3(logdec))
    winv = jnp.exp(-cum)
    wcum = jnp.exp(cum)
    at_ref[...] = -kkn * jnp.exp(cum - logdec)
    bt_ref[...] = kkn * a * winv
    kt_ref[...] = k2 * winv
    rt_ref[...] = r * wcum
    v_ref[...] = v
    wc_ref[...] = wcum

    lane = lax.broadcasted_iota(jnp.int32, (1, LANES), 1)
    lo = lane < HEAD_DIM
    tpos = r_i % C
    ipos = c_i % C
    strict = tpos > ipos
    incl = tpos >= ipos
    eye = r_i == c_i

    def stack2(m):
        return jnp.concatenate([jnp.where(lo, m, 0.0), jnp.where(lo, 0.0, m)], axis=0)

    def cast(m):
        return m if prec is not None else m.astype(BF16)

    def mm(p, q):
        return _mm(cast(p), cast(q), prec)

    def nt(p, q):
        return _nt(cast(p), cast(q), prec)

    def chunks(cc, carry):
        units = [(ci, p) for ci in range(CPI) for p in range(npair)]
        r0 = [pl.multiple_of((cc * CPI + ci) * C, C) for ci in range(CPI)]
        ld = lambda ref, ci, p: stack2(ref[pl.ds(r0[ci], C), LANES * p:LANES * (p + 1)])
        a2 = {u: ld(at_ref, *u) for u in units}
        r2 = {u: ld(rt_ref, *u) for u in units}
        b2 = {u: ld(bt_ref, *u) for u in units}
        k2s = {u: ld(kt_ref, *u) for u in units}
        v2 = {u: ld(v_ref, *u) for u in units}
        wl = {(ci, p): wc_ref[pl.ds(pl.multiple_of(r0[ci] + C - 8, 8), 8), LANES * p:LANES * (p + 1)][7:8, :]
              for (ci, p) in units}
        ar = {u: jnp.concatenate([a2[u], r2[u]], axis=0) for u in units}
        mb = {u: nt(ar[u], b2[u]) for u in units}
        mk = {u: nt(ar[u], k2s[u]) for u in units}
        lab = {u: jnp.where(strict, mb[u][0:LANES], 0.0) for u in units}
        mrb = {u: jnp.where(incl, mb[u][LANES:], 0.0) for u in units}
        lak = {u: jnp.where(strict, mk[u][0:LANES], 0.0) for u in units}
        mrk = {u: jnp.where(incl, mk[u][LANES:], 0.0) for u in units}
        xx = {u: jnp.concatenate([a2[u], mm(lak[u], v2[u])], axis=1) for u in units}
        lp = lab
        for it in range(6):
            xx = {u: xx[u] + mm(lp[u], xx[u]) for u in units}
            if it < 5:
                lp = {u: mm(lp[u], lp[u]) for u in units}
        mq = {u: mm(mrb[u], xx[u]) for u in units}
        mv = {u: mm(mrk[u], v2[u]) for u in units}
        bx = {u: mm((b2[u] * wl[u]).T, xx[u]) for u in units}
        kv = {u: mm((k2s[u] * wl[u]).T, v2[u]) for u in units}
        st = [st_ref[p] for p in range(npair)]
        for (ci, p) in units:
            u = (ci, p)
            q1 = r2[u] + mq[u][:, 0:LANES]
            q2 = mq[u][:, LANES:] + mv[u]
            gmat = jnp.where(eye, jnp.broadcast_to(wl[u], (LANES, LANES)), 0.0) + bx[u][:, 0:LANES]
            hmat = bx[u][:, LANES:] + kv[u]
            gs = mm(jnp.concatenate([gmat, q1], axis=0), st[p])
            st[p] = gs[0:LANES] + hmat
            yy = gs[LANES:] + q2
            y_ref[pl.ds(r0[ci], C), LANES * p:LANES * (p + 1)] = yy[0:C] + yy[C:]
        for p in range(npair):
            st_ref[p] = st[p]
        return carry

    lax.fori_loop(0, tt // (C * CPI), chunks, 0)

    y = y_ref[...]
    inv = 1.0 / HEAD_DIM
    mean = _seg_sum(y, seg, npair) * inv
    yc = y - mean
    var = _seg_sum(yc * yc, seg, npair) * inv
    yn = yc * lax.rsqrt(var + RWKV_LN_EPS) * lng_ref[...] + lnb_ref[...]
    o_ref[...] = ((yn + bon_ref[...]) * g_ref[...]).astype(o_ref.dtype)


def _rwkv(rcols, mu, w0, w2, a0, a2, g2, k_k, k_a, r_k, ln_g, ln_b, B, S, prec=HIGHEST):
    T = B * S
    W = RWKV_WIDTH
    tt = 512 if S % 512 == 0 else S
    nt_ = S // tt
    npair = RWKV_HEADS // 2
    pad = LANES - DECAY_LORA
    mu_p = jnp.concatenate([mu[:3 * W], mu[3 * W:3 * W + DECAY_LORA], jnp.zeros((pad,), F32),
                            mu[3 * W + DECAY_LORA:3 * W + DECAY_LORA + AAA_LORA], jnp.zeros((pad,), F32),
                            mu[3 * W + DECAY_LORA + AAA_LORA:]]).reshape(1, RW_PAD_COLS)
    w2p = jnp.concatenate([w2, jnp.zeros((pad, W), F32)], axis=0)
    a2p = jnp.concatenate([a2, jnp.zeros((pad, W), F32)], axis=0)
    vec = lambda t: t.reshape(1, W).astype(F32)
    full = lambda shape: pl.BlockSpec(shape, lambda b, i: (0,) * len(shape))
    sc = lambda: pltpu.VMEM((tt, W), F32)
    return pl.pallas_call(
        functools.partial(_rwkv_kernel, tt=tt, prec=prec),
        out_shape=jax.ShapeDtypeStruct((T, W), BF16),
        grid=(B, nt_),
        in_specs=[
            pl.BlockSpec((tt, RW_PAD_COLS), lambda b, i: (b * nt_ + i, 0)),
            full((1, RW_PAD_COLS)), full((1, W)), full((LANES, W)), full((1, W)), full((LANES, W)),
            full((GATE_LORA, W)), full((1, W)), full((1, W)), full((1, W)), full((1, W)), full((1, W)),
        ],
        out_specs=pl.BlockSpec((tt, W), lambda b, i: (b * nt_ + i, 0)),
        scratch_shapes=[
            pltpu.VMEM((1, RW_PAD_COLS), F32),
            pltpu.VMEM((npair, LANES, LANES), F32),
            sc(), sc(), sc(), sc(), sc(), sc(), sc(), sc(), sc(),
        ],
        compiler_params=_params(("parallel", "arbitrary")),
        name="rwkv7",
    )(rcols, mu_p, vec(w0), w2p, vec(a0), a2p, g2, vec(k_k), vec(k_a), vec(r_k), vec(ln_g), vec(ln_b))


def _outproj_kernel(ya_ref, yb_ref, yc_ref, x_ref, w_ref, gm_ref, g_ref, sh_ref, sc_ref, xo_ref, ho_ref):
    o1 = DIFF_WIDTH
    o2 = o1 + RWKV_WIDTH
    mix = (_mm(ya_ref[...], w_ref[0:o1, :]) + _mm(yb_ref[...], w_ref[o1:o2, :])
           + _mm(yc_ref[...], w_ref[o2:, :]))
    xn = x_ref[...] + gm_ref[...] * mix
    xo_ref[...] = xn
    ms = jnp.mean(xn * xn, axis=-1, keepdims=True)
    y = xn * lax.rsqrt(ms + RMS_EPS) * g_ref[...]
    ho_ref[...] = y * (1.0 + sc_ref[...]) + sh_ref[...]


def _out_proj(ya, yb, yc, x2, w_out, g, mod4, S):
    T, D = x2.shape
    tm = 512 if S % 512 == 0 else S
    nb = S // tm
    row = lambda i: (i, 0)
    modspec = lambda which: pl.BlockSpec((None, None, 1, D), lambda i: (i // nb, which, 0, 0))
    return pl.pallas_call(
        _outproj_kernel,
        out_shape=(jax.ShapeDtypeStruct((T, D), F32), jax.ShapeDtypeStruct((T, D), F32)),
        grid=(T // tm,),
        in_specs=[
            pl.BlockSpec((tm, DIFF_WIDTH), row), pl.BlockSpec((tm, RWKV_WIDTH), row), pl.BlockSpec((tm, FOX_WIDTH), row),
            pl.BlockSpec((tm, D), row),
            pl.BlockSpec((D, D), lambda i: (0, 0)),
            modspec(2),
            pl.BlockSpec((1, D), lambda i: (0, 0)),
            modspec(3), modspec(4),
        ],
        out_specs=(pl.BlockSpec((tm, D), row), pl.BlockSpec((tm, D), row)),
        compiler_params=_params(("parallel",)),
        name="out_proj",
    )(ya, yb, yc, x2, w_out.astype(BF16), mod4, g.reshape(1, D), mod4, mod4)


def _top16(s, iota_f, n):
    vals, poss = [], []
    for _ in range(PEER_TOPK):
        m = jnp.max(s, axis=0, keepdims=True)
        pos = jnp.min(jnp.where(s == m, iota_f, float(n)), axis=0, keepdims=True)
        vals.append(m)
        poss.append(pos)
        s = jnp.where(iota_f == pos, -jnp.inf, s)
    return jnp.concatenate(vals, axis=0), jnp.concatenate(poss, axis=0)


ROUTE_UNROLL = 8
PEER_NCAND = 56


def _peer_cand_tables():
    K = PEER_TOPK
    pairs = [(a, b) for a in range(K) for b in range(K) if (a + 1) * (b + 1) <= K]
    n = PEER_NCAND
    p0 = [[0.0] * K for _ in range(n)]
    p1 = [[0.0] * K for _ in range(n)]
    pad = [0.0] * n
    pos = [float(K * K + r) for r in range(n)]
    for r, (a, b) in enumerate(pairs):
        p0[r][a] = 1.0
        p1[r][b] = 1.0
        pos[r] = float(a * K + b)
    for r in range(len(pairs), n):
        pad[r] = -float("inf")
    col = lambda v: jnp.broadcast_to(jnp.asarray(v, F32)[:, None], (n, LANES))
    return jnp.asarray(p0, F32), jnp.asarray(p1, F32), col(pad), col(pos)


def _peer_route_kernel(h_ref, wq_ref, sk_ref, p0_ref, p1_ref, cpad_ref, cpos_ref, idx_ref, gate_ref, q_scr, e_scr,
                       g_scr):
    K = PEER_TOPK
    hb = h_ref[...].astype(BF16)
    q = _mm(hb, wq_ref[...])
    for hc in range(2 * PEER_HEADS):
        q_scr[hc] = q[:, LANES * hc:LANES * (hc + 1)].astype(BF16)
    iota_n = lax.broadcasted_iota(jnp.int32, (PEER_NKEYS, LANES), 0).astype(F32)
    cpos = cpos_ref[...]

    def one_head(h):
        sv0, si0 = _top16(_nt(sk_ref[2 * h], q_scr[2 * h]), iota_n, PEER_NKEYS)
        sv1, si1 = _top16(_nt(sk_ref[2 * h + 1], q_scr[2 * h + 1]), iota_n, PEER_NKEYS)
        cand = _mm(p0_ref[...], sv0, HIGHEST) + _mm(p1_ref[...], sv1, HIGHEST) + cpad_ref[...]
        cidx = _mm(p0_ref[...], si0) * float(PEER_NKEYS) + _mm(p1_ref[...], si1)
        fv, es = [], []
        for _ in range(K):
            m = jnp.max(cand, axis=0, keepdims=True)
            pos = jnp.min(jnp.where(cand == m, cpos, float(2 * K * K)), axis=0, keepdims=True)
            hit = cpos == pos
            fv.append(m)
            es.append(jnp.max(jnp.where(hit, cidx, -1.0), axis=0, keepdims=True))
            cand = jnp.where(hit, -jnp.inf, cand)
        fv = jnp.concatenate(fv, axis=0)
        ex = jnp.exp(fv - fv[0:1, :])
        g_scr[h] = ex / jnp.sum(ex, axis=0, keepdims=True)
        e_scr[h] = jnp.concatenate(es, axis=0)

    def heads(hh, carry):
        for j in range(ROUTE_UNROLL):
            one_head(hh * ROUTE_UNROLL + j)
        return carry

    lax.fori_loop(0, PEER_HEADS // ROUTE_UNROLL, heads, 0)
    e = e_scr[...].reshape(PEER_HEADS * K, LANES)
    idx_ref[...] = e.T.astype(jnp.int32)
    gate_ref[...] = g_scr[...].reshape(PEER_HEADS * K, LANES)


def _peer_route(h2, wq, subkeys, tok0=0, ntok=None):
    T, D = h2.shape
    T = T - tok0 if ntok is None else ntok
    tm = LANES
    blk0 = tok0 // tm
    nq = 2 * PEER_HEADS
    sk = subkeys.reshape(nq, PEER_NKEYS, PEER_HALF).astype(BF16)
    p0, p1, cpad, cpos = _peer_cand_tables()
    const = lambda shape: pl.BlockSpec(shape, lambda i: (0,) * len(shape))
    return pl.pallas_call(
        _peer_route_kernel,
        out_shape=(jax.ShapeDtypeStruct((T, PEER_HEADS * PEER_TOPK), jnp.int32),
                   jax.ShapeDtypeStruct((T // tm, PEER_HEADS * PEER_TOPK, tm), F32)),
        grid=(T // tm,),
        in_specs=[
            pl.BlockSpec((tm, D), lambda i: (blk0 + i, 0)),
            const((D, nq * PEER_HALF)),
            const((nq, PEER_NKEYS, PEER_HALF)),
            const((PEER_NCAND, PEER_TOPK)), const((PEER_NCAND, PEER_TOPK)),
            const((PEER_NCAND, LANES)), const((PEER_NCAND, LANES)),
        ],
        out_specs=(pl.BlockSpec((tm, PEER_HEADS * PEER_TOPK), lambda i: (i, 0)),
                   pl.BlockSpec((None, PEER_HEADS * PEER_TOPK, tm), lambda i: (i, 0, 0))),
        scratch_shapes=[
            pltpu.VMEM((nq, tm, PEER_HALF), BF16),
            pltpu.VMEM((PEER_HEADS, PEER_TOPK, tm), F32),
            pltpu.VMEM((PEER_HEADS, PEER_TOPK, tm), F32),
        ],
        compiler_params=_params(("parallel",)),
        name="peer_route",
    )(h2, wq.astype(BF16), sk, p0, p1, cpad, cpos)


PEER_G = 16
PEER_SLOTS = PEER_HEADS * PEER_TOPK


def _peer_eval_kernel(idx_ref, idxn_ref, gate_ref, h_ref, x_ref, gf_ref, fg_ref, uv_ref, o_ref, buf, sem, *, final):
    G = PEER_G
    R = G * PEER_SLOTS
    D = D_MODEL
    tiles = PEER_SLOTS // SUBLANES
    i = pl.program_id(0)
    n = pl.num_programs(0)

    def start(ids, off, s, t, u):
        pltpu.make_async_copy(uv_ref.at[ids[off + t * SUBLANES + u]], buf.at[s, t, pl.ds(u, 1), :],
                              sem.at[s]).start(priority=u % 2)

    def wait(s):
        pltpu.make_async_copy(buf.at[s], buf.at[s], sem.at[s]).wait()

    @pl.when(i == 0)
    def _():
        def body(t, carry):
            for u in range(SUBLANES):
                start(idx_ref, 0, 0, t, u)
            return carry
        lax.fori_loop(0, R // SUBLANES, body, 0)

    lane = lax.broadcasted_iota(jnp.int32, (1, LANES), 1)
    tbase = (i % (LANES // (2 * G))) * (2 * G)
    gate = gate_ref[...]
    outs = []
    for s in range(2):
        wait(s)
        nxt_ids, nxt_off = (idx_ref, R) if s == 0 else (idxn_ref, 0)
        for g in range(G):
            for t in range(tiles * g, tiles * (g + 1)):
                for u in range(SUBLANES):
                    start(nxt_ids, nxt_off, 1 - s, t, u)
            w_rows = buf[s, tiles * g:tiles * (g + 1)].reshape(PEER_SLOTS, D)
            u_rows = lax.bitcast_convert_type(w_rows & jnp.uint32(0xFFFF0000), F32)
            prod = u_rows * h_ref[G * s + g:G * s + g + 1, :]
            part = prod[:, 0:LANES]
            for c in range(1, D // LANES):
                part = part + prod[:, LANES * c:LANES * (c + 1)]
            act = jnp.sum(part, axis=1, keepdims=True)
            gcol = jnp.sum(jnp.where(lane == tbase + G * s + g, gate, 0.0), axis=1, keepdims=True)
            coef = gcol * (0.5 * act * (1.0 + lax.erf(act * (2.0 ** -0.5))))
            v_rows = lax.bitcast_convert_type(w_rows << 16, F32)
            outs.append(jnp.sum(v_rows * coef, axis=0, keepdims=True))
    xn = x_ref[...] + gf_ref[...] * jnp.concatenate(outs, axis=0)
    if final:
        ms = jnp.mean(xn * xn, axis=-1, keepdims=True)
        xn = xn * lax.rsqrt(ms + RMS_EPS) * fg_ref[...]
    o_ref[...] = xn

    @pl.when(i == n - 1)
    def _():
        wait(0)


def _peer_eval(eidx, gate_t, h2, x2, mod4, final_g, uv, S, final, ntok=None):
    T, D = x2.shape
    T = T if ntok is None else ntok
    G = PEER_G
    R = G * PEER_SLOTS
    n = T // (2 * G)
    return pl.pallas_call(
        functools.partial(_peer_eval_kernel, final=final),
        out_shape=jax.ShapeDtypeStruct(x2.shape, F32),
        input_output_aliases={4: 0},
        grid=(n,),
        in_specs=[
            pl.BlockSpec((2 * R,), lambda i: (i,), memory_space=pltpu.SMEM),
            pl.BlockSpec((R,), lambda i: (jnp.minimum(2 * i + 2, 2 * n - 2),), memory_space=pltpu.SMEM),
            pl.BlockSpec((None, PEER_SLOTS, LANES), lambda i: (i // (LANES // (2 * G)), 0, 0)),
            pl.BlockSpec((2 * G, D), lambda i: (i, 0)),
            pl.BlockSpec((2 * G, D), lambda i: (i, 0)),
            pl.BlockSpec((None, None, 1, D), lambda i: (i // (S // (2 * G)), 5, 0, 0)),
            pl.BlockSpec((1, D), lambda i: (0, 0)),
            pl.BlockSpec(memory_space=pl.ANY),
        ],
        out_specs=pl.BlockSpec((2 * G, D), lambda i: (i, 0)),
        scratch_shapes=[pltpu.VMEM((2, R // SUBLANES, SUBLANES, D), jnp.uint32), pltpu.SemaphoreType.DMA((2,))],
        compiler_params=_params(("arbitrary",)),
        name="peer_eval",
    )(eidx.reshape(-1), eidx.reshape(-1), gate_t, h2, x2, mod4, final_g.reshape(1, D), uv)


SC_WORKERS = 32
SC_WINDOW = 128
SC_ROWS = 32
PEER_GD = 16
SC_SHARE_NUM, SC_SHARE_DEN = 5, 8
SC_PARTS = 2
BATCH_GROUPS = 4


def _sc_gather(tab, idx):
    n = idx.shape[0]
    width = tab.shape[1]
    per = n // SC_WORKERS
    nsub = SC_WINDOW // SC_ROWS
    mesh = plsc.VectorSubcoreMesh(core_axis_name="core", subcore_axis_name="subcore")

    @pl.kernel(out_type=jax.ShapeDtypeStruct((n, width), tab.dtype), mesh=mesh,
               scratch_types=[pltpu.VMEM((SC_WINDOW,), jnp.int32), pltpu.VMEM((2, SC_ROWS, width), tab.dtype),
                              pltpu.SemaphoreType.DMA((2,)), pltpu.SemaphoreType.DMA((2,))])
    def gather(x_hbm, i_hbm, o_hbm, idx_v, rows_v, gsem, wsem):
        wid = lax.axis_index("core") * (SC_WORKERS // 2) + lax.axis_index("subcore")

        def start_gather(k):
            return pltpu.async_copy(x_hbm.at[idx_v.at[pl.ds(k * SC_ROWS, SC_ROWS)]], rows_v.at[k % 2], gsem.at[k % 2])

        @pl.loop(0, per // SC_WINDOW)
        def _(w):
            base = wid * per + w * SC_WINDOW
            pltpu.sync_copy(i_hbm.at[pl.ds(base, SC_WINDOW)], idx_v)
            gat = [start_gather(0)] + [None] * (nsub - 1)
            wrt = [None] * nsub
            for k in range(nsub):
                if k + 1 < nsub:
                    if k >= 1:
                        wrt[k - 1].wait()
                    gat[k + 1] = start_gather(k + 1)
                gat[k].wait()
                wrt[k] = pltpu.async_copy(rows_v.at[k % 2], o_hbm.at[pl.ds(base + k * SC_ROWS, SC_ROWS)],
                                          wsem.at[k % 2])
            for k in range(max(nsub - 2, 0), nsub):
                wrt[k].wait()

    return gather(tab, idx)


def _peer_dense_kernel(rows_ref, gate_ref, h_ref, x_ref, gf_ref, fg_ref, o_ref, *, final):
    G = PEER_GD
    D = D_MODEL
    i = pl.program_id(0)
    lane = lax.broadcasted_iota(jnp.int32, (1, LANES), 1)
    tbase = (i % (LANES // G)) * G
    gate = gate_ref[...]
    outs = []
    for g in range(G):
        w_rows = rows_ref[PEER_SLOTS * g:PEER_SLOTS * (g + 1), :]
        u_rows = lax.bitcast_convert_type(w_rows & jnp.uint32(0xFFFF0000), F32)
        prod = u_rows * h_ref[g:g + 1, :]
        part = prod[:, 0:LANES]
        for c in range(1, D // LANES):
            part = part + prod[:, LANES * c:LANES * (c + 1)]
        act = jnp.sum(part, axis=1, keepdims=True)
        gcol = jnp.sum(jnp.where(lane == tbase + g, gate, 0.0), axis=1, keepdims=True)
        coef = gcol * (0.5 * act * (1.0 + lax.erf(act * (2.0 ** -0.5))))
        v_rows = lax.bitcast_convert_type(w_rows << 16, F32)
        outs.append(jnp.sum(v_rows * coef, axis=0, keepdims=True))
    xn = x_ref[...] + gf_ref[...] * jnp.concatenate(outs, axis=0)
    if final:
        ms = jnp.mean(xn * xn, axis=-1, keepdims=True)
        xn = xn * lax.rsqrt(ms + RMS_EPS) * fg_ref[...]
    o_ref[...] = xn


def _peer_eval_dense(rows, gate_t, h2, x2, mod4, final_g, S, final, tok0):
    T, D = x2.shape
    G = PEER_GD
    tb = rows.shape[0] // PEER_SLOTS
    assert tok0 % LANES == 0 and tb % LANES == 0
    blk0 = tok0 // G
    return pl.pallas_call(
        functools.partial(_peer_dense_kernel, final=final),
        out_shape=jax.ShapeDtypeStruct(x2.shape, F32),
        input_output_aliases={3: 0},
        grid=(tb // G,),
        in_specs=[
            pl.BlockSpec((G * PEER_SLOTS, D), lambda i: (i, 0)),
            pl.BlockSpec((None, PEER_SLOTS, LANES), lambda i: (i // (LANES // G), 0, 0)),
            pl.BlockSpec((G, D), lambda i: (blk0 + i, 0)),
            pl.BlockSpec((G, D), lambda i: (blk0 + i, 0)),
            pl.BlockSpec((None, None, 1, D), lambda i: ((blk0 + i) // (S // G), 5, 0, 0)),
            pl.BlockSpec((1, D), lambda i: (0, 0)),
        ],
        out_specs=pl.BlockSpec((G, D), lambda i: (blk0 + i, 0)),
        compiler_params=_params(("parallel",)),
        name="peer_dense",
    )(rows, gate_t, h2, x2, mod4, final_g.reshape(1, D))


def _pack_uv(u, v):
    hi = lax.bitcast_convert_type(u.astype(BF16), jnp.uint16).astype(jnp.uint32) << 16
    lo = lax.bitcast_convert_type(v.astype(BF16), jnp.uint16).astype(jnp.uint32)
    return (hi | lo).reshape(u.shape[0], 1, u.shape[1])


def kernel(x, c, norm_mix_g, norm_ffn_g, final_norm_g, ada_w, ada_b, w_in, w_out, dif_lam, dif_subln_g, rw_mu, rw_w0,
           rw_w2, rw_a0, rw_a2, rw_g2, rw_kk, rw_ka, rw_rk, rw_ln_g, rw_ln_b, fox_bf, peer_wq, peer_subkeys, peer_u,
           peer_v):
    B, S, D = x.shape
    depth = ada_w.shape[0]
    mod = _ada_mod(c, ada_w, ada_b)
    ngrp = BATCH_GROUPS if B % BATCH_GROUPS == 0 else 1
    bg = B // ngrp
    tg = bg * S
    prep = [(_pad_w_in(w_in[l]), _pack_uv(peer_u[l], peer_v[l])) for l in range(depth)]

    def mix_route(l, g, x2):
        mod4 = mod[l, g * bg:(g + 1) * bg].reshape(bg, 6, 1, D)
        (w_pad, w_vt), uv = prep[l]
        dqk, rcols, fqk, fl, vtd, vtf = _in_proj(x2, norm_mix_g[l], mod4, w_pad, w_vt, S)
        ya = _diff_attention(dqk, vtd, dif_lam[l], dif_subln_g[l], l, bg, S)
        yb = _rwkv(rcols, rw_mu[l], rw_w0[l], rw_w2[l], rw_a0[l], rw_a2[l], rw_g2[l], rw_kk[l], rw_ka[l],
                   rw_rk[l].reshape(-1), rw_ln_g[l], rw_ln_b[l], bg, S, prec=None)
        crep, crow = _fox_cum(fl, fox_bf[l], bg, S)
        yc = _fox_attention(fqk, vtf, crep, crow, bg, S)
        x2, h2 = _out_proj(ya, yb, yc, x2, w_out[l], norm_ffn_g[l], mod4, S)
        unit = SC_PARTS * LANES
        tb = (tg * SC_SHARE_NUM // SC_SHARE_DEN) // unit * unit
        ta = tg - tb
        bounds = [ta + tb * j // SC_PARTS for j in range(SC_PARTS + 1)]
        parts = []
        for t0, t1 in zip(bounds[:-1], bounds[1:]):
            if t1 == t0:
                continue
            e_p, g_p = _peer_route(h2, peer_wq[l], peer_subkeys[l], t0, t1 - t0)
            parts.append((_sc_gather(uv.reshape(-1, D), e_p.reshape(-1)), g_p, t0))
        e_a, g_a = _peer_route(h2, peer_wq[l], peer_subkeys[l], 0, ta)
        return dict(x2=x2, h2=h2, mod4=mod4, uv=uv, ta=ta, e_a=e_a, g_a=g_a, parts=parts)

    def evaluate(l, st):
        final = l == depth - 1
        x2 = _peer_eval(st["e_a"], st["g_a"], st["h2"], st["x2"], st["mod4"], final_norm_g, st["uv"], S, final,
                        ntok=st["ta"])
        for r, g_p, t0 in st["parts"]:
            x2 = _peer_eval_dense(r, g_p, st["h2"], x2, st["mod4"], final_norm_g, S, final, t0)
        return x2

    xs = [x[g * bg:(g + 1) * bg].reshape(tg, D) for g in range(ngrp)]
    pending = []
    for l in range(depth):
        for g in range(ngrp):
            pending.append((l, g, mix_route(l, g, xs[g])))
            if len(pending) == ngrp:
                l0, g0, st = pending.pop(0)
                xs[g0] = evaluate(l0, st)
    for l0, g0, st in pending:
        xs[g0] = evaluate(l0, st)
    return jnp.concatenate(xs, axis=0).reshape(B, S, D)
```

```python
import functools
import math

import jax
import jax.numpy as jnp
from jax import lax
from jax.experimental import pallas as pl
from jax.experimental.pallas import tpu as pltpu
from jax.experimental.pallas import tpu_sc as plsc

F32 = jnp.float32
BF16 = jnp.bfloat16
HIGHEST = lax.Precision.HIGHEST

D_MODEL = 1024
HEAD_DIM = 64
DIFF_HEADS = 6
DIFF_QK_DIM = HEAD_DIM // 2
RWKV_HEADS = 6
FOX_HEADS = 4
DIFF_WIDTH = DIFF_HEADS * HEAD_DIM
RWKV_WIDTH = RWKV_HEADS * HEAD_DIM
FOX_WIDTH = FOX_HEADS * HEAD_DIM
DECAY_LORA = 64
AAA_LORA = 64
GATE_LORA = 128
DIFF_COLS = 3 * DIFF_WIDTH
RWKV_COLS = 3 * RWKV_WIDTH + DECAY_LORA + AAA_LORA + GATE_LORA
PEER_HEADS = 8
PEER_NKEYS = 128
PEER_TOPK = 16
PEER_QDIM = 256
PEER_HALF = PEER_QDIM // 2
RMS_EPS = 1e-6
RWKV_LN_EPS = 64e-5

LANES = 128
SUBLANES = 8
RW_PAD_COLS = 3 * RWKV_WIDTH + 3 * LANES
VMEM_LIMIT = 56 * 1024 * 1024

RW_CHUNK = 64
RW_CHUNKS_PER_ITER = 4


def _params(sem, vmem=VMEM_LIMIT):
    return pltpu.CompilerParams(dimension_semantics=sem, vmem_limit_bytes=vmem)


def _nt(a, b, precision=None):
    return lax.dot_general(a, b, (((1,), (1,)), ((), ())), preferred_element_type=F32, precision=precision)


def _mm(a, b, precision=None):
    return jnp.dot(a, b, preferred_element_type=F32, precision=precision)


def _ada_kernel(c_ref, w_ref, b_ref, o_ref):
    c = c_ref[...]
    ca = c * jax.nn.sigmoid(c)
    o_ref[...] = _mm(ca, w_ref[...], HIGHEST) + b_ref[...]


def _ada_mod(c, ada_w, ada_b):
    L, D, N = ada_w.shape
    B = c.shape[0]
    tn = 1536
    return pl.pallas_call(
        _ada_kernel,
        out_shape=jax.ShapeDtypeStruct((L, B, N), F32),
        grid=(L, N // tn),
        in_specs=[
            pl.BlockSpec((B, D), lambda l, j: (0, 0)),
            pl.BlockSpec((None, D, tn), lambda l, j: (l, 0, j)),
            pl.BlockSpec((None, 1, tn), lambda l, j: (l, 0, j)),
        ],
        out_specs=pl.BlockSpec((None, B, tn), lambda l, j: (l, 0, j)),
        compiler_params=_params(("parallel", "parallel")),
        name="ada_mod",
    )(c, ada_w, ada_b.reshape(L, 1, N))


ATT_T = 512
QK_DIFF = 2 * DIFF_WIDTH
QK_FOX = 2 * FOX_WIDTH
VT_ROWS = DIFF_WIDTH + FOX_WIDTH
IN_PAD_COLS = QK_DIFF + RW_PAD_COLS + QK_FOX + LANES


def _inproj_kernel(x_ref, g_ref, sh_ref, sc_ref, w_ref, wvt_ref, d_ref, r_ref, f_ref, fl_ref, vtd_ref, vtf_ref, *, ta):
    x = x_ref[...]
    ms = jnp.mean(x * x, axis=-1, keepdims=True)
    y = x * lax.rsqrt(ms + RMS_EPS) * g_ref[...]
    h = (y * (1.0 + sc_ref[...]) + sh_ref[...]).astype(BF16)
    o1 = QK_DIFF
    o2 = o1 + RW_PAD_COLS
    o3 = o2 + QK_FOX
    d_ref[...] = _mm(h, w_ref[:, 0:o1]).astype(BF16)
    r_ref[...] = _mm(h, w_ref[:, o1:o2])
    f_ref[...] = _mm(h, w_ref[:, o2:o3]).astype(BF16)
    fl_ref[...] = _mm(h, w_ref[:, o3:o3 + LANES])
    vt = _nt(wvt_ref[...], h).astype(BF16)
    for s in range(x.shape[0] // ta):
        vtd_ref[s] = vt[0:DIFF_WIDTH, ta * s:ta * (s + 1)]
        vtf_ref[s] = vt[DIFF_WIDTH:, ta * s:ta * (s + 1)]


def _in_proj(x2, g, mod4, w_pad, w_vt, S):
    T, D = x2.shape
    tm = 512 if S % 512 == 0 else S
    ta = min(ATT_T, S)
    nb = S // tm
    row = lambda i: (i, 0)
    return pl.pallas_call(
        functools.partial(_inproj_kernel, ta=ta),
        out_shape=(
            jax.ShapeDtypeStruct((T, QK_DIFF), BF16),
            jax.ShapeDtypeStruct((T, RW_PAD_COLS), F32),
            jax.ShapeDtypeStruct((T, QK_FOX), BF16),
            jax.ShapeDtypeStruct((T, LANES), F32),
            jax.ShapeDtypeStruct((T // ta, DIFF_WIDTH, ta), BF16),
            jax.ShapeDtypeStruct((T // ta, FOX_WIDTH, ta), BF16),
        ),
        grid=(T // tm,),
        in_specs=[
            pl.BlockSpec((tm, D), row),
            pl.BlockSpec((1, D), lambda i: (0, 0)),
            pl.BlockSpec((None, None, 1, D), lambda i: (i // nb, 0, 0, 0)),
            pl.BlockSpec((None, None, 1, D), lambda i: (i // nb, 1, 0, 0)),
            pl.BlockSpec((D, IN_PAD_COLS), lambda i: (0, 0)),
            pl.BlockSpec((VT_ROWS, D), lambda i: (0, 0)),
        ],
        out_specs=(
            pl.BlockSpec((tm, QK_DIFF), row),
            pl.BlockSpec((tm, RW_PAD_COLS), row),
            pl.BlockSpec((tm, QK_FOX), row),
            pl.BlockSpec((tm, LANES), row),
            pl.BlockSpec((tm // ta, DIFF_WIDTH, ta), lambda i: (i, 0, 0)),
            pl.BlockSpec((tm // ta, FOX_WIDTH, ta), lambda i: (i, 0, 0)),
        ),
        compiler_params=_params(("parallel",)),
        name="in_proj",
    )(x2, g.reshape(1, D), mod4, mod4, w_pad, w_vt)


def _pad_w_in(w_in):
    D = w_in.shape[0]
    W = RWKV_WIDTH
    o = DIFF_COLS
    z64 = jnp.zeros((D, LANES - DECAY_LORA), w_in.dtype)
    rw = w_in[:, o:o + RWKV_COLS]
    fx = w_in[:, o + RWKV_COLS:]
    zf = jnp.zeros((D, LANES - FOX_HEADS), w_in.dtype)
    w_pad = jnp.concatenate([
        w_in[:, :QK_DIFF],
        rw[:, :3 * W], rw[:, 3 * W:3 * W + DECAY_LORA], z64,
        rw[:, 3 * W + DECAY_LORA:3 * W + DECAY_LORA + AAA_LORA], z64,
        rw[:, 3 * W + DECAY_LORA + AAA_LORA:],
        fx[:, :QK_FOX], fx[:, 3 * FOX_WIDTH:], zf,
    ], axis=1).astype(BF16)
    w_vt = jnp.concatenate([w_in[:, QK_DIFF:o], fx[:, QK_FOX:3 * FOX_WIDTH]], axis=1).T.astype(BF16)
    return w_pad, w_vt


LOG2E = math.log2(math.e)


ACC_ROWS = LANES + 16


def _scaled_q(q, c):
    return (q.astype(F32) * c).astype(BF16)


def _with_ones(vt):
    return jnp.concatenate([vt, jnp.ones((ACC_ROWS - LANES, vt.shape[1]), vt.dtype)], axis=0)


def _flash_step(s2s, vt1, m_ref, acc_ref):
    n = len(s2s)
    m_old = [m_ref[x] for x in range(n)]
    m_new = [jnp.maximum(m_old[x], jnp.max(s2s[x], axis=0, keepdims=True)) for x in range(n)]
    alpha = [jnp.exp2(m_old[x] - m_new[x]) for x in range(n)]
    p = [jnp.exp2(s2s[x] - m_new[x]).astype(BF16) for x in range(n)]
    pv = [_mm(vt1, p[x]) for x in range(n)]
    for x in range(n):
        acc_ref[x] = alpha[x] * acc_ref[x] + pv[x]
        m_ref[x] = m_new[x]


def _diff_attn_kernel(lam_ref, g_ref, q_ref, k_ref, vt_ref, o_ref, m_ref, acc_ref, *, tq, lam_init):
    i = pl.program_id(2)
    lane = lax.broadcasted_iota(jnp.int32, (1, LANES), 1)
    q = _scaled_q(q_ref[...], (DIFF_QK_DIM ** -0.5) * LOG2E)
    zero = jnp.zeros_like(q)
    qm = [jnp.where((lane >= DIFF_QK_DIM * x) & (lane < DIFF_QK_DIM * (x + 1)), q, zero) for x in range(4)]
    m_ref[...] = jnp.full(m_ref.shape, -jnp.inf, F32)
    acc_ref[...] = jnp.zeros(acc_ref.shape, F32)

    def step(j, diag):
        k = k_ref[pl.ds(pl.multiple_of(j * tq, tq), tq), :]
        vt1 = _with_ones(vt_ref[j])
        if diag:
            keep = (lax.broadcasted_iota(jnp.int32, (tq, tq), 1) >= lax.broadcasted_iota(jnp.int32, (tq, tq), 0))
        s2s = [_nt(k, qm[x]) for x in range(4)]
        if diag:
            s2s = [jnp.where(keep, s2, -jnp.inf) for s2 in s2s]
        _flash_step(s2s, vt1, m_ref, acc_ref)

    def body(j, carry):
        step(j, False)
        return carry

    lax.fori_loop(0, i, body, 0)
    step(i, True)

    lp = lam_ref[...]
    lam = (jnp.exp(jnp.sum(lp[0:1] * lp[1:2], axis=-1, keepdims=True))
           - jnp.exp(jnp.sum(lp[2:3] * lp[3:4], axis=-1, keepdims=True)) + lam_init)
    sm = [acc_ref[x, 0:LANES, :] / acc_ref[x, LANES:LANES + 1, :] for x in range(4)]
    outs = [sm[2 * hh] - lam * sm[2 * hh + 1] for hh in range(2)]
    row = lax.broadcasted_iota(jnp.int32, (LANES, 1), 0)
    o = jnp.where(row < HEAD_DIM, outs[0], outs[1])
    sq = o * o
    ms = jnp.where(row < HEAD_DIM, jnp.sum(sq[0:HEAD_DIM], axis=0, keepdims=True),
                   jnp.sum(sq[HEAD_DIM:], axis=0, keepdims=True)) * (1.0 / HEAD_DIM)
    y = o * lax.rsqrt(ms + RMS_EPS) * g_ref[...] * (1.0 - lam_init)
    o_ref[...] = y.T.astype(o_ref.dtype)


def _diff_attention(dqk, vtd, lam_params, subln_g, layer_idx, B, S):
    T = B * S
    tq = min(ATT_T, S)
    nq = S // tq
    npair = DIFF_HEADS // 2
    lam_init = 0.8 - 0.6 * math.exp(-0.3 * layer_idx)
    g2 = jnp.concatenate([subln_g, subln_g]).reshape(LANES, 1).astype(F32)
    return pl.pallas_call(
        functools.partial(_diff_attn_kernel, tq=tq, lam_init=lam_init),
        out_shape=jax.ShapeDtypeStruct((T, DIFF_WIDTH), BF16),
        grid=(B, npair, nq),
        in_specs=[
            pl.BlockSpec((4, DIFF_QK_DIM), lambda b, p, i: (0, 0)),
            pl.BlockSpec((LANES, 1), lambda b, p, i: (0, 0)),
            pl.BlockSpec((tq, LANES), lambda b, p, i: (b * nq + i, p)),
            pl.BlockSpec((S, LANES), lambda b, p, i: (b, npair + p)),
            pl.BlockSpec((nq, LANES, tq), lambda b, p, i: (b, p, 0)),
        ],
        out_specs=pl.BlockSpec((tq, LANES), lambda b, p, i: (b * nq + i, p)),
        scratch_shapes=[
            pltpu.VMEM((4, 1, tq), F32),
            pltpu.VMEM((4, ACC_ROWS, tq), F32),
        ],
        compiler_params=_params(("parallel", "parallel", "arbitrary")),
        name="diff_attn",
    )(lam_params, g2, dqk, dqk, vtd)


def _fox_cum_kernel(f_ref, b_ref, rep_ref, row_ref, *, S, tc):
    rr = lax.broadcasted_iota(jnp.int32, (tc, tc), 0)
    cc = lax.broadcasted_iota(jnp.int32, (tc, tc), 1)
    tri = (rr >= cc).astype(F32)
    sel_r = lax.broadcasted_iota(jnp.int32, (LANES, LANES), 0)
    carry = jnp.zeros((1, LANES), F32)
    for c in range(S // tc):
        z = f_ref[c * tc:(c + 1) * tc, :] + b_ref[...]
        logf = -(jnp.maximum(-z, 0.0) + jnp.log(1.0 + jnp.exp(-jnp.abs(z))))
        cum = _mm(tri, logf, HIGHEST) + carry
        carry = cum[tc - 1:tc, :]
        row_ref[:, c * tc:(c + 1) * tc] = cum.T[0:8, :]
        for h in range(FOX_HEADS):
            rep_ref[h, c * tc:(c + 1) * tc, :] = _mm(cum, (sel_r == h).astype(F32), HIGHEST)


def _fox_cum(fl, b_f, B, S):
    tc = 256 if S % 256 == 0 else S
    bpad = jnp.zeros((1, LANES), F32).at[0, :FOX_HEADS].set(b_f.astype(F32))
    return pl.pallas_call(
        functools.partial(_fox_cum_kernel, S=S, tc=tc),
        out_shape=(jax.ShapeDtypeStruct((B, FOX_HEADS, S, LANES), F32), jax.ShapeDtypeStruct((B, 8, S), F32)),
        grid=(B,),
        in_specs=[pl.BlockSpec((S, LANES), lambda b: (b, 0)), pl.BlockSpec((1, LANES), lambda b: (0, 0))],
        out_specs=(pl.BlockSpec((None, FOX_HEADS, S, LANES), lambda b: (b, 0, 0, 0)),
                   pl.BlockSpec((None, 8, S), lambda b: (b, 0, 0))),
        compiler_params=_params(("parallel",)),
        name="fox_cum",
    )(fl, bpad)


def _fox_attn_kernel(q_ref, k_ref, vt_ref, c0_ref, c1_ref, cr_ref, o_ref, m_ref, acc_ref, *, tq):
    p_id = pl.program_id(1)
    i = pl.program_id(2)
    lane = lax.broadcasted_iota(jnp.int32, (1, LANES), 1)
    q = _scaled_q(q_ref[...], (HEAD_DIM ** -0.5) * LOG2E)
    zero = jnp.zeros_like(q)
    qm = [jnp.where((lane >= HEAD_DIM * x) & (lane < HEAD_DIM * (x + 1)), q, zero) for x in range(2)]
    ck_refs = (c0_ref, c1_ref)
    cq = [cr_ref[2 * p_id + x, pl.ds(i, 1), :] for x in range(2)]
    m_ref[...] = jnp.full(m_ref.shape, -jnp.inf, F32)
    acc_ref[...] = jnp.zeros(acc_ref.shape, F32)

    def step(j, diag):
        off = pl.multiple_of(j * tq, tq)
        k = k_ref[pl.ds(off, tq), :]
        vt1 = _with_ones(vt_ref[j])
        if diag:
            keep = (lax.broadcasted_iota(jnp.int32, (tq, tq), 1) >= lax.broadcasted_iota(jnp.int32, (tq, tq), 0))
        s2s = []
        for x in range(2):
            ck = ck_refs[x][pl.ds(off, tq), :]
            bias = (cq[x] - jnp.concatenate([ck] * (tq // LANES), axis=1)) * LOG2E
            s2s.append(_nt(k, qm[x]) + bias)
        if diag:
            s2s = [jnp.where(keep, s2, -jnp.inf) for s2 in s2s]
        _flash_step(s2s, vt1, m_ref, acc_ref)

    def body(j, carry):
        step(j, False)
        return carry

    lax.fori_loop(0, i, body, 0)
    step(i, True)
    row = lax.broadcasted_iota(jnp.int32, (LANES, 1), 0)
    sm = [acc_ref[x, 0:LANES, :] / acc_ref[x, LANES:LANES + 1, :] for x in range(2)]
    o = jnp.where(row < HEAD_DIM, sm[0], sm[1])
    o_ref[...] = o.T.astype(o_ref.dtype)


def _fox_attention(fqk, vtf, crep, crow, B, S):
    T = B * S
    tq = min(ATT_T, S)
    nq = S // tq
    npair = FOX_HEADS // 2
    crow4 = crow.reshape(B, 8, nq, tq)
    rep = lambda x: pl.BlockSpec((None, None, S, LANES), lambda b, p, i: (b, 2 * p + x, 0, 0))
    return pl.pallas_call(
        functools.partial(_fox_attn_kernel, tq=tq),
        out_shape=jax.ShapeDtypeStruct((T, FOX_WIDTH), BF16),
        grid=(B, npair, nq),
        in_specs=[
            pl.BlockSpec((tq, LANES), lambda b, p, i: (b * nq + i, p)),
            pl.BlockSpec((S, LANES), lambda b, p, i: (b, npair + p)),
            pl.BlockSpec((nq, LANES, tq), lambda b, p, i: (b, p, 0)),
            rep(0), rep(1),
            pl.BlockSpec((None, 8, nq, tq), lambda b, p, i: (b, 0, 0, 0)),
        ],
        out_specs=pl.BlockSpec((tq, LANES), lambda b, p, i: (b * nq + i, p)),
        scratch_shapes=[
            pltpu.VMEM((2, 1, tq), F32),
            pltpu.VMEM((2, ACC_ROWS, tq), F32),
        ],
        compiler_params=_params(("parallel", "parallel", "arbitrary")),
        name="fox_attn",
    )(fqk, fqk, vtf, crep, crep, crow4)


def _split3(x):
    hi = x.astype(BF16)
    r1 = x - hi.astype(F32)
    mid = r1.astype(BF16)
    lo = (r1 - mid.astype(F32)).astype(BF16)
    return hi, mid, lo


def _seg_sum(x, seg, npair):
    parts = _split3(x)
    return jnp.concatenate(
        [sum(_mm(t[:, LANES * p:LANES * (p + 1)], seg) for t in parts) for p in range(npair)], axis=1)


def _rwkv_kernel(x_ref, mu_ref, w0_ref, w2_ref, a0_ref, a2_ref, g2_ref, kk_ref, ka_ref, rk_ref, lng_ref, lnb_ref,
                 o_ref, carry_ref, st_ref, at_ref, rt_ref, bt_ref, kt_ref, v_ref, wc_ref, y_ref, g_ref, bon_ref,
                 *, tt, prec):
    i = pl.program_id(1)
    W = RWKV_WIDTH
    C = RW_CHUNK
    CPI = math.gcd(RW_CHUNKS_PER_ITER, tt // C)
    npair = RWKV_HEADS // 2

    @pl.when(i == 0)
    def _():
        carry_ref[...] = jnp.zeros(carry_ref.shape, F32)
        st_ref[...] = jnp.zeros(st_ref.shape, F32)

    x = x_ref[...]
    rows = lax.broadcasted_iota(jnp.int32, (tt, 1), 0)
    prev = jnp.where(rows == 0, carry_ref[...], pltpu.roll(x, 1, axis=0))
    carry_ref[...] = x[tt - 1:tt, :]
    xs = x + (prev - x) * mu_ref[...]
    r = xs[:, 0:W]
    k = xs[:, W:2 * W]
    v = xs[:, 2 * W:3 * W]
    xw = xs[:, 3 * W:3 * W + LANES]
    xa = xs[:, 3 * W + LANES:3 * W + 2 * LANES]
    xg = xs[:, 3 * W + 2 * LANES:]
    wl = w0_ref[...] + _mm(jnp.tanh(xw), w2_ref[...], HIGHEST)
    w = -(jnp.maximum(-wl, 0.0) + jnp.log(1.0 + jnp.exp(-jnp.abs(wl)))) - 0.5
    logdec = -jnp.exp(w)
    a = jax.nn.sigmoid(a0_ref[...] + _mm(xa, a2_ref[...], HIGHEST))
    g_ref[...] = _mm(jax.nn.sigmoid(xg), g2_ref[...], HIGHEST)

    r_i = lax.broadcasted_iota(jnp.int32, (LANES, LANES), 0)
    c_i = lax.broadcasted_iota(jnp.int32, (LANES, LANES), 1)
    seg = ((r_i // HEAD_DIM) == (c_i // HEAD_DIM)).astype(BF16)
    kkv = k * kk_ref[...]
    kkn = kkv / jnp.maximum(jnp.sqrt(_seg_sum(kkv * kkv, seg, npair)), 1e-12)
    k2 = k * (1.0 + (a - 1.0) * ka_ref[...])
    bon_ref[...] = _seg_sum(r * k2 * rk_ref[...], seg, npair) * v

    rt_i = lax.broadcasted_iota(jnp.int32, (tt, tt), 0)
    ct_i = lax.broadcasted_iota(jnp.int32, (tt, tt), 1)
    tri = (((rt_i // C) == (ct_i // C)) & (rt_i >= ct_i)).astype(BF16)
    cum = sum(_mm(tri, t) for t in _split3(logdec))
    winv = jnp.exp(-cum)
    wcum = jnp.exp(cum)
    at_ref[...] = -kkn * jnp.exp(cum - logdec)
    bt_ref[...] = kkn * a * winv
    kt_ref[...] = k2 * winv
    rt_ref[...] = r * wcum
    v_ref[...] = v
    wc_ref[...] = wcum

    lane = lax.broadcasted_iota(jnp.int32, (1, LANES), 1)
    lo = lane < HEAD_DIM
    tpos = r_i % C
    ipos = c_i % C
    strict = tpos > ipos
    incl = tpos >= ipos
    eye = r_i == c_i

    def stack2(m):
        return jnp.concatenate([jnp.where(lo, m, 0.0), jnp.where(lo, 0.0, m)], axis=0)

    def cast(m):
        return m if prec is not None else m.astype(BF16)

    def mm(p, q):
        return _mm(cast(p), cast(q), prec)

    def nt(p, q):
        return _nt(cast(p), cast(q), prec)

    def chunks(cc, carry):
        units = [(ci, p) for ci in range(CPI) for p in range(npair)]
        r0 = [pl.multiple_of((cc * CPI + ci) * C, C) for ci in range(CPI)]
        ld = lambda ref, ci, p: stack2(ref[pl.ds(r0[ci], C), LANES * p:LANES * (p + 1)])
        a2 = {u: ld(at_ref, *u) for u in units}
        r2 = {u: ld(rt_ref, *u) for u in units}
        b2 = {u: ld(bt_ref, *u) for u in units}
        k2s = {u: ld(kt_ref, *u) for u in units}
        v2 = {u: ld(v_ref, *u) for u in units}
        wl = {(ci, p): wc_ref[pl.ds(pl.multiple_of(r0[ci] + C - 8, 8), 8), LANES * p:LANES * (p + 1)][7:8, :]
              for (ci, p) in units}
        ar = {u: jnp.concatenate([a2[u], r2[u]], axis=0) for u in units}
        mb = {u: nt(ar[u], b2[u]) for u in units}
        mk = {u: nt(ar[u], k2s[u]) for u in units}
        lab = {u: jnp.where(strict, mb[u][0:LANES], 0.0) for u in units}
        mrb = {u: jnp.where(incl, mb[u][LANES:], 0.0) for u in units}
        lak = {u: jnp.where(strict, mk[u][0:LANES], 0.0) for u in units}
        mrk = {u: jnp.where(incl, mk[u][LANES:], 0.0) for u in units}
        xx = {u: jnp.concatenate([a2[u], mm(lak[u], v2[u])], axis=1) for u in units}
        lp = lab
        for it in range(6):
            xx = {u: xx[u] + mm(lp[u], xx[u]) for u in units}
            if it < 5:
                lp = {u: mm(lp[u], lp[u]) for u in units}
        mq = {u: mm(mrb[u], xx[u]) for u in units}
        mv = {u: mm(mrk[u], v2[u]) for u in units}
        bx = {u: mm((b2[u] * wl[u]).T, xx[u]) for u in units}
        kv = {u: mm((k2s[u] * wl[u]).T, v2[u]) for u in units}
        st = [st_ref[p] for p in range(npair)]
        for (ci, p) in units:
            u = (ci, p)
            q1 = r2[u] + mq[u][:, 0:LANES]
            q2 = mq[u][:, LANES:] + mv[u]
            gmat = jnp.where(eye, jnp.broadcast_to(wl[u], (LANES, LANES)), 0.0) + bx[u][:, 0:LANES]
            hmat = bx[u][:, LANES:] + kv[u]
            gs = mm(jnp.concatenate([gmat, q1], axis=0), st[p])
            st[p] = gs[0:LANES] + hmat
            yy = gs[LANES:] + q2
            y_ref[pl.ds(r0[ci], C), LANES * p:LANES * (p + 1)] = yy[0:C] + yy[C:]
        for p in range(npair):
            st_ref[p] = st[p]
        return carry

    lax.fori_loop(0, tt // (C * CPI), chunks, 0)

    y = y_ref[...]
    inv = 1.0 / HEAD_DIM
    mean = _seg_sum(y, seg, npair) * inv
    yc = y - mean
    var = _seg_sum(yc * yc, seg, npair) * inv
    yn = yc * lax.rsqrt(var + RWKV_LN_EPS) * lng_ref[...] + lnb_ref[...]
    o_ref[...] = ((yn + bon_ref[...]) * g_ref[...]).astype(o_ref.dtype)


def _rwkv(rcols, mu, w0, w2, a0, a2, g2, k_k, k_a, r_k, ln_g, ln_b, B, S, prec=HIGHEST):
    T = B * S
    W = RWKV_WIDTH
    tt = 512 if S % 512 == 0 else S
    nt_ = S // tt
    npair = RWKV_HEADS // 2
    pad = LANES - DECAY_LORA
    mu_p = jnp.concatenate([mu[:3 * W], mu[3 * W:3 * W + DECAY_LORA], jnp.zeros((pad,), F32),
                            mu[3 * W + DECAY_LORA:3 * W + DECAY_LORA + AAA_LORA], jnp.zeros((pad,), F32),
                            mu[3 * W + DECAY_LORA + AAA_LORA:]]).reshape(1, RW_PAD_COLS)
    w2p = jnp.concatenate([w2, jnp.zeros((pad, W), F32)], axis=0)
    a2p = jnp.concatenate([a2, jnp.zeros((pad, W), F32)], axis=0)
    vec = lambda t: t.reshape(1, W).astype(F32)
    full = lambda shape: pl.BlockSpec(shape, lambda b, i: (0,) * len(shape))
    sc = lambda: pltpu.VMEM((tt, W), F32)
    return pl.pallas_call(
        functools.partial(_rwkv_kernel, tt=tt, prec=prec),
        out_shape=jax.ShapeDtypeStruct((T, W), BF16),
        grid=(B, nt_),
        in_specs=[
            pl.BlockSpec((tt, RW_PAD_COLS), lambda b, i: (b * nt_ + i, 0)),
            full((1, RW_PAD_COLS)), full((1, W)), full((LANES, W)), full((1, W)), full((LANES, W)),
            full((GATE_LORA, W)), full((1, W)), full((1, W)), full((1, W)), full((1, W)), full((1, W)),
        ],
        out_specs=pl.BlockSpec((tt, W), lambda b, i: (b * nt_ + i, 0)),
        scratch_shapes=[
            pltpu.VMEM((1, RW_PAD_COLS), F32),
            pltpu.VMEM((npair, LANES, LANES), F32),
            sc(), sc(), sc(), sc(), sc(), sc(), sc(), sc(), sc(),
        ],
        compiler_params=_params(("parallel", "arbitrary")),
        name="rwkv7",
    )(rcols, mu_p, vec(w0), w2p, vec(a0), a2p, g2, vec(k_k), vec(k_a), vec(r_k), vec(ln_g), vec(ln_b))


def _outproj_kernel(ya_ref, yb_ref, yc_ref, x_ref, w_ref, gm_ref, g_ref, sh_ref, sc_ref, xo_ref, ho_ref):
    o1 = DIFF_WIDTH
    o2 = o1 + RWKV_WIDTH
    mix = (_mm(ya_ref[...], w_ref[0:o1, :]) + _mm(yb_ref[...], w_ref[o1:o2, :])
           + _mm(yc_ref[...], w_ref[o2:, :]))
    xn = x_ref[...] + gm_ref[...] * mix
    xo_ref[...] = xn
    ms = jnp.mean(xn * xn, axis=-1, keepdims=True)
    y = xn * lax.rsqrt(ms + RMS_EPS) * g_ref[...]
    ho_ref[...] = y * (1.0 + sc_ref[...]) + sh_ref[...]


def _out_proj(ya, yb, yc, x2, w_out, g, mod4, S):
    T, D = x2.shape
    tm = 512 if S % 512 == 0 else S
    nb = S // tm
    row = lambda i: (i, 0)
    modspec = lambda which: pl.BlockSpec((None, None, 1, D), lambda i: (i // nb, which, 0, 0))
    return pl.pallas_call(
        _outproj_kernel,
        out_shape=(jax.ShapeDtypeStruct((T, D), F32), jax.ShapeDtypeStruct((T, D), F32)),
        grid=(T // tm,),
        in_specs=[
            pl.BlockSpec((tm, DIFF_WIDTH), row), pl.BlockSpec((tm, RWKV_WIDTH), row), pl.BlockSpec((tm, FOX_WIDTH), row),
            pl.BlockSpec((tm, D), row),
            pl.BlockSpec((D, D), lambda i: (0, 0)),
            modspec(2),
            pl.BlockSpec((1, D), lambda i: (0, 0)),
            modspec(3), modspec(4),
        ],
        out_specs=(pl.BlockSpec((tm, D), row), pl.BlockSpec((tm, D), row)),
        compiler_params=_params(("parallel",)),
        name="out_proj",
    )(ya, yb, yc, x2, w_out.astype(BF16), mod4, g.reshape(1, D), mod4, mod4)


def _top16(s, iota_f, n):
    vals, poss = [], []
    for _ in range(PEER_TOPK):
        m = jnp.max(s, axis=0, keepdims=True)
        pos = jnp.min(jnp.where(s == m, iota_f, float(n)), axis=0, keepdims=True)
        vals.append(m)
        poss.append(pos)
        s = jnp.where(iota_f == pos, -jnp.inf, s)
    return jnp.concatenate(vals, axis=0), jnp.concatenate(poss, axis=0)


ROUTE_UNROLL = 8
PEER_NCAND = 56


def _peer_cand_tables():
    K = PEER_TOPK
    pairs = [(a, b) for a in range(K) for b in range(K) if (a + 1) * (b + 1) <= K]
    n = PEER_NCAND
    p0 = [[0.0] * K for _ in range(n)]
    p1 = [[0.0] * K for _ in range(n)]
    pad = [0.0] * n
    pos = [float(K * K + r) for r in range(n)]
    for r, (a, b) in enumerate(pairs):
        p0[r][a] = 1.0
        p1[r][b] = 1.0
        pos[r] = float(a * K + b)
    for r in range(len(pairs), n):
        pad[r] = -float("inf")
    col = lambda v: jnp.broadcast_to(jnp.asarray(v, F32)[:, None], (n, LANES))
    return jnp.asarray(p0, F32), jnp.asarray(p1, F32), col(pad), col(pos)


def _peer_route_kernel(h_ref, wq_ref, sk_ref, p0_ref, p1_ref, cpad_ref, cpos_ref, idx_ref, gate_ref, q_scr, e_scr,
                       g_scr):
    K = PEER_TOPK
    hb = h_ref[...].astype(BF16)
    q = _mm(hb, wq_ref[...])
    for hc in range(2 * PEER_HEADS):
        q_scr[hc] = q[:, LANES * hc:LANES * (hc + 1)].astype(BF16)
    iota_n = lax.broadcasted_iota(jnp.int32, (PEER_NKEYS, LANES), 0).astype(F32)
    cpos = cpos_ref[...]

    def one_head(h):
        sv0, si0 = _top16(_nt(sk_ref[2 * h], q_scr[2 * h]), iota_n, PEER_NKEYS)
        sv1, si1 = _top16(_nt(sk_ref[2 * h + 1], q_scr[2 * h + 1]), iota_n, PEER_NKEYS)
        cand = _mm(p0_ref[...], sv0, HIGHEST) + _mm(p1_ref[...], sv1, HIGHEST) + cpad_ref[...]
        cidx = _mm(p0_ref[...], si0) * float(PEER_NKEYS) + _mm(p1_ref[...], si1)
        fv, es = [], []
        for _ in range(K):
            m = jnp.max(cand, axis=0, keepdims=True)
            pos = jnp.min(jnp.where(cand == m, cpos, float(2 * K * K)), axis=0, keepdims=True)
            hit = cpos == pos
            fv.append(m)
            es.append(jnp.max(jnp.where(hit, cidx, -1.0), axis=0, keepdims=True))
            cand = jnp.where(hit, -jnp.inf, cand)
        fv = jnp.concatenate(fv, axis=0)
        ex = jnp.exp(fv - fv[0:1, :])
        g_scr[h] = ex / jnp.sum(ex, axis=0, keepdims=True)
        e_scr[h] = jnp.concatenate(es, axis=0)

    def heads(hh, carry):
        for j in range(ROUTE_UNROLL):
            one_head(hh * ROUTE_UNROLL + j)
        return carry

    lax.fori_loop(0, PEER_HEADS // ROUTE_UNROLL, heads, 0)
    e = e_scr[...].reshape(PEER_HEADS * K, LANES)
    idx_ref[...] = e.T.astype(jnp.int32)
    gate_ref[...] = g_scr[...].reshape(PEER_HEADS * K, LANES)


def _peer_route(h2, wq, subkeys, tok0=0, ntok=None):
    T, D = h2.shape
    T = T - tok0 if ntok is None else ntok
    tm = LANES
    blk0 = tok0 // tm
    nq = 2 * PEER_HEADS
    sk = subkeys.reshape(nq, PEER_NKEYS, PEER_HALF).astype(BF16)
    p0, p1, cpad, cpos = _peer_cand_tables()
    const = lambda shape: pl.BlockSpec(shape, lambda i: (0,) * len(shape))
    return pl.pallas_call(
        _peer_route_kernel,
        out_shape=(jax.ShapeDtypeStruct((T, PEER_HEADS * PEER_TOPK), jnp.int32),
                   jax.ShapeDtypeStruct((T // tm, PEER_HEADS * PEER_TOPK, tm), F32)),
        grid=(T // tm,),
        in_specs=[
            pl.BlockSpec((tm, D), lambda i: (blk0 + i, 0)),
            const((D, nq * PEER_HALF)),
            const((nq, PEER_NKEYS, PEER_HALF)),
            const((PEER_NCAND, PEER_TOPK)), const((PEER_NCAND, PEER_TOPK)),
            const((PEER_NCAND, LANES)), const((PEER_NCAND, LANES)),
        ],
        out_specs=(pl.BlockSpec((tm, PEER_HEADS * PEER_TOPK), lambda i: (i, 0)),
                   pl.BlockSpec((None, PEER_HEADS * PEER_TOPK, tm), lambda i: (i, 0, 0))),
        scratch_shapes=[
            pltpu.VMEM((nq, tm, PEER_HALF), BF16),
            pltpu.VMEM((PEER_HEADS, PEER_TOPK, tm), F32),
            pltpu.VMEM((PEER_HEADS, PEER_TOPK, tm), F32),
        ],
        compiler_params=_params(("parallel",)),
        name="peer_route",
    )(h2, wq.astype(BF16), sk, p0, p1, cpad, cpos)


PEER_G = 16
PEER_SLOTS = PEER_HEADS * PEER_TOPK


def _peer_eval_kernel(idx_ref, idxn_ref, gate_ref, h_ref, x_ref, gf_ref, fg_ref, uv_ref, o_ref, buf, sem, *, final):
    G = PEER_G
    R = G * PEER_SLOTS
    D = D_MODEL
    tiles = PEER_SLOTS // SUBLANES
    i = pl.program_id(0)
    n = pl.num_programs(0)

    def start(ids, off, s, t, u):
        pltpu.make_async_copy(uv_ref.at[ids[off + t * SUBLANES + u]], buf.at[s, t, pl.ds(u, 1), :],
                              sem.at[s]).start(priority=u % 2)

    def wait(s):
        pltpu.make_async_copy(buf.at[s], buf.at[s], sem.at[s]).wait()

    @pl.when(i == 0)
    def _():
        def body(t, carry):
            for u in range(SUBLANES):
                start(idx_ref, 0, 0, t, u)
            return carry
        lax.fori_loop(0, R // SUBLANES, body, 0)

    lane = lax.broadcasted_iota(jnp.int32, (1, LANES), 1)
    tbase = (i % (LANES // (2 * G))) * (2 * G)
    gate = gate_ref[...]
    outs = []
    for s in range(2):
        wait(s)
        nxt_ids, nxt_off = (idx_ref, R) if s == 0 else (idxn_ref, 0)
        for g in range(G):
            for t in range(tiles * g, tiles * (g + 1)):
                for u in range(SUBLANES):
                    start(nxt_ids, nxt_off, 1 - s, t, u)
            w_rows = buf[s, tiles * g:tiles * (g + 1)].reshape(PEER_SLOTS, D)
            u_rows = lax.bitcast_convert_type(w_rows & jnp.uint32(0xFFFF0000), F32)
            prod = u_rows * h_ref[G * s + g:G * s + g + 1, :]
            part = prod[:, 0:LANES]
            for c in range(1, D // LANES):
                part = part + prod[:, LANES * c:LANES * (c + 1)]
            act = jnp.sum(part, axis=1, keepdims=True)
            gcol = jnp.sum(jnp.where(lane == tbase + G * s + g, gate, 0.0), axis=1, keepdims=True)
            coef = gcol * (0.5 * act * (1.0 + lax.erf(act * (2.0 ** -0.5))))
            v_rows = lax.bitcast_convert_type(w_rows << 16, F32)
            outs.append(jnp.sum(v_rows * coef, axis=0, keepdims=True))
    xn = x_ref[...] + gf_ref[...] * jnp.concatenate(outs, axis=0)
    if final:
        ms = jnp.mean(xn * xn, axis=-1, keepdims=True)
        xn = xn * lax.rsqrt(ms + RMS_EPS) * fg_ref[...]
    o_ref[...] = xn

    @pl.when(i == n - 1)
    def _():
        wait(0)


def _peer_eval(eidx, gate_t, h2, x2, mod4, final_g, uv, S, final, ntok=None):
    T, D = x2.shape
    T = T if ntok is None else ntok
    G = PEER_G
    R = G * PEER_SLOTS
    n = T // (2 * G)
    return pl.pallas_call(
        functools.partial(_peer_eval_kernel, final=final),
        out_shape=jax.ShapeDtypeStruct(x2.shape, F32),
        input_output_aliases={4: 0},
        grid=(n,),
        in_specs=[
            pl.BlockSpec((2 * R,), lambda i: (i,), memory_space=pltpu.SMEM),
            pl.BlockSpec((R,), lambda i: (jnp.minimum(2 * i + 2, 2 * n - 2),), memory_space=pltpu.SMEM),
            pl.BlockSpec((None, PEER_SLOTS, LANES), lambda i: (i // (LANES // (2 * G)), 0, 0)),
            pl.BlockSpec((2 * G, D), lambda i: (i, 0)),
            pl.BlockSpec((2 * G, D), lambda i: (i, 0)),
            pl.BlockSpec((None, None, 1, D), lambda i: (i // (S // (2 * G)), 5, 0, 0)),
            pl.BlockSpec((1, D), lambda i: (0, 0)),
            pl.BlockSpec(memory_space=pl.ANY),
        ],
        out_specs=pl.BlockSpec((2 * G, D), lambda i: (i, 0)),
        scratch_shapes=[pltpu.VMEM((2, R // SUBLANES, SUBLANES, D), jnp.uint32), pltpu.SemaphoreType.DMA((2,))],
        compiler_params=_params(("arbitrary",)),
        name="peer_eval",
    )(eidx.reshape(-1), eidx.reshape(-1), gate_t, h2, x2, mod4, final_g.reshape(1, D), uv)


SC_WORKERS = 32
SC_WINDOW = 128
SC_ROWS = 32
PEER_GD = 16
SC_SHARE_NUM, SC_SHARE_DEN = 5, 8
SC_PARTS = 2
BATCH_GROUPS = 1


def _sc_gather(tab, idx):
    n = idx.shape[0]
    width = tab.shape[1]
    per = n // SC_WORKERS
    nsub = SC_WINDOW // SC_ROWS
    mesh = plsc.VectorSubcoreMesh(core_axis_name="core", subcore_axis_name="subcore")

    @pl.kernel(out_type=jax.ShapeDtypeStruct((n, width), tab.dtype), mesh=mesh,
               scratch_types=[pltpu.VMEM((SC_WINDOW,), jnp.int32), pltpu.VMEM((2, SC_ROWS, width), tab.dtype),
                              pltpu.SemaphoreType.DMA((2,)), pltpu.SemaphoreType.DMA((2,))])
    def gather(x_hbm, i_hbm, o_hbm, idx_v, rows_v, gsem, wsem):
        wid = lax.axis_index("core") * (SC_WORKERS // 2) + lax.axis_index("subcore")

        def start_gather(k):
            return pltpu.async_copy(x_hbm.at[idx_v.at[pl.ds(k * SC_ROWS, SC_ROWS)]], rows_v.at[k % 2], gsem.at[k % 2])

        @pl.loop(0, per // SC_WINDOW)
        def _(w):
            base = wid * per + w * SC_WINDOW
            pltpu.sync_copy(i_hbm.at[pl.ds(base, SC_WINDOW)], idx_v)
            gat = [start_gather(0)] + [None] * (nsub - 1)
            wrt = [None] * nsub
            for k in range(nsub):
                if k + 1 < nsub:
                    if k >= 1:
                        wrt[k - 1].wait()
                    gat[k + 1] = start_gather(k + 1)
                gat[k].wait()
                wrt[k] = pltpu.async_copy(rows_v.at[k % 2], o_hbm.at[pl.ds(base + k * SC_ROWS, SC_ROWS)],
                                          wsem.at[k % 2])
            for k in range(max(nsub - 2, 0), nsub):
                wrt[k].wait()

    return gather(tab, idx)


def _peer_dense_kernel(rows_ref, gate_ref, h_ref, x_ref, gf_ref, fg_ref, o_ref, *, final):
    G = PEER_GD
    D = D_MODEL
    i = pl.program_id(0)
    lane = lax.broadcasted_iota(jnp.int32, (1, LANES), 1)
    tbase = (i % (LANES // G)) * G
    gate = gate_ref[...]
    outs = []
    for g in range(G):
        w_rows = rows_ref[PEER_SLOTS * g:PEER_SLOTS * (g + 1), :]
        u_rows = lax.bitcast_convert_type(w_rows & jnp.uint32(0xFFFF0000), F32)
        prod = u_rows * h_ref[g:g + 1, :]
        part = prod[:, 0:LANES]
        for c in range(1, D // LANES):
            part = part + prod[:, LANES * c:LANES * (c + 1)]
        act = jnp.sum(part, axis=1, keepdims=True)
        gcol = jnp.sum(jnp.where(lane == tbase + g, gate, 0.0), axis=1, keepdims=True)
        coef = gcol * (0.5 * act * (1.0 + lax.erf(act * (2.0 ** -0.5))))
        v_rows = lax.bitcast_convert_type(w_rows << 16, F32)
        outs.append(jnp.sum(v_rows * coef, axis=0, keepdims=True))
    xn = x_ref[...] + gf_ref[...] * jnp.concatenate(outs, axis=0)
    if final:
        ms = jnp.mean(xn * xn, axis=-1, keepdims=True)
        xn = xn * lax.rsqrt(ms + RMS_EPS) * fg_ref[...]
    o_ref[...] = xn


def _peer_eval_dense(rows, gate_t, h2, x2, mod4, final_g, S, final, tok0):
    T, D = x2.shape
    G = PEER_GD
    tb = rows.shape[0] // PEER_SLOTS
    assert tok0 % LANES == 0 and tb % LANES == 0
    blk0 = tok0 // G
    return pl.pallas_call(
        functools.partial(_peer_dense_kernel, final=final),
        out_shape=jax.ShapeDtypeStruct(x2.shape, F32),
        input_output_aliases={3: 0},
        grid=(tb // G,),
        in_specs=[
            pl.BlockSpec((G * PEER_SLOTS, D), lambda i: (i, 0)),
            pl.BlockSpec((None, PEER_SLOTS, LANES), lambda i: (i // (LANES // G), 0, 0)),
            pl.BlockSpec((G, D), lambda i: (blk0 + i, 0)),
            pl.BlockSpec((G, D), lambda i: (blk0 + i, 0)),
            pl.BlockSpec((None, None, 1, D), lambda i: ((blk0 + i) // (S // G), 5, 0, 0)),
            pl.BlockSpec((1, D), lambda i: (0, 0)),
        ],
        out_specs=pl.BlockSpec((G, D), lambda i: (blk0 + i, 0)),
        compiler_params=_params(("parallel",)),
        name="peer_dense",
    )(rows, gate_t, h2, x2, mod4, final_g.reshape(1, D))


def _pack_uv(u, v):
    hi = lax.bitcast_convert_type(u.astype(BF16), jnp.uint16).astype(jnp.uint32) << 16
    lo = lax.bitcast_convert_type(v.astype(BF16), jnp.uint16).astype(jnp.uint32)
    return (hi | lo).reshape(u.shape[0], 1, u.shape[1])


def kernel(x, c, norm_mix_g, norm_ffn_g, final_norm_g, ada_w, ada_b, w_in, w_out, dif_lam, dif_subln_g, rw_mu, rw_w0,
           rw_w2, rw_a0, rw_a2, rw_g2, rw_kk, rw_ka, rw_rk, rw_ln_g, rw_ln_b, fox_bf, peer_wq, peer_subkeys, peer_u,
           peer_v):
    B, S, D = x.shape
    depth = ada_w.shape[0]
    mod = _ada_mod(c, ada_w, ada_b)
    ngrp = BATCH_GROUPS if B % BATCH_GROUPS == 0 else 1
    bg = B // ngrp
    tg = bg * S
    prep = [(_pad_w_in(w_in[l]), _pack_uv(peer_u[l], peer_v[l])) for l in range(depth)]

    def mix_route(l, g, x2):
        mod4 = mod[l, g * bg:(g + 1) * bg].reshape(bg, 6, 1, D)
        (w_pad, w_vt), uv = prep[l]
        dqk, rcols, fqk, fl, vtd, vtf = _in_proj(x2, norm_mix_g[l], mod4, w_pad, w_vt, S)
        ya = _diff_attention(dqk, vtd, dif_lam[l], dif_subln_g[l], l, bg, S)
        yb = _rwkv(rcols, rw_mu[l], rw_w0[l], rw_w2[l], rw_a0[l], rw_a2[l], rw_g2[l], rw_kk[l], rw_ka[l],
                   rw_rk[l].reshape(-1), rw_ln_g[l], rw_ln_b[l], bg, S, prec=None)
        crep, crow = _fox_cum(fl, fox_bf[l], bg, S)
        yc = _fox_attention(fqk, vtf, crep, crow, bg, S)
        x2, h2 = _out_proj(ya, yb, yc, x2, w_out[l], norm_ffn_g[l], mod4, S)
        unit = SC_PARTS * LANES
        tb = (tg * SC_SHARE_NUM // SC_SHARE_DEN) // unit * unit
        ta = tg - tb
        bounds = [ta + tb * j // SC_PARTS for j in range(SC_PARTS + 1)]
        parts = []
        for t0, t1 in zip(bounds[:-1], bounds[1:]):
            if t1 == t0:
                continue
            e_p, g_p = _peer_route(h2, peer_wq[l], peer_subkeys[l], t0, t1 - t0)
            parts.append((_sc_gather(uv.reshape(-1, D), e_p.reshape(-1)), g_p, t0))
        e_a, g_a = _peer_route(h2, peer_wq[l], peer_subkeys[l], 0, ta)
        return dict(x2=x2, h2=h2, mod4=mod4, uv=uv, ta=ta, e_a=e_a, g_a=g_a, parts=parts)

    def evaluate(l, st):
        final = l == depth - 1
        x2 = _peer_eval(st["e_a"], st["g_a"], st["h2"], st["x2"], st["mod4"], final_norm_g, st["uv"], S, final,
                        ntok=st["ta"])
        for r, g_p, t0 in st["parts"]:
            x2 = _peer_eval_dense(r, g_p, st["h2"], x2, st["mod4"], final_norm_g, S, final, t0)
        return x2

    xs = [x[g * bg:(g + 1) * bg].reshape(tg, D) for g in range(ngrp)]
    pending = []
    for l in range(depth):
        for g in range(ngrp):
            pending.append((l, g, mix_route(l, g, xs[g])))
            if len(pending) == ngrp:
                l0, g0, st = pending.pop(0)
                xs[g0] = evaluate(l0, st)
    for l0, g0, st in pending:
        xs[g0] = evaluate(l0, st)
    return jnp.concatenate(xs, axis=0).reshape(B, S, D)
```

```python
import functools
import math

import jax
import jax.numpy as jnp
from jax import lax
from jax.experimental import pallas as pl
from jax.experimental.pallas import tpu as pltpu
from jax.experimental.pallas import tpu_sc as plsc

F32 = jnp.float32
BF16 = jnp.bfloat16
HIGHEST = lax.Precision.HIGHEST

D_MODEL = 1024
HEAD_DIM = 64
DIFF_HEADS = 6
DIFF_QK_DIM = HEAD_DIM // 2
RWKV_HEADS = 6
FOX_HEADS = 4
DIFF_WIDTH = DIFF_HEADS * HEAD_DIM
RWKV_WIDTH = RWKV_HEADS * HEAD_DIM
FOX_WIDTH = FOX_HEADS * HEAD_DIM
DECAY_LORA = 64
AAA_LORA = 64
GATE_LORA = 128
DIFF_COLS = 3 * DIFF_WIDTH
RWKV_COLS = 3 * RWKV_WIDTH + DECAY_LORA + AAA_LORA + GATE_LORA
PEER_HEADS = 8
PEER_NKEYS = 128
PEER_TOPK = 16
PEER_QDIM = 256
PEER_HALF = PEER_QDIM // 2
RMS_EPS = 1e-6
RWKV_LN_EPS = 64e-5

LANES = 128
SUBLANES = 8
RW_PAD_COLS = 3 * RWKV_WIDTH + 3 * LANES
VMEM_LIMIT = 56 * 1024 * 1024

RW_CHUNK = 64
RW_CHUNKS_PER_ITER = 4


def _params(sem, vmem=VMEM_LIMIT):
    return pltpu.CompilerParams(dimension_semantics=sem, vmem_limit_bytes=vmem)


def _nt(a, b, precision=None):
    return lax.dot_general(a, b, (((1,), (1,)), ((), ())), preferred_element_type=F32, precision=precision)


def _mm(a, b, precision=None):
    return jnp.dot(a, b, preferred_element_type=F32, precision=precision)


def _ada_kernel(c_ref, w_ref, b_ref, o_ref):
    c = c_ref[...]
    ca = c * jax.nn.sigmoid(c)
    o_ref[...] = _mm(ca, w_ref[...], HIGHEST) + b_ref[...]


def _ada_mod(c, ada_w, ada_b):
    L, D, N = ada_w.shape
    B = c.shape[0]
    tn = 1536
    return pl.pallas_call(
        _ada_kernel,
        out_shape=jax.ShapeDtypeStruct((L, B, N), F32),
        grid=(L, N // tn),
        in_specs=[
            pl.BlockSpec((B, D), lambda l, j: (0, 0)),
            pl.BlockSpec((None, D, tn), lambda l, j: (l, 0, j)),
            pl.BlockSpec((None, 1, tn), lambda l, j: (l, 0, j)),
        ],
        out_specs=pl.BlockSpec((None, B, tn), lambda l, j: (l, 0, j)),
        compiler_params=_params(("parallel", "parallel")),
        name="ada_mod",
    )(c, ada_w, ada_b.reshape(L, 1, N))


ATT_T = 512
QK_DIFF = 2 * DIFF_WIDTH
QK_FOX = 2 * FOX_WIDTH
VT_ROWS = DIFF_WIDTH + FOX_WIDTH
IN_PAD_COLS = QK_DIFF + RW_PAD_COLS + QK_FOX + LANES


def _inproj_kernel(x_ref, g_ref, sh_ref, sc_ref, w_ref, wvt_ref, d_ref, r_ref, f_ref, fl_ref, vtd_ref, vtf_ref, *, ta):
    x = x_ref[...]
    ms = jnp.mean(x * x, axis=-1, keepdims=True)
    y = x * lax.rsqrt(ms + RMS_EPS) * g_ref[...]
    h = (y * (1.0 + sc_ref[...]) + sh_ref[...]).astype(BF16)
    o1 = QK_DIFF
    o2 = o1 + RW_PAD_COLS
    o3 = o2 + QK_FOX
    d_ref[...] = _mm(h, w_ref[:, 0:o1]).astype(BF16)
    r_ref[...] = _mm(h, w_ref[:, o1:o2])
    f_ref[...] = _mm(h, w_ref[:, o2:o3]).astype(BF16)
    fl_ref[...] = _mm(h, w_ref[:, o3:o3 + LANES])
    vt = _nt(wvt_ref[...], h).astype(BF16)
    for s in range(x.shape[0] // ta):
        vtd_ref[s] = vt[0:DIFF_WIDTH, ta * s:ta * (s + 1)]
        vtf_ref[s] = vt[DIFF_WIDTH:, ta * s:ta * (s + 1)]


def _in_proj(x2, g, mod4, w_pad, w_vt, S):
    T, D = x2.shape
    tm = 512 if S % 512 == 0 else S
    ta = min(ATT_T, S)
    nb = S // tm
    row = lambda i: (i, 0)
    return pl.pallas_call(
        functools.partial(_inproj_kernel, ta=ta),
        out_shape=(
            jax.ShapeDtypeStruct((T, QK_DIFF), BF16),
            jax.ShapeDtypeStruct((T, RW_PAD_COLS), F32),
            jax.ShapeDtypeStruct((T, QK_FOX), BF16),
            jax.ShapeDtypeStruct((T, LANES), F32),
            jax.ShapeDtypeStruct((T // ta, DIFF_WIDTH, ta), BF16),
            jax.ShapeDtypeStruct((T // ta, FOX_WIDTH, ta), BF16),
        ),
        grid=(T // tm,),
        in_specs=[
            pl.BlockSpec((tm, D), row),
            pl.BlockSpec((1, D), lambda i: (0, 0)),
            pl.BlockSpec((None, None, 1, D), lambda i: (i // nb, 0, 0, 0)),
            pl.BlockSpec((None, None, 1, D), lambda i: (i // nb, 1, 0, 0)),
            pl.BlockSpec((D, IN_PAD_COLS), lambda i: (0, 0)),
            pl.BlockSpec((VT_ROWS, D), lambda i: (0, 0)),
        ],
        out_specs=(
            pl.BlockSpec((tm, QK_DIFF), row),
            pl.BlockSpec((tm, RW_PAD_COLS), row),
            pl.BlockSpec((tm, QK_FOX), row),
            pl.BlockSpec((tm, LANES), row),
            pl.BlockSpec((tm // ta, DIFF_WIDTH, ta), lambda i: (i, 0, 0)),
            pl.BlockSpec((tm // ta, FOX_WIDTH, ta), lambda i: (i, 0, 0)),
        ),
        compiler_params=_params(("parallel",)),
        name="in_proj",
    )(x2, g.reshape(1, D), mod4, mod4, w_pad, w_vt)


def _pad_w_in(w_in):
    D = w_in.shape[0]
    W = RWKV_WIDTH
    o = DIFF_COLS
    z64 = jnp.zeros((D, LANES - DECAY_LORA), w_in.dtype)
    rw = w_in[:, o:o + RWKV_COLS]
    fx = w_in[:, o + RWKV_COLS:]
    zf = jnp.zeros((D, LANES - FOX_HEADS), w_in.dtype)
    w_pad = jnp.concatenate([
        w_in[:, :QK_DIFF],
        rw[:, :3 * W], rw[:, 3 * W:3 * W + DECAY_LORA], z64,
        rw[:, 3 * W + DECAY_LORA:3 * W + DECAY_LORA + AAA_LORA], z64,
        rw[:, 3 * W + DECAY_LORA + AAA_LORA:],
        fx[:, :QK_FOX], fx[:, 3 * FOX_WIDTH:], zf,
    ], axis=1).astype(BF16)
    w_vt = jnp.concatenate([w_in[:, QK_DIFF:o], fx[:, QK_FOX:3 * FOX_WIDTH]], axis=1).T.astype(BF16)
    return w_pad, w_vt


LOG2E = math.log2(math.e)


ACC_ROWS = LANES + 16


def _scaled_q(q, c):
    return (q.astype(F32) * c).astype(BF16)


def _with_ones(vt):
    return jnp.concatenate([vt, jnp.ones((ACC_ROWS - LANES, vt.shape[1]), vt.dtype)], axis=0)


def _flash_step(s2s, vt1, m_ref, acc_ref):
    n = len(s2s)
    m_old = [m_ref[x] for x in range(n)]
    m_new = [jnp.maximum(m_old[x], jnp.max(s2s[x], axis=0, keepdims=True)) for x in range(n)]
    alpha = [jnp.exp2(m_old[x] - m_new[x]) for x in range(n)]
    p = [jnp.exp2(s2s[x] - m_new[x]).astype(BF16) for x in range(n)]
    pv = [_mm(vt1, p[x]) for x in range(n)]
    for x in range(n):
        acc_ref[x] = alpha[x] * acc_ref[x] + pv[x]
        m_ref[x] = m_new[x]


def _diff_attn_kernel(lam_ref, g_ref, q_ref, k_ref, vt_ref, o_ref, m_ref, acc_ref, *, tq, lam_init):
    i = pl.program_id(2)
    lane = lax.broadcasted_iota(jnp.int32, (1, LANES), 1)
    q = _scaled_q(q_ref[...], (DIFF_QK_DIM ** -0.5) * LOG2E)
    zero = jnp.zeros_like(q)
    qm = [jnp.where((lane >= DIFF_QK_DIM * x) & (lane < DIFF_QK_DIM * (x + 1)), q, zero) for x in range(4)]
    m_ref[...] = jnp.full(m_ref.shape, -jnp.inf, F32)
    acc_ref[...] = jnp.zeros(acc_ref.shape, F32)

    def step(j, diag):
        k = k_ref[pl.ds(pl.multiple_of(j * tq, tq), tq), :]
        vt1 = _with_ones(vt_ref[j])
        if diag:
            keep = (lax.broadcasted_iota(jnp.int32, (tq, tq), 1) >= lax.broadcasted_iota(jnp.int32, (tq, tq), 0))
        s2s = [_nt(k, qm[x]) for x in range(4)]
        if diag:
            s2s = [jnp.where(keep, s2, -jnp.inf) for s2 in s2s]
        _flash_step(s2s, vt1, m_ref, acc_ref)

    def body(j, carry):
        step(j, False)
        return carry

    lax.fori_loop(0, i, body, 0)
    step(i, True)

    lp = lam_ref[...]
    lam = (jnp.exp(jnp.sum(lp[0:1] * lp[1:2], axis=-1, keepdims=True))
           - jnp.exp(jnp.sum(lp[2:3] * lp[3:4], axis=-1, keepdims=True)) + lam_init)
    sm = [acc_ref[x, 0:LANES, :] / acc_ref[x, LANES:LANES + 1, :] for x in range(4)]
    outs = [sm[2 * hh] - lam * sm[2 * hh + 1] for hh in range(2)]
    row = lax.broadcasted_iota(jnp.int32, (LANES, 1), 0)
    o = jnp.where(row < HEAD_DIM, outs[0], outs[1])
    sq = o * o
    ms = jnp.where(row < HEAD_DIM, jnp.sum(sq[0:HEAD_DIM], axis=0, keepdims=True),
                   jnp.sum(sq[HEAD_DIM:], axis=0, keepdims=True)) * (1.0 / HEAD_DIM)
    y = o * lax.rsqrt(ms + RMS_EPS) * g_ref[...] * (1.0 - lam_init)
    o_ref[...] = y.T.astype(o_ref.dtype)


def _diff_attention(dqk, vtd, lam_params, subln_g, layer_idx, B, S):
    T = B * S
    tq = min(ATT_T, S)
    nq = S // tq
    npair = DIFF_HEADS // 2
    lam_init = 0.8 - 0.6 * math.exp(-0.3 * layer_idx)
    g2 = jnp.concatenate([subln_g, subln_g]).reshape(LANES, 1).astype(F32)
    return pl.pallas_call(
        functools.partial(_diff_attn_kernel, tq=tq, lam_init=lam_init),
        out_shape=jax.ShapeDtypeStruct((T, DIFF_WIDTH), BF16),
        grid=(B, npair, nq),
        in_specs=[
            pl.BlockSpec((4, DIFF_QK_DIM), lambda b, p, i: (0, 0)),
            pl.BlockSpec((LANES, 1), lambda b, p, i: (0, 0)),
            pl.BlockSpec((tq, LANES), lambda b, p, i: (b * nq + i, p)),
            pl.BlockSpec((S, LANES), lambda b, p, i: (b, npair + p)),
            pl.BlockSpec((nq, LANES, tq), lambda b, p, i: (b, p, 0)),
        ],
        out_specs=pl.BlockSpec((tq, LANES), lambda b, p, i: (b * nq + i, p)),
        scratch_shapes=[
            pltpu.VMEM((4, 1, tq), F32),
            pltpu.VMEM((4, ACC_ROWS, tq), F32),
        ],
        compiler_params=_params(("parallel", "parallel", "arbitrary")),
        name="diff_attn",
    )(lam_params, g2, dqk, dqk, vtd)


def _fox_cum_kernel(f_ref, b_ref, rep_ref, row_ref, *, S, tc):
    rr = lax.broadcasted_iota(jnp.int32, (tc, tc), 0)
    cc = lax.broadcasted_iota(jnp.int32, (tc, tc), 1)
    tri = (rr >= cc).astype(F32)
    sel_r = lax.broadcasted_iota(jnp.int32, (LANES, LANES), 0)
    carry = jnp.zeros((1, LANES), F32)
    for c in range(S // tc):
        z = f_ref[c * tc:(c + 1) * tc, :] + b_ref[...]
        logf = -(jnp.maximum(-z, 0.0) + jnp.log(1.0 + jnp.exp(-jnp.abs(z))))
        cum = _mm(tri, logf, HIGHEST) + carry
        carry = cum[tc - 1:tc, :]
        row_ref[:, c * tc:(c + 1) * tc] = cum.T[0:8, :]
        for h in range(FOX_HEADS):
            rep_ref[h, c * tc:(c + 1) * tc, :] = _mm(cum, (sel_r == h).astype(F32), HIGHEST)


def _fox_cum(fl, b_f, B, S):
    tc = 256 if S % 256 == 0 else S
    bpad = jnp.zeros((1, LANES), F32).at[0, :FOX_HEADS].set(b_f.astype(F32))
    return pl.pallas_call(
        functools.partial(_fox_cum_kernel, S=S, tc=tc),
        out_shape=(jax.ShapeDtypeStruct((B, FOX_HEADS, S, LANES), F32), jax.ShapeDtypeStruct((B, 8, S), F32)),
        grid=(B,),
        in_specs=[pl.BlockSpec((S, LANES), lambda b: (b, 0)), pl.BlockSpec((1, LANES), lambda b: (0, 0))],
        out_specs=(pl.BlockSpec((None, FOX_HEADS, S, LANES), lambda b: (b, 0, 0, 0)),
                   pl.BlockSpec((None, 8, S), lambda b: (b, 0, 0))),
        compiler_params=_params(("parallel",)),
        name="fox_cum",
    )(fl, bpad)


def _fox_attn_kernel(q_ref, k_ref, vt_ref, c0_ref, c1_ref, cr_ref, o_ref, m_ref, acc_ref, *, tq):
    p_id = pl.program_id(1)
    i = pl.program_id(2)
    lane = lax.broadcasted_iota(jnp.int32, (1, LANES), 1)
    q = _scaled_q(q_ref[...], (HEAD_DIM ** -0.5) * LOG2E)
    zero = jnp.zeros_like(q)
    qm = [jnp.where((lane >= HEAD_DIM * x) & (lane < HEAD_DIM * (x + 1)), q, zero) for x in range(2)]
    ck_refs = (c0_ref, c1_ref)
    cq = [cr_ref[2 * p_id + x, pl.ds(i, 1), :] for x in range(2)]
    m_ref[...] = jnp.full(m_ref.shape, -jnp.inf, F32)
    acc_ref[...] = jnp.zeros(acc_ref.shape, F32)

    def step(j, diag):
        off = pl.multiple_of(j * tq, tq)
        k = k_ref[pl.ds(off, tq), :]
        vt1 = _with_ones(vt_ref[j])
        if diag:
            keep = (lax.broadcasted_iota(jnp.int32, (tq, tq), 1) >= lax.broadcasted_iota(jnp.int32, (tq, tq), 0))
        s2s = []
        for x in range(2):
            ck = ck_refs[x][pl.ds(off, tq), :]
            bias = (cq[x] - jnp.concatenate([ck] * (tq // LANES), axis=1)) * LOG2E
            s2s.append(_nt(k, qm[x]) + bias)
        if diag:
            s2s = [jnp.where(keep, s2, -jnp.inf) for s2 in s2s]
        _flash_step(s2s, vt1, m_ref, acc_ref)

    def body(j, carry):
        step(j, False)
        return carry

    lax.fori_loop(0, i, body, 0)
    step(i, True)
    row = lax.broadcasted_iota(jnp.int32, (LANES, 1), 0)
    sm = [acc_ref[x, 0:LANES, :] / acc_ref[x, LANES:LANES + 1, :] for x in range(2)]
    o = jnp.where(row < HEAD_DIM, sm[0], sm[1])
    o_ref[...] = o.T.astype(o_ref.dtype)


def _fox_attention(fqk, vtf, crep, crow, B, S):
    T = B * S
    tq = min(ATT_T, S)
    nq = S // tq
    npair = FOX_HEADS // 2
    crow4 = crow.reshape(B, 8, nq, tq)
    rep = lambda x: pl.BlockSpec((None, None, S, LANES), lambda b, p, i: (b, 2 * p + x, 0, 0))
    return pl.pallas_call(
        functools.partial(_fox_attn_kernel, tq=tq),
        out_shape=jax.ShapeDtypeStruct((T, FOX_WIDTH), BF16),
        grid=(B, npair, nq),
        in_specs=[
            pl.BlockSpec((tq, LANES), lambda b, p, i: (b * nq + i, p)),
            pl.BlockSpec((S, LANES), lambda b, p, i: (b, npair + p)),
            pl.BlockSpec((nq, LANES, tq), lambda b, p, i: (b, p, 0)),
            rep(0), rep(1),
            pl.BlockSpec((None, 8, nq, tq), lambda b, p, i: (b, 0, 0, 0)),
        ],
        out_specs=pl.BlockSpec((tq, LANES), lambda b, p, i: (b * nq + i, p)),
        scratch_shapes=[
            pltpu.VMEM((2, 1, tq), F32),
            pltpu.VMEM((2, ACC_ROWS, tq), F32),
        ],
        compiler_params=_params(("parallel", "parallel", "arbitrary")),
        name="fox_attn",
    )(fqk, fqk, vtf, crep, crep, crow4)


def _split3(x):
    hi = x.astype(BF16)
    r1 = x - hi.astype(F32)
    mid = r1.astype(BF16)
    lo = (r1 - mid.astype(F32)).astype(BF16)
    return hi, mid, lo


def _seg_sum(x, seg, npair):
    parts = _split3(x)
    return jnp.concatenate(
        [sum(_mm(t[:, LANES * p:LANES * (p + 1)], seg) for t in parts) for p in range(npair)], axis=1)


def _rwkv_kernel(x_ref, mu_ref, w0_ref, w2_ref, a0_ref, a2_ref, g2_ref, kk_ref, ka_ref, rk_ref, lng_ref, lnb_ref,
                 o_ref, carry_ref, st_ref, at_ref, rt_ref, bt_ref, kt_ref, v_ref, wc_ref, y_ref, g_ref, bon_ref,
                 *, tt, prec):
    i = pl.program_id(1)
    W = RWKV_WIDTH
    C = RW_CHUNK
    CPI = math.gcd(RW_CHUNKS_PER_ITER, tt // C)
    npair = RWKV_HEADS // 2

    @pl.when(i == 0)
    def _():
        carry_ref[...] = jnp.zeros(carry_ref.shape, F32)
        st_ref[...] = jnp.zeros(st_ref.shape, F32)

    x = x_ref[...]
    rows = lax.broadcasted_iota(jnp.int32, (tt, 1), 0)
    prev = jnp.where(rows == 0, carry_ref[...], pltpu.roll(x, 1, axis=0))
    carry_ref[...] = x[tt - 1:tt, :]
    xs = x + (prev - x) * mu_ref[...]
    r = xs[:, 0:W]
    k = xs[:, W:2 * W]
    v = xs[:, 2 * W:3 * W]
    xw = xs[:, 3 * W:3 * W + LANES]
    xa = xs[:, 3 * W + LANES:3 * W + 2 * LANES]
    xg = xs[:, 3 * W + 2 * LANES:]
    wl = w0_ref[...] + _mm(jnp.tanh(xw), w2_ref[...], HIGHEST)
    w = -(jnp.maximum(-wl, 0.0) + jnp.log(1.0 + jnp.exp(-jnp.abs(wl)))) - 0.5
    logdec = -jnp.exp(w)
    a = jax.nn.sigmoid(a0_ref[...] + _mm(xa, a2_ref[...], HIGHEST))
    g_ref[...] = _mm(jax.nn.sigmoid(xg), g2_ref[...], HIGHEST)

    r_i = lax.broadcasted_iota(jnp.int32, (LANES, LANES), 0)
    c_i = lax.broadcasted_iota(jnp.int32, (LANES, LANES), 1)
    seg = ((r_i // HEAD_DIM) == (c_i // HEAD_DIM)).astype(BF16)
    kkv = k * kk_ref[...]
    kkn = kkv / jnp.maximum(jnp.sqrt(_seg_sum(kkv * kkv, seg, npair)), 1e-12)
    k2 = k * (1.0 + (a - 1.0) * ka_ref[...])
    bon_ref[...] = _seg_sum(r * k2 * rk_ref[...], seg, npair) * v

    rt_i = lax.broadcasted_iota(jnp.int32, (tt, tt), 0)
    ct_i = lax.broadcasted_iota(jnp.int32, (tt, tt), 1)
    tri = (((rt_i // C) == (ct_i // C)) & (rt_i >= ct_i)).astype(BF16)
    cum = sum(_mm(tri, t) for t in _split3(logdec))
    winv = jnp.exp(-cum)
    wcum = jnp.exp(cum)
    at_ref[...] = -kkn * jnp.exp(cum - logdec)
    bt_ref[...] = kkn * a * winv
    kt_ref[...] = k2 * winv
    rt_ref[...] = r * wcum
    v_ref[...] = v
    wc_ref[...] = wcum

    lane = lax.broadcasted_iota(jnp.int32, (1, LANES), 1)
    lo = lane < HEAD_DIM
    tpos = r_i % C
    ipos = c_i % C
    strict = tpos > ipos
    incl = tpos >= ipos
    eye = r_i == c_i

    def stack2(m):
        return jnp.concatenate([jnp.where(lo, m, 0.0), jnp.where(lo, 0.0, m)], axis=0)

    def cast(m):
        return m if prec is not None else m.astype(BF16)

    def mm(p, q):
        return _mm(cast(p), cast(q), prec)

    def nt(p, q):
        return _nt(cast(p), cast(q), prec)

    def chunks(cc, carry):
        units = [(ci, p) for ci in range(CPI) for p in range(npair)]
        r0 = [pl.multiple_of((cc * CPI + ci) * C, C) for ci in range(CPI)]
        ld = lambda ref, ci, p: stack2(ref[pl.ds(r0[ci], C), LANES * p:LANES * (p + 1)])
        a2 = {u: ld(at_ref, *u) for u in units}
        r2 = {u: ld(rt_ref, *u) for u in units}
        b2 = {u: ld(bt_ref, *u) for u in units}
        k2s = {u: ld(kt_ref, *u) for u in units}
        v2 = {u: ld(v_ref, *u) for u in units}
        wl = {(ci, p): wc_ref[pl.ds(pl.multiple_of(r0[ci] + C - 8, 8), 8), LANES * p:LANES * (p + 1)][7:8, :]
              for (ci, p) in units}
        ar = {u: jnp.concatenate([a2[u], r2[u]], axis=0) for u in units}
        mb = {u: nt(ar[u], b2[u]) for u in units}
        mk = {u: nt(ar[u], k2s[u]) for u in units}
        lab = {u: jnp.where(strict, mb[u][0:LANES], 0.0) for u in units}
        mrb = {u: jnp.where(incl, mb[u][LANES:], 0.0) for u in units}
        lak = {u: jnp.where(strict, mk[u][0:LANES], 0.0) for u in units}
        mrk = {u: jnp.where(incl, mk[u][LANES:], 0.0) for u in units}
        xx = {u: jnp.concatenate([a2[u], mm(lak[u], v2[u])], axis=1) for u in units}
        lp = lab
        for it in range(6):
            xx = {u: xx[u] + mm(lp[u], xx[u]) for u in units}
            if it < 5:
                lp = {u: mm(lp[u], lp[u]) for u in units}
        mq = {u: mm(mrb[u], xx[u]) for u in units}
        mv = {u: mm(mrk[u], v2[u]) for u in units}
        bx = {u: mm((b2[u] * wl[u]).T, xx[u]) for u in units}
        kv = {u: mm((k2s[u] * wl[u]).T, v2[u]) for u in units}
        st = [st_ref[p] for p in range(npair)]
        for (ci, p) in units:
            u = (ci, p)
            q1 = r2[u] + mq[u][:, 0:LANES]
            q2 = mq[u][:, LANES:] + mv[u]
            gmat = jnp.where(eye, jnp.broadcast_to(wl[u], (LANES, LANES)), 0.0) + bx[u][:, 0:LANES]
            hmat = bx[u][:, LANES:] + kv[u]
            gs = mm(jnp.concatenate([gmat, q1], axis=0), st[p])
            st[p] = gs[0:LANES] + hmat
            yy = gs[LANES:] + q2
            y_ref[pl.ds(r0[ci], C), LANES * p:LANES * (p + 1)] = yy[0:C] + yy[C:]
        for p in range(npair):
            st_ref[p] = st[p]
        return carry

    lax.fori_loop(0, tt // (C * CPI), chunks, 0)

    y = y_ref[...]
    inv = 1.0 / HEAD_DIM
    mean = _seg_sum(y, seg, npair) * inv
    yc = y - mean
    var = _seg_sum(yc * yc, seg, npair) * inv
    yn = yc * lax.rsqrt(var + RWKV_LN_EPS) * lng_ref[...] + lnb_ref[...]
    o_ref[...] = ((yn + bon_ref[...]) * g_ref[...]).astype(o_ref.dtype)


def _rwkv(rcols, mu, w0, w2, a0, a2, g2, k_k, k_a, r_k, ln_g, ln_b, B, S, prec=HIGHEST):
    T = B * S
    W = RWKV_WIDTH
    tt = 512 if S % 512 == 0 else S
    nt_ = S // tt
    npair = RWKV_HEADS // 2
    pad = LANES - DECAY_LORA
    mu_p = jnp.concatenate([mu[:3 * W], mu[3 * W:3 * W + DECAY_LORA], jnp.zeros((pad,), F32),
                            mu[3 * W + DECAY_LORA:3 * W + DECAY_LORA + AAA_LORA], jnp.zeros((pad,), F32),
                            mu[3 * W + DECAY_LORA + AAA_LORA:]]).reshape(1, RW_PAD_COLS)
    w2p = jnp.concatenate([w2, jnp.zeros((pad, W), F32)], axis=0)
    a2p = jnp.concatenate([a2, jnp.zeros((pad, W), F32)], axis=0)
    vec = lambda t: t.reshape(1, W).astype(F32)
    full = lambda shape: pl.BlockSpec(shape, lambda b, i: (0,) * len(shape))
    sc = lambda: pltpu.VMEM((tt, W), F32)
    return pl.pallas_call(
        functools.partial(_rwkv_kernel, tt=tt, prec=prec),
        out_shape=jax.ShapeDtypeStruct((T, W), BF16),
        grid=(B, nt_),
        in_specs=[
            pl.BlockSpec((tt, RW_PAD_COLS), lambda b, i: (b * nt_ + i, 0)),
            full((1, RW_PAD_COLS)), full((1, W)), full((LANES, W)), full((1, W)), full((LANES, W)),
            full((GATE_LORA, W)), full((1, W)), full((1, W)), full((1, W)), full((1, W)), full((1, W)),
        ],
        out_specs=pl.BlockSpec((tt, W), lambda b, i: (b * nt_ + i, 0)),
        scratch_shapes=[
            pltpu.VMEM((1, RW_PAD_COLS), F32),
            pltpu.VMEM((npair, LANES, LANES), F32),
            sc(), sc(), sc(), sc(), sc(), sc(), sc(), sc(), sc(),
        ],
        compiler_params=_params(("parallel", "arbitrary")),
        name="rwkv7",
    )(rcols, mu_p, vec(w0), w2p, vec(a0), a2p, g2, vec(k_k), vec(k_a), vec(r_k), vec(ln_g), vec(ln_b))


def _outproj_kernel(ya_ref, yb_ref, yc_ref, x_ref, w_ref, gm_ref, g_ref, sh_ref, sc_ref, xo_ref, ho_ref):
    o1 = DIFF_WIDTH
    o2 = o1 + RWKV_WIDTH
    mix = (_mm(ya_ref[...], w_ref[0:o1, :]) + _mm(yb_ref[...], w_ref[o1:o2, :])
           + _mm(yc_ref[...], w_ref[o2:, :]))
    xn = x_ref[...] + gm_ref[...] * mix
    xo_ref[...] = xn
    ms = jnp.mean(xn * xn, axis=-1, keepdims=True)
    y = xn * lax.rsqrt(ms + RMS_EPS) * g_ref[...]
    ho_ref[...] = y * (1.0 + sc_ref[...]) + sh_ref[...]


def _out_proj(ya, yb, yc, x2, w_out, g, mod4, S):
    T, D = x2.shape
    tm = 512 if S % 512 == 0 else S
    nb = S // tm
    row = lambda i: (i, 0)
    modspec = lambda which: pl.BlockSpec((None, None, 1, D), lambda i: (i // nb, which, 0, 0))
    return pl.pallas_call(
        _outproj_kernel,
        out_shape=(jax.ShapeDtypeStruct((T, D), F32), jax.ShapeDtypeStruct((T, D), F32)),
        grid=(T // tm,),
        in_specs=[
            pl.BlockSpec((tm, DIFF_WIDTH), row), pl.BlockSpec((tm, RWKV_WIDTH), row), pl.BlockSpec((tm, FOX_WIDTH), row),
            pl.BlockSpec((tm, D), row),
            pl.BlockSpec((D, D), lambda i: (0, 0)),
            modspec(2),
            pl.BlockSpec((1, D), lambda i: (0, 0)),
            modspec(3), modspec(4),
        ],
        out_specs=(pl.BlockSpec((tm, D), row), pl.BlockSpec((tm, D), row)),
        compiler_params=_params(("parallel",)),
        name="out_proj",
    )(ya, yb, yc, x2, w_out.astype(BF16), mod4, g.reshape(1, D), mod4, mod4)


def _top16(s, iota_f, n):
    vals, poss = [], []
    for _ in range(PEER_TOPK):
        m = jnp.max(s, axis=0, keepdims=True)
        pos = jnp.min(jnp.where(s == m, iota_f, float(n)), axis=0, keepdims=True)
        vals.append(m)
        poss.append(pos)
        s = jnp.where(iota_f == pos, -jnp.inf, s)
    return jnp.concatenate(vals, axis=0), jnp.concatenate(poss, axis=0)


ROUTE_UNROLL = 8
PEER_NCAND = 56


def _peer_cand_tables():
    K = PEER_TOPK
    pairs = [(a, b) for a in range(K) for b in range(K) if (a + 1) * (b + 1) <= K]
    n = PEER_NCAND
    p0 = [[0.0] * K for _ in range(n)]
    p1 = [[0.0] * K for _ in range(n)]
    pad = [0.0] * n
    pos = [float(K * K + r) for r in range(n)]
    for r, (a, b) in enumerate(pairs):
        p0[r][a] = 1.0
        p1[r][b] = 1.0
        pos[r] = float(a * K + b)
    for r in range(len(pairs), n):
        pad[r] = -float("inf")
    col = lambda v: jnp.broadcast_to(jnp.asarray(v, F32)[:, None], (n, LANES))
    return jnp.asarray(p0, F32), jnp.asarray(p1, F32), col(pad), col(pos)


def _peer_route_kernel(h_ref, wq_ref, sk_ref, p0_ref, p1_ref, cpad_ref, cpos_ref, idx_ref, gate_ref, q_scr, e_scr,
                       g_scr):
    K = PEER_TOPK
    hb = h_ref[...].astype(BF16)
    q = _mm(hb, wq_ref[...])
    for hc in range(2 * PEER_HEADS):
        q_scr[hc] = q[:, LANES * hc:LANES * (hc + 1)].astype(BF16)
    iota_n = lax.broadcasted_iota(jnp.int32, (PEER_NKEYS, LANES), 0).astype(F32)
    cpos = cpos_ref[...]

    def one_head(h):
        sv0, si0 = _top16(_nt(sk_ref[2 * h], q_scr[2 * h]), iota_n, PEER_NKEYS)
        sv1, si1 = _top16(_nt(sk_ref[2 * h + 1], q_scr[2 * h + 1]), iota_n, PEER_NKEYS)
        cand = _mm(p0_ref[...], sv0, HIGHEST) + _mm(p1_ref[...], sv1, HIGHEST) + cpad_ref[...]
        cidx = _mm(p0_ref[...], si0) * float(PEER_NKEYS) + _mm(p1_ref[...], si1)
        fv, es = [], []
        for _ in range(K):
            m = jnp.max(cand, axis=0, keepdims=True)
            pos = jnp.min(jnp.where(cand == m, cpos, float(2 * K * K)), axis=0, keepdims=True)
            hit = cpos == pos
            fv.append(m)
            es.append(jnp.max(jnp.where(hit, cidx, -1.0), axis=0, keepdims=True))
            cand = jnp.where(hit, -jnp.inf, cand)
        fv = jnp.concatenate(fv, axis=0)
        ex = jnp.exp(fv - fv[0:1, :])
        g_scr[h] = ex / jnp.sum(ex, axis=0, keepdims=True)
        e_scr[h] = jnp.concatenate(es, axis=0)

    def heads(hh, carry):
        for j in range(ROUTE_UNROLL):
            one_head(hh * ROUTE_UNROLL + j)
        return carry

    lax.fori_loop(0, PEER_HEADS // ROUTE_UNROLL, heads, 0)
    e = e_scr[...].reshape(PEER_HEADS * K, LANES)
    idx_ref[...] = e.T.astype(jnp.int32)
    gate_ref[...] = g_scr[...].reshape(PEER_HEADS * K, LANES)


def _peer_route(h2, wq, subkeys, tok0=0, ntok=None):
    T, D = h2.shape
    T = T - tok0 if ntok is None else ntok
    tm = LANES
    blk0 = tok0 // tm
    nq = 2 * PEER_HEADS
    sk = subkeys.reshape(nq, PEER_NKEYS, PEER_HALF).astype(BF16)
    p0, p1, cpad, cpos = _peer_cand_tables()
    const = lambda shape: pl.BlockSpec(shape, lambda i: (0,) * len(shape))
    return pl.pallas_call(
        _peer_route_kernel,
        out_shape=(jax.ShapeDtypeStruct((T, PEER_HEADS * PEER_TOPK), jnp.int32),
                   jax.ShapeDtypeStruct((T // tm, PEER_HEADS * PEER_TOPK, tm), F32)),
        grid=(T // tm,),
        in_specs=[
            pl.BlockSpec((tm, D), lambda i: (blk0 + i, 0)),
            const((D, nq * PEER_HALF)),
            const((nq, PEER_NKEYS, PEER_HALF)),
            const((PEER_NCAND, PEER_TOPK)), const((PEER_NCAND, PEER_TOPK)),
            const((PEER_NCAND, LANES)), const((PEER_NCAND, LANES)),
        ],
        out_specs=(pl.BlockSpec((tm, PEER_HEADS * PEER_TOPK), lambda i: (i, 0)),
                   pl.BlockSpec((None, PEER_HEADS * PEER_TOPK, tm), lambda i: (i, 0, 0))),
        scratch_shapes=[
            pltpu.VMEM((nq, tm, PEER_HALF), BF16),
            pltpu.VMEM((PEER_HEADS, PEER_TOPK, tm), F32),
            pltpu.VMEM((PEER_HEADS, PEER_TOPK, tm), F32),
        ],
        compiler_params=_params(("parallel",)),
        name="peer_route",
    )(h2, wq.astype(BF16), sk, p0, p1, cpad, cpos)


PEER_G = 16
PEER_SLOTS = PEER_HEADS * PEER_TOPK


def _peer_eval_kernel(idx_ref, idxn_ref, gate_ref, h_ref, x_ref, gf_ref, fg_ref, uv_ref, o_ref, buf, sem, *, final):
    G = PEER_G
    R = G * PEER_SLOTS
    D = D_MODEL
    tiles = PEER_SLOTS // SUBLANES
    i = pl.program_id(0)
    n = pl.num_programs(0)

    def start(ids, off, s, t, u):
        pltpu.make_async_copy(uv_ref.at[ids[off + t * SUBLANES + u]], buf.at[s, t, pl.ds(u, 1), :],
                              sem.at[s]).start(priority=u % 2)

    def wait(s):
        pltpu.make_async_copy(buf.at[s], buf.at[s], sem.at[s]).wait()

    @pl.when(i == 0)
    def _():
        def body(t, carry):
            for u in range(SUBLANES):
                start(idx_ref, 0, 0, t, u)
            return carry
        lax.fori_loop(0, R // SUBLANES, body, 0)

    lane = lax.broadcasted_iota(jnp.int32, (1, LANES), 1)
    tbase = (i % (LANES // (2 * G))) * (2 * G)
    gate = gate_ref[...]
    outs = []
    for s in range(2):
        wait(s)
        nxt_ids, nxt_off = (idx_ref, R) if s == 0 else (idxn_ref, 0)
        for g in range(G):
            for t in range(tiles * g, tiles * (g + 1)):
                for u in range(SUBLANES):
                    start(nxt_ids, nxt_off, 1 - s, t, u)
            w_rows = buf[s, tiles * g:tiles * (g + 1)].reshape(PEER_SLOTS, D)
            u_rows = lax.bitcast_convert_type(w_rows & jnp.uint32(0xFFFF0000), F32)
            prod = u_rows * h_ref[G * s + g:G * s + g + 1, :]
            part = prod[:, 0:LANES]
            for c in range(1, D // LANES):
                part = part + prod[:, LANES * c:LANES * (c + 1)]
            act = jnp.sum(part, axis=1, keepdims=True)
            gcol = jnp.sum(jnp.where(lane == tbase + G * s + g, gate, 0.0), axis=1, keepdims=True)
            coef = gcol * (0.5 * act * (1.0 + lax.erf(act * (2.0 ** -0.5))))
            v_rows = lax.bitcast_convert_type(w_rows << 16, F32)
            outs.append(jnp.sum(v_rows * coef, axis=0, keepdims=True))
    xn = x_ref[...] + gf_ref[...] * jnp.concatenate(outs, axis=0)
    if final:
        ms = jnp.mean(xn * xn, axis=-1, keepdims=True)
        xn = xn * lax.rsqrt(ms + RMS_EPS) * fg_ref[...]
    o_ref[...] = xn

    @pl.when(i == n - 1)
    def _():
        wait(0)


def _peer_eval(eidx, gate_t, h2, x2, mod4, final_g, uv, S, final, ntok=None):
    T, D = x2.shape
    T = T if ntok is None else ntok
    G = PEER_G
    R = G * PEER_SLOTS
    n = T // (2 * G)
    return pl.pallas_call(
        functools.partial(_peer_eval_kernel, final=final),
        out_shape=jax.ShapeDtypeStruct(x2.shape, F32),
        input_output_aliases={4: 0},
        grid=(n,),
        in_specs=[
            pl.BlockSpec((2 * R,), lambda i: (i,), memory_space=pltpu.SMEM),
            pl.BlockSpec((R,), lambda i: (jnp.minimum(2 * i + 2, 2 * n - 2),), memory_space=pltpu.SMEM),
            pl.BlockSpec((None, PEER_SLOTS, LANES), lambda i: (i // (LANES // (2 * G)), 0, 0)),
            pl.BlockSpec((2 * G, D), lambda i: (i, 0)),
            pl.BlockSpec((2 * G, D), lambda i: (i, 0)),
            pl.BlockSpec((None, None, 1, D), lambda i: (i // (S // (2 * G)), 5, 0, 0)),
            pl.BlockSpec((1, D), lambda i: (0, 0)),
            pl.BlockSpec(memory_space=pl.ANY),
        ],
        out_specs=pl.BlockSpec((2 * G, D), lambda i: (i, 0)),
        scratch_shapes=[pltpu.VMEM((2, R // SUBLANES, SUBLANES, D), jnp.uint32), pltpu.SemaphoreType.DMA((2,))],
        compiler_params=_params(("arbitrary",)),
        name="peer_eval",
    )(eidx.reshape(-1), eidx.reshape(-1), gate_t, h2, x2, mod4, final_g.reshape(1, D), uv)


SC_WORKERS = 32
SC_WINDOW = 128
SC_ROWS = 32
PEER_GD = 16
SC_SHARE_NUM, SC_SHARE_DEN = 5, 8
SC_PARTS = 4
BATCH_GROUPS = 1


def _sc_gather(tab, idx):
    n = idx.shape[0]
    width = tab.shape[1]
    per = n // SC_WORKERS
    nsub = SC_WINDOW // SC_ROWS
    mesh = plsc.VectorSubcoreMesh(core_axis_name="core", subcore_axis_name="subcore")

    @pl.kernel(out_type=jax.ShapeDtypeStruct((n, width), tab.dtype), mesh=mesh,
               scratch_types=[pltpu.VMEM((SC_WINDOW,), jnp.int32), pltpu.VMEM((2, SC_ROWS, width), tab.dtype),
                              pltpu.SemaphoreType.DMA((2,)), pltpu.SemaphoreType.DMA((2,))])
    def gather(x_hbm, i_hbm, o_hbm, idx_v, rows_v, gsem, wsem):
        wid = lax.axis_index("core") * (SC_WORKERS // 2) + lax.axis_index("subcore")

        def start_gather(k):
            return pltpu.async_copy(x_hbm.at[idx_v.at[pl.ds(k * SC_ROWS, SC_ROWS)]], rows_v.at[k % 2], gsem.at[k % 2])

        @pl.loop(0, per // SC_WINDOW)
        def _(w):
            base = wid * per + w * SC_WINDOW
            pltpu.sync_copy(i_hbm.at[pl.ds(base, SC_WINDOW)], idx_v)
            gat = [start_gather(0)] + [None] * (nsub - 1)
            wrt = [None] * nsub
            for k in range(nsub):
                if k + 1 < nsub:
                    if k >= 1:
                        wrt[k - 1].wait()
                    gat[k + 1] = start_gather(k + 1)
                gat[k].wait()
                wrt[k] = pltpu.async_copy(rows_v.at[k % 2], o_hbm.at[pl.ds(base + k * SC_ROWS, SC_ROWS)],
                                          wsem.at[k % 2])
            for k in range(max(nsub - 2, 0), nsub):
                wrt[k].wait()

    return gather(tab, idx)


def _peer_dense_kernel(rows_ref, gate_ref, h_ref, x_ref, gf_ref, fg_ref, o_ref, *, final):
    G = PEER_GD
    D = D_MODEL
    i = pl.program_id(0)
    lane = lax.broadcasted_iota(jnp.int32, (1, LANES), 1)
    tbase = (i % (LANES // G)) * G
    gate = gate_ref[...]
    outs = []
    for g in range(G):
        w_rows = rows_ref[PEER_SLOTS * g:PEER_SLOTS * (g + 1), :]
        u_rows = lax.bitcast_convert_type(w_rows & jnp.uint32(0xFFFF0000), F32)
        prod = u_rows * h_ref[g:g + 1, :]
        part = prod[:, 0:LANES]
        for c in range(1, D // LANES):
            part = part + prod[:, LANES * c:LANES * (c + 1)]
        act = jnp.sum(part, axis=1, keepdims=True)
        gcol = jnp.sum(jnp.where(lane == tbase + g, gate, 0.0), axis=1, keepdims=True)
        coef = gcol * (0.5 * act * (1.0 + lax.erf(act * (2.0 ** -0.5))))
        v_rows = lax.bitcast_convert_type(w_rows << 16, F32)
        outs.append(jnp.sum(v_rows * coef, axis=0, keepdims=True))
    xn = x_ref[...] + gf_ref[...] * jnp.concatenate(outs, axis=0)
    if final:
        ms = jnp.mean(xn * xn, axis=-1, keepdims=True)
        xn = xn * lax.rsqrt(ms + RMS_EPS) * fg_ref[...]
    o_ref[...] = xn


def _peer_eval_dense(rows, gate_t, h2, x2, mod4, final_g, S, final, tok0):
    T, D = x2.shape
    G = PEER_GD
    tb = rows.shape[0] // PEER_SLOTS
    assert tok0 % LANES == 0 and tb % LANES == 0
    blk0 = tok0 // G
    return pl.pallas_call(
        functools.partial(_peer_dense_kernel, final=final),
        out_shape=jax.ShapeDtypeStruct(x2.shape, F32),
        input_output_aliases={3: 0},
        grid=(tb // G,),
        in_specs=[
            pl.BlockSpec((G * PEER_SLOTS, D), lambda i: (i, 0)),
            pl.BlockSpec((None, PEER_SLOTS, LANES), lambda i: (i // (LANES // G), 0, 0)),
            pl.BlockSpec((G, D), lambda i: (blk0 + i, 0)),
            pl.BlockSpec((G, D), lambda i: (blk0 + i, 0)),
            pl.BlockSpec((None, None, 1, D), lambda i: ((blk0 + i) // (S // G), 5, 0, 0)),
            pl.BlockSpec((1, D), lambda i: (0, 0)),
        ],
        out_specs=pl.BlockSpec((G, D), lambda i: (blk0 + i, 0)),
        compiler_params=_params(("parallel",)),
        name="peer_dense",
    )(rows, gate_t, h2, x2, mod4, final_g.reshape(1, D))


def _pack_uv(u, v):
    hi = lax.bitcast_convert_type(u.astype(BF16), jnp.uint16).astype(jnp.uint32) << 16
    lo = lax.bitcast_convert_type(v.astype(BF16), jnp.uint16).astype(jnp.uint32)
    return (hi | lo).reshape(u.shape[0], 1, u.shape[1])


def kernel(x, c, norm_mix_g, norm_ffn_g, final_norm_g, ada_w, ada_b, w_in, w_out, dif_lam, dif_subln_g, rw_mu, rw_w0,
           rw_w2, rw_a0, rw_a2, rw_g2, rw_kk, rw_ka, rw_rk, rw_ln_g, rw_ln_b, fox_bf, peer_wq, peer_subkeys, peer_u,
           peer_v):
    B, S, D = x.shape
    depth = ada_w.shape[0]
    mod = _ada_mod(c, ada_w, ada_b)
    ngrp = BATCH_GROUPS if B % BATCH_GROUPS == 0 else 1
    bg = B // ngrp
    tg = bg * S
    prep = [(_pad_w_in(w_in[l]), _pack_uv(peer_u[l], peer_v[l])) for l in range(depth)]

    def mix_route(l, g, x2):
        mod4 = mod[l, g * bg:(g + 1) * bg].reshape(bg, 6, 1, D)
        (w_pad, w_vt), uv = prep[l]
        dqk, rcols, fqk, fl, vtd, vtf = _in_proj(x2, norm_mix_g[l], mod4, w_pad, w_vt, S)
        ya = _diff_attention(dqk, vtd, dif_lam[l], dif_subln_g[l], l, bg, S)
        yb = _rwkv(rcols, rw_mu[l], rw_w0[l], rw_w2[l], rw_a0[l], rw_a2[l], rw_g2[l], rw_kk[l], rw_ka[l],
                   rw_rk[l].reshape(-1), rw_ln_g[l], rw_ln_b[l], bg, S, prec=None)
        crep, crow = _fox_cum(fl, fox_bf[l], bg, S)
        yc = _fox_attention(fqk, vtf, crep, crow, bg, S)
        x2, h2 = _out_proj(ya, yb, yc, x2, w_out[l], norm_ffn_g[l], mod4, S)
        unit = SC_PARTS * LANES
        tb = (tg * SC_SHARE_NUM // SC_SHARE_DEN) // unit * unit
        ta = tg - tb
        bounds = [ta + tb * j // SC_PARTS for j in range(SC_PARTS + 1)]
        parts = []
        for t0, t1 in zip(bounds[:-1], bounds[1:]):
            if t1 == t0:
                continue
            e_p, g_p = _peer_route(h2, peer_wq[l], peer_subkeys[l], t0, t1 - t0)
            parts.append((_sc_gather(uv.reshape(-1, D), e_p.reshape(-1)), g_p, t0))
        e_a, g_a = _peer_route(h2, peer_wq[l], peer_subkeys[l], 0, ta)
        return dict(x2=x2, h2=h2, mod4=mod4, uv=uv, ta=ta, e_a=e_a, g_a=g_a, parts=parts)

    def evaluate(l, st):
        final = l == depth - 1
        x2 = _peer_eval(st["e_a"], st["g_a"], st["h2"], st["x2"], st["mod4"], final_norm_g, st["uv"], S, final,
                        ntok=st["ta"])
        for r, g_p, t0 in st["parts"]:
            x2 = _peer_eval_dense(r, g_p, st["h2"], x2, st["mod4"], final_norm_g, S, final, t0)
        return x2

    xs = [x[g * bg:(g + 1) * bg].reshape(tg, D) for g in range(ngrp)]
    pending = []
    for l in range(depth):
        for g in range(ngrp):
            pending.append((l, g, mix_route(l, g, xs[g])))
            if len(pending) == ngrp:
                l0, g0, st = pending.pop(0)
                xs[g0] = evaluate(l0, st)
    for l0, g0, st in pending:
        xs[g0] = evaluate(l0, st)
    return jnp.concatenate(xs, axis=0).reshape(B, S, D)
```

```python
import functools
import math

import jax
import jax.numpy as jnp
from jax import lax
from jax.experimental import pallas as pl
from jax.experimental.pallas import tpu as pltpu
from jax.experimental.pallas import tpu_sc as plsc

F32 = jnp.float32
BF16 = jnp.bfloat16
HIGHEST = lax.Precision.HIGHEST

D_MODEL = 1024
HEAD_DIM = 64
DIFF_HEADS = 6
DIFF_QK_DIM = HEAD_DIM // 2
RWKV_HEADS = 6
FOX_HEADS = 4
DIFF_WIDTH = DIFF_HEADS * HEAD_DIM
RWKV_WIDTH = RWKV_HEADS * HEAD_DIM
FOX_WIDTH = FOX_HEADS * HEAD_DIM
DECAY_LORA = 64
AAA_LORA = 64
GATE_LORA = 128
DIFF_COLS = 3 * DIFF_WIDTH
RWKV_COLS = 3 * RWKV_WIDTH + DECAY_LORA + AAA_LORA + GATE_LORA
PEER_HEADS = 8
PEER_NKEYS = 128
PEER_TOPK = 16
PEER_QDIM = 256
PEER_HALF = PEER_QDIM // 2
RMS_EPS = 1e-6
RWKV_LN_EPS = 64e-5

LANES = 128
SUBLANES = 8
RW_PAD_COLS = 3 * RWKV_WIDTH + 3 * LANES
VMEM_LIMIT = 56 * 1024 * 1024

RW_CHUNK = 64
RW_CHUNKS_PER_ITER = 4


def _params(sem, vmem=VMEM_LIMIT):
    return pltpu.CompilerParams(dimension_semantics=sem, vmem_limit_bytes=vmem)


def _nt(a, b, precision=None):
    return lax.dot_general(a, b, (((1,), (1,)), ((), ())), preferred_element_type=F32, precision=precision)


def _mm(a, b, precision=None):
    return jnp.dot(a, b, preferred_element_type=F32, precision=precision)


def _ada_kernel(c_ref, w_ref, b_ref, o_ref):
    c = c_ref[...]
    ca = c * jax.nn.sigmoid(c)
    o_ref[...] = _mm(ca, w_ref[...], HIGHEST) + b_ref[...]


def _ada_mod(c, ada_w, ada_b):
    L, D, N = ada_w.shape
    B = c.shape[0]
    tn = 1536
    return pl.pallas_call(
        _ada_kernel,
        out_shape=jax.ShapeDtypeStruct((L, B, N), F32),
        grid=(L, N // tn),
        in_specs=[
            pl.BlockSpec((B, D), lambda l, j: (0, 0)),
            pl.BlockSpec((None, D, tn), lambda l, j: (l, 0, j)),
            pl.BlockSpec((None, 1, tn), lambda l, j: (l, 0, j)),
        ],
        out_specs=pl.BlockSpec((None, B, tn), lambda l, j: (l, 0, j)),
        compiler_params=_params(("parallel", "parallel")),
        name="ada_mod",
    )(c, ada_w, ada_b.reshape(L, 1, N))


ATT_T = 512
QK_DIFF = 2 * DIFF_WIDTH
QK_FOX = 2 * FOX_WIDTH
VT_ROWS = DIFF_WIDTH + FOX_WIDTH
IN_PAD_COLS = QK_DIFF + RW_PAD_COLS + QK_FOX + LANES


def _inproj_kernel(x_ref, g_ref, sh_ref, sc_ref, w_ref, wvt_ref, d_ref, r_ref, f_ref, fl_ref, vtd_ref, vtf_ref, *, ta):
    x = x_ref[...]
    ms = jnp.mean(x * x, axis=-1, keepdims=True)
    y = x * lax.rsqrt(ms + RMS_EPS) * g_ref[...]
    h = (y * (1.0 + sc_ref[...]) + sh_ref[...]).astype(BF16)
    o1 = QK_DIFF
    o2 = o1 + RW_PAD_COLS
    o3 = o2 + QK_FOX
    d_ref[...] = _mm(h, w_ref[:, 0:o1]).astype(BF16)
    r_ref[...] = _mm(h, w_ref[:, o1:o2])
    f_ref[...] = _mm(h, w_ref[:, o2:o3]).astype(BF16)
    fl_ref[...] = _mm(h, w_ref[:, o3:o3 + LANES])
    vt = _nt(wvt_ref[...], h).astype(BF16)
    for s in range(x.shape[0] // ta):
        vtd_ref[s] = vt[0:DIFF_WIDTH, ta * s:ta * (s + 1)]
        vtf_ref[s] = vt[DIFF_WIDTH:, ta * s:ta * (s + 1)]


def _in_proj(x2, g, mod4, w_pad, w_vt, S):
    T, D = x2.shape
    tm = 512 if S % 512 == 0 else S
    ta = min(ATT_T, S)
    nb = S // tm
    row = lambda i: (i, 0)
    return pl.pallas_call(
        functools.partial(_inproj_kernel, ta=ta),
        out_shape=(
            jax.ShapeDtypeStruct((T, QK_DIFF), BF16),
            jax.ShapeDtypeStruct((T, RW_PAD_COLS), F32),
            jax.ShapeDtypeStruct((T, QK_FOX), BF16),
            jax.ShapeDtypeStruct((T, LANES), F32),
            jax.ShapeDtypeStruct((T // ta, DIFF_WIDTH, ta), BF16),
            jax.ShapeDtypeStruct((T // ta, FOX_WIDTH, ta), BF16),
        ),
        grid=(T // tm,),
        in_specs=[
            pl.BlockSpec((tm, D), row),
            pl.BlockSpec((1, D), lambda i: (0, 0)),
            pl.BlockSpec((None, None, 1, D), lambda i: (i // nb, 0, 0, 0)),
            pl.BlockSpec((None, None, 1, D), lambda i: (i // nb, 1, 0, 0)),
            pl.BlockSpec((D, IN_PAD_COLS), lambda i: (0, 0)),
            pl.BlockSpec((VT_ROWS, D), lambda i: (0, 0)),
        ],
        out_specs=(
            pl.BlockSpec((tm, QK_DIFF), row),
            pl.BlockSpec((tm, RW_PAD_COLS), row),
            pl.BlockSpec((tm, QK_FOX), row),
            pl.BlockSpec((tm, LANES), row),
            pl.BlockSpec((tm // ta, DIFF_WIDTH, ta), lambda i: (i, 0, 0)),
            pl.BlockSpec((tm // ta, FOX_WIDTH, ta), lambda i: (i, 0, 0)),
        ),
        compiler_params=_params(("parallel",)),
        name="in_proj",
    )(x2, g.reshape(1, D), mod4, mod4, w_pad, w_vt)


def _pad_w_in(w_in):
    D = w_in.shape[0]
    W = RWKV_WIDTH
    o = DIFF_COLS
    z64 = jnp.zeros((D, LANES - DECAY_LORA), w_in.dtype)
    rw = w_in[:, o:o + RWKV_COLS]
    fx = w_in[:, o + RWKV_COLS:]
    zf = jnp.zeros((D, LANES - FOX_HEADS), w_in.dtype)
    w_pad = jnp.concatenate([
        w_in[:, :QK_DIFF],
        rw[:, :3 * W], rw[:, 3 * W:3 * W + DECAY_LORA], z64,
        rw[:, 3 * W + DECAY_LORA:3 * W + DECAY_LORA + AAA_LORA], z64,
        rw[:, 3 * W + DECAY_LORA + AAA_LORA:],
        fx[:, :QK_FOX], fx[:, 3 * FOX_WIDTH:], zf,
    ], axis=1).astype(BF16)
    w_vt = jnp.concatenate([w_in[:, QK_DIFF:o], fx[:, QK_FOX:3 * FOX_WIDTH]], axis=1).T.astype(BF16)
    return w_pad, w_vt


LOG2E = math.log2(math.e)


ACC_ROWS = LANES + 16


def _scaled_q(q, c):
    return (q.astype(F32) * c).astype(BF16)


def _with_ones(vt):
    return jnp.concatenate([vt, jnp.ones((ACC_ROWS - LANES, vt.shape[1]), vt.dtype)], axis=0)


def _flash_step(s2s, vt1, m_ref, acc_ref):
    n = len(s2s)
    m_old = [m_ref[x] for x in range(n)]
    m_new = [jnp.maximum(m_old[x], jnp.max(s2s[x], axis=0, keepdims=True)) for x in range(n)]
    alpha = [jnp.exp2(m_old[x] - m_new[x]) for x in range(n)]
    p = [jnp.exp2(s2s[x] - m_new[x]).astype(BF16) for x in range(n)]
    pv = [_mm(vt1, p[x]) for x in range(n)]
    for x in range(n):
        acc_ref[x] = alpha[x] * acc_ref[x] + pv[x]
        m_ref[x] = m_new[x]


def _diff_attn_kernel(lam_ref, g_ref, q_ref, k_ref, vt_ref, o_ref, m_ref, acc_ref, *, tq, lam_init):
    i = pl.program_id(2)
    lane = lax.broadcasted_iota(jnp.int32, (1, LANES), 1)
    q = _scaled_q(q_ref[...], (DIFF_QK_DIM ** -0.5) * LOG2E)
    zero = jnp.zeros_like(q)
    qm = [jnp.where((lane >= DIFF_QK_DIM * x) & (lane < DIFF_QK_DIM * (x + 1)), q, zero) for x in range(4)]
    m_ref[...] = jnp.full(m_ref.shape, -jnp.inf, F32)
    acc_ref[...] = jnp.zeros(acc_ref.shape, F32)

    def step(j, diag):
        k = k_ref[pl.ds(pl.multiple_of(j * tq, tq), tq), :]
        vt1 = _with_ones(vt_ref[j])
        if diag:
            keep = (lax.broadcasted_iota(jnp.int32, (tq, tq), 1) >= lax.broadcasted_iota(jnp.int32, (tq, tq), 0))
        s2s = [_nt(k, qm[x]) for x in range(4)]
        if diag:
            s2s = [jnp.where(keep, s2, -jnp.inf) for s2 in s2s]
        _flash_step(s2s, vt1, m_ref, acc_ref)

    def body(j, carry):
        step(j, False)
        return carry

    lax.fori_loop(0, i, body, 0)
    step(i, True)

    lp = lam_ref[...]
    lam = (jnp.exp(jnp.sum(lp[0:1] * lp[1:2], axis=-1, keepdims=True))
           - jnp.exp(jnp.sum(lp[2:3] * lp[3:4], axis=-1, keepdims=True)) + lam_init)
    sm = [acc_ref[x, 0:LANES, :] / acc_ref[x, LANES:LANES + 1, :] for x in range(4)]
    outs = [sm[2 * hh] - lam * sm[2 * hh + 1] for hh in range(2)]
    row = lax.broadcasted_iota(jnp.int32, (LANES, 1), 0)
    o = jnp.where(row < HEAD_DIM, outs[0], outs[1])
    sq = o * o
    ms = jnp.where(row < HEAD_DIM, jnp.sum(sq[0:HEAD_DIM], axis=0, keepdims=True),
                   jnp.sum(sq[HEAD_DIM:], axis=0, keepdims=True)) * (1.0 / HEAD_DIM)
    y = o * lax.rsqrt(ms + RMS_EPS) * g_ref[...] * (1.0 - lam_init)
    o_ref[...] = y.T.astype(o_ref.dtype)


def _diff_attention(dqk, vtd, lam_params, subln_g, layer_idx, B, S):
    T = B * S
    tq = min(ATT_T, S)
    nq = S // tq
    npair = DIFF_HEADS // 2
    lam_init = 0.8 - 0.6 * math.exp(-0.3 * layer_idx)
    g2 = jnp.concatenate([subln_g, subln_g]).reshape(LANES, 1).astype(F32)
    return pl.pallas_call(
        functools.partial(_diff_attn_kernel, tq=tq, lam_init=lam_init),
        out_shape=jax.ShapeDtypeStruct((T, DIFF_WIDTH), BF16),
        grid=(B, npair, nq),
        in_specs=[
            pl.BlockSpec((4, DIFF_QK_DIM), lambda b, p, i: (0, 0)),
            pl.BlockSpec((LANES, 1), lambda b, p, i: (0, 0)),
            pl.BlockSpec((tq, LANES), lambda b, p, i: (b * nq + i, p)),
            pl.BlockSpec((S, LANES), lambda b, p, i: (b, npair + p)),
            pl.BlockSpec((nq, LANES, tq), lambda b, p, i: (b, p, 0)),
        ],
        out_specs=pl.BlockSpec((tq, LANES), lambda b, p, i: (b * nq + i, p)),
        scratch_shapes=[
            pltpu.VMEM((4, 1, tq), F32),
            pltpu.VMEM((4, ACC_ROWS, tq), F32),
        ],
        compiler_params=_params(("parallel", "parallel", "arbitrary")),
        name="diff_attn",
    )(lam_params, g2, dqk, dqk, vtd)


def _fox_cum_kernel(f_ref, b_ref, rep_ref, row_ref, *, S, tc):
    rr = lax.broadcasted_iota(jnp.int32, (tc, tc), 0)
    cc = lax.broadcasted_iota(jnp.int32, (tc, tc), 1)
    tri = (rr >= cc).astype(F32)
    sel_r = lax.broadcasted_iota(jnp.int32, (LANES, LANES), 0)
    carry = jnp.zeros((1, LANES), F32)
    for c in range(S // tc):
        z = f_ref[c * tc:(c + 1) * tc, :] + b_ref[...]
        logf = -(jnp.maximum(-z, 0.0) + jnp.log(1.0 + jnp.exp(-jnp.abs(z))))
        cum = _mm(tri, logf, HIGHEST) + carry
        carry = cum[tc - 1:tc, :]
        row_ref[:, c * tc:(c + 1) * tc] = cum.T[0:8, :]
        for h in range(FOX_HEADS):
            rep_ref[h, c * tc:(c + 1) * tc, :] = _mm(cum, (sel_r == h).astype(F32), HIGHEST)


def _fox_cum(fl, b_f, B, S):
    tc = 256 if S % 256 == 0 else S
    bpad = jnp.zeros((1, LANES), F32).at[0, :FOX_HEADS].set(b_f.astype(F32))
    return pl.pallas_call(
        functools.partial(_fox_cum_kernel, S=S, tc=tc),
        out_shape=(jax.ShapeDtypeStruct((B, FOX_HEADS, S, LANES), F32), jax.ShapeDtypeStruct((B, 8, S), F32)),
        grid=(B,),
        in_specs=[pl.BlockSpec((S, LANES), lambda b: (b, 0)), pl.BlockSpec((1, LANES), lambda b: (0, 0))],
        out_specs=(pl.BlockSpec((None, FOX_HEADS, S, LANES), lambda b: (b, 0, 0, 0)),
                   pl.BlockSpec((None, 8, S), lambda b: (b, 0, 0))),
        compiler_params=_params(("parallel",)),
        name="fox_cum",
    )(fl, bpad)


def _fox_attn_kernel(q_ref, k_ref, vt_ref, c0_ref, c1_ref, cr_ref, o_ref, m_ref, acc_ref, *, tq):
    p_id = pl.program_id(1)
    i = pl.program_id(2)
    lane = lax.broadcasted_iota(jnp.int32, (1, LANES), 1)
    q = _scaled_q(q_ref[...], (HEAD_DIM ** -0.5) * LOG2E)
    zero = jnp.zeros_like(q)
    qm = [jnp.where((lane >= HEAD_DIM * x) & (lane < HEAD_DIM * (x + 1)), q, zero) for x in range(2)]
    ck_refs = (c0_ref, c1_ref)
    cq = [cr_ref[2 * p_id + x, pl.ds(i, 1), :] for x in range(2)]
    m_ref[...] = jnp.full(m_ref.shape, -jnp.inf, F32)
    acc_ref[...] = jnp.zeros(acc_ref.shape, F32)

    def step(j, diag):
        off = pl.multiple_of(j * tq, tq)
        k = k_ref[pl.ds(off, tq), :]
        vt1 = _with_ones(vt_ref[j])
        if diag:
            keep = (lax.broadcasted_iota(jnp.int32, (tq, tq), 1) >= lax.broadcasted_iota(jnp.int32, (tq, tq), 0))
        s2s = []
        for x in range(2):
            ck = ck_refs[x][pl.ds(off, tq), :]
            bias = (cq[x] - jnp.concatenate([ck] * (tq // LANES), axis=1)) * LOG2E
            s2s.append(_nt(k, qm[x]) + bias)
        if diag:
            s2s = [jnp.where(keep, s2, -jnp.inf) for s2 in s2s]
        _flash_step(s2s, vt1, m_ref, acc_ref)

    def body(j, carry):
        step(j, False)
        return carry

    lax.fori_loop(0, i, body, 0)
    step(i, True)
    row = lax.broadcasted_iota(jnp.int32, (LANES, 1), 0)
    sm = [acc_ref[x, 0:LANES, :] / acc_ref[x, LANES:LANES + 1, :] for x in range(2)]
    o = jnp.where(row < HEAD_DIM, sm[0], sm[1])
    o_ref[...] = o.T.astype(o_ref.dtype)


def _fox_attention(fqk, vtf, crep, crow, B, S):
    T = B * S
    tq = min(ATT_T, S)
    nq = S // tq
    npair = FOX_HEADS // 2
    crow4 = crow.reshape(B, 8, nq, tq)
    rep = lambda x: pl.BlockSpec((None, None, S, LANES), lambda b, p, i: (b, 2 * p + x, 0, 0))
    return pl.pallas_call(
        functools.partial(_fox_attn_kernel, tq=tq),
        out_shape=jax.ShapeDtypeStruct((T, FOX_WIDTH), BF16),
        grid=(B, npair, nq),
        in_specs=[
            pl.BlockSpec((tq, LANES), lambda b, p, i: (b * nq + i, p)),
            pl.BlockSpec((S, LANES), lambda b, p, i: (b, npair + p)),
            pl.BlockSpec((nq, LANES, tq), lambda b, p, i: (b, p, 0)),
            rep(0), rep(1),
            pl.BlockSpec((None, 8, nq, tq), lambda b, p, i: (b, 0, 0, 0)),
        ],
        out_specs=pl.BlockSpec((tq, LANES), lambda b, p, i: (b * nq + i, p)),
        scratch_shapes=[
            pltpu.VMEM((2, 1, tq), F32),
            pltpu.VMEM((2, ACC_ROWS, tq), F32),
        ],
        compiler_params=_params(("parallel", "parallel", "arbitrary")),
        name="fox_attn",
    )(fqk, fqk, vtf, crep, crep, crow4)


def _split3(x):
    hi = x.astype(BF16)
    r1 = x - hi.astype(F32)
    mid = r1.astype(BF16)
    lo = (r1 - mid.astype(F32)).astype(BF16)
    return hi, mid, lo


def _seg_sum(x, seg, npair):
    parts = _split3(x)
    return jnp.concatenate(
        [sum(_mm(t[:, LANES * p:LANES * (p + 1)], seg) for t in parts) for p in range(npair)], axis=1)


def _rwkv_kernel(x_ref, mu_ref, w0_ref, w2_ref, a0_ref, a2_ref, g2_ref, kk_ref, ka_ref, rk_ref, lng_ref, lnb_ref,
                 o_ref, carry_ref, st_ref, at_ref, rt_ref, bt_ref, kt_ref, v_ref, wc_ref, y_ref, g_ref, bon_ref,
                 *, tt, prec):
    i = pl.program_id(1)
    W = RWKV_WIDTH
    C = RW_CHUNK
    CPI = math.gcd(RW_CHUNKS_PER_ITER, tt // C)
    npair = RWKV_HEADS // 2

    @pl.when(i == 0)
    def _():
        carry_ref[...] = jnp.zeros(carry_ref.shape, F32)
        st_ref[...] = jnp.zeros(st_ref.shape, F32)

    x = x_ref[...]
    rows = lax.broadcasted_iota(jnp.int32, (tt, 1), 0)
    prev = jnp.where(rows == 0, carry_ref[...], pltpu.roll(x, 1, axis=0))
    carry_ref[...] = x[tt - 1:tt, :]
    xs = x + (prev - x) * mu_ref[...]
    r = xs[:, 0:W]
    k = xs[:, W:2 * W]
    v = xs[:, 2 * W:3 * W]
    xw = xs[:, 3 * W:3 * W + LANES]
    xa = xs[:, 3 * W + LANES:3 * W + 2 * LANES]
    xg = xs[:, 3 * W + 2 * LANES:]
    wl = w0_ref[...] + _mm(jnp.tanh(xw), w2_ref[...], HIGHEST)
    w = -(jnp.maximum(-wl, 0.0) + jnp.log(1.0 + jnp.exp(-jnp.abs(wl)))) - 0.5
    logdec = -jnp.exp(w)
    a = jax.nn.sigmoid(a0_ref[...] + _mm(xa, a2_ref[...], HIGHEST))
    g_ref[...] = _mm(jax.nn.sigmoid(xg), g2_ref[...], HIGHEST)

    r_i = lax.broadcasted_iota(jnp.int32, (LANES, LANES), 0)
    c_i = lax.broadcasted_iota(jnp.int32, (LANES, LANES), 1)
    seg = ((r_i // HEAD_DIM) == (c_i // HEAD_DIM)).astype(BF16)
    kkv = k * kk_ref[...]
    kkn = kkv / jnp.maximum(jnp.sqrt(_seg_sum(kkv * kkv, seg, npair)), 1e-12)
    k2 = k * (1.0 + (a - 1.0) * ka_ref[...])
    bon_ref[...] = _seg_sum(r * k2 * rk_ref[...], seg, npair) * v

    rt_i = lax.broadcasted_iota(jnp.int32, (tt, tt), 0)
    ct_i = lax.broadcasted_iota(jnp.int32, (tt, tt), 1)
    tri = (((rt_i // C) == (ct_i // C)) & (rt_i >= ct_i)).astype(BF16)
    cum = sum(_mm(tri, t) for t in _split3(logdec))
    winv = jnp.exp(-cum)
    wcum = jnp.exp(cum)
    at_ref[...] = -kkn * jnp.exp(cum - logdec)
    bt_ref[...] = kkn * a * winv
    kt_ref[...] = k2 * winv
    rt_ref[...] = r * wcum
    v_ref[...] = v
    wc_ref[...] = wcum

    lane = lax.broadcasted_iota(jnp.int32, (1, LANES), 1)
    lo = lane < HEAD_DIM
    tpos = r_i % C
    ipos = c_i % C
    strict = tpos > ipos
    incl = tpos >= ipos
    eye = r_i == c_i

    def stack2(m):
        return jnp.concatenate([jnp.where(lo, m, 0.0), jnp.where(lo, 0.0, m)], axis=0)

    def cast(m):
        return m if prec is not None else m.astype(BF16)

    def mm(p, q):
        return _mm(cast(p), cast(q), prec)

    def nt(p, q):
        return _nt(cast(p), cast(q), prec)

    def chunks(cc, carry):
        units = [(ci, p) for ci in range(CPI) for p in range(npair)]
        r0 = [pl.multiple_of((cc * CPI + ci) * C, C) for ci in range(CPI)]
        ld = lambda ref, ci, p: stack2(ref[pl.ds(r0[ci], C), LANES * p:LANES * (p + 1)])
        a2 = {u: ld(at_ref, *u) for u in units}
        r2 = {u: ld(rt_ref, *u) for u in units}
        b2 = {u: ld(bt_ref, *u) for u in units}
        k2s = {u: ld(kt_ref, *u) for u in units}
        v2 = {u: ld(v_ref, *u) for u in units}
        wl = {(ci, p): wc_ref[pl.ds(pl.multiple_of(r0[ci] + C - 8, 8), 8), LANES * p:LANES * (p + 1)][7:8, :]
              for (ci, p) in units}
        ar = {u: jnp.concatenate([a2[u], r2[u]], axis=0) for u in units}
        mb = {u: nt(ar[u], b2[u]) for u in units}
        mk = {u: nt(ar[u], k2s[u]) for u in units}
        lab = {u: jnp.where(strict, mb[u][0:LANES], 0.0) for u in units}
        mrb = {u: jnp.where(incl, mb[u][LANES:], 0.0) for u in units}
        lak = {u: jnp.where(strict, mk[u][0:LANES], 0.0) for u in units}
        mrk = {u: jnp.where(incl, mk[u][LANES:], 0.0) for u in units}
        xx = {u: jnp.concatenate([a2[u], mm(lak[u], v2[u])], axis=1) for u in units}
        lp = lab
        for it in range(6):
            xx = {u: xx[u] + mm(lp[u], xx[u]) for u in units}
            if it < 5:
                lp = {u: mm(lp[u], lp[u]) for u in units}
        mq = {u: mm(mrb[u], xx[u]) for u in units}
        mv = {u: mm(mrk[u], v2[u]) for u in units}
        bx = {u: mm((b2[u] * wl[u]).T, xx[u]) for u in units}
        kv = {u: mm((k2s[u] * wl[u]).T, v2[u]) for u in units}
        st = [st_ref[p] for p in range(npair)]
        for (ci, p) in units:
            u = (ci, p)
            q1 = r2[u] + mq[u][:, 0:LANES]
            q2 = mq[u][:, LANES:] + mv[u]
            gmat = jnp.where(eye, jnp.broadcast_to(wl[u], (LANES, LANES)), 0.0) + bx[u][:, 0:LANES]
            hmat = bx[u][:, LANES:] + kv[u]
            gs = mm(jnp.concatenate([gmat, q1], axis=0), st[p])
            st[p] = gs[0:LANES] + hmat
            yy = gs[LANES:] + q2
            y_ref[pl.ds(r0[ci], C), LANES * p:LANES * (p + 1)] = yy[0:C] + yy[C:]
        for p in range(npair):
            st_ref[p] = st[p]
        return carry

    lax.fori_loop(0, tt // (C * CPI), chunks, 0)

    y = y_ref[...]
    inv = 1.0 / HEAD_DIM
    mean = _seg_sum(y, seg, npair) * inv
    yc = y - mean
    var = _seg_sum(yc * yc, seg, npair) * inv
    yn = yc * lax.rsqrt(var + RWKV_LN_EPS) * lng_ref[...] + lnb_ref[...]
    o_ref[...] = ((yn + bon_ref[...]) * g_ref[...]).astype(o_ref.dtype)


def _rwkv(rcols, mu, w0, w2, a0, a2, g2, k_k, k_a, r_k, ln_g, ln_b, B, S, prec=HIGHEST):
    T = B * S
    W = RWKV_WIDTH
    tt = 512 if S % 512 == 0 else S
    nt_ = S // tt
    npair = RWKV_HEADS // 2
    pad = LANES - DECAY_LORA
    mu_p = jnp.concatenate([mu[:3 * W], mu[3 * W:3 * W + DECAY_LORA], jnp.zeros((pad,), F32),
                            mu[3 * W + DECAY_LORA:3 * W + DECAY_LORA + AAA_LORA], jnp.zeros((pad,), F32),
                            mu[3 * W + DECAY_LORA + AAA_LORA:]]).reshape(1, RW_PAD_COLS)
    w2p = jnp.concatenate([w2, jnp.zeros((pad, W), F32)], axis=0)
    a2p = jnp.concatenate([a2, jnp.zeros((pad, W), F32)], axis=0)
    vec = lambda t: t.reshape(1, W).astype(F32)
    full = lambda shape: pl.BlockSpec(shape, lambda b, i: (0,) * len(shape))
    sc = lambda: pltpu.VMEM((tt, W), F32)
    return pl.pallas_call(
        functools.partial(_rwkv_kernel, tt=tt, prec=prec),
        out_shape=jax.ShapeDtypeStruct((T, W), BF16),
        grid=(B, nt_),
        in_specs=[
            pl.BlockSpec((tt, RW_PAD_COLS), lambda b, i: (b * nt_ + i, 0)),
            full((1, RW_PAD_COLS)), full((1, W)), full((LANES, W)), full((1, W)), full((LANES, W)),
            full((GATE_LORA, W)), full((1, W)), full((1, W)), full((1, W)), full((1, W)), full((1, W)),
        ],
        out_specs=pl.BlockSpec((tt, W), lambda b, i: (b * nt_ + i, 0)),
        scratch_shapes=[
            pltpu.VMEM((1, RW_PAD_COLS), F32),
            pltpu.VMEM((npair, LANES, LANES), F32),
            sc(), sc(), sc(), sc(), sc(), sc(), sc(), sc(), sc(),
        ],
        compiler_params=_params(("parallel", "arbitrary")),
        name="rwkv7",
    )(rcols, mu_p, vec(w0), w2p, vec(a0), a2p, g2, vec(k_k), vec(k_a), vec(r_k), vec(ln_g), vec(ln_b))


def _outproj_kernel(ya_ref, yb_ref, yc_ref, x_ref, w_ref, gm_ref, g_ref, sh_ref, sc_ref, xo_ref, ho_ref):
    o1 = DIFF_WIDTH
    o2 = o1 + RWKV_WIDTH
    mix = (_mm(ya_ref[...], w_ref[0:o1, :]) + _mm(yb_ref[...], w_ref[o1:o2, :])
           + _mm(yc_ref[...], w_ref[o2:, :]))
    xn = x_ref[...] + gm_ref[...] * mix
    xo_ref[...] = xn
    ms = jnp.mean(xn * xn, axis=-1, keepdims=True)
    y = xn * lax.rsqrt(ms + RMS_EPS) * g_ref[...]
    ho_ref[...] = y * (1.0 + sc_ref[...]) + sh_ref[...]


def _out_proj(ya, yb, yc, x2, w_out, g, mod4, S):
    T, D = x2.shape
    tm = 512 if S % 512 == 0 else S
    nb = S // tm
    row = lambda i: (i, 0)
    modspec = lambda which: pl.BlockSpec((None, None, 1, D), lambda i: (i // nb, which, 0, 0))
    return pl.pallas_call(
        _outproj_kernel,
        out_shape=(jax.ShapeDtypeStruct((T, D), F32), jax.ShapeDtypeStruct((T, D), F32)),
        grid=(T // tm,),
        in_specs=[
            pl.BlockSpec((tm, DIFF_WIDTH), row), pl.BlockSpec((tm, RWKV_WIDTH), row), pl.BlockSpec((tm, FOX_WIDTH), row),
            pl.BlockSpec((tm, D), row),
            pl.BlockSpec((D, D), lambda i: (0, 0)),
            modspec(2),
            pl.BlockSpec((1, D), lambda i: (0, 0)),
            modspec(3), modspec(4),
        ],
        out_specs=(pl.BlockSpec((tm, D), row), pl.BlockSpec((tm, D), row)),
        compiler_params=_params(("parallel",)),
        name="out_proj",
    )(ya, yb, yc, x2, w_out.astype(BF16), mod4, g.reshape(1, D), mod4, mod4)


def _top16(s, iota_f, n):
    vals, poss = [], []
    for _ in range(PEER_TOPK):
        m = jnp.max(s, axis=0, keepdims=True)
        pos = jnp.min(jnp.where(s == m, iota_f, float(n)), axis=0, keepdims=True)
        vals.append(m)
        poss.append(pos)
        s = jnp.where(iota_f == pos, -jnp.inf, s)
    return jnp.concatenate(vals, axis=0), jnp.concatenate(poss, axis=0)


ROUTE_UNROLL = 8
PEER_NCAND = 56


def _peer_cand_tables():
    K = PEER_TOPK
    pairs = [(a, b) for a in range(K) for b in range(K) if (a + 1) * (b + 1) <= K]
    n = PEER_NCAND
    p0 = [[0.0] * K for _ in range(n)]
    p1 = [[0.0] * K for _ in range(n)]
    pad = [0.0] * n
    pos = [float(K * K + r) for r in range(n)]
    for r, (a, b) in enumerate(pairs):
        p0[r][a] = 1.0
        p1[r][b] = 1.0
        pos[r] = float(a * K + b)
    for r in range(len(pairs), n):
        pad[r] = -float("inf")
    col = lambda v: jnp.broadcast_to(jnp.asarray(v, F32)[:, None], (n, LANES))
    return jnp.asarray(p0, F32), jnp.asarray(p1, F32), col(pad), col(pos)


def _peer_route_kernel(h_ref, wq_ref, sk_ref, p0_ref, p1_ref, cpad_ref, cpos_ref, idx_ref, gate_ref, q_scr, e_scr,
                       g_scr):
    K = PEER_TOPK
    hb = h_ref[...].astype(BF16)
    q = _mm(hb, wq_ref[...])
    for hc in range(2 * PEER_HEADS):
        q_scr[hc] = q[:, LANES * hc:LANES * (hc + 1)].astype(BF16)
    iota_n = lax.broadcasted_iota(jnp.int32, (PEER_NKEYS, LANES), 0).astype(F32)
    cpos = cpos_ref[...]

    def one_head(h):
        sv0, si0 = _top16(_nt(sk_ref[2 * h], q_scr[2 * h]), iota_n, PEER_NKEYS)
        sv1, si1 = _top16(_nt(sk_ref[2 * h + 1], q_scr[2 * h + 1]), iota_n, PEER_NKEYS)
        cand = _mm(p0_ref[...], sv0, HIGHEST) + _mm(p1_ref[...], sv1, HIGHEST) + cpad_ref[...]
        cidx = _mm(p0_ref[...], si0) * float(PEER_NKEYS) + _mm(p1_ref[...], si1)
        fv, es = [], []
        for _ in range(K):
            m = jnp.max(cand, axis=0, keepdims=True)
            pos = jnp.min(jnp.where(cand == m, cpos, float(2 * K * K)), axis=0, keepdims=True)
            hit = cpos == pos
            fv.append(m)
            es.append(jnp.max(jnp.where(hit, cidx, -1.0), axis=0, keepdims=True))
            cand = jnp.where(hit, -jnp.inf, cand)
        fv = jnp.concatenate(fv, axis=0)
        ex = jnp.exp(fv - fv[0:1, :])
        g_scr[h] = ex / jnp.sum(ex, axis=0, keepdims=True)
        e_scr[h] = jnp.concatenate(es, axis=0)

    def heads(hh, carry):
        for j in range(ROUTE_UNROLL):
            one_head(hh * ROUTE_UNROLL + j)
        return carry

    lax.fori_loop(0, PEER_HEADS // ROUTE_UNROLL, heads, 0)
    e = e_scr[...].reshape(PEER_HEADS * K, LANES)
    idx_ref[...] = e.T.astype(jnp.int32)
    gate_ref[...] = g_scr[...].reshape(PEER_HEADS * K, LANES)


def _peer_route(h2, wq, subkeys, tok0=0, ntok=None):
    T, D = h2.shape
    T = T - tok0 if ntok is None else ntok
    tm = LANES
    blk0 = tok0 // tm
    nq = 2 * PEER_HEADS
    sk = subkeys.reshape(nq, PEER_NKEYS, PEER_HALF).astype(BF16)
    p0, p1, cpad, cpos = _peer_cand_tables()
    const = lambda shape: pl.BlockSpec(shape, lambda i: (0,) * len(shape))
    return pl.pallas_call(
        _peer_route_kernel,
        out_shape=(jax.ShapeDtypeStruct((T, PEER_HEADS * PEER_TOPK), jnp.int32),
                   jax.ShapeDtypeStruct((T // tm, PEER_HEADS * PEER_TOPK, tm), F32)),
        grid=(T // tm,),
        in_specs=[
            pl.BlockSpec((tm, D), lambda i: (blk0 + i, 0)),
            const((D, nq * PEER_HALF)),
            const((nq, PEER_NKEYS, PEER_HALF)),
            const((PEER_NCAND, PEER_TOPK)), const((PEER_NCAND, PEER_TOPK)),
            const((PEER_NCAND, LANES)), const((PEER_NCAND, LANES)),
        ],
        out_specs=(pl.BlockSpec((tm, PEER_HEADS * PEER_TOPK), lambda i: (i, 0)),
                   pl.BlockSpec((None, PEER_HEADS * PEER_TOPK, tm), lambda i: (i, 0, 0))),
        scratch_shapes=[
            pltpu.VMEM((nq, tm, PEER_HALF), BF16),
            pltpu.VMEM((PEER_HEADS, PEER_TOPK, tm), F32),
            pltpu.VMEM((PEER_HEADS, PEER_TOPK, tm), F32),
        ],
        compiler_params=_params(("parallel",)),
        name="peer_route",
    )(h2, wq.astype(BF16), sk, p0, p1, cpad, cpos)


PEER_G = 16
PEER_SLOTS = PEER_HEADS * PEER_TOPK


def _peer_eval_kernel(idx_ref, idxn_ref, gate_ref, h_ref, x_ref, gf_ref, fg_ref, uv_ref, o_ref, buf, sem, *, final):
    G = PEER_G
    R = G * PEER_SLOTS
    D = D_MODEL
    tiles = PEER_SLOTS // SUBLANES
    i = pl.program_id(0)
    n = pl.num_programs(0)

    def start(ids, off, s, t, u):
        pltpu.make_async_copy(uv_ref.at[ids[off + t * SUBLANES + u]], buf.at[s, t, pl.ds(u, 1), :],
                              sem.at[s]).start(priority=u % 2)

    def wait(s):
        pltpu.make_async_copy(buf.at[s], buf.at[s], sem.at[s]).wait()

    @pl.when(i == 0)
    def _():
        def body(t, carry):
            for u in range(SUBLANES):
                start(idx_ref, 0, 0, t, u)
            return carry
        lax.fori_loop(0, R // SUBLANES, body, 0)

    lane = lax.broadcasted_iota(jnp.int32, (1, LANES), 1)
    tbase = (i % (LANES // (2 * G))) * (2 * G)
    gate = gate_ref[...]
    outs = []
    for s in range(2):
        wait(s)
        nxt_ids, nxt_off = (idx_ref, R) if s == 0 else (idxn_ref, 0)
        for g in range(G):
            for t in range(tiles * g, tiles * (g + 1)):
                for u in range(SUBLANES):
                    start(nxt_ids, nxt_off, 1 - s, t, u)
            w_rows = buf[s, tiles * g:tiles * (g + 1)].reshape(PEER_SLOTS, D)
            u_rows = lax.bitcast_convert_type(w_rows & jnp.uint32(0xFFFF0000), F32)
            prod = u_rows * h_ref[G * s + g:G * s + g + 1, :]
            part = prod[:, 0:LANES]
            for c in range(1, D // LANES):
                part = part + prod[:, LANES * c:LANES * (c + 1)]
            act = jnp.sum(part, axis=1, keepdims=True)
            gcol = jnp.sum(jnp.where(lane == tbase + G * s + g, gate, 0.0), axis=1, keepdims=True)
            coef = gcol * (0.5 * act * (1.0 + lax.erf(act * (2.0 ** -0.5))))
            v_rows = lax.bitcast_convert_type(w_rows << 16, F32)
            outs.append(jnp.sum(v_rows * coef, axis=0, keepdims=True))
    xn = x_ref[...] + gf_ref[...] * jnp.concatenate(outs, axis=0)
    if final:
        ms = jnp.mean(xn * xn, axis=-1, keepdims=True)
        xn = xn * lax.rsqrt(ms + RMS_EPS) * fg_ref[...]
    o_ref[...] = xn

    @pl.when(i == n - 1)
    def _():
        wait(0)


def _peer_eval(eidx, gate_t, h2, x2, mod4, final_g, uv, S, final, ntok=None):
    T, D = x2.shape
    T = T if ntok is None else ntok
    G = PEER_G
    R = G * PEER_SLOTS
    n = T // (2 * G)
    return pl.pallas_call(
        functools.partial(_peer_eval_kernel, final=final),
        out_shape=jax.ShapeDtypeStruct(x2.shape, F32),
        input_output_aliases={4: 0},
        grid=(n,),
        in_specs=[
            pl.BlockSpec((2 * R,), lambda i: (i,), memory_space=pltpu.SMEM),
            pl.BlockSpec((R,), lambda i: (jnp.minimum(2 * i + 2, 2 * n - 2),), memory_space=pltpu.SMEM),
            pl.BlockSpec((None, PEER_SLOTS, LANES), lambda i: (i // (LANES // (2 * G)), 0, 0)),
            pl.BlockSpec((2 * G, D), lambda i: (i, 0)),
            pl.BlockSpec((2 * G, D), lambda i: (i, 0)),
            pl.BlockSpec((None, None, 1, D), lambda i: (i // (S // (2 * G)), 5, 0, 0)),
            pl.BlockSpec((1, D), lambda i: (0, 0)),
            pl.BlockSpec(memory_space=pl.ANY),
        ],
        out_specs=pl.BlockSpec((2 * G, D), lambda i: (i, 0)),
        scratch_shapes=[pltpu.VMEM((2, R // SUBLANES, SUBLANES, D), jnp.uint32), pltpu.SemaphoreType.DMA((2,))],
        compiler_params=_params(("arbitrary",)),
        name="peer_eval",
    )(eidx.reshape(-1), eidx.reshape(-1), gate_t, h2, x2, mod4, final_g.reshape(1, D), uv)


SC_WORKERS = 32
SC_WINDOW = 128
SC_ROWS = 32
PEER_GD = 16
SC_SHARE_NUM, SC_SHARE_DEN = 21, 32
SC_PARTS = 8
BATCH_GROUPS = 1


def _sc_gather(tab, idx):
    n = idx.shape[0]
    width = tab.shape[1]
    per = n // SC_WORKERS
    nsub = SC_WINDOW // SC_ROWS
    mesh = plsc.VectorSubcoreMesh(core_axis_name="core", subcore_axis_name="subcore")

    @pl.kernel(out_type=jax.ShapeDtypeStruct((n, width), tab.dtype), mesh=mesh,
               scratch_types=[pltpu.VMEM((SC_WINDOW,), jnp.int32), pltpu.VMEM((2, SC_ROWS, width), tab.dtype),
                              pltpu.SemaphoreType.DMA((2,)), pltpu.SemaphoreType.DMA((2,))])
    def gather(x_hbm, i_hbm, o_hbm, idx_v, rows_v, gsem, wsem):
        wid = lax.axis_index("core") * (SC_WORKERS // 2) + lax.axis_index("subcore")

        def start_gather(k):
            return pltpu.async_copy(x_hbm.at[idx_v.at[pl.ds(k * SC_ROWS, SC_ROWS)]], rows_v.at[k % 2], gsem.at[k % 2])

        @pl.loop(0, per // SC_WINDOW)
        def _(w):
            base = wid * per + w * SC_WINDOW
            pltpu.sync_copy(i_hbm.at[pl.ds(base, SC_WINDOW)], idx_v)
            gat = [start_gather(0)] + [None] * (nsub - 1)
            wrt = [None] * nsub
            for k in range(nsub):
                if k + 1 < nsub:
                    if k >= 1:
                        wrt[k - 1].wait()
                    gat[k + 1] = start_gather(k + 1)
                gat[k].wait()
                wrt[k] = pltpu.async_copy(rows_v.at[k % 2], o_hbm.at[pl.ds(base + k * SC_ROWS, SC_ROWS)],
                                          wsem.at[k % 2])
            for k in range(max(nsub - 2, 0), nsub):
                wrt[k].wait()

    return gather(tab, idx)


def _peer_dense_kernel(rows_ref, gate_ref, h_ref, x_ref, gf_ref, fg_ref, o_ref, *, final):
    G = PEER_GD
    D = D_MODEL
    i = pl.program_id(0)
    lane = lax.broadcasted_iota(jnp.int32, (1, LANES), 1)
    tbase = (i % (LANES // G)) * G
    gate = gate_ref[...]
    outs = []
    for g in range(G):
        w_rows = rows_ref[PEER_SLOTS * g:PEER_SLOTS * (g + 1), :]
        u_rows = lax.bitcast_convert_type(w_rows & jnp.uint32(0xFFFF0000), F32)
        prod = u_rows * h_ref[g:g + 1, :]
        part = prod[:, 0:LANES]
        for c in range(1, D // LANES):
            part = part + prod[:, LANES * c:LANES * (c + 1)]
        act = jnp.sum(part, axis=1, keepdims=True)
        gcol = jnp.sum(jnp.where(lane == tbase + g, gate, 0.0), axis=1, keepdims=True)
        coef = gcol * (0.5 * act * (1.0 + lax.erf(act * (2.0 ** -0.5))))
        v_rows = lax.bitcast_convert_type(w_rows << 16, F32)
        outs.append(jnp.sum(v_rows * coef, axis=0, keepdims=True))
    xn = x_ref[...] + gf_ref[...] * jnp.concatenate(outs, axis=0)
    if final:
        ms = jnp.mean(xn * xn, axis=-1, keepdims=True)
        xn = xn * lax.rsqrt(ms + RMS_EPS) * fg_ref[...]
    o_ref[...] = xn


def _peer_eval_dense(rows, gate_t, h2, x2, mod4, final_g, S, final, tok0):
    T, D = x2.shape
    G = PEER_GD
    tb = rows.shape[0] // PEER_SLOTS
    assert tok0 % LANES == 0 and tb % LANES == 0
    blk0 = tok0 // G
    return pl.pallas_call(
        functools.partial(_peer_dense_kernel, final=final),
        out_shape=jax.ShapeDtypeStruct(x2.shape, F32),
        input_output_aliases={3: 0},
        grid=(tb // G,),
        in_specs=[
            pl.BlockSpec((G * PEER_SLOTS, D), lambda i: (i, 0)),
            pl.BlockSpec((None, PEER_SLOTS, LANES), lambda i: (i // (LANES // G), 0, 0)),
            pl.BlockSpec((G, D), lambda i: (blk0 + i, 0)),
            pl.BlockSpec((G, D), lambda i: (blk0 + i, 0)),
            pl.BlockSpec((None, None, 1, D), lambda i: ((blk0 + i) // (S // G), 5, 0, 0)),
            pl.BlockSpec((1, D), lambda i: (0, 0)),
        ],
        out_specs=pl.BlockSpec((G, D), lambda i: (blk0 + i, 0)),
        compiler_params=_params(("parallel",)),
        name="peer_dense",
    )(rows, gate_t, h2, x2, mod4, final_g.reshape(1, D))


def _pack_uv(u, v):
    hi = lax.bitcast_convert_type(u.astype(BF16), jnp.uint16).astype(jnp.uint32) << 16
    lo = lax.bitcast_convert_type(v.astype(BF16), jnp.uint16).astype(jnp.uint32)
    return (hi | lo).reshape(u.shape[0], 1, u.shape[1])


def kernel(x, c, norm_mix_g, norm_ffn_g, final_norm_g, ada_w, ada_b, w_in, w_out, dif_lam, dif_subln_g, rw_mu, rw_w0,
           rw_w2, rw_a0, rw_a2, rw_g2, rw_kk, rw_ka, rw_rk, rw_ln_g, rw_ln_b, fox_bf, peer_wq, peer_subkeys, peer_u,
           peer_v):
    B, S, D = x.shape
    depth = ada_w.shape[0]
    mod = _ada_mod(c, ada_w, ada_b)
    ngrp = BATCH_GROUPS if B % BATCH_GROUPS == 0 else 1
    bg = B // ngrp
    tg = bg * S
    prep = [(_pad_w_in(w_in[l]), _pack_uv(peer_u[l], peer_v[l])) for l in range(depth)]

    def mix_route(l, g, x2):
        mod4 = mod[l, g * bg:(g + 1) * bg].reshape(bg, 6, 1, D)
        (w_pad, w_vt), uv = prep[l]
        dqk, rcols, fqk, fl, vtd, vtf = _in_proj(x2, norm_mix_g[l], mod4, w_pad, w_vt, S)
        ya = _diff_attention(dqk, vtd, dif_lam[l], dif_subln_g[l], l, bg, S)
        yb = _rwkv(rcols, rw_mu[l], rw_w0[l], rw_w2[l], rw_a0[l], rw_a2[l], rw_g2[l], rw_kk[l], rw_ka[l],
                   rw_rk[l].reshape(-1), rw_ln_g[l], rw_ln_b[l], bg, S, prec=None)
        crep, crow = _fox_cum(fl, fox_bf[l], bg, S)
        yc = _fox_attention(fqk, vtf, crep, crow, bg, S)
        x2, h2 = _out_proj(ya, yb, yc, x2, w_out[l], norm_ffn_g[l], mod4, S)
        unit = SC_PARTS * LANES
        tb = (tg * SC_SHARE_NUM // SC_SHARE_DEN) // unit * unit
        ta = tg - tb
        bounds = [ta + tb * j // SC_PARTS for j in range(SC_PARTS + 1)]
        parts = []
        for t0, t1 in zip(bounds[:-1], bounds[1:]):
            if t1 == t0:
                continue
            e_p, g_p = _peer_route(h2, peer_wq[l], peer_subkeys[l], t0, t1 - t0)
            parts.append((_sc_gather(uv.reshape(-1, D), e_p.reshape(-1)), g_p, t0))
        e_a, g_a = _peer_route(h2, peer_wq[l], peer_subkeys[l], 0, ta)
        return dict(x2=x2, h2=h2, mod4=mod4, uv=uv, ta=ta, e_a=e_a, g_a=g_a, parts=parts)

    def evaluate(l, st):
        final = l == depth - 1
        x2 = _peer_eval(st["e_a"], st["g_a"], st["h2"], st["x2"], st["mod4"], final_norm_g, st["uv"], S, final,
                        ntok=st["ta"])
        for r, g_p, t0 in st["parts"]:
            x2 = _peer_eval_dense(r, g_p, st["h2"], x2, st["mod4"], final_norm_g, S, final, t0)
        return x2

    xs = [x[g * bg:(g + 1) * bg].reshape(tg, D) for g in range(ngrp)]
    pending = []
    for l in range(depth):
        for g in range(ngrp):
            pending.append((l, g, mix_route(l, g, xs[g])))
            if len(pending) == ngrp:
                l0, g0, st = pending.pop(0)
                xs[g0] = evaluate(l0, st)
    for l0, g0, st in pending:
        xs[g0] = evaluate(l0, st)
    return jnp.concatenate(xs, axis=0).reshape(B, S, D)
```

```python
import functools
import math

import jax
import jax.numpy as jnp
from jax import lax
from jax.experimental import pallas as pl
from jax.experimental.pallas import tpu as pltpu
from jax.experimental.pallas import tpu_sc as plsc

F32 = jnp.float32
BF16 = jnp.bfloat16
HIGHEST = lax.Precision.HIGHEST

D_MODEL = 1024
HEAD_DIM = 64
DIFF_HEADS = 6
DIFF_QK_DIM = HEAD_DIM // 2
RWKV_HEADS = 6
FOX_HEADS = 4
DIFF_WIDTH = DIFF_HEADS * HEAD_DIM
RWKV_WIDTH = RWKV_HEADS * HEAD_DIM
FOX_WIDTH = FOX_HEADS * HEAD_DIM
DECAY_LORA = 64
AAA_LORA = 64
GATE_LORA = 128
DIFF_COLS = 3 * DIFF_WIDTH
RWKV_COLS = 3 * RWKV_WIDTH + DECAY_LORA + AAA_LORA + GATE_LORA
PEER_HEADS = 8
PEER_NKEYS = 128
PEER_TOPK = 16
PEER_QDIM = 256
PEER_HALF = PEER_QDIM // 2
RMS_EPS = 1e-6
RWKV_LN_EPS = 64e-5

LANES = 128
SUBLANES = 8
RW_PAD_COLS = 3 * RWKV_WIDTH + 3 * LANES
VMEM_LIMIT = 56 * 1024 * 1024

RW_CHUNK = 64
RW_CHUNKS_PER_ITER = 4


def _params(sem, vmem=VMEM_LIMIT):
    return pltpu.CompilerParams(dimension_semantics=sem, vmem_limit_bytes=vmem)


def _nt(a, b, precision=None):
    return lax.dot_general(a, b, (((1,), (1,)), ((), ())), preferred_element_type=F32, precision=precision)


def _mm(a, b, precision=None):
    return jnp.dot(a, b, preferred_element_type=F32, precision=precision)


def _ada_kernel(c_ref, w_ref, b_ref, o_ref):
    c = c_ref[...]
    ca = c * jax.nn.sigmoid(c)
    o_ref[...] = _mm(ca, w_ref[...], HIGHEST) + b_ref[...]


def _ada_mod(c, ada_w, ada_b):
    L, D, N = ada_w.shape
    B = c.shape[0]
    tn = 1536
    return pl.pallas_call(
        _ada_kernel,
        out_shape=jax.ShapeDtypeStruct((L, B, N), F32),
        grid=(L, N // tn),
        in_specs=[
            pl.BlockSpec((B, D), lambda l, j: (0, 0)),
            pl.BlockSpec((None, D, tn), lambda l, j: (l, 0, j)),
            pl.BlockSpec((None, 1, tn), lambda l, j: (l, 0, j)),
        ],
        out_specs=pl.BlockSpec((None, B, tn), lambda l, j: (l, 0, j)),
        compiler_params=_params(("parallel", "parallel")),
        name="ada_mod",
    )(c, ada_w, ada_b.reshape(L, 1, N))


ATT_T = 512
QK_DIFF = 2 * DIFF_WIDTH
QK_FOX = 2 * FOX_WIDTH
VT_ROWS = DIFF_WIDTH + FOX_WIDTH
IN_PAD_COLS = QK_DIFF + RW_PAD_COLS + QK_FOX + LANES


def _inproj_kernel(x_ref, g_ref, sh_ref, sc_ref, w_ref, wvt_ref, d_ref, r_ref, f_ref, fl_ref, vtd_ref, vtf_ref, *, ta):
    x = x_ref[...]
    ms = jnp.mean(x * x, axis=-1, keepdims=True)
    y = x * lax.rsqrt(ms + RMS_EPS) * g_ref[...]
    h = (y * (1.0 + sc_ref[...]) + sh_ref[...]).astype(BF16)
    o1 = QK_DIFF
    o2 = o1 + RW_PAD_COLS
    o3 = o2 + QK_FOX
    d_ref[...] = _mm(h, w_ref[:, 0:o1]).astype(BF16)
    r_ref[...] = _mm(h, w_ref[:, o1:o2])
    f_ref[...] = _mm(h, w_ref[:, o2:o3]).astype(BF16)
    fl_ref[...] = _mm(h, w_ref[:, o3:o3 + LANES])
    vt = _nt(wvt_ref[...], h).astype(BF16)
    for s in range(x.shape[0] // ta):
        vtd_ref[s] = vt[0:DIFF_WIDTH, ta * s:ta * (s + 1)]
        vtf_ref[s] = vt[DIFF_WIDTH:, ta * s:ta * (s + 1)]


def _in_proj(x2, g, mod4, w_pad, w_vt, S):
    T, D = x2.shape
    tm = 512 if S % 512 == 0 else S
    ta = min(ATT_T, S)
    nb = S // tm
    row = lambda i: (i, 0)
    return pl.pallas_call(
        functools.partial(_inproj_kernel, ta=ta),
        out_shape=(
            jax.ShapeDtypeStruct((T, QK_DIFF), BF16),
            jax.ShapeDtypeStruct((T, RW_PAD_COLS), F32),
            jax.ShapeDtypeStruct((T, QK_FOX), BF16),
            jax.ShapeDtypeStruct((T, LANES), F32),
            jax.ShapeDtypeStruct((T // ta, DIFF_WIDTH, ta), BF16),
            jax.ShapeDtypeStruct((T // ta, FOX_WIDTH, ta), BF16),
        ),
        grid=(T // tm,),
        in_specs=[
            pl.BlockSpec((tm, D), row),
            pl.BlockSpec((1, D), lambda i: (0, 0)),
            pl.BlockSpec((None, None, 1, D), lambda i: (i // nb, 0, 0, 0)),
            pl.BlockSpec((None, None, 1, D), lambda i: (i // nb, 1, 0, 0)),
            pl.BlockSpec((D, IN_PAD_COLS), lambda i: (0, 0)),
            pl.BlockSpec((VT_ROWS, D), lambda i: (0, 0)),
        ],
        out_specs=(
            pl.BlockSpec((tm, QK_DIFF), row),
            pl.BlockSpec((tm, RW_PAD_COLS), row),
            pl.BlockSpec((tm, QK_FOX), row),
            pl.BlockSpec((tm, LANES), row),
            pl.BlockSpec((tm // ta, DIFF_WIDTH, ta), lambda i: (i, 0, 0)),
            pl.BlockSpec((tm // ta, FOX_WIDTH, ta), lambda i: (i, 0, 0)),
        ),
        compiler_params=_params(("parallel",)),
        name="in_proj",
    )(x2, g.reshape(1, D), mod4, mod4, w_pad, w_vt)


def _pad_w_in(w_in):
    D = w_in.shape[0]
    W = RWKV_WIDTH
    o = DIFF_COLS
    z64 = jnp.zeros((D, LANES - DECAY_LORA), w_in.dtype)
    rw = w_in[:, o:o + RWKV_COLS]
    fx = w_in[:, o + RWKV_COLS:]
    zf = jnp.zeros((D, LANES - FOX_HEADS), w_in.dtype)
    w_pad = jnp.concatenate([
        w_in[:, :QK_DIFF],
        rw[:, :3 * W], rw[:, 3 * W:3 * W + DECAY_LORA], z64,
        rw[:, 3 * W + DECAY_LORA:3 * W + DECAY_LORA + AAA_LORA], z64,
        rw[:, 3 * W + DECAY_LORA + AAA_LORA:],
        fx[:, :QK_FOX], fx[:, 3 * FOX_WIDTH:], zf,
    ], axis=1).astype(BF16)
    w_vt = jnp.concatenate([w_in[:, QK_DIFF:o], fx[:, QK_FOX:3 * FOX_WIDTH]], axis=1).T.astype(BF16)
    return w_pad, w_vt


LOG2E = math.log2(math.e)


ACC_ROWS = LANES + 16


def _scaled_q(q, c):
    return (q.astype(F32) * c).astype(BF16)


def _with_ones(vt):
    return jnp.concatenate([vt, jnp.ones((ACC_ROWS - LANES, vt.shape[1]), vt.dtype)], axis=0)


def _flash_step(s2s, vt1, m_ref, acc_ref):
    n = len(s2s)
    m_old = [m_ref[x] for x in range(n)]
    m_new = [jnp.maximum(m_old[x], jnp.max(s2s[x], axis=0, keepdims=True)) for x in range(n)]
    alpha = [jnp.exp2(m_old[x] - m_new[x]) for x in range(n)]
    p = [jnp.exp2(s2s[x] - m_new[x]).astype(BF16) for x in range(n)]
    pv = [_mm(vt1, p[x]) for x in range(n)]
    for x in range(n):
        acc_ref[x] = alpha[x] * acc_ref[x] + pv[x]
        m_ref[x] = m_new[x]


def _diff_attn_kernel(lam_ref, g_ref, q_ref, k_ref, vt_ref, o_ref, m_ref, acc_ref, *, tq, lam_init):
    i = pl.program_id(2)
    lane = lax.broadcasted_iota(jnp.int32, (1, LANES), 1)
    q = _scaled_q(q_ref[...], (DIFF_QK_DIM ** -0.5) * LOG2E)
    zero = jnp.zeros_like(q)
    qm = [jnp.where((lane >= DIFF_QK_DIM * x) & (lane < DIFF_QK_DIM * (x + 1)), q, zero) for x in range(4)]
    m_ref[...] = jnp.full(m_ref.shape, -jnp.inf, F32)
    acc_ref[...] = jnp.zeros(acc_ref.shape, F32)

    def step(j, diag):
        k = k_ref[pl.ds(pl.multiple_of(j * tq, tq), tq), :]
        vt1 = _with_ones(vt_ref[j])
        if diag:
            keep = (lax.broadcasted_iota(jnp.int32, (tq, tq), 1) >= lax.broadcasted_iota(jnp.int32, (tq, tq), 0))
        s2s = [_nt(k, qm[x]) for x in range(4)]
        if diag:
            s2s = [jnp.where(keep, s2, -jnp.inf) for s2 in s2s]
        _flash_step(s2s, vt1, m_ref, acc_ref)

    def body(j, carry):
        step(j, False)
        return carry

    lax.fori_loop(0, i, body, 0)
    step(i, True)

    lp = lam_ref[...]
    lam = (jnp.exp(jnp.sum(lp[0:1] * lp[1:2], axis=-1, keepdims=True))
           - jnp.exp(jnp.sum(lp[2:3] * lp[3:4], axis=-1, keepdims=True)) + lam_init)
    sm = [acc_ref[x, 0:LANES, :] / acc_ref[x, LANES:LANES + 1, :] for x in range(4)]
    outs = [sm[2 * hh] - lam * sm[2 * hh + 1] for hh in range(2)]
    row = lax.broadcasted_iota(jnp.int32, (LANES, 1), 0)
    o = jnp.where(row < HEAD_DIM, outs[0], outs[1])
    sq = o * o
    ms = jnp.where(row < HEAD_DIM, jnp.sum(sq[0:HEAD_DIM], axis=0, keepdims=True),
                   jnp.sum(sq[HEAD_DIM:], axis=0, keepdims=True)) * (1.0 / HEAD_DIM)
    y = o * lax.rsqrt(ms + RMS_EPS) * g_ref[...] * (1.0 - lam_init)
    o_ref[...] = y.T.astype(o_ref.dtype)


def _diff_attention(dqk, vtd, lam_params, subln_g, layer_idx, B, S):
    T = B * S
    tq = min(ATT_T, S)
    nq = S // tq
    npair = DIFF_HEADS // 2
    lam_init = 0.8 - 0.6 * math.exp(-0.3 * layer_idx)
    g2 = jnp.concatenate([subln_g, subln_g]).reshape(LANES, 1).astype(F32)
    return pl.pallas_call(
        functools.partial(_diff_attn_kernel, tq=tq, lam_init=lam_init),
        out_shape=jax.ShapeDtypeStruct((T, DIFF_WIDTH), BF16),
        grid=(B, npair, nq),
        in_specs=[
            pl.BlockSpec((4, DIFF_QK_DIM), lambda b, p, i: (0, 0)),
            pl.BlockSpec((LANES, 1), lambda b, p, i: (0, 0)),
            pl.BlockSpec((tq, LANES), lambda b, p, i: (b * nq + i, p)),
            pl.BlockSpec((S, LANES), lambda b, p, i: (b, npair + p)),
            pl.BlockSpec((nq, LANES, tq), lambda b, p, i: (b, p, 0)),
        ],
        out_specs=pl.BlockSpec((tq, LANES), lambda b, p, i: (b * nq + i, p)),
        scratch_shapes=[
            pltpu.VMEM((4, 1, tq), F32),
            pltpu.VMEM((4, ACC_ROWS, tq), F32),
        ],
        compiler_params=_params(("parallel", "parallel", "arbitrary")),
        name="diff_attn",
    )(lam_params, g2, dqk, dqk, vtd)


def _fox_cum_kernel(f_ref, b_ref, rep_ref, row_ref, *, S, tc):
    rr = lax.broadcasted_iota(jnp.int32, (tc, tc), 0)
    cc = lax.broadcasted_iota(jnp.int32, (tc, tc), 1)
    tri = (rr >= cc).astype(F32)
    sel_r = lax.broadcasted_iota(jnp.int32, (LANES, LANES), 0)
    carry = jnp.zeros((1, LANES), F32)
    for c in range(S // tc):
        z = f_ref[c * tc:(c + 1) * tc, :] + b_ref[...]
        logf = -(jnp.maximum(-z, 0.0) + jnp.log(1.0 + jnp.exp(-jnp.abs(z))))
        cum = _mm(tri, logf, HIGHEST) + carry
        carry = cum[tc - 1:tc, :]
        row_ref[:, c * tc:(c + 1) * tc] = cum.T[0:8, :]
        for h in range(FOX_HEADS):
            rep_ref[h, c * tc:(c + 1) * tc, :] = _mm(cum, (sel_r == h).astype(F32), HIGHEST)


def _fox_cum(fl, b_f, B, S):
    tc = 256 if S % 256 == 0 else S
    bpad = jnp.zeros((1, LANES), F32).at[0, :FOX_HEADS].set(b_f.astype(F32))
    return pl.pallas_call(
        functools.partial(_fox_cum_kernel, S=S, tc=tc),
        out_shape=(jax.ShapeDtypeStruct((B, FOX_HEADS, S, LANES), F32), jax.ShapeDtypeStruct((B, 8, S), F32)),
        grid=(B,),
        in_specs=[pl.BlockSpec((S, LANES), lambda b: (b, 0)), pl.BlockSpec((1, LANES), lambda b: (0, 0))],
        out_specs=(pl.BlockSpec((None, FOX_HEADS, S, LANES), lambda b: (b, 0, 0, 0)),
                   pl.BlockSpec((None, 8, S), lambda b: (b, 0, 0))),
        compiler_params=_params(("parallel",)),
        name="fox_cum",
    )(fl, bpad)


def _fox_attn_kernel(q_ref, k_ref, vt_ref, c0_ref, c1_ref, cr_ref, o_ref, m_ref, acc_ref, *, tq):
    p_id = pl.program_id(1)
    i = pl.program_id(2)
    lane = lax.broadcasted_iota(jnp.int32, (1, LANES), 1)
    q = _scaled_q(q_ref[...], (HEAD_DIM ** -0.5) * LOG2E)
    zero = jnp.zeros_like(q)
    qm = [jnp.where((lane >= HEAD_DIM * x) & (lane < HEAD_DIM * (x + 1)), q, zero) for x in range(2)]
    ck_refs = (c0_ref, c1_ref)
    cq = [cr_ref[2 * p_id + x, pl.ds(i, 1), :] for x in range(2)]
    m_ref[...] = jnp.full(m_ref.shape, -jnp.inf, F32)
    acc_ref[...] = jnp.zeros(acc_ref.shape, F32)

    def step(j, diag):
        off = pl.multiple_of(j * tq, tq)
        k = k_ref[pl.ds(off, tq), :]
        vt1 = _with_ones(vt_ref[j])
        if diag:
            keep = (lax.broadcasted_iota(jnp.int32, (tq, tq), 1) >= lax.broadcasted_iota(jnp.int32, (tq, tq), 0))
        s2s = []
        for x in range(2):
            ck = ck_refs[x][pl.ds(off, tq), :]
            bias = (cq[x] - jnp.concatenate([ck] * (tq // LANES), axis=1)) * LOG2E
            s2s.append(_nt(k, qm[x]) + bias)
        if diag:
            s2s = [jnp.where(keep, s2, -jnp.inf) for s2 in s2s]
        _flash_step(s2s, vt1, m_ref, acc_ref)

    def body(j, carry):
        step(j, False)
        return carry

    lax.fori_loop(0, i, body, 0)
    step(i, True)
    row = lax.broadcasted_iota(jnp.int32, (LANES, 1), 0)
    sm = [acc_ref[x, 0:LANES, :] / acc_ref[x, LANES:LANES + 1, :] for x in range(2)]
    o = jnp.where(row < HEAD_DIM, sm[0], sm[1])
    o_ref[...] = o.T.astype(o_ref.dtype)


def _fox_attention(fqk, vtf, crep, crow, B, S):
    T = B * S
    tq = min(ATT_T, S)
    nq = S // tq
    npair = FOX_HEADS // 2
    crow4 = crow.reshape(B, 8, nq, tq)
    rep = lambda x: pl.BlockSpec((None, None, S, LANES), lambda b, p, i: (b, 2 * p + x, 0, 0))
    return pl.pallas_call(
        functools.partial(_fox_attn_kernel, tq=tq),
        out_shape=jax.ShapeDtypeStruct((T, FOX_WIDTH), BF16),
        grid=(B, npair, nq),
        in_specs=[
            pl.BlockSpec((tq, LANES), lambda b, p, i: (b * nq + i, p)),
            pl.BlockSpec((S, LANES), lambda b, p, i: (b, npair + p)),
            pl.BlockSpec((nq, LANES, tq), lambda b, p, i: (b, p, 0)),
            rep(0), rep(1),
            pl.BlockSpec((None, 8, nq, tq), lambda b, p, i: (b, 0, 0, 0)),
        ],
        out_specs=pl.BlockSpec((tq, LANES), lambda b, p, i: (b * nq + i, p)),
        scratch_shapes=[
            pltpu.VMEM((2, 1, tq), F32),
            pltpu.VMEM((2, ACC_ROWS, tq), F32),
        ],
        compiler_params=_params(("parallel", "parallel", "arbitrary")),
        name="fox_attn",
    )(fqk, fqk, vtf, crep, crep, crow4)


def _split3(x):
    hi = x.astype(BF16)
    r1 = x - hi.astype(F32)
    mid = r1.astype(BF16)
    lo = (r1 - mid.astype(F32)).astype(BF16)
    return hi, mid, lo


def _seg_sum(x, seg, npair):
    parts = _split3(x)
    return jnp.concatenate(
        [sum(_mm(t[:, LANES * p:LANES * (p + 1)], seg) for t in parts) for p in range(npair)], axis=1)


def _rwkv_kernel(x_ref, mu_ref, w0_ref, w2_ref, a0_ref, a2_ref, g2_ref, kk_ref, ka_ref, rk_ref, lng_ref, lnb_ref,
                 o_ref, carry_ref, st_ref, at_ref, rt_ref, bt_ref, kt_ref, v_ref, wc_ref, y_ref, g_ref, bon_ref,
                 *, tt, prec):
    i = pl.program_id(1)
    W = RWKV_WIDTH
    C = RW_CHUNK
    CPI = math.gcd(RW_CHUNKS_PER_ITER, tt // C)
    npair = RWKV_HEADS // 2

    @pl.when(i == 0)
    def _():
        carry_ref[...] = jnp.zeros(carry_ref.shape, F32)
        st_ref[...] = jnp.zeros(st_ref.shape, F32)

    x = x_ref[...]
    rows = lax.broadcasted_iota(jnp.int32, (tt, 1), 0)
    prev = jnp.where(rows == 0, carry_ref[...], pltpu.roll(x, 1, axis=0))
    carry_ref[...] = x[tt - 1:tt, :]
    xs = x + (prev - x) * mu_ref[...]
    r = xs[:, 0:W]
    k = xs[:, W:2 * W]
    v = xs[:, 2 * W:3 * W]
    xw = xs[:, 3 * W:3 * W + LANES]
    xa = xs[:, 3 * W + LANES:3 * W + 2 * LANES]
    xg = xs[:, 3 * W + 2 * LANES:]
    wl = w0_ref[...] + _mm(jnp.tanh(xw), w2_ref[...], HIGHEST)
    w = -(jnp.maximum(-wl, 0.0) + jnp.log(1.0 + jnp.exp(-jnp.abs(wl)))) - 0.5
    logdec = -jnp.exp(w)
    a = jax.nn.sigmoid(a0_ref[...] + _mm(xa, a2_ref[...], HIGHEST))
    g_ref[...] = _mm(jax.nn.sigmoid(xg), g2_ref[...], HIGHEST)

    r_i = lax.broadcasted_iota(jnp.int32, (LANES, LANES), 0)
    c_i = lax.broadcasted_iota(jnp.int32, (LANES, LANES), 1)
    seg = ((r_i // HEAD_DIM) == (c_i // HEAD_DIM)).astype(BF16)
    kkv = k * kk_ref[...]
    kkn = kkv / jnp.maximum(jnp.sqrt(_seg_sum(kkv * kkv, seg, npair)), 1e-12)
    k2 = k * (1.0 + (a - 1.0) * ka_ref[...])
    bon_ref[...] = _seg_sum(r * k2 * rk_ref[...], seg, npair) * v

    rt_i = lax.broadcasted_iota(jnp.int32, (tt, tt), 0)
    ct_i = lax.broadcasted_iota(jnp.int32, (tt, tt), 1)
    tri = (((rt_i // C) == (ct_i // C)) & (rt_i >= ct_i)).astype(BF16)
    cum = sum(_mm(tri, t) for t in _split3(logdec))
    winv = jnp.exp(-cum)
    wcum = jnp.exp(cum)
    at_ref[...] = -kkn * jnp.exp(cum - logdec)
    bt_ref[...] = kkn * a * winv
    kt_ref[...] = k2 * winv
    rt_ref[...] = r * wcum
    v_ref[...] = v
    wc_ref[...] = wcum

    lane = lax.broadcasted_iota(jnp.int32, (1, LANES), 1)
    lo = lane < HEAD_DIM
    tpos = r_i % C
    ipos = c_i % C
    strict = tpos > ipos
    incl = tpos >= ipos
    eye = r_i == c_i

    def stack2(m):
        return jnp.concatenate([jnp.where(lo, m, 0.0), jnp.where(lo, 0.0, m)], axis=0)

    def cast(m):
        return m if prec is not None else m.astype(BF16)

    def mm(p, q):
        return _mm(cast(p), cast(q), prec)

    def nt(p, q):
        return _nt(cast(p), cast(q), prec)

    def chunks(cc, carry):
        units = [(ci, p) for ci in range(CPI) for p in range(npair)]
        r0 = [pl.multiple_of((cc * CPI + ci) * C, C) for ci in range(CPI)]
        ld = lambda ref, ci, p: stack2(ref[pl.ds(r0[ci], C), LANES * p:LANES * (p + 1)])
        a2 = {u: ld(at_ref, *u) for u in units}
        r2 = {u: ld(rt_ref, *u) for u in units}
        b2 = {u: ld(bt_ref, *u) for u in units}
        k2s = {u: ld(kt_ref, *u) for u in units}
        v2 = {u: ld(v_ref, *u) for u in units}
        wl = {(ci, p): wc_ref[pl.ds(pl.multiple_of(r0[ci] + C - 8, 8), 8), LANES * p:LANES * (p + 1)][7:8, :]
              for (ci, p) in units}
        ar = {u: jnp.concatenate([a2[u], r2[u]], axis=0) for u in units}
        mb = {u: nt(ar[u], b2[u]) for u in units}
        mk = {u: nt(ar[u], k2s[u]) for u in units}
        lab = {u: jnp.where(strict, mb[u][0:LANES], 0.0) for u in units}
        mrb = {u: jnp.where(incl, mb[u][LANES:], 0.0) for u in units}
        lak = {u: jnp.where(strict, mk[u][0:LANES], 0.0) for u in units}
        mrk = {u: jnp.where(incl, mk[u][LANES:], 0.0) for u in units}
        xx = {u: jnp.concatenate([a2[u], mm(lak[u], v2[u])], axis=1) for u in units}
        lp = lab
        for it in range(6):
            xx = {u: xx[u] + mm(lp[u], xx[u]) for u in units}
            if it < 5:
                lp = {u: mm(lp[u], lp[u]) for u in units}
        mq = {u: mm(mrb[u], xx[u]) for u in units}
        mv = {u: mm(mrk[u], v2[u]) for u in units}
        bx = {u: mm((b2[u] * wl[u]).T, xx[u]) for u in units}
        kv = {u: mm((k2s[u] * wl[u]).T, v2[u]) for u in units}
        st = [st_ref[p] for p in range(npair)]
        for (ci, p) in units:
            u = (ci, p)
            q1 = r2[u] + mq[u][:, 0:LANES]
            q2 = mq[u][:, LANES:] + mv[u]
            gmat = jnp.where(eye, jnp.broadcast_to(wl[u], (LANES, LANES)), 0.0) + bx[u][:, 0:LANES]
            hmat = bx[u][:, LANES:] + kv[u]
            gs = mm(jnp.concatenate([gmat, q1], axis=0), st[p])
            st[p] = gs[0:LANES] + hmat
            yy = gs[LANES:] + q2
            y_ref[pl.ds(r0[ci], C), LANES * p:LANES * (p + 1)] = yy[0:C] + yy[C:]
        for p in range(npair):
            st_ref[p] = st[p]
        return carry

    lax.fori_loop(0, tt // (C * CPI), chunks, 0)

    y = y_ref[...]
    inv = 1.0 / HEAD_DIM
    mean = _seg_sum(y, seg, npair) * inv
    yc = y - mean
    var = _seg_sum(yc * yc, seg, npair) * inv
    yn = yc * lax.rsqrt(var + RWKV_LN_EPS) * lng_ref[...] + lnb_ref[...]
    o_ref[...] = ((yn + bon_ref[...]) * g_ref[...]).astype(o_ref.dtype)


def _rwkv(rcols, mu, w0, w2, a0, a2, g2, k_k, k_a, r_k, ln_g, ln_b, B, S, prec=HIGHEST):
    T = B * S
    W = RWKV_WIDTH
    tt = 512 if S % 512 == 0 else S
    nt_ = S // tt
    npair = RWKV_HEADS // 2
    pad = LANES - DECAY_LORA
    mu_p = jnp.concatenate([mu[:3 * W], mu[3 * W:3 * W + DECAY_LORA], jnp.zeros((pad,), F32),
                            mu[3 * W + DECAY_LORA:3 * W + DECAY_LORA + AAA_LORA], jnp.zeros((pad,), F32),
                            mu[3 * W + DECAY_LORA + AAA_LORA:]]).reshape(1, RW_PAD_COLS)
    w2p = jnp.concatenate([w2, jnp.zeros((pad, W), F32)], axis=0)
    a2p = jnp.concatenate([a2, jnp.zeros((pad, W), F32)], axis=0)
    vec = lambda t: t.reshape(1, W).astype(F32)
    full = lambda shape: pl.BlockSpec(shape, lambda b, i: (0,) * len(shape))
    sc = lambda: pltpu.VMEM((tt, W), F32)
    return pl.pallas_call(
        functools.partial(_rwkv_kernel, tt=tt, prec=prec),
        out_shape=jax.ShapeDtypeStruct((T, W), BF16),
        grid=(B, nt_),
        in_specs=[
            pl.BlockSpec((tt, RW_PAD_COLS), lambda b, i: (b * nt_ + i, 0)),
            full((1, RW_PAD_COLS)), full((1, W)), full((LANES, W)), full((1, W)), full((LANES, W)),
            full((GATE_LORA, W)), full((1, W)), full((1, W)), full((1, W)), full((1, W)), full((1, W)),
        ],
        out_specs=pl.BlockSpec((tt, W), lambda b, i: (b * nt_ + i, 0)),
        scratch_shapes=[
            pltpu.VMEM((1, RW_PAD_COLS), F32),
            pltpu.VMEM((npair, LANES, LANES), F32),
            sc(), sc(), sc(), sc(), sc(), sc(), sc(), sc(), sc(),
        ],
        compiler_params=_params(("parallel", "arbitrary")),
        name="rwkv7",
    )(rcols, mu_p, vec(w0), w2p, vec(a0), a2p, g2, vec(k_k), vec(k_a), vec(r_k), vec(ln_g), vec(ln_b))


def _outproj_kernel(ya_ref, yb_ref, yc_ref, x_ref, w_ref, gm_ref, g_ref, sh_ref, sc_ref, xo_ref, ho_ref):
    o1 = DIFF_WIDTH
    o2 = o1 + RWKV_WIDTH
    mix = (_mm(ya_ref[...], w_ref[0:o1, :]) + _mm(yb_ref[...], w_ref[o1:o2, :])
           + _mm(yc_ref[...], w_ref[o2:, :]))
    xn = x_ref[...] + gm_ref[...] * mix
    xo_ref[...] = xn
    ms = jnp.mean(xn * xn, axis=-1, keepdims=True)
    y = xn * lax.rsqrt(ms + RMS_EPS) * g_ref[...]
    ho_ref[...] = y * (1.0 + sc_ref[...]) + sh_ref[...]


def _out_proj(ya, yb, yc, x2, w_out, g, mod4, S):
    T, D = x2.shape
    tm = 512 if S % 512 == 0 else S
    nb = S // tm
    row = lambda i: (i, 0)
    modspec = lambda which: pl.BlockSpec((None, None, 1, D), lambda i: (i // nb, which, 0, 0))
    return pl.pallas_call(
        _outproj_kernel,
        out_shape=(jax.ShapeDtypeStruct((T, D), F32), jax.ShapeDtypeStruct((T, D), F32)),
        grid=(T // tm,),
        in_specs=[
            pl.BlockSpec((tm, DIFF_WIDTH), row), pl.BlockSpec((tm, RWKV_WIDTH), row), pl.BlockSpec((tm, FOX_WIDTH), row),
            pl.BlockSpec((tm, D), row),
            pl.BlockSpec((D, D), lambda i: (0, 0)),
            modspec(2),
            pl.BlockSpec((1, D), lambda i: (0, 0)),
            modspec(3), modspec(4),
        ],
        out_specs=(pl.BlockSpec((tm, D), row), pl.BlockSpec((tm, D), row)),
        compiler_params=_params(("parallel",)),
        name="out_proj",
    )(ya, yb, yc, x2, w_out.astype(BF16), mod4, g.reshape(1, D), mod4, mod4)


def _top16(s, iota_f, n):
    vals, poss = [], []
    for _ in range(PEER_TOPK):
        m = jnp.max(s, axis=0, keepdims=True)
        pos = jnp.min(jnp.where(s == m, iota_f, float(n)), axis=0, keepdims=True)
        vals.append(m)
        poss.append(pos)
        s = jnp.where(iota_f == pos, -jnp.inf, s)
    return jnp.concatenate(vals, axis=0), jnp.concatenate(poss, axis=0)


ROUTE_UNROLL = 8
PEER_NCAND = 56


def _peer_cand_tables():
    K = PEER_TOPK
    pairs = [(a, b) for a in range(K) for b in range(K) if (a + 1) * (b + 1) <= K]
    n = PEER_NCAND
    p0 = [[0.0] * K for _ in range(n)]
    p1 = [[0.0] * K for _ in range(n)]
    pad = [0.0] * n
    pos = [float(K * K + r) for r in range(n)]
    for r, (a, b) in enumerate(pairs):
        p0[r][a] = 1.0
        p1[r][b] = 1.0
        pos[r] = float(a * K + b)
    for r in range(len(pairs), n):
        pad[r] = -float("inf")
    col = lambda v: jnp.broadcast_to(jnp.asarray(v, F32)[:, None], (n, LANES))
    return jnp.asarray(p0, F32), jnp.asarray(p1, F32), col(pad), col(pos)


def _peer_route_kernel(h_ref, wq_ref, sk_ref, p0_ref, p1_ref, cpad_ref, cpos_ref, idx_ref, gate_ref, q_scr, e_scr,
                       g_scr):
    K = PEER_TOPK
    hb = h_ref[...].astype(BF16)
    q = _mm(hb, wq_ref[...])
    for hc in range(2 * PEER_HEADS):
        q_scr[hc] = q[:, LANES * hc:LANES * (hc + 1)].astype(BF16)
    iota_n = lax.broadcasted_iota(jnp.int32, (PEER_NKEYS, LANES), 0).astype(F32)
    cpos = cpos_ref[...]

    def one_head(h):
        sv0, si0 = _top16(_nt(sk_ref[2 * h], q_scr[2 * h]), iota_n, PEER_NKEYS)
        sv1, si1 = _top16(_nt(sk_ref[2 * h + 1], q_scr[2 * h + 1]), iota_n, PEER_NKEYS)
        cand = _mm(p0_ref[...], sv0, HIGHEST) + _mm(p1_ref[...], sv1, HIGHEST) + cpad_ref[...]
        cidx = _mm(p0_ref[...], si0) * float(PEER_NKEYS) + _mm(p1_ref[...], si1)
        fv, es = [], []
        for _ in range(K):
            m = jnp.max(cand, axis=0, keepdims=True)
            pos = jnp.min(jnp.where(cand == m, cpos, float(2 * K * K)), axis=0, keepdims=True)
            hit = cpos == pos
            fv.append(m)
            es.append(jnp.max(jnp.where(hit, cidx, -1.0), axis=0, keepdims=True))
            cand = jnp.where(hit, -jnp.inf, cand)
        fv = jnp.concatenate(fv, axis=0)
        ex = jnp.exp(fv - fv[0:1, :])
        g_scr[h] = ex / jnp.sum(ex, axis=0, keepdims=True)
        e_scr[h] = jnp.concatenate(es, axis=0)

    def heads(hh, carry):
        for j in range(ROUTE_UNROLL):
            one_head(hh * ROUTE_UNROLL + j)
        return carry

    lax.fori_loop(0, PEER_HEADS // ROUTE_UNROLL, heads, 0)
    e = e_scr[...].reshape(PEER_HEADS * K, LANES)
    idx_ref[...] = e.T.astype(jnp.int32)
    gate_ref[...] = g_scr[...].reshape(PEER_HEADS * K, LANES)


def _peer_route(h2, wq, subkeys, tok0=0, ntok=None):
    T, D = h2.shape
    T = T - tok0 if ntok is None else ntok
    tm = LANES
    blk0 = tok0 // tm
    nq = 2 * PEER_HEADS
    sk = subkeys.reshape(nq, PEER_NKEYS, PEER_HALF).astype(BF16)
    p0, p1, cpad, cpos = _peer_cand_tables()
    const = lambda shape: pl.BlockSpec(shape, lambda i: (0,) * len(shape))
    return pl.pallas_call(
        _peer_route_kernel,
        out_shape=(jax.ShapeDtypeStruct((T, PEER_HEADS * PEER_TOPK), jnp.int32),
                   jax.ShapeDtypeStruct((T // tm, PEER_HEADS * PEER_TOPK, tm), F32)),
        grid=(T // tm,),
        in_specs=[
            pl.BlockSpec((tm, D), lambda i: (blk0 + i, 0)),
            const((D, nq * PEER_HALF)),
            const((nq, PEER_NKEYS, PEER_HALF)),
            const((PEER_NCAND, PEER_TOPK)), const((PEER_NCAND, PEER_TOPK)),
            const((PEER_NCAND, LANES)), const((PEER_NCAND, LANES)),
        ],
        out_specs=(pl.BlockSpec((tm, PEER_HEADS * PEER_TOPK), lambda i: (i, 0)),
                   pl.BlockSpec((None, PEER_HEADS * PEER_TOPK, tm), lambda i: (i, 0, 0))),
        scratch_shapes=[
            pltpu.VMEM((nq, tm, PEER_HALF), BF16),
            pltpu.VMEM((PEER_HEADS, PEER_TOPK, tm), F32),
            pltpu.VMEM((PEER_HEADS, PEER_TOPK, tm), F32),
        ],
        compiler_params=_params(("parallel",)),
        name="peer_route",
    )(h2, wq.astype(BF16), sk, p0, p1, cpad, cpos)


PEER_G = 16
PEER_SLOTS = PEER_HEADS * PEER_TOPK


def _peer_eval_kernel(idx_ref, idxn_ref, gate_ref, h_ref, x_ref, gf_ref, fg_ref, uv_ref, o_ref, buf, sem, *, final):
    G = PEER_G
    R = G * PEER_SLOTS
    D = D_MODEL
    tiles = PEER_SLOTS // SUBLANES
    i = pl.program_id(0)
    n = pl.num_programs(0)

    def start(ids, off, s, t, u):
        pltpu.make_async_copy(uv_ref.at[ids[off + t * SUBLANES + u]], buf.at[s, t, pl.ds(u, 1), :],
                              sem.at[s]).start(priority=u % 2)

    def wait(s):
        pltpu.make_async_copy(buf.at[s], buf.at[s], sem.at[s]).wait()

    @pl.when(i == 0)
    def _():
        def body(t, carry):
            for u in range(SUBLANES):
                start(idx_ref, 0, 0, t, u)
            return carry
        lax.fori_loop(0, R // SUBLANES, body, 0)

    lane = lax.broadcasted_iota(jnp.int32, (1, LANES), 1)
    tbase = (i % (LANES // (2 * G))) * (2 * G)
    gate = gate_ref[...]
    outs = []
    for s in range(2):
        wait(s)
        nxt_ids, nxt_off = (idx_ref, R) if s == 0 else (idxn_ref, 0)
        for g in range(G):
            for t in range(tiles * g, tiles * (g + 1)):
                for u in range(SUBLANES):
                    start(nxt_ids, nxt_off, 1 - s, t, u)
            w_rows = buf[s, tiles * g:tiles * (g + 1)].reshape(PEER_SLOTS, D)
            u_rows = lax.bitcast_convert_type(w_rows & jnp.uint32(0xFFFF0000), F32)
            prod = u_rows * h_ref[G * s + g:G * s + g + 1, :]
            part = prod[:, 0:LANES]
            for c in range(1, D // LANES):
                part = part + prod[:, LANES * c:LANES * (c + 1)]
            act = jnp.sum(part, axis=1, keepdims=True)
            gcol = jnp.sum(jnp.where(lane == tbase + G * s + g, gate, 0.0), axis=1, keepdims=True)
            coef = gcol * (0.5 * act * (1.0 + lax.erf(act * (2.0 ** -0.5))))
            v_rows = lax.bitcast_convert_type(w_rows << 16, F32)
            outs.append(jnp.sum(v_rows * coef, axis=0, keepdims=True))
    xn = x_ref[...] + gf_ref[...] * jnp.concatenate(outs, axis=0)
    if final:
        ms = jnp.mean(xn * xn, axis=-1, keepdims=True)
        xn = xn * lax.rsqrt(ms + RMS_EPS) * fg_ref[...]
    o_ref[...] = xn

    @pl.when(i == n - 1)
    def _():
        wait(0)


def _peer_eval(eidx, gate_t, h2, x2, mod4, final_g, uv, S, final, ntok=None):
    T, D = x2.shape
    T = T if ntok is None else ntok
    G = PEER_G
    R = G * PEER_SLOTS
    n = T // (2 * G)
    return pl.pallas_call(
        functools.partial(_peer_eval_kernel, final=final),
        out_shape=jax.ShapeDtypeStruct(x2.shape, F32),
        input_output_aliases={4: 0},
        grid=(n,),
        in_specs=[
            pl.BlockSpec((2 * R,), lambda i: (i,), memory_space=pltpu.SMEM),
            pl.BlockSpec((R,), lambda i: (jnp.minimum(2 * i + 2, 2 * n - 2),), memory_space=pltpu.SMEM),
            pl.BlockSpec((None, PEER_SLOTS, LANES), lambda i: (i // (LANES // (2 * G)), 0, 0)),
            pl.BlockSpec((2 * G, D), lambda i: (i, 0)),
            pl.BlockSpec((2 * G, D), lambda i: (i, 0)),
            pl.BlockSpec((None, None, 1, D), lambda i: (i // (S // (2 * G)), 5, 0, 0)),
            pl.BlockSpec((1, D), lambda i: (0, 0)),
            pl.BlockSpec(memory_space=pl.ANY),
        ],
        out_specs=pl.BlockSpec((2 * G, D), lambda i: (i, 0)),
        scratch_shapes=[pltpu.VMEM((2, R // SUBLANES, SUBLANES, D), jnp.uint32), pltpu.SemaphoreType.DMA((2,))],
        compiler_params=_params(("arbitrary",)),
        name="peer_eval",
    )(eidx.reshape(-1), eidx.reshape(-1), gate_t, h2, x2, mod4, final_g.reshape(1, D), uv)


SC_WORKERS = 32
SC_WINDOW = 128
SC_ROWS = 32
PEER_GD = 16
SC_SHARE_NUM, SC_SHARE_DEN = 11, 16
SC_PARTS = 16
BATCH_GROUPS = 1


def _sc_gather(tab, idx):
    n = idx.shape[0]
    width = tab.shape[1]
    per = n // SC_WORKERS
    nsub = SC_WINDOW // SC_ROWS
    mesh = plsc.VectorSubcoreMesh(core_axis_name="core", subcore_axis_name="subcore")

    @pl.kernel(out_type=jax.ShapeDtypeStruct((n, width), tab.dtype), mesh=mesh,
               scratch_types=[pltpu.VMEM((SC_WINDOW,), jnp.int32), pltpu.VMEM((2, SC_ROWS, width), tab.dtype),
                              pltpu.SemaphoreType.DMA((2,)), pltpu.SemaphoreType.DMA((2,))])
    def gather(x_hbm, i_hbm, o_hbm, idx_v, rows_v, gsem, wsem):
        wid = lax.axis_index("core") * (SC_WORKERS // 2) + lax.axis_index("subcore")

        def start_gather(k):
            return pltpu.async_copy(x_hbm.at[idx_v.at[pl.ds(k * SC_ROWS, SC_ROWS)]], rows_v.at[k % 2], gsem.at[k % 2])

        @pl.loop(0, per // SC_WINDOW)
        def _(w):
            base = wid * per + w * SC_WINDOW
            pltpu.sync_copy(i_hbm.at[pl.ds(base, SC_WINDOW)], idx_v)
            gat = [start_gather(0)] + [None] * (nsub - 1)
            wrt = [None] * nsub
            for k in range(nsub):
                if k + 1 < nsub:
                    if k >= 1:
                        wrt[k - 1].wait()
                    gat[k + 1] = start_gather(k + 1)
                gat[k].wait()
                wrt[k] = pltpu.async_copy(rows_v.at[k % 2], o_hbm.at[pl.ds(base + k * SC_ROWS, SC_ROWS)],
                                          wsem.at[k % 2])
            for k in range(max(nsub - 2, 0), nsub):
                wrt[k].wait()

    return gather(tab, idx)


def _peer_dense_kernel(rows_ref, gate_ref, h_ref, x_ref, gf_ref, fg_ref, o_ref, *, final):
    G = PEER_GD
    D = D_MODEL
    i = pl.program_id(0)
    lane = lax.broadcasted_iota(jnp.int32, (1, LANES), 1)
    tbase = (i % (LANES // G)) * G
    gate = gate_ref[...]
    outs = []
    for g in range(G):
        w_rows = rows_ref[PEER_SLOTS * g:PEER_SLOTS * (g + 1), :]
        u_rows = lax.bitcast_convert_type(w_rows & jnp.uint32(0xFFFF0000), F32)
        prod = u_rows * h_ref[g:g + 1, :]
        part = prod[:, 0:LANES]
        for c in range(1, D // LANES):
            part = part + prod[:, LANES * c:LANES * (c + 1)]
        act = jnp.sum(part, axis=1, keepdims=True)
        gcol = jnp.sum(jnp.where(lane == tbase + g, gate, 0.0), axis=1, keepdims=True)
        coef = gcol * (0.5 * act * (1.0 + lax.erf(act * (2.0 ** -0.5))))
        v_rows = lax.bitcast_convert_type(w_rows << 16, F32)
        outs.append(jnp.sum(v_rows * coef, axis=0, keepdims=True))
    xn = x_ref[...] + gf_ref[...] * jnp.concatenate(outs, axis=0)
    if final:
        ms = jnp.mean(xn * xn, axis=-1, keepdims=True)
        xn = xn * lax.rsqrt(ms + RMS_EPS) * fg_ref[...]
    o_ref[...] = xn


def _peer_eval_dense(rows, gate_t, h2, x2, mod4, final_g, S, final, tok0):
    T, D = x2.shape
    G = PEER_GD
    tb = rows.shape[0] // PEER_SLOTS
    assert tok0 % LANES == 0 and tb % LANES == 0
    blk0 = tok0 // G
    return pl.pallas_call(
        functools.partial(_peer_dense_kernel, final=final),
        out_shape=jax.ShapeDtypeStruct(x2.shape, F32),
        input_output_aliases={3: 0},
        grid=(tb // G,),
        in_specs=[
            pl.BlockSpec((G * PEER_SLOTS, D), lambda i: (i, 0)),
            pl.BlockSpec((None, PEER_SLOTS, LANES), lambda i: (i // (LANES // G), 0, 0)),
            pl.BlockSpec((G, D), lambda i: (blk0 + i, 0)),
            pl.BlockSpec((G, D), lambda i: (blk0 + i, 0)),
            pl.BlockSpec((None, None, 1, D), lambda i: ((blk0 + i) // (S // G), 5, 0, 0)),
            pl.BlockSpec((1, D), lambda i: (0, 0)),
        ],
        out_specs=pl.BlockSpec((G, D), lambda i: (blk0 + i, 0)),
        compiler_params=_params(("parallel",)),
        name="peer_dense",
    )(rows, gate_t, h2, x2, mod4, final_g.reshape(1, D))


def _pack_uv(u, v):
    hi = lax.bitcast_convert_type(u.astype(BF16), jnp.uint16).astype(jnp.uint32) << 16
    lo = lax.bitcast_convert_type(v.astype(BF16), jnp.uint16).astype(jnp.uint32)
    return (hi | lo).reshape(u.shape[0], 1, u.shape[1])


def kernel(x, c, norm_mix_g, norm_ffn_g, final_norm_g, ada_w, ada_b, w_in, w_out, dif_lam, dif_subln_g, rw_mu, rw_w0,
           rw_w2, rw_a0, rw_a2, rw_g2, rw_kk, rw_ka, rw_rk, rw_ln_g, rw_ln_b, fox_bf, peer_wq, peer_subkeys, peer_u,
           peer_v):
    B, S, D = x.shape
    depth = ada_w.shape[0]
    mod = _ada_mod(c, ada_w, ada_b)
    ngrp = BATCH_GROUPS if B % BATCH_GROUPS == 0 else 1
    bg = B // ngrp
    tg = bg * S
    prep = [(_pad_w_in(w_in[l]), _pack_uv(peer_u[l], peer_v[l])) for l in range(depth)]

    def mix_route(l, g, x2):
        mod4 = mod[l, g * bg:(g + 1) * bg].reshape(bg, 6, 1, D)
        (w_pad, w_vt), uv = prep[l]
        dqk, rcols, fqk, fl, vtd, vtf = _in_proj(x2, norm_mix_g[l], mod4, w_pad, w_vt, S)
        ya = _diff_attention(dqk, vtd, dif_lam[l], dif_subln_g[l], l, bg, S)
        yb = _rwkv(rcols, rw_mu[l], rw_w0[l], rw_w2[l], rw_a0[l], rw_a2[l], rw_g2[l], rw_kk[l], rw_ka[l],
                   rw_rk[l].reshape(-1), rw_ln_g[l], rw_ln_b[l], bg, S, prec=None)
        crep, crow = _fox_cum(fl, fox_bf[l], bg, S)
        yc = _fox_attention(fqk, vtf, crep, crow, bg, S)
        x2, h2 = _out_proj(ya, yb, yc, x2, w_out[l], norm_ffn_g[l], mod4, S)
        unit = SC_PARTS * LANES
        tb = (tg * SC_SHARE_NUM // SC_SHARE_DEN) // unit * unit
        ta = tg - tb
        bounds = [ta + tb * j // SC_PARTS for j in range(SC_PARTS + 1)]
        parts = []
        for t0, t1 in zip(bounds[:-1], bounds[1:]):
            if t1 == t0:
                continue
            e_p, g_p = _peer_route(h2, peer_wq[l], peer_subkeys[l], t0, t1 - t0)
            parts.append((_sc_gather(uv.reshape(-1, D), e_p.reshape(-1)), g_p, t0))
        e_a, g_a = _peer_route(h2, peer_wq[l], peer_subkeys[l], 0, ta)
        return dict(x2=x2, h2=h2, mod4=mod4, uv=uv, ta=ta, e_a=e_a, g_a=g_a, parts=parts)

    def evaluate(l, st):
        final = l == depth - 1
        x2 = _peer_eval(st["e_a"], st["g_a"], st["h2"], st["x2"], st["mod4"], final_norm_g, st["uv"], S, final,
                        ntok=st["ta"])
        for r, g_p, t0 in st["parts"]:
            x2 = _peer_eval_dense(r, g_p, st["h2"], x2, st["mod4"], final_norm_g, S, final, t0)
        return x2

    xs = [x[g * bg:(g + 1) * bg].reshape(tg, D) for g in range(ngrp)]
    pending = []
    for l in range(depth):
        for g in range(ngrp):
            pending.append((l, g, mix_route(l, g, xs[g])))
            if len(pending) == ngrp:
                l0, g0, st = pending.pop(0)
                xs[g0] = evaluate(l0, st)
    for l0, g0, st in pending:
        xs[g0] = evaluate(l0, st)
    return jnp.concatenate(xs, axis=0).reshape(B, S, D)
```

```python
import functools
import math

import jax
import jax.numpy as jnp
from jax import lax
from jax.experimental import pallas as pl
from jax.experimental.pallas import tpu as pltpu
from jax.experimental.pallas import tpu_sc as plsc

F32 = jnp.float32
BF16 = jnp.bfloat16
HIGHEST = lax.Precision.HIGHEST

D_MODEL = 1024
HEAD_DIM = 64
DIFF_HEADS = 6
DIFF_QK_DIM = HEAD_DIM // 2
RWKV_HEADS = 6
FOX_HEADS = 4
DIFF_WIDTH = DIFF_HEADS * HEAD_DIM
RWKV_WIDTH = RWKV_HEADS * HEAD_DIM
FOX_WIDTH = FOX_HEADS * HEAD_DIM
DECAY_LORA = 64
AAA_LORA = 64
GATE_LORA = 128
DIFF_COLS = 3 * DIFF_WIDTH
RWKV_COLS = 3 * RWKV_WIDTH + DECAY_LORA + AAA_LORA + GATE_LORA
PEER_HEADS = 8
PEER_NKEYS = 128
PEER_TOPK = 16
PEER_QDIM = 256
PEER_HALF = PEER_QDIM // 2
RMS_EPS = 1e-6
RWKV_LN_EPS = 64e-5

LANES = 128
SUBLANES = 8
RW_PAD_COLS = 3 * RWKV_WIDTH + 3 * LANES
VMEM_LIMIT = 56 * 1024 * 1024

RW_CHUNK = 64
RW_CHUNKS_PER_ITER = 4


def _params(sem, vmem=VMEM_LIMIT):
    return pltpu.CompilerParams(dimension_semantics=sem, vmem_limit_bytes=vmem)


def _nt(a, b, precision=None):
    return lax.dot_general(a, b, (((1,), (1,)), ((), ())), preferred_element_type=F32, precision=precision)


def _mm(a, b, precision=None):
    return jnp.dot(a, b, preferred_element_type=F32, precision=precision)


def _ada_kernel(c_ref, w_ref, b_ref, o_ref):
    c = c_ref[...]
    ca = c * jax.nn.sigmoid(c)
    o_ref[...] = _mm(ca, w_ref[...], HIGHEST) + b_ref[...]


def _ada_mod(c, ada_w, ada_b):
    L, D, N = ada_w.shape
    B = c.shape[0]
    tn = 1536
    return pl.pallas_call(
        _ada_kernel,
        out_shape=jax.ShapeDtypeStruct((L, B, N), F32),
        grid=(L, N // tn),
        in_specs=[
            pl.BlockSpec((B, D), lambda l, j: (0, 0)),
            pl.BlockSpec((None, D, tn), lambda l, j: (l, 0, j)),
            pl.BlockSpec((None, 1, tn), lambda l, j: (l, 0, j)),
        ],
        out_specs=pl.BlockSpec((None, B, tn), lambda l, j: (l, 0, j)),
        compiler_params=_params(("parallel", "parallel")),
        name="ada_mod",
    )(c, ada_w, ada_b.reshape(L, 1, N))


ATT_T = 512
QK_DIFF = 2 * DIFF_WIDTH
QK_FOX = 2 * FOX_WIDTH
VT_ROWS = DIFF_WIDTH + FOX_WIDTH
IN_PAD_COLS = QK_DIFF + RW_PAD_COLS + QK_FOX + LANES


def _inproj_kernel(x_ref, g_ref, sh_ref, sc_ref, w_ref, wvt_ref, d_ref, r_ref, f_ref, fl_ref, vtd_ref, vtf_ref, *, ta):
    x = x_ref[...]
    ms = jnp.mean(x * x, axis=-1, keepdims=True)
    y = x * lax.rsqrt(ms + RMS_EPS) * g_ref[...]
    h = (y * (1.0 + sc_ref[...]) + sh_ref[...]).astype(BF16)
    o1 = QK_DIFF
    o2 = o1 + RW_PAD_COLS
    o3 = o2 + QK_FOX
    d_ref[...] = _mm(h, w_ref[:, 0:o1]).astype(BF16)
    r_ref[...] = _mm(h, w_ref[:, o1:o2])
    f_ref[...] = _mm(h, w_ref[:, o2:o3]).astype(BF16)
    fl_ref[...] = _mm(h, w_ref[:, o3:o3 + LANES])
    vt = _nt(wvt_ref[...], h).astype(BF16)
    for s in range(x.shape[0] // ta):
        vtd_ref[s] = vt[0:DIFF_WIDTH, ta * s:ta * (s + 1)]
        vtf_ref[s] = vt[DIFF_WIDTH:, ta * s:ta * (s + 1)]


def _in_proj(x2, g, mod4, w_pad, w_vt, S):
    T, D = x2.shape
    tm = 512 if S % 512 == 0 else S
    ta = min(ATT_T, S)
    nb = S // tm
    row = lambda i: (i, 0)
    return pl.pallas_call(
        functools.partial(_inproj_kernel, ta=ta),
        out_shape=(
            jax.ShapeDtypeStruct((T, QK_DIFF), BF16),
            jax.ShapeDtypeStruct((T, RW_PAD_COLS), F32),
            jax.ShapeDtypeStruct((T, QK_FOX), BF16),
            jax.ShapeDtypeStruct((T, LANES), F32),
            jax.ShapeDtypeStruct((T // ta, DIFF_WIDTH, ta), BF16),
            jax.ShapeDtypeStruct((T // ta, FOX_WIDTH, ta), BF16),
        ),
        grid=(T // tm,),
        in_specs=[
            pl.BlockSpec((tm, D), row),
            pl.BlockSpec((1, D), lambda i: (0, 0)),
            pl.BlockSpec((None, None, 1, D), lambda i: (i // nb, 0, 0, 0)),
            pl.BlockSpec((None, None, 1, D), lambda i: (i // nb, 1, 0, 0)),
            pl.BlockSpec((D, IN_PAD_COLS), lambda i: (0, 0)),
            pl.BlockSpec((VT_ROWS, D), lambda i: (0, 0)),
        ],
        out_specs=(
            pl.BlockSpec((tm, QK_DIFF), row),
            pl.BlockSpec((tm, RW_PAD_COLS), row),
            pl.BlockSpec((tm, QK_FOX), row),
            pl.BlockSpec((tm, LANES), row),
            pl.BlockSpec((tm // ta, DIFF_WIDTH, ta), lambda i: (i, 0, 0)),
            pl.BlockSpec((tm // ta, FOX_WIDTH, ta), lambda i: (i, 0, 0)),
        ),
        compiler_params=_params(("parallel",)),
        name="in_proj",
    )(x2, g.reshape(1, D), mod4, mod4, w_pad, w_vt)


def _pad_w_in(w_in):
    D = w_in.shape[0]
    W = RWKV_WIDTH
    o = DIFF_COLS
    z64 = jnp.zeros((D, LANES - DECAY_LORA), w_in.dtype)
    rw = w_in[:, o:o + RWKV_COLS]
    fx = w_in[:, o + RWKV_COLS:]
    zf = jnp.zeros((D, LANES - FOX_HEADS), w_in.dtype)
    w_pad = jnp.concatenate([
        w_in[:, :QK_DIFF],
        rw[:, :3 * W], rw[:, 3 * W:3 * W + DECAY_LORA], z64,
        rw[:, 3 * W + DECAY_LORA:3 * W + DECAY_LORA + AAA_LORA], z64,
        rw[:, 3 * W + DECAY_LORA + AAA_LORA:],
        fx[:, :QK_FOX], fx[:, 3 * FOX_WIDTH:], zf,
    ], axis=1).astype(BF16)
    w_vt = jnp.concatenate([w_in[:, QK_DIFF:o], fx[:, QK_FOX:3 * FOX_WIDTH]], axis=1).T.astype(BF16)
    return w_pad, w_vt


LOG2E = math.log2(math.e)


ACC_ROWS = LANES + 16


def _scaled_q(q, c):
    return (q.astype(F32) * c).astype(BF16)


def _with_ones(vt):
    return jnp.concatenate([vt, jnp.ones((ACC_ROWS - LANES, vt.shape[1]), vt.dtype)], axis=0)


def _flash_step(s2s, vt1, m_ref, acc_ref):
    n = len(s2s)
    m_old = [m_ref[x] for x in range(n)]
    m_new = [jnp.maximum(m_old[x], jnp.max(s2s[x], axis=0, keepdims=True)) for x in range(n)]
    alpha = [jnp.exp2(m_old[x] - m_new[x]) for x in range(n)]
    p = [jnp.exp2(s2s[x] - m_new[x]).astype(BF16) for x in range(n)]
    pv = [_mm(vt1, p[x]) for x in range(n)]
    for x in range(n):
        acc_ref[x] = alpha[x] * acc_ref[x] + pv[x]
        m_ref[x] = m_new[x]


def _diff_attn_kernel(lam_ref, g_ref, q_ref, k_ref, vt_ref, o_ref, m_ref, acc_ref, *, tq, lam_init):
    i = pl.program_id(2)
    lane = lax.broadcasted_iota(jnp.int32, (1, LANES), 1)
    q = _scaled_q(q_ref[...], (DIFF_QK_DIM ** -0.5) * LOG2E)
    zero = jnp.zeros_like(q)
    qm = [jnp.where((lane >= DIFF_QK_DIM * x) & (lane < DIFF_QK_DIM * (x + 1)), q, zero) for x in range(4)]
    m_ref[...] = jnp.full(m_ref.shape, -jnp.inf, F32)
    acc_ref[...] = jnp.zeros(acc_ref.shape, F32)

    def step(j, diag):
        k = k_ref[pl.ds(pl.multiple_of(j * tq, tq), tq), :]
        vt1 = _with_ones(vt_ref[j])
        if diag:
            keep = (lax.broadcasted_iota(jnp.int32, (tq, tq), 1) >= lax.broadcasted_iota(jnp.int32, (tq, tq), 0))
        s2s = [_nt(k, qm[x]) for x in range(4)]
        if diag:
            s2s = [jnp.where(keep, s2, -jnp.inf) for s2 in s2s]
        _flash_step(s2s, vt1, m_ref, acc_ref)

    def body(j, carry):
        step(j, False)
        return carry

    lax.fori_loop(0, i, body, 0)
    step(i, True)

    lp = lam_ref[...]
    lam = (jnp.exp(jnp.sum(lp[0:1] * lp[1:2], axis=-1, keepdims=True))
           - jnp.exp(jnp.sum(lp[2:3] * lp[3:4], axis=-1, keepdims=True)) + lam_init)
    sm = [acc_ref[x, 0:LANES, :] / acc_ref[x, LANES:LANES + 1, :] for x in range(4)]
    outs = [sm[2 * hh] - lam * sm[2 * hh + 1] for hh in range(2)]
    row = lax.broadcasted_iota(jnp.int32, (LANES, 1), 0)
    o = jnp.where(row < HEAD_DIM, outs[0], outs[1])
    sq = o * o
    ms = jnp.where(row < HEAD_DIM, jnp.sum(sq[0:HEAD_DIM], axis=0, keepdims=True),
                   jnp.sum(sq[HEAD_DIM:], axis=0, keepdims=True)) * (1.0 / HEAD_DIM)
    y = o * lax.rsqrt(ms + RMS_EPS) * g_ref[...] * (1.0 - lam_init)
    o_ref[...] = y.T.astype(o_ref.dtype)


def _diff_attention(dqk, vtd, lam_params, subln_g, layer_idx, B, S):
    T = B * S
    tq = min(ATT_T, S)
    nq = S // tq
    npair = DIFF_HEADS // 2
    lam_init = 0.8 - 0.6 * math.exp(-0.3 * layer_idx)
    g2 = jnp.concatenate([subln_g, subln_g]).reshape(LANES, 1).astype(F32)
    return pl.pallas_call(
        functools.partial(_diff_attn_kernel, tq=tq, lam_init=lam_init),
        out_shape=jax.ShapeDtypeStruct((T, DIFF_WIDTH), BF16),
        grid=(B, npair, nq),
        in_specs=[
            pl.BlockSpec((4, DIFF_QK_DIM), lambda b, p, i: (0, 0)),
            pl.BlockSpec((LANES, 1), lambda b, p, i: (0, 0)),
            pl.BlockSpec((tq, LANES), lambda b, p, i: (b * nq + i, p)),
            pl.BlockSpec((S, LANES), lambda b, p, i: (b, npair + p)),
            pl.BlockSpec((nq, LANES, tq), lambda b, p, i: (b, p, 0)),
        ],
        out_specs=pl.BlockSpec((tq, LANES), lambda b, p, i: (b * nq + i, p)),
        scratch_shapes=[
            pltpu.VMEM((4, 1, tq), F32),
            pltpu.VMEM((4, ACC_ROWS, tq), F32),
        ],
        compiler_params=_params(("parallel", "parallel", "arbitrary")),
        name="diff_attn",
    )(lam_params, g2, dqk, dqk, vtd)


def _fox_cum_kernel(f_ref, b_ref, rep_ref, row_ref, *, S, tc):
    rr = lax.broadcasted_iota(jnp.int32, (tc, tc), 0)
    cc = lax.broadcasted_iota(jnp.int32, (tc, tc), 1)
    tri = (rr >= cc).astype(F32)
    sel_r = lax.broadcasted_iota(jnp.int32, (LANES, LANES), 0)
    carry = jnp.zeros((1, LANES), F32)
    for c in range(S // tc):
        z = f_ref[c * tc:(c + 1) * tc, :] + b_ref[...]
        logf = -(jnp.maximum(-z, 0.0) + jnp.log(1.0 + jnp.exp(-jnp.abs(z))))
        cum = _mm(tri, logf, HIGHEST) + carry
        carry = cum[tc - 1:tc, :]
        row_ref[:, c * tc:(c + 1) * tc] = cum.T[0:8, :]
        for h in range(FOX_HEADS):
            rep_ref[h, c * tc:(c + 1) * tc, :] = _mm(cum, (sel_r == h).astype(F32), HIGHEST)


def _fox_cum(fl, b_f, B, S):
    tc = 256 if S % 256 == 0 else S
    bpad = jnp.zeros((1, LANES), F32).at[0, :FOX_HEADS].set(b_f.astype(F32))
    return pl.pallas_call(
        functools.partial(_fox_cum_kernel, S=S, tc=tc),
        out_shape=(jax.ShapeDtypeStruct((B, FOX_HEADS, S, LANES), F32), jax.ShapeDtypeStruct((B, 8, S), F32)),
        grid=(B,),
        in_specs=[pl.BlockSpec((S, LANES), lambda b: (b, 0)), pl.BlockSpec((1, LANES), lambda b: (0, 0))],
        out_specs=(pl.BlockSpec((None, FOX_HEADS, S, LANES), lambda b: (b, 0, 0, 0)),
                   pl.BlockSpec((None, 8, S), lambda b: (b, 0, 0))),
        compiler_params=_params(("parallel",)),
        name="fox_cum",
    )(fl, bpad)


def _fox_attn_kernel(q_ref, k_ref, vt_ref, c0_ref, c1_ref, cr_ref, o_ref, m_ref, acc_ref, *, tq):
    p_id = pl.program_id(1)
    i = pl.program_id(2)
    lane = lax.broadcasted_iota(jnp.int32, (1, LANES), 1)
    q = _scaled_q(q_ref[...], (HEAD_DIM ** -0.5) * LOG2E)
    zero = jnp.zeros_like(q)
    qm = [jnp.where((lane >= HEAD_DIM * x) & (lane < HEAD_DIM * (x + 1)), q, zero) for x in range(2)]
    ck_refs = (c0_ref, c1_ref)
    cq = [cr_ref[2 * p_id + x, pl.ds(i, 1), :] for x in range(2)]
    m_ref[...] = jnp.full(m_ref.shape, -jnp.inf, F32)
    acc_ref[...] = jnp.zeros(acc_ref.shape, F32)

    def step(j, diag):
        off = pl.multiple_of(j * tq, tq)
        k = k_ref[pl.ds(off, tq), :]
        vt1 = _with_ones(vt_ref[j])
        if diag:
            keep = (lax.broadcasted_iota(jnp.int32, (tq, tq), 1) >= lax.broadcasted_iota(jnp.int32, (tq, tq), 0))
        s2s = []
        for x in range(2):
            ck = ck_refs[x][pl.ds(off, tq), :]
            bias = (cq[x] - jnp.concatenate([ck] * (tq // LANES), axis=1)) * LOG2E
            s2s.append(_nt(k, qm[x]) + bias)
        if diag:
            s2s = [jnp.where(keep, s2, -jnp.inf) for s2 in s2s]
        _flash_step(s2s, vt1, m_ref, acc_ref)

    def body(j, carry):
        step(j, False)
        return carry

    lax.fori_loop(0, i, body, 0)
    step(i, True)
    row = lax.broadcasted_iota(jnp.int32, (LANES, 1), 0)
    sm = [acc_ref[x, 0:LANES, :] / acc_ref[x, LANES:LANES + 1, :] for x in range(2)]
    o = jnp.where(row < HEAD_DIM, sm[0], sm[1])
    o_ref[...] = o.T.astype(o_ref.dtype)


def _fox_attention(fqk, vtf, crep, crow, B, S):
    T = B * S
    tq = min(ATT_T, S)
    nq = S // tq
    npair = FOX_HEADS // 2
    crow4 = crow.reshape(B, 8, nq, tq)
    rep = lambda x: pl.BlockSpec((None, None, S, LANES), lambda b, p, i: (b, 2 * p + x, 0, 0))
    return pl.pallas_call(
        functools.partial(_fox_attn_kernel, tq=tq),
        out_shape=jax.ShapeDtypeStruct((T, FOX_WIDTH), BF16),
        grid=(B, npair, nq),
        in_specs=[
            pl.BlockSpec((tq, LANES), lambda b, p, i: (b * nq + i, p)),
            pl.BlockSpec((S, LANES), lambda b, p, i: (b, npair + p)),
            pl.BlockSpec((nq, LANES, tq), lambda b, p, i: (b, p, 0)),
            rep(0), rep(1),
            pl.BlockSpec((None, 8, nq, tq), lambda b, p, i: (b, 0, 0, 0)),
        ],
        out_specs=pl.BlockSpec((tq, LANES), lambda b, p, i: (b * nq + i, p)),
        scratch_shapes=[
            pltpu.VMEM((2, 1, tq), F32),
            pltpu.VMEM((2, ACC_ROWS, tq), F32),
        ],
        compiler_params=_params(("parallel", "parallel", "arbitrary")),
        name="fox_attn",
    )(fqk, fqk, vtf, crep, crep, crow4)


def _split3(x):
    hi = x.astype(BF16)
    r1 = x - hi.astype(F32)
    mid = r1.astype(BF16)
    lo = (r1 - mid.astype(F32)).astype(BF16)
    return hi, mid, lo


def _seg_sum(x, seg, npair):
    parts = _split3(x)
    return jnp.concatenate(
        [sum(_mm(t[:, LANES * p:LANES * (p + 1)], seg) for t in parts) for p in range(npair)], axis=1)


def _rwkv_kernel(x_ref, mu_ref, w0_ref, w2_ref, a0_ref, a2_ref, g2_ref, kk_ref, ka_ref, rk_ref, lng_ref, lnb_ref,
                 o_ref, carry_ref, st_ref, at_ref, rt_ref, bt_ref, kt_ref, v_ref, wc_ref, y_ref, g_ref, bon_ref,
                 *, tt, prec):
    i = pl.program_id(1)
    W = RWKV_WIDTH
    C = RW_CHUNK
    CPI = math.gcd(RW_CHUNKS_PER_ITER, tt // C)
    npair = RWKV_HEADS // 2

    @pl.when(i == 0)
    def _():
        carry_ref[...] = jnp.zeros(carry_ref.shape, F32)
        st_ref[...] = jnp.zeros(st_ref.shape, F32)

    x = x_ref[...]
    rows = lax.broadcasted_iota(jnp.int32, (tt, 1), 0)
    prev = jnp.where(rows == 0, carry_ref[...], pltpu.roll(x, 1, axis=0))
    carry_ref[...] = x[tt - 1:tt, :]
    xs = x + (prev - x) * mu_ref[...]
    r = xs[:, 0:W]
    k = xs[:, W:2 * W]
    v = xs[:, 2 * W:3 * W]
    xw = xs[:, 3 * W:3 * W + LANES]
    xa = xs[:, 3 * W + LANES:3 * W + 2 * LANES]
    xg = xs[:, 3 * W + 2 * LANES:]
    wl = w0_ref[...] + _mm(jnp.tanh(xw), w2_ref[...], HIGHEST)
    w = -(jnp.maximum(-wl, 0.0) + jnp.log(1.0 + jnp.exp(-jnp.abs(wl)))) - 0.5
    logdec = -jnp.exp(w)
    a = jax.nn.sigmoid(a0_ref[...] + _mm(xa, a2_ref[...], HIGHEST))
    g_ref[...] = _mm(jax.nn.sigmoid(xg), g2_ref[...], HIGHEST)

    r_i = lax.broadcasted_iota(jnp.int32, (LANES, LANES), 0)
    c_i = lax.broadcasted_iota(jnp.int32, (LANES, LANES), 1)
    seg = ((r_i // HEAD_DIM) == (c_i // HEAD_DIM)).astype(BF16)
    kkv = k * kk_ref[...]
    kkn = kkv / jnp.maximum(jnp.sqrt(_seg_sum(kkv * kkv, seg, npair)), 1e-12)
    k2 = k * (1.0 + (a - 1.0) * ka_ref[...])
    bon_ref[...] = _seg_sum(r * k2 * rk_ref[...], seg, npair) * v

    rt_i = lax.broadcasted_iota(jnp.int32, (tt, tt), 0)
    ct_i = lax.broadcasted_iota(jnp.int32, (tt, tt), 1)
    tri = (((rt_i // C) == (ct_i // C)) & (rt_i >= ct_i)).astype(BF16)
    cum = sum(_mm(tri, t) for t in _split3(logdec))
    winv = jnp.exp(-cum)
    wcum = jnp.exp(cum)
    at_ref[...] = -kkn * jnp.exp(cum - logdec)
    bt_ref[...] = kkn * a * winv
    kt_ref[...] = k2 * winv
    rt_ref[...] = r * wcum
    v_ref[...] = v
    wc_ref[...] = wcum

    lane = lax.broadcasted_iota(jnp.int32, (1, LANES), 1)
    lo = lane < HEAD_DIM
    tpos = r_i % C
    ipos = c_i % C
    strict = tpos > ipos
    incl = tpos >= ipos
    eye = r_i == c_i

    def stack2(m):
        return jnp.concatenate([jnp.where(lo, m, 0.0), jnp.where(lo, 0.0, m)], axis=0)

    def cast(m):
        return m if prec is not None else m.astype(BF16)

    def mm(p, q):
        return _mm(cast(p), cast(q), prec)

    def nt(p, q):
        return _nt(cast(p), cast(q), prec)

    def chunks(cc, carry):
        units = [(ci, p) for ci in range(CPI) for p in range(npair)]
        r0 = [pl.multiple_of((cc * CPI + ci) * C, C) for ci in range(CPI)]
        ld = lambda ref, ci, p: stack2(ref[pl.ds(r0[ci], C), LANES * p:LANES * (p + 1)])
        a2 = {u: ld(at_ref, *u) for u in units}
        r2 = {u: ld(rt_ref, *u) for u in units}
        b2 = {u: ld(bt_ref, *u) for u in units}
        k2s = {u: ld(kt_ref, *u) for u in units}
        v2 = {u: ld(v_ref, *u) for u in units}
        wl = {(ci, p): wc_ref[pl.ds(pl.multiple_of(r0[ci] + C - 8, 8), 8), LANES * p:LANES * (p + 1)][7:8, :]
              for (ci, p) in units}
        ar = {u: jnp.concatenate([a2[u], r2[u]], axis=0) for u in units}
        mb = {u: nt(ar[u], b2[u]) for u in units}
        mk = {u: nt(ar[u], k2s[u]) for u in units}
        lab = {u: jnp.where(strict, mb[u][0:LANES], 0.0) for u in units}
        mrb = {u: jnp.where(incl, mb[u][LANES:], 0.0) for u in units}
        lak = {u: jnp.where(strict, mk[u][0:LANES], 0.0) for u in units}
        mrk = {u: jnp.where(incl, mk[u][LANES:], 0.0) for u in units}
        xx = {u: jnp.concatenate([a2[u], mm(lak[u], v2[u])], axis=1) for u in units}
        lp = lab
        for it in range(6):
            xx = {u: xx[u] + mm(lp[u], xx[u]) for u in units}
            if it < 5:
                lp = {u: mm(lp[u], lp[u]) for u in units}
        mq = {u: mm(mrb[u], xx[u]) for u in units}
        mv = {u: mm(mrk[u], v2[u]) for u in units}
        bx = {u: mm((b2[u] * wl[u]).T, xx[u]) for u in units}
        kv = {u: mm((k2s[u] * wl[u]).T, v2[u]) for u in units}
        st = [st_ref[p] for p in range(npair)]
        for (ci, p) in units:
            u = (ci, p)
            q1 = r2[u] + mq[u][:, 0:LANES]
            q2 = mq[u][:, LANES:] + mv[u]
            gmat = jnp.where(eye, jnp.broadcast_to(wl[u], (LANES, LANES)), 0.0) + bx[u][:, 0:LANES]
            hmat = bx[u][:, LANES:] + kv[u]
            gs = mm(jnp.concatenate([gmat, q1], axis=0), st[p])
            st[p] = gs[0:LANES] + hmat
            yy = gs[LANES:] + q2
            y_ref[pl.ds(r0[ci], C), LANES * p:LANES * (p + 1)] = yy[0:C] + yy[C:]
        for p in range(npair):
            st_ref[p] = st[p]
        return carry

    lax.fori_loop(0, tt // (C * CPI), chunks, 0)

    y = y_ref[...]
    inv = 1.0 / HEAD_DIM
    mean = _seg_sum(y, seg, npair) * inv
    yc = y - mean
    var = _seg_sum(yc * yc, seg, npair) * inv
    yn = yc * lax.rsqrt(var + RWKV_LN_EPS) * lng_ref[...] + lnb_ref[...]
    o_ref[...] = ((yn + bon_ref[...]) * g_ref[...]).astype(o_ref.dtype)


def _rwkv(rcols, mu, w0, w2, a0, a2, g2, k_k, k_a, r_k, ln_g, ln_b, B, S, prec=HIGHEST):
    T = B * S
    W = RWKV_WIDTH
    tt = 512 if S % 512 == 0 else S
    nt_ = S // tt
    npair = RWKV_HEADS // 2
    pad = LANES - DECAY_LORA
    mu_p = jnp.concatenate([mu[:3 * W], mu[3 * W:3 * W + DECAY_LORA], jnp.zeros((pad,), F32),
                            mu[3 * W + DECAY_LORA:3 * W + DECAY_LORA + AAA_LORA], jnp.zeros((pad,), F32),
                            mu[3 * W + DECAY_LORA + AAA_LORA:]]).reshape(1, RW_PAD_COLS)
    w2p = jnp.concatenate([w2, jnp.zeros((pad, W), F32)], axis=0)
    a2p = jnp.concatenate([a2, jnp.zeros((pad, W), F32)], axis=0)
    vec = lambda t: t.reshape(1, W).astype(F32)
    full = lambda shape: pl.BlockSpec(shape, lambda b, i: (0,) * len(shape))
    sc = lambda: pltpu.VMEM((tt, W), F32)
    return pl.pallas_call(
        functools.partial(_rwkv_kernel, tt=tt, prec=prec),
        out_shape=jax.ShapeDtypeStruct((T, W), BF16),
        grid=(B, nt_),
        in_specs=[
            pl.BlockSpec((tt, RW_PAD_COLS), lambda b, i: (b * nt_ + i, 0)),
            full((1, RW_PAD_COLS)), full((1, W)), full((LANES, W)), full((1, W)), full((LANES, W)),
            full((GATE_LORA, W)), full((1, W)), full((1, W)), full((1, W)), full((1, W)), full((1, W)),
        ],
        out_specs=pl.BlockSpec((tt, W), lambda b, i: (b * nt_ + i, 0)),
        scratch_shapes=[
            pltpu.VMEM((1, RW_PAD_COLS), F32),
            pltpu.VMEM((npair, LANES, LANES), F32),
            sc(), sc(), sc(), sc(), sc(), sc(), sc(), sc(), sc(),
        ],
        compiler_params=_params(("parallel", "arbitrary")),
        name="rwkv7",
    )(rcols, mu_p, vec(w0), w2p, vec(a0), a2p, g2, vec(k_k), vec(k_a), vec(r_k), vec(ln_g), vec(ln_b))


def _outproj_kernel(ya_ref, yb_ref, yc_ref, x_ref, w_ref, gm_ref, g_ref, sh_ref, sc_ref, xo_ref, ho_ref):
    o1 = DIFF_WIDTH
    o2 = o1 + RWKV_WIDTH
    mix = (_mm(ya_ref[...], w_ref[0:o1, :]) + _mm(yb_ref[...], w_ref[o1:o2, :])
           + _mm(yc_ref[...], w_ref[o2:, :]))
    xn = x_ref[...] + gm_ref[...] * mix
    xo_ref[...] = xn
    ms = jnp.mean(xn * xn, axis=-1, keepdims=True)
    y = xn * lax.rsqrt(ms + RMS_EPS) * g_ref[...]
    ho_ref[...] = y * (1.0 + sc_ref[...]) + sh_ref[...]


def _out_proj(ya, yb, yc, x2, w_out, g, mod4, S):
    T, D = x2.shape
    tm = 512 if S % 512 == 0 else S
    nb = S // tm
    row = lambda i: (i, 0)
    modspec = lambda which: pl.BlockSpec((None, None, 1, D), lambda i: (i // nb, which, 0, 0))
    return pl.pallas_call(
        _outproj_kernel,
        out_shape=(jax.ShapeDtypeStruct((T, D), F32), jax.ShapeDtypeStruct((T, D), F32)),
        grid=(T // tm,),
        in_specs=[
            pl.BlockSpec((tm, DIFF_WIDTH), row), pl.BlockSpec((tm, RWKV_WIDTH), row), pl.BlockSpec((tm, FOX_WIDTH), row),
            pl.BlockSpec((tm, D), row),
            pl.BlockSpec((D, D), lambda i: (0, 0)),
            modspec(2),
            pl.BlockSpec((1, D), lambda i: (0, 0)),
            modspec(3), modspec(4),
        ],
        out_specs=(pl.BlockSpec((tm, D), row), pl.BlockSpec((tm, D), row)),
        compiler_params=_params(("parallel",)),
        name="out_proj",
    )(ya, yb, yc, x2, w_out.astype(BF16), mod4, g.reshape(1, D), mod4, mod4)


def _top16(s, iota_f, n):
    vals, poss = [], []
    for _ in range(PEER_TOPK):
        m = jnp.max(s, axis=0, keepdims=True)
        pos = jnp.min(jnp.where(s == m, iota_f, float(n)), axis=0, keepdims=True)
        vals.append(m)
        poss.append(pos)
        s = jnp.where(iota_f == pos, -jnp.inf, s)
    return jnp.concatenate(vals, axis=0), jnp.concatenate(poss, axis=0)


ROUTE_UNROLL = 8
PEER_NCAND = 56


def _peer_cand_tables():
    K = PEER_TOPK
    pairs = [(a, b) for a in range(K) for b in range(K) if (a + 1) * (b + 1) <= K]
    n = PEER_NCAND
    p0 = [[0.0] * K for _ in range(n)]
    p1 = [[0.0] * K for _ in range(n)]
    pad = [0.0] * n
    pos = [float(K * K + r) for r in range(n)]
    for r, (a, b) in enumerate(pairs):
        p0[r][a] = 1.0
        p1[r][b] = 1.0
        pos[r] = float(a * K + b)
    for r in range(len(pairs), n):
        pad[r] = -float("inf")
    col = lambda v: jnp.broadcast_to(jnp.asarray(v, F32)[:, None], (n, LANES))
    return jnp.asarray(p0, F32), jnp.asarray(p1, F32), col(pad), col(pos)


def _peer_route_kernel(h_ref, wq_ref, sk_ref, p0_ref, p1_ref, cpad_ref, cpos_ref, idx_ref, gate_ref, q_scr, e_scr,
                       g_scr):
    K = PEER_TOPK
    hb = h_ref[...].astype(BF16)
    q = _mm(hb, wq_ref[...])
    for hc in range(2 * PEER_HEADS):
        q_scr[hc] = q[:, LANES * hc:LANES * (hc + 1)].astype(BF16)
    iota_n = lax.broadcasted_iota(jnp.int32, (PEER_NKEYS, LANES), 0).astype(F32)
    cpos = cpos_ref[...]

    def one_head(h):
        sv0, si0 = _top16(_nt(sk_ref[2 * h], q_scr[2 * h]), iota_n, PEER_NKEYS)
        sv1, si1 = _top16(_nt(sk_ref[2 * h + 1], q_scr[2 * h + 1]), iota_n, PEER_NKEYS)
        cand = _mm(p0_ref[...], sv0, HIGHEST) + _mm(p1_ref[...], sv1, HIGHEST) + cpad_ref[...]
        cidx = _mm(p0_ref[...], si0) * float(PEER_NKEYS) + _mm(p1_ref[...], si1)
        fv, es = [], []
        for _ in range(K):
            m = jnp.max(cand, axis=0, keepdims=True)
            pos = jnp.min(jnp.where(cand == m, cpos, float(2 * K * K)), axis=0, keepdims=True)
            hit = cpos == pos
            fv.append(m)
            es.append(jnp.max(jnp.where(hit, cidx, -1.0), axis=0, keepdims=True))
            cand = jnp.where(hit, -jnp.inf, cand)
        fv = jnp.concatenate(fv, axis=0)
        ex = jnp.exp(fv - fv[0:1, :])
        g_scr[h] = ex / jnp.sum(ex, axis=0, keepdims=True)
        e_scr[h] = jnp.concatenate(es, axis=0)

    def heads(hh, carry):
        for j in range(ROUTE_UNROLL):
            one_head(hh * ROUTE_UNROLL + j)
        return carry

    lax.fori_loop(0, PEER_HEADS // ROUTE_UNROLL, heads, 0)
    e = e_scr[...].reshape(PEER_HEADS * K, LANES)
    idx_ref[...] = e.T.astype(jnp.int32)
    gate_ref[...] = g_scr[...].reshape(PEER_HEADS * K, LANES)


def _peer_route(h2, wq, subkeys, tok0=0, ntok=None):
    T, D = h2.shape
    T = T - tok0 if ntok is None else ntok
    tm = LANES
    blk0 = tok0 // tm
    nq = 2 * PEER_HEADS
    sk = subkeys.reshape(nq, PEER_NKEYS, PEER_HALF).astype(BF16)
    p0, p1, cpad, cpos = _peer_cand_tables()
    const = lambda shape: pl.BlockSpec(shape, lambda i: (0,) * len(shape))
    return pl.pallas_call(
        _peer_route_kernel,
        out_shape=(jax.ShapeDtypeStruct((T, PEER_HEADS * PEER_TOPK), jnp.int32),
                   jax.ShapeDtypeStruct((T // tm, PEER_HEADS * PEER_TOPK, tm), F32)),
        grid=(T // tm,),
        in_specs=[
            pl.BlockSpec((tm, D), lambda i: (blk0 + i, 0)),
            const((D, nq * PEER_HALF)),
            const((nq, PEER_NKEYS, PEER_HALF)),
            const((PEER_NCAND, PEER_TOPK)), const((PEER_NCAND, PEER_TOPK)),
            const((PEER_NCAND, LANES)), const((PEER_NCAND, LANES)),
        ],
        out_specs=(pl.BlockSpec((tm, PEER_HEADS * PEER_TOPK), lambda i: (i, 0)),
                   pl.BlockSpec((None, PEER_HEADS * PEER_TOPK, tm), lambda i: (i, 0, 0))),
        scratch_shapes=[
            pltpu.VMEM((nq, tm, PEER_HALF), BF16),
            pltpu.VMEM((PEER_HEADS, PEER_TOPK, tm), F32),
            pltpu.VMEM((PEER_HEADS, PEER_TOPK, tm), F32),
        ],
        compiler_params=_params(("parallel",)),
        name="peer_route",
    )(h2, wq.astype(BF16), sk, p0, p1, cpad, cpos)


PEER_G = 16
PEER_SLOTS = PEER_HEADS * PEER_TOPK


def _peer_eval_kernel(idx_ref, idxn_ref, gate_ref, h_ref, x_ref, gf_ref, fg_ref, uv_ref, o_ref, buf, sem, *, final):
    G = PEER_G
    R = G * PEER_SLOTS
    D = D_MODEL
    tiles = PEER_SLOTS // SUBLANES
    i = pl.program_id(0)
    n = pl.num_programs(0)

    def start(ids, off, s, t, u):
        pltpu.make_async_copy(uv_ref.at[ids[off + t * SUBLANES + u]], buf.at[s, t, pl.ds(u, 1), :],
                              sem.at[s]).start(priority=u % 2)

    def wait(s):
        pltpu.make_async_copy(buf.at[s], buf.at[s], sem.at[s]).wait()

    @pl.when(i == 0)
    def _():
        def body(t, carry):
            for u in range(SUBLANES):
                start(idx_ref, 0, 0, t, u)
            return carry
        lax.fori_loop(0, R // SUBLANES, body, 0)

    lane = lax.broadcasted_iota(jnp.int32, (1, LANES), 1)
    tbase = (i % (LANES // (2 * G))) * (2 * G)
    gate = gate_ref[...]
    outs = []
    for s in range(2):
        wait(s)
        nxt_ids, nxt_off = (idx_ref, R) if s == 0 else (idxn_ref, 0)
        for g in range(G):
            for t in range(tiles * g, tiles * (g + 1)):
                for u in range(SUBLANES):
                    start(nxt_ids, nxt_off, 1 - s, t, u)
            w_rows = buf[s, tiles * g:tiles * (g + 1)].reshape(PEER_SLOTS, D)
            u_rows = lax.bitcast_convert_type(w_rows & jnp.uint32(0xFFFF0000), F32)
            prod = u_rows * h_ref[G * s + g:G * s + g + 1, :]
            part = prod[:, 0:LANES]
            for c in range(1, D // LANES):
                part = part + prod[:, LANES * c:LANES * (c + 1)]
            act = jnp.sum(part, axis=1, keepdims=True)
            gcol = jnp.sum(jnp.where(lane == tbase + G * s + g, gate, 0.0), axis=1, keepdims=True)
            coef = gcol * (0.5 * act * (1.0 + lax.erf(act * (2.0 ** -0.5))))
            v_rows = lax.bitcast_convert_type(w_rows << 16, F32)
            outs.append(jnp.sum(v_rows * coef, axis=0, keepdims=True))
    xn = x_ref[...] + gf_ref[...] * jnp.concatenate(outs, axis=0)
    if final:
        ms = jnp.mean(xn * xn, axis=-1, keepdims=True)
        xn = xn * lax.rsqrt(ms + RMS_EPS) * fg_ref[...]
    o_ref[...] = xn

    @pl.when(i == n - 1)
    def _():
        wait(0)


def _peer_eval(eidx, gate_t, h2, x2, mod4, final_g, uv, S, final, ntok=None):
    T, D = x2.shape
    T = T if ntok is None else ntok
    G = PEER_G
    R = G * PEER_SLOTS
    n = T // (2 * G)
    return pl.pallas_call(
        functools.partial(_peer_eval_kernel, final=final),
        out_shape=jax.ShapeDtypeStruct(x2.shape, F32),
        input_output_aliases={4: 0},
        grid=(n,),
        in_specs=[
            pl.BlockSpec((2 * R,), lambda i: (i,), memory_space=pltpu.SMEM),
            pl.BlockSpec((R,), lambda i: (jnp.minimum(2 * i + 2, 2 * n - 2),), memory_space=pltpu.SMEM),
            pl.BlockSpec((None, PEER_SLOTS, LANES), lambda i: (i // (LANES // (2 * G)), 0, 0)),
            pl.BlockSpec((2 * G, D), lambda i: (i, 0)),
            pl.BlockSpec((2 * G, D), lambda i: (i, 0)),
            pl.BlockSpec((None, None, 1, D), lambda i: (i // (S // (2 * G)), 5, 0, 0)),
            pl.BlockSpec((1, D), lambda i: (0, 0)),
            pl.BlockSpec(memory_space=pl.ANY),
        ],
        out_specs=pl.BlockSpec((2 * G, D), lambda i: (i, 0)),
        scratch_shapes=[pltpu.VMEM((2, R // SUBLANES, SUBLANES, D), jnp.uint32), pltpu.SemaphoreType.DMA((2,))],
        compiler_params=_params(("arbitrary",)),
        name="peer_eval",
    )(eidx.reshape(-1), eidx.reshape(-1), gate_t, h2, x2, mod4, final_g.reshape(1, D), uv)


SC_WORKERS = 32
SC_WINDOW = 128
SC_ROWS = 32
PEER_GD = 32
SC_SHARE_NUM, SC_SHARE_DEN = 21, 32
SC_PARTS = 8
BATCH_GROUPS = 1


def _sc_gather(tab, idx):
    n = idx.shape[0]
    width = tab.shape[1]
    per = n // SC_WORKERS
    nsub = SC_WINDOW // SC_ROWS
    mesh = plsc.VectorSubcoreMesh(core_axis_name="core", subcore_axis_name="subcore")

    @pl.kernel(out_type=jax.ShapeDtypeStruct((n, width), tab.dtype), mesh=mesh,
               scratch_types=[pltpu.VMEM((SC_WINDOW,), jnp.int32), pltpu.VMEM((2, SC_ROWS, width), tab.dtype),
                              pltpu.SemaphoreType.DMA((2,)), pltpu.SemaphoreType.DMA((2,))])
    def gather(x_hbm, i_hbm, o_hbm, idx_v, rows_v, gsem, wsem):
        wid = lax.axis_index("core") * (SC_WORKERS // 2) + lax.axis_index("subcore")

        def start_gather(k):
            return pltpu.async_copy(x_hbm.at[idx_v.at[pl.ds(k * SC_ROWS, SC_ROWS)]], rows_v.at[k % 2], gsem.at[k % 2])

        @pl.loop(0, per // SC_WINDOW)
        def _(w):
            base = wid * per + w * SC_WINDOW
            pltpu.sync_copy(i_hbm.at[pl.ds(base, SC_WINDOW)], idx_v)
            gat = [start_gather(0)] + [None] * (nsub - 1)
            wrt = [None] * nsub
            for k in range(nsub):
                if k + 1 < nsub:
                    if k >= 1:
                        wrt[k - 1].wait()
                    gat[k + 1] = start_gather(k + 1)
                gat[k].wait()
                wrt[k] = pltpu.async_copy(rows_v.at[k % 2], o_hbm.at[pl.ds(base + k * SC_ROWS, SC_ROWS)],
                                          wsem.at[k % 2])
            for k in range(max(nsub - 2, 0), nsub):
                wrt[k].wait()

    return gather(tab, idx)


def _peer_dense_kernel(rows_ref, gate_ref, h_ref, x_ref, gf_ref, fg_ref, o_ref, *, final):
    G = PEER_GD
    D = D_MODEL
    i = pl.program_id(0)
    lane = lax.broadcasted_iota(jnp.int32, (1, LANES), 1)
    tbase = (i % (LANES // G)) * G
    gate = gate_ref[...]
    outs = []
    for g in range(G):
        w_rows = rows_ref[PEER_SLOTS * g:PEER_SLOTS * (g + 1), :]
        u_rows = lax.bitcast_convert_type(w_rows & jnp.uint32(0xFFFF0000), F32)
        prod = u_rows * h_ref[g:g + 1, :]
        part = prod[:, 0:LANES]
        for c in range(1, D // LANES):
            part = part + prod[:, LANES * c:LANES * (c + 1)]
        act = jnp.sum(part, axis=1, keepdims=True)
        gcol = jnp.sum(jnp.where(lane == tbase + g, gate, 0.0), axis=1, keepdims=True)
        coef = gcol * (0.5 * act * (1.0 + lax.erf(act * (2.0 ** -0.5))))
        v_rows = lax.bitcast_convert_type(w_rows << 16, F32)
        outs.append(jnp.sum(v_rows * coef, axis=0, keepdims=True))
    xn = x_ref[...] + gf_ref[...] * jnp.concatenate(outs, axis=0)
    if final:
        ms = jnp.mean(xn * xn, axis=-1, keepdims=True)
        xn = xn * lax.rsqrt(ms + RMS_EPS) * fg_ref[...]
    o_ref[...] = xn


def _peer_eval_dense(rows, gate_t, h2, x2, mod4, final_g, S, final, tok0):
    T, D = x2.shape
    G = PEER_GD
    tb = rows.shape[0] // PEER_SLOTS
    assert tok0 % LANES == 0 and tb % LANES == 0
    blk0 = tok0 // G
    return pl.pallas_call(
        functools.partial(_peer_dense_kernel, final=final),
        out_shape=jax.ShapeDtypeStruct(x2.shape, F32),
        input_output_aliases={3: 0},
        grid=(tb // G,),
        in_specs=[
            pl.BlockSpec((G * PEER_SLOTS, D), lambda i: (i, 0)),
            pl.BlockSpec((None, PEER_SLOTS, LANES), lambda i: (i // (LANES // G), 0, 0)),
            pl.BlockSpec((G, D), lambda i: (blk0 + i, 0)),
            pl.BlockSpec((G, D), lambda i: (blk0 + i, 0)),
            pl.BlockSpec((None, None, 1, D), lambda i: ((blk0 + i) // (S // G), 5, 0, 0)),
            pl.BlockSpec((1, D), lambda i: (0, 0)),
        ],
        out_specs=pl.BlockSpec((G, D), lambda i: (blk0 + i, 0)),
        compiler_params=_params(("parallel",)),
        name="peer_dense",
    )(rows, gate_t, h2, x2, mod4, final_g.reshape(1, D))


def _pack_uv(u, v):
    hi = lax.bitcast_convert_type(u.astype(BF16), jnp.uint16).astype(jnp.uint32) << 16
    lo = lax.bitcast_convert_type(v.astype(BF16), jnp.uint16).astype(jnp.uint32)
    return (hi | lo).reshape(u.shape[0], 1, u.shape[1])


def kernel(x, c, norm_mix_g, norm_ffn_g, final_norm_g, ada_w, ada_b, w_in, w_out, dif_lam, dif_subln_g, rw_mu, rw_w0,
           rw_w2, rw_a0, rw_a2, rw_g2, rw_kk, rw_ka, rw_rk, rw_ln_g, rw_ln_b, fox_bf, peer_wq, peer_subkeys, peer_u,
           peer_v):
    B, S, D = x.shape
    depth = ada_w.shape[0]
    mod = _ada_mod(c, ada_w, ada_b)
    ngrp = BATCH_GROUPS if B % BATCH_GROUPS == 0 else 1
    bg = B // ngrp
    tg = bg * S
    prep = [(_pad_w_in(w_in[l]), _pack_uv(peer_u[l], peer_v[l])) for l in range(depth)]

    def mix_route(l, g, x2):
        mod4 = mod[l, g * bg:(g + 1) * bg].reshape(bg, 6, 1, D)
        (w_pad, w_vt), uv = prep[l]
        dqk, rcols, fqk, fl, vtd, vtf = _in_proj(x2, norm_mix_g[l], mod4, w_pad, w_vt, S)
        ya = _diff_attention(dqk, vtd, dif_lam[l], dif_subln_g[l], l, bg, S)
        yb = _rwkv(rcols, rw_mu[l], rw_w0[l], rw_w2[l], rw_a0[l], rw_a2[l], rw_g2[l], rw_kk[l], rw_ka[l],
                   rw_rk[l].reshape(-1), rw_ln_g[l], rw_ln_b[l], bg, S, prec=None)
        crep, crow = _fox_cum(fl, fox_bf[l], bg, S)
        yc = _fox_attention(fqk, vtf, crep, crow, bg, S)
        x2, h2 = _out_proj(ya, yb, yc, x2, w_out[l], norm_ffn_g[l], mod4, S)
        unit = SC_PARTS * LANES
        tb = (tg * SC_SHARE_NUM // SC_SHARE_DEN) // unit * unit
        ta = tg - tb
        bounds = [ta + tb * j // SC_PARTS for j in range(SC_PARTS + 1)]
        parts = []
        for t0, t1 in zip(bounds[:-1], bounds[1:]):
            if t1 == t0:
                continue
            e_p, g_p = _peer_route(h2, peer_wq[l], peer_subkeys[l], t0, t1 - t0)
            parts.append((_sc_gather(uv.reshape(-1, D), e_p.reshape(-1)), g_p, t0))
        e_a, g_a = _peer_route(h2, peer_wq[l], peer_subkeys[l], 0, ta)
        return dict(x2=x2, h2=h2, mod4=mod4, uv=uv, ta=ta, e_a=e_a, g_a=g_a, parts=parts)

    def evaluate(l, st):
        final = l == depth - 1
        x2 = _peer_eval(st["e_a"], st["g_a"], st["h2"], st["x2"], st["mod4"], final_norm_g, st["uv"], S, final,
                        ntok=st["ta"])
        for r, g_p, t0 in st["parts"]:
            x2 = _peer_eval_dense(r, g_p, st["h2"], x2, st["mod4"], final_norm_g, S, final, t0)
        return x2

    xs = [x[g * bg:(g + 1) * bg].reshape(tg, D) for g in range(ngrp)]
    pending = []
    for l in range(depth):
        for g in range(ngrp):
            pending.append((l, g, mix_route(l, g, xs[g])))
            if len(pending) == ngrp:
                l0, g0, st = pending.pop(0)
                xs[g0] = evaluate(l0, st)
    for l0, g0, st in pending:
        xs[g0] = evaluate(l0, st)
    return jnp.concatenate(xs, axis=0).reshape(B, S, D)
```

```python
import functools
import math

import jax
import jax.numpy as jnp
from jax import lax
from jax.experimental import pallas as pl
from jax.experimental.pallas import tpu as pltpu
from jax.experimental.pallas import tpu_sc as plsc

F32 = jnp.float32
BF16 = jnp.bfloat16
HIGHEST = lax.Precision.HIGHEST

D_MODEL = 1024
HEAD_DIM = 64
DIFF_HEADS = 6
DIFF_QK_DIM = HEAD_DIM // 2
RWKV_HEADS = 6
FOX_HEADS = 4
DIFF_WIDTH = DIFF_HEADS * HEAD_DIM
RWKV_WIDTH = RWKV_HEADS * HEAD_DIM
FOX_WIDTH = FOX_HEADS * HEAD_DIM
DECAY_LORA = 64
AAA_LORA = 64
GATE_LORA = 128
DIFF_COLS = 3 * DIFF_WIDTH
RWKV_COLS = 3 * RWKV_WIDTH + DECAY_LORA + AAA_LORA + GATE_LORA
PEER_HEADS = 8
PEER_NKEYS = 128
PEER_TOPK = 16
PEER_QDIM = 256
PEER_HALF = PEER_QDIM // 2
RMS_EPS = 1e-6
RWKV_LN_EPS = 64e-5

LANES = 128
SUBLANES = 8
RW_PAD_COLS = 3 * RWKV_WIDTH + 3 * LANES
VMEM_LIMIT = 56 * 1024 * 1024

RW_CHUNK = 64
RW_CHUNKS_PER_ITER = 4


def _params(sem, vmem=VMEM_LIMIT):
    return pltpu.CompilerParams(dimension_semantics=sem, vmem_limit_bytes=vmem)


def _nt(a, b, precision=None):
    return lax.dot_general(a, b, (((1,), (1,)), ((), ())), preferred_element_type=F32, precision=precision)


def _mm(a, b, precision=None):
    return jnp.dot(a, b, preferred_element_type=F32, precision=precision)


def _ada_kernel(c_ref, w_ref, b_ref, o_ref):
    c = c_ref[...]
    ca = c * jax.nn.sigmoid(c)
    o_ref[...] = _mm(ca, w_ref[...], HIGHEST) + b_ref[...]


def _ada_mod(c, ada_w, ada_b):
    L, D, N = ada_w.shape
    B = c.shape[0]
    tn = 1536
    return pl.pallas_call(
        _ada_kernel,
        out_shape=jax.ShapeDtypeStruct((L, B, N), F32),
        grid=(L, N // tn),
        in_specs=[
            pl.BlockSpec((B, D), lambda l, j: (0, 0)),
            pl.BlockSpec((None, D, tn), lambda l, j: (l, 0, j)),
            pl.BlockSpec((None, 1, tn), lambda l, j: (l, 0, j)),
        ],
        out_specs=pl.BlockSpec((None, B, tn), lambda l, j: (l, 0, j)),
        compiler_params=_params(("parallel", "parallel")),
        name="ada_mod",
    )(c, ada_w, ada_b.reshape(L, 1, N))


ATT_T = 512
QK_DIFF = 2 * DIFF_WIDTH
QK_FOX = 2 * FOX_WIDTH
VT_ROWS = DIFF_WIDTH + FOX_WIDTH
IN_PAD_COLS = QK_DIFF + RW_PAD_COLS + QK_FOX + LANES


def _inproj_kernel(x_ref, g_ref, sh_ref, sc_ref, w_ref, wvt_ref, d_ref, r_ref, f_ref, fl_ref, vtd_ref, vtf_ref, *, ta):
    x = x_ref[...]
    ms = jnp.mean(x * x, axis=-1, keepdims=True)
    y = x * lax.rsqrt(ms + RMS_EPS) * g_ref[...]
    h = (y * (1.0 + sc_ref[...]) + sh_ref[...]).astype(BF16)
    o1 = QK_DIFF
    o2 = o1 + RW_PAD_COLS
    o3 = o2 + QK_FOX
    d_ref[...] = _mm(h, w_ref[:, 0:o1]).astype(BF16)
    r_ref[...] = _mm(h, w_ref[:, o1:o2])
    f_ref[...] = _mm(h, w_ref[:, o2:o3]).astype(BF16)
    fl_ref[...] = _mm(h, w_ref[:, o3:o3 + LANES])
    vt = _nt(wvt_ref[...], h).astype(BF16)
    for s in range(x.shape[0] // ta):
        vtd_ref[s] = vt[0:DIFF_WIDTH, ta * s:ta * (s + 1)]
        vtf_ref[s] = vt[DIFF_WIDTH:, ta * s:ta * (s + 1)]


def _in_proj(x2, g, mod4, w_pad, w_vt, S):
    T, D = x2.shape
    tm = 512 if S % 512 == 0 else S
    ta = min(ATT_T, S)
    nb = S // tm
    row = lambda i: (i, 0)
    return pl.pallas_call(
        functools.partial(_inproj_kernel, ta=ta),
        out_shape=(
            jax.ShapeDtypeStruct((T, QK_DIFF), BF16),
            jax.ShapeDtypeStruct((T, RW_PAD_COLS), F32),
            jax.ShapeDtypeStruct((T, QK_FOX), BF16),
            jax.ShapeDtypeStruct((T, LANES), F32),
            jax.ShapeDtypeStruct((T // ta, DIFF_WIDTH, ta), BF16),
            jax.ShapeDtypeStruct((T // ta, FOX_WIDTH, ta), BF16),
        ),
        grid=(T // tm,),
        in_specs=[
            pl.BlockSpec((tm, D), row),
            pl.BlockSpec((1, D), lambda i: (0, 0)),
            pl.BlockSpec((None, None, 1, D), lambda i: (i // nb, 0, 0, 0)),
            pl.BlockSpec((None, None, 1, D), lambda i: (i // nb, 1, 0, 0)),
            pl.BlockSpec((D, IN_PAD_COLS), lambda i: (0, 0)),
            pl.BlockSpec((VT_ROWS, D), lambda i: (0, 0)),
        ],
        out_specs=(
            pl.BlockSpec((tm, QK_DIFF), row),
            pl.BlockSpec((tm, RW_PAD_COLS), row),
            pl.BlockSpec((tm, QK_FOX), row),
            pl.BlockSpec((tm, LANES), row),
            pl.BlockSpec((tm // ta, DIFF_WIDTH, ta), lambda i: (i, 0, 0)),
            pl.BlockSpec((tm // ta, FOX_WIDTH, ta), lambda i: (i, 0, 0)),
        ),
        compiler_params=_params(("parallel",)),
        name="in_proj",
    )(x2, g.reshape(1, D), mod4, mod4, w_pad, w_vt)


def _pad_w_in(w_in):
    D = w_in.shape[0]
    W = RWKV_WIDTH
    o = DIFF_COLS
    z64 = jnp.zeros((D, LANES - DECAY_LORA), w_in.dtype)
    rw = w_in[:, o:o + RWKV_COLS]
    fx = w_in[:, o + RWKV_COLS:]
    zf = jnp.zeros((D, LANES - FOX_HEADS), w_in.dtype)
    w_pad = jnp.concatenate([
        w_in[:, :QK_DIFF],
        rw[:, :3 * W], rw[:, 3 * W:3 * W + DECAY_LORA], z64,
        rw[:, 3 * W + DECAY_LORA:3 * W + DECAY_LORA + AAA_LORA], z64,
        rw[:, 3 * W + DECAY_LORA + AAA_LORA:],
        fx[:, :QK_FOX], fx[:, 3 * FOX_WIDTH:], zf,
    ], axis=1).astype(BF16)
    w_vt = jnp.concatenate([w_in[:, QK_DIFF:o], fx[:, QK_FOX:3 * FOX_WIDTH]], axis=1).T.astype(BF16)
    return w_pad, w_vt


LOG2E = math.log2(math.e)


ACC_ROWS = LANES + 16


def _scaled_q(q, c):
    return (q.astype(F32) * c).astype(BF16)


def _with_ones(vt):
    return jnp.concatenate([vt, jnp.ones((ACC_ROWS - LANES, vt.shape[1]), vt.dtype)], axis=0)


def _flash_step(s2s, vt1, m_ref, acc_ref):
    n = len(s2s)
    m_old = [m_ref[x] for x in range(n)]
    m_new = [jnp.maximum(m_old[x], jnp.max(s2s[x], axis=0, keepdims=True)) for x in range(n)]
    alpha = [jnp.exp2(m_old[x] - m_new[x]) for x in range(n)]
    p = [jnp.exp2(s2s[x] - m_new[x]).astype(BF16) for x in range(n)]
    pv = [_mm(vt1, p[x]) for x in range(n)]
    for x in range(n):
        acc_ref[x] = alpha[x] * acc_ref[x] + pv[x]
        m_ref[x] = m_new[x]


def _diff_attn_kernel(lam_ref, g_ref, q_ref, k_ref, vt_ref, o_ref, m_ref, acc_ref, *, tq, lam_init):
    i = pl.program_id(2)
    lane = lax.broadcasted_iota(jnp.int32, (1, LANES), 1)
    q = _scaled_q(q_ref[...], (DIFF_QK_DIM ** -0.5) * LOG2E)
    zero = jnp.zeros_like(q)
    qm = [jnp.where((lane >= DIFF_QK_DIM * x) & (lane < DIFF_QK_DIM * (x + 1)), q, zero) for x in range(4)]
    m_ref[...] = jnp.full(m_ref.shape, -jnp.inf, F32)
    acc_ref[...] = jnp.zeros(acc_ref.shape, F32)

    def step(j, diag):
        k = k_ref[pl.ds(pl.multiple_of(j * tq, tq), tq), :]
        vt1 = _with_ones(vt_ref[j])
        if diag:
            keep = (lax.broadcasted_iota(jnp.int32, (tq, tq), 1) >= lax.broadcasted_iota(jnp.int32, (tq, tq), 0))
        s2s = [_nt(k, qm[x]) for x in range(4)]
        if diag:
            s2s = [jnp.where(keep, s2, -jnp.inf) for s2 in s2s]
        _flash_step(s2s, vt1, m_ref, acc_ref)

    def body(j, carry):
        step(j, False)
        return carry

    lax.fori_loop(0, i, body, 0)
    step(i, True)

    lp = lam_ref[...]
    lam = (jnp.exp(jnp.sum(lp[0:1] * lp[1:2], axis=-1, keepdims=True))
           - jnp.exp(jnp.sum(lp[2:3] * lp[3:4], axis=-1, keepdims=True)) + lam_init)
    sm = [acc_ref[x, 0:LANES, :] / acc_ref[x, LANES:LANES + 1, :] for x in range(4)]
    outs = [sm[2 * hh] - lam * sm[2 * hh + 1] for hh in range(2)]
    row = lax.broadcasted_iota(jnp.int32, (LANES, 1), 0)
    o = jnp.where(row < HEAD_DIM, outs[0], outs[1])
    sq = o * o
    ms = jnp.where(row < HEAD_DIM, jnp.sum(sq[0:HEAD_DIM], axis=0, keepdims=True),
                   jnp.sum(sq[HEAD_DIM:], axis=0, keepdims=True)) * (1.0 / HEAD_DIM)
    y = o * lax.rsqrt(ms + RMS_EPS) * g_ref[...] * (1.0 - lam_init)
    o_ref[...] = y.T.astype(o_ref.dtype)


def _diff_attention(dqk, vtd, lam_params, subln_g, layer_idx, B, S):
    T = B * S
    tq = min(ATT_T, S)
    nq = S // tq
    npair = DIFF_HEADS // 2
    lam_init = 0.8 - 0.6 * math.exp(-0.3 * layer_idx)
    g2 = jnp.concatenate([subln_g, subln_g]).reshape(LANES, 1).astype(F32)
    return pl.pallas_call(
        functools.partial(_diff_attn_kernel, tq=tq, lam_init=lam_init),
        out_shape=jax.ShapeDtypeStruct((T, DIFF_WIDTH), BF16),
        grid=(B, npair, nq),
        in_specs=[
            pl.BlockSpec((4, DIFF_QK_DIM), lambda b, p, i: (0, 0)),
            pl.BlockSpec((LANES, 1), lambda b, p, i: (0, 0)),
            pl.BlockSpec((tq, LANES), lambda b, p, i: (b * nq + i, p)),
            pl.BlockSpec((S, LANES), lambda b, p, i: (b, npair + p)),
            pl.BlockSpec((nq, LANES, tq), lambda b, p, i: (b, p, 0)),
        ],
        out_specs=pl.BlockSpec((tq, LANES), lambda b, p, i: (b * nq + i, p)),
        scratch_shapes=[
            pltpu.VMEM((4, 1, tq), F32),
            pltpu.VMEM((4, ACC_ROWS, tq), F32),
        ],
        compiler_params=_params(("parallel", "parallel", "arbitrary")),
        name="diff_attn",
    )(lam_params, g2, dqk, dqk, vtd)


def _fox_cum_kernel(f_ref, b_ref, rep_ref, row_ref, *, S, tc):
    rr = lax.broadcasted_iota(jnp.int32, (tc, tc), 0)
    cc = lax.broadcasted_iota(jnp.int32, (tc, tc), 1)
    tri = (rr >= cc).astype(F32)
    sel_r = lax.broadcasted_iota(jnp.int32, (LANES, LANES), 0)
    carry = jnp.zeros((1, LANES), F32)
    for c in range(S // tc):
        z = f_ref[c * tc:(c + 1) * tc, :] + b_ref[...]
        logf = -(jnp.maximum(-z, 0.0) + jnp.log(1.0 + jnp.exp(-jnp.abs(z))))
        cum = _mm(tri, logf, HIGHEST) + carry
        carry = cum[tc - 1:tc, :]
        row_ref[:, c * tc:(c + 1) * tc] = cum.T[0:8, :]
        for h in range(FOX_HEADS):
            rep_ref[h, c * tc:(c + 1) * tc, :] = _mm(cum, (sel_r == h).astype(F32), HIGHEST)


def _fox_cum(fl, b_f, B, S):
    tc = 256 if S % 256 == 0 else S
    bpad = jnp.zeros((1, LANES), F32).at[0, :FOX_HEADS].set(b_f.astype(F32))
    return pl.pallas_call(
        functools.partial(_fox_cum_kernel, S=S, tc=tc),
        out_shape=(jax.ShapeDtypeStruct((B, FOX_HEADS, S, LANES), F32), jax.ShapeDtypeStruct((B, 8, S), F32)),
        grid=(B,),
        in_specs=[pl.BlockSpec((S, LANES), lambda b: (b, 0)), pl.BlockSpec((1, LANES), lambda b: (0, 0))],
        out_specs=(pl.BlockSpec((None, FOX_HEADS, S, LANES), lambda b: (b, 0, 0, 0)),
                   pl.BlockSpec((None, 8, S), lambda b: (b, 0, 0))),
        compiler_params=_params(("parallel",)),
        name="fox_cum",
    )(fl, bpad)


def _fox_attn_kernel(q_ref, k_ref, vt_ref, c0_ref, c1_ref, cr_ref, o_ref, m_ref, acc_ref, *, tq):
    p_id = pl.program_id(1)
    i = pl.program_id(2)
    lane = lax.broadcasted_iota(jnp.int32, (1, LANES), 1)
    q = _scaled_q(q_ref[...], (HEAD_DIM ** -0.5) * LOG2E)
    zero = jnp.zeros_like(q)
    qm = [jnp.where((lane >= HEAD_DIM * x) & (lane < HEAD_DIM * (x + 1)), q, zero) for x in range(2)]
    ck_refs = (c0_ref, c1_ref)
    cq = [cr_ref[2 * p_id + x, pl.ds(i, 1), :] for x in range(2)]
    m_ref[...] = jnp.full(m_ref.shape, -jnp.inf, F32)
    acc_ref[...] = jnp.zeros(acc_ref.shape, F32)

    def step(j, diag):
        off = pl.multiple_of(j * tq, tq)
        k = k_ref[pl.ds(off, tq), :]
        vt1 = _with_ones(vt_ref[j])
        if diag:
            keep = (lax.broadcasted_iota(jnp.int32, (tq, tq), 1) >= lax.broadcasted_iota(jnp.int32, (tq, tq), 0))
        s2s = []
        for x in range(2):
            ck = ck_refs[x][pl.ds(off, tq), :]
            bias = (cq[x] - jnp.concatenate([ck] * (tq // LANES), axis=1)) * LOG2E
            s2s.append(_nt(k, qm[x]) + bias)
        if diag:
            s2s = [jnp.where(keep, s2, -jnp.inf) for s2 in s2s]
        _flash_step(s2s, vt1, m_ref, acc_ref)

    def body(j, carry):
        step(j, False)
        return carry

    lax.fori_loop(0, i, body, 0)
    step(i, True)
    row = lax.broadcasted_iota(jnp.int32, (LANES, 1), 0)
    sm = [acc_ref[x, 0:LANES, :] / acc_ref[x, LANES:LANES + 1, :] for x in range(2)]
    o = jnp.where(row < HEAD_DIM, sm[0], sm[1])
    o_ref[...] = o.T.astype(o_ref.dtype)


def _fox_attention(fqk, vtf, crep, crow, B, S):
    T = B * S
    tq = min(ATT_T, S)
    nq = S // tq
    npair = FOX_HEADS // 2
    crow4 = crow.reshape(B, 8, nq, tq)
    rep = lambda x: pl.BlockSpec((None, None, S, LANES), lambda b, p, i: (b, 2 * p + x, 0, 0))
    return pl.pallas_call(
        functools.partial(_fox_attn_kernel, tq=tq),
        out_shape=jax.ShapeDtypeStruct((T, FOX_WIDTH), BF16),
        grid=(B, npair, nq),
        in_specs=[
            pl.BlockSpec((tq, LANES), lambda b, p, i: (b * nq + i, p)),
            pl.BlockSpec((S, LANES), lambda b, p, i: (b, npair + p)),
            pl.BlockSpec((nq, LANES, tq), lambda b, p, i: (b, p, 0)),
            rep(0), rep(1),
            pl.BlockSpec((None, 8, nq, tq), lambda b, p, i: (b, 0, 0, 0)),
        ],
        out_specs=pl.BlockSpec((tq, LANES), lambda b, p, i: (b * nq + i, p)),
        scratch_shapes=[
            pltpu.VMEM((2, 1, tq), F32),
            pltpu.VMEM((2, ACC_ROWS, tq), F32),
        ],
        compiler_params=_params(("parallel", "parallel", "arbitrary")),
        name="fox_attn",
    )(fqk, fqk, vtf, crep, crep, crow4)


def _split3(x):
    hi = x.astype(BF16)
    r1 = x - hi.astype(F32)
    mid = r1.astype(BF16)
    lo = (r1 - mid.astype(F32)).astype(BF16)
    return hi, mid, lo


def _seg_sum(x, seg, npair):
    parts = _split3(x)
    return jnp.concatenate(
        [sum(_mm(t[:, LANES * p:LANES * (p + 1)], seg) for t in parts) for p in range(npair)], axis=1)


def _rwkv_kernel(x_ref, mu_ref, w0_ref, w2_ref, a0_ref, a2_ref, g2_ref, kk_ref, ka_ref, rk_ref, lng_ref, lnb_ref,
                 o_ref, carry_ref, st_ref, at_ref, rt_ref, bt_ref, kt_ref, v_ref, wc_ref, y_ref, g_ref, bon_ref,
                 *, tt, prec):
    i = pl.program_id(1)
    W = RWKV_WIDTH
    C = RW_CHUNK
    CPI = math.gcd(RW_CHUNKS_PER_ITER, tt // C)
    npair = RWKV_HEADS // 2

    @pl.when(i == 0)
    def _():
        carry_ref[...] = jnp.zeros(carry_ref.shape, F32)
        st_ref[...] = jnp.zeros(st_ref.shape, F32)

    x = x_ref[...]
    rows = lax.broadcasted_iota(jnp.int32, (tt, 1), 0)
    prev = jnp.where(rows == 0, carry_ref[...], pltpu.roll(x, 1, axis=0))
    carry_ref[...] = x[tt - 1:tt, :]
    xs = x + (prev - x) * mu_ref[...]
    r = xs[:, 0:W]
    k = xs[:, W:2 * W]
    v = xs[:, 2 * W:3 * W]
    xw = xs[:, 3 * W:3 * W + LANES]
    xa = xs[:, 3 * W + LANES:3 * W + 2 * LANES]
    xg = xs[:, 3 * W + 2 * LANES:]
    wl = w0_ref[...] + _mm(jnp.tanh(xw), w2_ref[...], HIGHEST)
    w = -(jnp.maximum(-wl, 0.0) + jnp.log(1.0 + jnp.exp(-jnp.abs(wl)))) - 0.5
    logdec = -jnp.exp(w)
    a = jax.nn.sigmoid(a0_ref[...] + _mm(xa, a2_ref[...], HIGHEST))
    g_ref[...] = _mm(jax.nn.sigmoid(xg), g2_ref[...], HIGHEST)

    r_i = lax.broadcasted_iota(jnp.int32, (LANES, LANES), 0)
    c_i = lax.broadcasted_iota(jnp.int32, (LANES, LANES), 1)
    seg = ((r_i // HEAD_DIM) == (c_i // HEAD_DIM)).astype(BF16)
    kkv = k * kk_ref[...]
    kkn = kkv / jnp.maximum(jnp.sqrt(_seg_sum(kkv * kkv, seg, npair)), 1e-12)
    k2 = k * (1.0 + (a - 1.0) * ka_ref[...])
    bon_ref[...] = _seg_sum(r * k2 * rk_ref[...], seg, npair) * v

    rt_i = lax.broadcasted_iota(jnp.int32, (tt, tt), 0)
    ct_i = lax.broadcasted_iota(jnp.int32, (tt, tt), 1)
    tri = (((rt_i // C) == (ct_i // C)) & (rt_i >= ct_i)).astype(BF16)
    cum = sum(_mm(tri, t) for t in _split3(logdec))
    winv = jnp.exp(-cum)
    wcum = jnp.exp(cum)
    at_ref[...] = -kkn * jnp.exp(cum - logdec)
    bt_ref[...] = kkn * a * winv
    kt_ref[...] = k2 * winv
    rt_ref[...] = r * wcum
    v_ref[...] = v
    wc_ref[...] = wcum

    lane = lax.broadcasted_iota(jnp.int32, (1, LANES), 1)
    lo = lane < HEAD_DIM
    tpos = r_i % C
    ipos = c_i % C
    strict = tpos > ipos
    incl = tpos >= ipos
    eye = r_i == c_i

    def stack2(m):
        return jnp.concatenate([jnp.where(lo, m, 0.0), jnp.where(lo, 0.0, m)], axis=0)

    def cast(m):
        return m if prec is not None else m.astype(BF16)

    def mm(p, q):
        return _mm(cast(p), cast(q), prec)

    def nt(p, q):
        return _nt(cast(p), cast(q), prec)

    def chunks(cc, carry):
        units = [(ci, p) for ci in range(CPI) for p in range(npair)]
        r0 = [pl.multiple_of((cc * CPI + ci) * C, C) for ci in range(CPI)]
        ld = lambda ref, ci, p: stack2(ref[pl.ds(r0[ci], C), LANES * p:LANES * (p + 1)])
        a2 = {u: ld(at_ref, *u) for u in units}
        r2 = {u: ld(rt_ref, *u) for u in units}
        b2 = {u: ld(bt_ref, *u) for u in units}
        k2s = {u: ld(kt_ref, *u) for u in units}
        v2 = {u: ld(v_ref, *u) for u in units}
        wl = {(ci, p): wc_ref[pl.ds(pl.multiple_of(r0[ci] + C - 8, 8), 8), LANES * p:LANES * (p + 1)][7:8, :]
              for (ci, p) in units}
        ar = {u: jnp.concatenate([a2[u], r2[u]], axis=0) for u in units}
        mb = {u: nt(ar[u], b2[u]) for u in units}
        mk = {u: nt(ar[u], k2s[u]) for u in units}
        lab = {u: jnp.where(strict, mb[u][0:LANES], 0.0) for u in units}
        mrb = {u: jnp.where(incl, mb[u][LANES:], 0.0) for u in units}
        lak = {u: jnp.where(strict, mk[u][0:LANES], 0.0) for u in units}
        mrk = {u: jnp.where(incl, mk[u][LANES:], 0.0) for u in units}
        xx = {u: jnp.concatenate([a2[u], mm(lak[u], v2[u])], axis=1) for u in units}
        lp = lab
        for it in range(6):
            xx = {u: xx[u] + mm(lp[u], xx[u]) for u in units}
            if it < 5:
                lp = {u: mm(lp[u], lp[u]) for u in units}
        mq = {u: mm(mrb[u], xx[u]) for u in units}
        mv = {u: mm(mrk[u], v2[u]) for u in units}
        bx = {u: mm((b2[u] * wl[u]).T, xx[u]) for u in units}
        kv = {u: mm((k2s[u] * wl[u]).T, v2[u]) for u in units}
        st = [st_ref[p] for p in range(npair)]
        for (ci, p) in units:
            u = (ci, p)
            q1 = r2[u] + mq[u][:, 0:LANES]
            q2 = mq[u][:, LANES:] + mv[u]
            gmat = jnp.where(eye, jnp.broadcast_to(wl[u], (LANES, LANES)), 0.0) + bx[u][:, 0:LANES]
            hmat = bx[u][:, LANES:] + kv[u]
            gs = mm(jnp.concatenate([gmat, q1], axis=0), st[p])
            st[p] = gs[0:LANES] + hmat
            yy = gs[LANES:] + q2
            y_ref[pl.ds(r0[ci], C), LANES * p:LANES * (p + 1)] = yy[0:C] + yy[C:]
        for p in range(npair):
            st_ref[p] = st[p]
        return carry

    lax.fori_loop(0, tt // (C * CPI), chunks, 0)

    y = y_ref[...]
    inv = 1.0 / HEAD_DIM
    mean = _seg_sum(y, seg, npair) * inv
    yc = y - mean
    var = _seg_sum(yc * yc, seg, npair) * inv
    yn = yc * lax.rsqrt(var + RWKV_LN_EPS) * lng_ref[...] + lnb_ref[...]
    o_ref[...] = ((yn + bon_ref[...]) * g_ref[...]).astype(o_ref.dtype)


def _rwkv(rcols, mu, w0, w2, a0, a2, g2, k_k, k_a, r_k, ln_g, ln_b, B, S, prec=HIGHEST):
    T = B * S
    W = RWKV_WIDTH
    tt = 512 if S % 512 == 0 else S
    nt_ = S // tt
    npair = RWKV_HEADS // 2
    pad = LANES - DECAY_LORA
    mu_p = jnp.concatenate([mu[:3 * W], mu[3 * W:3 * W + DECAY_LORA], jnp.zeros((pad,), F32),
                            mu[3 * W + DECAY_LORA:3 * W + DECAY_LORA + AAA_LORA], jnp.zeros((pad,), F32),
                            mu[3 * W + DECAY_LORA + AAA_LORA:]]).reshape(1, RW_PAD_COLS)
    w2p = jnp.concatenate([w2, jnp.zeros((pad, W), F32)], axis=0)
    a2p = jnp.concatenate([a2, jnp.zeros((pad, W), F32)], axis=0)
    vec = lambda t: t.reshape(1, W).astype(F32)
    full = lambda shape: pl.BlockSpec(shape, lambda b, i: (0,) * len(shape))
    sc = lambda: pltpu.VMEM((tt, W), F32)
    return pl.pallas_call(
        functools.partial(_rwkv_kernel, tt=tt, prec=prec),
        out_shape=jax.ShapeDtypeStruct((T, W), BF16),
        grid=(B, nt_),
        in_specs=[
            pl.BlockSpec((tt, RW_PAD_COLS), lambda b, i: (b * nt_ + i, 0)),
            full((1, RW_PAD_COLS)), full((1, W)), full((LANES, W)), full((1, W)), full((LANES, W)),
            full((GATE_LORA, W)), full((1, W)), full((1, W)), full((1, W)), full((1, W)), full((1, W)),
        ],
        out_specs=pl.BlockSpec((tt, W), lambda b, i: (b * nt_ + i, 0)),
        scratch_shapes=[
            pltpu.VMEM((1, RW_PAD_COLS), F32),
            pltpu.VMEM((npair, LANES, LANES), F32),
            sc(), sc(), sc(), sc(), sc(), sc(), sc(), sc(), sc(),
        ],
        compiler_params=_params(("parallel", "arbitrary")),
        name="rwkv7",
    )(rcols, mu_p, vec(w0), w2p, vec(a0), a2p, g2, vec(k_k), vec(k_a), vec(r_k), vec(ln_g), vec(ln_b))


def _outproj_kernel(ya_ref, yb_ref, yc_ref, x_ref, w_ref, gm_ref, g_ref, sh_ref, sc_ref, xo_ref, ho_ref):
    o1 = DIFF_WIDTH
    o2 = o1 + RWKV_WIDTH
    mix = (_mm(ya_ref[...], w_ref[0:o1, :]) + _mm(yb_ref[...], w_ref[o1:o2, :])
           + _mm(yc_ref[...], w_ref[o2:, :]))
    xn = x_ref[...] + gm_ref[...] * mix
    xo_ref[...] = xn
    ms = jnp.mean(xn * xn, axis=-1, keepdims=True)
    y = xn * lax.rsqrt(ms + RMS_EPS) * g_ref[...]
    ho_ref[...] = y * (1.0 + sc_ref[...]) + sh_ref[...]


def _out_proj(ya, yb, yc, x2, w_out, g, mod4, S):
    T, D = x2.shape
    tm = 512 if S % 512 == 0 else S
    nb = S // tm
    row = lambda i: (i, 0)
    modspec = lambda which: pl.BlockSpec((None, None, 1, D), lambda i: (i // nb, which, 0, 0))
    return pl.pallas_call(
        _outproj_kernel,
        out_shape=(jax.ShapeDtypeStruct((T, D), F32), jax.ShapeDtypeStruct((T, D), F32)),
        grid=(T // tm,),
        in_specs=[
            pl.BlockSpec((tm, DIFF_WIDTH), row), pl.BlockSpec((tm, RWKV_WIDTH), row), pl.BlockSpec((tm, FOX_WIDTH), row),
            pl.BlockSpec((tm, D), row),
            pl.BlockSpec((D, D), lambda i: (0, 0)),
            modspec(2),
            pl.BlockSpec((1, D), lambda i: (0, 0)),
            modspec(3), modspec(4),
        ],
        out_specs=(pl.BlockSpec((tm, D), row), pl.BlockSpec((tm, D), row)),
        compiler_params=_params(("parallel",)),
        name="out_proj",
    )(ya, yb, yc, x2, w_out.astype(BF16), mod4, g.reshape(1, D), mod4, mod4)


def _top16(s, iota_f, n):
    vals, poss = [], []
    for _ in range(PEER_TOPK):
        m = jnp.max(s, axis=0, keepdims=True)
        pos = jnp.min(jnp.where(s == m, iota_f, float(n)), axis=0, keepdims=True)
        vals.append(m)
        poss.append(pos)
        s = jnp.where(iota_f == pos, -jnp.inf, s)
    return jnp.concatenate(vals, axis=0), jnp.concatenate(poss, axis=0)


ROUTE_UNROLL = 8
PEER_NCAND = 56


def _peer_cand_tables():
    K = PEER_TOPK
    pairs = [(a, b) for a in range(K) for b in range(K) if (a + 1) * (b + 1) <= K]
    n = PEER_NCAND
    p0 = [[0.0] * K for _ in range(n)]
    p1 = [[0.0] * K for _ in range(n)]
    pad = [0.0] * n
    pos = [float(K * K + r) for r in range(n)]
    for r, (a, b) in enumerate(pairs):
        p0[r][a] = 1.0
        p1[r][b] = 1.0
        pos[r] = float(a * K + b)
    for r in range(len(pairs), n):
        pad[r] = -float("inf")
    col = lambda v: jnp.broadcast_to(jnp.asarray(v, F32)[:, None], (n, LANES))
    return jnp.asarray(p0, F32), jnp.asarray(p1, F32), col(pad), col(pos)


def _peer_route_kernel(h_ref, wq_ref, sk_ref, p0_ref, p1_ref, cpad_ref, cpos_ref, idx_ref, gate_ref, q_scr, e_scr,
                       g_scr):
    K = PEER_TOPK
    hb = h_ref[...].astype(BF16)
    q = _mm(hb, wq_ref[...])
    for hc in range(2 * PEER_HEADS):
        q_scr[hc] = q[:, LANES * hc:LANES * (hc + 1)].astype(BF16)
    iota_n = lax.broadcasted_iota(jnp.int32, (PEER_NKEYS, LANES), 0).astype(F32)
    cpos = cpos_ref[...]

    def one_head(h):
        sv0, si0 = _top16(_nt(sk_ref[2 * h], q_scr[2 * h]), iota_n, PEER_NKEYS)
        sv1, si1 = _top16(_nt(sk_ref[2 * h + 1], q_scr[2 * h + 1]), iota_n, PEER_NKEYS)
        cand = _mm(p0_ref[...], sv0, HIGHEST) + _mm(p1_ref[...], sv1, HIGHEST) + cpad_ref[...]
        cidx = _mm(p0_ref[...], si0) * float(PEER_NKEYS) + _mm(p1_ref[...], si1)
        fv, es = [], []
        for _ in range(K):
            m = jnp.max(cand, axis=0, keepdims=True)
            pos = jnp.min(jnp.where(cand == m, cpos, float(2 * K * K)), axis=0, keepdims=True)
            hit = cpos == pos
            fv.append(m)
            es.append(jnp.max(jnp.where(hit, cidx, -1.0), axis=0, keepdims=True))
            cand = jnp.where(hit, -jnp.inf, cand)
        fv = jnp.concatenate(fv, axis=0)
        ex = jnp.exp(fv - fv[0:1, :])
        g_scr[h] = ex / jnp.sum(ex, axis=0, keepdims=True)
        e_scr[h] = jnp.concatenate(es, axis=0)

    def heads(hh, carry):
        for j in range(ROUTE_UNROLL):
            one_head(hh * ROUTE_UNROLL + j)
        return carry

    lax.fori_loop(0, PEER_HEADS // ROUTE_UNROLL, heads, 0)
    e = e_scr[...].reshape(PEER_HEADS * K, LANES)
    idx_ref[...] = e.T.astype(jnp.int32)
    gate_ref[...] = g_scr[...].reshape(PEER_HEADS * K, LANES)


def _peer_route(h2, wq, subkeys, tok0=0, ntok=None):
    T, D = h2.shape
    T = T - tok0 if ntok is None else ntok
    tm = LANES
    blk0 = tok0 // tm
    nq = 2 * PEER_HEADS
    sk = subkeys.reshape(nq, PEER_NKEYS, PEER_HALF).astype(BF16)
    p0, p1, cpad, cpos = _peer_cand_tables()
    const = lambda shape: pl.BlockSpec(shape, lambda i: (0,) * len(shape))
    return pl.pallas_call(
        _peer_route_kernel,
        out_shape=(jax.ShapeDtypeStruct((T, PEER_HEADS * PEER_TOPK), jnp.int32),
                   jax.ShapeDtypeStruct((T // tm, PEER_HEADS * PEER_TOPK, tm), F32)),
        grid=(T // tm,),
        in_specs=[
            pl.BlockSpec((tm, D), lambda i: (blk0 + i, 0)),
            const((D, nq * PEER_HALF)),
            const((nq, PEER_NKEYS, PEER_HALF)),
            const((PEER_NCAND, PEER_TOPK)), const((PEER_NCAND, PEER_TOPK)),
            const((PEER_NCAND, LANES)), const((PEER_NCAND, LANES)),
        ],
        out_specs=(pl.BlockSpec((tm, PEER_HEADS * PEER_TOPK), lambda i: (i, 0)),
                   pl.BlockSpec((None, PEER_HEADS * PEER_TOPK, tm), lambda i: (i, 0, 0))),
        scratch_shapes=[
            pltpu.VMEM((nq, tm, PEER_HALF), BF16),
            pltpu.VMEM((PEER_HEADS, PEER_TOPK, tm), F32),
            pltpu.VMEM((PEER_HEADS, PEER_TOPK, tm), F32),
        ],
        compiler_params=_params(("parallel",)),
        name="peer_route",
    )(h2, wq.astype(BF16), sk, p0, p1, cpad, cpos)


PEER_G = 32
PEER_SLOTS = PEER_HEADS * PEER_TOPK


def _peer_eval_kernel(idx_ref, idxn_ref, gate_ref, h_ref, x_ref, gf_ref, fg_ref, uv_ref, o_ref, buf, sem, *, final):
    G = PEER_G
    R = G * PEER_SLOTS
    D = D_MODEL
    tiles = PEER_SLOTS // SUBLANES
    i = pl.program_id(0)
    n = pl.num_programs(0)

    def start(ids, off, s, t, u):
        pltpu.make_async_copy(uv_ref.at[ids[off + t * SUBLANES + u]], buf.at[s, t, pl.ds(u, 1), :],
                              sem.at[s]).start(priority=u % 2)

    def wait(s):
        pltpu.make_async_copy(buf.at[s], buf.at[s], sem.at[s]).wait()

    @pl.when(i == 0)
    def _():
        def body(t, carry):
            for u in range(SUBLANES):
                start(idx_ref, 0, 0, t, u)
            return carry
        lax.fori_loop(0, R // SUBLANES, body, 0)

    lane = lax.broadcasted_iota(jnp.int32, (1, LANES), 1)
    tbase = (i % (LANES // (2 * G))) * (2 * G)
    gate = gate_ref[...]
    outs = []
    for s in range(2):
        wait(s)
        nxt_ids, nxt_off = (idx_ref, R) if s == 0 else (idxn_ref, 0)
        for g in range(G):
            for t in range(tiles * g, tiles * (g + 1)):
                for u in range(SUBLANES):
                    start(nxt_ids, nxt_off, 1 - s, t, u)
            w_rows = buf[s, tiles * g:tiles * (g + 1)].reshape(PEER_SLOTS, D)
            u_rows = lax.bitcast_convert_type(w_rows & jnp.uint32(0xFFFF0000), F32)
            prod = u_rows * h_ref[G * s + g:G * s + g + 1, :]
            part = prod[:, 0:LANES]
            for c in range(1, D // LANES):
                part = part + prod[:, LANES * c:LANES * (c + 1)]
            act = jnp.sum(part, axis=1, keepdims=True)
            gcol = jnp.sum(jnp.where(lane == tbase + G * s + g, gate, 0.0), axis=1, keepdims=True)
            coef = gcol * (0.5 * act * (1.0 + lax.erf(act * (2.0 ** -0.5))))
            v_rows = lax.bitcast_convert_type(w_rows << 16, F32)
            outs.append(jnp.sum(v_rows * coef, axis=0, keepdims=True))
    xn = x_ref[...] + gf_ref[...] * jnp.concatenate(outs, axis=0)
    if final:
        ms = jnp.mean(xn * xn, axis=-1, keepdims=True)
        xn = xn * lax.rsqrt(ms + RMS_EPS) * fg_ref[...]
    o_ref[...] = xn

    @pl.when(i == n - 1)
    def _():
        wait(0)


def _peer_eval(eidx, gate_t, h2, x2, mod4, final_g, uv, S, final, ntok=None):
    T, D = x2.shape
    T = T if ntok is None else ntok
    G = PEER_G
    R = G * PEER_SLOTS
    n = T // (2 * G)
    return pl.pallas_call(
        functools.partial(_peer_eval_kernel, final=final),
        out_shape=jax.ShapeDtypeStruct(x2.shape, F32),
        input_output_aliases={4: 0},
        grid=(n,),
        in_specs=[
            pl.BlockSpec((2 * R,), lambda i: (i,), memory_space=pltpu.SMEM),
            pl.BlockSpec((R,), lambda i: (jnp.minimum(2 * i + 2, 2 * n - 2),), memory_space=pltpu.SMEM),
            pl.BlockSpec((None, PEER_SLOTS, LANES), lambda i: (i // (LANES // (2 * G)), 0, 0)),
            pl.BlockSpec((2 * G, D), lambda i: (i, 0)),
            pl.BlockSpec((2 * G, D), lambda i: (i, 0)),
            pl.BlockSpec((None, None, 1, D), lambda i: (i // (S // (2 * G)), 5, 0, 0)),
            pl.BlockSpec((1, D), lambda i: (0, 0)),
            pl.BlockSpec(memory_space=pl.ANY),
        ],
        out_specs=pl.BlockSpec((2 * G, D), lambda i: (i, 0)),
        scratch_shapes=[pltpu.VMEM((2, R // SUBLANES, SUBLANES, D), jnp.uint32), pltpu.SemaphoreType.DMA((2,))],
        compiler_params=_params(("arbitrary",)),
        name="peer_eval",
    )(eidx.reshape(-1), eidx.reshape(-1), gate_t, h2, x2, mod4, final_g.reshape(1, D), uv)


SC_WORKERS = 32
SC_WINDOW = 128
SC_ROWS = 32
PEER_GD = 32
SC_SHARE_NUM, SC_SHARE_DEN = 21, 32
SC_PARTS = 8
BATCH_GROUPS = 1


def _sc_gather(tab, idx):
    n = idx.shape[0]
    width = tab.shape[1]
    per = n // SC_WORKERS
    nsub = SC_WINDOW // SC_ROWS
    mesh = plsc.VectorSubcoreMesh(core_axis_name="core", subcore_axis_name="subcore")

    @pl.kernel(out_type=jax.ShapeDtypeStruct((n, width), tab.dtype), mesh=mesh,
               scratch_types=[pltpu.VMEM((SC_WINDOW,), jnp.int32), pltpu.VMEM((2, SC_ROWS, width), tab.dtype),
                              pltpu.SemaphoreType.DMA((2,)), pltpu.SemaphoreType.DMA((2,))])
    def gather(x_hbm, i_hbm, o_hbm, idx_v, rows_v, gsem, wsem):
        wid = lax.axis_index("core") * (SC_WORKERS // 2) + lax.axis_index("subcore")

        def start_gather(k):
            return pltpu.async_copy(x_hbm.at[idx_v.at[pl.ds(k * SC_ROWS, SC_ROWS)]], rows_v.at[k % 2], gsem.at[k % 2])

        @pl.loop(0, per // SC_WINDOW)
        def _(w):
            base = wid * per + w * SC_WINDOW
            pltpu.sync_copy(i_hbm.at[pl.ds(base, SC_WINDOW)], idx_v)
            gat = [start_gather(0)] + [None] * (nsub - 1)
            wrt = [None] * nsub
            for k in range(nsub):
                if k + 1 < nsub:
                    if k >= 1:
                        wrt[k - 1].wait()
                    gat[k + 1] = start_gather(k + 1)
                gat[k].wait()
                wrt[k] = pltpu.async_copy(rows_v.at[k % 2], o_hbm.at[pl.ds(base + k * SC_ROWS, SC_ROWS)],
                                          wsem.at[k % 2])
            for k in range(max(nsub - 2, 0), nsub):
                wrt[k].wait()

    return gather(tab, idx)


def _peer_dense_kernel(rows_ref, gate_ref, h_ref, x_ref, gf_ref, fg_ref, o_ref, *, final):
    G = PEER_GD
    D = D_MODEL
    i = pl.program_id(0)
    lane = lax.broadcasted_iota(jnp.int32, (1, LANES), 1)
    tbase = (i % (LANES // G)) * G
    gate = gate_ref[...]
    outs = []
    for g in range(G):
        w_rows = rows_ref[PEER_SLOTS * g:PEER_SLOTS * (g + 1), :]
        u_rows = lax.bitcast_convert_type(w_rows & jnp.uint32(0xFFFF0000), F32)
        prod = u_rows * h_ref[g:g + 1, :]
        part = prod[:, 0:LANES]
        for c in range(1, D // LANES):
            part = part + prod[:, LANES * c:LANES * (c + 1)]
        act = jnp.sum(part, axis=1, keepdims=True)
        gcol = jnp.sum(jnp.where(lane == tbase + g, gate, 0.0), axis=1, keepdims=True)
        coef = gcol * (0.5 * act * (1.0 + lax.erf(act * (2.0 ** -0.5))))
        v_rows = lax.bitcast_convert_type(w_rows << 16, F32)
        outs.append(jnp.sum(v_rows * coef, axis=0, keepdims=True))
    xn = x_ref[...] + gf_ref[...] * jnp.concatenate(outs, axis=0)
    if final:
        ms = jnp.mean(xn * xn, axis=-1, keepdims=True)
        xn = xn * lax.rsqrt(ms + RMS_EPS) * fg_ref[...]
    o_ref[...] = xn


def _peer_eval_dense(rows, gate_t, h2, x2, mod4, final_g, S, final, tok0):
    T, D = x2.shape
    G = PEER_GD
    tb = rows.shape[0] // PEER_SLOTS
    assert tok0 % LANES == 0 and tb % LANES == 0
    blk0 = tok0 // G
    return pl.pallas_call(
        functools.partial(_peer_dense_kernel, final=final),
        out_shape=jax.ShapeDtypeStruct(x2.shape, F32),
        input_output_aliases={3: 0},
        grid=(tb // G,),
        in_specs=[
            pl.BlockSpec((G * PEER_SLOTS, D), lambda i: (i, 0)),
            pl.BlockSpec((None, PEER_SLOTS, LANES), lambda i: (i // (LANES // G), 0, 0)),
            pl.BlockSpec((G, D), lambda i: (blk0 + i, 0)),
            pl.BlockSpec((G, D), lambda i: (blk0 + i, 0)),
            pl.BlockSpec((None, None, 1, D), lambda i: ((blk0 + i) // (S // G), 5, 0, 0)),
            pl.BlockSpec((1, D), lambda i: (0, 0)),
        ],
        out_specs=pl.BlockSpec((G, D), lambda i: (blk0 + i, 0)),
        compiler_params=_params(("parallel",)),
        name="peer_dense",
    )(rows, gate_t, h2, x2, mod4, final_g.reshape(1, D))


def _pack_uv(u, v):
    hi = lax.bitcast_convert_type(u.astype(BF16), jnp.uint16).astype(jnp.uint32) << 16
    lo = lax.bitcast_convert_type(v.astype(BF16), jnp.uint16).astype(jnp.uint32)
    return (hi | lo).reshape(u.shape[0], 1, u.shape[1])


def kernel(x, c, norm_mix_g, norm_ffn_g, final_norm_g, ada_w, ada_b, w_in, w_out, dif_lam, dif_subln_g, rw_mu, rw_w0,
           rw_w2, rw_a0, rw_a2, rw_g2, rw_kk, rw_ka, rw_rk, rw_ln_g, rw_ln_b, fox_bf, peer_wq, peer_subkeys, peer_u,
           peer_v):
    B, S, D = x.shape
    depth = ada_w.shape[0]
    mod = _ada_mod(c, ada_w, ada_b)
    ngrp = BATCH_GROUPS if B % BATCH_GROUPS == 0 else 1
    bg = B // ngrp
    tg = bg * S
    prep = [(_pad_w_in(w_in[l]), _pack_uv(peer_u[l], peer_v[l])) for l in range(depth)]

    def mix_route(l, g, x2):
        mod4 = mod[l, g * bg:(g + 1) * bg].reshape(bg, 6, 1, D)
        (w_pad, w_vt), uv = prep[l]
        dqk, rcols, fqk, fl, vtd, vtf = _in_proj(x2, norm_mix_g[l], mod4, w_pad, w_vt, S)
        ya = _diff_attention(dqk, vtd, dif_lam[l], dif_subln_g[l], l, bg, S)
        yb = _rwkv(rcols, rw_mu[l], rw_w0[l], rw_w2[l], rw_a0[l], rw_a2[l], rw_g2[l], rw_kk[l], rw_ka[l],
                   rw_rk[l].reshape(-1), rw_ln_g[l], rw_ln_b[l], bg, S, prec=None)
        crep, crow = _fox_cum(fl, fox_bf[l], bg, S)
        yc = _fox_attention(fqk, vtf, crep, crow, bg, S)
        x2, h2 = _out_proj(ya, yb, yc, x2, w_out[l], norm_ffn_g[l], mod4, S)
        unit = SC_PARTS * LANES
        tb = (tg * SC_SHARE_NUM // SC_SHARE_DEN) // unit * unit
        ta = tg - tb
        bounds = [ta + tb * j // SC_PARTS for j in range(SC_PARTS + 1)]
        parts = []
        for t0, t1 in zip(bounds[:-1], bounds[1:]):
            if t1 == t0:
                continue
            e_p, g_p = _peer_route(h2, peer_wq[l], peer_subkeys[l], t0, t1 - t0)
            parts.append((_sc_gather(uv.reshape(-1, D), e_p.reshape(-1)), g_p, t0))
        e_a, g_a = _peer_route(h2, peer_wq[l], peer_subkeys[l], 0, ta)
        return dict(x2=x2, h2=h2, mod4=mod4, uv=uv, ta=ta, e_a=e_a, g_a=g_a, parts=parts)

    def evaluate(l, st):
        final = l == depth - 1
        x2 = _peer_eval(st["e_a"], st["g_a"], st["h2"], st["x2"], st["mod4"], final_norm_g, st["uv"], S, final,
                        ntok=st["ta"])
        for r, g_p, t0 in st["parts"]:
            x2 = _peer_eval_dense(r, g_p, st["h2"], x2, st["mod4"], final_norm_g, S, final, t0)
        return x2

    xs = [x[g * bg:(g + 1) * bg].reshape(tg, D) for g in range(ngrp)]
    pending = []
    for l in range(depth):
        for g in range(ngrp):
            pending.append((l, g, mix_route(l, g, xs[g])))
            if len(pending) == ngrp:
                l0, g0, st = pending.pop(0)
                xs[g0] = evaluate(l0, st)
    for l0, g0, st in pending:
        xs[g0] = evaluate(l0, st)
    return jnp.concatenate(xs, axis=0).reshape(B, S, D)
```

```python
import functools
import math

import jax
import jax.numpy as jnp
from jax import lax
from jax.experimental import pallas as pl
from jax.experimental.pallas import tpu as pltpu
from jax.experimental.pallas import tpu_sc as plsc

F32 = jnp.float32
BF16 = jnp.bfloat16
HIGHEST = lax.Precision.HIGHEST

D_MODEL = 1024
HEAD_DIM = 64
DIFF_HEADS = 6
DIFF_QK_DIM = HEAD_DIM // 2
RWKV_HEADS = 6
FOX_HEADS = 4
DIFF_WIDTH = DIFF_HEADS * HEAD_DIM
RWKV_WIDTH = RWKV_HEADS * HEAD_DIM
FOX_WIDTH = FOX_HEADS * HEAD_DIM
DECAY_LORA = 64
AAA_LORA = 64
GATE_LORA = 128
DIFF_COLS = 3 * DIFF_WIDTH
RWKV_COLS = 3 * RWKV_WIDTH + DECAY_LORA + AAA_LORA + GATE_LORA
PEER_HEADS = 8
PEER_NKEYS = 128
PEER_TOPK = 16
PEER_QDIM = 256
PEER_HALF = PEER_QDIM // 2
RMS_EPS = 1e-6
RWKV_LN_EPS = 64e-5

LANES = 128
SUBLANES = 8
RW_PAD_COLS = 3 * RWKV_WIDTH + 3 * LANES
VMEM_LIMIT = 56 * 1024 * 1024

RW_CHUNK = 64
RW_CHUNKS_PER_ITER = 4


def _params(sem, vmem=VMEM_LIMIT):
    return pltpu.CompilerParams(dimension_semantics=sem, vmem_limit_bytes=vmem)


def _nt(a, b, precision=None):
    return lax.dot_general(a, b, (((1,), (1,)), ((), ())), preferred_element_type=F32, precision=precision)


def _mm(a, b, precision=None):
    return jnp.dot(a, b, preferred_element_type=F32, precision=precision)


def _ada_kernel(c_ref, w_ref, b_ref, o_ref):
    c = c_ref[...]
    ca = c * jax.nn.sigmoid(c)
    o_ref[...] = _mm(ca, w_ref[...], HIGHEST) + b_ref[...]


def _ada_mod(c, ada_w, ada_b):
    L, D, N = ada_w.shape
    B = c.shape[0]
    tn = 1536
    return pl.pallas_call(
        _ada_kernel,
        out_shape=jax.ShapeDtypeStruct((L, B, N), F32),
        grid=(L, N // tn),
        in_specs=[
            pl.BlockSpec((B, D), lambda l, j: (0, 0)),
            pl.BlockSpec((None, D, tn), lambda l, j: (l, 0, j)),
            pl.BlockSpec((None, 1, tn), lambda l, j: (l, 0, j)),
        ],
        out_specs=pl.BlockSpec((None, B, tn), lambda l, j: (l, 0, j)),
        compiler_params=_params(("parallel", "parallel")),
        name="ada_mod",
    )(c, ada_w, ada_b.reshape(L, 1, N))


ATT_T = 512
QK_DIFF = 2 * DIFF_WIDTH
QK_FOX = 2 * FOX_WIDTH
VT_ROWS = DIFF_WIDTH + FOX_WIDTH
IN_PAD_COLS = QK_DIFF + RW_PAD_COLS + QK_FOX + LANES


def _inproj_kernel(x_ref, g_ref, sh_ref, sc_ref, w_ref, wvt_ref, d_ref, r_ref, f_ref, fl_ref, vtd_ref, vtf_ref, *, ta):
    x = x_ref[...]
    ms = jnp.mean(x * x, axis=-1, keepdims=True)
    y = x * lax.rsqrt(ms + RMS_EPS) * g_ref[...]
    h = (y * (1.0 + sc_ref[...]) + sh_ref[...]).astype(BF16)
    o1 = QK_DIFF
    o2 = o1 + RW_PAD_COLS
    o3 = o2 + QK_FOX
    d_ref[...] = _mm(h, w_ref[:, 0:o1]).astype(BF16)
    r_ref[...] = _mm(h, w_ref[:, o1:o2])
    f_ref[...] = _mm(h, w_ref[:, o2:o3]).astype(BF16)
    fl_ref[...] = _mm(h, w_ref[:, o3:o3 + LANES])
    vt = _nt(wvt_ref[...], h).astype(BF16)
    for s in range(x.shape[0] // ta):
        vtd_ref[s] = vt[0:DIFF_WIDTH, ta * s:ta * (s + 1)]
        vtf_ref[s] = vt[DIFF_WIDTH:, ta * s:ta * (s + 1)]


def _in_proj(x2, g, mod4, w_pad, w_vt, S):
    T, D = x2.shape
    tm = 512 if S % 512 == 0 else S
    ta = min(ATT_T, S)
    nb = S // tm
    row = lambda i: (i, 0)
    return pl.pallas_call(
        functools.partial(_inproj_kernel, ta=ta),
        out_shape=(
            jax.ShapeDtypeStruct((T, QK_DIFF), BF16),
            jax.ShapeDtypeStruct((T, RW_PAD_COLS), F32),
            jax.ShapeDtypeStruct((T, QK_FOX), BF16),
            jax.ShapeDtypeStruct((T, LANES), F32),
            jax.ShapeDtypeStruct((T // ta, DIFF_WIDTH, ta), BF16),
            jax.ShapeDtypeStruct((T // ta, FOX_WIDTH, ta), BF16),
        ),
        grid=(T // tm,),
        in_specs=[
            pl.BlockSpec((tm, D), row),
            pl.BlockSpec((1, D), lambda i: (0, 0)),
            pl.BlockSpec((None, None, 1, D), lambda i: (i // nb, 0, 0, 0)),
            pl.BlockSpec((None, None, 1, D), lambda i: (i // nb, 1, 0, 0)),
            pl.BlockSpec((D, IN_PAD_COLS), lambda i: (0, 0)),
            pl.BlockSpec((VT_ROWS, D), lambda i: (0, 0)),
        ],
        out_specs=(
            pl.BlockSpec((tm, QK_DIFF), row),
            pl.BlockSpec((tm, RW_PAD_COLS), row),
            pl.BlockSpec((tm, QK_FOX), row),
            pl.BlockSpec((tm, LANES), row),
            pl.BlockSpec((tm // ta, DIFF_WIDTH, ta), lambda i: (i, 0, 0)),
            pl.BlockSpec((tm // ta, FOX_WIDTH, ta), lambda i: (i, 0, 0)),
        ),
        compiler_params=_params(("parallel",)),
        name="in_proj",
    )(x2, g.reshape(1, D), mod4, mod4, w_pad, w_vt)


def _pad_w_in(w_in):
    D = w_in.shape[0]
    W = RWKV_WIDTH
    o = DIFF_COLS
    z64 = jnp.zeros((D, LANES - DECAY_LORA), w_in.dtype)
    rw = w_in[:, o:o + RWKV_COLS]
    fx = w_in[:, o + RWKV_COLS:]
    zf = jnp.zeros((D, LANES - FOX_HEADS), w_in.dtype)
    w_pad = jnp.concatenate([
        w_in[:, :QK_DIFF],
        rw[:, :3 * W], rw[:, 3 * W:3 * W + DECAY_LORA], z64,
        rw[:, 3 * W + DECAY_LORA:3 * W + DECAY_LORA + AAA_LORA], z64,
        rw[:, 3 * W + DECAY_LORA + AAA_LORA:],
        fx[:, :QK_FOX], fx[:, 3 * FOX_WIDTH:], zf,
    ], axis=1).astype(BF16)
    w_vt = jnp.concatenate([w_in[:, QK_DIFF:o], fx[:, QK_FOX:3 * FOX_WIDTH]], axis=1).T.astype(BF16)
    return w_pad, w_vt


LOG2E = math.log2(math.e)


ACC_ROWS = LANES + 16


def _scaled_q(q, c):
    return (q.astype(F32) * c).astype(BF16)


def _with_ones(vt):
    return jnp.concatenate([vt, jnp.ones((ACC_ROWS - LANES, vt.shape[1]), vt.dtype)], axis=0)


def _flash_step(s2s, vt1, m_ref, acc_ref):
    n = len(s2s)
    m_old = [m_ref[x] for x in range(n)]
    m_new = [jnp.maximum(m_old[x], jnp.max(s2s[x], axis=0, keepdims=True)) for x in range(n)]
    alpha = [jnp.exp2(m_old[x] - m_new[x]) for x in range(n)]
    p = [jnp.exp2(s2s[x] - m_new[x]).astype(BF16) for x in range(n)]
    pv = [_mm(vt1, p[x]) for x in range(n)]
    for x in range(n):
        acc_ref[x] = alpha[x] * acc_ref[x] + pv[x]
        m_ref[x] = m_new[x]


def _diff_attn_kernel(lam_ref, g_ref, q_ref, k_ref, vt_ref, o_ref, m_ref, acc_ref, *, tq, lam_init):
    i = pl.program_id(2)
    lane = lax.broadcasted_iota(jnp.int32, (1, LANES), 1)
    q = _scaled_q(q_ref[...], (DIFF_QK_DIM ** -0.5) * LOG2E)
    zero = jnp.zeros_like(q)
    qm = [jnp.where((lane >= DIFF_QK_DIM * x) & (lane < DIFF_QK_DIM * (x + 1)), q, zero) for x in range(4)]
    m_ref[...] = jnp.full(m_ref.shape, -jnp.inf, F32)
    acc_ref[...] = jnp.zeros(acc_ref.shape, F32)

    def step(j, diag):
        k = k_ref[pl.ds(pl.multiple_of(j * tq, tq), tq), :]
        vt1 = _with_ones(vt_ref[j])
        if diag:
            keep = (lax.broadcasted_iota(jnp.int32, (tq, tq), 1) >= lax.broadcasted_iota(jnp.int32, (tq, tq), 0))
        s2s = [_nt(k, qm[x]) for x in range(4)]
        if diag:
            s2s = [jnp.where(keep, s2, -jnp.inf) for s2 in s2s]
        _flash_step(s2s, vt1, m_ref, acc_ref)

    def body(j, carry):
        step(j, False)
        return carry

    lax.fori_loop(0, i, body, 0)
    step(i, True)

    lp = lam_ref[...]
    lam = (jnp.exp(jnp.sum(lp[0:1] * lp[1:2], axis=-1, keepdims=True))
           - jnp.exp(jnp.sum(lp[2:3] * lp[3:4], axis=-1, keepdims=True)) + lam_init)
    sm = [acc_ref[x, 0:LANES, :] / acc_ref[x, LANES:LANES + 1, :] for x in range(4)]
    outs = [sm[2 * hh] - lam * sm[2 * hh + 1] for hh in range(2)]
    row = lax.broadcasted_iota(jnp.int32, (LANES, 1), 0)
    o = jnp.where(row < HEAD_DIM, outs[0], outs[1])
    sq = o * o
    ms = jnp.where(row < HEAD_DIM, jnp.sum(sq[0:HEAD_DIM], axis=0, keepdims=True),
                   jnp.sum(sq[HEAD_DIM:], axis=0, keepdims=True)) * (1.0 / HEAD_DIM)
    y = o * lax.rsqrt(ms + RMS_EPS) * g_ref[...] * (1.0 - lam_init)
    o_ref[...] = y.T.astype(o_ref.dtype)


def _diff_attention(dqk, vtd, lam_params, subln_g, layer_idx, B, S):
    T = B * S
    tq = min(ATT_T, S)
    nq = S // tq
    npair = DIFF_HEADS // 2
    lam_init = 0.8 - 0.6 * math.exp(-0.3 * layer_idx)
    g2 = jnp.concatenate([subln_g, subln_g]).reshape(LANES, 1).astype(F32)
    return pl.pallas_call(
        functools.partial(_diff_attn_kernel, tq=tq, lam_init=lam_init),
        out_shape=jax.ShapeDtypeStruct((T, DIFF_WIDTH), BF16),
        grid=(B, npair, nq),
        in_specs=[
            pl.BlockSpec((4, DIFF_QK_DIM), lambda b, p, i: (0, 0)),
            pl.BlockSpec((LANES, 1), lambda b, p, i: (0, 0)),
            pl.BlockSpec((tq, LANES), lambda b, p, i: (b * nq + i, p)),
            pl.BlockSpec((S, LANES), lambda b, p, i: (b, npair + p)),
            pl.BlockSpec((nq, LANES, tq), lambda b, p, i: (b, p, 0)),
        ],
        out_specs=pl.BlockSpec((tq, LANES), lambda b, p, i: (b * nq + i, p)),
        scratch_shapes=[
            pltpu.VMEM((4, 1, tq), F32),
            pltpu.VMEM((4, ACC_ROWS, tq), F32),
        ],
        compiler_params=_params(("parallel", "parallel", "arbitrary")),
        name="diff_attn",
    )(lam_params, g2, dqk, dqk, vtd)


def _fox_cum_kernel(f_ref, b_ref, rep_ref, row_ref, *, S, tc):
    rr = lax.broadcasted_iota(jnp.int32, (tc, tc), 0)
    cc = lax.broadcasted_iota(jnp.int32, (tc, tc), 1)
    tri = (rr >= cc).astype(F32)
    sel_r = lax.broadcasted_iota(jnp.int32, (LANES, LANES), 0)
    carry = jnp.zeros((1, LANES), F32)
    for c in range(S // tc):
        z = f_ref[c * tc:(c + 1) * tc, :] + b_ref[...]
        logf = -(jnp.maximum(-z, 0.0) + jnp.log(1.0 + jnp.exp(-jnp.abs(z))))
        cum = _mm(tri, logf, HIGHEST) + carry
        carry = cum[tc - 1:tc, :]
        row_ref[:, c * tc:(c + 1) * tc] = cum.T[0:8, :]
        for h in range(FOX_HEADS):
            rep_ref[h, c * tc:(c + 1) * tc, :] = _mm(cum, (sel_r == h).astype(F32), HIGHEST)


def _fox_cum(fl, b_f, B, S):
    tc = 256 if S % 256 == 0 else S
    bpad = jnp.zeros((1, LANES), F32).at[0, :FOX_HEADS].set(b_f.astype(F32))
    return pl.pallas_call(
        functools.partial(_fox_cum_kernel, S=S, tc=tc),
        out_shape=(jax.ShapeDtypeStruct((B, FOX_HEADS, S, LANES), F32), jax.ShapeDtypeStruct((B, 8, S), F32)),
        grid=(B,),
        in_specs=[pl.BlockSpec((S, LANES), lambda b: (b, 0)), pl.BlockSpec((1, LANES), lambda b: (0, 0))],
        out_specs=(pl.BlockSpec((None, FOX_HEADS, S, LANES), lambda b: (b, 0, 0, 0)),
                   pl.BlockSpec((None, 8, S), lambda b: (b, 0, 0))),
        compiler_params=_params(("parallel",)),
        name="fox_cum",
    )(fl, bpad)


def _fox_attn_kernel(q_ref, k_ref, vt_ref, c0_ref, c1_ref, cr_ref, o_ref, m_ref, acc_ref, *, tq):
    p_id = pl.program_id(1)
    i = pl.program_id(2)
    lane = lax.broadcasted_iota(jnp.int32, (1, LANES), 1)
    q = _scaled_q(q_ref[...], (HEAD_DIM ** -0.5) * LOG2E)
    zero = jnp.zeros_like(q)
    qm = [jnp.where((lane >= HEAD_DIM * x) & (lane < HEAD_DIM * (x + 1)), q, zero) for x in range(2)]
    ck_refs = (c0_ref, c1_ref)
    cq = [cr_ref[2 * p_id + x, pl.ds(i, 1), :] for x in range(2)]
    m_ref[...] = jnp.full(m_ref.shape, -jnp.inf, F32)
    acc_ref[...] = jnp.zeros(acc_ref.shape, F32)

    def step(j, diag):
        off = pl.multiple_of(j * tq, tq)
        k = k_ref[pl.ds(off, tq), :]
        vt1 = _with_ones(vt_ref[j])
        if diag:
            keep = (lax.broadcasted_iota(jnp.int32, (tq, tq), 1) >= lax.broadcasted_iota(jnp.int32, (tq, tq), 0))
        s2s = []
        for x in range(2):
            ck = ck_refs[x][pl.ds(off, tq), :]
            bias = (cq[x] - jnp.concatenate([ck] * (tq // LANES), axis=1)) * LOG2E
            s2s.append(_nt(k, qm[x]) + bias)
        if diag:
            s2s = [jnp.where(keep, s2, -jnp.inf) for s2 in s2s]
        _flash_step(s2s, vt1, m_ref, acc_ref)

    def body(j, carry):
        step(j, False)
        return carry

    lax.fori_loop(0, i, body, 0)
    step(i, True)
    row = lax.broadcasted_iota(jnp.int32, (LANES, 1), 0)
    sm = [acc_ref[x, 0:LANES, :] / acc_ref[x, LANES:LANES + 1, :] for x in range(2)]
    o = jnp.where(row < HEAD_DIM, sm[0], sm[1])
    o_ref[...] = o.T.astype(o_ref.dtype)


def _fox_attention(fqk, vtf, crep, crow, B, S):
    T = B * S
    tq = min(ATT_T, S)
    nq = S // tq
    npair = FOX_HEADS // 2
    crow4 = crow.reshape(B, 8, nq, tq)
    rep = lambda x: pl.BlockSpec((None, None, S, LANES), lambda b, p, i: (b, 2 * p + x, 0, 0))
    return pl.pallas_call(
        functools.partial(_fox_attn_kernel, tq=tq),
        out_shape=jax.ShapeDtypeStruct((T, FOX_WIDTH), BF16),
        grid=(B, npair, nq),
        in_specs=[
            pl.BlockSpec((tq, LANES), lambda b, p, i: (b * nq + i, p)),
            pl.BlockSpec((S, LANES), lambda b, p, i: (b, npair + p)),
            pl.BlockSpec((nq, LANES, tq), lambda b, p, i: (b, p, 0)),
            rep(0), rep(1),
            pl.BlockSpec((None, 8, nq, tq), lambda b, p, i: (b, 0, 0, 0)),
        ],
        out_specs=pl.BlockSpec((tq, LANES), lambda b, p, i: (b * nq + i, p)),
        scratch_shapes=[
            pltpu.VMEM((2, 1, tq), F32),
            pltpu.VMEM((2, ACC_ROWS, tq), F32),
        ],
        compiler_params=_params(("parallel", "parallel", "arbitrary")),
        name="fox_attn",
    )(fqk, fqk, vtf, crep, crep, crow4)


def _split3(x):
    hi = x.astype(BF16)
    r1 = x - hi.astype(F32)
    mid = r1.astype(BF16)
    lo = (r1 - mid.astype(F32)).astype(BF16)
    return hi, mid, lo


def _seg_sum(x, seg, npair):
    parts = _split3(x)
    return jnp.concatenate(
        [sum(_mm(t[:, LANES * p:LANES * (p + 1)], seg) for t in parts) for p in range(npair)], axis=1)


def _rwkv_kernel(x_ref, mu_ref, w0_ref, w2_ref, a0_ref, a2_ref, g2_ref, kk_ref, ka_ref, rk_ref, lng_ref, lnb_ref,
                 o_ref, carry_ref, st_ref, at_ref, rt_ref, bt_ref, kt_ref, v_ref, wc_ref, y_ref, g_ref, bon_ref,
                 *, tt, prec):
    i = pl.program_id(1)
    W = RWKV_WIDTH
    C = RW_CHUNK
    CPI = math.gcd(RW_CHUNKS_PER_ITER, tt // C)
    npair = RWKV_HEADS // 2

    @pl.when(i == 0)
    def _():
        carry_ref[...] = jnp.zeros(carry_ref.shape, F32)
        st_ref[...] = jnp.zeros(st_ref.shape, F32)

    x = x_ref[...]
    rows = lax.broadcasted_iota(jnp.int32, (tt, 1), 0)
    prev = jnp.where(rows == 0, carry_ref[...], pltpu.roll(x, 1, axis=0))
    carry_ref[...] = x[tt - 1:tt, :]
    xs = x + (prev - x) * mu_ref[...]
    r = xs[:, 0:W]
    k = xs[:, W:2 * W]
    v = xs[:, 2 * W:3 * W]
    xw = xs[:, 3 * W:3 * W + LANES]
    xa = xs[:, 3 * W + LANES:3 * W + 2 * LANES]
    xg = xs[:, 3 * W + 2 * LANES:]
    wl = w0_ref[...] + _mm(jnp.tanh(xw), w2_ref[...], HIGHEST)
    w = -(jnp.maximum(-wl, 0.0) + jnp.log(1.0 + jnp.exp(-jnp.abs(wl)))) - 0.5
    logdec = -jnp.exp(w)
    a = jax.nn.sigmoid(a0_ref[...] + _mm(xa, a2_ref[...], HIGHEST))
    g_ref[...] = _mm(jax.nn.sigmoid(xg), g2_ref[...], HIGHEST)

    r_i = lax.broadcasted_iota(jnp.int32, (LANES, LANES), 0)
    c_i = lax.broadcasted_iota(jnp.int32, (LANES, LANES), 1)
    seg = ((r_i // HEAD_DIM) == (c_i // HEAD_DIM)).astype(BF16)
    kkv = k * kk_ref[...]
    kkn = kkv / jnp.maximum(jnp.sqrt(_seg_sum(kkv * kkv, seg, npair)), 1e-12)
    k2 = k * (1.0 + (a - 1.0) * ka_ref[...])
    bon_ref[...] = _seg_sum(r * k2 * rk_ref[...], seg, npair) * v

    rt_i = lax.broadcasted_iota(jnp.int32, (tt, tt), 0)
    ct_i = lax.broadcasted_iota(jnp.int32, (tt, tt), 1)
    tri = (((rt_i // C) == (ct_i // C)) & (rt_i >= ct_i)).astype(BF16)
    cum = sum(_mm(tri, t) for t in _split3(logdec))
    winv = jnp.exp(-cum)
    wcum = jnp.exp(cum)
    at_ref[...] = -kkn * jnp.exp(cum - logdec)
    bt_ref[...] = kkn * a * winv
    kt_ref[...] = k2 * winv
    rt_ref[...] = r * wcum
    v_ref[...] = v
    wc_ref[...] = wcum

    lane = lax.broadcasted_iota(jnp.int32, (1, LANES), 1)
    lo = lane < HEAD_DIM
    tpos = r_i % C
    ipos = c_i % C
    strict = tpos > ipos
    incl = tpos >= ipos
    eye = r_i == c_i

    def stack2(m):
        return jnp.concatenate([jnp.where(lo, m, 0.0), jnp.where(lo, 0.0, m)], axis=0)

    def cast(m):
        return m if prec is not None else m.astype(BF16)

    def mm(p, q):
        return _mm(cast(p), cast(q), prec)

    def nt(p, q):
        return _nt(cast(p), cast(q), prec)

    def chunks(cc, carry):
        units = [(ci, p) for ci in range(CPI) for p in range(npair)]
        r0 = [pl.multiple_of((cc * CPI + ci) * C, C) for ci in range(CPI)]
        ld = lambda ref, ci, p: stack2(ref[pl.ds(r0[ci], C), LANES * p:LANES * (p + 1)])
        a2 = {u: ld(at_ref, *u) for u in units}
        r2 = {u: ld(rt_ref, *u) for u in units}
        b2 = {u: ld(bt_ref, *u) for u in units}
        k2s = {u: ld(kt_ref, *u) for u in units}
        v2 = {u: ld(v_ref, *u) for u in units}
        wl = {(ci, p): wc_ref[pl.ds(pl.multiple_of(r0[ci] + C - 8, 8), 8), LANES * p:LANES * (p + 1)][7:8, :]
              for (ci, p) in units}
        ar = {u: jnp.concatenate([a2[u], r2[u]], axis=0) for u in units}
        mb = {u: nt(ar[u], b2[u]) for u in units}
        mk = {u: nt(ar[u], k2s[u]) for u in units}
        lab = {u: jnp.where(strict, mb[u][0:LANES], 0.0) for u in units}
        mrb = {u: jnp.where(incl, mb[u][LANES:], 0.0) for u in units}
        lak = {u: jnp.where(strict, mk[u][0:LANES], 0.0) for u in units}
        mrk = {u: jnp.where(incl, mk[u][LANES:], 0.0) for u in units}
        xx = {u: jnp.concatenate([a2[u], mm(lak[u], v2[u])], axis=1) for u in units}
        lp = lab
        for it in range(6):
            xx = {u: xx[u] + mm(lp[u], xx[u]) for u in units}
            if it < 5:
                lp = {u: mm(lp[u], lp[u]) for u in units}
        mq = {u: mm(mrb[u], xx[u]) for u in units}
        mv = {u: mm(mrk[u], v2[u]) for u in units}
        bx = {u: mm((b2[u] * wl[u]).T, xx[u]) for u in units}
        kv = {u: mm((k2s[u] * wl[u]).T, v2[u]) for u in units}
        st = [st_ref[p] for p in range(npair)]
        for (ci, p) in units:
            u = (ci, p)
            q1 = r2[u] + mq[u][:, 0:LANES]
            q2 = mq[u][:, LANES:] + mv[u]
            gmat = jnp.where(eye, jnp.broadcast_to(wl[u], (LANES, LANES)), 0.0) + bx[u][:, 0:LANES]
            hmat = bx[u][:, LANES:] + kv[u]
            gs = mm(jnp.concatenate([gmat, q1], axis=0), st[p])
            st[p] = gs[0:LANES] + hmat
            yy = gs[LANES:] + q2
            y_ref[pl.ds(r0[ci], C), LANES * p:LANES * (p + 1)] = yy[0:C] + yy[C:]
        for p in range(npair):
            st_ref[p] = st[p]
        return carry

    lax.fori_loop(0, tt // (C * CPI), chunks, 0)

    y = y_ref[...]
    inv = 1.0 / HEAD_DIM
    mean = _seg_sum(y, seg, npair) * inv
    yc = y - mean
    var = _seg_sum(yc * yc, seg, npair) * inv
    yn = yc * lax.rsqrt(var + RWKV_LN_EPS) * lng_ref[...] + lnb_ref[...]
    o_ref[...] = ((yn + bon_ref[...]) * g_ref[...]).astype(o_ref.dtype)


def _rwkv(rcols, mu, w0, w2, a0, a2, g2, k_k, k_a, r_k, ln_g, ln_b, B, S, prec=HIGHEST):
    T = B * S
    W = RWKV_WIDTH
    tt = 512 if S % 512 == 0 else S
    nt_ = S // tt
    npair = RWKV_HEADS // 2
    pad = LANES - DECAY_LORA
    mu_p = jnp.concatenate([mu[:3 * W], mu[3 * W:3 * W + DECAY_LORA], jnp.zeros((pad,), F32),
                            mu[3 * W + DECAY_LORA:3 * W + DECAY_LORA + AAA_LORA], jnp.zeros((pad,), F32),
                            mu[3 * W + DECAY_LORA + AAA_LORA:]]).reshape(1, RW_PAD_COLS)
    w2p = jnp.concatenate([w2, jnp.zeros((pad, W), F32)], axis=0)
    a2p = jnp.concatenate([a2, jnp.zeros((pad, W), F32)], axis=0)
    vec = lambda t: t.reshape(1, W).astype(F32)
    full = lambda shape: pl.BlockSpec(shape, lambda b, i: (0,) * len(shape))
    sc = lambda: pltpu.VMEM((tt, W), F32)
    return pl.pallas_call(
        functools.partial(_rwkv_kernel, tt=tt, prec=prec),
        out_shape=jax.ShapeDtypeStruct((T, W), BF16),
        grid=(B, nt_),
        in_specs=[
            pl.BlockSpec((tt, RW_PAD_COLS), lambda b, i: (b * nt_ + i, 0)),
            full((1, RW_PAD_COLS)), full((1, W)), full((LANES, W)), full((1, W)), full((LANES, W)),
            full((GATE_LORA, W)), full((1, W)), full((1, W)), full((1, W)), full((1, W)), full((1, W)),
        ],
        out_specs=pl.BlockSpec((tt, W), lambda b, i: (b * nt_ + i, 0)),
        scratch_shapes=[
            pltpu.VMEM((1, RW_PAD_COLS), F32),
            pltpu.VMEM((npair, LANES, LANES), F32),
            sc(), sc(), sc(), sc(), sc(), sc(), sc(), sc(), sc(),
        ],
        compiler_params=_params(("parallel", "arbitrary")),
        name="rwkv7",
    )(rcols, mu_p, vec(w0), w2p, vec(a0), a2p, g2, vec(k_k), vec(k_a), vec(r_k), vec(ln_g), vec(ln_b))


def _outproj_kernel(ya_ref, yb_ref, yc_ref, x_ref, w_ref, gm_ref, g_ref, sh_ref, sc_ref, xo_ref, ho_ref):
    o1 = DIFF_WIDTH
    o2 = o1 + RWKV_WIDTH
    mix = (_mm(ya_ref[...], w_ref[0:o1, :]) + _mm(yb_ref[...], w_ref[o1:o2, :])
           + _mm(yc_ref[...], w_ref[o2:, :]))
    xn = x_ref[...] + gm_ref[...] * mix
    xo_ref[...] = xn
    ms = jnp.mean(xn * xn, axis=-1, keepdims=True)
    y = xn * lax.rsqrt(ms + RMS_EPS) * g_ref[...]
    ho_ref[...] = y * (1.0 + sc_ref[...]) + sh_ref[...]


def _out_proj(ya, yb, yc, x2, w_out, g, mod4, S):
    T, D = x2.shape
    tm = 512 if S % 512 == 0 else S
    nb = S // tm
    row = lambda i: (i, 0)
    modspec = lambda which: pl.BlockSpec((None, None, 1, D), lambda i: (i // nb, which, 0, 0))
    return pl.pallas_call(
        _outproj_kernel,
        out_shape=(jax.ShapeDtypeStruct((T, D), F32), jax.ShapeDtypeStruct((T, D), F32)),
        grid=(T // tm,),
        in_specs=[
            pl.BlockSpec((tm, DIFF_WIDTH), row), pl.BlockSpec((tm, RWKV_WIDTH), row), pl.BlockSpec((tm, FOX_WIDTH), row),
            pl.BlockSpec((tm, D), row),
            pl.BlockSpec((D, D), lambda i: (0, 0)),
            modspec(2),
            pl.BlockSpec((1, D), lambda i: (0, 0)),
            modspec(3), modspec(4),
        ],
        out_specs=(pl.BlockSpec((tm, D), row), pl.BlockSpec((tm, D), row)),
        compiler_params=_params(("parallel",)),
        name="out_proj",
    )(ya, yb, yc, x2, w_out.astype(BF16), mod4, g.reshape(1, D), mod4, mod4)


def _top16(s, iota_f, n):
    vals, poss = [], []
    for _ in range(PEER_TOPK):
        m = jnp.max(s, axis=0, keepdims=True)
        pos = jnp.min(jnp.where(s == m, iota_f, float(n)), axis=0, keepdims=True)
        vals.append(m)
        poss.append(pos)
        s = jnp.where(iota_f == pos, -jnp.inf, s)
    return jnp.concatenate(vals, axis=0), jnp.concatenate(poss, axis=0)


ROUTE_UNROLL = 8
PEER_NCAND = 56


def _peer_cand_tables():
    K = PEER_TOPK
    pairs = [(a, b) for a in range(K) for b in range(K) if (a + 1) * (b + 1) <= K]
    n = PEER_NCAND
    p0 = [[0.0] * K for _ in range(n)]
    p1 = [[0.0] * K for _ in range(n)]
    pad = [0.0] * n
    pos = [float(K * K + r) for r in range(n)]
    for r, (a, b) in enumerate(pairs):
        p0[r][a] = 1.0
        p1[r][b] = 1.0
        pos[r] = float(a * K + b)
    for r in range(len(pairs), n):
        pad[r] = -float("inf")
    col = lambda v: jnp.broadcast_to(jnp.asarray(v, F32)[:, None], (n, LANES))
    return jnp.asarray(p0, F32), jnp.asarray(p1, F32), col(pad), col(pos)


def _peer_route_kernel(h_ref, wq_ref, sk_ref, p0_ref, p1_ref, cpad_ref, cpos_ref, idx_ref, gate_ref, q_scr, e_scr,
                       g_scr):
    K = PEER_TOPK
    hb = h_ref[...].astype(BF16)
    q = _mm(hb, wq_ref[...])
    for hc in range(2 * PEER_HEADS):
        q_scr[hc] = q[:, LANES * hc:LANES * (hc + 1)].astype(BF16)
    iota_n = lax.broadcasted_iota(jnp.int32, (PEER_NKEYS, LANES), 0).astype(F32)
    cpos = cpos_ref[...]

    def one_head(h):
        sv0, si0 = _top16(_nt(sk_ref[2 * h], q_scr[2 * h]), iota_n, PEER_NKEYS)
        sv1, si1 = _top16(_nt(sk_ref[2 * h + 1], q_scr[2 * h + 1]), iota_n, PEER_NKEYS)
        cand = _mm(p0_ref[...], sv0, HIGHEST) + _mm(p1_ref[...], sv1, HIGHEST) + cpad_ref[...]
        cidx = _mm(p0_ref[...], si0) * float(PEER_NKEYS) + _mm(p1_ref[...], si1)
        fv, es = [], []
        for _ in range(K):
            m = jnp.max(cand, axis=0, keepdims=True)
            pos = jnp.min(jnp.where(cand == m, cpos, float(2 * K * K)), axis=0, keepdims=True)
            hit = cpos == pos
            fv.append(m)
            es.append(jnp.max(jnp.where(hit, cidx, -1.0), axis=0, keepdims=True))
            cand = jnp.where(hit, -jnp.inf, cand)
        fv = jnp.concatenate(fv, axis=0)
        ex = jnp.exp(fv - fv[0:1, :])
        g_scr[h] = ex / jnp.sum(ex, axis=0, keepdims=True)
        e_scr[h] = jnp.concatenate(es, axis=0)

    def heads(hh, carry):
        for j in range(ROUTE_UNROLL):
            one_head(hh * ROUTE_UNROLL + j)
        return carry

    lax.fori_loop(0, PEER_HEADS // ROUTE_UNROLL, heads, 0)
    e = e_scr[...].reshape(PEER_HEADS * K, LANES)
    idx_ref[...] = e.T.astype(jnp.int32)
    gate_ref[...] = g_scr[...].reshape(PEER_HEADS * K, LANES)


def _peer_route(h2, wq, subkeys, tok0=0, ntok=None):
    T, D = h2.shape
    T = T - tok0 if ntok is None else ntok
    tm = LANES
    blk0 = tok0 // tm
    nq = 2 * PEER_HEADS
    sk = subkeys.reshape(nq, PEER_NKEYS, PEER_HALF).astype(BF16)
    p0, p1, cpad, cpos = _peer_cand_tables()
    const = lambda shape: pl.BlockSpec(shape, lambda i: (0,) * len(shape))
    return pl.pallas_call(
        _peer_route_kernel,
        out_shape=(jax.ShapeDtypeStruct((T, PEER_HEADS * PEER_TOPK), jnp.int32),
                   jax.ShapeDtypeStruct((T // tm, PEER_HEADS * PEER_TOPK, tm), F32)),
        grid=(T // tm,),
        in_specs=[
            pl.BlockSpec((tm, D), lambda i: (blk0 + i, 0)),
            const((D, nq * PEER_HALF)),
            const((nq, PEER_NKEYS, PEER_HALF)),
            const((PEER_NCAND, PEER_TOPK)), const((PEER_NCAND, PEER_TOPK)),
            const((PEER_NCAND, LANES)), const((PEER_NCAND, LANES)),
        ],
        out_specs=(pl.BlockSpec((tm, PEER_HEADS * PEER_TOPK), lambda i: (i, 0)),
                   pl.BlockSpec((None, PEER_HEADS * PEER_TOPK, tm), lambda i: (i, 0, 0))),
        scratch_shapes=[
            pltpu.VMEM((nq, tm, PEER_HALF), BF16),
            pltpu.VMEM((PEER_HEADS, PEER_TOPK, tm), F32),
            pltpu.VMEM((PEER_HEADS, PEER_TOPK, tm), F32),
        ],
        compiler_params=_params(("parallel",)),
        name="peer_route",
    )(h2, wq.astype(BF16), sk, p0, p1, cpad, cpos)


PEER_G = 32
PEER_SLOTS = PEER_HEADS * PEER_TOPK


def _peer_eval_kernel(idx_ref, idxn_ref, gate_ref, h_ref, x_ref, gf_ref, fg_ref, uv_ref, o_ref, buf, sem, *, final):
    G = PEER_G
    R = G * PEER_SLOTS
    D = D_MODEL
    tiles = PEER_SLOTS // SUBLANES
    i = pl.program_id(0)
    n = pl.num_programs(0)

    def start(ids, off, s, t, u):
        pltpu.make_async_copy(uv_ref.at[ids[off + t * SUBLANES + u]], buf.at[s, t, pl.ds(u, 1), :],
                              sem.at[s]).start(priority=u % 2)

    def wait(s):
        pltpu.make_async_copy(buf.at[s], buf.at[s], sem.at[s]).wait()

    @pl.when(i == 0)
    def _():
        def body(t, carry):
            for u in range(SUBLANES):
                start(idx_ref, 0, 0, t, u)
            return carry
        lax.fori_loop(0, R // SUBLANES, body, 0)

    lane = lax.broadcasted_iota(jnp.int32, (1, LANES), 1)
    tbase = (i % (LANES // (2 * G))) * (2 * G)
    gate = gate_ref[...]
    outs = []
    for s in range(2):
        wait(s)
        nxt_ids, nxt_off = (idx_ref, R) if s == 0 else (idxn_ref, 0)
        for g in range(G):
            for t in range(tiles * g, tiles * (g + 1)):
                for u in range(SUBLANES):
                    start(nxt_ids, nxt_off, 1 - s, t, u)
            w_rows = buf[s, tiles * g:tiles * (g + 1)].reshape(PEER_SLOTS, D)
            u_rows = lax.bitcast_convert_type(w_rows & jnp.uint32(0xFFFF0000), F32)
            prod = u_rows * h_ref[G * s + g:G * s + g + 1, :]
            part = prod[:, 0:LANES]
            for c in range(1, D // LANES):
                part = part + prod[:, LANES * c:LANES * (c + 1)]
            act = jnp.sum(part, axis=1, keepdims=True)
            gcol = jnp.sum(jnp.where(lane == tbase + G * s + g, gate, 0.0), axis=1, keepdims=True)
            coef = gcol * (0.5 * act * (1.0 + lax.erf(act * (2.0 ** -0.5))))
            v_rows = lax.bitcast_convert_type(w_rows << 16, F32)
            outs.append(jnp.sum(v_rows * coef, axis=0, keepdims=True))
    xn = x_ref[...] + gf_ref[...] * jnp.concatenate(outs, axis=0)
    if final:
        ms = jnp.mean(xn * xn, axis=-1, keepdims=True)
        xn = xn * lax.rsqrt(ms + RMS_EPS) * fg_ref[...]
    o_ref[...] = xn

    @pl.when(i == n - 1)
    def _():
        wait(0)


def _peer_eval(eidx, gate_t, h2, x2, mod4, final_g, uv, S, final, ntok=None):
    T, D = x2.shape
    T = T if ntok is None else ntok
    G = PEER_G
    R = G * PEER_SLOTS
    n = T // (2 * G)
    return pl.pallas_call(
        functools.partial(_peer_eval_kernel, final=final),
        out_shape=jax.ShapeDtypeStruct(x2.shape, F32),
        input_output_aliases={4: 0},
        grid=(n,),
        in_specs=[
            pl.BlockSpec((2 * R,), lambda i: (i,), memory_space=pltpu.SMEM),
            pl.BlockSpec((R,), lambda i: (jnp.minimum(2 * i + 2, 2 * n - 2),), memory_space=pltpu.SMEM),
            pl.BlockSpec((None, PEER_SLOTS, LANES), lambda i: (i // (LANES // (2 * G)), 0, 0)),
            pl.BlockSpec((2 * G, D), lambda i: (i, 0)),
            pl.BlockSpec((2 * G, D), lambda i: (i, 0)),
            pl.BlockSpec((None, None, 1, D), lambda i: (i // (S // (2 * G)), 5, 0, 0)),
            pl.BlockSpec((1, D), lambda i: (0, 0)),
            pl.BlockSpec(memory_space=pl.ANY),
        ],
        out_specs=pl.BlockSpec((2 * G, D), lambda i: (i, 0)),
        scratch_shapes=[pltpu.VMEM((2, R // SUBLANES, SUBLANES, D), jnp.uint32), pltpu.SemaphoreType.DMA((2,))],
        compiler_params=_params(("arbitrary",)),
        name="peer_eval",
    )(eidx.reshape(-1), eidx.reshape(-1), gate_t, h2, x2, mod4, final_g.reshape(1, D), uv)


SC_WORKERS = 32
SC_WINDOW = 128
SC_ROWS = 32
PEER_GD = 32
SC_SHARE_NUM, SC_SHARE_DEN = 5, 8
SC_PARTS = 8
BATCH_GROUPS = 1


def _sc_gather(tab, idx):
    n = idx.shape[0]
    width = tab.shape[1]
    per = n // SC_WORKERS
    nsub = SC_WINDOW // SC_ROWS
    mesh = plsc.VectorSubcoreMesh(core_axis_name="core", subcore_axis_name="subcore")

    @pl.kernel(out_type=jax.ShapeDtypeStruct((n, width), tab.dtype), mesh=mesh,
               scratch_types=[pltpu.VMEM((SC_WINDOW,), jnp.int32), pltpu.VMEM((2, SC_ROWS, width), tab.dtype),
                              pltpu.SemaphoreType.DMA((2,)), pltpu.SemaphoreType.DMA((2,))])
    def gather(x_hbm, i_hbm, o_hbm, idx_v, rows_v, gsem, wsem):
        wid = lax.axis_index("core") * (SC_WORKERS // 2) + lax.axis_index("subcore")

        def start_gather(k):
            return pltpu.async_copy(x_hbm.at[idx_v.at[pl.ds(k * SC_ROWS, SC_ROWS)]], rows_v.at[k % 2], gsem.at[k % 2])

        @pl.loop(0, per // SC_WINDOW)
        def _(w):
            base = wid * per + w * SC_WINDOW
            pltpu.sync_copy(i_hbm.at[pl.ds(base, SC_WINDOW)], idx_v)
            gat = [start_gather(0)] + [None] * (nsub - 1)
            wrt = [None] * nsub
            for k in range(nsub):
                if k + 1 < nsub:
                    if k >= 1:
                        wrt[k - 1].wait()
                    gat[k + 1] = start_gather(k + 1)
                gat[k].wait()
                wrt[k] = pltpu.async_copy(rows_v.at[k % 2], o_hbm.at[pl.ds(base + k * SC_ROWS, SC_ROWS)],
                                          wsem.at[k % 2])
            for k in range(max(nsub - 2, 0), nsub):
                wrt[k].wait()

    return gather(tab, idx)


def _peer_dense_kernel(rows_ref, gate_ref, h_ref, x_ref, gf_ref, fg_ref, o_ref, *, final):
    G = PEER_GD
    D = D_MODEL
    i = pl.program_id(0)
    lane = lax.broadcasted_iota(jnp.int32, (1, LANES), 1)
    tbase = (i % (LANES // G)) * G
    gate = gate_ref[...]
    outs = []
    for g in range(G):
        w_rows = rows_ref[PEER_SLOTS * g:PEER_SLOTS * (g + 1), :]
        u_rows = lax.bitcast_convert_type(w_rows & jnp.uint32(0xFFFF0000), F32)
        prod = u_rows * h_ref[g:g + 1, :]
        part = prod[:, 0:LANES]
        for c in range(1, D // LANES):
            part = part + prod[:, LANES * c:LANES * (c + 1)]
        act = jnp.sum(part, axis=1, keepdims=True)
        gcol = jnp.sum(jnp.where(lane == tbase + g, gate, 0.0), axis=1, keepdims=True)
        coef = gcol * (0.5 * act * (1.0 + lax.erf(act * (2.0 ** -0.5))))
        v_rows = lax.bitcast_convert_type(w_rows << 16, F32)
        outs.append(jnp.sum(v_rows * coef, axis=0, keepdims=True))
    xn = x_ref[...] + gf_ref[...] * jnp.concatenate(outs, axis=0)
    if final:
        ms = jnp.mean(xn * xn, axis=-1, keepdims=True)
        xn = xn * lax.rsqrt(ms + RMS_EPS) * fg_ref[...]
    o_ref[...] = xn


def _peer_eval_dense(rows, gate_t, h2, x2, mod4, final_g, S, final, tok0):
    T, D = x2.shape
    G = PEER_GD
    tb = rows.shape[0] // PEER_SLOTS
    assert tok0 % LANES == 0 and tb % LANES == 0
    blk0 = tok0 // G
    return pl.pallas_call(
        functools.partial(_peer_dense_kernel, final=final),
        out_shape=jax.ShapeDtypeStruct(x2.shape, F32),
        input_output_aliases={3: 0},
        grid=(tb // G,),
        in_specs=[
            pl.BlockSpec((G * PEER_SLOTS, D), lambda i: (i, 0)),
            pl.BlockSpec((None, PEER_SLOTS, LANES), lambda i: (i // (LANES // G), 0, 0)),
            pl.BlockSpec((G, D), lambda i: (blk0 + i, 0)),
            pl.BlockSpec((G, D), lambda i: (blk0 + i, 0)),
            pl.BlockSpec((None, None, 1, D), lambda i: ((blk0 + i) // (S // G), 5, 0, 0)),
            pl.BlockSpec((1, D), lambda i: (0, 0)),
        ],
        out_specs=pl.BlockSpec((G, D), lambda i: (blk0 + i, 0)),
        compiler_params=_params(("parallel",)),
        name="peer_dense",
    )(rows, gate_t, h2, x2, mod4, final_g.reshape(1, D))


def _pack_uv(u, v):
    hi = lax.bitcast_convert_type(u.astype(BF16), jnp.uint16).astype(jnp.uint32) << 16
    lo = lax.bitcast_convert_type(v.astype(BF16), jnp.uint16).astype(jnp.uint32)
    return (hi | lo).reshape(u.shape[0], 1, u.shape[1])


def kernel(x, c, norm_mix_g, norm_ffn_g, final_norm_g, ada_w, ada_b, w_in, w_out, dif_lam, dif_subln_g, rw_mu, rw_w0,
           rw_w2, rw_a0, rw_a2, rw_g2, rw_kk, rw_ka, rw_rk, rw_ln_g, rw_ln_b, fox_bf, peer_wq, peer_subkeys, peer_u,
           peer_v):
    B, S, D = x.shape
    depth = ada_w.shape[0]
    mod = _ada_mod(c, ada_w, ada_b)
    ngrp = BATCH_GROUPS if B % BATCH_GROUPS == 0 else 1
    bg = B // ngrp
    tg = bg * S
    prep = [(_pad_w_in(w_in[l]), _pack_uv(peer_u[l], peer_v[l])) for l in range(depth)]

    def mix_route(l, g, x2):
        mod4 = mod[l, g * bg:(g + 1) * bg].reshape(bg, 6, 1, D)
        (w_pad, w_vt), uv = prep[l]
        dqk, rcols, fqk, fl, vtd, vtf = _in_proj(x2, norm_mix_g[l], mod4, w_pad, w_vt, S)
        ya = _diff_attention(dqk, vtd, dif_lam[l], dif_subln_g[l], l, bg, S)
        yb = _rwkv(rcols, rw_mu[l], rw_w0[l], rw_w2[l], rw_a0[l], rw_a2[l], rw_g2[l], rw_kk[l], rw_ka[l],
                   rw_rk[l].reshape(-1), rw_ln_g[l], rw_ln_b[l], bg, S, prec=None)
        crep, crow = _fox_cum(fl, fox_bf[l], bg, S)
        yc = _fox_attention(fqk, vtf, crep, crow, bg, S)
        x2, h2 = _out_proj(ya, yb, yc, x2, w_out[l], norm_ffn_g[l], mod4, S)
        unit = SC_PARTS * LANES
        tb = (tg * SC_SHARE_NUM // SC_SHARE_DEN) // unit * unit
        ta = tg - tb
        bounds = [ta + tb * j // SC_PARTS for j in range(SC_PARTS + 1)]
        parts = []
        for t0, t1 in zip(bounds[:-1], bounds[1:]):
            if t1 == t0:
                continue
            e_p, g_p = _peer_route(h2, peer_wq[l], peer_subkeys[l], t0, t1 - t0)
            parts.append((_sc_gather(uv.reshape(-1, D), e_p.reshape(-1)), g_p, t0))
        e_a, g_a = _peer_route(h2, peer_wq[l], peer_subkeys[l], 0, ta)
        return dict(x2=x2, h2=h2, mod4=mod4, uv=uv, ta=ta, e_a=e_a, g_a=g_a, parts=parts)

    def evaluate(l, st):
        final = l == depth - 1
        x2 = _peer_eval(st["e_a"], st["g_a"], st["h2"], st["x2"], st["mod4"], final_norm_g, st["uv"], S, final,
                        ntok=st["ta"])
        for r, g_p, t0 in st["parts"]:
            x2 = _peer_eval_dense(r, g_p, st["h2"], x2, st["mod4"], final_norm_g, S, final, t0)
        return x2

    xs = [x[g * bg:(g + 1) * bg].reshape(tg, D) for g in range(ngrp)]
    pending = []
    for l in range(depth):
        for g in range(ngrp):
            pending.append((l, g, mix_route(l, g, xs[g])))
            if len(pending) == ngrp:
                l0, g0, st = pending.pop(0)
                xs[g0] = evaluate(l0, st)
    for l0, g0, st in pending:
        xs[g0] = evaluate(l0, st)
    return jnp.concatenate(xs, axis=0).reshape(B, S, D)
```
